```python
import math
import jax, jax.numpy as jnp
from jax import lax
import numpy as np

D_MODEL = 1024
BATCH = 8
SEQ = 2048
DEPTH = 1
DEC_BATCH = 128
DEC_SEQ = 4
PAST_LEN = 8192
PAGE_SIZE = 128

ATTN_HEADS = 8
ATTN_KV_HEADS = 2
HEAD_DIM = 64
ATTN_GROUP = ATTN_HEADS // ATTN_KV_HEADS
WINDOW = 128
ATTN_BLOCK = 128
ROT_DIM = HEAD_DIM // 4
ROPE_THETA = 500000.0
GDN_HEADS = 4
GDN_DK = 128
GDN_DV = 128
CONV_W = 4
GDN_CHUNK = 64
QK_COLS = GDN_HEADS * GDN_DK
CONV_DIM = 2 * QK_COLS + GDN_HEADS * GDN_DV
Z_COLS = GDN_HEADS * GDN_DV
Q_COLS = ATTN_HEADS * HEAD_DIM
KV_COLS = ATTN_KV_HEADS * HEAD_DIM
IN_COLS = Q_COLS + 2 * KV_COLS + CONV_DIM + Z_COLS + 2 * GDN_HEADS
MIX_WIDTH = Q_COLS + GDN_HEADS * GDN_DV
N_GROUPS = 4
EXPERTS_PER_GROUP = 8
N_EXPERTS = N_GROUPS * EXPERTS_PER_GROUP
TOP_K = 2
EXPERT_FF = 256
NORM_EPS = 1e-5
L2_EPS = 1e-6
DEEPNORM_ALPHA = (2 * DEPTH) ** 0.25
DEEPNORM_BETA = (8 * DEPTH) ** -0.25

kernel_name = "hymba_swa_sink_gdn_hier_moe_step"


def layer_norm(x, g, b):
    xf = x.astype(jnp.float32)
    mu = jnp.mean(xf, -1, keepdims=True)
    var = jnp.mean(jnp.square(xf - mu), -1, keepdims=True)
    return ((xf - mu) * lax.rsqrt(var + NORM_EPS) * g.astype(jnp.float32) + b.astype(jnp.float32)).astype(x.dtype)


def partial_rope(x, pos):
    half = ROT_DIM // 2
    inv_freq = ROPE_THETA ** (-jnp.arange(half, dtype=jnp.float32) * 2.0 / ROT_DIM)
    ang = pos.astype(jnp.float32)[:, None] * inv_freq[None, :]
    cos = jnp.cos(ang)[None, :, None, :]
    sin = jnp.sin(ang)[None, :, None, :]
    xr = x[..., :ROT_DIM].astype(jnp.float32)
    x1, x2 = xr[..., :half], xr[..., half:]
    rot = jnp.concatenate([x1 * cos - x2 * sin, x2 * cos + x1 * sin], axis=-1).astype(x.dtype)
    return jnp.concatenate([rot, x[..., ROT_DIM:]], axis=-1)


def sink_attention(q, k, v, q_pos, k_pos, sinks):
    s = jnp.einsum('bnqhgd,bnkhd->bnhgqk', q, k).astype(jnp.float32) * (HEAD_DIM ** -0.5)
    dpos = q_pos[:, :, None] - k_pos[:, None, :]
    visible = (dpos >= 0) & (dpos < WINDOW) & (k_pos[:, None, :] >= 0)
    s = jnp.where(visible[None, :, None, None], s, -jnp.inf)
    sink = sinks.astype(jnp.float32).reshape(ATTN_KV_HEADS, ATTN_GROUP)[None, None, :, :, None, None]
    m = jnp.maximum(jnp.max(s, -1, keepdims=True), sink)
    p = jnp.exp(s - m)
    p = p / (jnp.sum(p, -1, keepdims=True) + jnp.exp(sink - m))
    return jnp.einsum('bnhgqk,bnkhd->bnqhgd', p.astype(v.dtype), v)


def swa_prompt(q, k, v, sinks):
    B, S = q.shape[:2]
    nb = S // ATTN_BLOCK
    qb = q.reshape(B, nb, ATTN_BLOCK, ATTN_KV_HEADS, ATTN_GROUP, HEAD_DIM)

    def band(t):
        tb = t.reshape(B, nb, ATTN_BLOCK, ATTN_KV_HEADS, HEAD_DIM)
        prev = jnp.pad(tb[:, :-1], ((0, 0), (1, 0), (0, 0), (0, 0), (0, 0)))
        return jnp.concatenate([prev, tb], axis=2)

    pos = jnp.arange(S, dtype=jnp.int32).reshape(nb, ATTN_BLOCK)
    k_pos = jnp.concatenate([pos - ATTN_BLOCK, pos], axis=1)
    o = sink_attention(qb, band(k), band(v), pos, k_pos, sinks)
    return o.reshape(B, S, Q_COLS)


def swa_sample(q, k_new, v_new, win_k, win_v, sinks):
    Bd, T = q.shape[:2]
    k_all = jnp.concatenate([win_k.astype(k_new.dtype), k_new], axis=1)
    v_all = jnp.concatenate([win_v.astype(v_new.dtype), v_new], axis=1)
    q_pos = PAST_LEN + jnp.arange(T, dtype=jnp.int32)
    k_pos = jnp.concatenate([PAST_LEN - WINDOW + jnp.arange(WINDOW, dtype=jnp.int32), q_pos])
    qb = q.reshape(Bd, 1, T, ATTN_KV_HEADS, ATTN_GROUP, HEAD_DIM)
    o = sink_attention(qb, k_all[:, None], v_all[:, None], q_pos[None], k_pos[None], sinks)
    return o.reshape(Bd, T, Q_COLS), k_all[:, -WINDOW:], v_all[:, -WINDOW:]


def causal_conv(u, buf, w):
    T = u.shape[1]
    ext = jnp.concatenate([buf.astype(u.dtype), u], axis=1)
    out = ext[:, 0:T] * w[0]
    for i in range(1, CONV_W):
        out = out + ext[:, i:i + T] * w[i]
    return jax.nn.silu(out), ext[:, -(CONV_W - 1):]


def l2_normalize(t):
    tf = t.astype(jnp.float32)
    return tf * lax.rsqrt(jnp.sum(tf * tf, -1, keepdims=True) + L2_EPS)


def gated_delta_rule(q, k, v, g, beta, s0, chunk):
    B, T, H, DK = k.shape
    nc = T // chunk
    f32 = jnp.float32

    def blocks(t):
        t = t.astype(f32).reshape((B, nc, chunk) + t.shape[2:])
        return jnp.moveaxis(t, 3, 1)

    q, k, v, g, beta = blocks(q), blocks(k), blocks(v), blocks(g), blocks(beta)
    q = q * (DK ** -0.5)
    gc = jnp.cumsum(g, axis=-1)
    idx = jnp.arange(chunk)
    causal = idx[:, None] >= idx[None, :]
    strict = idx[:, None] > idx[None, :]
    decay = jnp.exp(jnp.where(causal, gc[..., :, None] - gc[..., None, :], -jnp.inf))
    kb = k * beta[..., None]
    vb = v * beta[..., None]
    m = jnp.where(strict, jnp.einsum('bhnid,bhnjd->bhnij', kb, k) * decay, 0.0)
    eye = jnp.eye(chunk, dtype=f32)
    tmat = lax.linalg.triangular_solve(eye + m, jnp.broadcast_to(eye, m.shape), left_side=True, lower=True)
    u = tmat @ vb
    w = tmat @ (kb * jnp.exp(gc)[..., None])
    attn = jnp.einsum('bhnid,bhnjd->bhnij', q, k) * decay
    q_dec = q * jnp.exp(gc)[..., None]
    g_last = gc[..., -1]
    k_dec = k * jnp.exp(g_last[..., None] - gc)[..., None]

    def step(s, xs):
        u_c, w_c, attn_c, qd_c, kd_c, gl_c = xs
        v_new = u_c - w_c @ s
        o_c = qd_c @ s + attn_c @ v_new
        s = s * jnp.exp(gl_c)[..., None, None] + jnp.swapaxes(kd_c, -1, -2) @ v_new
        return s, o_c

    xs = tuple(jnp.moveaxis(t, 2, 0) for t in (u, w, attn, q_dec, k_dec, g_last))
    s_final, o = lax.scan(step, s0.astype(f32), xs)
    o = jnp.moveaxis(jnp.moveaxis(o, 0, 2), 1, 3).reshape(B, T, H, v.shape[-1])
    return o, s_final


def gated_rms_norm(o, z, w):
    of = o.astype(jnp.float32)
    of = of * lax.rsqrt(jnp.mean(of * of, -1, keepdims=True) + NORM_EPS) * w.astype(jnp.float32)
    return of * jax.nn.silu(z.astype(jnp.float32))


def gdn_mixer(qkv_raw, z, a, b, conv_buf, s0, chunk, conv_w, a_log, dt_bias, norm_w):
    B, T, _ = qkv_raw.shape
    qkv, new_buf = causal_conv(qkv_raw, conv_buf, conv_w)
    qg = l2_normalize(qkv[..., :QK_COLS].reshape(B, T, GDN_HEADS, GDN_DK))
    kg = l2_normalize(qkv[..., QK_COLS:2 * QK_COLS].reshape(B, T, GDN_HEADS, GDN_DK))
    vg = qkv[..., 2 * QK_COLS:].reshape(B, T, GDN_HEADS, GDN_DV).astype(jnp.float32)
    g = -jnp.exp(a_log.astype(jnp.float32)) * jax.nn.softplus(a.astype(jnp.float32) + dt_bias.astype(jnp.float32))
    beta = jax.nn.sigmoid(b.astype(jnp.float32))
    o, s_new = gated_delta_rule(qg, kg, vg, g, beta, s0, chunk)
    o = gated_rms_norm(o, z.reshape(B, T, GDN_HEADS, GDN_DV), norm_w)
    return o.reshape(B, T, Z_COLS).astype(qkv_raw.dtype), s_new.astype(qkv_raw.dtype), new_buf


def hier_moe(x, w_router_group, w_router_expert, w_gate, w_up, w_down):
    B, T, D = x.shape
    xt = x.reshape(B * T, D)
    n = xt.shape[0]
    g_prob = jax.nn.softmax((xt @ w_router_group).astype(jnp.float32), axis=-1)
    g_top_p, g_top = lax.top_k(g_prob, 1)
    e_logits = (xt @ w_router_expert).astype(jnp.float32).reshape(n, N_GROUPS, EXPERTS_PER_GROUP)
    e_sel = e_logits[jnp.arange(n), g_top[:, 0]]
    e_top_p, e_top = lax.top_k(jax.nn.softmax(e_sel, axis=-1), TOP_K)
    e_top_p = e_top_p / jnp.sum(e_top_p, -1, keepdims=True)
    expert_id = g_top * EXPERTS_PER_GROUP + e_top
    gates = g_top_p * e_top_p
    combine = jnp.sum(jax.nn.one_hot(expert_id, N_EXPERTS, dtype=jnp.float32) * gates[..., None], axis=1)
    h = jax.nn.silu(jnp.einsum('nd,edf->nef', xt, w_gate)) * jnp.einsum('nd,edf->nef', xt, w_up)
    h = h * combine.astype(h.dtype)[..., None]
    y = jnp.einsum('nef,efd->nd', h, w_down)
    return y.reshape(B, T, D)


def hybrid_layer(x, pos, win_k, win_v, conv_buf, s0, chunk,
                 w_in, w_out, attn_sinks, conv_w, a_log, dt_bias, gdn_norm_w, ln1_g, ln1_b,
                 w_router_group, w_router_expert, w_gate, w_up, w_down, ln2_g, ln2_b):
    B, T, _ = x.shape
    proj = x @ w_in
    o1 = Q_COLS
    o2 = o1 + KV_COLS
    o3 = o2 + KV_COLS
    o4 = o3 + CONV_DIM
    o5 = o4 + Z_COLS
    o6 = o5 + GDN_HEADS
    q, k, v, qkv_g, z, a, b = jnp.split(proj, [o1, o2, o3, o4, o5, o6], axis=-1)
    q = partial_rope(q.reshape(B, T, ATTN_HEADS, HEAD_DIM), pos)
    k = partial_rope(k.reshape(B, T, ATTN_KV_HEADS, HEAD_DIM), pos)
    v = v.reshape(B, T, ATTN_KV_HEADS, HEAD_DIM)
    if win_k is None:
        o_attn = swa_prompt(q, k, v, attn_sinks)
        new_k, new_v = k[:, -WINDOW:], v[:, -WINDOW:]
        conv_buf = jnp.zeros((B, CONV_W - 1, CONV_DIM), x.dtype)
        s0 = jnp.zeros((B, GDN_HEADS, GDN_DK, GDN_DV), jnp.float32)
    else:
        o_attn, new_k, new_v = swa_sample(q, k, v, win_k, win_v, attn_sinks)
    o_gdn, new_s, new_conv = gdn_mixer(qkv_g, z, a, b, conv_buf, s0, chunk, conv_w, a_log, dt_bias, gdn_norm_w)
    mix = jnp.concatenate([o_attn, o_gdn], axis=-1) @ w_out
    x = layer_norm(DEEPNORM_ALPHA * x + mix, ln1_g, ln1_b)
    x = layer_norm(DEEPNORM_ALPHA * x + hier_moe(x, w_router_group, w_router_expert, w_gate, w_up, w_down), ln2_g, ln2_b)
    return x, new_k, new_v, new_s, new_conv


def setup_inputs(seed: int = 0) -> dict:
    key = jax.random.key(seed)
    ks = jax.random.split(key, 24)
    f32 = jnp.float32

    def nrm(k, shape, scale):
        return jax.random.normal(k, shape, f32) * scale

    dt = jnp.exp(jax.random.uniform(ks[11], (DEPTH, GDN_HEADS), f32, math.log(1e-3), math.log(1e-1)))
    return {
        'x_prompt': nrm(ks[0], (BATCH, SEQ, D_MODEL), 1.0),
        'x_sample': nrm(ks[1], (DEC_BATCH, DEC_SEQ, D_MODEL), 1.0),
        'cache_attn_k': nrm(ks[2], (DEPTH, DEC_BATCH, WINDOW, ATTN_KV_HEADS, HEAD_DIM), 1.0),
        'cache_attn_v': nrm(ks[3], (DEPTH, DEC_BATCH, WINDOW, ATTN_KV_HEADS, HEAD_DIM), 1.0),
        'state_gdn': nrm(ks[4], (DEPTH, DEC_BATCH, GDN_HEADS, GDN_DK, GDN_DV), 0.1),
        'state_conv': nrm(ks[5], (DEPTH, DEC_BATCH, CONV_W - 1, CONV_DIM), 1.0),
        'w_in': nrm(ks[6], (DEPTH, D_MODEL, IN_COLS), D_MODEL ** -0.5),
        'w_out': nrm(ks[7], (DEPTH, MIX_WIDTH, D_MODEL), MIX_WIDTH ** -0.5 * DEEPNORM_BETA),
        'attn_sinks': nrm(ks[8], (DEPTH, ATTN_HEADS), 1.0),
        'conv_w': nrm(ks[9], (DEPTH, CONV_W, CONV_DIM), CONV_W ** -0.5),
        'a_log': jnp.log(jax.random.uniform(ks[10], (DEPTH, GDN_HEADS), f32, 1.0, 16.0)),
        'dt_bias': dt + jnp.log(-jnp.expm1(-dt)),
        'gdn_norm_w': 1.0 + nrm(ks[12], (DEPTH, GDN_DV), 0.02),
        'ln1_g': 1.0 + nrm(ks[13], (DEPTH, D_MODEL), 0.02),
        'ln1_b': nrm(ks[14], (DEPTH, D_MODEL), 0.02),
        'w_router_group': nrm(ks[15], (DEPTH, D_MODEL, N_GROUPS), D_MODEL ** -0.5),
        'w_router_expert': nrm(ks[16], (DEPTH, D_MODEL, N_EXPERTS), D_MODEL ** -0.5),
        'w_gate': nrm(ks[17], (DEPTH, N_EXPERTS, D_MODEL, EXPERT_FF), D_MODEL ** -0.5),
        'w_up': nrm(ks[18], (DEPTH, N_EXPERTS, D_MODEL, EXPERT_FF), D_MODEL ** -0.5),
        'w_down': nrm(ks[19], (DEPTH, N_EXPERTS, EXPERT_FF, D_MODEL), EXPERT_FF ** -0.5 * DEEPNORM_BETA),
        'ln2_g': 1.0 + nrm(ks[20], (DEPTH, D_MODEL), 0.02),
        'ln2_b': nrm(ks[21], (DEPTH, D_MODEL), 0.02),
    }


def reference(x_prompt, x_sample, cache_attn_k, cache_attn_v, state_gdn, state_conv,
              w_in, w_out, attn_sinks, conv_w, a_log, dt_bias, gdn_norm_w, ln1_g, ln1_b,
              w_router_group, w_router_expert, w_gate, w_up, w_down, ln2_g, ln2_b):
    pos_p = jnp.arange(x_prompt.shape[1], dtype=jnp.int32)
    pos_s = PAST_LEN + jnp.arange(x_sample.shape[1], dtype=jnp.int32)
    yp, ys = x_prompt, x_sample
    kp, vp, sp, cp, ksm, vsm, ssm, csm = [], [], [], [], [], [], [], []
    for l in range(DEPTH):
        wl = (w_in[l], w_out[l], attn_sinks[l], conv_w[l], a_log[l], dt_bias[l], gdn_norm_w[l], ln1_g[l], ln1_b[l],
              w_router_group[l], w_router_expert[l], w_gate[l], w_up[l], w_down[l], ln2_g[l], ln2_b[l])
        yp, k_, v_, s_, c_ = hybrid_layer(yp, pos_p, None, None, None, None, GDN_CHUNK, *wl)
        kp.append(k_); vp.append(v_); sp.append(s_); cp.append(c_)
        ys, k_, v_, s_, c_ = hybrid_layer(ys, pos_s, cache_attn_k[l], cache_attn_v[l], state_conv[l], state_gdn[l],
                                          x_sample.shape[1], *wl)
        ksm.append(k_); vsm.append(v_); ssm.append(s_); csm.append(c_)
    return (yp, ys, jnp.stack(kp), jnp.stack(vp), jnp.stack(sp), jnp.stack(cp),
            jnp.stack(ksm), jnp.stack(vsm), jnp.stack(ssm), jnp.stack(csm))
```

```python
import functools
import math

import jax
import jax.numpy as jnp
from jax import lax
from jax.experimental import pallas as pl
from jax.experimental.pallas import tpu as pltpu

F32 = jnp.float32
BF16 = jnp.bfloat16
I32 = jnp.int32

D_MODEL = 1024
ATTN_HEADS = 8
ATTN_KV_HEADS = 2
HEAD_DIM = 64
WINDOW = 128
ROT_DIM = HEAD_DIM // 4
ROPE_THETA = 500000.0
GDN_HEADS = 4
GDN_DK = 128
GDN_DV = 128
CONV_W = 4
QK_COLS = GDN_HEADS * GDN_DK
CONV_DIM = 2 * QK_COLS + GDN_HEADS * GDN_DV
Z_COLS = GDN_HEADS * GDN_DV
Q_COLS = ATTN_HEADS * HEAD_DIM
KV_COLS = ATTN_KV_HEADS * HEAD_DIM
N_GROUPS = 4
EXPERTS_PER_GROUP = 8
N_EXPERTS = N_GROUPS * EXPERTS_PER_GROUP
TOP_K = 2
EXPERT_FF = 256
NORM_EPS = 1e-5
L2_EPS = 1e-6
DEPTH = 1
DEEPNORM_ALPHA = (2 * DEPTH) ** 0.25
PAST_LEN = 8192

LANES = 128
SUBLANES = 8
TOK_TILE = 256
GDN_CHUNK = 128
INV_BASE = 16
SAMPLE_SLOTS = 8
SAMPLE_FIRST = CONV_W - 1
ROW_TILE = 256
VMEM_LIMIT = 48 * 1024 * 1024
NEG_BIG = -1e30


def _cparams(sem):
    return pltpu.CompilerParams(dimension_semantics=sem, vmem_limit_bytes=VMEM_LIMIT)


def _bdot(a, b):
    return jnp.dot(a.astype(BF16), b.astype(BF16), preferred_element_type=F32)


def _bdot_nt(a, b):
    return lax.dot_general(a.astype(BF16), b.astype(BF16), (((1,), (1,)), ((), ())),
                           preferred_element_type=F32)


def _bdot_tn(a, b):
    return lax.dot_general(a.astype(BF16), b.astype(BF16), (((0,), (0,)), ((), ())),
                           preferred_element_type=F32)


def _div_pow2(x, n):
    return jnp.right_shift(x, int(math.log2(n)))


def _mod_pow2(x, n):
    return jnp.bitwise_and(x, n - 1)


def _split3(x):
    hi = x.astype(BF16)
    r = x - hi.astype(F32)
    mid = r.astype(BF16)
    lo = (r - mid.astype(F32)).astype(BF16)
    return hi, mid, lo


def _dot_exact_lhs01(m01, x):
    hi, mid, lo = _split3(x)
    d = lambda t: jnp.dot(m01, t, preferred_element_type=F32)
    return d(hi) + d(mid) + d(lo)


def _sigmoid(x):
    return 1.0 / (1.0 + jnp.exp(-x))


def _silu(x):
    return x * _sigmoid(x)


def _softplus(x):
    return jnp.maximum(x, 0.0) + jnp.log1p(jnp.exp(-jnp.abs(x)))


def _layer_norm(h, g, b):
    mu = jnp.mean(h, axis=-1, keepdims=True)
    d = h - mu
    var = jnp.mean(d * d, axis=-1, keepdims=True)
    return d * lax.rsqrt(var + NORM_EPS) * g + b


def _proj_kernel(*refs, tm, has_hist, full_u):
    it = iter(refs)
    x_ref, cos_ref, sin_ref = next(it), next(it), next(it)
    wqkv_ref, wg_ref, wz_ref, wab_ref = next(it), next(it), next(it), next(it)
    convw_ref, alog_ref, dtb_ref, tri_ref, seg_ref = next(it), next(it), next(it), next(it), next(it)
    hist_ref = valid_ref = None
    if has_hist:
        hist_ref, valid_ref = next(it), next(it)
    q_ref, k_ref, v_ref = next(it), next(it), next(it)
    qg_ref, kg_ref, vg_ref, z_ref, gcb_ref, u_ref = (next(it) for _ in range(6))
    ubuf = next(it)

    t = pl.program_id(1)
    xb = x_ref[...].astype(BF16)
    lane = lax.broadcasted_iota(I32, (tm, LANES), 1)

    pq = jnp.dot(xb, wqkv_ref[...], preferred_element_type=F32)
    cosv, sinv = cos_ref[...], sin_ref[...]
    first_half = _mod_pow2(lane, HEAD_DIM) < (ROT_DIM // 2)

    def rope(s):
        sw = jnp.where(first_half, pltpu.roll(s, LANES - ROT_DIM // 2, axis=1),
                       pltpu.roll(s, ROT_DIM // 2, axis=1))
        return s * cosv + sw * sinv

    for j in range(Q_COLS // LANES):
        q_ref[:, j * LANES:(j + 1) * LANES] = rope(pq[:, j * LANES:(j + 1) * LANES])
    k_ref[...] = rope(pq[:, Q_COLS:Q_COLS + KV_COLS])
    v_ref[...] = pq[:, Q_COLS + KV_COLS:Q_COLS + 2 * KV_COLS]

    u = jnp.dot(xb, wg_ref[...], preferred_element_type=F32)
    if has_hist:
        u = u + hist_ref[...]
    if full_u:
        u_ref[...] = u
    else:
        u_ref[...] = u[tm - SUBLANES:, :]

    @pl.when(t == 0)
    def _():
        ubuf[0:SUBLANES, :] = jnp.zeros((SUBLANES, CONV_DIM), F32)

    @pl.when(t > 0)
    def _():
        ubuf[0:SUBLANES, :] = ubuf[tm:tm + SUBLANES, :]

    ubuf[SUBLANES:SUBLANES + tm, :] = u
    acc = u * convw_ref[CONV_W - 1:CONV_W, :]
    for j in range(1, CONV_W):
        acc = acc + ubuf[SUBLANES - j:SUBLANES - j + tm, :] * convw_ref[CONV_W - 1 - j:CONV_W - j, :]
    c = _silu(acc)
    if has_hist:
        c = c * valid_ref[...]

    def l2n(s):
        return s * lax.rsqrt(jnp.sum(s * s, axis=1, keepdims=True) + L2_EPS)

    for h in range(GDN_HEADS):
        sl = slice(h * GDN_DK, (h + 1) * GDN_DK)
        qg_ref[:, sl] = l2n(c[:, sl]) * (GDN_DK ** -0.5)
        kg_ref[:, sl] = l2n(c[:, QK_COLS + h * GDN_DK:QK_COLS + (h + 1) * GDN_DK])
    vg_ref[...] = c[:, 2 * QK_COLS:]
    z_ref[...] = jnp.dot(xb, wz_ref[...], preferred_element_type=F32)

    ab = jnp.dot(xb, wab_ref[...], preferred_element_type=F32)
    g = -jnp.exp(alog_ref[...]) * _softplus(ab + dtb_ref[...])
    beta = _sigmoid(ab)
    if has_hist:
        g = g * valid_ref[...]
        beta = beta * valid_ref[...]
    g = jnp.where(lane < GDN_HEADS, g, 0.0)
    gc = _dot_exact_lhs01(tri_ref[...], g)
    gl = _dot_exact_lhs01(seg_ref[...], g)
    gcb_ref[...] = jnp.where(lane < GDN_HEADS, gc,
                             jnp.where(lane < 2 * GDN_HEADS, beta,
                                       jnp.where(lane < 3 * GDN_HEADS,
                                                 pltpu.roll(gl, 2 * GDN_HEADS, axis=1), 0.0)))


def _rope_tables(pos):
    half = ROT_DIM // 2
    inv_freq = ROPE_THETA ** (-jnp.arange(half, dtype=F32) * 2.0 / ROT_DIM)
    ang = pos.astype(F32)[:, None] * inv_freq[None, :]
    cos, sin = jnp.cos(ang), jnp.sin(ang)
    p = pos.shape[0]
    cpat = jnp.concatenate([cos, cos, jnp.ones((p, HEAD_DIM - ROT_DIM), F32)], axis=1)
    spat = jnp.concatenate([-sin, sin, jnp.zeros((p, HEAD_DIM - ROT_DIM), F32)], axis=1)
    return jnp.tile(cpat, (1, LANES // HEAD_DIM)), jnp.tile(spat, (1, LANES // HEAD_DIM))


def _segment_matrices(tm, seg_len):
    i = jnp.arange(tm)
    same = (i[:, None] // seg_len) == (i[None, :] // seg_len)
    tri = same & (i[None, :] <= i[:, None])
    return tri.astype(BF16), same.astype(BF16)


def _proj(x, pos, wts, seg_len, n_seq, hist=None, valid=None):
    n = x.shape[0]
    rows = n // n_seq
    tm = min(TOK_TILE, rows)
    nt = rows // tm
    has_hist = hist is not None
    cos_t, sin_t = _rope_tables(pos)
    tri, seg = _segment_matrices(tm, seg_len)

    tok = lambda w: pl.BlockSpec((tm, w), lambda b, t: (b * nt + t, 0))
    const = lambda a: pl.BlockSpec(a.shape, lambda b, t: (0,) * a.ndim)
    in_arrays = [x, cos_t, sin_t, wts['wqkv'], wts['wg'], wts['wz'], wts['wab'],
                 wts['convw'], wts['alog'], wts['dtb'], tri, seg]
    in_specs = [tok(D_MODEL), pl.BlockSpec((tm, LANES), lambda b, t: (t, 0)),
                pl.BlockSpec((tm, LANES), lambda b, t: (t, 0))] + [const(a) for a in in_arrays[3:]]
    if has_hist:
        in_arrays += [hist, valid]
        in_specs += [tok(CONV_DIM), tok(1)]
    u_rows = n if has_hist else (n // tm) * SUBLANES
    u_block = tm if has_hist else SUBLANES
    out_shape = [jax.ShapeDtypeStruct((n, Q_COLS), F32), jax.ShapeDtypeStruct((n, KV_COLS), F32),
                 jax.ShapeDtypeStruct((n, KV_COLS), F32), jax.ShapeDtypeStruct((n, QK_COLS), F32),
                 jax.ShapeDtypeStruct((n, QK_COLS), F32), jax.ShapeDtypeStruct((n, Z_COLS), F32),
                 jax.ShapeDtypeStruct((n, Z_COLS), F32), jax.ShapeDtypeStruct((n, LANES), F32),
                 jax.ShapeDtypeStruct((u_rows, CONV_DIM), F32)]
    out_specs = [tok(Q_COLS), tok(KV_COLS), tok(KV_COLS), tok(QK_COLS), tok(QK_COLS), tok(Z_COLS),
                 tok(Z_COLS), tok(LANES),
                 pl.BlockSpec((u_block, CONV_DIM), lambda b, t: (b * nt + t, 0))]
    return pl.pallas_call(
        functools.partial(_proj_kernel, tm=tm, has_hist=has_hist, full_u=has_hist),
        out_shape=out_shape, grid=(n_seq, nt), in_specs=in_specs, out_specs=out_specs,
        scratch_shapes=[pltpu.VMEM((tm + SUBLANES, CONV_DIM), F32)],
        compiler_params=_cparams(("arbitrary", "arbitrary")),
        name="proj_hist" if has_hist else "proj",
    )(*in_arrays)


def _attn_kernel(q_ref, kc_ref, vc_ref, kp_ref, vp_ref, sink_ref, o_ref, *, tq, q_off, k_lo, k_hi,
                 first_has_prev):
    rows = ATTN_HEADS * tq
    lane = lax.broadcasted_iota(I32, (tq, LANES), 1)
    q = q_ref[...]
    parts = []
    for h in range(ATTN_HEADS):
        kv = h // (ATTN_HEADS // ATTN_KV_HEADS)
        slab = q[:, (h // 2) * LANES:(h // 2 + 1) * LANES]
        if (h % 2) != kv:
            slab = pltpu.roll(slab, HEAD_DIM, axis=1)
        parts.append(jnp.where(_div_pow2(lane, HEAD_DIM) == kv, slab, 0.0))
    q8 = jnp.concatenate(parts, axis=0)

    pad = WINDOW - tq
    kc, vc = kc_ref[...], vc_ref[...]
    if pad:
        zpad = jnp.zeros((pad, LANES), F32)
        kc = jnp.concatenate([kc, zpad], axis=0)
        vc = jnp.concatenate([vc, zpad], axis=0)
    kcat = jnp.concatenate([kp_ref[...].reshape(WINDOW, LANES), kc], axis=0)
    vcat = jnp.concatenate([vp_ref[...].reshape(WINDOW, LANES), vc], axis=0)

    s = _bdot_nt(q8, kcat) * (HEAD_DIM ** -0.5)
    qi = _mod_pow2(lax.broadcasted_iota(I32, (rows, 2 * WINDOW), 0), tq)
    c = lax.broadcasted_iota(I32, (rows, 2 * WINDOW), 1)
    cj = c - WINDOW
    prev_cols = WINDOW if first_has_prev else jnp.where(pl.program_id(1) > 0, WINDOW, 0)
    vis_prev = (c < prev_cols) & (c > qi - q_off)
    vis_cur = (c >= WINDOW) & (cj <= qi) & (cj >= k_lo) & (cj <= k_hi)
    s = jnp.where(vis_prev | vis_cur, s, NEG_BIG)
    sink = sink_ref[...]
    m = jnp.maximum(jnp.max(s, axis=1, keepdims=True), sink)
    p = jnp.exp(s - m)
    denom = jnp.sum(p, axis=1, keepdims=True) + jnp.exp(sink - m)
    o8 = _bdot(p, vcat) / denom

    for j in range(Q_COLS // LANES):
        he, ho = 2 * j, 2 * j + 1
        kv = he // (ATTN_HEADS // ATTN_KV_HEADS)
        re = o8[he * tq:(he + 1) * tq, :]
        ro = o8[ho * tq:(ho + 1) * tq, :]
        if kv == 0:
            ro = pltpu.roll(ro, HEAD_DIM, axis=1)
        else:
            re = pltpu.roll(re, HEAD_DIM, axis=1)
        o_ref[:, j * LANES:(j + 1) * LANES] = jnp.where(lane < HEAD_DIM, re, ro)


def _sink_rows(sinks, tq):
    return jnp.repeat(sinks.astype(F32), tq)[:, None]


def _attn_prompt(q, k, v, sinks, n_seq):
    n = q.shape[0]
    nb = n // n_seq // WINDOW
    cur = lambda w: pl.BlockSpec((WINDOW, w), lambda b, i: (b * nb + i, 0))
    prev = pl.BlockSpec((WINDOW, LANES), lambda b, i: (b * nb + jnp.maximum(i - 1, 0), 0))
    return pl.pallas_call(
        functools.partial(_attn_kernel, tq=WINDOW, q_off=0, k_lo=0, k_hi=WINDOW - 1,
                          first_has_prev=False),
        out_shape=jax.ShapeDtypeStruct((n, Q_COLS), F32), grid=(n_seq, nb),
        in_specs=[cur(Q_COLS), cur(LANES), cur(LANES), prev, prev,
                  pl.BlockSpec((ATTN_HEADS * WINDOW, 1), lambda b, i: (0, 0))],
        out_specs=cur(Q_COLS),
        compiler_params=_cparams(("arbitrary", "arbitrary")), name="attn_prompt",
    )(q, k, v, k, v, _sink_rows(sinks, WINDOW))


def _attn_sample(q, k, v, cache_k, cache_v, sinks, n_seq):
    tq = SAMPLE_SLOTS
    cur = lambda w: pl.BlockSpec((tq, w), lambda b, i: (b, 0))
    prev = pl.BlockSpec((1, WINDOW, LANES), lambda b, i: (b, 0, 0))
    return pl.pallas_call(
        functools.partial(_attn_kernel, tq=tq, q_off=SAMPLE_FIRST, k_lo=SAMPLE_FIRST,
                          k_hi=SAMPLE_FIRST + 3, first_has_prev=True),
        out_shape=jax.ShapeDtypeStruct((n_seq * tq, Q_COLS), F32), grid=(n_seq, 1),
        in_specs=[cur(Q_COLS), cur(LANES), cur(LANES), prev, prev,
                  pl.BlockSpec((ATTN_HEADS * tq, 1), lambda b, i: (0, 0))],
        out_specs=cur(Q_COLS),
        compiler_params=_cparams(("arbitrary", "arbitrary")), name="attn_sample",
    )(q, k, v, cache_k, cache_v, _sink_rows(sinks, tq))


def _unit_lower_inverse(m, eye, same_base):
    c = m.shape[0]
    d = jnp.where(same_base, m, 0.0)
    l = m - d
    x = eye - d
    p = d
    for _ in range(int(math.log2(INV_BASE)) - 1):
        p = _bdot(p, p)
        x = x + _bdot(x, p)
    nblk = c // INV_BASE
    if nblk == 1:
        return x
    nn = _bdot(x, l)
    y = eye - nn
    pw = nn
    for _ in range(int(math.log2(nblk)) - 1):
        pw = _bdot(pw, pw)
        y = y + _bdot(y, pw)
    return _bdot(y, x)


def _gdn_intra(q, k, v, gc, gl, beta, same_seq, low_incl, low_strict, eye, same_base):
    e_gc = jnp.exp(gc)
    gc_row = jnp.sum(jnp.where(eye > 0, gc, 0.0), axis=0, keepdims=True)
    diff = jnp.where(low_incl, gc - gc_row, 0.0)
    decay = jnp.where(low_incl, jnp.exp(diff), 0.0)
    kb = k * beta
    vb = v * beta
    m = jnp.where(low_strict, _bdot_nt(kb, k) * decay, 0.0)
    tmat = _unit_lower_inverse(m, eye, same_base)
    u = _bdot(tmat, vb)
    w = _bdot(tmat, kb * e_gc)
    attn = jnp.where(low_incl, _bdot_nt(q, k) * decay, 0.0)
    q_dec = q * e_gc
    k_dec = k * jnp.exp(gl - gc)
    return u, w, attn, q_dec, k_dec


def _chunk_masks(c, seq_len):
    i = lax.broadcasted_iota(I32, (c, c), 0)
    j = lax.broadcasted_iota(I32, (c, c), 1)
    same_seq = _div_pow2(i, seq_len) == _div_pow2(j, seq_len)
    low_incl = same_seq & (i >= j)
    low_strict = same_seq & (i > j)
    eye = (i == j).astype(F32)
    same_base = _div_pow2(i, INV_BASE) == _div_pow2(j, INV_BASE)
    return same_seq, low_incl, low_strict, eye, same_base


def _gated_rms(o, z, nw):
    o = o * lax.rsqrt(jnp.mean(o * o, axis=1, keepdims=True) + NORM_EPS) * nw
    return o * _silu(z)


def _gdn_prompt_kernel(qg_ref, kg_ref, vg_ref, z_ref, gcb_ref, nw_ref, o_ref, s_out_ref, s_scr):
    c = GDN_CHUNK
    n = pl.program_id(1)

    @pl.when(n == 0)
    def _():
        s_scr[...] = jnp.zeros_like(s_scr)

    masks = _chunk_masks(c, c)
    gcb = gcb_ref[...]
    nw = nw_ref[...]
    for h in range(GDN_HEADS):
        sl = slice(h * GDN_DK, (h + 1) * GDN_DK)
        gc = gcb[:, h:h + 1]
        beta = gcb[:, GDN_HEADS + h:GDN_HEADS + h + 1]
        gl = gcb[:, 2 * GDN_HEADS + h:2 * GDN_HEADS + h + 1]
        u, w, attn, q_dec, k_dec = _gdn_intra(qg_ref[:, sl], kg_ref[:, sl], vg_ref[:, sl],
                                              gc, gl, beta, *masks)
        s = s_scr[h]
        v_new = u - _bdot(w, s)
        o = _bdot(q_dec, s) + _bdot(attn, v_new)
        s_new = s * jnp.exp(gl[0:1, :]) + _bdot_tn(k_dec, v_new)
        s_scr[h] = s_new
        o_ref[:, sl] = _gated_rms(o, z_ref[:, sl], nw)

    @pl.when(n == pl.num_programs(1) - 1)
    def _():
        s_out_ref[...] = s_scr[...].reshape(s_out_ref.shape)


def _gdn_prompt(qg, kg, vg, z, gcb, norm_w, n_seq):
    n = qg.shape[0]
    nc = n // n_seq // GDN_CHUNK
    tok = lambda w: pl.BlockSpec((GDN_CHUNK, w), lambda b, i: (b * nc + i, 0))
    return pl.pallas_call(
        _gdn_prompt_kernel,
        out_shape=[jax.ShapeDtypeStruct((n, Z_COLS), F32),
                   jax.ShapeDtypeStruct((n_seq, GDN_HEADS, GDN_DK, GDN_DV), F32)],
        grid=(n_seq, nc),
        in_specs=[tok(QK_COLS), tok(QK_COLS), tok(Z_COLS), tok(Z_COLS), tok(LANES),
                  pl.BlockSpec((1, GDN_DV), lambda b, i: (0, 0))],
        out_specs=[tok(Z_COLS),
                   pl.BlockSpec((1, GDN_HEADS, GDN_DK, GDN_DV), lambda b, i: (b, 0, 0, 0))],
        scratch_shapes=[pltpu.VMEM((GDN_HEADS, GDN_DK, GDN_DV), F32)],
        compiler_params=_cparams(("arbitrary", "arbitrary")), name="gdn_prompt",
    )(qg, kg, vg, z, gcb, norm_w)


def _gdn_sample_kernel(qg_ref, kg_ref, vg_ref, z_ref, gcb_ref, nw_ref, s_in_ref, o_ref, s_out_ref,
                       ws_scr, qs_scr):
    c = GDN_CHUNK
    h = pl.program_id(1)
    n_sub = c // SAMPLE_SLOTS
    masks = _chunk_masks(c, SAMPLE_SLOTS)
    gcb = gcb_ref[...]
    lane = lax.broadcasted_iota(I32, (c, LANES), 1)
    pick = lambda off: jnp.sum(jnp.where(lane == h + off, gcb, 0.0), axis=1, keepdims=True)
    gc, beta, gl = pick(0), pick(GDN_HEADS), pick(2 * GDN_HEADS)
    u, w, attn, q_dec, k_dec = _gdn_intra(qg_ref[...], kg_ref[...], vg_ref[...], gc, gl, beta, *masks)
    for s_i in range(n_sub):
        rs = slice(s_i * SAMPLE_SLOTS, (s_i + 1) * SAMPLE_SLOTS)
        st = s_in_ref[s_i, 0]
        both = jnp.concatenate([w[rs, :], q_dec[rs, :]], axis=0)
        r = _bdot(both, st)
        ws_scr[rs, :] = r[:SAMPLE_SLOTS, :]
        qs_scr[rs, :] = r[SAMPLE_SLOTS:, :]
    v_new = u - ws_scr[...]
    o = qs_scr[...] + _bdot(attn, v_new)
    row = lax.broadcasted_iota(I32, (c, LANES), 0)
    egl = jnp.exp(gl)
    for s_i in range(n_sub):
        in_seq = _div_pow2(row, SAMPLE_SLOTS) == s_i
        kd = jnp.where(in_seq, k_dec, 0.0)
        st = s_in_ref[s_i, 0]
        s_out_ref[s_i, 0] = st * egl[s_i * SAMPLE_SLOTS:s_i * SAMPLE_SLOTS + 1, :] + _bdot_tn(kd, v_new)
    o_ref[...] = _gated_rms(o, z_ref[...], nw_ref[...])


def _gdn_sample(qg, kg, vg, z, gcb, norm_w, state):
    n = qg.shape[0]
    nblk = n // GDN_CHUNK
    n_sub = GDN_CHUNK // SAMPLE_SLOTS
    head = pl.BlockSpec((GDN_CHUNK, GDN_DK), lambda i, h: (i, h))
    st = pl.BlockSpec((n_sub, 1, GDN_DK, GDN_DV), lambda i, h: (i, h, 0, 0))
    return pl.pallas_call(
        _gdn_sample_kernel,
        out_shape=[jax.ShapeDtypeStruct((n, Z_COLS), F32),
                   jax.ShapeDtypeStruct(state.shape, F32)],
        grid=(nblk, GDN_HEADS),
        in_specs=[head, head, head, head, pl.BlockSpec((GDN_CHUNK, LANES), lambda i, h: (i, 0)),
                  pl.BlockSpec((1, GDN_DV), lambda i, h: (0, 0)), st],
        out_specs=[head, st],
        scratch_shapes=[pltpu.VMEM((GDN_CHUNK, GDN_DV), F32), pltpu.VMEM((GDN_CHUNK, GDN_DV), F32)],
        compiler_params=_cparams(("arbitrary", "arbitrary")), name="gdn_sample",
    )(qg, kg, vg, z, gcb, norm_w, state)


def _post_kernel(a_ref, g_ref, x_ref, wo_ref, ln_g_ref, ln_b_ref, wr_ref, x1_ref, route_ref, *, tm):
    mix = (jnp.dot(a_ref[...].astype(BF16), wo_ref[0:Q_COLS, :], preferred_element_type=F32)
           + jnp.dot(g_ref[...].astype(BF16), wo_ref[Q_COLS:, :], preferred_element_type=F32))
    x1 = _layer_norm(DEEPNORM_ALPHA * x_ref[...] + mix, ln_g_ref[...], ln_b_ref[...])
    x1_ref[...] = x1

    xh = x1.astype(BF16)
    xm = (x1 - xh.astype(F32)).astype(BF16)
    wh, wm = wr_ref[0], wr_ref[1]
    d = lambda a, b: jnp.dot(a, b, preferred_element_type=F32)
    lg = d(xh, wh) + d(xh, wm) + d(xm, wh)
    lane = lax.broadcasted_iota(I32, (tm, LANES), 1)
    lane_f = lane.astype(F32)
    big = float(LANES)

    def first_max(vals, mask):
        v = jnp.where(mask, vals, NEG_BIG)
        mx = jnp.max(v, axis=1, keepdims=True)
        idx = jnp.min(jnp.where(mask & (v == mx), lane_f, big), axis=1, keepdims=True)
        return mx, idx

    gmask = lane < N_GROUPS
    gmax, gidx = first_max(lg, gmask)
    gden = jnp.sum(jnp.where(gmask, jnp.exp(lg - gmax), 0.0), axis=1, keepdims=True)
    g_top_p = 1.0 / gden
    e_lane = lane - N_GROUPS
    e_group = _div_pow2(jnp.maximum(e_lane, 0), EXPERTS_PER_GROUP).astype(F32)
    emask = (e_lane >= 0) & (e_lane < N_EXPERTS) & (e_group == gidx)
    m1, i1 = first_max(lg, emask)
    eden = jnp.sum(jnp.where(emask, jnp.exp(lg - m1), 0.0), axis=1, keepdims=True)
    m2, i2 = first_max(lg, emask & (lane_f != i1))
    p1 = 1.0 / eden
    p2 = jnp.exp(m2 - m1) / eden
    tot = p1 + p2
    gate1 = g_top_p * (p1 / tot)
    gate2 = g_top_p * (p2 / tot)
    route_ref[...] = jnp.where(lane == 0, gate1,
                               jnp.where(lane == 1, gate2,
                                         jnp.where(lane == 2, i1 - N_GROUPS,
                                                   jnp.where(lane == 3, i2 - N_GROUPS, 0.0))))


def _post(attn_o, gdn_o, x, wts):
    n = x.shape[0]
    tm = TOK_TILE
    tok = lambda w: pl.BlockSpec((tm, w), lambda i: (i, 0))
    const = lambda a: pl.BlockSpec(a.shape, lambda i: (0,) * a.ndim)
    consts = [wts['wo'], wts['ln1_g'], wts['ln1_b'], wts['wr']]
    return pl.pallas_call(
        functools.partial(_post_kernel, tm=tm),
        out_shape=[jax.ShapeDtypeStruct((n, D_MODEL), F32), jax.ShapeDtypeStruct((n, LANES), F32)],
        grid=(n // tm,),
        in_specs=[tok(Q_COLS), tok(Z_COLS), tok(D_MODEL)] + [const(a) for a in consts],
        out_specs=[tok(D_MODEL), tok(LANES)],
        compiler_params=_cparams(("arbitrary",)), name="post",
    )(attn_o, gdn_o, x, *consts)


def _dispatch_kernel(dest_ref, x_ref, xs_in_ref, xs_ref, sem, *, tm):
    del xs_in_ref
    base = pl.program_id(0) * (TOP_K * tm)

    def row_copy(r, k):
        d = dest_ref[base + TOP_K * r + k]
        return pltpu.make_async_copy(x_ref.at[pl.ds(r, 1), :], xs_ref.at[pl.ds(d, 1), :], sem)

    def issue(r, carry):
        for k in range(TOP_K):
            row_copy(r, k).start()
        return carry

    lax.fori_loop(0, tm, issue, 0)

    def drain(r, carry):
        for k in range(TOP_K):
            row_copy(r, k).wait()
        return carry

    lax.fori_loop(0, tm, drain, 0)


def _dispatch(dest_flat, x1, xs):
    n = x1.shape[0]
    tm = TOK_TILE
    return pl.pallas_call(
        functools.partial(_dispatch_kernel, tm=tm),
        out_shape=jax.ShapeDtypeStruct(xs.shape, F32),
        grid_spec=pltpu.PrefetchScalarGridSpec(
            num_scalar_prefetch=1, grid=(n // tm,),
            in_specs=[pl.BlockSpec((tm, D_MODEL), lambda i, d: (i, 0)),
                      pl.BlockSpec(memory_space=pl.ANY)],
            out_specs=pl.BlockSpec(memory_space=pl.ANY),
            scratch_shapes=[pltpu.SemaphoreType.DMA(())]),
        input_output_aliases={2: 0},
        compiler_params=_cparams(("arbitrary",)), name="moe_dispatch",
    )(dest_flat, x1, xs)


def _expert_kernel(te_ref, nu_ref, xs_ref, wg_ref, wu_ref, wd_ref, ye_ref):
    del te_ref

    @pl.when(pl.program_id(0) < nu_ref[0])
    def _():
        xb = xs_ref[...].astype(BF16)
        hg = jnp.dot(xb, wg_ref[0].astype(BF16), preferred_element_type=F32)
        hu = jnp.dot(xb, wu_ref[0].astype(BF16), preferred_element_type=F32)
        hh = (_silu(hg) * hu).astype(BF16)
        ye_ref[...] = jnp.dot(hh, wd_ref[0].astype(BF16), preferred_element_type=F32)

    @pl.when(pl.program_id(0) >= nu_ref[0])
    def _():
        ye_ref[...] = jnp.zeros_like(ye_ref)


def _experts(tile_expert, n_used, xs, w_gate, w_up, w_down):
    n_tiles = xs.shape[0] // ROW_TILE
    row_in = lambda i, te, nu: (jnp.minimum(i, nu[0] - 1), 0)
    row = lambda i, te, nu: (i, 0)
    wsel = lambda i, te, nu: (te[i], 0, 0)
    return pl.pallas_call(
        _expert_kernel,
        out_shape=jax.ShapeDtypeStruct(xs.shape, F32),
        grid_spec=pltpu.PrefetchScalarGridSpec(
            num_scalar_prefetch=2, grid=(n_tiles,),
            in_specs=[pl.BlockSpec((ROW_TILE, D_MODEL), row_in),
                      pl.BlockSpec((1, D_MODEL, EXPERT_FF), wsel),
                      pl.BlockSpec((1, D_MODEL, EXPERT_FF), wsel),
                      pl.BlockSpec((1, EXPERT_FF, D_MODEL), wsel)],
            out_specs=pl.BlockSpec((ROW_TILE, D_MODEL), row)),
        compiler_params=_cparams(("arbitrary",)), name="moe_experts",
    )(tile_expert, n_used, xs, w_gate, w_up, w_down)


def _combine_kernel(dest_ref, x1_ref, route_ref, ye_ref, ln_g_ref, ln_b_ref, y_ref, buf, sem, *, tm):
    base = pl.program_id(0) * (TOP_K * tm)

    def row_copy(r, k):
        d = dest_ref[base + TOP_K * r + k]
        return pltpu.make_async_copy(ye_ref.at[pl.ds(d, 1), :], buf.at[k, pl.ds(r, 1), :], sem)

    def issue(r, carry):
        for k in range(TOP_K):
            row_copy(r, k).start()
        return carry

    lax.fori_loop(0, tm, issue, 0)

    def drain(r, carry):
        for k in range(TOP_K):
            row_copy(r, k).wait()
        return carry

    lax.fori_loop(0, tm, drain, 0)

    route = route_ref[...]
    moe = route[:, 0:1] * buf[0] + route[:, 1:2] * buf[1]
    y_ref[...] = _layer_norm(DEEPNORM_ALPHA * x1_ref[...] + moe, ln_g_ref[...], ln_b_ref[...])


def _combine(dest_flat, x1, route, ye, ln_g, ln_b):
    n = x1.shape[0]
    tm = TOK_TILE
    tok = lambda w: pl.BlockSpec((tm, w), lambda i, d: (i, 0))
    const = lambda a: pl.BlockSpec(a.shape, lambda i, d: (0,) * a.ndim)
    return pl.pallas_call(
        functools.partial(_combine_kernel, tm=tm),
        out_shape=jax.ShapeDtypeStruct((n, D_MODEL), F32),
        grid_spec=pltpu.PrefetchScalarGridSpec(
            num_scalar_prefetch=1, grid=(n // tm,),
            in_specs=[tok(D_MODEL), tok(LANES), pl.BlockSpec(memory_space=pl.ANY),
                      const(ln_g), const(ln_b)],
            out_specs=tok(D_MODEL),
            scratch_shapes=[pltpu.VMEM((TOP_K, tm, D_MODEL), F32), pltpu.SemaphoreType.DMA(())]),
        compiler_params=_cparams(("arbitrary",)), name="moe_combine",
    )(dest_flat, x1, route, ye, ln_g, ln_b)


def _routing_plan(route):
    ids = route[:, 2:2 + TOP_K].astype(I32)
    flat = ids.reshape(-1)
    onehot = (flat[:, None] == jnp.arange(N_EXPERTS, dtype=I32)[None, :]).astype(I32)
    csum = jnp.cumsum(onehot, axis=0)
    rank = jnp.sum(onehot * (csum - 1), axis=1)
    counts = csum[-1]
    ntiles = (counts + ROW_TILE - 1) // ROW_TILE
    tile_end = jnp.cumsum(ntiles)
    tile_start = tile_end - ntiles
    dest = jnp.sum(onehot * (tile_start * ROW_TILE)[None, :], axis=1) + rank
    n_used = tile_end[-1]
    return dest.astype(I32), ntiles, tile_start, tile_end, n_used


def _moe(x1, route, wts):
    n = x1.shape[0]
    max_tiles = (TOP_K * n) // ROW_TILE + N_EXPERTS
    dest, ntiles, tile_start, tile_end, n_used = _routing_plan(route)
    t = jnp.arange(max_tiles, dtype=I32)
    te = jnp.sum((t[:, None] >= tile_end[None, :]).astype(I32), axis=1)
    te_last = jnp.sum((n_used - 1 >= tile_end).astype(I32))
    tile_expert = jnp.where(t < n_used, jnp.minimum(te, N_EXPERTS - 1), te_last).astype(I32)
    xs = _dispatch(dest, x1, jnp.zeros((max_tiles * ROW_TILE, D_MODEL), F32))
    ye = _experts(tile_expert, n_used.reshape(1).astype(I32), xs, wts['w_gate'], wts['w_up'],
                  wts['w_down'])
    return _combine(dest, x1, route, ye, wts['ln2_g'], wts['ln2_b'])


def _prep_weights(w_in, w_out, conv_w, a_log, dt_bias, gdn_norm_w, ln1_g, ln1_b, w_router_group,
                  w_router_expert, w_gate, w_up, w_down, ln2_g, ln2_b):
    o1 = Q_COLS + 2 * KV_COLS
    o2 = o1 + CONV_DIM
    o3 = o2 + Z_COLS
    pad_row = lambda v: jnp.pad(v.astype(F32), (0, LANES - v.shape[0]))[None, :]
    wab = jnp.pad(w_in[:, o3:], ((0, 0), (0, LANES - 2 * GDN_HEADS)))
    wr = jnp.pad(jnp.concatenate([w_router_group, w_router_expert], axis=1),
                 ((0, 0), (0, LANES - N_GROUPS - N_EXPERTS)))
    wr_hi = wr.astype(BF16)
    wr_mid = (wr - wr_hi.astype(F32)).astype(BF16)
    return dict(
        wqkv=w_in[:, :o1].astype(BF16), wg=w_in[:, o1:o2].astype(BF16), wz=w_in[:, o2:o3].astype(BF16),
        wab=wab.astype(BF16), convw=conv_w.astype(F32), alog=pad_row(a_log), dtb=pad_row(dt_bias),
        norm_w=gdn_norm_w.astype(F32)[None, :], wo=w_out.astype(BF16),
        ln1_g=ln1_g[None, :], ln1_b=ln1_b[None, :], wr=jnp.stack([wr_hi, wr_mid]),
        w_gate=w_gate, w_up=w_up, w_down=w_down, ln2_g=ln2_g[None, :], ln2_b=ln2_b[None, :])


def _layer(x_prompt, x_sample, cache_k, cache_v, state_gdn, state_conv, wts):
    bp, sp, _ = x_prompt.shape
    bs, ts, _ = x_sample.shape
    n_p = bp * sp

    xp = x_prompt.reshape(n_p, D_MODEL)
    (q, k, v, qg, kg, vg, z, gcb, utail) = _proj(xp, jnp.arange(sp, dtype=I32), wts, GDN_CHUNK, bp)
    attn_p = _attn_prompt(q, k, v, wts['sinks'], bp)
    gdn_p, s_p = _gdn_prompt(qg, kg, vg, z, gcb, wts['norm_w'], bp)
    new_k_p = k.reshape(bp, sp, ATTN_KV_HEADS, HEAD_DIM)[:, sp - WINDOW:]
    new_v_p = v.reshape(bp, sp, ATTN_KV_HEADS, HEAD_DIM)[:, sp - WINDOW:]
    tiles_per_seq = sp // TOK_TILE
    conv_p = utail.reshape(bp, tiles_per_seq, SUBLANES, CONV_DIM)[:, -1, SUBLANES - (CONV_W - 1):]

    lo, hi = SAMPLE_FIRST, SAMPLE_FIRST + ts
    xs_rows = jnp.pad(x_sample, ((0, 0), (lo, SAMPLE_SLOTS - hi), (0, 0))).reshape(bs * SAMPLE_SLOTS, D_MODEL)
    hist = jnp.pad(state_conv, ((0, 0), (0, SAMPLE_SLOTS - lo), (0, 0))).reshape(bs * SAMPLE_SLOTS, CONV_DIM)
    slot = jnp.arange(SAMPLE_SLOTS, dtype=I32)
    valid = jnp.tile(((slot >= lo) & (slot < hi)).astype(F32), bs)[:, None]
    pos_s = jnp.tile(PAST_LEN + slot - lo, bs)
    (q, k, v, qg, kg, vg, z, gcb, u_s) = _proj(xs_rows, pos_s, wts, SAMPLE_SLOTS, 1, hist, valid)
    ck = cache_k.reshape(bs, WINDOW, KV_COLS)
    cv = cache_v.reshape(bs, WINDOW, KV_COLS)
    attn_s = _attn_sample(q, k, v, ck, cv, wts['sinks'], bs)
    gdn_s, s_s = _gdn_sample(qg, kg, vg, z, gcb, wts['norm_w'], state_gdn)
    real = lambda a: a.reshape(bs, SAMPLE_SLOTS, -1)[:, lo:hi]
    k_new, v_new = real(k), real(v)
    new_k_s = jnp.concatenate([ck, k_new], axis=1)[:, -WINDOW:].reshape(bs, WINDOW, ATTN_KV_HEADS, HEAD_DIM)
    new_v_s = jnp.concatenate([cv, v_new], axis=1)[:, -WINDOW:].reshape(bs, WINDOW, ATTN_KV_HEADS, HEAD_DIM)
    conv_s = u_s.reshape(bs, SAMPLE_SLOTS, CONV_DIM)[:, hi - (CONV_W - 1):hi]

    attn_all = jnp.concatenate([attn_p, real(attn_s).reshape(bs * ts, Q_COLS)], axis=0)
    gdn_all = jnp.concatenate([gdn_p, real(gdn_s).reshape(bs * ts, Z_COLS)], axis=0)
    x_all = jnp.concatenate([xp, x_sample.reshape(bs * ts, D_MODEL)], axis=0)
    x1, route = _post(attn_all, gdn_all, x_all, wts)
    y = _moe(x1, route, wts)
    y_p = y[:n_p].reshape(bp, sp, D_MODEL)
    y_s = y[n_p:].reshape(bs, ts, D_MODEL)
    return (y_p, y_s, new_k_p, new_v_p, s_p, conv_p, new_k_s, new_v_s, s_s, conv_s)


def kernel(x_prompt, x_sample, cache_attn_k, cache_attn_v, state_gdn, state_conv, w_in, w_out,
           attn_sinks, conv_w, a_log, dt_bias, gdn_norm_w, ln1_g, ln1_b, w_router_group,
           w_router_expert, w_gate, w_up, w_down, ln2_g, ln2_b):
    assert w_in.shape[0] == DEPTH
    l = 0
    wts = _prep_weights(w_in[l], w_out[l], conv_w[l], a_log[l], dt_bias[l], gdn_norm_w[l], ln1_g[l],
                        ln1_b[l], w_router_group[l], w_router_expert[l], w_gate[l], w_up[l],
                        w_down[l], ln2_g[l], ln2_b[l])
    wts['sinks'] = attn_sinks[l]
    outs = _layer(x_prompt, x_sample, cache_attn_k[l], cache_attn_v[l], state_gdn[l], state_conv[l], wts)
    (y_p, y_s, k_p, v_p, s_p, c_p, k_s, v_s, s_s, c_s) = outs
    add = lambda a: a[None]
    return (y_p, y_s, add(k_p), add(v_p), add(s_p), add(c_p), add(k_s), add(v_s), add(s_s), add(c_s))
```

```python
import functools
import math

import jax
import jax.numpy as jnp
from jax import lax
from jax.experimental import pallas as pl
from jax.experimental.pallas import tpu as pltpu

F32 = jnp.float32
BF16 = jnp.bfloat16
I32 = jnp.int32

D_MODEL = 1024
ATTN_HEADS = 8
ATTN_KV_HEADS = 2
HEAD_DIM = 64
WINDOW = 128
ROT_DIM = HEAD_DIM // 4
ROPE_THETA = 500000.0
GDN_HEADS = 4
GDN_DK = 128
GDN_DV = 128
CONV_W = 4
QK_COLS = GDN_HEADS * GDN_DK
CONV_DIM = 2 * QK_COLS + GDN_HEADS * GDN_DV
Z_COLS = GDN_HEADS * GDN_DV
Q_COLS = ATTN_HEADS * HEAD_DIM
KV_COLS = ATTN_KV_HEADS * HEAD_DIM
N_GROUPS = 4
EXPERTS_PER_GROUP = 8
N_EXPERTS = N_GROUPS * EXPERTS_PER_GROUP
TOP_K = 2
EXPERT_FF = 256
NORM_EPS = 1e-5
L2_EPS = 1e-6
DEPTH = 1
DEEPNORM_ALPHA = (2 * DEPTH) ** 0.25
PAST_LEN = 8192

LANES = 128
SUBLANES = 8
TOK_TILE = 256
GDN_CHUNK = 128
GDN_SEQ_PER_STEP = 4
INV_BASE = 16
SAMPLE_SLOTS = 8
SAMPLE_FIRST = CONV_W - 1
ROW_TILE = 256
VMEM_LIMIT = 48 * 1024 * 1024
NEG_BIG = -1e30


def _cparams(sem):
    return pltpu.CompilerParams(dimension_semantics=sem, vmem_limit_bytes=VMEM_LIMIT)


def _bdot(a, b):
    return jnp.dot(a.astype(BF16), b.astype(BF16), preferred_element_type=F32)


def _bdot_nt(a, b):
    return lax.dot_general(a.astype(BF16), b.astype(BF16), (((1,), (1,)), ((), ())),
                           preferred_element_type=F32)


def _bdot_tn(a, b):
    return lax.dot_general(a.astype(BF16), b.astype(BF16), (((0,), (0,)), ((), ())),
                           preferred_element_type=F32)


def _div_pow2(x, n):
    return jnp.right_shift(x, int(math.log2(n)))


def _mod_pow2(x, n):
    return jnp.bitwise_and(x, n - 1)


def _split3(x):
    hi = x.astype(BF16)
    r = x - hi.astype(F32)
    mid = r.astype(BF16)
    lo = (r - mid.astype(F32)).astype(BF16)
    return hi, mid, lo


def _dot_exact_lhs01(m01, x):
    hi, mid, lo = _split3(x)
    d = lambda t: jnp.dot(m01, t, preferred_element_type=F32)
    return d(hi) + d(mid) + d(lo)


def _sigmoid(x):
    return 1.0 / (1.0 + jnp.exp(-x))


def _silu(x):
    return x * _sigmoid(x)


def _softplus(x):
    return jnp.maximum(x, 0.0) + jnp.log1p(jnp.exp(-jnp.abs(x)))


def _layer_norm(h, g, b):
    mu = jnp.mean(h, axis=-1, keepdims=True)
    d = h - mu
    var = jnp.mean(d * d, axis=-1, keepdims=True)
    return d * lax.rsqrt(var + NORM_EPS) * g + b


def _proj_kernel(*refs, tm, has_hist, full_u):
    it = iter(refs)
    x_ref, cos_ref, sin_ref = next(it), next(it), next(it)
    wqkv_ref, wg_ref, wz_ref, wab_ref = next(it), next(it), next(it), next(it)
    convw_ref, alog_ref, dtb_ref, tri_ref, seg_ref = next(it), next(it), next(it), next(it), next(it)
    hist_ref = valid_ref = None
    if has_hist:
        hist_ref, valid_ref = next(it), next(it)
    q_ref, k_ref, v_ref = next(it), next(it), next(it)
    qg_ref, kg_ref, vg_ref, z_ref, gcb_ref, u_ref = (next(it) for _ in range(6))
    ubuf = next(it)

    t = pl.program_id(1)
    xb = x_ref[...].astype(BF16)
    lane = lax.broadcasted_iota(I32, (tm, LANES), 1)

    pq = jnp.dot(xb, wqkv_ref[...], preferred_element_type=F32)
    cosv, sinv = cos_ref[...], sin_ref[...]
    first_half = _mod_pow2(lane, HEAD_DIM) < (ROT_DIM // 2)

    def rope(s):
        sw = jnp.where(first_half, pltpu.roll(s, LANES - ROT_DIM // 2, axis=1),
                       pltpu.roll(s, ROT_DIM // 2, axis=1))
        return s * cosv + sw * sinv

    for j in range(Q_COLS // LANES):
        q_ref[:, j * LANES:(j + 1) * LANES] = rope(pq[:, j * LANES:(j + 1) * LANES])
    k_ref[...] = rope(pq[:, Q_COLS:Q_COLS + KV_COLS])
    v_ref[...] = pq[:, Q_COLS + KV_COLS:Q_COLS + 2 * KV_COLS]

    u = jnp.dot(xb, wg_ref[...], preferred_element_type=F32)
    if has_hist:
        u = u + hist_ref[...]
    if full_u:
        u_ref[...] = u
    else:
        u_ref[...] = u[tm - SUBLANES:, :]

    @pl.when(t == 0)
    def _():
        ubuf[0:SUBLANES, :] = jnp.zeros((SUBLANES, CONV_DIM), F32)

    @pl.when(t > 0)
    def _():
        ubuf[0:SUBLANES, :] = ubuf[tm:tm + SUBLANES, :]

    ubuf[SUBLANES:SUBLANES + tm, :] = u
    acc = u * convw_ref[CONV_W - 1:CONV_W, :]
    for j in range(1, CONV_W):
        acc = acc + ubuf[SUBLANES - j:SUBLANES - j + tm, :] * convw_ref[CONV_W - 1 - j:CONV_W - j, :]
    c = _silu(acc)
    if has_hist:
        c = c * valid_ref[...]

    def l2n(s):
        return s * lax.rsqrt(jnp.sum(s * s, axis=1, keepdims=True) + L2_EPS)

    for h in range(GDN_HEADS):
        sl = slice(h * GDN_DK, (h + 1) * GDN_DK)
        qg_ref[:, sl] = l2n(c[:, sl]) * (GDN_DK ** -0.5)
        kg_ref[:, sl] = l2n(c[:, QK_COLS + h * GDN_DK:QK_COLS + (h + 1) * GDN_DK])
    vg_ref[...] = c[:, 2 * QK_COLS:]
    z_ref[...] = jnp.dot(xb, wz_ref[...], preferred_element_type=F32)

    ab = jnp.dot(xb, wab_ref[...], preferred_element_type=F32)
    g = -jnp.exp(alog_ref[...]) * _softplus(ab + dtb_ref[...])
    beta = _sigmoid(ab)
    if has_hist:
        g = g * valid_ref[...]
        beta = beta * valid_ref[...]
    g = jnp.where(lane < GDN_HEADS, g, 0.0)
    gc = _dot_exact_lhs01(tri_ref[...], g)
    gl = _dot_exact_lhs01(seg_ref[...], g)
    gcb_ref[...] = jnp.where(lane < GDN_HEADS, gc,
                             jnp.where(lane < 2 * GDN_HEADS, beta,
                                       jnp.where(lane < 3 * GDN_HEADS,
                                                 pltpu.roll(gl, 2 * GDN_HEADS, axis=1), 0.0)))


def _rope_tables(pos):
    half = ROT_DIM // 2
    inv_freq = ROPE_THETA ** (-jnp.arange(half, dtype=F32) * 2.0 / ROT_DIM)
    ang = pos.astype(F32)[:, None] * inv_freq[None, :]
    cos, sin = jnp.cos(ang), jnp.sin(ang)
    p = pos.shape[0]
    cpat = jnp.concatenate([cos, cos, jnp.ones((p, HEAD_DIM - ROT_DIM), F32)], axis=1)
    spat = jnp.concatenate([-sin, sin, jnp.zeros((p, HEAD_DIM - ROT_DIM), F32)], axis=1)
    return jnp.tile(cpat, (1, LANES // HEAD_DIM)), jnp.tile(spat, (1, LANES // HEAD_DIM))


def _segment_matrices(tm, seg_len):
    i = jnp.arange(tm)
    same = (i[:, None] // seg_len) == (i[None, :] // seg_len)
    tri = same & (i[None, :] <= i[:, None])
    return tri.astype(BF16), same.astype(BF16)


def _proj(x, pos, wts, seg_len, n_seq, hist=None, valid=None):
    n = x.shape[0]
    rows = n // n_seq
    tm = min(TOK_TILE, rows)
    nt = rows // tm
    has_hist = hist is not None
    cos_t, sin_t = _rope_tables(pos)
    tri, seg = _segment_matrices(tm, seg_len)

    tok = lambda w: pl.BlockSpec((tm, w), lambda b, t: (b * nt + t, 0))
    const = lambda a: pl.BlockSpec(a.shape, lambda b, t: (0,) * a.ndim)
    in_arrays = [x, cos_t, sin_t, wts['wqkv'], wts['wg'], wts['wz'], wts['wab'],
                 wts['convw'], wts['alog'], wts['dtb'], tri, seg]
    in_specs = [tok(D_MODEL), pl.BlockSpec((tm, LANES), lambda b, t: (t, 0)),
                pl.BlockSpec((tm, LANES), lambda b, t: (t, 0))] + [const(a) for a in in_arrays[3:]]
    if has_hist:
        in_arrays += [hist, valid]
        in_specs += [tok(CONV_DIM), tok(1)]
    u_rows = n if has_hist else (n // tm) * SUBLANES
    u_block = tm if has_hist else SUBLANES
    out_shape = [jax.ShapeDtypeStruct((n, Q_COLS), F32), jax.ShapeDtypeStruct((n, KV_COLS), F32),
                 jax.ShapeDtypeStruct((n, KV_COLS), F32), jax.ShapeDtypeStruct((n, QK_COLS), F32),
                 jax.ShapeDtypeStruct((n, QK_COLS), F32), jax.ShapeDtypeStruct((n, Z_COLS), F32),
                 jax.ShapeDtypeStruct((n, Z_COLS), F32), jax.ShapeDtypeStruct((n, LANES), F32),
                 jax.ShapeDtypeStruct((u_rows, CONV_DIM), F32)]
    out_specs = [tok(Q_COLS), tok(KV_COLS), tok(KV_COLS), tok(QK_COLS), tok(QK_COLS), tok(Z_COLS),
                 tok(Z_COLS), tok(LANES),
                 pl.BlockSpec((u_block, CONV_DIM), lambda b, t: (b * nt + t, 0))]
    return pl.pallas_call(
        functools.partial(_proj_kernel, tm=tm, has_hist=has_hist, full_u=has_hist),
        out_shape=out_shape, grid=(n_seq, nt), in_specs=in_specs, out_specs=out_specs,
        scratch_shapes=[pltpu.VMEM((tm + SUBLANES, CONV_DIM), F32)],
        compiler_params=_cparams(("arbitrary", "arbitrary")),
        name="proj_hist" if has_hist else "proj",
    )(*in_arrays)


def _attn_kernel(q_ref, kc_ref, vc_ref, kp_ref, vp_ref, sink_ref, o_ref, *, tq, q_off, k_lo, k_hi,
                 first_has_prev):
    rows = ATTN_HEADS * tq
    lane = lax.broadcasted_iota(I32, (tq, LANES), 1)
    q = q_ref[...]
    parts = []
    for h in range(ATTN_HEADS):
        kv = h // (ATTN_HEADS // ATTN_KV_HEADS)
        slab = q[:, (h // 2) * LANES:(h // 2 + 1) * LANES]
        if (h % 2) != kv:
            slab = pltpu.roll(slab, HEAD_DIM, axis=1)
        parts.append(jnp.where(_div_pow2(lane, HEAD_DIM) == kv, slab, 0.0))
    q8 = jnp.concatenate(parts, axis=0)

    pad = WINDOW - tq
    kc, vc = kc_ref[...], vc_ref[...]
    if pad:
        zpad = jnp.zeros((pad, LANES), F32)
        kc = jnp.concatenate([kc, zpad], axis=0)
        vc = jnp.concatenate([vc, zpad], axis=0)
    kcat = jnp.concatenate([kp_ref[...].reshape(WINDOW, LANES), kc], axis=0)
    vcat = jnp.concatenate([vp_ref[...].reshape(WINDOW, LANES), vc], axis=0)

    s = _bdot_nt(q8, kcat) * (HEAD_DIM ** -0.5)
    qi = _mod_pow2(lax.broadcasted_iota(I32, (rows, 2 * WINDOW), 0), tq)
    c = lax.broadcasted_iota(I32, (rows, 2 * WINDOW), 1)
    cj = c - WINDOW
    prev_cols = WINDOW if first_has_prev else jnp.where(pl.program_id(1) > 0, WINDOW, 0)
    vis_prev = (c < prev_cols) & (c > qi - q_off)
    vis_cur = (c >= WINDOW) & (cj <= qi) & (cj >= k_lo) & (cj <= k_hi)
    s = jnp.where(vis_prev | vis_cur, s, NEG_BIG)
    sink = sink_ref[...]
    m = jnp.maximum(jnp.max(s, axis=1, keepdims=True), sink)
    p = jnp.exp(s - m)
    denom = jnp.sum(p, axis=1, keepdims=True) + jnp.exp(sink - m)
    o8 = _bdot(p, vcat) / denom

    for j in range(Q_COLS // LANES):
        he, ho = 2 * j, 2 * j + 1
        kv = he // (ATTN_HEADS // ATTN_KV_HEADS)
        re = o8[he * tq:(he + 1) * tq, :]
        ro = o8[ho * tq:(ho + 1) * tq, :]
        if kv == 0:
            ro = pltpu.roll(ro, HEAD_DIM, axis=1)
        else:
            re = pltpu.roll(re, HEAD_DIM, axis=1)
        o_ref[:, j * LANES:(j + 1) * LANES] = jnp.where(lane < HEAD_DIM, re, ro)


def _sink_rows(sinks, tq):
    return jnp.repeat(sinks.astype(F32), tq)[:, None]


def _attn_prompt(q, k, v, sinks, n_seq):
    n = q.shape[0]
    nb = n // n_seq // WINDOW
    cur = lambda w: pl.BlockSpec((WINDOW, w), lambda b, i: (b * nb + i, 0))
    prev = pl.BlockSpec((WINDOW, LANES), lambda b, i: (b * nb + jnp.maximum(i - 1, 0), 0))
    return pl.pallas_call(
        functools.partial(_attn_kernel, tq=WINDOW, q_off=0, k_lo=0, k_hi=WINDOW - 1,
                          first_has_prev=False),
        out_shape=jax.ShapeDtypeStruct((n, Q_COLS), F32), grid=(n_seq, nb),
        in_specs=[cur(Q_COLS), cur(LANES), cur(LANES), prev, prev,
                  pl.BlockSpec((ATTN_HEADS * WINDOW, 1), lambda b, i: (0, 0))],
        out_specs=cur(Q_COLS),
        compiler_params=_cparams(("arbitrary", "arbitrary")), name="attn_prompt",
    )(q, k, v, k, v, _sink_rows(sinks, WINDOW))


def _attn_sample(q, k, v, cache_k, cache_v, sinks, n_seq):
    tq = SAMPLE_SLOTS
    cur = lambda w: pl.BlockSpec((tq, w), lambda b, i: (b, 0))
    prev = pl.BlockSpec((1, WINDOW, LANES), lambda b, i: (b, 0, 0))
    return pl.pallas_call(
        functools.partial(_attn_kernel, tq=tq, q_off=SAMPLE_FIRST, k_lo=SAMPLE_FIRST,
                          k_hi=SAMPLE_FIRST + 3, first_has_prev=True),
        out_shape=jax.ShapeDtypeStruct((n_seq * tq, Q_COLS), F32), grid=(n_seq, 1),
        in_specs=[cur(Q_COLS), cur(LANES), cur(LANES), prev, prev,
                  pl.BlockSpec((ATTN_HEADS * tq, 1), lambda b, i: (0, 0))],
        out_specs=cur(Q_COLS),
        compiler_params=_cparams(("arbitrary", "arbitrary")), name="attn_sample",
    )(q, k, v, cache_k, cache_v, _sink_rows(sinks, tq))


def _each(f, *lists):
    return [f(*args) for args in zip(*lists)]


def _unit_lower_inverse(ms, eye, same_base):
    c = ms[0].shape[0]
    ds = _each(lambda m: jnp.where(same_base, m, 0.0), ms)
    ls = _each(lambda m, d: m - d, ms, ds)
    xs = _each(lambda d: eye - d, ds)
    ps = ds
    for _ in range(int(math.log2(INV_BASE)) - 1):
        ps = _each(_bdot, ps, ps)
        xs = _each(lambda x, p: x + _bdot(x, p), xs, ps)
    nblk = c // INV_BASE
    if nblk == 1:
        return xs
    ns = _each(_bdot, xs, ls)
    ys = _each(lambda n: eye - n, ns)
    pws = ns
    for _ in range(int(math.log2(nblk)) - 1):
        pws = _each(_bdot, pws, pws)
        ys = _each(lambda y, p: y + _bdot(y, p), ys, pws)
    return _each(_bdot, ys, xs)


def _gdn_intra(qs, ks, vs, gcs, gls, betas, same_seq, low_incl, low_strict, eye, same_base):
    del same_seq
    e_gcs = _each(jnp.exp, gcs)

    def decay_of(gc):
        gc_row = jnp.sum(jnp.where(eye > 0, gc, 0.0), axis=0, keepdims=True)
        return jnp.where(low_incl, jnp.exp(jnp.where(low_incl, gc - gc_row, 0.0)), 0.0)

    decays = _each(decay_of, gcs)
    kbs = _each(lambda k, b: k * b, ks, betas)
    vbs = _each(lambda v, b: v * b, vs, betas)
    kks = _each(_bdot_nt, kbs, ks)
    ms = _each(lambda kk, d: jnp.where(low_strict, kk * d, 0.0), kks, decays)
    tmats = _unit_lower_inverse(ms, eye, same_base)
    us = _each(_bdot, tmats, vbs)
    ws = _each(lambda t, kb, e: _bdot(t, kb * e), tmats, kbs, e_gcs)
    qks = _each(_bdot_nt, qs, ks)
    attns = _each(lambda qk, d: jnp.where(low_incl, qk * d, 0.0), qks, decays)
    q_decs = _each(lambda q, e: q * e, qs, e_gcs)
    k_decs = _each(lambda k, gl, gc: k * jnp.exp(gl - gc), ks, gls, gcs)
    return us, ws, attns, q_decs, k_decs


def _chunk_masks(c, seq_len):
    i = lax.broadcasted_iota(I32, (c, c), 0)
    j = lax.broadcasted_iota(I32, (c, c), 1)
    same_seq = _div_pow2(i, seq_len) == _div_pow2(j, seq_len)
    low_incl = same_seq & (i >= j)
    low_strict = same_seq & (i > j)
    eye = (i == j).astype(F32)
    same_base = _div_pow2(i, INV_BASE) == _div_pow2(j, INV_BASE)
    return same_seq, low_incl, low_strict, eye, same_base


def _gated_rms(o, z, nw):
    o = o * lax.rsqrt(jnp.mean(o * o, axis=1, keepdims=True) + NORM_EPS) * nw
    return o * _silu(z)


def _gdn_prompt_kernel(qg_ref, kg_ref, vg_ref, z_ref, gcb_ref, nw_ref, o_ref, s_out_ref, s_scr):
    c = GDN_CHUNK
    n = pl.program_id(1)

    @pl.when(n == 0)
    def _():
        s_scr[...] = jnp.zeros_like(s_scr)

    masks = _chunk_masks(c, c)
    nw = nw_ref[...]
    chains = [(b, h) for b in range(qg_ref.shape[0]) for h in range(GDN_HEADS)]
    hs = lambda h: slice(h * GDN_DK, (h + 1) * GDN_DK)
    col = lambda off: [gcb_ref[b, :, off + h:off + h + 1] for b, h in chains]
    gcs, betas, gls = col(0), col(GDN_HEADS), col(2 * GDN_HEADS)
    qs = [qg_ref[b, :, hs(h)] for b, h in chains]
    ks = [kg_ref[b, :, hs(h)] for b, h in chains]
    vs = [vg_ref[b, :, hs(h)] for b, h in chains]
    us, ws, attns, q_decs, k_decs = _gdn_intra(qs, ks, vs, gcs, gls, betas, *masks)
    ss = [s_scr[b, h] for b, h in chains]
    wss = _each(_bdot, ws, ss)
    qss = _each(_bdot, q_decs, ss)
    v_news = _each(lambda u, x: u - x, us, wss)
    avs = _each(_bdot, attns, v_news)
    kvs = _each(_bdot_tn, k_decs, v_news)
    for (b, h), s, gl, qsv, av, kv in zip(chains, ss, gls, qss, avs, kvs):
        s_scr[b, h] = s * jnp.exp(gl[0:1, :]) + kv
        o_ref[b, :, hs(h)] = _gated_rms(qsv + av, z_ref[b, :, hs(h)], nw)

    @pl.when(n == pl.num_programs(1) - 1)
    def _():
        s_out_ref[...] = s_scr[...]


def _gdn_prompt(qg, kg, vg, z, gcb, norm_w, n_seq):
    n = qg.shape[0]
    s_len = n // n_seq
    nb = min(GDN_SEQ_PER_STEP, n_seq)
    v3 = lambda a: a.reshape(n_seq, s_len, a.shape[-1])
    tok = lambda w: pl.BlockSpec((nb, GDN_CHUNK, w), lambda b, i: (b, i, 0))
    o, s = pl.pallas_call(
        _gdn_prompt_kernel,
        out_shape=[jax.ShapeDtypeStruct((n_seq, s_len, Z_COLS), F32),
                   jax.ShapeDtypeStruct((n_seq, GDN_HEADS, GDN_DK, GDN_DV), F32)],
        grid=(n_seq // nb, s_len // GDN_CHUNK),
        in_specs=[tok(QK_COLS), tok(QK_COLS), tok(Z_COLS), tok(Z_COLS), tok(LANES),
                  pl.BlockSpec((1, GDN_DV), lambda b, i: (0, 0))],
        out_specs=[tok(Z_COLS),
                   pl.BlockSpec((nb, GDN_HEADS, GDN_DK, GDN_DV), lambda b, i: (b, 0, 0, 0))],
        scratch_shapes=[pltpu.VMEM((nb, GDN_HEADS, GDN_DK, GDN_DV), F32)],
        compiler_params=_cparams(("arbitrary", "arbitrary")), name="gdn_prompt",
    )(v3(qg), v3(kg), v3(vg), v3(z), v3(gcb), norm_w)
    return o.reshape(n, Z_COLS), s


def _gdn_sample_kernel(qg_ref, kg_ref, vg_ref, z_ref, gcb_ref, nw_ref, s_in_ref, o_ref, s_out_ref,
                       ws_scr, qs_scr):
    c = GDN_CHUNK
    h = pl.program_id(1)
    n_sub = c // SAMPLE_SLOTS
    masks = _chunk_masks(c, SAMPLE_SLOTS)
    gcb = gcb_ref[...]
    lane = lax.broadcasted_iota(I32, (c, LANES), 1)
    pick = lambda off: jnp.sum(jnp.where(lane == h + off, gcb, 0.0), axis=1, keepdims=True)
    gc, beta, gl = pick(0), pick(GDN_HEADS), pick(2 * GDN_HEADS)
    (u,), (w,), (attn,), (q_dec,), (k_dec,) = _gdn_intra([qg_ref[...]], [kg_ref[...]], [vg_ref[...]],
                                                         [gc], [gl], [beta], *masks)
    for s_i in range(n_sub):
        rs = slice(s_i * SAMPLE_SLOTS, (s_i + 1) * SAMPLE_SLOTS)
        st = s_in_ref[s_i, 0]
        both = jnp.concatenate([w[rs, :], q_dec[rs, :]], axis=0)
        r = _bdot(both, st)
        ws_scr[rs, :] = r[:SAMPLE_SLOTS, :]
        qs_scr[rs, :] = r[SAMPLE_SLOTS:, :]
    v_new = u - ws_scr[...]
    o = qs_scr[...] + _bdot(attn, v_new)
    row = lax.broadcasted_iota(I32, (c, LANES), 0)
    egl = jnp.exp(gl)
    for s_i in range(n_sub):
        in_seq = _div_pow2(row, SAMPLE_SLOTS) == s_i
        kd = jnp.where(in_seq, k_dec, 0.0)
        st = s_in_ref[s_i, 0]
        s_out_ref[s_i, 0] = st * egl[s_i * SAMPLE_SLOTS:s_i * SAMPLE_SLOTS + 1, :] + _bdot_tn(kd, v_new)
    o_ref[...] = _gated_rms(o, z_ref[...], nw_ref[...])


def _gdn_sample(qg, kg, vg, z, gcb, norm_w, state):
    n = qg.shape[0]
    nblk = n // GDN_CHUNK
    n_sub = GDN_CHUNK // SAMPLE_SLOTS
    head = pl.BlockSpec((GDN_CHUNK, GDN_DK), lambda i, h: (i, h))
    st = pl.BlockSpec((n_sub, 1, GDN_DK, GDN_DV), lambda i, h: (i, h, 0, 0))
    return pl.pallas_call(
        _gdn_sample_kernel,
        out_shape=[jax.ShapeDtypeStruct((n, Z_COLS), F32),
                   jax.ShapeDtypeStruct(state.shape, F32)],
        grid=(nblk, GDN_HEADS),
        in_specs=[head, head, head, head, pl.BlockSpec((GDN_CHUNK, LANES), lambda i, h: (i, 0)),
                  pl.BlockSpec((1, GDN_DV), lambda i, h: (0, 0)), st],
        out_specs=[head, st],
        scratch_shapes=[pltpu.VMEM((GDN_CHUNK, GDN_DV), F32), pltpu.VMEM((GDN_CHUNK, GDN_DV), F32)],
        compiler_params=_cparams(("arbitrary", "arbitrary")), name="gdn_sample",
    )(qg, kg, vg, z, gcb, norm_w, state)


def _post_kernel(a_ref, g_ref, x_ref, wo_ref, ln_g_ref, ln_b_ref, wr_ref, x1_ref, route_ref, *, tm):
    mix = (jnp.dot(a_ref[...].astype(BF16), wo_ref[0:Q_COLS, :], preferred_element_type=F32)
           + jnp.dot(g_ref[...].astype(BF16), wo_ref[Q_COLS:, :], preferred_element_type=F32))
    x1 = _layer_norm(DEEPNORM_ALPHA * x_ref[...] + mix, ln_g_ref[...], ln_b_ref[...])
    x1_ref[...] = x1

    xh = x1.astype(BF16)
    xm = (x1 - xh.astype(F32)).astype(BF16)
    wh, wm = wr_ref[0], wr_ref[1]
    d = lambda a, b: jnp.dot(a, b, preferred_element_type=F32)
    lg = d(xh, wh) + d(xh, wm) + d(xm, wh)
    lane = lax.broadcasted_iota(I32, (tm, LANES), 1)
    lane_f = lane.astype(F32)
    big = float(LANES)

    def first_max(vals, mask):
        v = jnp.where(mask, vals, NEG_BIG)
        mx = jnp.max(v, axis=1, keepdims=True)
        idx = jnp.min(jnp.where(mask & (v == mx), lane_f, big), axis=1, keepdims=True)
        return mx, idx

    gmask = lane < N_GROUPS
    gmax, gidx = first_max(lg, gmask)
    gden = jnp.sum(jnp.where(gmask, jnp.exp(lg - gmax), 0.0), axis=1, keepdims=True)
    g_top_p = 1.0 / gden
    e_lane = lane - N_GROUPS
    e_group = _div_pow2(jnp.maximum(e_lane, 0), EXPERTS_PER_GROUP).astype(F32)
    emask = (e_lane >= 0) & (e_lane < N_EXPERTS) & (e_group == gidx)
    m1, i1 = first_max(lg, emask)
    eden = jnp.sum(jnp.where(emask, jnp.exp(lg - m1), 0.0), axis=1, keepdims=True)
    m2, i2 = first_max(lg, emask & (lane_f != i1))
    p1 = 1.0 / eden
    p2 = jnp.exp(m2 - m1) / eden
    tot = p1 + p2
    gate1 = g_top_p * (p1 / tot)
    gate2 = g_top_p * (p2 / tot)
    route_ref[...] = jnp.where(lane == 0, gate1,
                               jnp.where(lane == 1, gate2,
                                         jnp.where(lane == 2, i1 - N_GROUPS,
                                                   jnp.where(lane == 3, i2 - N_GROUPS, 0.0))))


def _post(attn_o, gdn_o, x, wts):
    n = x.shape[0]
    tm = TOK_TILE
    tok = lambda w: pl.BlockSpec((tm, w), lambda i: (i, 0))
    const = lambda a: pl.BlockSpec(a.shape, lambda i: (0,) * a.ndim)
    consts = [wts['wo'], wts['ln1_g'], wts['ln1_b'], wts['wr']]
    return pl.pallas_call(
        functools.partial(_post_kernel, tm=tm),
        out_shape=[jax.ShapeDtypeStruct((n, D_MODEL), F32), jax.ShapeDtypeStruct((n, LANES), F32)],
        grid=(n // tm,),
        in_specs=[tok(Q_COLS), tok(Z_COLS), tok(D_MODEL)] + [const(a) for a in consts],
        out_specs=[tok(D_MODEL), tok(LANES)],
        compiler_params=_cparams(("arbitrary",)), name="post",
    )(attn_o, gdn_o, x, *consts)


def _dispatch_kernel(dest_ref, x_ref, xs_in_ref, xs_ref, sem, *, tm):
    del xs_in_ref
    base = pl.program_id(0) * (TOP_K * tm)

    def row_copy(r, k):
        d = dest_ref[base + TOP_K * r + k]
        return pltpu.make_async_copy(x_ref.at[pl.ds(r, 1), :], xs_ref.at[pl.ds(d, 1), :], sem)

    def issue(r, carry):
        for k in range(TOP_K):
            row_copy(r, k).start()
        return carry

    lax.fori_loop(0, tm, issue, 0)

    def drain(r, carry):
        for k in range(TOP_K):
            row_copy(r, k).wait()
        return carry

    lax.fori_loop(0, tm, drain, 0)


def _dispatch(dest_flat, x1, xs):
    n = x1.shape[0]
    tm = TOK_TILE
    return pl.pallas_call(
        functools.partial(_dispatch_kernel, tm=tm),
        out_shape=jax.ShapeDtypeStruct(xs.shape, F32),
        grid_spec=pltpu.PrefetchScalarGridSpec(
            num_scalar_prefetch=1, grid=(n // tm,),
            in_specs=[pl.BlockSpec((tm, D_MODEL), lambda i, d: (i, 0)),
                      pl.BlockSpec(memory_space=pl.ANY)],
            out_specs=pl.BlockSpec(memory_space=pl.ANY),
            scratch_shapes=[pltpu.SemaphoreType.DMA(())]),
        input_output_aliases={2: 0},
        compiler_params=_cparams(("arbitrary",)), name="moe_dispatch",
    )(dest_flat, x1, xs)


def _expert_kernel(te_ref, nu_ref, xs_ref, wg_ref, wu_ref, wd_ref, ye_ref):
    del te_ref

    @pl.when(pl.program_id(0) < nu_ref[0])
    def _():
        xb = xs_ref[...].astype(BF16)
        hg = jnp.dot(xb, wg_ref[0].astype(BF16), preferred_element_type=F32)
        hu = jnp.dot(xb, wu_ref[0].astype(BF16), preferred_element_type=F32)
        hh = (_silu(hg) * hu).astype(BF16)
        ye_ref[...] = jnp.dot(hh, wd_ref[0].astype(BF16), preferred_element_type=F32)

    @pl.when(pl.program_id(0) >= nu_ref[0])
    def _():
        ye_ref[...] = jnp.zeros_like(ye_ref)


def _experts(tile_expert, n_used, xs, w_gate, w_up, w_down):
    n_tiles = xs.shape[0] // ROW_TILE
    row_in = lambda i, te, nu: (jnp.minimum(i, nu[0] - 1), 0)
    row = lambda i, te, nu: (i, 0)
    wsel = lambda i, te, nu: (te[i], 0, 0)
    return pl.pallas_call(
        _expert_kernel,
        out_shape=jax.ShapeDtypeStruct(xs.shape, F32),
        grid_spec=pltpu.PrefetchScalarGridSpec(
            num_scalar_prefetch=2, grid=(n_tiles,),
            in_specs=[pl.BlockSpec((ROW_TILE, D_MODEL), row_in),
                      pl.BlockSpec((1, D_MODEL, EXPERT_FF), wsel),
                      pl.BlockSpec((1, D_MODEL, EXPERT_FF), wsel),
                      pl.BlockSpec((1, EXPERT_FF, D_MODEL), wsel)],
            out_specs=pl.BlockSpec((ROW_TILE, D_MODEL), row)),
        compiler_params=_cparams(("arbitrary",)), name="moe_experts",
    )(tile_expert, n_used, xs, w_gate, w_up, w_down)


def _combine_kernel(dest_ref, x1_ref, route_ref, ye_ref, ln_g_ref, ln_b_ref, y_ref, buf, sem, *, tm):
    base = pl.program_id(0) * (TOP_K * tm)

    def row_copy(r, k):
        d = dest_ref[base + TOP_K * r + k]
        return pltpu.make_async_copy(ye_ref.at[pl.ds(d, 1), :], buf.at[k, pl.ds(r, 1), :], sem)

    def issue(r, carry):
        for k in range(TOP_K):
            row_copy(r, k).start()
        return carry

    lax.fori_loop(0, tm, issue, 0)

    def drain(r, carry):
        for k in range(TOP_K):
            row_copy(r, k).wait()
        return carry

    lax.fori_loop(0, tm, drain, 0)

    route = route_ref[...]
    moe = route[:, 0:1] * buf[0] + route[:, 1:2] * buf[1]
    y_ref[...] = _layer_norm(DEEPNORM_ALPHA * x1_ref[...] + moe, ln_g_ref[...], ln_b_ref[...])


def _combine(dest_flat, x1, route, ye, ln_g, ln_b):
    n = x1.shape[0]
    tm = TOK_TILE
    tok = lambda w: pl.BlockSpec((tm, w), lambda i, d: (i, 0))
    const = lambda a: pl.BlockSpec(a.shape, lambda i, d: (0,) * a.ndim)
    return pl.pallas_call(
        functools.partial(_combine_kernel, tm=tm),
        out_shape=jax.ShapeDtypeStruct((n, D_MODEL), F32),
        grid_spec=pltpu.PrefetchScalarGridSpec(
            num_scalar_prefetch=1, grid=(n // tm,),
            in_specs=[tok(D_MODEL), tok(LANES), pl.BlockSpec(memory_space=pl.ANY),
                      const(ln_g), const(ln_b)],
            out_specs=tok(D_MODEL),
            scratch_shapes=[pltpu.VMEM((TOP_K, tm, D_MODEL), F32), pltpu.SemaphoreType.DMA(())]),
        compiler_params=_cparams(("arbitrary",)), name="moe_combine",
    )(dest_flat, x1, route, ye, ln_g, ln_b)


def _routing_plan(route):
    ids = route[:, 2:2 + TOP_K].astype(I32)
    flat = ids.reshape(-1)
    onehot = (flat[:, None] == jnp.arange(N_EXPERTS, dtype=I32)[None, :]).astype(I32)
    csum = jnp.cumsum(onehot, axis=0)
    rank = jnp.sum(onehot * (csum - 1), axis=1)
    counts = csum[-1]
    ntiles = (counts + ROW_TILE - 1) // ROW_TILE
    tile_end = jnp.cumsum(ntiles)
    tile_start = tile_end - ntiles
    dest = jnp.sum(onehot * (tile_start * ROW_TILE)[None, :], axis=1) + rank
    n_used = tile_end[-1]
    return dest.astype(I32), ntiles, tile_start, tile_end, n_used


def _moe(x1, route, wts):
    n = x1.shape[0]
    max_tiles = (TOP_K * n) // ROW_TILE + N_EXPERTS
    dest, ntiles, tile_start, tile_end, n_used = _routing_plan(route)
    t = jnp.arange(max_tiles, dtype=I32)
    te = jnp.sum((t[:, None] >= tile_end[None, :]).astype(I32), axis=1)
    te_last = jnp.sum((n_used - 1 >= tile_end).astype(I32))
    tile_expert = jnp.where(t < n_used, jnp.minimum(te, N_EXPERTS - 1), te_last).astype(I32)
    xs = _dispatch(dest, x1, jnp.zeros((max_tiles * ROW_TILE, D_MODEL), F32))
    ye = _experts(tile_expert, n_used.reshape(1).astype(I32), xs, wts['w_gate'], wts['w_up'],
                  wts['w_down'])
    return _combine(dest, x1, route, ye, wts['ln2_g'], wts['ln2_b'])


def _prep_weights(w_in, w_out, conv_w, a_log, dt_bias, gdn_norm_w, ln1_g, ln1_b, w_router_group,
                  w_router_expert, w_gate, w_up, w_down, ln2_g, ln2_b):
    o1 = Q_COLS + 2 * KV_COLS
    o2 = o1 + CONV_DIM
    o3 = o2 + Z_COLS
    pad_row = lambda v: jnp.pad(v.astype(F32), (0, LANES - v.shape[0]))[None, :]
    wab = jnp.pad(w_in[:, o3:], ((0, 0), (0, LANES - 2 * GDN_HEADS)))
    wr = jnp.pad(jnp.concatenate([w_router_group, w_router_expert], axis=1),
                 ((0, 0), (0, LANES - N_GROUPS - N_EXPERTS)))
    wr_hi = wr.astype(BF16)
    wr_mid = (wr - wr_hi.astype(F32)).astype(BF16)
    return dict(
        wqkv=w_in[:, :o1].astype(BF16), wg=w_in[:, o1:o2].astype(BF16), wz=w_in[:, o2:o3].astype(BF16),
        wab=wab.astype(BF16), convw=conv_w.astype(F32), alog=pad_row(a_log), dtb=pad_row(dt_bias),
        norm_w=gdn_norm_w.astype(F32)[None, :], wo=w_out.astype(BF16),
        ln1_g=ln1_g[None, :], ln1_b=ln1_b[None, :], wr=jnp.stack([wr_hi, wr_mid]),
        w_gate=w_gate, w_up=w_up, w_down=w_down, ln2_g=ln2_g[None, :], ln2_b=ln2_b[None, :])


def _layer(x_prompt, x_sample, cache_k, cache_v, state_gdn, state_conv, wts):
    bp, sp, _ = x_prompt.shape
    bs, ts, _ = x_sample.shape
    n_p = bp * sp

    xp = x_prompt.reshape(n_p, D_MODEL)
    (q, k, v, qg, kg, vg, z, gcb, utail) = _proj(xp, jnp.arange(sp, dtype=I32), wts, GDN_CHUNK, bp)
    attn_p = _attn_prompt(q, k, v, wts['sinks'], bp)
    gdn_p, s_p = _gdn_prompt(qg, kg, vg, z, gcb, wts['norm_w'], bp)
    new_k_p = k.reshape(bp, sp, ATTN_KV_HEADS, HEAD_DIM)[:, sp - WINDOW:]
    new_v_p = v.reshape(bp, sp, ATTN_KV_HEADS, HEAD_DIM)[:, sp - WINDOW:]
    tiles_per_seq = sp // TOK_TILE
    conv_p = utail.reshape(bp, tiles_per_seq, SUBLANES, CONV_DIM)[:, -1, SUBLANES - (CONV_W - 1):]

    lo, hi = SAMPLE_FIRST, SAMPLE_FIRST + ts
    xs_rows = jnp.pad(x_sample, ((0, 0), (lo, SAMPLE_SLOTS - hi), (0, 0))).reshape(bs * SAMPLE_SLOTS, D_MODEL)
    hist = jnp.pad(state_conv, ((0, 0), (0, SAMPLE_SLOTS - lo), (0, 0))).reshape(bs * SAMPLE_SLOTS, CONV_DIM)
    slot = jnp.arange(SAMPLE_SLOTS, dtype=I32)
    valid = jnp.tile(((slot >= lo) & (slot < hi)).astype(F32), bs)[:, None]
    pos_s = jnp.tile(PAST_LEN + slot - lo, bs)
    (q, k, v, qg, kg, vg, z, gcb, u_s) = _proj(xs_rows, pos_s, wts, SAMPLE_SLOTS, 1, hist, valid)
    ck = cache_k.reshape(bs, WINDOW, KV_COLS)
    cv = cache_v.reshape(bs, WINDOW, KV_COLS)
    attn_s = _attn_sample(q, k, v, ck, cv, wts['sinks'], bs)
    gdn_s, s_s = _gdn_sample(qg, kg, vg, z, gcb, wts['norm_w'], state_gdn)
    real = lambda a: a.reshape(bs, SAMPLE_SLOTS, -1)[:, lo:hi]
    k_new, v_new = real(k), real(v)
    new_k_s = jnp.concatenate([ck, k_new], axis=1)[:, -WINDOW:].reshape(bs, WINDOW, ATTN_KV_HEADS, HEAD_DIM)
    new_v_s = jnp.concatenate([cv, v_new], axis=1)[:, -WINDOW:].reshape(bs, WINDOW, ATTN_KV_HEADS, HEAD_DIM)
    conv_s = u_s.reshape(bs, SAMPLE_SLOTS, CONV_DIM)[:, hi - (CONV_W - 1):hi]

    attn_all = jnp.concatenate([attn_p, real(attn_s).reshape(bs * ts, Q_COLS)], axis=0)
    gdn_all = jnp.concatenate([gdn_p, real(gdn_s).reshape(bs * ts, Z_COLS)], axis=0)
    x_all = jnp.concatenate([xp, x_sample.reshape(bs * ts, D_MODEL)], axis=0)
    x1, route = _post(attn_all, gdn_all, x_all, wts)
    y = _moe(x1, route, wts)
    y_p = y[:n_p].reshape(bp, sp, D_MODEL)
    y_s = y[n_p:].reshape(bs, ts, D_MODEL)
    return (y_p, y_s, new_k_p, new_v_p, s_p, conv_p, new_k_s, new_v_s, s_s, conv_s)


def kernel(x_prompt, x_sample, cache_attn_k, cache_attn_v, state_gdn, state_conv, w_in, w_out,
           attn_sinks, conv_w, a_log, dt_bias, gdn_norm_w, ln1_g, ln1_b, w_router_group,
           w_router_expert, w_gate, w_up, w_down, ln2_g, ln2_b):
    assert w_in.shape[0] == DEPTH
    l = 0
    wts = _prep_weights(w_in[l], w_out[l], conv_w[l], a_log[l], dt_bias[l], gdn_norm_w[l], ln1_g[l],
                        ln1_b[l], w_router_group[l], w_router_expert[l], w_gate[l], w_up[l],
                        w_down[l], ln2_g[l], ln2_b[l])
    wts['sinks'] = attn_sinks[l]
    outs = _layer(x_prompt, x_sample, cache_attn_k[l], cache_attn_v[l], state_gdn[l], state_conv[l], wts)
    (y_p, y_s, k_p, v_p, s_p, c_p, k_s, v_s, s_s, c_s) = outs
    add = lambda a: a[None]
    return (y_p, y_s, add(k_p), add(v_p), add(s_p), add(c_p), add(k_s), add(v_s), add(s_s), add(c_s))
```

```python
import functools
import math

import jax
import jax.numpy as jnp
from jax import lax
from jax.experimental import pallas as pl
from jax.experimental.pallas import tpu as pltpu

F32 = jnp.float32
BF16 = jnp.bfloat16
I32 = jnp.int32

D_MODEL = 1024
ATTN_HEADS = 8
ATTN_KV_HEADS = 2
HEAD_DIM = 64
WINDOW = 128
ROT_DIM = HEAD_DIM // 4
ROPE_THETA = 500000.0
GDN_HEADS = 4
GDN_DK = 128
GDN_DV = 128
CONV_W = 4
QK_COLS = GDN_HEADS * GDN_DK
CONV_DIM = 2 * QK_COLS + GDN_HEADS * GDN_DV
Z_COLS = GDN_HEADS * GDN_DV
Q_COLS = ATTN_HEADS * HEAD_DIM
KV_COLS = ATTN_KV_HEADS * HEAD_DIM
N_GROUPS = 4
EXPERTS_PER_GROUP = 8
N_EXPERTS = N_GROUPS * EXPERTS_PER_GROUP
TOP_K = 2
EXPERT_FF = 256
NORM_EPS = 1e-5
L2_EPS = 1e-6
DEPTH = 1
DEEPNORM_ALPHA = (2 * DEPTH) ** 0.25
PAST_LEN = 8192

LANES = 128
SUBLANES = 8
TOK_TILE = 256
GDN_CHUNK = 128
GDN_SEQ_PER_STEP = 4
ATTN_BLOCKS_PER_STEP = 4
ATTN_SEQS_PER_STEP = 8
INV_BASE = 16
SAMPLE_SLOTS = 8
SAMPLE_FIRST = CONV_W - 1
ROW_TILE = 256
PERM_ROWS = TOP_K * TOK_TILE + N_EXPERTS * SUBLANES
PERM_SLABS = PERM_ROWS // SUBLANES
VMEM_LIMIT = 48 * 1024 * 1024
NEG_BIG = -1e30


def _cparams(sem):
    return pltpu.CompilerParams(dimension_semantics=sem, vmem_limit_bytes=VMEM_LIMIT)


def _bdot(a, b):
    return jnp.dot(a.astype(BF16), b.astype(BF16), preferred_element_type=F32)


def _bdot_nt(a, b):
    return lax.dot_general(a.astype(BF16), b.astype(BF16), (((1,), (1,)), ((), ())),
                           preferred_element_type=F32)


def _bdot_tn(a, b):
    return lax.dot_general(a.astype(BF16), b.astype(BF16), (((0,), (0,)), ((), ())),
                           preferred_element_type=F32)


def _div_pow2(x, n):
    return jnp.right_shift(x, int(math.log2(n)))


def _mod_pow2(x, n):
    return jnp.bitwise_and(x, n - 1)


def _split3(x):
    hi = x.astype(BF16)
    r = x - hi.astype(F32)
    mid = r.astype(BF16)
    lo = (r - mid.astype(F32)).astype(BF16)
    return hi, mid, lo


def _dot_exact_lhs01(m01, x):
    hi, mid, lo = _split3(x)
    d = lambda t: jnp.dot(m01, t, preferred_element_type=F32)
    return d(hi) + d(mid) + d(lo)


def _sigmoid(x):
    return 1.0 / (1.0 + jnp.exp(-x))


def _silu(x):
    return x * _sigmoid(x)


def _softplus(x):
    return jnp.maximum(x, 0.0) + jnp.log1p(jnp.exp(-jnp.abs(x)))


def _layer_norm(h, g, b):
    mu = jnp.mean(h, axis=-1, keepdims=True)
    d = h - mu
    var = jnp.mean(d * d, axis=-1, keepdims=True)
    return d * lax.rsqrt(var + NORM_EPS) * g + b


def _proj_kernel(*refs, tm, has_hist, full_u):
    it = iter(refs)
    x_ref, cos_ref, sin_ref = next(it), next(it), next(it)
    wqkv_ref, wg_ref, wz_ref, wab_ref = next(it), next(it), next(it), next(it)
    convw_ref, alog_ref, dtb_ref, tri_ref, seg_ref = next(it), next(it), next(it), next(it), next(it)
    hist_ref = valid_ref = None
    if has_hist:
        hist_ref, valid_ref = next(it), next(it)
    q_ref, k_ref, v_ref = next(it), next(it), next(it)
    qg_ref, kg_ref, vg_ref, z_ref, gcb_ref, u_ref = (next(it) for _ in range(6))
    ubuf = next(it)

    t = pl.program_id(1)
    xb = x_ref[...].astype(BF16)
    lane = lax.broadcasted_iota(I32, (tm, LANES), 1)

    pq = jnp.dot(xb, wqkv_ref[...], preferred_element_type=F32)
    cosv, sinv = cos_ref[...], sin_ref[...]
    first_half = _mod_pow2(lane, HEAD_DIM) < (ROT_DIM // 2)

    def rope(s):
        sw = jnp.where(first_half, pltpu.roll(s, LANES - ROT_DIM // 2, axis=1),
                       pltpu.roll(s, ROT_DIM // 2, axis=1))
        return s * cosv + sw * sinv

    for j in range(Q_COLS // LANES):
        q_ref[:, j * LANES:(j + 1) * LANES] = rope(pq[:, j * LANES:(j + 1) * LANES])
    k_ref[...] = rope(pq[:, Q_COLS:Q_COLS + KV_COLS])
    v_ref[...] = pq[:, Q_COLS + KV_COLS:Q_COLS + 2 * KV_COLS]

    u = jnp.dot(xb, wg_ref[...], preferred_element_type=F32)
    if has_hist:
        u = u + hist_ref[...]
    if full_u:
        u_ref[...] = u
    else:
        u_ref[...] = u[tm - SUBLANES:, :]

    @pl.when(t == 0)
    def _():
        ubuf[0:SUBLANES, :] = jnp.zeros((SUBLANES, CONV_DIM), F32)

    @pl.when(t > 0)
    def _():
        ubuf[0:SUBLANES, :] = ubuf[tm:tm + SUBLANES, :]

    ubuf[SUBLANES:SUBLANES + tm, :] = u
    acc = u * convw_ref[CONV_W - 1:CONV_W, :]
    for j in range(1, CONV_W):
        acc = acc + ubuf[SUBLANES - j:SUBLANES - j + tm, :] * convw_ref[CONV_W - 1 - j:CONV_W - j, :]
    c = _silu(acc)
    if has_hist:
        c = c * valid_ref[...]

    def l2n(s):
        return s * lax.rsqrt(jnp.sum(s * s, axis=1, keepdims=True) + L2_EPS)

    for h in range(GDN_HEADS):
        sl = slice(h * GDN_DK, (h + 1) * GDN_DK)
        qg_ref[:, sl] = l2n(c[:, sl]) * (GDN_DK ** -0.5)
        kg_ref[:, sl] = l2n(c[:, QK_COLS + h * GDN_DK:QK_COLS + (h + 1) * GDN_DK])
    vg_ref[...] = c[:, 2 * QK_COLS:]
    z_ref[...] = jnp.dot(xb, wz_ref[...], preferred_element_type=F32)

    ab = jnp.dot(xb, wab_ref[...], preferred_element_type=F32)
    g = -jnp.exp(alog_ref[...]) * _softplus(ab + dtb_ref[...])
    beta = _sigmoid(ab)
    if has_hist:
        g = g * valid_ref[...]
        beta = beta * valid_ref[...]
    g = jnp.where(lane < GDN_HEADS, g, 0.0)
    gc = _dot_exact_lhs01(tri_ref[...], g)
    gl = _dot_exact_lhs01(seg_ref[...], g)
    gcb_ref[...] = jnp.where(lane < GDN_HEADS, gc,
                             jnp.where(lane < 2 * GDN_HEADS, beta,
                                       jnp.where(lane < 3 * GDN_HEADS,
                                                 pltpu.roll(gl, 2 * GDN_HEADS, axis=1), 0.0)))


def _rope_tables(pos):
    half = ROT_DIM // 2
    inv_freq = ROPE_THETA ** (-jnp.arange(half, dtype=F32) * 2.0 / ROT_DIM)
    ang = pos.astype(F32)[:, None] * inv_freq[None, :]
    cos, sin = jnp.cos(ang), jnp.sin(ang)
    p = pos.shape[0]
    cpat = jnp.concatenate([cos, cos, jnp.ones((p, HEAD_DIM - ROT_DIM), F32)], axis=1)
    spat = jnp.concatenate([-sin, sin, jnp.zeros((p, HEAD_DIM - ROT_DIM), F32)], axis=1)
    return jnp.tile(cpat, (1, LANES // HEAD_DIM)), jnp.tile(spat, (1, LANES // HEAD_DIM))


def _segment_matrices(tm, seg_len):
    i = jnp.arange(tm)
    same = (i[:, None] // seg_len) == (i[None, :] // seg_len)
    tri = same & (i[None, :] <= i[:, None])
    return tri.astype(BF16), same.astype(BF16)


def _proj(x, pos, wts, seg_len, n_seq, hist=None, valid=None):
    n = x.shape[0]
    rows = n // n_seq
    tm = min(TOK_TILE, rows)
    nt = rows // tm
    has_hist = hist is not None
    cos_t, sin_t = _rope_tables(pos)
    tri, seg = _segment_matrices(tm, seg_len)

    tok = lambda w: pl.BlockSpec((tm, w), lambda b, t: (b * nt + t, 0))
    const = lambda a: pl.BlockSpec(a.shape, lambda b, t: (0,) * a.ndim)
    in_arrays = [x, cos_t, sin_t, wts['wqkv'], wts['wg'], wts['wz'], wts['wab'],
                 wts['convw'], wts['alog'], wts['dtb'], tri, seg]
    in_specs = [tok(D_MODEL), pl.BlockSpec((tm, LANES), lambda b, t: (t, 0)),
                pl.BlockSpec((tm, LANES), lambda b, t: (t, 0))] + [const(a) for a in in_arrays[3:]]
    if has_hist:
        in_arrays += [hist, valid]
        in_specs += [tok(CONV_DIM), tok(1)]
    u_rows = n if has_hist else (n // tm) * SUBLANES
    u_block = tm if has_hist else SUBLANES
    out_shape = [jax.ShapeDtypeStruct((n, Q_COLS), F32), jax.ShapeDtypeStruct((n, KV_COLS), F32),
                 jax.ShapeDtypeStruct((n, KV_COLS), F32), jax.ShapeDtypeStruct((n, QK_COLS), F32),
                 jax.ShapeDtypeStruct((n, QK_COLS), F32), jax.ShapeDtypeStruct((n, Z_COLS), F32),
                 jax.ShapeDtypeStruct((n, Z_COLS), F32), jax.ShapeDtypeStruct((n, LANES), F32),
                 jax.ShapeDtypeStruct((u_rows, CONV_DIM), F32)]
    out_specs = [tok(Q_COLS), tok(KV_COLS), tok(KV_COLS), tok(QK_COLS), tok(QK_COLS), tok(Z_COLS),
                 tok(Z_COLS), tok(LANES),
                 pl.BlockSpec((u_block, CONV_DIM), lambda b, t: (b * nt + t, 0))]
    return pl.pallas_call(
        functools.partial(_proj_kernel, tm=tm, has_hist=has_hist, full_u=has_hist),
        out_shape=out_shape, grid=(n_seq, nt), in_specs=in_specs, out_specs=out_specs,
        scratch_shapes=[pltpu.VMEM((tm + SUBLANES, CONV_DIM), F32)],
        compiler_params=_cparams(("arbitrary", "arbitrary")),
        name="proj_hist" if has_hist else "proj",
    )(*in_arrays)


def _attn_blocks(qs, kcats, vcats, biases, sink, tq):
    lane = lax.broadcasted_iota(I32, (tq, LANES), 1)
    low = lane < HEAD_DIM
    n_slab = Q_COLS // LANES

    def stack(q):
        slabs = [q[:, j * LANES:(j + 1) * LANES] * (HEAD_DIM ** -0.5) for j in range(n_slab)]
        parts = ([jnp.where(low, s, 0.0) for s in slabs] + [jnp.where(low, 0.0, s) for s in slabs])
        return jnp.concatenate(parts, axis=0).astype(BF16)

    def unstack(o8):
        return [jnp.where(low, o8[j * tq:(j + 1) * tq, :], o8[(n_slab + j) * tq:(n_slab + j + 1) * tq, :])
                for j in range(n_slab)]

    ones = jnp.ones((2 * WINDOW, LANES), BF16)
    rows = ATTN_HEADS * tq
    q8s = _each(stack, qs)
    ss = _each(lambda q8, kc, b: _bdot_nt(q8, kc) + b, q8s, kcats, biases)
    ms = _each(lambda s: jnp.maximum(jnp.broadcast_to(jnp.max(s, axis=1, keepdims=True), (rows, LANES)),
                                     sink), ss)
    ps = _each(lambda s, m: jnp.exp(s - jnp.concatenate([m, m], axis=1)).astype(BF16), ss, ms)
    dens = _each(lambda p, m: jnp.dot(p, ones, preferred_element_type=F32) + jnp.exp(sink - m), ps, ms)
    o8s = _each(lambda p, vc, d: jnp.dot(p, vc, preferred_element_type=F32) / d, ps, vcats, dens)
    return _each(unstack, o8s)


def _attn_prompt_kernel(q_ref, kc_ref, vc_ref, kp_ref, vp_ref, bias0_ref, bias_ref, sink_ref, o_ref, *,
                        nblk):
    kall = jnp.concatenate([kp_ref[...], kc_ref[...]], axis=0).astype(BF16)
    vall = jnp.concatenate([vp_ref[...], vc_ref[...]], axis=0).astype(BF16)
    win = lambda a, j: a[j * WINDOW:(j + 2) * WINDOW, :]
    qs = [q_ref[j * WINDOW:(j + 1) * WINDOW, :] for j in range(nblk)]
    biases = [bias0_ref[0]] + [bias_ref[...]] * (nblk - 1)
    outs = _attn_blocks(qs, [win(kall, j) for j in range(nblk)], [win(vall, j) for j in range(nblk)],
                        biases, sink_ref[...], WINDOW)
    for j, slabs in enumerate(outs):
        for c, slab in enumerate(slabs):
            o_ref[j * WINDOW:(j + 1) * WINDOW, c * LANES:(c + 1) * LANES] = slab


def _attn_sample_kernel(q_ref, kc_ref, vc_ref, kp_ref, vp_ref, bias_ref, sink_ref, o_ref, *, nseq):
    tq = SAMPLE_SLOTS
    zpad = jnp.zeros((WINDOW - tq, LANES), F32)
    rows = lambda ref, j: ref[j * tq:(j + 1) * tq, :]
    cat = lambda pref, cref, j: jnp.concatenate([pref[j], rows(cref, j), zpad], axis=0).astype(BF16)
    outs = _attn_blocks([rows(q_ref, j) for j in range(nseq)],
                        [cat(kp_ref, kc_ref, j) for j in range(nseq)],
                        [cat(vp_ref, vc_ref, j) for j in range(nseq)],
                        [bias_ref[...]] * nseq, sink_ref[...], tq)
    for j, slabs in enumerate(outs):
        for c, slab in enumerate(slabs):
            o_ref[j * tq:(j + 1) * tq, c * LANES:(c + 1) * LANES] = slab


def _sink_rows(sinks, tq):
    return jnp.broadcast_to(jnp.repeat(sinks.astype(F32), tq)[:, None], (ATTN_HEADS * tq, LANES))


def _attn_bias(tq, q_off, k_lo, k_hi, has_prev):
    qi = (jnp.arange(ATTN_HEADS * tq, dtype=I32) % tq)[:, None]
    c = jnp.arange(2 * WINDOW, dtype=I32)[None, :]
    cj = c - WINDOW
    vis_prev = (c < WINDOW) & (c > qi - q_off) & has_prev
    vis_cur = (c >= WINDOW) & (cj <= qi) & (cj >= k_lo) & (cj <= k_hi)
    return jnp.where(vis_prev | vis_cur, 0.0, NEG_BIG).astype(F32)


def _attn_prompt(q, k, v, sinks, n_seq):
    n = q.shape[0]
    nb = n // n_seq // WINDOW
    nblk = min(ATTN_BLOCKS_PER_STEP, nb)
    steps = nb // nblk
    tq = nblk * WINDOW
    cur = lambda w: pl.BlockSpec((tq, w), lambda b, i: (b * steps + i, 0))
    prev = pl.BlockSpec((WINDOW, LANES), lambda b, i: (b * nb + jnp.maximum(i * nblk - 1, 0), 0))
    bias2 = jnp.stack([_attn_bias(WINDOW, 0, 0, WINDOW - 1, False),
                       _attn_bias(WINDOW, 0, 0, WINDOW - 1, True)])
    rows = ATTN_HEADS * WINDOW
    return pl.pallas_call(
        functools.partial(_attn_prompt_kernel, nblk=nblk),
        out_shape=jax.ShapeDtypeStruct((n, Q_COLS), F32), grid=(n_seq, steps),
        in_specs=[cur(Q_COLS), cur(LANES), cur(LANES), prev, prev,
                  pl.BlockSpec((1, rows, 2 * WINDOW), lambda b, i: (jnp.minimum(i, 1), 0, 0)),
                  pl.BlockSpec((rows, 2 * WINDOW), lambda b, i: (0, 0)),
                  pl.BlockSpec((rows, LANES), lambda b, i: (0, 0))],
        out_specs=cur(Q_COLS),
        compiler_params=_cparams(("arbitrary", "arbitrary")), name="attn_prompt",
    )(q, k, v, k, v, bias2, bias2[1], _sink_rows(sinks, WINDOW))


def _attn_sample(q, k, v, cache_k, cache_v, sinks, n_seq):
    tq = SAMPLE_SLOTS
    nseq = min(ATTN_SEQS_PER_STEP, n_seq)
    cur = lambda w: pl.BlockSpec((nseq * tq, w), lambda b: (b, 0))
    prev = pl.BlockSpec((nseq, WINDOW, LANES), lambda b: (b, 0, 0))
    bias = _attn_bias(tq, SAMPLE_FIRST, SAMPLE_FIRST, SAMPLE_FIRST + 3, True)
    return pl.pallas_call(
        functools.partial(_attn_sample_kernel, nseq=nseq),
        out_shape=jax.ShapeDtypeStruct((n_seq * tq, Q_COLS), F32), grid=(n_seq // nseq,),
        in_specs=[cur(Q_COLS), cur(LANES), cur(LANES), prev, prev,
                  pl.BlockSpec(bias.shape, lambda b: (0, 0)),
                  pl.BlockSpec((ATTN_HEADS * tq, LANES), lambda b: (0, 0))],
        out_specs=cur(Q_COLS),
        compiler_params=_cparams(("arbitrary",)), name="attn_sample",
    )(q, k, v, cache_k, cache_v, bias, _sink_rows(sinks, tq))


def _each(f, *lists):
    return [f(*args) for args in zip(*lists)]


def _unit_lower_inverse(ms, eye, same_base):
    c = ms[0].shape[0]
    ds = _each(lambda m: jnp.where(same_base, m, 0.0), ms)
    ls = _each(lambda m, d: m - d, ms, ds)
    xs = _each(lambda d: eye - d, ds)
    ps = ds
    for _ in range(int(math.log2(INV_BASE)) - 1):
        ps = _each(_bdot, ps, ps)
        xs = _each(lambda x, p: x + _bdot(x, p), xs, ps)
    nblk = c // INV_BASE
    if nblk == 1:
        return xs
    ns = _each(_bdot, xs, ls)
    ys = _each(lambda n: eye - n, ns)
    pws = ns
    for _ in range(int(math.log2(nblk)) - 1):
        pws = _each(_bdot, pws, pws)
        ys = _each(lambda y, p: y + _bdot(y, p), ys, pws)
    return _each(_bdot, ys, xs)


def _gdn_intra(qs, ks, vs, gcs, gls, betas, same_seq, low_incl, low_strict, eye, same_base):
    del same_seq
    e_gcs = _each(jnp.exp, gcs)

    def decay_of(gc):
        gc_row = jnp.sum(jnp.where(eye > 0, gc, 0.0), axis=0, keepdims=True)
        return jnp.where(low_incl, jnp.exp(jnp.where(low_incl, gc - gc_row, 0.0)), 0.0)

    decays = _each(decay_of, gcs)
    kbs = _each(lambda k, b: k * b, ks, betas)
    vbs = _each(lambda v, b: v * b, vs, betas)
    kks = _each(_bdot_nt, kbs, ks)
    ms = _each(lambda kk, d: jnp.where(low_strict, kk * d, 0.0), kks, decays)
    tmats = _unit_lower_inverse(ms, eye, same_base)
    us = _each(_bdot, tmats, vbs)
    ws = _each(lambda t, kb, e: _bdot(t, kb * e), tmats, kbs, e_gcs)
    qks = _each(_bdot_nt, qs, ks)
    attns = _each(lambda qk, d: jnp.where(low_incl, qk * d, 0.0), qks, decays)
    q_decs = _each(lambda q, e: q * e, qs, e_gcs)
    k_decs = _each(lambda k, gl, gc: k * jnp.exp(gl - gc), ks, gls, gcs)
    return us, ws, attns, q_decs, k_decs


def _chunk_masks(c, seq_len):
    i = lax.broadcasted_iota(I32, (c, c), 0)
    j = lax.broadcasted_iota(I32, (c, c), 1)
    same_seq = _div_pow2(i, seq_len) == _div_pow2(j, seq_len)
    low_incl = same_seq & (i >= j)
    low_strict = same_seq & (i > j)
    eye = (i == j).astype(F32)
    same_base = _div_pow2(i, INV_BASE) == _div_pow2(j, INV_BASE)
    return same_seq, low_incl, low_strict, eye, same_base


def _gated_rms(o, z, nw):
    o = o * lax.rsqrt(jnp.mean(o * o, axis=1, keepdims=True) + NORM_EPS) * nw
    return o * _silu(z)


def _gdn_prompt_kernel(qg_ref, kg_ref, vg_ref, z_ref, gcb_ref, nw_ref, o_ref, s_out_ref, s_scr):
    c = GDN_CHUNK
    n = pl.program_id(1)

    @pl.when(n == 0)
    def _():
        s_scr[...] = jnp.zeros_like(s_scr)

    masks = _chunk_masks(c, c)
    nw = nw_ref[...]
    chains = [(b, h) for b in range(qg_ref.shape[0]) for h in range(GDN_HEADS)]
    hs = lambda h: slice(h * GDN_DK, (h + 1) * GDN_DK)
    col = lambda off: [gcb_ref[b, :, off + h:off + h + 1] for b, h in chains]
    gcs, betas, gls = col(0), col(GDN_HEADS), col(2 * GDN_HEADS)
    qs = [qg_ref[b, :, hs(h)] for b, h in chains]
    ks = [kg_ref[b, :, hs(h)] for b, h in chains]
    vs = [vg_ref[b, :, hs(h)] for b, h in chains]
    us, ws, attns, q_decs, k_decs = _gdn_intra(qs, ks, vs, gcs, gls, betas, *masks)
    ss = [s_scr[b, h] for b, h in chains]
    wss = _each(_bdot, ws, ss)
    qss = _each(_bdot, q_decs, ss)
    v_news = _each(lambda u, x: u - x, us, wss)
    avs = _each(_bdot, attns, v_news)
    kvs = _each(_bdot_tn, k_decs, v_news)
    for (b, h), s, gl, qsv, av, kv in zip(chains, ss, gls, qss, avs, kvs):
        s_scr[b, h] = s * jnp.exp(gl[0:1, :]) + kv
        o_ref[b, :, hs(h)] = _gated_rms(qsv + av, z_ref[b, :, hs(h)], nw)

    @pl.when(n == pl.num_programs(1) - 1)
    def _():
        s_out_ref[...] = s_scr[...]


def _gdn_prompt(qg, kg, vg, z, gcb, norm_w, n_seq):
    n = qg.shape[0]
    s_len = n // n_seq
    nb = min(GDN_SEQ_PER_STEP, n_seq)
    v3 = lambda a: a.reshape(n_seq, s_len, a.shape[-1])
    tok = lambda w: pl.BlockSpec((nb, GDN_CHUNK, w), lambda b, i: (b, i, 0))
    o, s = pl.pallas_call(
        _gdn_prompt_kernel,
        out_shape=[jax.ShapeDtypeStruct((n_seq, s_len, Z_COLS), F32),
                   jax.ShapeDtypeStruct((n_seq, GDN_HEADS, GDN_DK, GDN_DV), F32)],
        grid=(n_seq // nb, s_len // GDN_CHUNK),
        in_specs=[tok(QK_COLS), tok(QK_COLS), tok(Z_COLS), tok(Z_COLS), tok(LANES),
                  pl.BlockSpec((1, GDN_DV), lambda b, i: (0, 0))],
        out_specs=[tok(Z_COLS),
                   pl.BlockSpec((nb, GDN_HEADS, GDN_DK, GDN_DV), lambda b, i: (b, 0, 0, 0))],
        scratch_shapes=[pltpu.VMEM((nb, GDN_HEADS, GDN_DK, GDN_DV), F32)],
        compiler_params=_cparams(("arbitrary", "arbitrary")), name="gdn_prompt",
    )(v3(qg), v3(kg), v3(vg), v3(z), v3(gcb), norm_w)
    return o.reshape(n, Z_COLS), s


def _gdn_sample_kernel(qg_ref, kg_ref, vg_ref, z_ref, gcb_ref, nw_ref, s_in_ref, o_ref, s_out_ref,
                       ws_scr, qs_scr):
    c = GDN_CHUNK
    h = pl.program_id(1)
    n_sub = c // SAMPLE_SLOTS
    masks = _chunk_masks(c, SAMPLE_SLOTS)
    gcb = gcb_ref[...]
    lane = lax.broadcasted_iota(I32, (c, LANES), 1)
    pick = lambda off: jnp.sum(jnp.where(lane == h + off, gcb, 0.0), axis=1, keepdims=True)
    gc, beta, gl = pick(0), pick(GDN_HEADS), pick(2 * GDN_HEADS)
    (u,), (w,), (attn,), (q_dec,), (k_dec,) = _gdn_intra([qg_ref[...]], [kg_ref[...]], [vg_ref[...]],
                                                         [gc], [gl], [beta], *masks)
    for s_i in range(n_sub):
        rs = slice(s_i * SAMPLE_SLOTS, (s_i + 1) * SAMPLE_SLOTS)
        st = s_in_ref[s_i, 0]
        both = jnp.concatenate([w[rs, :], q_dec[rs, :]], axis=0)
        r = _bdot(both, st)
        ws_scr[rs, :] = r[:SAMPLE_SLOTS, :]
        qs_scr[rs, :] = r[SAMPLE_SLOTS:, :]
    v_new = u - ws_scr[...]
    o = qs_scr[...] + _bdot(attn, v_new)
    row = lax.broadcasted_iota(I32, (c, LANES), 0)
    egl = jnp.exp(gl)
    for s_i in range(n_sub):
        in_seq = _div_pow2(row, SAMPLE_SLOTS) == s_i
        kd = jnp.where(in_seq, k_dec, 0.0)
        st = s_in_ref[s_i, 0]
        s_out_ref[s_i, 0] = st * egl[s_i * SAMPLE_SLOTS:s_i * SAMPLE_SLOTS + 1, :] + _bdot_tn(kd, v_new)
    o_ref[...] = _gated_rms(o, z_ref[...], nw_ref[...])


def _gdn_sample(qg, kg, vg, z, gcb, norm_w, state):
    n = qg.shape[0]
    nblk = n // GDN_CHUNK
    n_sub = GDN_CHUNK // SAMPLE_SLOTS
    head = pl.BlockSpec((GDN_CHUNK, GDN_DK), lambda i, h: (i, h))
    st = pl.BlockSpec((n_sub, 1, GDN_DK, GDN_DV), lambda i, h: (i, h, 0, 0))
    return pl.pallas_call(
        _gdn_sample_kernel,
        out_shape=[jax.ShapeDtypeStruct((n, Z_COLS), F32),
                   jax.ShapeDtypeStruct(state.shape, F32)],
        grid=(nblk, GDN_HEADS),
        in_specs=[head, head, head, head, pl.BlockSpec((GDN_CHUNK, LANES), lambda i, h: (i, 0)),
                  pl.BlockSpec((1, GDN_DV), lambda i, h: (0, 0)), st],
        out_specs=[head, st],
        scratch_shapes=[pltpu.VMEM((GDN_CHUNK, GDN_DV), F32), pltpu.VMEM((GDN_CHUNK, GDN_DV), F32)],
        compiler_params=_cparams(("arbitrary", "arbitrary")), name="gdn_sample",
    )(qg, kg, vg, z, gcb, norm_w, state)


def _post_kernel(a_ref, g_ref, x_ref, wo_ref, ln_g_ref, ln_b_ref, wr_ref, x1_ref, route_ref, *, tm):
    mix = (jnp.dot(a_ref[...].astype(BF16), wo_ref[0:Q_COLS, :], preferred_element_type=F32)
           + jnp.dot(g_ref[...].astype(BF16), wo_ref[Q_COLS:, :], preferred_element_type=F32))
    x1 = _layer_norm(DEEPNORM_ALPHA * x_ref[...] + mix, ln_g_ref[...], ln_b_ref[...])
    x1_ref[...] = x1

    xh = x1.astype(BF16)
    xm = (x1 - xh.astype(F32)).astype(BF16)
    wh, wm = wr_ref[0], wr_ref[1]
    d = lambda a, b: jnp.dot(a, b, preferred_element_type=F32)
    lg = d(xh, wh) + d(xh, wm) + d(xm, wh)
    lane = lax.broadcasted_iota(I32, (tm, LANES), 1)
    lane_f = lane.astype(F32)
    big = float(LANES)

    def first_max(vals, mask):
        v = jnp.where(mask, vals, NEG_BIG)
        mx = jnp.max(v, axis=1, keepdims=True)
        idx = jnp.min(jnp.where(mask & (v == mx), lane_f, big), axis=1, keepdims=True)
        return mx, idx

    gmask = lane < N_GROUPS
    gmax, gidx = first_max(lg, gmask)
    gden = jnp.sum(jnp.where(gmask, jnp.exp(lg - gmax), 0.0), axis=1, keepdims=True)
    g_top_p = 1.0 / gden
    e_lane = lane - N_GROUPS
    e_group = _div_pow2(jnp.maximum(e_lane, 0), EXPERTS_PER_GROUP).astype(F32)
    emask = (e_lane >= 0) & (e_lane < N_EXPERTS) & (e_group == gidx)
    m1, i1 = first_max(lg, emask)
    eden = jnp.sum(jnp.where(emask, jnp.exp(lg - m1), 0.0), axis=1, keepdims=True)
    m2, i2 = first_max(lg, emask & (lane_f != i1))
    p1 = 1.0 / eden
    p2 = jnp.exp(m2 - m1) / eden
    tot = p1 + p2
    gate1 = g_top_p * (p1 / tot)
    gate2 = g_top_p * (p2 / tot)
    route_ref[...] = jnp.where(lane == 0, gate1,
                               jnp.where(lane == 1, gate2,
                                         jnp.where(lane == 2, i1 - N_GROUPS,
                                                   jnp.where(lane == 3, i2 - N_GROUPS, 0.0))))


def _post(attn_o, gdn_o, x, wts):
    n = x.shape[0]
    tm = TOK_TILE
    tok = lambda w: pl.BlockSpec((tm, w), lambda i: (i, 0))
    const = lambda a: pl.BlockSpec(a.shape, lambda i: (0,) * a.ndim)
    consts = [wts['wo'], wts['ln1_g'], wts['ln1_b'], wts['wr']]
    return pl.pallas_call(
        functools.partial(_post_kernel, tm=tm),
        out_shape=[jax.ShapeDtypeStruct((n, D_MODEL), F32), jax.ShapeDtypeStruct((n, LANES), F32)],
        grid=(n // tm,),
        in_specs=[tok(Q_COLS), tok(Z_COLS), tok(D_MODEL)] + [const(a) for a in consts],
        out_specs=[tok(D_MODEL), tok(LANES)],
        compiler_params=_cparams(("arbitrary",)), name="post_%d" % (n // tm),
    )(attn_o, gdn_o, x, *consts)


def _slab_loop(n, body):
    lax.fori_loop(0, n, lambda j, c: (body(j), c)[1], 0)


def _dispatch_kernel(dst_ref, nslab_ref, x_ref, slot_ref, xs_in_ref, xs_ref, pbuf, sem, *,
                     tile_base, n_tiles):
    del xs_in_ref
    i = pl.program_id(0)
    g = tile_base + i
    cur = lax.rem(i, 2)

    def slab_copy(tile, buf_slot, j):
        d = pl.multiple_of(dst_ref[tile * PERM_SLABS + j], SUBLANES)
        src = pbuf.at[buf_slot, pl.ds(pl.multiple_of(j * SUBLANES, SUBLANES), SUBLANES), :]
        return pltpu.make_async_copy(src, xs_ref.at[pl.ds(d, SUBLANES), :], sem.at[buf_slot])

    r = lax.broadcasted_iota(I32, (PERM_ROWS, TOK_TILE), 0)
    sl = slot_ref[0]
    onehot = jnp.where((r == sl[0:1, :]) | (r == sl[1:2, :]), 1.0, 0.0).astype(BF16)
    pbuf[cur] = jnp.dot(onehot, x_ref[...].astype(BF16), preferred_element_type=F32)

    @pl.when(i > 0)
    def _():
        _slab_loop(nslab_ref[g - 1], lambda j: slab_copy(g - 1, 1 - cur, j).wait())

    _slab_loop(nslab_ref[g], lambda j: slab_copy(g, cur, j).start())

    @pl.when(i == n_tiles - 1)
    def _():
        _slab_loop(nslab_ref[g], lambda j: slab_copy(g, cur, j).wait())


def _dispatch(plan, tile_base, x1, xs):
    n_tiles = x1.shape[0] // TOK_TILE
    return pl.pallas_call(
        functools.partial(_dispatch_kernel, tile_base=tile_base, n_tiles=n_tiles),
        out_shape=jax.ShapeDtypeStruct(xs.shape, F32),
        grid_spec=pltpu.PrefetchScalarGridSpec(
            num_scalar_prefetch=2, grid=(n_tiles,),
            in_specs=[pl.BlockSpec((TOK_TILE, D_MODEL), lambda i, d, ns: (i, 0)),
                      pl.BlockSpec((1, TOP_K, TOK_TILE), lambda i, d, ns: (tile_base + i, 0, 0)),
                      pl.BlockSpec(memory_space=pl.ANY)],
            out_specs=pl.BlockSpec(memory_space=pl.ANY),
            scratch_shapes=[pltpu.VMEM((2, PERM_ROWS, D_MODEL), F32), pltpu.SemaphoreType.DMA((2,))]),
        input_output_aliases={4: 0},
        compiler_params=_cparams(("arbitrary",)), name="moe_dispatch_%d" % tile_base,
    )(plan['slab_dst'], plan['nslab'], x1, plan['slot_rows'], xs)


def _expert_kernel(te_ref, nu_ref, xs_ref, wg_ref, wu_ref, wd_ref, ye_ref):
    del te_ref

    @pl.when(pl.program_id(0) < nu_ref[0])
    def _():
        xb = xs_ref[...].astype(BF16)
        hg = jnp.dot(xb, wg_ref[0].astype(BF16), preferred_element_type=F32)
        hu = jnp.dot(xb, wu_ref[0].astype(BF16), preferred_element_type=F32)
        hh = (_silu(hg) * hu).astype(BF16)
        ye_ref[...] = jnp.dot(hh, wd_ref[0].astype(BF16), preferred_element_type=F32)

    @pl.when(pl.program_id(0) >= nu_ref[0])
    def _():
        ye_ref[...] = jnp.zeros_like(ye_ref)


def _experts(tile_expert, n_used, xs, w_gate, w_up, w_down):
    n_tiles = xs.shape[0] // ROW_TILE
    row_in = lambda i, te, nu: (jnp.minimum(i, nu[0] - 1), 0)
    row = lambda i, te, nu: (i, 0)
    wsel = lambda i, te, nu: (te[i], 0, 0)
    return pl.pallas_call(
        _expert_kernel,
        out_shape=jax.ShapeDtypeStruct(xs.shape, F32),
        grid_spec=pltpu.PrefetchScalarGridSpec(
            num_scalar_prefetch=2, grid=(n_tiles,),
            in_specs=[pl.BlockSpec((ROW_TILE, D_MODEL), row_in),
                      pl.BlockSpec((1, D_MODEL, EXPERT_FF), wsel),
                      pl.BlockSpec((1, D_MODEL, EXPERT_FF), wsel),
                      pl.BlockSpec((1, EXPERT_FF, D_MODEL), wsel)],
            out_specs=pl.BlockSpec((ROW_TILE, D_MODEL), row)),
        compiler_params=_cparams(("arbitrary",)), name="moe_experts",
    )(tile_expert, n_used, xs, w_gate, w_up, w_down)


def _combine_kernel(dst_ref, nslab_ref, x1_ref, route_ref, slot_ref, ye_ref, ln_g_ref, ln_b_ref, y_ref,
                    buf, sem, *, tile_base, n_tiles):
    i = pl.program_id(0)
    g = tile_base + i
    cur = lax.rem(i, 2)

    def slab_copy(tile, buf_slot, j):
        d = pl.multiple_of(dst_ref[tile * PERM_SLABS + j], SUBLANES)
        dst = buf.at[buf_slot, pl.ds(pl.multiple_of(j * SUBLANES, SUBLANES), SUBLANES), :]
        return pltpu.make_async_copy(ye_ref.at[pl.ds(d, SUBLANES), :], dst, sem.at[buf_slot])

    @pl.when(i == 0)
    def _():
        buf[...] = jnp.zeros_like(buf)
        _slab_loop(nslab_ref[g], lambda j: slab_copy(g, cur, j).start())

    @pl.when(i + 1 < n_tiles)
    def _():
        _slab_loop(nslab_ref[g + 1], lambda j: slab_copy(g + 1, 1 - cur, j).start())

    _slab_loop(nslab_ref[g], lambda j: slab_copy(g, cur, j).wait())

    col = lax.broadcasted_iota(I32, (TOK_TILE, PERM_ROWS), 1)
    route = route_ref[...]
    sc = slot_ref[...]
    gmat = (jnp.where(col == sc[:, 0:1], route[:, 0:1], 0.0)
            + jnp.where(col == sc[:, 1:2], route[:, 1:2], 0.0))
    g_hi = gmat.astype(BF16)
    g_lo = (gmat - g_hi.astype(F32)).astype(BF16)
    rows = buf[cur]
    r_hi = rows.astype(BF16)
    r_lo = (rows - r_hi.astype(F32)).astype(BF16)
    d = lambda a, b: jnp.dot(a, b, preferred_element_type=F32)
    moe = d(g_hi, r_hi) + d(g_hi, r_lo) + d(g_lo, r_hi)
    y_ref[...] = _layer_norm(DEEPNORM_ALPHA * x1_ref[...] + moe, ln_g_ref[...], ln_b_ref[...])


def _combine(plan, tile_base, x1, route, ye, ln_g, ln_b):
    n = x1.shape[0]
    n_tiles = n // TOK_TILE
    tok = lambda w: pl.BlockSpec((TOK_TILE, w), lambda i, d, ns: (i, 0))
    const = lambda a: pl.BlockSpec(a.shape, lambda i, d, ns: (0,) * a.ndim)
    return pl.pallas_call(
        functools.partial(_combine_kernel, tile_base=tile_base, n_tiles=n_tiles),
        out_shape=jax.ShapeDtypeStruct((n, D_MODEL), F32),
        grid_spec=pltpu.PrefetchScalarGridSpec(
            num_scalar_prefetch=2, grid=(n_tiles,),
            in_specs=[tok(D_MODEL), tok(LANES),
                      pl.BlockSpec((TOK_TILE, TOP_K), lambda i, d, ns: (tile_base + i, 0)),
                      pl.BlockSpec(memory_space=pl.ANY), const(ln_g), const(ln_b)],
            out_specs=tok(D_MODEL),
            scratch_shapes=[pltpu.VMEM((2, PERM_ROWS, D_MODEL), F32), pltpu.SemaphoreType.DMA((2,))]),
        compiler_params=_cparams(("arbitrary",)), name="moe_combine_%d" % tile_base,
    )(plan['slab_dst'], plan['nslab'], x1, route, plan['slot_cols'], ye, ln_g, ln_b)


def _routing_plan(ids):
    nt = ids.shape[0] // TOK_TILE
    ex = jnp.arange(N_EXPERTS, dtype=I32)
    flat = ids.reshape(nt, TOP_K * TOK_TILE)
    onehot = (flat[:, :, None] == ex).astype(I32)
    csum = jnp.cumsum(onehot, axis=1)
    rank = jnp.sum(onehot * (csum - 1), axis=2)
    cnt = csum[:, -1, :]
    cpad = (cnt + SUBLANES - 1) // SUBLANES * SUBLANES
    seg_end = jnp.cumsum(cpad, axis=1)
    seg_off = seg_end - cpad
    slot = jnp.sum(onehot * seg_off[:, None, :], axis=2) + rank
    run_end = jnp.cumsum(cpad, axis=0)
    ntiles_e = (run_end[-1] + ROW_TILE - 1) // ROW_TILE
    tile_end = jnp.cumsum(ntiles_e)
    dst_run = ((tile_end - ntiles_e) * ROW_TILE)[None, :] + run_end - cpad
    j8 = jnp.arange(PERM_SLABS, dtype=I32) * SUBLANES
    e_of = jnp.minimum(jnp.sum((j8[None, :, None] >= seg_end[:, None, :]).astype(I32), axis=2),
                       N_EXPERTS - 1)
    sel = (e_of[:, :, None] == ex).astype(I32)
    slab_dst = jnp.sum(sel * (dst_run - seg_off)[:, None, :], axis=2) + j8[None, :]
    n_used = tile_end[-1]
    max_tiles = _max_row_tiles(ids.shape[0])
    t = jnp.arange(max_tiles, dtype=I32)
    te = jnp.sum((t[:, None] >= tile_end[None, :]).astype(I32), axis=1)
    te_last = jnp.sum((n_used - 1 >= tile_end).astype(I32))
    pair_slots = slot.reshape(nt, TOK_TILE, TOP_K)
    return dict(
        slab_dst=slab_dst.reshape(-1).astype(I32), nslab=(seg_end[:, -1] // SUBLANES).astype(I32),
        slot_rows=jnp.swapaxes(pair_slots, 1, 2).astype(I32),
        slot_cols=pair_slots.reshape(nt * TOK_TILE, TOP_K).astype(I32),
        tile_expert=jnp.where(t < n_used, jnp.minimum(te, N_EXPERTS - 1), te_last).astype(I32),
        n_used=n_used.reshape(1).astype(I32))


def _max_row_tiles(n_tokens):
    rows = TOP_K * n_tokens + (n_tokens // TOK_TILE) * N_EXPERTS * (SUBLANES - 1)
    return (rows + ROW_TILE - 1) // ROW_TILE + N_EXPERTS


def _moe(x1s, routes, wts):
    ids = jnp.concatenate([r[:, 2:2 + TOP_K] for r in routes], axis=0).astype(I32)
    plan = _routing_plan(ids)
    bases = [0]
    for x1 in x1s[:-1]:
        bases.append(bases[-1] + x1.shape[0] // TOK_TILE)
    xs = jnp.zeros((_max_row_tiles(ids.shape[0]) * ROW_TILE, D_MODEL), F32)
    for base, x1 in zip(bases, x1s):
        xs = _dispatch(plan, base, x1, xs)
    ye = _experts(plan['tile_expert'], plan['n_used'], xs, wts['w_gate'], wts['w_up'], wts['w_down'])
    return [_combine(plan, base, x1, route, ye, wts['ln2_g'], wts['ln2_b'])
            for base, x1, route in zip(bases, x1s, routes)]


def _prep_weights(w_in, w_out, conv_w, a_log, dt_bias, gdn_norm_w, ln1_g, ln1_b, w_router_group,
                  w_router_expert, w_gate, w_up, w_down, ln2_g, ln2_b):
    o1 = Q_COLS + 2 * KV_COLS
    o2 = o1 + CONV_DIM
    o3 = o2 + Z_COLS
    pad_row = lambda v: jnp.pad(v.astype(F32), (0, LANES - v.shape[0]))[None, :]
    wab = jnp.pad(w_in[:, o3:], ((0, 0), (0, LANES - 2 * GDN_HEADS)))
    wr = jnp.pad(jnp.concatenate([w_router_group, w_router_expert], axis=1),
                 ((0, 0), (0, LANES - N_GROUPS - N_EXPERTS)))
    wr_hi = wr.astype(BF16)
    wr_mid = (wr - wr_hi.astype(F32)).astype(BF16)
    group = ATTN_HEADS // ATTN_KV_HEADS
    head_order = jnp.array([h for j in range(group) for h in (j, j + group)], I32)
    col_order = (head_order[:, None] * HEAD_DIM + jnp.arange(HEAD_DIM, dtype=I32)[None, :]).reshape(-1)
    wqkv = jnp.concatenate([w_in[:, :Q_COLS][:, col_order], w_in[:, Q_COLS:o1]], axis=1)
    wo = jnp.concatenate([w_out[:Q_COLS][col_order], w_out[Q_COLS:]], axis=0)
    return dict(
        wqkv=wqkv.astype(BF16), wg=w_in[:, o1:o2].astype(BF16), wz=w_in[:, o2:o3].astype(BF16),
        wab=wab.astype(BF16), convw=conv_w.astype(F32), alog=pad_row(a_log), dtb=pad_row(dt_bias),
        norm_w=gdn_norm_w.astype(F32)[None, :], wo=wo.astype(BF16),
        ln1_g=ln1_g[None, :], ln1_b=ln1_b[None, :], wr=jnp.stack([wr_hi, wr_mid]),
        w_gate=w_gate, w_up=w_up, w_down=w_down, ln2_g=ln2_g[None, :], ln2_b=ln2_b[None, :])


def _layer(x_prompt, x_sample, cache_k, cache_v, state_gdn, state_conv, wts):
    bp, sp, _ = x_prompt.shape
    bs, ts, _ = x_sample.shape
    n_p = bp * sp

    xp = x_prompt.reshape(n_p, D_MODEL)
    (q, k, v, qg, kg, vg, z, gcb, utail) = _proj(xp, jnp.arange(sp, dtype=I32), wts, GDN_CHUNK, bp)
    attn_p = _attn_prompt(q, k, v, wts['sinks'], bp)
    gdn_p, s_p = _gdn_prompt(qg, kg, vg, z, gcb, wts['norm_w'], bp)
    new_k_p = k.reshape(bp, sp, ATTN_KV_HEADS, HEAD_DIM)[:, sp - WINDOW:]
    new_v_p = v.reshape(bp, sp, ATTN_KV_HEADS, HEAD_DIM)[:, sp - WINDOW:]
    tiles_per_seq = sp // TOK_TILE
    conv_p = utail.reshape(bp, tiles_per_seq, SUBLANES, CONV_DIM)[:, -1, SUBLANES - (CONV_W - 1):]

    lo, hi = SAMPLE_FIRST, SAMPLE_FIRST + ts
    xs_rows = jnp.pad(x_sample, ((0, 0), (lo, SAMPLE_SLOTS - hi), (0, 0))).reshape(bs * SAMPLE_SLOTS, D_MODEL)
    hist = jnp.pad(state_conv, ((0, 0), (0, SAMPLE_SLOTS - lo), (0, 0))).reshape(bs * SAMPLE_SLOTS, CONV_DIM)
    slot = jnp.arange(SAMPLE_SLOTS, dtype=I32)
    valid = jnp.tile(((slot >= lo) & (slot < hi)).astype(F32), bs)[:, None]
    pos_s = jnp.tile(PAST_LEN + slot - lo, bs)
    (q, k, v, qg, kg, vg, z, gcb, u_s) = _proj(xs_rows, pos_s, wts, SAMPLE_SLOTS, 1, hist, valid)
    ck = cache_k.reshape(bs, WINDOW, KV_COLS)
    cv = cache_v.reshape(bs, WINDOW, KV_COLS)
    attn_s = _attn_sample(q, k, v, ck, cv, wts['sinks'], bs)
    gdn_s, s_s = _gdn_sample(qg, kg, vg, z, gcb, wts['norm_w'], state_gdn)
    real = lambda a: a.reshape(bs, SAMPLE_SLOTS, -1)[:, lo:hi]
    k_new, v_new = real(k), real(v)
    new_k_s = jnp.concatenate([ck, k_new], axis=1)[:, -WINDOW:].reshape(bs, WINDOW, ATTN_KV_HEADS, HEAD_DIM)
    new_v_s = jnp.concatenate([cv, v_new], axis=1)[:, -WINDOW:].reshape(bs, WINDOW, ATTN_KV_HEADS, HEAD_DIM)
    conv_s = u_s.reshape(bs, SAMPLE_SLOTS, CONV_DIM)[:, hi - (CONV_W - 1):hi]

    x1_p, route_p = _post(attn_p, gdn_p, xp, wts)
    x1_s, route_s = _post(real(attn_s).reshape(bs * ts, Q_COLS), real(gdn_s).reshape(bs * ts, Z_COLS),
                          x_sample.reshape(bs * ts, D_MODEL), wts)
    y_p, y_s = _moe([x1_p, x1_s], [route_p, route_s], wts)
    return (y_p.reshape(bp, sp, D_MODEL), y_s.reshape(bs, ts, D_MODEL), new_k_p, new_v_p, s_p, conv_p,
            new_k_s, new_v_s, s_s, conv_s)


def kernel(x_prompt, x_sample, cache_attn_k, cache_attn_v, state_gdn, state_conv, w_in, w_out,
           attn_sinks, conv_w, a_log, dt_bias, gdn_norm_w, ln1_g, ln1_b, w_router_group,
           w_router_expert, w_gate, w_up, w_down, ln2_g, ln2_b):
    assert w_in.shape[0] == DEPTH
    l = 0
    wts = _prep_weights(w_in[l], w_out[l], conv_w[l], a_log[l], dt_bias[l], gdn_norm_w[l], ln1_g[l],
                        ln1_b[l], w_router_group[l], w_router_expert[l], w_gate[l], w_up[l],
                        w_down[l], ln2_g[l], ln2_b[l])
    wts['sinks'] = attn_sinks[l]
    outs = _layer(x_prompt, x_sample, cache_attn_k[l], cache_attn_v[l], state_gdn[l], state_conv[l], wts)
    (y_p, y_s, k_p, v_p, s_p, c_p, k_s, v_s, s_s, c_s) = outs
    add = lambda a: a[None]
    return (y_p, y_s, add(k_p), add(v_p), add(s_p), add(c_p), add(k_s), add(v_s), add(s_s), add(c_s))
```

```python
import functools
import math

import jax
import jax.numpy as jnp
from jax import lax
from jax.experimental import pallas as pl
from jax.experimental.pallas import tpu as pltpu

F32 = jnp.float32
BF16 = jnp.bfloat16
I32 = jnp.int32

D_MODEL = 1024
ATTN_HEADS = 8
ATTN_KV_HEADS = 2
HEAD_DIM = 64
WINDOW = 128
ROT_DIM = HEAD_DIM // 4
ROPE_THETA = 500000.0
GDN_HEADS = 4
GDN_DK = 128
GDN_DV = 128
CONV_W = 4
QK_COLS = GDN_HEADS * GDN_DK
CONV_DIM = 2 * QK_COLS + GDN_HEADS * GDN_DV
Z_COLS = GDN_HEADS * GDN_DV
Q_COLS = ATTN_HEADS * HEAD_DIM
KV_COLS = ATTN_KV_HEADS * HEAD_DIM
N_GROUPS = 4
EXPERTS_PER_GROUP = 8
N_EXPERTS = N_GROUPS * EXPERTS_PER_GROUP
TOP_K = 2
EXPERT_FF = 256
NORM_EPS = 1e-5
L2_EPS = 1e-6
DEPTH = 1
DEEPNORM_ALPHA = (2 * DEPTH) ** 0.25
PAST_LEN = 8192

LANES = 128
SUBLANES = 8
TOK_TILE = 256
PROJ_TILE = 512
PROJ_SUB = 128
GDN_CHUNK = 128
GDN_SEQ_PER_STEP = 4
ATTN_BLOCKS_PER_STEP = 4
ATTN_SEQS_PER_STEP = 8
INV_BASE = 16
SAMPLE_SLOTS = 8
SAMPLE_FIRST = CONV_W - 1
ROW_TILE = 256
PERM_ROWS = TOP_K * TOK_TILE + N_EXPERTS * SUBLANES
PERM_SLABS = PERM_ROWS // SUBLANES
XS_WORDS = D_MODEL + LANES
ZERO_TABLE = N_EXPERTS * (ROW_TILE // SUBLANES)
VMEM_LIMIT = 48 * 1024 * 1024
NEG_BIG = -1e30


def _cparams(sem):
    return pltpu.CompilerParams(dimension_semantics=sem, vmem_limit_bytes=VMEM_LIMIT)


def _bdot(a, b):
    return jnp.dot(a.astype(BF16), b.astype(BF16), preferred_element_type=F32)


def _bdot_nt(a, b):
    return lax.dot_general(a.astype(BF16), b.astype(BF16), (((1,), (1,)), ((), ())),
                           preferred_element_type=F32)


def _bdot_tn(a, b):
    return lax.dot_general(a.astype(BF16), b.astype(BF16), (((0,), (0,)), ((), ())),
                           preferred_element_type=F32)


def _div_pow2(x, n):
    return jnp.right_shift(x, int(math.log2(n)))


def _mod_pow2(x, n):
    return jnp.bitwise_and(x, n - 1)


def _split3(x):
    hi = x.astype(BF16)
    r = x - hi.astype(F32)
    mid = r.astype(BF16)
    lo = (r - mid.astype(F32)).astype(BF16)
    return hi, mid, lo


def _dot_exact_lhs01(m01, x):
    hi, mid, lo = _split3(x)
    d = lambda t: jnp.dot(m01, t, preferred_element_type=F32)
    return d(hi) + d(mid) + d(lo)


def _sigmoid(x):
    return 1.0 / (1.0 + jnp.exp(-x))


def _silu(x):
    return x * _sigmoid(x)


def _softplus(x):
    return jnp.maximum(x, 0.0) + jnp.log1p(jnp.exp(-jnp.abs(x)))


def _layer_norm(h, g, b):
    mu = jnp.mean(h, axis=-1, keepdims=True)
    d = h - mu
    var = jnp.mean(d * d, axis=-1, keepdims=True)
    return d * lax.rsqrt(var + NORM_EPS) * g + b


def _proj_kernel(*refs, tm, has_hist, full_u):
    it = iter(refs)
    x_ref, cos_ref, sin_ref = next(it), next(it), next(it)
    wqkv_ref, wg_ref, wz_ref, wab_ref = next(it), next(it), next(it), next(it)
    convw_ref, alog_ref, dtb_ref, tri_ref, seg_ref = next(it), next(it), next(it), next(it), next(it)
    hist_ref = valid_ref = None
    if has_hist:
        hist_ref, valid_ref = next(it), next(it)
    q_ref, k_ref, v_ref = next(it), next(it), next(it)
    qg_ref, kg_ref, vg_ref, z_ref, gcb_ref, u_ref = (next(it) for _ in range(6))
    ubuf = next(it)

    t = pl.program_id(1)
    sub = PROJ_SUB
    rows = [slice(j * sub, (j + 1) * sub) for j in range(tm // sub)]
    lane = lax.broadcasted_iota(I32, (sub, LANES), 1)
    first_half = _mod_pow2(lane, HEAD_DIM) < (ROT_DIM // 2)

    @pl.when(t == 0)
    def _():
        ubuf[0:SUBLANES, :] = jnp.zeros((SUBLANES, CONV_DIM), F32)

    @pl.when(t > 0)
    def _():
        ubuf[0:SUBLANES, :] = ubuf[tm:tm + SUBLANES, :]

    dots = []
    for r in rows:
        xb = x_ref[r, :].astype(BF16)
        dots.append([jnp.dot(xb, w[...], preferred_element_type=F32)
                     for w in (wqkv_ref, wg_ref, wz_ref, wab_ref)])

    def l2n(s):
        return s * lax.rsqrt(jnp.sum(s * s, axis=1, keepdims=True) + L2_EPS)

    for r, (pq, u, z, ab) in zip(rows, dots):
        cosv, sinv = cos_ref[r, :], sin_ref[r, :]

        def rope(s):
            sw = jnp.where(first_half, pltpu.roll(s, LANES - ROT_DIM // 2, axis=1),
                           pltpu.roll(s, ROT_DIM // 2, axis=1))
            return s * cosv + sw * sinv

        for j in range(Q_COLS // LANES):
            q_ref[r, j * LANES:(j + 1) * LANES] = rope(pq[:, j * LANES:(j + 1) * LANES])
        k_ref[r, :] = rope(pq[:, Q_COLS:Q_COLS + KV_COLS])
        v_ref[r, :] = pq[:, Q_COLS + KV_COLS:Q_COLS + 2 * KV_COLS]
        z_ref[r, :] = z

        if has_hist:
            u = u + hist_ref[r, :]
        if full_u:
            u_ref[r, :] = u
        elif r.stop == tm:
            u_ref[...] = u[sub - SUBLANES:, :]
        base = SUBLANES + r.start
        ubuf[base:base + sub, :] = u
        acc = u * convw_ref[CONV_W - 1:CONV_W, :]
        for j in range(1, CONV_W):
            acc = acc + ubuf[base - j:base - j + sub, :] * convw_ref[CONV_W - 1 - j:CONV_W - j, :]
        c = _silu(acc)
        if has_hist:
            c = c * valid_ref[r, :]
        for h in range(GDN_HEADS):
            sl = slice(h * GDN_DK, (h + 1) * GDN_DK)
            qg_ref[r, sl] = l2n(c[:, sl]) * (GDN_DK ** -0.5)
            kg_ref[r, sl] = l2n(c[:, QK_COLS + h * GDN_DK:QK_COLS + (h + 1) * GDN_DK])
        vg_ref[r, :] = c[:, 2 * QK_COLS:]

        g = -jnp.exp(alog_ref[...]) * _softplus(ab + dtb_ref[...])
        beta = _sigmoid(ab)
        if has_hist:
            g = g * valid_ref[r, :]
            beta = beta * valid_ref[r, :]
        g = jnp.where(lane < GDN_HEADS, g, 0.0)
        gc = _dot_exact_lhs01(tri_ref[...], g)
        gl = _dot_exact_lhs01(seg_ref[...], g)
        gcb_ref[r, :] = jnp.where(lane < GDN_HEADS, gc,
                                  jnp.where(lane < 2 * GDN_HEADS, beta,
                                            jnp.where(lane < 3 * GDN_HEADS,
                                                      pltpu.roll(gl, 2 * GDN_HEADS, axis=1), 0.0)))


def _rope_tables(pos):
    half = ROT_DIM // 2
    inv_freq = ROPE_THETA ** (-jnp.arange(half, dtype=F32) * 2.0 / ROT_DIM)
    ang = pos.astype(F32)[:, None] * inv_freq[None, :]
    cos, sin = jnp.cos(ang), jnp.sin(ang)
    p = pos.shape[0]
    cpat = jnp.concatenate([cos, cos, jnp.ones((p, HEAD_DIM - ROT_DIM), F32)], axis=1)
    spat = jnp.concatenate([-sin, sin, jnp.zeros((p, HEAD_DIM - ROT_DIM), F32)], axis=1)
    return jnp.tile(cpat, (1, LANES // HEAD_DIM)), jnp.tile(spat, (1, LANES // HEAD_DIM))


def _segment_matrices(tm, seg_len):
    i = jnp.arange(tm)
    same = (i[:, None] // seg_len) == (i[None, :] // seg_len)
    tri = same & (i[None, :] <= i[:, None])
    return tri.astype(BF16), same.astype(BF16)


def _proj(x, pos, wts, seg_len, n_seq, hist=None, valid=None):
    n = x.shape[0]
    rows = n // n_seq
    tm = min(PROJ_TILE, rows)
    nt = rows // tm
    has_hist = hist is not None
    cos_t, sin_t = _rope_tables(pos)
    tri, seg = _segment_matrices(PROJ_SUB, seg_len)

    tok = lambda w: pl.BlockSpec((tm, w), lambda b, t: (b * nt + t, 0))
    const = lambda a: pl.BlockSpec(a.shape, lambda b, t: (0,) * a.ndim)
    in_arrays = [x, cos_t, sin_t, wts['wqkv'], wts['wg'], wts['wz'], wts['wab'],
                 wts['convw'], wts['alog'], wts['dtb'], tri, seg]
    in_specs = [tok(D_MODEL), pl.BlockSpec((tm, LANES), lambda b, t: (t, 0)),
                pl.BlockSpec((tm, LANES), lambda b, t: (t, 0))] + [const(a) for a in in_arrays[3:]]
    if has_hist:
        in_arrays += [hist, valid]
        in_specs += [tok(CONV_DIM), tok(1)]
    u_rows = n if has_hist else (n // tm) * SUBLANES
    u_block = tm if has_hist else SUBLANES
    out_shape = [jax.ShapeDtypeStruct((n, Q_COLS), F32), jax.ShapeDtypeStruct((n, KV_COLS), F32),
                 jax.ShapeDtypeStruct((n, KV_COLS), F32), jax.ShapeDtypeStruct((n, QK_COLS), F32),
                 jax.ShapeDtypeStruct((n, QK_COLS), F32), jax.ShapeDtypeStruct((n, Z_COLS), F32),
                 jax.ShapeDtypeStruct((n, Z_COLS), F32), jax.ShapeDtypeStruct((n, LANES), F32),
                 jax.ShapeDtypeStruct((u_rows, CONV_DIM), F32)]
    out_specs = [tok(Q_COLS), tok(KV_COLS), tok(KV_COLS), tok(QK_COLS), tok(QK_COLS), tok(Z_COLS),
                 tok(Z_COLS), tok(LANES),
                 pl.BlockSpec((u_block, CONV_DIM), lambda b, t: (b * nt + t, 0))]
    return pl.pallas_call(
        functools.partial(_proj_kernel, tm=tm, has_hist=has_hist, full_u=has_hist),
        out_shape=out_shape, grid=(n_seq, nt), in_specs=in_specs, out_specs=out_specs,
        scratch_shapes=[pltpu.VMEM((tm + SUBLANES, CONV_DIM), F32)],
        compiler_params=_cparams(("arbitrary", "arbitrary")),
        name="proj_hist" if has_hist else "proj",
    )(*in_arrays)


def _attn_blocks(qs, kcats, vcats, biases, sink, tq):
    lane = lax.broadcasted_iota(I32, (tq, LANES), 1)
    low = lane < HEAD_DIM
    n_slab = Q_COLS // LANES

    def stack(q):
        slabs = [q[:, j * LANES:(j + 1) * LANES] * (HEAD_DIM ** -0.5) for j in range(n_slab)]
        parts = ([jnp.where(low, s, 0.0) for s in slabs] + [jnp.where(low, 0.0, s) for s in slabs])
        return jnp.concatenate(parts, axis=0).astype(BF16)

    def unstack(o8):
        return [jnp.where(low, o8[j * tq:(j + 1) * tq, :], o8[(n_slab + j) * tq:(n_slab + j + 1) * tq, :])
                for j in range(n_slab)]

    ones = jnp.ones((2 * WINDOW, LANES), BF16)
    rows = ATTN_HEADS * tq
    q8s = _each(stack, qs)
    ss = _each(lambda q8, kc, b: _bdot_nt(q8, kc) + b, q8s, kcats, biases)
    ms = _each(lambda s: jnp.maximum(jnp.broadcast_to(jnp.max(s, axis=1, keepdims=True), (rows, LANES)),
                                     sink), ss)
    ps = _each(lambda s, m: jnp.exp(s - jnp.concatenate([m, m], axis=1)).astype(BF16), ss, ms)
    dens = _each(lambda p, m: jnp.dot(p, ones, preferred_element_type=F32) + jnp.exp(sink - m), ps, ms)
    o8s = _each(lambda p, vc, d: jnp.dot(p, vc, preferred_element_type=F32) / d, ps, vcats, dens)
    return _each(unstack, o8s)


def _attn_prompt_kernel(q_ref, kc_ref, vc_ref, kp_ref, vp_ref, bias0_ref, bias_ref, sink_ref, o_ref, *,
                        nblk):
    kall = jnp.concatenate([kp_ref[...], kc_ref[...]], axis=0).astype(BF16)
    vall = jnp.concatenate([vp_ref[...], vc_ref[...]], axis=0).astype(BF16)
    win = lambda a, j: a[j * WINDOW:(j + 2) * WINDOW, :]
    qs = [q_ref[j * WINDOW:(j + 1) * WINDOW, :] for j in range(nblk)]
    biases = [bias0_ref[0]] + [bias_ref[...]] * (nblk - 1)
    outs = _attn_blocks(qs, [win(kall, j) for j in range(nblk)], [win(vall, j) for j in range(nblk)],
                        biases, sink_ref[...], WINDOW)
    for j, slabs in enumerate(outs):
        for c, slab in enumerate(slabs):
            o_ref[j * WINDOW:(j + 1) * WINDOW, c * LANES:(c + 1) * LANES] = slab


def _attn_sample_kernel(q_ref, kc_ref, vc_ref, kp_ref, vp_ref, bias_ref, sink_ref, o_ref, *, nseq):
    tq = SAMPLE_SLOTS
    zpad = jnp.zeros((WINDOW - tq, LANES), F32)
    rows = lambda ref, j: ref[j * tq:(j + 1) * tq, :]
    cat = lambda pref, cref, j: jnp.concatenate([pref[j], rows(cref, j), zpad], axis=0).astype(BF16)
    outs = _attn_blocks([rows(q_ref, j) for j in range(nseq)],
                        [cat(kp_ref, kc_ref, j) for j in range(nseq)],
                        [cat(vp_ref, vc_ref, j) for j in range(nseq)],
                        [bias_ref[...]] * nseq, sink_ref[...], tq)
    for j, slabs in enumerate(outs):
        for c, slab in enumerate(slabs):
            o_ref[j * tq:(j + 1) * tq, c * LANES:(c + 1) * LANES] = slab


def _sink_rows(sinks, tq):
    return jnp.broadcast_to(jnp.repeat(sinks.astype(F32), tq)[:, None], (ATTN_HEADS * tq, LANES))


def _attn_bias(tq, q_off, k_lo, k_hi, has_prev):
    qi = (jnp.arange(ATTN_HEADS * tq, dtype=I32) % tq)[:, None]
    c = jnp.arange(2 * WINDOW, dtype=I32)[None, :]
    cj = c - WINDOW
    vis_prev = (c < WINDOW) & (c > qi - q_off) & has_prev
    vis_cur = (c >= WINDOW) & (cj <= qi) & (cj >= k_lo) & (cj <= k_hi)
    return jnp.where(vis_prev | vis_cur, 0.0, NEG_BIG).astype(F32)


def _attn_prompt(q, k, v, sinks, n_seq):
    n = q.shape[0]
    nb = n // n_seq // WINDOW
    nblk = min(ATTN_BLOCKS_PER_STEP, nb)
    steps = nb // nblk
    tq = nblk * WINDOW
    cur = lambda w: pl.BlockSpec((tq, w), lambda b, i: (b * steps + i, 0))
    prev = pl.BlockSpec((WINDOW, LANES), lambda b, i: (b * nb + jnp.maximum(i * nblk - 1, 0), 0))
    bias2 = jnp.stack([_attn_bias(WINDOW, 0, 0, WINDOW - 1, False),
                       _attn_bias(WINDOW, 0, 0, WINDOW - 1, True)])
    rows = ATTN_HEADS * WINDOW
    return pl.pallas_call(
        functools.partial(_attn_prompt_kernel, nblk=nblk),
        out_shape=jax.ShapeDtypeStruct((n, Q_COLS), F32), grid=(n_seq, steps),
        in_specs=[cur(Q_COLS), cur(LANES), cur(LANES), prev, prev,
                  pl.BlockSpec((1, rows, 2 * WINDOW), lambda b, i: (jnp.minimum(i, 1), 0, 0)),
                  pl.BlockSpec((rows, 2 * WINDOW), lambda b, i: (0, 0)),
                  pl.BlockSpec((rows, LANES), lambda b, i: (0, 0))],
        out_specs=cur(Q_COLS),
        compiler_params=_cparams(("arbitrary", "arbitrary")), name="attn_prompt",
    )(q, k, v, k, v, bias2, bias2[1], _sink_rows(sinks, WINDOW))


def _attn_sample(q, k, v, cache_k, cache_v, sinks, n_seq):
    tq = SAMPLE_SLOTS
    nseq = min(ATTN_SEQS_PER_STEP, n_seq)
    cur = lambda w: pl.BlockSpec((nseq * tq, w), lambda b: (b, 0))
    prev = pl.BlockSpec((nseq, WINDOW, LANES), lambda b: (b, 0, 0))
    bias = _attn_bias(tq, SAMPLE_FIRST, SAMPLE_FIRST, SAMPLE_FIRST + 3, True)
    return pl.pallas_call(
        functools.partial(_attn_sample_kernel, nseq=nseq),
        out_shape=jax.ShapeDtypeStruct((n_seq * tq, Q_COLS), F32), grid=(n_seq // nseq,),
        in_specs=[cur(Q_COLS), cur(LANES), cur(LANES), prev, prev,
                  pl.BlockSpec(bias.shape, lambda b: (0, 0)),
                  pl.BlockSpec((ATTN_HEADS * tq, LANES), lambda b: (0, 0))],
        out_specs=cur(Q_COLS),
        compiler_params=_cparams(("arbitrary",)), name="attn_sample",
    )(q, k, v, cache_k, cache_v, bias, _sink_rows(sinks, tq))


def _each(f, *lists):
    return [f(*args) for args in zip(*lists)]


def _unit_lower_inverse(ms, eye, same_base):
    c = ms[0].shape[0]
    ds = _each(lambda m: jnp.where(same_base, m, 0.0), ms)
    ls = _each(lambda m, d: m - d, ms, ds)
    xs = _each(lambda d: eye - d, ds)
    ps = ds
    for _ in range(int(math.log2(INV_BASE)) - 1):
        ps = _each(_bdot, ps, ps)
        xs = _each(lambda x, p: x + _bdot(x, p), xs, ps)
    nblk = c // INV_BASE
    if nblk == 1:
        return xs
    ns = _each(_bdot, xs, ls)
    ys = _each(lambda n: eye - n, ns)
    pws = ns
    for _ in range(int(math.log2(nblk)) - 1):
        pws = _each(_bdot, pws, pws)
        ys = _each(lambda y, p: y + _bdot(y, p), ys, pws)
    return _each(_bdot, ys, xs)


def _gdn_intra(qs, ks, vs, gcs, gls, betas, same_seq, low_incl, low_strict, eye, same_base):
    del same_seq
    e_gcs = _each(jnp.exp, gcs)

    def decay_of(gc):
        gc_row = jnp.sum(jnp.where(eye > 0, gc, 0.0), axis=0, keepdims=True)
        return jnp.where(low_incl, jnp.exp(jnp.where(low_incl, gc - gc_row, 0.0)), 0.0)

    decays = _each(decay_of, gcs)
    kbs = _each(lambda k, b: k * b, ks, betas)
    vbs = _each(lambda v, b: v * b, vs, betas)
    kks = _each(_bdot_nt, kbs, ks)
    ms = _each(lambda kk, d: jnp.where(low_strict, kk * d, 0.0), kks, decays)
    tmats = _unit_lower_inverse(ms, eye, same_base)
    us = _each(_bdot, tmats, vbs)
    ws = _each(lambda t, kb, e: _bdot(t, kb * e), tmats, kbs, e_gcs)
    qks = _each(_bdot_nt, qs, ks)
    attns = _each(lambda qk, d: jnp.where(low_incl, qk * d, 0.0), qks, decays)
    q_decs = _each(lambda q, e: q * e, qs, e_gcs)
    k_decs = _each(lambda k, gl, gc: k * jnp.exp(gl - gc), ks, gls, gcs)
    return us, ws, attns, q_decs, k_decs


def _chunk_masks(c, seq_len):
    i = lax.broadcasted_iota(I32, (c, c), 0)
    j = lax.broadcasted_iota(I32, (c, c), 1)
    same_seq = _div_pow2(i, seq_len) == _div_pow2(j, seq_len)
    low_incl = same_seq & (i >= j)
    low_strict = same_seq & (i > j)
    eye = (i == j).astype(F32)
    same_base = _div_pow2(i, INV_BASE) == _div_pow2(j, INV_BASE)
    return same_seq, low_incl, low_strict, eye, same_base


def _gated_rms(o, z, nw):
    o = o * lax.rsqrt(jnp.mean(o * o, axis=1, keepdims=True) + NORM_EPS) * nw
    return o * _silu(z)


def _gdn_prompt_kernel(qg_ref, kg_ref, vg_ref, z_ref, gcb_ref, nw_ref, o_ref, s_out_ref, s_scr):
    c = GDN_CHUNK
    n = pl.program_id(1)

    @pl.when(n == 0)
    def _():
        s_scr[...] = jnp.zeros_like(s_scr)

    masks = _chunk_masks(c, c)
    nw = nw_ref[...]
    chains = [(b, h) for b in range(qg_ref.shape[0]) for h in range(GDN_HEADS)]
    hs = lambda h: slice(h * GDN_DK, (h + 1) * GDN_DK)
    col = lambda off: [gcb_ref[b, :, off + h:off + h + 1] for b, h in chains]
    gcs, betas, gls = col(0), col(GDN_HEADS), col(2 * GDN_HEADS)
    qs = [qg_ref[b, :, hs(h)] for b, h in chains]
    ks = [kg_ref[b, :, hs(h)] for b, h in chains]
    vs = [vg_ref[b, :, hs(h)] for b, h in chains]
    us, ws, attns, q_decs, k_decs = _gdn_intra(qs, ks, vs, gcs, gls, betas, *masks)
    ss = [s_scr[b, h] for b, h in chains]
    wss = _each(_bdot, ws, ss)
    qss = _each(_bdot, q_decs, ss)
    v_news = _each(lambda u, x: u - x, us, wss)
    avs = _each(_bdot, attns, v_news)
    kvs = _each(_bdot_tn, k_decs, v_news)
    for (b, h), s, gl, qsv, av, kv in zip(chains, ss, gls, qss, avs, kvs):
        s_scr[b, h] = s * jnp.exp(gl[0:1, :]) + kv
        o_ref[b, :, hs(h)] = _gated_rms(qsv + av, z_ref[b, :, hs(h)], nw)

    @pl.when(n == pl.num_programs(1) - 1)
    def _():
        s_out_ref[...] = s_scr[...]


def _gdn_prompt(qg, kg, vg, z, gcb, norm_w, n_seq):
    n = qg.shape[0]
    s_len = n // n_seq
    nb = min(GDN_SEQ_PER_STEP, n_seq)
    v3 = lambda a: a.reshape(n_seq, s_len, a.shape[-1])
    tok = lambda w: pl.BlockSpec((nb, GDN_CHUNK, w), lambda b, i: (b, i, 0))
    o, s = pl.pallas_call(
        _gdn_prompt_kernel,
        out_shape=[jax.ShapeDtypeStruct((n_seq, s_len, Z_COLS), F32),
                   jax.ShapeDtypeStruct((n_seq, GDN_HEADS, GDN_DK, GDN_DV), F32)],
        grid=(n_seq // nb, s_len // GDN_CHUNK),
        in_specs=[tok(QK_COLS), tok(QK_COLS), tok(Z_COLS), tok(Z_COLS), tok(LANES),
                  pl.BlockSpec((1, GDN_DV), lambda b, i: (0, 0))],
        out_specs=[tok(Z_COLS),
                   pl.BlockSpec((nb, GDN_HEADS, GDN_DK, GDN_DV), lambda b, i: (b, 0, 0, 0))],
        scratch_shapes=[pltpu.VMEM((nb, GDN_HEADS, GDN_DK, GDN_DV), F32)],
        compiler_params=_cparams(("arbitrary", "arbitrary")), name="gdn_prompt",
    )(v3(qg), v3(kg), v3(vg), v3(z), v3(gcb), norm_w)
    return o.reshape(n, Z_COLS), s


def _gdn_sample_kernel(qg_ref, kg_ref, vg_ref, z_ref, gcb_ref, nw_ref, s_in_ref, o_ref, s_out_ref):
    c = GDN_CHUNK
    n_sub = c // SAMPLE_SLOTS
    masks = _chunk_masks(c, SAMPLE_SLOTS)
    heads = range(GDN_HEADS)
    hs = lambda h: slice(h * GDN_DK, (h + 1) * GDN_DK)
    rs = lambda s: slice(s * SAMPLE_SLOTS, (s + 1) * SAMPLE_SLOTS)
    col = lambda off: [gcb_ref[:, off + h:off + h + 1] for h in heads]
    gcs, betas, gls = col(0), col(GDN_HEADS), col(2 * GDN_HEADS)
    us, ws, attns, q_decs, k_decs = _gdn_intra([qg_ref[:, hs(h)] for h in heads],
                                               [kg_ref[:, hs(h)] for h in heads],
                                               [vg_ref[:, hs(h)] for h in heads], gcs, gls, betas, *masks)
    pairs = [(h, s) for h in heads for s in range(n_sub)]
    sts = [s_in_ref[s, h] for h, s in pairs]
    boths = [jnp.concatenate([ws[h][rs(s), :], q_decs[h][rs(s), :]], axis=0) for h, s in pairs]
    rr = _each(_bdot, boths, sts)
    gather = lambda h, part: jnp.concatenate(
        [rr[h * n_sub + s][part * SAMPLE_SLOTS:(part + 1) * SAMPLE_SLOTS, :] for s in range(n_sub)], axis=0)
    v_news = [us[h] - gather(h, 0) for h in heads]
    avs = _each(_bdot, attns, v_news)
    row = lax.broadcasted_iota(I32, (c, LANES), 0)
    seq_of_row = _div_pow2(row, SAMPLE_SLOTS)
    kds = [jnp.where(seq_of_row == s, k_decs[h], 0.0) for h, s in pairs]
    kvs = _each(_bdot_tn, kds, [v_news[h] for h, _ in pairs])
    egls = _each(jnp.exp, gls)
    for (h, s), st, kv in zip(pairs, sts, kvs):
        s_out_ref[s, h] = st * egls[h][s * SAMPLE_SLOTS:s * SAMPLE_SLOTS + 1, :] + kv
    nw = nw_ref[...]
    for h in heads:
        o_ref[:, hs(h)] = _gated_rms(gather(h, 1) + avs[h], z_ref[:, hs(h)], nw)


def _gdn_sample(qg, kg, vg, z, gcb, norm_w, state):
    n = qg.shape[0]
    n_sub = GDN_CHUNK // SAMPLE_SLOTS
    tok = lambda w: pl.BlockSpec((GDN_CHUNK, w), lambda i: (i, 0))
    st = pl.BlockSpec((n_sub, GDN_HEADS, GDN_DK, GDN_DV), lambda i: (i, 0, 0, 0))
    return pl.pallas_call(
        _gdn_sample_kernel,
        out_shape=[jax.ShapeDtypeStruct((n, Z_COLS), F32),
                   jax.ShapeDtypeStruct(state.shape, F32)],
        grid=(n // GDN_CHUNK,),
        in_specs=[tok(QK_COLS), tok(QK_COLS), tok(Z_COLS), tok(Z_COLS), tok(LANES),
                  pl.BlockSpec((1, GDN_DV), lambda i: (0, 0)), st],
        out_specs=[tok(Z_COLS), st],
        compiler_params=_cparams(("arbitrary",)), name="gdn_sample",
    )(qg, kg, vg, z, gcb, norm_w, state)


def _post_kernel(a_ref, g_ref, x_ref, wo_ref, ln_g_ref, ln_b_ref, wr_ref, x1_ref, route_ref, *, tm):
    sub = PROJ_SUB
    rows = [slice(j * sub, (j + 1) * sub) for j in range(tm // sub)]
    d = lambda a, b: jnp.dot(a, b, preferred_element_type=F32)
    mixes = [d(a_ref[r, :].astype(BF16), wo_ref[0:Q_COLS, :]) + d(g_ref[r, :].astype(BF16), wo_ref[Q_COLS:, :])
             for r in rows]
    x1s = [_layer_norm(DEEPNORM_ALPHA * x_ref[r, :] + mix, ln_g_ref[...], ln_b_ref[...])
           for r, mix in zip(rows, mixes)]
    for r, x1 in zip(rows, x1s):
        x1_ref[r, :] = x1
    wh, wm = wr_ref[0], wr_ref[1]
    lgs = []
    for x1 in x1s:
        xh = x1.astype(BF16)
        xm = (x1 - xh.astype(F32)).astype(BF16)
        lgs.append(d(xh, wh) + d(xh, wm) + d(xm, wh))
    for r, lg in zip(rows, lgs):
        route_ref[r, :] = _route(lg)


def _route(lg):
    lane = lax.broadcasted_iota(I32, lg.shape, 1)
    lane_f = lane.astype(F32)
    big = float(LANES)

    def first_max(vals, mask):
        v = jnp.where(mask, vals, NEG_BIG)
        mx = jnp.max(v, axis=1, keepdims=True)
        idx = jnp.min(jnp.where(mask & (v == mx), lane_f, big), axis=1, keepdims=True)
        return mx, idx

    gmask = lane < N_GROUPS
    gmax, gidx = first_max(lg, gmask)
    gden = jnp.sum(jnp.where(gmask, jnp.exp(lg - gmax), 0.0), axis=1, keepdims=True)
    g_top_p = 1.0 / gden
    e_lane = lane - N_GROUPS
    e_group = _div_pow2(jnp.maximum(e_lane, 0), EXPERTS_PER_GROUP).astype(F32)
    emask = (e_lane >= 0) & (e_lane < N_EXPERTS) & (e_group == gidx)
    m1, i1 = first_max(lg, emask)
    eden = jnp.sum(jnp.where(emask, jnp.exp(lg - m1), 0.0), axis=1, keepdims=True)
    m2, i2 = first_max(lg, emask & (lane_f != i1))
    p1 = 1.0 / eden
    p2 = jnp.exp(m2 - m1) / eden
    tot = p1 + p2
    gate1 = g_top_p * (p1 / tot)
    gate2 = g_top_p * (p2 / tot)
    return jnp.where(lane == 0, gate1,
                     jnp.where(lane == 1, gate2,
                               jnp.where(lane == 2, i1 - N_GROUPS,
                                         jnp.where(lane == 3, i2 - N_GROUPS, 0.0))))


def _post(attn_o, gdn_o, x, wts):
    n = x.shape[0]
    tm = min(PROJ_TILE, n)
    tok = lambda w: pl.BlockSpec((tm, w), lambda i: (i, 0))
    const = lambda a: pl.BlockSpec(a.shape, lambda i: (0,) * a.ndim)
    consts = [wts['wo'], wts['ln1_g'], wts['ln1_b'], wts['wr']]
    return pl.pallas_call(
        functools.partial(_post_kernel, tm=tm),
        out_shape=[jax.ShapeDtypeStruct((n, D_MODEL), F32), jax.ShapeDtypeStruct((n, LANES), F32)],
        grid=(n // tm,),
        in_specs=[tok(Q_COLS), tok(Z_COLS), tok(D_MODEL)] + [const(a) for a in consts],
        out_specs=[tok(D_MODEL), tok(LANES)],
        compiler_params=_cparams(("arbitrary",)), name="post_%d" % (n // tm),
    )(attn_o, gdn_o, x, *consts)


def _slab_loop(n, body):
    lax.fori_loop(0, n, lambda j, c: (body(j), c)[1], 0)


def _dispatch_kernel(dst_ref, nslab_ref, ztab_ref, zinfo_ref, slot_ref, gate_ref, *rest,
                     group_tiles, max_tiles):
    x_refs = rest[:len(group_tiles)]
    xs_ref, pbuf, sem, zbuf, zsem = rest[len(group_tiles):]
    n_tiles = sum(group_tiles)
    g = pl.program_id(0)
    cur = lax.rem(g, 2)

    def slab_copy(tile, buf_slot, j):
        d = pl.multiple_of(dst_ref[tile * PERM_SLABS + j], SUBLANES)
        src = pbuf.at[buf_slot, pl.ds(pl.multiple_of(j * SUBLANES, SUBLANES), SUBLANES), :]
        return pltpu.make_async_copy(src, xs_ref.at[pl.ds(d, SUBLANES), :], sem.at[buf_slot])

    def tail_copy(k):
        d = pl.multiple_of(ztab_ref[k], SUBLANES)
        return pltpu.make_async_copy(zbuf.at[pl.ds(0, SUBLANES), :], xs_ref.at[pl.ds(d, SUBLANES), :], zsem)

    def tile_copy(t):
        d = pl.multiple_of(t * ROW_TILE, ROW_TILE)
        return pltpu.make_async_copy(zbuf, xs_ref.at[pl.ds(d, ROW_TILE), :], zsem)

    @pl.when(g == 0)
    def _():
        zbuf[...] = jnp.zeros_like(zbuf)
        _slab_loop(zinfo_ref[0], lambda k: tail_copy(k).start())
        lax.fori_loop(zinfo_ref[1], max_tiles, lambda t, c: (tile_copy(t).start(), c)[1], 0)
        _slab_loop(zinfo_ref[0], lambda k: tail_copy(k).wait())
        lax.fori_loop(zinfo_ref[1], max_tiles, lambda t, c: (tile_copy(t).wait(), c)[1], 0)

    x = x_refs[-1][...]
    bound = n_tiles
    for x_ref, nt in zip(x_refs[-2::-1], group_tiles[:0:-1]):
        bound -= nt
        x = jnp.where(g < bound, x_ref[...], x)

    r = lax.broadcasted_iota(I32, (PERM_ROWS, TOK_TILE), 0)
    sl = slot_ref[0]
    hit0, hit1 = r == sl[0:1, :], r == sl[1:2, :]
    onehot = jnp.where(hit0 | hit1, 1.0, 0.0).astype(BF16)
    gt = gate_ref[0]
    gcol = jnp.sum(jnp.where(hit0, gt[0:1, :], 0.0) + jnp.where(hit1, gt[1:2, :], 0.0),
                   axis=1, keepdims=True)
    pbuf[cur, :, 0:D_MODEL] = jnp.dot(onehot, x.astype(BF16), preferred_element_type=F32)
    pbuf[cur, :, D_MODEL:] = jnp.broadcast_to(gcol, (PERM_ROWS, LANES))

    @pl.when(g > 0)
    def _():
        _slab_loop(nslab_ref[g - 1], lambda j: slab_copy(g - 1, 1 - cur, j).wait())

    _slab_loop(nslab_ref[g], lambda j: slab_copy(g, cur, j).start())

    @pl.when(g == n_tiles - 1)
    def _():
        _slab_loop(nslab_ref[g], lambda j: slab_copy(g, cur, j).wait())


def _dispatch(plan, x1s, max_tiles):
    group_tiles = tuple(x1.shape[0] // TOK_TILE for x1 in x1s)
    n_tiles = sum(group_tiles)
    tile = lambda i, d, ns, zt, zi: (i, 0, 0)
    in_specs = [pl.BlockSpec((1, TOP_K, TOK_TILE), tile), pl.BlockSpec((1, TOP_K, TOK_TILE), tile)]
    base = 0
    for nt in group_tiles:
        in_specs.append(pl.BlockSpec(
            (TOK_TILE, D_MODEL),
            lambda i, d, ns, zt, zi, base=base, nt=nt: (jnp.clip(i - base, 0, nt - 1), 0)))
        base += nt
    return pl.pallas_call(
        functools.partial(_dispatch_kernel, group_tiles=group_tiles, max_tiles=max_tiles),
        out_shape=jax.ShapeDtypeStruct((max_tiles * ROW_TILE, XS_WORDS), F32),
        grid_spec=pltpu.PrefetchScalarGridSpec(
            num_scalar_prefetch=4, grid=(n_tiles,), in_specs=in_specs,
            out_specs=pl.BlockSpec(memory_space=pl.ANY),
            scratch_shapes=[pltpu.VMEM((2, PERM_ROWS, XS_WORDS), F32), pltpu.SemaphoreType.DMA((2,)),
                            pltpu.VMEM((ROW_TILE, XS_WORDS), F32), pltpu.SemaphoreType.DMA(())]),
        compiler_params=_cparams(("arbitrary",)), name="moe_dispatch",
    )(plan['slab_dst'], plan['nslab'], plan['ztab'], plan['zinfo'], plan['slot_rows'], plan['gate_rows'],
      *x1s)


def _expert_kernel(te_ref, nu_ref, xs_ref, wg_ref, wu_ref, wd_ref, ye_ref):
    del te_ref

    @pl.when(pl.program_id(0) < nu_ref[0])
    def _():
        sub = PROJ_SUB
        rows = [slice(j * sub, (j + 1) * sub) for j in range(ROW_TILE // sub)]
        d = lambda a, b: jnp.dot(a, b, preferred_element_type=F32)
        wg, wu, wd = wg_ref[0].astype(BF16), wu_ref[0].astype(BF16), wd_ref[0].astype(BF16)
        xs = [xs_ref[r, 0:D_MODEL].astype(BF16) for r in rows]
        hgs = [d(x, wg) for x in xs]
        hus = [d(x, wu) for x in xs]
        hhs = [(_silu(hg) * hu).astype(BF16) for hg, hu in zip(hgs, hus)]
        ys = [d(hh, wd) for hh in hhs]
        for r, y in zip(rows, ys):
            gate = xs_ref[r, D_MODEL:]
            ye_ref[r, :] = y * jnp.concatenate([gate] * (D_MODEL // LANES), axis=1)

    @pl.when(pl.program_id(0) >= nu_ref[0])
    def _():
        ye_ref[...] = jnp.zeros_like(ye_ref)


def _experts(tile_expert, n_used, xs, w_gate, w_up, w_down):
    n_tiles = xs.shape[0] // ROW_TILE
    row_in = lambda i, te, nu: (jnp.minimum(i, nu[0] - 1), 0)
    row = lambda i, te, nu: (i, 0)
    wsel = lambda i, te, nu: (te[i], 0, 0)
    return pl.pallas_call(
        _expert_kernel,
        out_shape=jax.ShapeDtypeStruct((xs.shape[0], D_MODEL), F32),
        grid_spec=pltpu.PrefetchScalarGridSpec(
            num_scalar_prefetch=2, grid=(n_tiles,),
            in_specs=[pl.BlockSpec((ROW_TILE, XS_WORDS), row_in),
                      pl.BlockSpec((1, D_MODEL, EXPERT_FF), wsel),
                      pl.BlockSpec((1, D_MODEL, EXPERT_FF), wsel),
                      pl.BlockSpec((1, EXPERT_FF, D_MODEL), wsel)],
            out_specs=pl.BlockSpec((ROW_TILE, D_MODEL), row)),
        compiler_params=_cparams(("arbitrary",)), name="moe_experts",
    )(tile_expert, n_used, xs, w_gate, w_up, w_down)


def _combine_kernel(dst_ref, nslab_ref, x1_ref, slot_ref, ye_ref, ln_g_ref, ln_b_ref, y_ref,
                    buf, sem, *, tile_base, n_tiles):
    i = pl.program_id(0)
    g = tile_base + i
    cur = lax.rem(i, 2)

    def slab_copy(tile, buf_slot, j):
        d = pl.multiple_of(dst_ref[tile * PERM_SLABS + j], SUBLANES)
        dst = buf.at[buf_slot, pl.ds(pl.multiple_of(j * SUBLANES, SUBLANES), SUBLANES), :]
        return pltpu.make_async_copy(ye_ref.at[pl.ds(d, SUBLANES), :], dst, sem.at[buf_slot])

    @pl.when(i == 0)
    def _():
        buf[...] = jnp.zeros_like(buf)
        _slab_loop(nslab_ref[g], lambda j: slab_copy(g, cur, j).start())

    @pl.when(i + 1 < n_tiles)
    def _():
        _slab_loop(nslab_ref[g + 1], lambda j: slab_copy(g + 1, 1 - cur, j).start())

    _slab_loop(nslab_ref[g], lambda j: slab_copy(g, cur, j).wait())

    col = lax.broadcasted_iota(I32, (TOK_TILE, PERM_ROWS), 1)
    sc = slot_ref[...]
    pick = jnp.where((col == sc[:, 0:1]) | (col == sc[:, 1:2]), 1.0, 0.0).astype(BF16)
    rows = buf[cur]
    r_hi = rows.astype(BF16)
    r_lo = (rows - r_hi.astype(F32)).astype(BF16)
    moe = (jnp.dot(pick, r_hi, preferred_element_type=F32)
           + jnp.dot(pick, r_lo, preferred_element_type=F32))
    y_ref[...] = _layer_norm(DEEPNORM_ALPHA * x1_ref[...] + moe, ln_g_ref[...], ln_b_ref[...])


def _combine(plan, tile_base, x1, ye, ln_g, ln_b):
    n = x1.shape[0]
    n_tiles = n // TOK_TILE
    tok = lambda w: pl.BlockSpec((TOK_TILE, w), lambda i, d, ns: (i, 0))
    const = lambda a: pl.BlockSpec(a.shape, lambda i, d, ns: (0,) * a.ndim)
    return pl.pallas_call(
        functools.partial(_combine_kernel, tile_base=tile_base, n_tiles=n_tiles),
        out_shape=jax.ShapeDtypeStruct((n, D_MODEL), F32),
        grid_spec=pltpu.PrefetchScalarGridSpec(
            num_scalar_prefetch=2, grid=(n_tiles,),
            in_specs=[tok(D_MODEL),
                      pl.BlockSpec((TOK_TILE, TOP_K), lambda i, d, ns: (tile_base + i, 0)),
                      pl.BlockSpec(memory_space=pl.ANY), const(ln_g), const(ln_b)],
            out_specs=tok(D_MODEL),
            scratch_shapes=[pltpu.VMEM((2, PERM_ROWS, D_MODEL), F32), pltpu.SemaphoreType.DMA((2,))]),
        compiler_params=_cparams(("arbitrary",)), name="moe_combine_%d" % tile_base,
    )(plan['slab_dst'], plan['nslab'], x1, plan['slot_cols'], ye, ln_g, ln_b)


def _routing_plan(ids, gates):
    nt = ids.shape[0] // TOK_TILE
    ex = jnp.arange(N_EXPERTS, dtype=I32)
    flat = ids.reshape(nt, TOP_K * TOK_TILE)
    onehot = (flat[:, :, None] == ex).astype(I32)
    p = jnp.arange(TOP_K * TOK_TILE, dtype=I32)
    tri = (p[None, :] <= p[:, None]).astype(BF16)
    csum = jnp.einsum('pq,tqe->tpe', tri, onehot.astype(BF16), preferred_element_type=F32).astype(I32)
    rank = jnp.sum(onehot * (csum - 1), axis=2)
    cnt = csum[:, -1, :]
    cpad = (cnt + SUBLANES - 1) // SUBLANES * SUBLANES
    seg_end = jnp.cumsum(cpad, axis=1)
    seg_off = seg_end - cpad
    slot = jnp.sum(onehot * seg_off[:, None, :], axis=2) + rank
    run_end = jnp.cumsum(cpad, axis=0)
    ntiles_e = (run_end[-1] + ROW_TILE - 1) // ROW_TILE
    tile_end = jnp.cumsum(ntiles_e)
    dst_run = ((tile_end - ntiles_e) * ROW_TILE)[None, :] + run_end - cpad
    j8 = jnp.arange(PERM_SLABS, dtype=I32) * SUBLANES
    e_of = jnp.minimum(jnp.sum((j8[None, :, None] >= seg_end[:, None, :]).astype(I32), axis=2),
                       N_EXPERTS - 1)
    sel = (e_of[:, :, None] == ex).astype(I32)
    slab_dst = jnp.sum(sel * (dst_run - seg_off)[:, None, :], axis=2) + j8[None, :]
    n_used = tile_end[-1]
    max_tiles = _max_row_tiles(ids.shape[0])
    t = jnp.arange(max_tiles, dtype=I32)
    te = jnp.sum((t[:, None] >= tile_end[None, :]).astype(I32), axis=1)
    te_last = jnp.sum((n_used - 1 >= tile_end).astype(I32))
    pair_slots = slot.reshape(nt, TOK_TILE, TOP_K)
    row_start = (tile_end - ntiles_e) * ROW_TILE
    tail_cnt = (ntiles_e * ROW_TILE - run_end[-1]) // SUBLANES
    tail_end = jnp.cumsum(tail_cnt)
    k = jnp.arange(ZERO_TABLE, dtype=I32)
    e_k = jnp.minimum(jnp.sum((k[:, None] >= tail_end[None, :]).astype(I32), axis=1), N_EXPERTS - 1)
    base_k = jnp.sum((e_k[:, None] == ex).astype(I32)
                     * (row_start + run_end[-1] - SUBLANES * (tail_end - tail_cnt))[None, :], axis=1)
    return dict(
        slab_dst=slab_dst.reshape(-1).astype(I32), nslab=(seg_end[:, -1] // SUBLANES).astype(I32),
        ztab=(base_k + SUBLANES * k).astype(I32), zinfo=jnp.stack([tail_end[-1], n_used]).astype(I32),
        slot_rows=jnp.swapaxes(pair_slots, 1, 2).astype(I32),
        slot_cols=pair_slots.reshape(nt * TOK_TILE, TOP_K).astype(I32),
        gate_rows=jnp.swapaxes(gates.reshape(nt, TOK_TILE, TOP_K), 1, 2).astype(F32),
        tile_expert=jnp.where(t < n_used, jnp.minimum(te, N_EXPERTS - 1), te_last).astype(I32),
        n_used=n_used.reshape(1).astype(I32))


def _max_row_tiles(n_tokens):
    rows = TOP_K * n_tokens + (n_tokens // TOK_TILE) * N_EXPERTS * (SUBLANES - 1)
    return (rows + ROW_TILE - 1) // ROW_TILE + N_EXPERTS


def _moe(x1s, routes, wts):
    ids = jnp.concatenate([r[:, 2:2 + TOP_K] for r in routes], axis=0).astype(I32)
    gates = jnp.concatenate([r[:, 0:TOP_K] for r in routes], axis=0)
    plan = _routing_plan(ids, gates)
    max_tiles = _max_row_tiles(ids.shape[0])
    bases = [0]
    for x1 in x1s[:-1]:
        bases.append(bases[-1] + x1.shape[0] // TOK_TILE)
    xs = _dispatch(plan, x1s, max_tiles)
    ye = _experts(plan['tile_expert'], plan['n_used'], xs, wts['w_gate'], wts['w_up'], wts['w_down'])
    return [_combine(plan, base, x1, ye, wts['ln2_g'], wts['ln2_b']) for base, x1 in zip(bases, x1s)]


def _prep_weights(w_in, w_out, conv_w, a_log, dt_bias, gdn_norm_w, ln1_g, ln1_b, w_router_group,
                  w_router_expert, w_gate, w_up, w_down, ln2_g, ln2_b):
    o1 = Q_COLS + 2 * KV_COLS
    o2 = o1 + CONV_DIM
    o3 = o2 + Z_COLS
    pad_row = lambda v: jnp.pad(v.astype(F32), (0, LANES - v.shape[0]))[None, :]
    wab = jnp.pad(w_in[:, o3:], ((0, 0), (0, LANES - 2 * GDN_HEADS)))
    wr = jnp.pad(jnp.concatenate([w_router_group, w_router_expert], axis=1),
                 ((0, 0), (0, LANES - N_GROUPS - N_EXPERTS)))
    wr_hi = wr.astype(BF16)
    wr_mid = (wr - wr_hi.astype(F32)).astype(BF16)
    group = ATTN_HEADS // ATTN_KV_HEADS
    head_order = jnp.array([h for j in range(group) for h in (j, j + group)], I32)
    col_order = (head_order[:, None] * HEAD_DIM + jnp.arange(HEAD_DIM, dtype=I32)[None, :]).reshape(-1)
    wqkv = jnp.concatenate([w_in[:, :Q_COLS][:, col_order], w_in[:, Q_COLS:o1]], axis=1)
    wo = jnp.concatenate([w_out[:Q_COLS][col_order], w_out[Q_COLS:]], axis=0)
    return dict(
        wqkv=wqkv.astype(BF16), wg=w_in[:, o1:o2].astype(BF16), wz=w_in[:, o2:o3].astype(BF16),
        wab=wab.astype(BF16), convw=conv_w.astype(F32), alog=pad_row(a_log), dtb=pad_row(dt_bias),
        norm_w=gdn_norm_w.astype(F32)[None, :], wo=wo.astype(BF16),
        ln1_g=ln1_g[None, :], ln1_b=ln1_b[None, :], wr=jnp.stack([wr_hi, wr_mid]),
        w_gate=w_gate, w_up=w_up, w_down=w_down, ln2_g=ln2_g[None, :], ln2_b=ln2_b[None, :])


def _layer(x_prompt, x_sample, cache_k, cache_v, state_gdn, state_conv, wts):
    bp, sp, _ = x_prompt.shape
    bs, ts, _ = x_sample.shape
    n_p = bp * sp

    xp = x_prompt.reshape(n_p, D_MODEL)
    (q, k, v, qg, kg, vg, z, gcb, utail) = _proj(xp, jnp.arange(sp, dtype=I32), wts, GDN_CHUNK, bp)
    attn_p = _attn_prompt(q, k, v, wts['sinks'], bp)
    gdn_p, s_p = _gdn_prompt(qg, kg, vg, z, gcb, wts['norm_w'], bp)
    new_k_p = k.reshape(bp, sp, ATTN_KV_HEADS, HEAD_DIM)[:, sp - WINDOW:]
    new_v_p = v.reshape(bp, sp, ATTN_KV_HEADS, HEAD_DIM)[:, sp - WINDOW:]
    tiles_per_seq = sp // min(PROJ_TILE, sp)
    conv_p = utail.reshape(bp, tiles_per_seq, SUBLANES, CONV_DIM)[:, -1, SUBLANES - (CONV_W - 1):]

    lo, hi = SAMPLE_FIRST, SAMPLE_FIRST + ts
    xs_rows = jnp.pad(x_sample, ((0, 0), (lo, SAMPLE_SLOTS - hi), (0, 0))).reshape(bs * SAMPLE_SLOTS, D_MODEL)
    hist = jnp.pad(state_conv, ((0, 0), (0, SAMPLE_SLOTS - lo), (0, 0))).reshape(bs * SAMPLE_SLOTS, CONV_DIM)
    slot = jnp.arange(SAMPLE_SLOTS, dtype=I32)
    valid = jnp.tile(((slot >= lo) & (slot < hi)).astype(F32), bs)[:, None]
    pos_s = jnp.tile(PAST_LEN + slot - lo, bs)
    (q, k, v, qg, kg, vg, z, gcb, u_s) = _proj(xs_rows, pos_s, wts, SAMPLE_SLOTS, 1, hist, valid)
    ck = cache_k.reshape(bs, WINDOW, KV_COLS)
    cv = cache_v.reshape(bs, WINDOW, KV_COLS)
    attn_s = _attn_sample(q, k, v, ck, cv, wts['sinks'], bs)
    gdn_s, s_s = _gdn_sample(qg, kg, vg, z, gcb, wts['norm_w'], state_gdn)
    real = lambda a: a.reshape(bs, SAMPLE_SLOTS, -1)[:, lo:hi]
    k_new, v_new = real(k), real(v)
    new_k_s = jnp.concatenate([ck, k_new], axis=1)[:, -WINDOW:].reshape(bs, WINDOW, ATTN_KV_HEADS, HEAD_DIM)
    new_v_s = jnp.concatenate([cv, v_new], axis=1)[:, -WINDOW:].reshape(bs, WINDOW, ATTN_KV_HEADS, HEAD_DIM)
    conv_s = u_s.reshape(bs, SAMPLE_SLOTS, CONV_DIM)[:, hi - (CONV_W - 1):hi]

    x1_p, route_p = _post(attn_p, gdn_p, xp, wts)
    x1_s, route_s = _post(real(attn_s).reshape(bs * ts, Q_COLS), real(gdn_s).reshape(bs * ts, Z_COLS),
                          x_sample.reshape(bs * ts, D_MODEL), wts)
    y_p, y_s = _moe([x1_p, x1_s], [route_p, route_s], wts)
    return (y_p.reshape(bp, sp, D_MODEL), y_s.reshape(bs, ts, D_MODEL), new_k_p, new_v_p, s_p, conv_p,
            new_k_s, new_v_s, s_s, conv_s)


def kernel(x_prompt, x_sample, cache_attn_k, cache_attn_v, state_gdn, state_conv, w_in, w_out,
           attn_sinks, conv_w, a_log, dt_bias, gdn_norm_w, ln1_g, ln1_b, w_router_group,
           w_router_expert, w_gate, w_up, w_down, ln2_g, ln2_b):
    assert w_in.shape[0] == DEPTH
    l = 0
    wts = _prep_weights(w_in[l], w_out[l], conv_w[l], a_log[l], dt_bias[l], gdn_norm_w[l], ln1_g[l],
                        ln1_b[l], w_router_group[l], w_router_expert[l], w_gate[l], w_up[l],
                        w_down[l], ln2_g[l], ln2_b[l])
    wts['sinks'] = attn_sinks[l]
    outs = _layer(x_prompt, x_sample, cache_attn_k[l], cache_attn_v[l], state_gdn[l], state_conv[l], wts)
    (y_p, y_s, k_p, v_p, s_p, c_p, k_s, v_s, s_s, c_s) = outs
    add = lambda a: a[None]
    return (y_p, y_s, add(k_p), add(v_p), add(s_p), add(c_p), add(k_s), add(v_s), add(s_s), add(c_s))
```

```python
import functools
import math

import jax
import jax.numpy as jnp
from jax import lax
from jax.experimental import pallas as pl
from jax.experimental.pallas import tpu as pltpu

F32 = jnp.float32
BF16 = jnp.bfloat16
I32 = jnp.int32

D_MODEL = 1024
ATTN_HEADS = 8
ATTN_KV_HEADS = 2
HEAD_DIM = 64
WINDOW = 128
ROT_DIM = HEAD_DIM // 4
ROPE_THETA = 500000.0
GDN_HEADS = 4
GDN_DK = 128
GDN_DV = 128
CONV_W = 4
QK_COLS = GDN_HEADS * GDN_DK
CONV_DIM = 2 * QK_COLS + GDN_HEADS * GDN_DV
Z_COLS = GDN_HEADS * GDN_DV
Q_COLS = ATTN_HEADS * HEAD_DIM
KV_COLS = ATTN_KV_HEADS * HEAD_DIM
N_GROUPS = 4
EXPERTS_PER_GROUP = 8
N_EXPERTS = N_GROUPS * EXPERTS_PER_GROUP
TOP_K = 2
EXPERT_FF = 256
NORM_EPS = 1e-5
L2_EPS = 1e-6
DEPTH = 1
DEEPNORM_ALPHA = (2 * DEPTH) ** 0.25
PAST_LEN = 8192

LANES = 128
SUBLANES = 8
TOK_TILE = 256
PROJ_TILE = 512
PROJ_SUB = 128
GDN_CHUNK = 128
GDN_SEQ_PER_STEP = 4
ATTN_BLOCKS_PER_STEP = 4
ATTN_SEQS_PER_STEP = 8
INV_BASE = 16
SAMPLE_SLOTS = 8
SAMPLE_FIRST = CONV_W - 1
ROW_TILE = 512
EXPERT_SUB = 256
SLAB_UNROLL = 4
PERM_ROWS = TOP_K * TOK_TILE + N_EXPERTS * SUBLANES
PERM_SLABS = PERM_ROWS // SUBLANES
XS_WORDS = D_MODEL + LANES
ZERO_TABLE = N_EXPERTS * (ROW_TILE // SUBLANES)
VMEM_LIMIT = 48 * 1024 * 1024
NEG_BIG = -1e30


def _cparams(sem):
    return pltpu.CompilerParams(dimension_semantics=sem, vmem_limit_bytes=VMEM_LIMIT)


def _bdot(a, b):
    return jnp.dot(a.astype(BF16), b.astype(BF16), preferred_element_type=F32)


def _bdot_nt(a, b):
    return lax.dot_general(a.astype(BF16), b.astype(BF16), (((1,), (1,)), ((), ())),
                           preferred_element_type=F32)


def _bdot_tn(a, b):
    return lax.dot_general(a.astype(BF16), b.astype(BF16), (((0,), (0,)), ((), ())),
                           preferred_element_type=F32)


def _div_pow2(x, n):
    return jnp.right_shift(x, int(math.log2(n)))


def _mod_pow2(x, n):
    return jnp.bitwise_and(x, n - 1)


def _split3(x):
    hi = x.astype(BF16)
    r = x - hi.astype(F32)
    mid = r.astype(BF16)
    lo = (r - mid.astype(F32)).astype(BF16)
    return hi, mid, lo


def _dot_exact_lhs01(m01, x):
    hi, mid, lo = _split3(x)
    d = lambda t: jnp.dot(m01, t, preferred_element_type=F32)
    return d(hi) + d(mid) + d(lo)


def _sigmoid(x):
    return 1.0 / (1.0 + jnp.exp(-x))


def _silu(x):
    return x * _sigmoid(x)


def _softplus(x):
    return jnp.maximum(x, 0.0) + jnp.log1p(jnp.exp(-jnp.abs(x)))


def _layer_norm(h, g, b):
    mu = jnp.mean(h, axis=-1, keepdims=True)
    d = h - mu
    var = jnp.mean(d * d, axis=-1, keepdims=True)
    return d * lax.rsqrt(var + NORM_EPS) * g + b


def _proj_kernel(*refs, tm, has_hist, full_u):
    it = iter(refs)
    x_ref, cos_ref, sin_ref = next(it), next(it), next(it)
    wqkv_ref, wg_ref, wz_ref, wab_ref = next(it), next(it), next(it), next(it)
    convw_ref, alog_ref, dtb_ref, tri_ref, seg_ref = next(it), next(it), next(it), next(it), next(it)
    hist_ref = valid_ref = None
    if has_hist:
        hist_ref, valid_ref = next(it), next(it)
    q_ref, k_ref, v_ref = next(it), next(it), next(it)
    qg_ref, kg_ref, vg_ref, z_ref, gcb_ref, u_ref = (next(it) for _ in range(6))
    ubuf = next(it)

    t = pl.program_id(1)
    sub = PROJ_SUB
    rows = [slice(j * sub, (j + 1) * sub) for j in range(tm // sub)]
    lane = lax.broadcasted_iota(I32, (sub, LANES), 1)
    first_half = _mod_pow2(lane, HEAD_DIM) < (ROT_DIM // 2)

    @pl.when(t == 0)
    def _():
        ubuf[0:SUBLANES, :] = jnp.zeros((SUBLANES, CONV_DIM), F32)

    @pl.when(t > 0)
    def _():
        ubuf[0:SUBLANES, :] = ubuf[tm:tm + SUBLANES, :]

    dots = []
    for r in rows:
        xb = x_ref[r, :].astype(BF16)
        dots.append([jnp.dot(xb, w[...], preferred_element_type=F32)
                     for w in (wqkv_ref, wg_ref, wz_ref, wab_ref)])

    def l2n(s):
        return s * lax.rsqrt(jnp.sum(s * s, axis=1, keepdims=True) + L2_EPS)

    for r, (pq, u, z, ab) in zip(rows, dots):
        cosv, sinv = cos_ref[r, :], sin_ref[r, :]

        def rope(s):
            sw = jnp.where(first_half, pltpu.roll(s, LANES - ROT_DIM // 2, axis=1),
                           pltpu.roll(s, ROT_DIM // 2, axis=1))
            return s * cosv + sw * sinv

        for j in range(Q_COLS // LANES):
            q_ref[r, j * LANES:(j + 1) * LANES] = rope(pq[:, j * LANES:(j + 1) * LANES])
        k_ref[r, :] = rope(pq[:, Q_COLS:Q_COLS + KV_COLS])
        v_ref[r, :] = pq[:, Q_COLS + KV_COLS:Q_COLS + 2 * KV_COLS]
        z_ref[r, :] = z

        if has_hist:
            u = u + hist_ref[r, :]
        if full_u:
            u_ref[r, :] = u
        elif r.stop == tm:
            u_ref[...] = u[sub - SUBLANES:, :]
        base = SUBLANES + r.start
        ubuf[base:base + sub, :] = u
        acc = u * convw_ref[CONV_W - 1:CONV_W, :]
        for j in range(1, CONV_W):
            acc = acc + ubuf[base - j:base - j + sub, :] * convw_ref[CONV_W - 1 - j:CONV_W - j, :]
        c = _silu(acc)
        if has_hist:
            c = c * valid_ref[r, :]
        for h in range(GDN_HEADS):
            sl = slice(h * GDN_DK, (h + 1) * GDN_DK)
            qg_ref[r, sl] = l2n(c[:, sl]) * (GDN_DK ** -0.5)
            kg_ref[r, sl] = l2n(c[:, QK_COLS + h * GDN_DK:QK_COLS + (h + 1) * GDN_DK])
        vg_ref[r, :] = c[:, 2 * QK_COLS:]

        g = -jnp.exp(alog_ref[...]) * _softplus(ab + dtb_ref[...])
        beta = _sigmoid(ab)
        if has_hist:
            g = g * valid_ref[r, :]
            beta = beta * valid_ref[r, :]
        g = jnp.where(lane < GDN_HEADS, g, 0.0)
        gc = _dot_exact_lhs01(tri_ref[...], g)
        gl = _dot_exact_lhs01(seg_ref[...], g)
        gcb_ref[r, :] = jnp.where(lane < GDN_HEADS, gc,
                                  jnp.where(lane < 2 * GDN_HEADS, beta,
                                            jnp.where(lane < 3 * GDN_HEADS,
                                                      pltpu.roll(gl, 2 * GDN_HEADS, axis=1), 0.0)))


def _rope_tables(pos):
    half = ROT_DIM // 2
    inv_freq = ROPE_THETA ** (-jnp.arange(half, dtype=F32) * 2.0 / ROT_DIM)
    ang = pos.astype(F32)[:, None] * inv_freq[None, :]
    cos, sin = jnp.cos(ang), jnp.sin(ang)
    p = pos.shape[0]
    cpat = jnp.concatenate([cos, cos, jnp.ones((p, HEAD_DIM - ROT_DIM), F32)], axis=1)
    spat = jnp.concatenate([-sin, sin, jnp.zeros((p, HEAD_DIM - ROT_DIM), F32)], axis=1)
    return jnp.tile(cpat, (1, LANES // HEAD_DIM)), jnp.tile(spat, (1, LANES // HEAD_DIM))


def _segment_matrices(tm, seg_len):
    i = jnp.arange(tm)
    same = (i[:, None] // seg_len) == (i[None, :] // seg_len)
    tri = same & (i[None, :] <= i[:, None])
    return tri.astype(BF16), same.astype(BF16)


def _proj(x, pos, wts, seg_len, n_seq, hist=None, valid=None):
    n = x.shape[0]
    rows = n // n_seq
    tm = min(PROJ_TILE, rows)
    nt = rows // tm
    has_hist = hist is not None
    cos_t, sin_t = _rope_tables(pos)
    tri, seg = _segment_matrices(PROJ_SUB, seg_len)

    tok = lambda w: pl.BlockSpec((tm, w), lambda b, t: (b * nt + t, 0))
    const = lambda a: pl.BlockSpec(a.shape, lambda b, t: (0,) * a.ndim)
    in_arrays = [x, cos_t, sin_t, wts['wqkv'], wts['wg'], wts['wz'], wts['wab'],
                 wts['convw'], wts['alog'], wts['dtb'], tri, seg]
    in_specs = [tok(D_MODEL), pl.BlockSpec((tm, LANES), lambda b, t: (t, 0)),
                pl.BlockSpec((tm, LANES), lambda b, t: (t, 0))] + [const(a) for a in in_arrays[3:]]
    if has_hist:
        in_arrays += [hist, valid]
        in_specs += [tok(CONV_DIM), tok(1)]
    u_rows = n if has_hist else (n // tm) * SUBLANES
    u_block = tm if has_hist else SUBLANES
    out_shape = [jax.ShapeDtypeStruct((n, Q_COLS), F32), jax.ShapeDtypeStruct((n, KV_COLS), F32),
                 jax.ShapeDtypeStruct((n, KV_COLS), F32), jax.ShapeDtypeStruct((n, QK_COLS), F32),
                 jax.ShapeDtypeStruct((n, QK_COLS), F32), jax.ShapeDtypeStruct((n, Z_COLS), F32),
                 jax.ShapeDtypeStruct((n, Z_COLS), F32), jax.ShapeDtypeStruct((n, LANES), F32),
                 jax.ShapeDtypeStruct((u_rows, CONV_DIM), F32)]
    out_specs = [tok(Q_COLS), tok(KV_COLS), tok(KV_COLS), tok(QK_COLS), tok(QK_COLS), tok(Z_COLS),
                 tok(Z_COLS), tok(LANES),
                 pl.BlockSpec((u_block, CONV_DIM), lambda b, t: (b * nt + t, 0))]
    return pl.pallas_call(
        functools.partial(_proj_kernel, tm=tm, has_hist=has_hist, full_u=has_hist),
        out_shape=out_shape, grid=(n_seq, nt), in_specs=in_specs, out_specs=out_specs,
        scratch_shapes=[pltpu.VMEM((tm + SUBLANES, CONV_DIM), F32)],
        compiler_params=_cparams(("arbitrary", "arbitrary")),
        name="proj_hist" if has_hist else "proj",
    )(*in_arrays)


def _attn_blocks(qs, kcats, vcats, biases, sink, tq):
    lane = lax.broadcasted_iota(I32, (tq, LANES), 1)
    low = lane < HEAD_DIM
    n_slab = Q_COLS // LANES

    def stack(q):
        slabs = [q[:, j * LANES:(j + 1) * LANES] * (HEAD_DIM ** -0.5) for j in range(n_slab)]
        parts = ([jnp.where(low, s, 0.0) for s in slabs] + [jnp.where(low, 0.0, s) for s in slabs])
        return jnp.concatenate(parts, axis=0).astype(BF16)

    def unstack(o8):
        return [jnp.where(low, o8[j * tq:(j + 1) * tq, :], o8[(n_slab + j) * tq:(n_slab + j + 1) * tq, :])
                for j in range(n_slab)]

    rows = ATTN_HEADS * tq
    half = rows // 2
    klow = lax.broadcasted_iota(I32, (2 * WINDOW, LANES), 1) < HEAD_DIM
    one = jnp.ones((), BF16)
    q8s = _each(stack, qs)
    ss = _each(lambda q8, kc, b: _bdot_nt(q8, kc) + b, q8s, kcats, biases)
    ms = _each(lambda s: jnp.maximum(jnp.broadcast_to(jnp.max(s, axis=1, keepdims=True), (rows, LANES)),
                                     sink), ss)
    ps = _each(lambda s, m: jnp.exp(s - jnp.concatenate([m, m], axis=1)).astype(BF16), ss, ms)
    pv0 = _each(lambda p, vc: jnp.dot(p[:half], jnp.where(klow, vc, one), preferred_element_type=F32),
                ps, vcats)
    pv1 = _each(lambda p, vc: jnp.dot(p[half:], jnp.where(klow, one, vc), preferred_element_type=F32),
                ps, vcats)
    pvs = _each(lambda a, b: jnp.concatenate([a, b], axis=0), pv0, pv1)
    o8s = _each(lambda pv, m: pv / (pltpu.roll(pv, HEAD_DIM, axis=1) + jnp.exp(sink - m)), pvs, ms)
    return _each(unstack, o8s)


def _attn_prompt_kernel(q_ref, kc_ref, vc_ref, kp_ref, vp_ref, bias0_ref, bias_ref, sink_ref, o_ref, *,
                        nblk):
    kall = jnp.concatenate([kp_ref[...], kc_ref[...]], axis=0).astype(BF16)
    vall = jnp.concatenate([vp_ref[...], vc_ref[...]], axis=0).astype(BF16)
    win = lambda a, j: a[j * WINDOW:(j + 2) * WINDOW, :]
    qs = [q_ref[j * WINDOW:(j + 1) * WINDOW, :] for j in range(nblk)]
    biases = [bias0_ref[0]] + [bias_ref[...]] * (nblk - 1)
    outs = _attn_blocks(qs, [win(kall, j) for j in range(nblk)], [win(vall, j) for j in range(nblk)],
                        biases, sink_ref[...], WINDOW)
    for j, slabs in enumerate(outs):
        for c, slab in enumerate(slabs):
            o_ref[j * WINDOW:(j + 1) * WINDOW, c * LANES:(c + 1) * LANES] = slab


def _attn_sample_kernel(q_ref, kc_ref, vc_ref, kp_ref, vp_ref, bias_ref, sink_ref, o_ref, *, nseq):
    tq = SAMPLE_SLOTS
    zpad = jnp.zeros((WINDOW - tq, LANES), F32)
    rows = lambda ref, j: ref[j * tq:(j + 1) * tq, :]
    cat = lambda pref, cref, j: jnp.concatenate([pref[j], rows(cref, j), zpad], axis=0).astype(BF16)
    outs = _attn_blocks([rows(q_ref, j) for j in range(nseq)],
                        [cat(kp_ref, kc_ref, j) for j in range(nseq)],
                        [cat(vp_ref, vc_ref, j) for j in range(nseq)],
                        [bias_ref[...]] * nseq, sink_ref[...], tq)
    for j, slabs in enumerate(outs):
        for c, slab in enumerate(slabs):
            o_ref[j * tq:(j + 1) * tq, c * LANES:(c + 1) * LANES] = slab


def _sink_rows(sinks, tq):
    return jnp.broadcast_to(jnp.repeat(sinks.astype(F32), tq)[:, None], (ATTN_HEADS * tq, LANES))


def _attn_bias(tq, q_off, k_lo, k_hi, has_prev):
    qi = (jnp.arange(ATTN_HEADS * tq, dtype=I32) % tq)[:, None]
    c = jnp.arange(2 * WINDOW, dtype=I32)[None, :]
    cj = c - WINDOW
    vis_prev = (c < WINDOW) & (c > qi - q_off) & has_prev
    vis_cur = (c >= WINDOW) & (cj <= qi) & (cj >= k_lo) & (cj <= k_hi)
    return jnp.where(vis_prev | vis_cur, 0.0, NEG_BIG).astype(F32)


def _attn_prompt(q, k, v, sinks, n_seq):
    n = q.shape[0]
    nb = n // n_seq // WINDOW
    nblk = min(ATTN_BLOCKS_PER_STEP, nb)
    steps = nb // nblk
    tq = nblk * WINDOW
    cur = lambda w: pl.BlockSpec((tq, w), lambda b, i: (b * steps + i, 0))
    prev = pl.BlockSpec((WINDOW, LANES), lambda b, i: (b * nb + jnp.maximum(i * nblk - 1, 0), 0))
    bias2 = jnp.stack([_attn_bias(WINDOW, 0, 0, WINDOW - 1, False),
                       _attn_bias(WINDOW, 0, 0, WINDOW - 1, True)])
    rows = ATTN_HEADS * WINDOW
    return pl.pallas_call(
        functools.partial(_attn_prompt_kernel, nblk=nblk),
        out_shape=jax.ShapeDtypeStruct((n, Q_COLS), F32), grid=(n_seq, steps),
        in_specs=[cur(Q_COLS), cur(LANES), cur(LANES), prev, prev,
                  pl.BlockSpec((1, rows, 2 * WINDOW), lambda b, i: (jnp.minimum(i, 1), 0, 0)),
                  pl.BlockSpec((rows, 2 * WINDOW), lambda b, i: (0, 0)),
                  pl.BlockSpec((rows, LANES), lambda b, i: (0, 0))],
        out_specs=cur(Q_COLS),
        compiler_params=_cparams(("arbitrary", "arbitrary")), name="attn_prompt",
    )(q, k, v, k, v, bias2, bias2[1], _sink_rows(sinks, WINDOW))


def _attn_sample(q, k, v, cache_k, cache_v, sinks, n_seq):
    tq = SAMPLE_SLOTS
    nseq = min(ATTN_SEQS_PER_STEP, n_seq)
    cur = lambda w: pl.BlockSpec((nseq * tq, w), lambda b: (b, 0))
    prev = pl.BlockSpec((nseq, WINDOW, LANES), lambda b: (b, 0, 0))
    bias = _attn_bias(tq, SAMPLE_FIRST, SAMPLE_FIRST, SAMPLE_FIRST + 3, True)
    return pl.pallas_call(
        functools.partial(_attn_sample_kernel, nseq=nseq),
        out_shape=jax.ShapeDtypeStruct((n_seq * tq, Q_COLS), F32), grid=(n_seq // nseq,),
        in_specs=[cur(Q_COLS), cur(LANES), cur(LANES), prev, prev,
                  pl.BlockSpec(bias.shape, lambda b: (0, 0)),
                  pl.BlockSpec((ATTN_HEADS * tq, LANES), lambda b: (0, 0))],
        out_specs=cur(Q_COLS),
        compiler_params=_cparams(("arbitrary",)), name="attn_sample",
    )(q, k, v, cache_k, cache_v, bias, _sink_rows(sinks, tq))


def _each(f, *lists):
    return [f(*args) for args in zip(*lists)]


def _unit_lower_inverse(ms, eye, same_base):
    c = ms[0].shape[0]
    ds = _each(lambda m: jnp.where(same_base, m, 0.0), ms)
    ls = _each(lambda m, d: m - d, ms, ds)
    xs = _each(lambda d: eye - d, ds)
    ps = ds
    for _ in range(int(math.log2(INV_BASE)) - 1):
        ps = _each(_bdot, ps, ps)
        xs = _each(lambda x, p: x + _bdot(x, p), xs, ps)
    nblk = c // INV_BASE
    if nblk == 1:
        return xs
    ns = _each(_bdot, xs, ls)
    ys = _each(lambda n: eye - n, ns)
    pws = ns
    for _ in range(int(math.log2(nblk)) - 1):
        pws = _each(_bdot, pws, pws)
        ys = _each(lambda y, p: y + _bdot(y, p), ys, pws)
    return _each(_bdot, ys, xs)


def _gdn_intra(qs, ks, vs, gcs, gls, betas, same_seq, low_incl, low_strict, eye, same_base):
    del same_seq
    e_gcs = _each(jnp.exp, gcs)

    def decay_of(gc):
        gc_row = jnp.sum(jnp.where(eye > 0, gc, 0.0), axis=0, keepdims=True)
        return jnp.where(low_incl, jnp.exp(jnp.where(low_incl, gc - gc_row, 0.0)), 0.0)

    decays = _each(decay_of, gcs)
    kbs = _each(lambda k, b: k * b, ks, betas)
    vbs = _each(lambda v, b: v * b, vs, betas)
    kks = _each(_bdot_nt, kbs, ks)
    ms = _each(lambda kk, d: jnp.where(low_strict, kk * d, 0.0), kks, decays)
    tmats = _unit_lower_inverse(ms, eye, same_base)
    us = _each(_bdot, tmats, vbs)
    ws = _each(lambda t, kb, e: _bdot(t, kb * e), tmats, kbs, e_gcs)
    qks = _each(_bdot_nt, qs, ks)
    attns = _each(lambda qk, d: jnp.where(low_incl, qk * d, 0.0), qks, decays)
    q_decs = _each(lambda q, e: q * e, qs, e_gcs)
    k_decs = _each(lambda k, gl, gc: k * jnp.exp(gl - gc), ks, gls, gcs)
    return us, ws, attns, q_decs, k_decs


def _chunk_masks(c, seq_len):
    i = lax.broadcasted_iota(I32, (c, c), 0)
    j = lax.broadcasted_iota(I32, (c, c), 1)
    same_seq = _div_pow2(i, seq_len) == _div_pow2(j, seq_len)
    low_incl = same_seq & (i >= j)
    low_strict = same_seq & (i > j)
    eye = (i == j).astype(F32)
    same_base = _div_pow2(i, INV_BASE) == _div_pow2(j, INV_BASE)
    return same_seq, low_incl, low_strict, eye, same_base


def _gated_rms(o, z, nw):
    o = o * lax.rsqrt(jnp.mean(o * o, axis=1, keepdims=True) + NORM_EPS) * nw
    return o * _silu(z)


def _gdn_prompt_kernel(qg_ref, kg_ref, vg_ref, z_ref, gcb_ref, nw_ref, o_ref, s_out_ref, s_scr):
    c = GDN_CHUNK
    n = pl.program_id(1)

    @pl.when(n == 0)
    def _():
        s_scr[...] = jnp.zeros_like(s_scr)

    masks = _chunk_masks(c, c)
    nw = nw_ref[...]
    chains = [(b, h) for b in range(qg_ref.shape[0]) for h in range(GDN_HEADS)]
    hs = lambda h: slice(h * GDN_DK, (h + 1) * GDN_DK)
    col = lambda off: [gcb_ref[b, :, off + h:off + h + 1] for b, h in chains]
    gcs, betas, gls = col(0), col(GDN_HEADS), col(2 * GDN_HEADS)
    qs = [qg_ref[b, :, hs(h)] for b, h in chains]
    ks = [kg_ref[b, :, hs(h)] for b, h in chains]
    vs = [vg_ref[b, :, hs(h)] for b, h in chains]
    us, ws, attns, q_decs, k_decs = _gdn_intra(qs, ks, vs, gcs, gls, betas, *masks)
    ss = [s_scr[b, h] for b, h in chains]
    wss = _each(_bdot, ws, ss)
    qss = _each(_bdot, q_decs, ss)
    v_news = _each(lambda u, x: u - x, us, wss)
    avs = _each(_bdot, attns, v_news)
    kvs = _each(_bdot_tn, k_decs, v_news)
    for (b, h), s, gl, qsv, av, kv in zip(chains, ss, gls, qss, avs, kvs):
        s_scr[b, h] = s * jnp.exp(gl[0:1, :]) + kv
        o_ref[b, :, hs(h)] = _gated_rms(qsv + av, z_ref[b, :, hs(h)], nw)

    @pl.when(n == pl.num_programs(1) - 1)
    def _():
        s_out_ref[...] = s_scr[...]


def _gdn_prompt(qg, kg, vg, z, gcb, norm_w, n_seq):
    n = qg.shape[0]
    s_len = n // n_seq
    nb = min(GDN_SEQ_PER_STEP, n_seq)
    v3 = lambda a: a.reshape(n_seq, s_len, a.shape[-1])
    tok = lambda w: pl.BlockSpec((nb, GDN_CHUNK, w), lambda b, i: (b, i, 0))
    o, s = pl.pallas_call(
        _gdn_prompt_kernel,
        out_shape=[jax.ShapeDtypeStruct((n_seq, s_len, Z_COLS), F32),
                   jax.ShapeDtypeStruct((n_seq, GDN_HEADS, GDN_DK, GDN_DV), F32)],
        grid=(n_seq // nb, s_len // GDN_CHUNK),
        in_specs=[tok(QK_COLS), tok(QK_COLS), tok(Z_COLS), tok(Z_COLS), tok(LANES),
                  pl.BlockSpec((1, GDN_DV), lambda b, i: (0, 0))],
        out_specs=[tok(Z_COLS),
                   pl.BlockSpec((nb, GDN_HEADS, GDN_DK, GDN_DV), lambda b, i: (b, 0, 0, 0))],
        scratch_shapes=[pltpu.VMEM((nb, GDN_HEADS, GDN_DK, GDN_DV), F32)],
        compiler_params=_cparams(("arbitrary", "arbitrary")), name="gdn_prompt",
    )(v3(qg), v3(kg), v3(vg), v3(z), v3(gcb), norm_w)
    return o.reshape(n, Z_COLS), s


def _gdn_sample_kernel(qg_ref, kg_ref, vg_ref, z_ref, gcb_ref, nw_ref, s_in_ref, o_ref, s_out_ref):
    c = GDN_CHUNK
    n_sub = c // SAMPLE_SLOTS
    masks = _chunk_masks(c, SAMPLE_SLOTS)
    heads = range(GDN_HEADS)
    hs = lambda h: slice(h * GDN_DK, (h + 1) * GDN_DK)
    rs = lambda s: slice(s * SAMPLE_SLOTS, (s + 1) * SAMPLE_SLOTS)
    col = lambda off: [gcb_ref[:, off + h:off + h + 1] for h in heads]
    gcs, betas, gls = col(0), col(GDN_HEADS), col(2 * GDN_HEADS)
    us, ws, attns, q_decs, k_decs = _gdn_intra([qg_ref[:, hs(h)] for h in heads],
                                               [kg_ref[:, hs(h)] for h in heads],
                                               [vg_ref[:, hs(h)] for h in heads], gcs, gls, betas, *masks)
    pairs = [(h, s) for h in heads for s in range(n_sub)]
    sts = [s_in_ref[s, h] for h, s in pairs]
    boths = [jnp.concatenate([ws[h][rs(s), :], q_decs[h][rs(s), :]], axis=0) for h, s in pairs]
    rr = _each(_bdot, boths, sts)
    gather = lambda h, part: jnp.concatenate(
        [rr[h * n_sub + s][part * SAMPLE_SLOTS:(part + 1) * SAMPLE_SLOTS, :] for s in range(n_sub)], axis=0)
    v_news = [us[h] - gather(h, 0) for h in heads]
    avs = _each(_bdot, attns, v_news)
    row = lax.broadcasted_iota(I32, (c, LANES), 0)
    seq_of_row = _div_pow2(row, SAMPLE_SLOTS)
    kds = [jnp.where(seq_of_row == s, k_decs[h], 0.0) for h, s in pairs]
    kvs = _each(_bdot_tn, kds, [v_news[h] for h, _ in pairs])
    egls = _each(jnp.exp, gls)
    for (h, s), st, kv in zip(pairs, sts, kvs):
        s_out_ref[s, h] = st * egls[h][s * SAMPLE_SLOTS:s * SAMPLE_SLOTS + 1, :] + kv
    nw = nw_ref[...]
    for h in heads:
        o_ref[:, hs(h)] = _gated_rms(gather(h, 1) + avs[h], z_ref[:, hs(h)], nw)


def _gdn_sample(qg, kg, vg, z, gcb, norm_w, state):
    n = qg.shape[0]
    n_sub = GDN_CHUNK // SAMPLE_SLOTS
    tok = lambda w: pl.BlockSpec((GDN_CHUNK, w), lambda i: (i, 0))
    st = pl.BlockSpec((n_sub, GDN_HEADS, GDN_DK, GDN_DV), lambda i: (i, 0, 0, 0))
    return pl.pallas_call(
        _gdn_sample_kernel,
        out_shape=[jax.ShapeDtypeStruct((n, Z_COLS), F32),
                   jax.ShapeDtypeStruct(state.shape, F32)],
        grid=(n // GDN_CHUNK,),
        in_specs=[tok(QK_COLS), tok(QK_COLS), tok(Z_COLS), tok(Z_COLS), tok(LANES),
                  pl.BlockSpec((1, GDN_DV), lambda i: (0, 0)), st],
        out_specs=[tok(Z_COLS), st],
        compiler_params=_cparams(("arbitrary",)), name="gdn_sample",
    )(qg, kg, vg, z, gcb, norm_w, state)


def _post_kernel(a_ref, g_ref, x_ref, wo_ref, ln_g_ref, ln_b_ref, wr_ref, x1_ref, route_ref, *, tm):
    sub = PROJ_SUB
    rows = [slice(j * sub, (j + 1) * sub) for j in range(tm // sub)]
    d = lambda a, b: jnp.dot(a, b, preferred_element_type=F32)
    mixes = [d(a_ref[r, :].astype(BF16), wo_ref[0:Q_COLS, :]) + d(g_ref[r, :].astype(BF16), wo_ref[Q_COLS:, :])
             for r in rows]
    x1s = [_layer_norm(DEEPNORM_ALPHA * x_ref[r, :] + mix, ln_g_ref[...], ln_b_ref[...])
           for r, mix in zip(rows, mixes)]
    for r, x1 in zip(rows, x1s):
        x1_ref[r, :] = x1
    wh, wm = wr_ref[0], wr_ref[1]
    lgs = []
    for x1 in x1s:
        xh = x1.astype(BF16)
        xm = (x1 - xh.astype(F32)).astype(BF16)
        lgs.append(d(xh, wh) + d(xh, wm) + d(xm, wh))
    for r, lg in zip(rows, lgs):
        route_ref[r, :] = _route(lg)


def _route(lg):
    lane = lax.broadcasted_iota(I32, lg.shape, 1)
    lane_f = lane.astype(F32)
    big = float(LANES)

    def first_max(vals, mask):
        v = jnp.where(mask, vals, NEG_BIG)
        mx = jnp.max(v, axis=1, keepdims=True)
        idx = jnp.min(jnp.where(mask & (v == mx), lane_f, big), axis=1, keepdims=True)
        return mx, idx

    gmask = lane < N_GROUPS
    gmax, gidx = first_max(lg, gmask)
    gden = jnp.sum(jnp.where(gmask, jnp.exp(lg - gmax), 0.0), axis=1, keepdims=True)
    g_top_p = 1.0 / gden
    e_lane = lane - N_GROUPS
    e_group = _div_pow2(jnp.maximum(e_lane, 0), EXPERTS_PER_GROUP).astype(F32)
    emask = (e_lane >= 0) & (e_lane < N_EXPERTS) & (e_group == gidx)
    m1, i1 = first_max(lg, emask)
    eden = jnp.sum(jnp.where(emask, jnp.exp(lg - m1), 0.0), axis=1, keepdims=True)
    m2, i2 = first_max(lg, emask & (lane_f != i1))
    p1 = 1.0 / eden
    p2 = jnp.exp(m2 - m1) / eden
    tot = p1 + p2
    gate1 = g_top_p * (p1 / tot)
    gate2 = g_top_p * (p2 / tot)
    return jnp.where(lane == 0, gate1,
                     jnp.where(lane == 1, gate2,
                               jnp.where(lane == 2, i1 - N_GROUPS,
                                         jnp.where(lane == 3, i2 - N_GROUPS, 0.0))))


def _post(attn_o, gdn_o, x, wts):
    n = x.shape[0]
    tm = min(PROJ_TILE, n)
    tok = lambda w: pl.BlockSpec((tm, w), lambda i: (i, 0))
    const = lambda a: pl.BlockSpec(a.shape, lambda i: (0,) * a.ndim)
    consts = [wts['wo'], wts['ln1_g'], wts['ln1_b'], wts['wr']]
    return pl.pallas_call(
        functools.partial(_post_kernel, tm=tm),
        out_shape=[jax.ShapeDtypeStruct((n, D_MODEL), F32), jax.ShapeDtypeStruct((n, LANES), F32)],
        grid=(n // tm,),
        in_specs=[tok(Q_COLS), tok(Z_COLS), tok(D_MODEL)] + [const(a) for a in consts],
        out_specs=[tok(D_MODEL), tok(LANES)],
        compiler_params=_cparams(("arbitrary",)), name="post_%d" % (n // tm),
    )(attn_o, gdn_o, x, *consts)


def _slab_loop(n, body):
    n_main = jnp.right_shift(n, int(math.log2(SLAB_UNROLL)))

    def main(i, c):
        for u in range(SLAB_UNROLL):
            body(i * SLAB_UNROLL + u)
        return c

    lax.fori_loop(0, n_main, main, 0)
    lax.fori_loop(n_main * SLAB_UNROLL, n, lambda j, c: (body(j), c)[1], 0)


def _dispatch_kernel(dst_ref, nslab_ref, ztab_ref, zinfo_ref, slot_ref, gate_ref, *rest,
                     group_tiles, max_tiles):
    x_refs = rest[:len(group_tiles)]
    xs_ref, pbuf, sem, zbuf, zsem = rest[len(group_tiles):]
    n_tiles = sum(group_tiles)
    g = pl.program_id(0)
    cur = lax.rem(g, 2)

    def slab_copy(tile, buf_slot, j):
        d = pl.multiple_of(dst_ref[tile * PERM_SLABS + j], SUBLANES)
        src = pbuf.at[buf_slot, pl.ds(pl.multiple_of(j * SUBLANES, SUBLANES), SUBLANES), :]
        return pltpu.make_async_copy(src, xs_ref.at[pl.ds(d, SUBLANES), :], sem.at[buf_slot])

    def tail_copy(k):
        d = pl.multiple_of(ztab_ref[k], SUBLANES)
        return pltpu.make_async_copy(zbuf.at[pl.ds(0, SUBLANES), :], xs_ref.at[pl.ds(d, SUBLANES), :], zsem)

    def tile_copy(t):
        d = pl.multiple_of(t * ROW_TILE, ROW_TILE)
        return pltpu.make_async_copy(zbuf, xs_ref.at[pl.ds(d, ROW_TILE), :], zsem)

    @pl.when(g == 0)
    def _():
        zbuf[...] = jnp.zeros_like(zbuf)
        _slab_loop(zinfo_ref[0], lambda k: tail_copy(k).start())
        lax.fori_loop(zinfo_ref[1], max_tiles, lambda t, c: (tile_copy(t).start(), c)[1], 0)

    x = x_refs[-1][...]
    bound = n_tiles
    for x_ref, nt in zip(x_refs[-2::-1], group_tiles[:0:-1]):
        bound -= nt
        x = jnp.where(g < bound, x_ref[...], x)

    r = lax.broadcasted_iota(I32, (PERM_ROWS, TOK_TILE), 0)
    sl = slot_ref[0]
    hit0, hit1 = r == sl[0:1, :], r == sl[1:2, :]
    onehot = jnp.where(hit0 | hit1, 1.0, 0.0).astype(BF16)
    gt = gate_ref[0]
    gcol = jnp.sum(jnp.where(hit0, gt[0:1, :], 0.0) + jnp.where(hit1, gt[1:2, :], 0.0),
                   axis=1, keepdims=True)
    pbuf[cur, :, 0:D_MODEL] = jnp.dot(onehot, x.astype(BF16), preferred_element_type=F32)
    pbuf[cur, :, D_MODEL:] = jnp.broadcast_to(gcol, (PERM_ROWS, LANES))

    @pl.when(g > 0)
    def _():
        _slab_loop(nslab_ref[g - 1], lambda j: slab_copy(g - 1, 1 - cur, j).wait())

    _slab_loop(nslab_ref[g], lambda j: slab_copy(g, cur, j).start())

    @pl.when(g == n_tiles - 1)
    def _():
        _slab_loop(nslab_ref[g], lambda j: slab_copy(g, cur, j).wait())
        _slab_loop(zinfo_ref[0], lambda k: tail_copy(k).wait())
        lax.fori_loop(zinfo_ref[1], max_tiles, lambda t, c: (tile_copy(t).wait(), c)[1], 0)


def _dispatch(plan, x1s, max_tiles):
    group_tiles = tuple(x1.shape[0] // TOK_TILE for x1 in x1s)
    n_tiles = sum(group_tiles)
    tile = lambda i, d, ns, zt, zi: (i, 0, 0)
    in_specs = [pl.BlockSpec((1, TOP_K, TOK_TILE), tile), pl.BlockSpec((1, TOP_K, TOK_TILE), tile)]
    base = 0
    for nt in group_tiles:
        in_specs.append(pl.BlockSpec(
            (TOK_TILE, D_MODEL),
            lambda i, d, ns, zt, zi, base=base, nt=nt: (jnp.clip(i - base, 0, nt - 1), 0)))
        base += nt
    return pl.pallas_call(
        functools.partial(_dispatch_kernel, group_tiles=group_tiles, max_tiles=max_tiles),
        out_shape=jax.ShapeDtypeStruct((max_tiles * ROW_TILE, XS_WORDS), F32),
        grid_spec=pltpu.PrefetchScalarGridSpec(
            num_scalar_prefetch=4, grid=(n_tiles,), in_specs=in_specs,
            out_specs=pl.BlockSpec(memory_space=pl.ANY),
            scratch_shapes=[pltpu.VMEM((2, PERM_ROWS, XS_WORDS), F32), pltpu.SemaphoreType.DMA((2,)),
                            pltpu.VMEM((ROW_TILE, XS_WORDS), F32), pltpu.SemaphoreType.DMA(())]),
        compiler_params=_cparams(("arbitrary",)), name="moe_dispatch",
    )(plan['slab_dst'], plan['nslab'], plan['ztab'], plan['zinfo'], plan['slot_rows'], plan['gate_rows'],
      *x1s)


def _expert_kernel(te_ref, nu_ref, xs_ref, wg_ref, wu_ref, wd_ref, ye_ref):
    del te_ref

    @pl.when(pl.program_id(0) < nu_ref[0])
    def _():
        sub = EXPERT_SUB
        rows = [slice(j * sub, (j + 1) * sub) for j in range(ROW_TILE // sub)]
        d = lambda a, b: jnp.dot(a, b, preferred_element_type=F32)
        wg, wu, wd = wg_ref[0].astype(BF16), wu_ref[0].astype(BF16), wd_ref[0].astype(BF16)
        xs = [xs_ref[r, 0:D_MODEL].astype(BF16) for r in rows]
        hgs = [d(x, wg) for x in xs]
        hus = [d(x, wu) for x in xs]
        hhs = [(_silu(hg) * hu).astype(BF16) for hg, hu in zip(hgs, hus)]
        ys = [d(hh, wd) for hh in hhs]
        for r, y in zip(rows, ys):
            gate = xs_ref[r, D_MODEL:]
            ye_ref[r, :] = y * jnp.concatenate([gate] * (D_MODEL // LANES), axis=1)

    @pl.when(pl.program_id(0) >= nu_ref[0])
    def _():
        ye_ref[...] = jnp.zeros_like(ye_ref)


def _experts(tile_expert, n_used, xs, w_gate, w_up, w_down):
    n_tiles = xs.shape[0] // ROW_TILE
    row_in = lambda i, te, nu: (jnp.minimum(i, nu[0] - 1), 0)
    row = lambda i, te, nu: (i, 0)
    wsel = lambda i, te, nu: (te[i], 0, 0)
    return pl.pallas_call(
        _expert_kernel,
        out_shape=jax.ShapeDtypeStruct((xs.shape[0], D_MODEL), F32),
        grid_spec=pltpu.PrefetchScalarGridSpec(
            num_scalar_prefetch=2, grid=(n_tiles,),
            in_specs=[pl.BlockSpec((ROW_TILE, XS_WORDS), row_in),
                      pl.BlockSpec((1, D_MODEL, EXPERT_FF), wsel),
                      pl.BlockSpec((1, D_MODEL, EXPERT_FF), wsel),
                      pl.BlockSpec((1, EXPERT_FF, D_MODEL), wsel)],
            out_specs=pl.BlockSpec((ROW_TILE, D_MODEL), row)),
        compiler_params=_cparams(("arbitrary",)), name="moe_experts",
    )(tile_expert, n_used, xs, w_gate, w_up, w_down)


def _combine_kernel(dst_ref, nslab_ref, x1_ref, slot_ref, ye_ref, ln_g_ref, ln_b_ref, y_ref,
                    buf, sem, *, tile_base, n_tiles):
    i = pl.program_id(0)
    g = tile_base + i
    cur = lax.rem(i, 2)

    def slab_copy(tile, buf_slot, j):
        d = pl.multiple_of(dst_ref[tile * PERM_SLABS + j], SUBLANES)
        dst = buf.at[buf_slot, pl.ds(pl.multiple_of(j * SUBLANES, SUBLANES), SUBLANES), :]
        return pltpu.make_async_copy(ye_ref.at[pl.ds(d, SUBLANES), :], dst, sem.at[buf_slot])

    @pl.when(i == 0)
    def _():
        buf[...] = jnp.zeros_like(buf)
        _slab_loop(nslab_ref[g], lambda j: slab_copy(g, cur, j).start())

    @pl.when(i + 1 < n_tiles)
    def _():
        _slab_loop(nslab_ref[g + 1], lambda j: slab_copy(g + 1, 1 - cur, j).start())

    _slab_loop(nslab_ref[g], lambda j: slab_copy(g, cur, j).wait())

    col = lax.broadcasted_iota(I32, (TOK_TILE, PERM_ROWS), 1)
    sc = slot_ref[...]
    pick = jnp.where((col == sc[:, 0:1]) | (col == sc[:, 1:2]), 1.0, 0.0).astype(BF16)
    rows = buf[cur]
    r_hi = rows.astype(BF16)
    r_lo = (rows - r_hi.astype(F32)).astype(BF16)
    moe = (jnp.dot(pick, r_hi, preferred_element_type=F32)
           + jnp.dot(pick, r_lo, preferred_element_type=F32))
    y_ref[...] = _layer_norm(DEEPNORM_ALPHA * x1_ref[...] + moe, ln_g_ref[...], ln_b_ref[...])


def _combine(plan, tile_base, x1, ye, ln_g, ln_b):
    n = x1.shape[0]
    n_tiles = n // TOK_TILE
    tok = lambda w: pl.BlockSpec((TOK_TILE, w), lambda i, d, ns: (i, 0))
    const = lambda a: pl.BlockSpec(a.shape, lambda i, d, ns: (0,) * a.ndim)
    return pl.pallas_call(
        functools.partial(_combine_kernel, tile_base=tile_base, n_tiles=n_tiles),
        out_shape=jax.ShapeDtypeStruct((n, D_MODEL), F32),
        grid_spec=pltpu.PrefetchScalarGridSpec(
            num_scalar_prefetch=2, grid=(n_tiles,),
            in_specs=[tok(D_MODEL),
                      pl.BlockSpec((TOK_TILE, TOP_K), lambda i, d, ns: (tile_base + i, 0)),
                      pl.BlockSpec(memory_space=pl.ANY), const(ln_g), const(ln_b)],
            out_specs=tok(D_MODEL),
            scratch_shapes=[pltpu.VMEM((2, PERM_ROWS, D_MODEL), F32), pltpu.SemaphoreType.DMA((2,))]),
        compiler_params=_cparams(("arbitrary",)), name="moe_combine_%d" % tile_base,
    )(plan['slab_dst'], plan['nslab'], x1, plan['slot_cols'], ye, ln_g, ln_b)


def _routing_plan(ids, gates):
    nt = ids.shape[0] // TOK_TILE
    ex = jnp.arange(N_EXPERTS, dtype=I32)
    flat = ids.reshape(nt, TOP_K * TOK_TILE)
    onehot = (flat[:, :, None] == ex).astype(I32)
    p = jnp.arange(TOP_K * TOK_TILE, dtype=I32)
    tri = (p[None, :] <= p[:, None]).astype(BF16)
    csum = jnp.einsum('pq,tqe->tpe', tri, onehot.astype(BF16), preferred_element_type=F32).astype(I32)
    rank = jnp.sum(onehot * (csum - 1), axis=2)
    cnt = csum[:, -1, :]
    cpad = (cnt + SUBLANES - 1) // SUBLANES * SUBLANES
    seg_end = jnp.cumsum(cpad, axis=1)
    seg_off = seg_end - cpad
    slot = jnp.sum(onehot * seg_off[:, None, :], axis=2) + rank
    run_end = jnp.cumsum(cpad, axis=0)
    ntiles_e = (run_end[-1] + ROW_TILE - 1) // ROW_TILE
    tile_end = jnp.cumsum(ntiles_e)
    dst_run = ((tile_end - ntiles_e) * ROW_TILE)[None, :] + run_end - cpad
    j8 = jnp.arange(PERM_SLABS, dtype=I32) * SUBLANES
    e_of = jnp.minimum(jnp.sum((j8[None, :, None] >= seg_end[:, None, :]).astype(I32), axis=2),
                       N_EXPERTS - 1)
    sel = (e_of[:, :, None] == ex).astype(I32)
    slab_dst = jnp.sum(sel * (dst_run - seg_off)[:, None, :], axis=2) + j8[None, :]
    n_used = tile_end[-1]
    max_tiles = _max_row_tiles(ids.shape[0])
    t = jnp.arange(max_tiles, dtype=I32)
    te = jnp.sum((t[:, None] >= tile_end[None, :]).astype(I32), axis=1)
    te_last = jnp.sum((n_used - 1 >= tile_end).astype(I32))
    pair_slots = slot.reshape(nt, TOK_TILE, TOP_K)
    row_start = (tile_end - ntiles_e) * ROW_TILE
    tail_cnt = (ntiles_e * ROW_TILE - run_end[-1]) // SUBLANES
    tail_end = jnp.cumsum(tail_cnt)
    k = jnp.arange(ZERO_TABLE, dtype=I32)
    e_k = jnp.minimum(jnp.sum((k[:, None] >= tail_end[None, :]).astype(I32), axis=1), N_EXPERTS - 1)
    base_k = jnp.sum((e_k[:, None] == ex).astype(I32)
                     * (row_start + run_end[-1] - SUBLANES * (tail_end - tail_cnt))[None, :], axis=1)
    return dict(
        slab_dst=slab_dst.reshape(-1).astype(I32), nslab=(seg_end[:, -1] // SUBLANES).astype(I32),
        ztab=(base_k + SUBLANES * k).astype(I32), zinfo=jnp.stack([tail_end[-1], n_used]).astype(I32),
        slot_rows=jnp.swapaxes(pair_slots, 1, 2).astype(I32),
        slot_cols=pair_slots.reshape(nt * TOK_TILE, TOP_K).astype(I32),
        gate_rows=jnp.swapaxes(gates.reshape(nt, TOK_TILE, TOP_K), 1, 2).astype(F32),
        tile_expert=jnp.where(t < n_used, jnp.minimum(te, N_EXPERTS - 1), te_last).astype(I32),
        n_used=n_used.reshape(1).astype(I32))


def _max_row_tiles(n_tokens):
    rows = TOP_K * n_tokens + (n_tokens // TOK_TILE) * N_EXPERTS * (SUBLANES - 1)
    return (rows + ROW_TILE - 1) // ROW_TILE + N_EXPERTS


def _moe(x1s, routes, wts):
    ids = jnp.concatenate([r[:, 2:2 + TOP_K] for r in routes], axis=0).astype(I32)
    gates = jnp.concatenate([r[:, 0:TOP_K] for r in routes], axis=0)
    plan = _routing_plan(ids, gates)
    max_tiles = _max_row_tiles(ids.shape[0])
    bases = [0]
    for x1 in x1s[:-1]:
        bases.append(bases[-1] + x1.shape[0] // TOK_TILE)
    xs = _dispatch(plan, x1s, max_tiles)
    ye = _experts(plan['tile_expert'], plan['n_used'], xs, wts['w_gate'], wts['w_up'], wts['w_down'])
    return [_combine(plan, base, x1, ye, wts['ln2_g'], wts['ln2_b']) for base, x1 in zip(bases, x1s)]


def _prep_weights(w_in, w_out, conv_w, a_log, dt_bias, gdn_norm_w, ln1_g, ln1_b, w_router_group,
                  w_router_expert, w_gate, w_up, w_down, ln2_g, ln2_b):
    o1 = Q_COLS + 2 * KV_COLS
    o2 = o1 + CONV_DIM
    o3 = o2 + Z_COLS
    pad_row = lambda v: jnp.pad(v.astype(F32), (0, LANES - v.shape[0]))[None, :]
    wab = jnp.pad(w_in[:, o3:], ((0, 0), (0, LANES - 2 * GDN_HEADS)))
    wr = jnp.pad(jnp.concatenate([w_router_group, w_router_expert], axis=1),
                 ((0, 0), (0, LANES - N_GROUPS - N_EXPERTS)))
    wr_hi = wr.astype(BF16)
    wr_mid = (wr - wr_hi.astype(F32)).astype(BF16)
    group = ATTN_HEADS // ATTN_KV_HEADS
    head_order = jnp.array([h for j in range(group) for h in (j, j + group)], I32)
    col_order = (head_order[:, None] * HEAD_DIM + jnp.arange(HEAD_DIM, dtype=I32)[None, :]).reshape(-1)
    wqkv = jnp.concatenate([w_in[:, :Q_COLS][:, col_order], w_in[:, Q_COLS:o1]], axis=1)
    wo = jnp.concatenate([w_out[:Q_COLS][col_order], w_out[Q_COLS:]], axis=0)
    return dict(
        wqkv=wqkv.astype(BF16), wg=w_in[:, o1:o2].astype(BF16), wz=w_in[:, o2:o3].astype(BF16),
        wab=wab.astype(BF16), convw=conv_w.astype(F32), alog=pad_row(a_log), dtb=pad_row(dt_bias),
        norm_w=gdn_norm_w.astype(F32)[None, :], wo=wo.astype(BF16),
        ln1_g=ln1_g[None, :], ln1_b=ln1_b[None, :], wr=jnp.stack([wr_hi, wr_mid]),
        w_gate=w_gate, w_up=w_up, w_down=w_down, ln2_g=ln2_g[None, :], ln2_b=ln2_b[None, :])


def _layer(x_prompt, x_sample, cache_k, cache_v, state_gdn, state_conv, wts):
    bp, sp, _ = x_prompt.shape
    bs, ts, _ = x_sample.shape
    n_p = bp * sp

    xp = x_prompt.reshape(n_p, D_MODEL)
    (q, k, v, qg, kg, vg, z, gcb, utail) = _proj(xp, jnp.arange(sp, dtype=I32), wts, GDN_CHUNK, bp)
    attn_p = _attn_prompt(q, k, v, wts['sinks'], bp)
    gdn_p, s_p = _gdn_prompt(qg, kg, vg, z, gcb, wts['norm_w'], bp)
    new_k_p = k.reshape(bp, sp, ATTN_KV_HEADS, HEAD_DIM)[:, sp - WINDOW:]
    new_v_p = v.reshape(bp, sp, ATTN_KV_HEADS, HEAD_DIM)[:, sp - WINDOW:]
    tiles_per_seq = sp // min(PROJ_TILE, sp)
    conv_p = utail.reshape(bp, tiles_per_seq, SUBLANES, CONV_DIM)[:, -1, SUBLANES - (CONV_W - 1):]

    lo, hi = SAMPLE_FIRST, SAMPLE_FIRST + ts
    xs_rows = jnp.pad(x_sample, ((0, 0), (lo, SAMPLE_SLOTS - hi), (0, 0))).reshape(bs * SAMPLE_SLOTS, D_MODEL)
    hist = jnp.pad(state_conv, ((0, 0), (0, SAMPLE_SLOTS - lo), (0, 0))).reshape(bs * SAMPLE_SLOTS, CONV_DIM)
    slot = jnp.arange(SAMPLE_SLOTS, dtype=I32)
    valid = jnp.tile(((slot >= lo) & (slot < hi)).astype(F32), bs)[:, None]
    pos_s = jnp.tile(PAST_LEN + slot - lo, bs)
    (q, k, v, qg, kg, vg, z, gcb, u_s) = _proj(xs_rows, pos_s, wts, SAMPLE_SLOTS, 1, hist, valid)
    ck = cache_k.reshape(bs, WINDOW, KV_COLS)
    cv = cache_v.reshape(bs, WINDOW, KV_COLS)
    attn_s = _attn_sample(q, k, v, ck, cv, wts['sinks'], bs)
    gdn_s, s_s = _gdn_sample(qg, kg, vg, z, gcb, wts['norm_w'], state_gdn)
    real = lambda a: a.reshape(bs, SAMPLE_SLOTS, -1)[:, lo:hi]
    k_new, v_new = real(k), real(v)
    new_k_s = jnp.concatenate([ck, k_new], axis=1)[:, -WINDOW:].reshape(bs, WINDOW, ATTN_KV_HEADS, HEAD_DIM)
    new_v_s = jnp.concatenate([cv, v_new], axis=1)[:, -WINDOW:].reshape(bs, WINDOW, ATTN_KV_HEADS, HEAD_DIM)
    conv_s = u_s.reshape(bs, SAMPLE_SLOTS, CONV_DIM)[:, hi - (CONV_W - 1):hi]

    x1_p, route_p = _post(attn_p, gdn_p, xp, wts)
    x1_s, route_s = _post(real(attn_s).reshape(bs * ts, Q_COLS), real(gdn_s).reshape(bs * ts, Z_COLS),
                          x_sample.reshape(bs * ts, D_MODEL), wts)
    y_p, y_s = _moe([x1_p, x1_s], [route_p, route_s], wts)
    return (y_p.reshape(bp, sp, D_MODEL), y_s.reshape(bs, ts, D_MODEL), new_k_p, new_v_p, s_p, conv_p,
            new_k_s, new_v_s, s_s, conv_s)


def kernel(x_prompt, x_sample, cache_attn_k, cache_attn_v, state_gdn, state_conv, w_in, w_out,
           attn_sinks, conv_w, a_log, dt_bias, gdn_norm_w, ln1_g, ln1_b, w_router_group,
           w_router_expert, w_gate, w_up, w_down, ln2_g, ln2_b):
    assert w_in.shape[0] == DEPTH
    l = 0
    wts = _prep_weights(w_in[l], w_out[l], conv_w[l], a_log[l], dt_bias[l], gdn_norm_w[l], ln1_g[l],
                        ln1_b[l], w_router_group[l], w_router_expert[l], w_gate[l], w_up[l],
                        w_down[l], ln2_g[l], ln2_b[l])
    wts['sinks'] = attn_sinks[l]
    outs = _layer(x_prompt, x_sample, cache_attn_k[l], cache_attn_v[l], state_gdn[l], state_conv[l], wts)
    (y_p, y_s, k_p, v_p, s_p, c_p, k_s, v_s, s_s, c_s) = outs
    add = lambda a: a[None]
    return (y_p, y_s, add(k_p), add(v_p), add(s_p), add(c_p), add(k_s), add(v_s), add(s_s), add(c_s))
```

```python
import functools
import math

import jax
import jax.numpy as jnp
from jax import lax
from jax.experimental import pallas as pl
from jax.experimental.pallas import tpu as pltpu

F32 = jnp.float32
BF16 = jnp.bfloat16
I32 = jnp.int32

D_MODEL = 1024
ATTN_HEADS = 8
ATTN_KV_HEADS = 2
HEAD_DIM = 64
WINDOW = 128
ROT_DIM = HEAD_DIM // 4
ROPE_THETA = 500000.0
GDN_HEADS = 4
GDN_DK = 128
GDN_DV = 128
CONV_W = 4
QK_COLS = GDN_HEADS * GDN_DK
CONV_DIM = 2 * QK_COLS + GDN_HEADS * GDN_DV
Z_COLS = GDN_HEADS * GDN_DV
Q_COLS = ATTN_HEADS * HEAD_DIM
KV_COLS = ATTN_KV_HEADS * HEAD_DIM
N_GROUPS = 4
EXPERTS_PER_GROUP = 8
N_EXPERTS = N_GROUPS * EXPERTS_PER_GROUP
TOP_K = 2
EXPERT_FF = 256
NORM_EPS = 1e-5
L2_EPS = 1e-6
DEPTH = 1
DEEPNORM_ALPHA = (2 * DEPTH) ** 0.25
PAST_LEN = 8192

LANES = 128
SUBLANES = 8
TOK_TILE = 256
PROJ_TILE = 512
PROJ_SUB = 128
GDN_CHUNK = 128
GDN_SEQ_PER_STEP = 4
ATTN_BLOCKS_PER_STEP = 4
ATTN_SEQS_PER_STEP = 8
INV_BASE = 16
SAMPLE_SLOTS = 8
SAMPLE_FIRST = CONV_W - 1
ROW_TILE = 512
EXPERT_SUB = 256
SLAB_UNROLL = 4
PERM_ROWS = TOP_K * TOK_TILE + N_EXPERTS * SUBLANES
PERM_SLABS = PERM_ROWS // SUBLANES
XS_WORDS = D_MODEL + LANES
ZERO_TABLE = N_EXPERTS * (ROW_TILE // SUBLANES)
VMEM_LIMIT = 48 * 1024 * 1024
NEG_BIG = -1e30


def _cparams(sem):
    return pltpu.CompilerParams(dimension_semantics=sem, vmem_limit_bytes=VMEM_LIMIT)


def _bdot(a, b):
    return jnp.dot(a.astype(BF16), b.astype(BF16), preferred_element_type=F32)


def _bdot_nt(a, b):
    return lax.dot_general(a.astype(BF16), b.astype(BF16), (((1,), (1,)), ((), ())),
                           preferred_element_type=F32)


def _bdot_tn(a, b):
    return lax.dot_general(a.astype(BF16), b.astype(BF16), (((0,), (0,)), ((), ())),
                           preferred_element_type=F32)


def _div_pow2(x, n):
    return jnp.right_shift(x, int(math.log2(n)))


def _mod_pow2(x, n):
    return jnp.bitwise_and(x, n - 1)


def _split3(x):
    hi = x.astype(BF16)
    r = x - hi.astype(F32)
    mid = r.astype(BF16)
    lo = (r - mid.astype(F32)).astype(BF16)
    return hi, mid, lo


def _dot_exact_lhs01(m01, x):
    hi, mid, lo = _split3(x)
    d = lambda t: jnp.dot(m01, t, preferred_element_type=F32)
    return d(hi) + d(mid) + d(lo)


def _sigmoid(x):
    return 1.0 / (1.0 + jnp.exp(-x))


def _silu(x):
    return x * _sigmoid(x)


def _softplus(x):
    return jnp.maximum(x, 0.0) + jnp.log1p(jnp.exp(-jnp.abs(x)))


def _layer_norm(h, g, b):
    mu = jnp.mean(h, axis=-1, keepdims=True)
    d = h - mu
    var = jnp.mean(d * d, axis=-1, keepdims=True)
    return d * lax.rsqrt(var + NORM_EPS) * g + b


def _proj_kernel(*refs, tm, has_hist, full_u):
    it = iter(refs)
    x_ref, cos_ref, sin_ref = next(it), next(it), next(it)
    wqkv_ref, wg_ref, wz_ref, wab_ref = next(it), next(it), next(it), next(it)
    convw_ref, alog_ref, dtb_ref, tri_ref, seg_ref = next(it), next(it), next(it), next(it), next(it)
    hist_ref = valid_ref = None
    if has_hist:
        hist_ref, valid_ref = next(it), next(it)
    q_ref, k_ref, v_ref = next(it), next(it), next(it)
    qg_ref, kg_ref, vg_ref, z_ref, gcb_ref, u_ref = (next(it) for _ in range(6))
    ubuf = next(it)

    t = pl.program_id(1)
    sub = PROJ_SUB
    rows = [slice(j * sub, (j + 1) * sub) for j in range(tm // sub)]
    lane = lax.broadcasted_iota(I32, (sub, LANES), 1)
    first_half = _mod_pow2(lane, HEAD_DIM) < (ROT_DIM // 2)

    @pl.when(t == 0)
    def _():
        ubuf[0:SUBLANES, :] = jnp.zeros((SUBLANES, CONV_DIM), F32)

    @pl.when(t > 0)
    def _():
        ubuf[0:SUBLANES, :] = ubuf[tm:tm + SUBLANES, :]

    dots = []
    for r in rows:
        xb = x_ref[r, :].astype(BF16)
        dots.append([jnp.dot(xb, w[...], preferred_element_type=F32)
                     for w in (wqkv_ref, wg_ref, wz_ref, wab_ref)])

    def l2n(s):
        return s * lax.rsqrt(jnp.sum(s * s, axis=1, keepdims=True) + L2_EPS)

    for r, (pq, u, z, ab) in zip(rows, dots):
        cosv, sinv = cos_ref[r, :], sin_ref[r, :]

        def rope(s):
            sw = jnp.where(first_half, pltpu.roll(s, LANES - ROT_DIM // 2, axis=1),
                           pltpu.roll(s, ROT_DIM // 2, axis=1))
            return s * cosv + sw * sinv

        for j in range(Q_COLS // LANES):
            q_ref[r, j * LANES:(j + 1) * LANES] = rope(pq[:, j * LANES:(j + 1) * LANES])
        k_ref[r, :] = rope(pq[:, Q_COLS:Q_COLS + KV_COLS])
        v_ref[r, :] = pq[:, Q_COLS + KV_COLS:Q_COLS + 2 * KV_COLS]
        z_ref[r, :] = z

        if has_hist:
            u = u + hist_ref[r, :]
        if full_u:
            u_ref[r, :] = u
        elif r.stop == tm:
            u_ref[...] = u[sub - SUBLANES:, :]
        base = SUBLANES + r.start
        ubuf[base:base + sub, :] = u
        acc = u * convw_ref[CONV_W - 1:CONV_W, :]
        for j in range(1, CONV_W):
            acc = acc + ubuf[base - j:base - j + sub, :] * convw_ref[CONV_W - 1 - j:CONV_W - j, :]
        c = _silu(acc)
        if has_hist:
            c = c * valid_ref[r, :]
        for h in range(GDN_HEADS):
            sl = slice(h * GDN_DK, (h + 1) * GDN_DK)
            qg_ref[r, sl] = l2n(c[:, sl]) * (GDN_DK ** -0.5)
            kg_ref[r, sl] = l2n(c[:, QK_COLS + h * GDN_DK:QK_COLS + (h + 1) * GDN_DK])
        vg_ref[r, :] = c[:, 2 * QK_COLS:]

        g = -jnp.exp(alog_ref[...]) * _softplus(ab + dtb_ref[...])
        beta = _sigmoid(ab)
        if has_hist:
            g = g * valid_ref[r, :]
            beta = beta * valid_ref[r, :]
        g = jnp.where(lane < GDN_HEADS, g, 0.0)
        gc = _dot_exact_lhs01(tri_ref[...], g)
        gl = _dot_exact_lhs01(seg_ref[...], g)
        gcb_ref[r, :] = jnp.where(lane < GDN_HEADS, gc,
                                  jnp.where(lane < 2 * GDN_HEADS, beta,
                                            jnp.where(lane < 3 * GDN_HEADS,
                                                      pltpu.roll(gl, 2 * GDN_HEADS, axis=1), 0.0)))


def _rope_tables(pos):
    half = ROT_DIM // 2
    inv_freq = ROPE_THETA ** (-jnp.arange(half, dtype=F32) * 2.0 / ROT_DIM)
    ang = pos.astype(F32)[:, None] * inv_freq[None, :]
    cos, sin = jnp.cos(ang), jnp.sin(ang)
    p = pos.shape[0]
    cpat = jnp.concatenate([cos, cos, jnp.ones((p, HEAD_DIM - ROT_DIM), F32)], axis=1)
    spat = jnp.concatenate([-sin, sin, jnp.zeros((p, HEAD_DIM - ROT_DIM), F32)], axis=1)
    return jnp.tile(cpat, (1, LANES // HEAD_DIM)), jnp.tile(spat, (1, LANES // HEAD_DIM))


def _segment_matrices(tm, seg_len):
    i = jnp.arange(tm)
    same = (i[:, None] // seg_len) == (i[None, :] // seg_len)
    tri = same & (i[None, :] <= i[:, None])
    return tri.astype(BF16), same.astype(BF16)


def _proj(x, pos, wts, seg_len, n_seq, hist=None, valid=None):
    n = x.shape[0]
    rows = n // n_seq
    tm = min(PROJ_TILE, rows)
    nt = rows // tm
    has_hist = hist is not None
    cos_t, sin_t = _rope_tables(pos)
    tri, seg = _segment_matrices(PROJ_SUB, seg_len)

    tok = lambda w: pl.BlockSpec((tm, w), lambda b, t: (b * nt + t, 0))
    const = lambda a: pl.BlockSpec(a.shape, lambda b, t: (0,) * a.ndim)
    in_arrays = [x, cos_t, sin_t, wts['wqkv'], wts['wg'], wts['wz'], wts['wab'],
                 wts['convw'], wts['alog'], wts['dtb'], tri, seg]
    in_specs = [tok(D_MODEL), pl.BlockSpec((tm, LANES), lambda b, t: (t, 0)),
                pl.BlockSpec((tm, LANES), lambda b, t: (t, 0))] + [const(a) for a in in_arrays[3:]]
    if has_hist:
        in_arrays += [hist, valid]
        in_specs += [tok(CONV_DIM), tok(1)]
    u_rows = n if has_hist else (n // tm) * SUBLANES
    u_block = tm if has_hist else SUBLANES
    out_shape = [jax.ShapeDtypeStruct((n, Q_COLS), F32), jax.ShapeDtypeStruct((n, KV_COLS), F32),
                 jax.ShapeDtypeStruct((n, KV_COLS), F32), jax.ShapeDtypeStruct((n, QK_COLS), F32),
                 jax.ShapeDtypeStruct((n, QK_COLS), F32), jax.ShapeDtypeStruct((n, Z_COLS), F32),
                 jax.ShapeDtypeStruct((n, Z_COLS), F32), jax.ShapeDtypeStruct((n, LANES), F32),
                 jax.ShapeDtypeStruct((u_rows, CONV_DIM), F32)]
    out_specs = [tok(Q_COLS), tok(KV_COLS), tok(KV_COLS), tok(QK_COLS), tok(QK_COLS), tok(Z_COLS),
                 tok(Z_COLS), tok(LANES),
                 pl.BlockSpec((u_block, CONV_DIM), lambda b, t: (b * nt + t, 0))]
    return pl.pallas_call(
        functools.partial(_proj_kernel, tm=tm, has_hist=has_hist, full_u=has_hist),
        out_shape=out_shape, grid=(n_seq, nt), in_specs=in_specs, out_specs=out_specs,
        scratch_shapes=[pltpu.VMEM((tm + SUBLANES, CONV_DIM), F32)],
        compiler_params=_cparams(("arbitrary", "arbitrary")),
        name="proj_hist" if has_hist else "proj",
    )(*in_arrays)


def _attn_blocks(qs, kcats, vcats, biases, sink, tq):
    lane = lax.broadcasted_iota(I32, (tq, LANES), 1)
    low = lane < HEAD_DIM
    n_slab = Q_COLS // LANES

    def stack(q):
        slabs = [q[:, j * LANES:(j + 1) * LANES] * (HEAD_DIM ** -0.5) for j in range(n_slab)]
        parts = ([jnp.where(low, s, 0.0) for s in slabs] + [jnp.where(low, 0.0, s) for s in slabs])
        return jnp.concatenate(parts, axis=0).astype(BF16)

    def unstack(o8):
        return [jnp.where(low, o8[j * tq:(j + 1) * tq, :], o8[(n_slab + j) * tq:(n_slab + j + 1) * tq, :])
                for j in range(n_slab)]

    rows = ATTN_HEADS * tq
    half = rows // 2
    klow = lax.broadcasted_iota(I32, (2 * WINDOW, LANES), 1) < HEAD_DIM
    one = jnp.ones((), BF16)
    q8s = _each(stack, qs)
    ss = _each(lambda q8, kc, b: _bdot_nt(q8, kc) + b, q8s, kcats, biases)
    ms = _each(lambda s: jnp.maximum(jnp.broadcast_to(jnp.max(s, axis=1, keepdims=True), (rows, LANES)),
                                     sink), ss)
    ps = _each(lambda s, m: jnp.exp(s - jnp.concatenate([m, m], axis=1)).astype(BF16), ss, ms)
    pv0 = _each(lambda p, vc: jnp.dot(p[:half], jnp.where(klow, vc, one), preferred_element_type=F32),
                ps, vcats)
    pv1 = _each(lambda p, vc: jnp.dot(p[half:], jnp.where(klow, one, vc), preferred_element_type=F32),
                ps, vcats)
    pvs = _each(lambda a, b: jnp.concatenate([a, b], axis=0), pv0, pv1)
    o8s = _each(lambda pv, m: pv / (pltpu.roll(pv, HEAD_DIM, axis=1) + jnp.exp(sink - m)), pvs, ms)
    return _each(unstack, o8s)


def _attn_prompt_kernel(q_ref, kc_ref, vc_ref, kp_ref, vp_ref, bias0_ref, bias_ref, sink_ref, o_ref, *,
                        nblk):
    kall = jnp.concatenate([kp_ref[...], kc_ref[...]], axis=0).astype(BF16)
    vall = jnp.concatenate([vp_ref[...], vc_ref[...]], axis=0).astype(BF16)
    win = lambda a, j: a[j * WINDOW:(j + 2) * WINDOW, :]
    qs = [q_ref[j * WINDOW:(j + 1) * WINDOW, :] for j in range(nblk)]
    biases = [bias0_ref[0]] + [bias_ref[...]] * (nblk - 1)
    outs = _attn_blocks(qs, [win(kall, j) for j in range(nblk)], [win(vall, j) for j in range(nblk)],
                        biases, sink_ref[...], WINDOW)
    for j, slabs in enumerate(outs):
        for c, slab in enumerate(slabs):
            o_ref[j * WINDOW:(j + 1) * WINDOW, c * LANES:(c + 1) * LANES] = slab


def _attn_sample_kernel(q_ref, kc_ref, vc_ref, kp_ref, vp_ref, bias_ref, sink_ref, o_ref, kw_ref, vw_ref,
                        *, nseq, n_new):
    tq = SAMPLE_SLOTS
    zpad = jnp.zeros((WINDOW - tq, LANES), F32)
    rows = lambda ref, j: ref[j * tq:(j + 1) * tq, :]
    cat = lambda pref, cref, j: jnp.concatenate([pref[j], rows(cref, j), zpad], axis=0).astype(BF16)
    outs = _attn_blocks([rows(q_ref, j) for j in range(nseq)],
                        [cat(kp_ref, kc_ref, j) for j in range(nseq)],
                        [cat(vp_ref, vc_ref, j) for j in range(nseq)],
                        [bias_ref[...]] * nseq, sink_ref[...], tq)
    for j, slabs in enumerate(outs):
        for c, slab in enumerate(slabs):
            o_ref[j * tq:(j + 1) * tq, c * LANES:(c + 1) * LANES] = slab
    row = lax.broadcasted_iota(I32, (WINDOW, LANES), 0)
    keep = WINDOW - n_new
    for pref, cref, wref in ((kp_ref, kc_ref, kw_ref), (vp_ref, vc_ref, vw_ref)):
        for j in range(nseq):
            new = jnp.concatenate([rows(cref, j), zpad], axis=0)
            wref[j] = jnp.where(row < keep, pltpu.roll(pref[j], keep, axis=0),
                                pltpu.roll(new, keep - SAMPLE_FIRST, axis=0))


def _sink_rows(sinks, tq):
    return jnp.broadcast_to(jnp.repeat(sinks.astype(F32), tq)[:, None], (ATTN_HEADS * tq, LANES))


def _attn_bias(tq, q_off, k_lo, k_hi, has_prev):
    qi = (jnp.arange(ATTN_HEADS * tq, dtype=I32) % tq)[:, None]
    c = jnp.arange(2 * WINDOW, dtype=I32)[None, :]
    cj = c - WINDOW
    vis_prev = (c < WINDOW) & (c > qi - q_off) & has_prev
    vis_cur = (c >= WINDOW) & (cj <= qi) & (cj >= k_lo) & (cj <= k_hi)
    return jnp.where(vis_prev | vis_cur, 0.0, NEG_BIG).astype(F32)


def _attn_prompt(q, k, v, sinks, n_seq):
    n = q.shape[0]
    nb = n // n_seq // WINDOW
    nblk = min(ATTN_BLOCKS_PER_STEP, nb)
    steps = nb // nblk
    tq = nblk * WINDOW
    cur = lambda w: pl.BlockSpec((tq, w), lambda b, i: (b * steps + i, 0))
    prev = pl.BlockSpec((WINDOW, LANES), lambda b, i: (b * nb + jnp.maximum(i * nblk - 1, 0), 0))
    bias2 = jnp.stack([_attn_bias(WINDOW, 0, 0, WINDOW - 1, False),
                       _attn_bias(WINDOW, 0, 0, WINDOW - 1, True)])
    rows = ATTN_HEADS * WINDOW
    return pl.pallas_call(
        functools.partial(_attn_prompt_kernel, nblk=nblk),
        out_shape=jax.ShapeDtypeStruct((n, Q_COLS), F32), grid=(n_seq, steps),
        in_specs=[cur(Q_COLS), cur(LANES), cur(LANES), prev, prev,
                  pl.BlockSpec((1, rows, 2 * WINDOW), lambda b, i: (jnp.minimum(i, 1), 0, 0)),
                  pl.BlockSpec((rows, 2 * WINDOW), lambda b, i: (0, 0)),
                  pl.BlockSpec((rows, LANES), lambda b, i: (0, 0))],
        out_specs=cur(Q_COLS),
        compiler_params=_cparams(("arbitrary", "arbitrary")), name="attn_prompt",
    )(q, k, v, k, v, bias2, bias2[1], _sink_rows(sinks, WINDOW))


def _attn_sample(q, k, v, cache_k, cache_v, sinks, n_seq, n_new):
    tq = SAMPLE_SLOTS
    nseq = min(ATTN_SEQS_PER_STEP, n_seq)
    cur = lambda w: pl.BlockSpec((nseq * tq, w), lambda b: (b, 0))
    prev = pl.BlockSpec((nseq, WINDOW, LANES), lambda b: (b, 0, 0))
    bias = _attn_bias(tq, SAMPLE_FIRST, SAMPLE_FIRST, SAMPLE_FIRST + 3, True)
    win = jax.ShapeDtypeStruct((n_seq, WINDOW, LANES), F32)
    return pl.pallas_call(
        functools.partial(_attn_sample_kernel, nseq=nseq, n_new=n_new),
        out_shape=[jax.ShapeDtypeStruct((n_seq * tq, Q_COLS), F32), win, win], grid=(n_seq // nseq,),
        in_specs=[cur(Q_COLS), cur(LANES), cur(LANES), prev, prev,
                  pl.BlockSpec(bias.shape, lambda b: (0, 0)),
                  pl.BlockSpec((ATTN_HEADS * tq, LANES), lambda b: (0, 0))],
        out_specs=[cur(Q_COLS), prev, prev],
        compiler_params=_cparams(("arbitrary",)), name="attn_sample",
    )(q, k, v, cache_k, cache_v, bias, _sink_rows(sinks, tq))


def _each(f, *lists):
    return [f(*args) for args in zip(*lists)]


def _unit_lower_inverse(ms, eye, same_base):
    c = ms[0].shape[0]
    ds = _each(lambda m: jnp.where(same_base, m, 0.0), ms)
    ls = _each(lambda m, d: m - d, ms, ds)
    xs = _each(lambda d: eye - d, ds)
    ps = ds
    for _ in range(int(math.log2(INV_BASE)) - 1):
        ps = _each(_bdot, ps, ps)
        xs = _each(lambda x, p: x + _bdot(x, p), xs, ps)
    nblk = c // INV_BASE
    if nblk == 1:
        return xs
    ns = _each(_bdot, xs, ls)
    ys = _each(lambda n: eye - n, ns)
    pws = ns
    for _ in range(int(math.log2(nblk)) - 1):
        pws = _each(_bdot, pws, pws)
        ys = _each(lambda y, p: y + _bdot(y, p), ys, pws)
    return _each(_bdot, ys, xs)


def _gdn_intra(qs, ks, vs, gcs, gls, betas, same_seq, low_incl, low_strict, eye, same_base):
    del same_seq
    e_gcs = _each(jnp.exp, gcs)

    def decay_of(gc):
        gc_row = jnp.sum(jnp.where(eye > 0, gc, 0.0), axis=0, keepdims=True)
        return jnp.where(low_incl, jnp.exp(jnp.where(low_incl, gc - gc_row, 0.0)), 0.0)

    decays = _each(decay_of, gcs)
    kbs = _each(lambda k, b: k * b, ks, betas)
    vbs = _each(lambda v, b: v * b, vs, betas)
    kks = _each(_bdot_nt, kbs, ks)
    ms = _each(lambda kk, d: jnp.where(low_strict, kk * d, 0.0), kks, decays)
    tmats = _unit_lower_inverse(ms, eye, same_base)
    us = _each(_bdot, tmats, vbs)
    ws = _each(lambda t, kb, e: _bdot(t, kb * e), tmats, kbs, e_gcs)
    qks = _each(_bdot_nt, qs, ks)
    attns = _each(lambda qk, d: jnp.where(low_incl, qk * d, 0.0), qks, decays)
    q_decs = _each(lambda q, e: q * e, qs, e_gcs)
    k_decs = _each(lambda k, gl, gc: k * jnp.exp(gl - gc), ks, gls, gcs)
    return us, ws, attns, q_decs, k_decs


def _chunk_masks(c, seq_len):
    i = lax.broadcasted_iota(I32, (c, c), 0)
    j = lax.broadcasted_iota(I32, (c, c), 1)
    same_seq = _div_pow2(i, seq_len) == _div_pow2(j, seq_len)
    low_incl = same_seq & (i >= j)
    low_strict = same_seq & (i > j)
    eye = (i == j).astype(F32)
    same_base = _div_pow2(i, INV_BASE) == _div_pow2(j, INV_BASE)
    return same_seq, low_incl, low_strict, eye, same_base


def _gated_rms(o, z, nw):
    o = o * lax.rsqrt(jnp.mean(o * o, axis=1, keepdims=True) + NORM_EPS) * nw
    return o * _silu(z)


def _gdn_prompt_kernel(qg_ref, kg_ref, vg_ref, z_ref, gcb_ref, nw_ref, o_ref, s_out_ref, s_scr):
    c = GDN_CHUNK
    n = pl.program_id(1)

    @pl.when(n == 0)
    def _():
        s_scr[...] = jnp.zeros_like(s_scr)

    masks = _chunk_masks(c, c)
    nw = nw_ref[...]
    chains = [(b, h) for b in range(qg_ref.shape[0]) for h in range(GDN_HEADS)]
    hs = lambda h: slice(h * GDN_DK, (h + 1) * GDN_DK)
    col = lambda off: [gcb_ref[b, :, off + h:off + h + 1] for b, h in chains]
    gcs, betas, gls = col(0), col(GDN_HEADS), col(2 * GDN_HEADS)
    qs = [qg_ref[b, :, hs(h)] for b, h in chains]
    ks = [kg_ref[b, :, hs(h)] for b, h in chains]
    vs = [vg_ref[b, :, hs(h)] for b, h in chains]
    us, ws, attns, q_decs, k_decs = _gdn_intra(qs, ks, vs, gcs, gls, betas, *masks)
    ss = [s_scr[b, h] for b, h in chains]
    wss = _each(_bdot, ws, ss)
    qss = _each(_bdot, q_decs, ss)
    v_news = _each(lambda u, x: u - x, us, wss)
    avs = _each(_bdot, attns, v_news)
    kvs = _each(_bdot_tn, k_decs, v_news)
    for (b, h), s, gl, qsv, av, kv in zip(chains, ss, gls, qss, avs, kvs):
        s_scr[b, h] = s * jnp.exp(gl[0:1, :]) + kv
        o_ref[b, :, hs(h)] = _gated_rms(qsv + av, z_ref[b, :, hs(h)], nw)

    @pl.when(n == pl.num_programs(1) - 1)
    def _():
        s_out_ref[...] = s_scr[...]


def _gdn_prompt(qg, kg, vg, z, gcb, norm_w, n_seq):
    n = qg.shape[0]
    s_len = n // n_seq
    nb = min(GDN_SEQ_PER_STEP, n_seq)
    v3 = lambda a: a.reshape(n_seq, s_len, a.shape[-1])
    tok = lambda w: pl.BlockSpec((nb, GDN_CHUNK, w), lambda b, i: (b, i, 0))
    o, s = pl.pallas_call(
        _gdn_prompt_kernel,
        out_shape=[jax.ShapeDtypeStruct((n_seq, s_len, Z_COLS), F32),
                   jax.ShapeDtypeStruct((n_seq, GDN_HEADS, GDN_DK, GDN_DV), F32)],
        grid=(n_seq // nb, s_len // GDN_CHUNK),
        in_specs=[tok(QK_COLS), tok(QK_COLS), tok(Z_COLS), tok(Z_COLS), tok(LANES),
                  pl.BlockSpec((1, GDN_DV), lambda b, i: (0, 0))],
        out_specs=[tok(Z_COLS),
                   pl.BlockSpec((nb, GDN_HEADS, GDN_DK, GDN_DV), lambda b, i: (b, 0, 0, 0))],
        scratch_shapes=[pltpu.VMEM((nb, GDN_HEADS, GDN_DK, GDN_DV), F32)],
        compiler_params=_cparams(("arbitrary", "arbitrary")), name="gdn_prompt",
    )(v3(qg), v3(kg), v3(vg), v3(z), v3(gcb), norm_w)
    return o.reshape(n, Z_COLS), s


def _gdn_sample_kernel(qg_ref, kg_ref, vg_ref, z_ref, gcb_ref, nw_ref, s_in_ref, o_ref, s_out_ref):
    c = GDN_CHUNK
    n_sub = c // SAMPLE_SLOTS
    masks = _chunk_masks(c, SAMPLE_SLOTS)
    heads = range(GDN_HEADS)
    hs = lambda h: slice(h * GDN_DK, (h + 1) * GDN_DK)
    rs = lambda s: slice(s * SAMPLE_SLOTS, (s + 1) * SAMPLE_SLOTS)
    col = lambda off: [gcb_ref[:, off + h:off + h + 1] for h in heads]
    gcs, betas, gls = col(0), col(GDN_HEADS), col(2 * GDN_HEADS)
    us, ws, attns, q_decs, k_decs = _gdn_intra([qg_ref[:, hs(h)] for h in heads],
                                               [kg_ref[:, hs(h)] for h in heads],
                                               [vg_ref[:, hs(h)] for h in heads], gcs, gls, betas, *masks)
    pairs = [(h, s) for h in heads for s in range(n_sub)]
    sts = [s_in_ref[s, h] for h, s in pairs]
    boths = [jnp.concatenate([ws[h][rs(s), :], q_decs[h][rs(s), :]], axis=0) for h, s in pairs]
    rr = _each(_bdot, boths, sts)
    gather = lambda h, part: jnp.concatenate(
        [rr[h * n_sub + s][part * SAMPLE_SLOTS:(part + 1) * SAMPLE_SLOTS, :] for s in range(n_sub)], axis=0)
    v_news = [us[h] - gather(h, 0) for h in heads]
    avs = _each(_bdot, attns, v_news)
    row = lax.broadcasted_iota(I32, (c, LANES), 0)
    seq_of_row = _div_pow2(row, SAMPLE_SLOTS)
    kds = [jnp.where(seq_of_row == s, k_decs[h], 0.0) for h, s in pairs]
    kvs = _each(_bdot_tn, kds, [v_news[h] for h, _ in pairs])
    egls = _each(jnp.exp, gls)
    for (h, s), st, kv in zip(pairs, sts, kvs):
        s_out_ref[s, h] = st * egls[h][s * SAMPLE_SLOTS:s * SAMPLE_SLOTS + 1, :] + kv
    nw = nw_ref[...]
    for h in heads:
        o_ref[:, hs(h)] = _gated_rms(gather(h, 1) + avs[h], z_ref[:, hs(h)], nw)


def _gdn_sample(qg, kg, vg, z, gcb, norm_w, state):
    n = qg.shape[0]
    n_sub = GDN_CHUNK // SAMPLE_SLOTS
    tok = lambda w: pl.BlockSpec((GDN_CHUNK, w), lambda i: (i, 0))
    st = pl.BlockSpec((n_sub, GDN_HEADS, GDN_DK, GDN_DV), lambda i: (i, 0, 0, 0))
    return pl.pallas_call(
        _gdn_sample_kernel,
        out_shape=[jax.ShapeDtypeStruct((n, Z_COLS), F32),
                   jax.ShapeDtypeStruct(state.shape, F32)],
        grid=(n // GDN_CHUNK,),
        in_specs=[tok(QK_COLS), tok(QK_COLS), tok(Z_COLS), tok(Z_COLS), tok(LANES),
                  pl.BlockSpec((1, GDN_DV), lambda i: (0, 0)), st],
        out_specs=[tok(Z_COLS), st],
        compiler_params=_cparams(("arbitrary",)), name="gdn_sample",
    )(qg, kg, vg, z, gcb, norm_w, state)


def _post_kernel(a_ref, g_ref, x_ref, wo_ref, ln_g_ref, ln_b_ref, wr_ref, x1_ref, route_ref, *, tm):
    sub = PROJ_SUB
    rows = [slice(j * sub, (j + 1) * sub) for j in range(tm // sub)]
    d = lambda a, b: jnp.dot(a, b, preferred_element_type=F32)
    mixes = [d(a_ref[r, :].astype(BF16), wo_ref[0:Q_COLS, :]) + d(g_ref[r, :].astype(BF16), wo_ref[Q_COLS:, :])
             for r in rows]
    x1s = [_layer_norm(DEEPNORM_ALPHA * x_ref[r, :] + mix, ln_g_ref[...], ln_b_ref[...])
           for r, mix in zip(rows, mixes)]
    for r, x1 in zip(rows, x1s):
        x1_ref[r, :] = x1
    wh, wm = wr_ref[0], wr_ref[1]
    lgs = []
    for x1 in x1s:
        xh = x1.astype(BF16)
        xm = (x1 - xh.astype(F32)).astype(BF16)
        lgs.append(d(xh, wh) + d(xh, wm) + d(xm, wh))
    for r, lg in zip(rows, lgs):
        route_ref[:, r] = jnp.transpose(_route(lg))[0:SUBLANES, :]


def _route(lg):
    lane = lax.broadcasted_iota(I32, lg.shape, 1)
    lane_f = lane.astype(F32)
    big = float(LANES)

    def first_max(vals, mask):
        v = jnp.where(mask, vals, NEG_BIG)
        mx = jnp.max(v, axis=1, keepdims=True)
        idx = jnp.min(jnp.where(mask & (v == mx), lane_f, big), axis=1, keepdims=True)
        return mx, idx

    gmask = lane < N_GROUPS
    gmax, gidx = first_max(lg, gmask)
    gden = jnp.sum(jnp.where(gmask, jnp.exp(lg - gmax), 0.0), axis=1, keepdims=True)
    g_top_p = 1.0 / gden
    e_lane = lane - N_GROUPS
    e_group = _div_pow2(jnp.maximum(e_lane, 0), EXPERTS_PER_GROUP).astype(F32)
    emask = (e_lane >= 0) & (e_lane < N_EXPERTS) & (e_group == gidx)
    m1, i1 = first_max(lg, emask)
    eden = jnp.sum(jnp.where(emask, jnp.exp(lg - m1), 0.0), axis=1, keepdims=True)
    m2, i2 = first_max(lg, emask & (lane_f != i1))
    p1 = 1.0 / eden
    p2 = jnp.exp(m2 - m1) / eden
    tot = p1 + p2
    gate1 = g_top_p * (p1 / tot)
    gate2 = g_top_p * (p2 / tot)
    return jnp.where(lane == 0, gate1,
                     jnp.where(lane == 1, gate2,
                               jnp.where(lane == 2, i1 - N_GROUPS,
                                         jnp.where(lane == 3, i2 - N_GROUPS, 0.0))))


def _post(attn_o, gdn_o, x, wts):
    n = x.shape[0]
    tm = min(PROJ_TILE, n)
    tok = lambda w: pl.BlockSpec((tm, w), lambda i: (i, 0))
    const = lambda a: pl.BlockSpec(a.shape, lambda i: (0,) * a.ndim)
    consts = [wts['wo'], wts['ln1_g'], wts['ln1_b'], wts['wr']]
    return pl.pallas_call(
        functools.partial(_post_kernel, tm=tm),
        out_shape=[jax.ShapeDtypeStruct((n, D_MODEL), F32), jax.ShapeDtypeStruct((SUBLANES, n), F32)],
        grid=(n // tm,),
        in_specs=[tok(Q_COLS), tok(Z_COLS), tok(D_MODEL)] + [const(a) for a in consts],
        out_specs=[tok(D_MODEL), pl.BlockSpec((SUBLANES, tm), lambda i: (0, i))],
        compiler_params=_cparams(("arbitrary",)), name="post_%d" % (n // tm),
    )(attn_o, gdn_o, x, *consts)


def _slab_loop(n, body):
    n_main = jnp.right_shift(n, int(math.log2(SLAB_UNROLL)))

    def main(i, c):
        for u in range(SLAB_UNROLL):
            body(i * SLAB_UNROLL + u)
        return c

    lax.fori_loop(0, n_main, main, 0)
    lax.fori_loop(n_main * SLAB_UNROLL, n, lambda j, c: (body(j), c)[1], 0)


def _dispatch_kernel(dst_ref, nslab_ref, ztab_ref, zinfo_ref, slot_ref, gate_ref, *rest,
                     group_tiles, max_tiles):
    x_refs = rest[:len(group_tiles)]
    xs_ref, pbuf, sem, zbuf, zsem = rest[len(group_tiles):]
    n_tiles = sum(group_tiles)
    g = pl.program_id(0)
    cur = lax.rem(g, 2)

    def slab_copy(tile, buf_slot, j):
        d = pl.multiple_of(dst_ref[tile * PERM_SLABS + j], SUBLANES)
        src = pbuf.at[buf_slot, pl.ds(pl.multiple_of(j * SUBLANES, SUBLANES), SUBLANES), :]
        return pltpu.make_async_copy(src, xs_ref.at[pl.ds(d, SUBLANES), :], sem.at[buf_slot])

    def tail_copy(k):
        d = pl.multiple_of(ztab_ref[k], SUBLANES)
        return pltpu.make_async_copy(zbuf.at[pl.ds(0, SUBLANES), :], xs_ref.at[pl.ds(d, SUBLANES), :], zsem)

    def tile_copy(t):
        d = pl.multiple_of(t * ROW_TILE, ROW_TILE)
        return pltpu.make_async_copy(zbuf, xs_ref.at[pl.ds(d, ROW_TILE), :], zsem)

    @pl.when(g == 0)
    def _():
        zbuf[...] = jnp.zeros_like(zbuf)
        _slab_loop(zinfo_ref[0], lambda k: tail_copy(k).start(priority=1))
        lax.fori_loop(zinfo_ref[1], max_tiles, lambda t, c: (tile_copy(t).start(priority=1), c)[1], 0)

    x = x_refs[-1][...]
    bound = n_tiles
    for x_ref, nt in zip(x_refs[-2::-1], group_tiles[:0:-1]):
        bound -= nt
        x = jnp.where(g < bound, x_ref[...], x)

    r = lax.broadcasted_iota(I32, (PERM_ROWS, TOK_TILE), 0)
    sl = slot_ref[0]
    hit0, hit1 = r == sl[0:1, :], r == sl[1:2, :]
    onehot = jnp.where(hit0 | hit1, 1.0, 0.0).astype(BF16)
    gt = gate_ref[0]
    gcol = jnp.sum(jnp.where(hit0, gt[0:1, :], 0.0) + jnp.where(hit1, gt[1:2, :], 0.0),
                   axis=1, keepdims=True)
    pbuf[cur, :, 0:D_MODEL] = jnp.dot(onehot, x.astype(BF16), preferred_element_type=F32)
    pbuf[cur, :, D_MODEL:] = jnp.broadcast_to(gcol, (PERM_ROWS, LANES))

    @pl.when(g > 0)
    def _():
        _slab_loop(nslab_ref[g - 1], lambda j: slab_copy(g - 1, 1 - cur, j).wait())

    _slab_loop(nslab_ref[g], lambda j: slab_copy(g, cur, j).start())

    @pl.when(g == n_tiles - 1)
    def _():
        _slab_loop(nslab_ref[g], lambda j: slab_copy(g, cur, j).wait())
        _slab_loop(zinfo_ref[0], lambda k: tail_copy(k).wait())
        lax.fori_loop(zinfo_ref[1], max_tiles, lambda t, c: (tile_copy(t).wait(), c)[1], 0)


def _dispatch(plan, x1s, max_tiles):
    group_tiles = tuple(x1.shape[0] // TOK_TILE for x1 in x1s)
    n_tiles = sum(group_tiles)
    tile = lambda i, d, ns, zt, zi: (i, 0, 0)
    in_specs = [pl.BlockSpec((1, TOP_K, TOK_TILE), tile), pl.BlockSpec((1, TOP_K, TOK_TILE), tile)]
    base = 0
    for nt in group_tiles:
        in_specs.append(pl.BlockSpec(
            (TOK_TILE, D_MODEL),
            lambda i, d, ns, zt, zi, base=base, nt=nt: (jnp.clip(i - base, 0, nt - 1), 0)))
        base += nt
    return pl.pallas_call(
        functools.partial(_dispatch_kernel, group_tiles=group_tiles, max_tiles=max_tiles),
        out_shape=jax.ShapeDtypeStruct((max_tiles * ROW_TILE, XS_WORDS), F32),
        grid_spec=pltpu.PrefetchScalarGridSpec(
            num_scalar_prefetch=4, grid=(n_tiles,), in_specs=in_specs,
            out_specs=pl.BlockSpec(memory_space=pl.ANY),
            scratch_shapes=[pltpu.VMEM((2, PERM_ROWS, XS_WORDS), F32), pltpu.SemaphoreType.DMA((2,)),
                            pltpu.VMEM((ROW_TILE, XS_WORDS), F32), pltpu.SemaphoreType.DMA(())]),
        compiler_params=_cparams(("arbitrary",)), name="moe_dispatch",
    )(plan['slab_dst'], plan['nslab'], plan['ztab'], plan['zinfo'], plan['slot_rows'], plan['gate_rows'],
      *x1s)


def _expert_kernel(te_ref, nu_ref, xs_ref, wg_ref, wu_ref, wd_ref, ye_ref):
    del te_ref

    @pl.when(pl.program_id(0) < nu_ref[0])
    def _():
        sub = EXPERT_SUB
        rows = [slice(j * sub, (j + 1) * sub) for j in range(ROW_TILE // sub)]
        d = lambda a, b: jnp.dot(a, b, preferred_element_type=F32)
        wg, wu, wd = wg_ref[0].astype(BF16), wu_ref[0].astype(BF16), wd_ref[0].astype(BF16)
        xs = [xs_ref[r, 0:D_MODEL].astype(BF16) for r in rows]
        hgs = [d(x, wg) for x in xs]
        hus = [d(x, wu) for x in xs]
        hhs = [(_silu(hg) * hu).astype(BF16) for hg, hu in zip(hgs, hus)]
        ys = [d(hh, wd) for hh in hhs]
        for r, y in zip(rows, ys):
            gate = xs_ref[r, D_MODEL:]
            ye_ref[r, :] = y * jnp.concatenate([gate] * (D_MODEL // LANES), axis=1)

    @pl.when(pl.program_id(0) >= nu_ref[0])
    def _():
        ye_ref[...] = jnp.zeros_like(ye_ref)


def _experts(tile_expert, n_used, xs, w_gate, w_up, w_down):
    n_tiles = xs.shape[0] // ROW_TILE
    row_in = lambda i, te, nu: (jnp.minimum(i, nu[0] - 1), 0)
    row = lambda i, te, nu: (i, 0)
    wsel = lambda i, te, nu: (te[i], 0, 0)
    return pl.pallas_call(
        _expert_kernel,
        out_shape=jax.ShapeDtypeStruct((xs.shape[0], D_MODEL), F32),
        grid_spec=pltpu.PrefetchScalarGridSpec(
            num_scalar_prefetch=2, grid=(n_tiles,),
            in_specs=[pl.BlockSpec((ROW_TILE, XS_WORDS), row_in),
                      pl.BlockSpec((1, D_MODEL, EXPERT_FF), wsel),
                      pl.BlockSpec((1, D_MODEL, EXPERT_FF), wsel),
                      pl.BlockSpec((1, EXPERT_FF, D_MODEL), wsel)],
            out_specs=pl.BlockSpec((ROW_TILE, D_MODEL), row)),
        compiler_params=_cparams(("arbitrary",)), name="moe_experts",
    )(tile_expert, n_used, xs, w_gate, w_up, w_down)


def _combine_kernel(dst_ref, nslab_ref, x1_ref, slot_ref, ye_ref, ln_g_ref, ln_b_ref, y_ref,
                    buf, sem, *, tile_base, n_tiles):
    i = pl.program_id(0)
    g = tile_base + i
    cur = lax.rem(i, 2)

    def slab_copy(tile, buf_slot, j):
        d = pl.multiple_of(dst_ref[tile * PERM_SLABS + j], SUBLANES)
        dst = buf.at[buf_slot, pl.ds(pl.multiple_of(j * SUBLANES, SUBLANES), SUBLANES), :]
        return pltpu.make_async_copy(ye_ref.at[pl.ds(d, SUBLANES), :], dst, sem.at[buf_slot])

    @pl.when(i == 0)
    def _():
        buf[...] = jnp.zeros_like(buf)
        _slab_loop(nslab_ref[g], lambda j: slab_copy(g, cur, j).start())

    @pl.when(i + 1 < n_tiles)
    def _():
        _slab_loop(nslab_ref[g + 1], lambda j: slab_copy(g + 1, 1 - cur, j).start())

    _slab_loop(nslab_ref[g], lambda j: slab_copy(g, cur, j).wait())

    col = lax.broadcasted_iota(I32, (TOK_TILE, PERM_ROWS), 1)
    sl = slot_ref[0]
    diag = (lax.broadcasted_iota(I32, (TOK_TILE, TOK_TILE), 0)
            == lax.broadcasted_iota(I32, (TOK_TILE, TOK_TILE), 1))
    as_col = lambda row: jnp.sum(jnp.where(diag, row, 0), axis=1, keepdims=True)
    pick = jnp.where((col == as_col(sl[0:1, :])) | (col == as_col(sl[1:2, :])), 1.0, 0.0).astype(BF16)
    rows = buf[cur]
    r_hi = rows.astype(BF16)
    r_lo = (rows - r_hi.astype(F32)).astype(BF16)
    moe = (jnp.dot(pick, r_hi, preferred_element_type=F32)
           + jnp.dot(pick, r_lo, preferred_element_type=F32))
    y_ref[...] = _layer_norm(DEEPNORM_ALPHA * x1_ref[...] + moe, ln_g_ref[...], ln_b_ref[...])


def _combine(plan, tile_base, x1, ye, ln_g, ln_b):
    n = x1.shape[0]
    n_tiles = n // TOK_TILE
    tok = lambda w: pl.BlockSpec((TOK_TILE, w), lambda i, d, ns: (i, 0))
    const = lambda a: pl.BlockSpec(a.shape, lambda i, d, ns: (0,) * a.ndim)
    return pl.pallas_call(
        functools.partial(_combine_kernel, tile_base=tile_base, n_tiles=n_tiles),
        out_shape=jax.ShapeDtypeStruct((n, D_MODEL), F32),
        grid_spec=pltpu.PrefetchScalarGridSpec(
            num_scalar_prefetch=2, grid=(n_tiles,),
            in_specs=[tok(D_MODEL),
                      pl.BlockSpec((1, TOP_K, TOK_TILE), lambda i, d, ns: (tile_base + i, 0, 0)),
                      pl.BlockSpec(memory_space=pl.ANY), const(ln_g), const(ln_b)],
            out_specs=tok(D_MODEL),
            scratch_shapes=[pltpu.VMEM((2, PERM_ROWS, D_MODEL), F32), pltpu.SemaphoreType.DMA((2,))]),
        compiler_params=_cparams(("arbitrary",)), name="moe_combine_%d" % tile_base,
    )(plan['slab_dst'], plan['nslab'], x1, plan['slot_rows'], ye, ln_g, ln_b)


def _routing_plan(ids, gates):
    nt = ids.shape[1] // TOK_TILE
    pairs = TOP_K * TOK_TILE
    ex = jnp.arange(N_EXPERTS, dtype=I32)
    per_tile = lambda a: jnp.swapaxes(a.reshape(TOP_K, nt, TOK_TILE), 0, 1)
    flat = per_tile(ids).reshape(nt, pairs)
    onehot = (flat[:, None, :] == ex[None, :, None])
    p = jnp.arange(pairs, dtype=I32)
    triu = (p[:, None] <= p[None, :]).astype(BF16)
    csum = jnp.dot(onehot.astype(BF16).reshape(nt * N_EXPERTS, pairs), triu,
                   preferred_element_type=F32).astype(I32).reshape(nt, N_EXPERTS, pairs)
    oh = onehot.astype(I32)
    rank = jnp.sum(oh * (csum - 1), axis=1)
    cnt = csum[:, :, -1]
    cpad = (cnt + SUBLANES - 1) // SUBLANES * SUBLANES
    seg_end = jnp.cumsum(cpad, axis=1)
    seg_off = seg_end - cpad
    slot = jnp.sum(oh * seg_off[:, :, None], axis=1) + rank
    run_end = jnp.cumsum(cpad, axis=0)
    ntiles_e = (run_end[-1] + ROW_TILE - 1) // ROW_TILE
    tile_end = jnp.cumsum(ntiles_e)
    dst_run = ((tile_end - ntiles_e) * ROW_TILE)[None, :] + run_end - cpad
    j8 = jnp.arange(PERM_SLABS, dtype=I32) * SUBLANES
    e_of = jnp.minimum(jnp.sum((j8[None, :, None] >= seg_end[:, None, :]).astype(I32), axis=2),
                       N_EXPERTS - 1)
    sel = (e_of[:, :, None] == ex).astype(I32)
    slab_dst = jnp.sum(sel * (dst_run - seg_off)[:, None, :], axis=2) + j8[None, :]
    n_used = tile_end[-1]
    max_tiles = _max_row_tiles(ids.shape[1])
    t = jnp.arange(max_tiles, dtype=I32)
    te = jnp.sum((t[:, None] >= tile_end[None, :]).astype(I32), axis=1)
    te_last = jnp.sum((n_used - 1 >= tile_end).astype(I32))
    row_start = (tile_end - ntiles_e) * ROW_TILE
    tail_cnt = (ntiles_e * ROW_TILE - run_end[-1]) // SUBLANES
    tail_end = jnp.cumsum(tail_cnt)
    k = jnp.arange(ZERO_TABLE, dtype=I32)
    e_k = jnp.minimum(jnp.sum((k[:, None] >= tail_end[None, :]).astype(I32), axis=1), N_EXPERTS - 1)
    base_k = jnp.sum((e_k[:, None] == ex).astype(I32)
                     * (row_start + run_end[-1] - SUBLANES * (tail_end - tail_cnt))[None, :], axis=1)
    return dict(
        slab_dst=slab_dst.reshape(-1).astype(I32), nslab=(seg_end[:, -1] // SUBLANES).astype(I32),
        ztab=(base_k + SUBLANES * k).astype(I32), zinfo=jnp.stack([tail_end[-1], n_used]).astype(I32),
        slot_rows=slot.reshape(nt, TOP_K, TOK_TILE).astype(I32),
        gate_rows=per_tile(gates).astype(F32),
        tile_expert=jnp.where(t < n_used, jnp.minimum(te, N_EXPERTS - 1), te_last).astype(I32),
        n_used=n_used.reshape(1).astype(I32))


def _max_row_tiles(n_tokens):
    rows = TOP_K * n_tokens + (n_tokens // TOK_TILE) * N_EXPERTS * (SUBLANES - 1)
    return (rows + ROW_TILE - 1) // ROW_TILE + N_EXPERTS


def _moe(x1s, routes, wts):
    ids = jnp.concatenate([r[TOP_K:2 * TOP_K, :] for r in routes], axis=1).astype(I32)
    gates = jnp.concatenate([r[0:TOP_K, :] for r in routes], axis=1)
    plan = _routing_plan(ids, gates)
    max_tiles = _max_row_tiles(ids.shape[1])
    bases = [0]
    for x1 in x1s[:-1]:
        bases.append(bases[-1] + x1.shape[0] // TOK_TILE)
    xs = _dispatch(plan, x1s, max_tiles)
    ye = _experts(plan['tile_expert'], plan['n_used'], xs, wts['w_gate'], wts['w_up'], wts['w_down'])
    return [_combine(plan, base, x1, ye, wts['ln2_g'], wts['ln2_b']) for base, x1 in zip(bases, x1s)]


def _prep_weights(w_in, w_out, conv_w, a_log, dt_bias, gdn_norm_w, ln1_g, ln1_b, w_router_group,
                  w_router_expert, w_gate, w_up, w_down, ln2_g, ln2_b):
    o1 = Q_COLS + 2 * KV_COLS
    o2 = o1 + CONV_DIM
    o3 = o2 + Z_COLS
    pad_row = lambda v: jnp.pad(v.astype(F32), (0, LANES - v.shape[0]))[None, :]
    wab = jnp.pad(w_in[:, o3:], ((0, 0), (0, LANES - 2 * GDN_HEADS)))
    wr = jnp.pad(jnp.concatenate([w_router_group, w_router_expert], axis=1),
                 ((0, 0), (0, LANES - N_GROUPS - N_EXPERTS)))
    wr_hi = wr.astype(BF16)
    wr_mid = (wr - wr_hi.astype(F32)).astype(BF16)
    group = ATTN_HEADS // ATTN_KV_HEADS
    head_order = jnp.array([h for j in range(group) for h in (j, j + group)], I32)
    col_order = (head_order[:, None] * HEAD_DIM + jnp.arange(HEAD_DIM, dtype=I32)[None, :]).reshape(-1)
    wqkv = jnp.concatenate([w_in[:, :Q_COLS][:, col_order], w_in[:, Q_COLS:o1]], axis=1)
    wo = jnp.concatenate([w_out[:Q_COLS][col_order], w_out[Q_COLS:]], axis=0)
    return dict(
        wqkv=wqkv.astype(BF16), wg=w_in[:, o1:o2].astype(BF16), wz=w_in[:, o2:o3].astype(BF16),
        wab=wab.astype(BF16), convw=conv_w.astype(F32), alog=pad_row(a_log), dtb=pad_row(dt_bias),
        norm_w=gdn_norm_w.astype(F32)[None, :], wo=wo.astype(BF16),
        ln1_g=ln1_g[None, :], ln1_b=ln1_b[None, :], wr=jnp.stack([wr_hi, wr_mid]),
        w_gate=w_gate, w_up=w_up, w_down=w_down, ln2_g=ln2_g[None, :], ln2_b=ln2_b[None, :])


def _layer(x_prompt, x_sample, cache_k, cache_v, state_gdn, state_conv, wts):
    bp, sp, _ = x_prompt.shape
    bs, ts, _ = x_sample.shape
    n_p = bp * sp

    xp = x_prompt.reshape(n_p, D_MODEL)
    (q, k, v, qg, kg, vg, z, gcb, utail) = _proj(xp, jnp.arange(sp, dtype=I32), wts, GDN_CHUNK, bp)
    attn_p = _attn_prompt(q, k, v, wts['sinks'], bp)
    gdn_p, s_p = _gdn_prompt(qg, kg, vg, z, gcb, wts['norm_w'], bp)
    new_k_p = k.reshape(bp, sp, ATTN_KV_HEADS, HEAD_DIM)[:, sp - WINDOW:]
    new_v_p = v.reshape(bp, sp, ATTN_KV_HEADS, HEAD_DIM)[:, sp - WINDOW:]
    tiles_per_seq = sp // min(PROJ_TILE, sp)
    conv_p = utail.reshape(bp, tiles_per_seq, SUBLANES, CONV_DIM)[:, -1, SUBLANES - (CONV_W - 1):]

    lo, hi = SAMPLE_FIRST, SAMPLE_FIRST + ts
    xs_rows = jnp.pad(x_sample, ((0, 0), (lo, SAMPLE_SLOTS - hi), (0, 0))).reshape(bs * SAMPLE_SLOTS, D_MODEL)
    hist = jnp.pad(state_conv, ((0, 0), (0, SAMPLE_SLOTS - lo), (0, 0))).reshape(bs * SAMPLE_SLOTS, CONV_DIM)
    slot = jnp.arange(SAMPLE_SLOTS, dtype=I32)
    valid = jnp.tile(((slot >= lo) & (slot < hi)).astype(F32), bs)[:, None]
    pos_s = jnp.tile(PAST_LEN + slot - lo, bs)
    (q, k, v, qg, kg, vg, z, gcb, u_s) = _proj(xs_rows, pos_s, wts, SAMPLE_SLOTS, 1, hist, valid)
    ck = cache_k.reshape(bs, WINDOW, KV_COLS)
    cv = cache_v.reshape(bs, WINDOW, KV_COLS)
    attn_s, kwin, vwin = _attn_sample(q, k, v, ck, cv, wts['sinks'], bs, ts)
    gdn_s, s_s = _gdn_sample(qg, kg, vg, z, gcb, wts['norm_w'], state_gdn)
    real = lambda a: a.reshape(bs, SAMPLE_SLOTS, -1)[:, lo:hi]
    new_k_s = kwin.reshape(bs, WINDOW, ATTN_KV_HEADS, HEAD_DIM)
    new_v_s = vwin.reshape(bs, WINDOW, ATTN_KV_HEADS, HEAD_DIM)
    conv_s = u_s.reshape(bs, SAMPLE_SLOTS, CONV_DIM)[:, hi - (CONV_W - 1):hi]

    x1_p, route_p = _post(attn_p, gdn_p, xp, wts)
    x1_s, route_s = _post(real(attn_s).reshape(bs * ts, Q_COLS), real(gdn_s).reshape(bs * ts, Z_COLS),
                          x_sample.reshape(bs * ts, D_MODEL), wts)
    y_p, y_s = _moe([x1_p, x1_s], [route_p, route_s], wts)
    return (y_p.reshape(bp, sp, D_MODEL), y_s.reshape(bs, ts, D_MODEL), new_k_p, new_v_p, s_p, conv_p,
            new_k_s, new_v_s, s_s, conv_s)


def kernel(x_prompt, x_sample, cache_attn_k, cache_attn_v, state_gdn, state_conv, w_in, w_out,
           attn_sinks, conv_w, a_log, dt_bias, gdn_norm_w, ln1_g, ln1_b, w_router_group,
           w_router_expert, w_gate, w_up, w_down, ln2_g, ln2_b):
    assert w_in.shape[0] == DEPTH
    l = 0
    wts = _prep_weights(w_in[l], w_out[l], conv_w[l], a_log[l], dt_bias[l], gdn_norm_w[l], ln1_g[l],
                        ln1_b[l], w_router_group[l], w_router_expert[l], w_gate[l], w_up[l],
                        w_down[l], ln2_g[l], ln2_b[l])
    wts['sinks'] = attn_sinks[l]
    outs = _layer(x_prompt, x_sample, cache_attn_k[l], cache_attn_v[l], state_gdn[l], state_conv[l], wts)
    (y_p, y_s, k_p, v_p, s_p, c_p, k_s, v_s, s_s, c_s) = outs
    add = lambda a: a[None]
    return (y_p, y_s, add(k_p), add(v_p), add(s_p), add(c_p), add(k_s), add(v_s), add(s_s), add(c_s))
```

```python
import functools
import math

import jax
import jax.numpy as jnp
from jax import lax
from jax.experimental import pallas as pl
from jax.experimental.pallas import tpu as pltpu

F32 = jnp.float32
BF16 = jnp.bfloat16
I32 = jnp.int32

D_MODEL = 1024
ATTN_HEADS = 8
ATTN_KV_HEADS = 2
HEAD_DIM = 64
WINDOW = 128
ROT_DIM = HEAD_DIM // 4
ROPE_THETA = 500000.0
GDN_HEADS = 4
GDN_DK = 128
GDN_DV = 128
CONV_W = 4
QK_COLS = GDN_HEADS * GDN_DK
CONV_DIM = 2 * QK_COLS + GDN_HEADS * GDN_DV
Z_COLS = GDN_HEADS * GDN_DV
Q_COLS = ATTN_HEADS * HEAD_DIM
KV_COLS = ATTN_KV_HEADS * HEAD_DIM
N_GROUPS = 4
EXPERTS_PER_GROUP = 8
N_EXPERTS = N_GROUPS * EXPERTS_PER_GROUP
TOP_K = 2
EXPERT_FF = 256
NORM_EPS = 1e-5
L2_EPS = 1e-6
DEPTH = 1
DEEPNORM_ALPHA = (2 * DEPTH) ** 0.25
PAST_LEN = 8192

LANES = 128
SUBLANES = 8
TOK_TILE = 256
PROJ_TILE = 512
PROJ_SUB = 128
GDN_CHUNK = 128
GDN_SEQ_PER_STEP = 4
ATTN_BLOCKS_PER_STEP = 4
ATTN_SEQS_PER_STEP = 8
INV_BASE = 16
SAMPLE_SLOTS = 8
SAMPLE_FIRST = CONV_W - 1
ROW_TILE = 512
EXPERT_SUB = 256
SLAB_UNROLL = 4
PERM_ROWS = TOP_K * TOK_TILE + N_EXPERTS * SUBLANES
PERM_SLABS = PERM_ROWS // SUBLANES
XS_WORDS = D_MODEL + LANES
ZERO_TABLE = N_EXPERTS * (ROW_TILE // SUBLANES)
VMEM_LIMIT = 48 * 1024 * 1024
NEG_BIG = -1e30


def _cparams(sem):
    return pltpu.CompilerParams(dimension_semantics=sem, vmem_limit_bytes=VMEM_LIMIT)


def _bdot(a, b):
    return jnp.dot(a.astype(BF16), b.astype(BF16), preferred_element_type=F32)


def _bdot_nt(a, b):
    return lax.dot_general(a.astype(BF16), b.astype(BF16), (((1,), (1,)), ((), ())),
                           preferred_element_type=F32)


def _bdot_tn(a, b):
    return lax.dot_general(a.astype(BF16), b.astype(BF16), (((0,), (0,)), ((), ())),
                           preferred_element_type=F32)


def _div_pow2(x, n):
    return jnp.right_shift(x, int(math.log2(n)))


def _mod_pow2(x, n):
    return jnp.bitwise_and(x, n - 1)


def _split3(x):
    hi = x.astype(BF16)
    r = x - hi.astype(F32)
    mid = r.astype(BF16)
    lo = (r - mid.astype(F32)).astype(BF16)
    return hi, mid, lo


def _dot_exact_lhs01(m01, x):
    hi, mid, lo = _split3(x)
    d = lambda t: jnp.dot(m01, t, preferred_element_type=F32)
    return d(hi) + d(mid) + d(lo)


def _sigmoid(x):
    return 1.0 / (1.0 + jnp.exp(-x))


def _silu(x):
    return x * _sigmoid(x)


def _softplus(x):
    return jnp.maximum(x, 0.0) + jnp.log1p(jnp.exp(-jnp.abs(x)))


def _layer_norm(h, g, b):
    mu = jnp.mean(h, axis=-1, keepdims=True)
    d = h - mu
    var = jnp.mean(d * d, axis=-1, keepdims=True)
    return d * lax.rsqrt(var + NORM_EPS) * g + b


def _proj_kernel(*refs, tm, has_hist, full_u, one_segment):
    it = iter(refs)
    x_ref, cos_ref, sin_ref = next(it), next(it), next(it)
    wqkv_ref, wg_ref, wz_ref, wab_ref = next(it), next(it), next(it), next(it)
    convw_ref, alog_ref, dtb_ref, tri_ref, seg_ref = next(it), next(it), next(it), next(it), next(it)
    hist_ref = valid_ref = None
    if has_hist:
        hist_ref, valid_ref = next(it), next(it)
    q_ref, k_ref, v_ref = next(it), next(it), next(it)
    qg_ref, kg_ref, vg_ref, z_ref, gcb_ref, u_ref = (next(it) for _ in range(6))
    ubuf = next(it)

    t = pl.program_id(1)
    sub = PROJ_SUB
    rows = [slice(j * sub, (j + 1) * sub) for j in range(tm // sub)]
    lane = lax.broadcasted_iota(I32, (sub, LANES), 1)
    first_half = _mod_pow2(lane, HEAD_DIM) < (ROT_DIM // 2)

    @pl.when(t == 0)
    def _():
        ubuf[0:SUBLANES, :] = jnp.zeros((SUBLANES, CONV_DIM), F32)

    @pl.when(t > 0)
    def _():
        ubuf[0:SUBLANES, :] = ubuf[tm:tm + SUBLANES, :]

    dots = []
    for r in rows:
        xb = x_ref[r, :].astype(BF16)
        dots.append([jnp.dot(xb, w[...], preferred_element_type=F32)
                     for w in (wqkv_ref, wg_ref, wz_ref, wab_ref)])

    def l2n(s):
        return s * lax.rsqrt(jnp.sum(s * s, axis=1, keepdims=True) + L2_EPS)

    for r, (pq, u, z, ab) in zip(rows, dots):
        cosv, sinv = cos_ref[r, :], sin_ref[r, :]

        def rope(s):
            sw = jnp.where(first_half, pltpu.roll(s, LANES - ROT_DIM // 2, axis=1),
                           pltpu.roll(s, ROT_DIM // 2, axis=1))
            return s * cosv + sw * sinv

        for j in range(Q_COLS // LANES):
            q_ref[r, j * LANES:(j + 1) * LANES] = rope(pq[:, j * LANES:(j + 1) * LANES])
        k_ref[r, :] = rope(pq[:, Q_COLS:Q_COLS + KV_COLS])
        v_ref[r, :] = pq[:, Q_COLS + KV_COLS:Q_COLS + 2 * KV_COLS]
        z_ref[r, :] = z

        if has_hist:
            u = u + hist_ref[r, :]
        if full_u:
            u_ref[r, :] = u
        elif r.stop == tm:
            u_ref[...] = u[sub - SUBLANES:, :]
        base = SUBLANES + r.start
        ubuf[base:base + sub, :] = u
        acc = u * convw_ref[CONV_W - 1:CONV_W, :]
        for j in range(1, CONV_W):
            acc = acc + ubuf[base - j:base - j + sub, :] * convw_ref[CONV_W - 1 - j:CONV_W - j, :]
        c = _silu(acc)
        if has_hist:
            c = c * valid_ref[r, :]
        for h in range(GDN_HEADS):
            sl = slice(h * GDN_DK, (h + 1) * GDN_DK)
            qg_ref[r, sl] = l2n(c[:, sl]) * (GDN_DK ** -0.5)
            kg_ref[r, sl] = l2n(c[:, QK_COLS + h * GDN_DK:QK_COLS + (h + 1) * GDN_DK])
        vg_ref[r, :] = c[:, 2 * QK_COLS:]

        g = -jnp.exp(alog_ref[...]) * _softplus(ab + dtb_ref[...])
        beta = _sigmoid(ab)
        if has_hist:
            g = g * valid_ref[r, :]
            beta = beta * valid_ref[r, :]
        g = jnp.where(lane < GDN_HEADS, g, 0.0)
        gc = _dot_exact_lhs01(tri_ref[...], g)
        if one_segment:
            gl = jnp.broadcast_to(gc[sub - 1:sub, :], (sub, LANES))
        else:
            gl = _dot_exact_lhs01(seg_ref[...], g)
        gcb_ref[r, :] = jnp.where(lane < GDN_HEADS, gc,
                                  jnp.where(lane < 2 * GDN_HEADS, beta,
                                            jnp.where(lane < 3 * GDN_HEADS,
                                                      pltpu.roll(gl, 2 * GDN_HEADS, axis=1), 0.0)))


def _rope_tables(pos):
    half = ROT_DIM // 2
    inv_freq = ROPE_THETA ** (-jnp.arange(half, dtype=F32) * 2.0 / ROT_DIM)
    ang = pos.astype(F32)[:, None] * inv_freq[None, :]
    cos, sin = jnp.cos(ang), jnp.sin(ang)
    p = pos.shape[0]
    cpat = jnp.concatenate([cos, cos, jnp.ones((p, HEAD_DIM - ROT_DIM), F32)], axis=1)
    spat = jnp.concatenate([-sin, sin, jnp.zeros((p, HEAD_DIM - ROT_DIM), F32)], axis=1)
    return jnp.tile(cpat, (1, LANES // HEAD_DIM)), jnp.tile(spat, (1, LANES // HEAD_DIM))


def _segment_matrices(tm, seg_len):
    i = jnp.arange(tm)
    same = (i[:, None] // seg_len) == (i[None, :] // seg_len)
    tri = same & (i[None, :] <= i[:, None])
    return tri.astype(BF16), same.astype(BF16)


def _proj(x, pos, wts, seg_len, n_seq, hist=None, valid=None):
    n = x.shape[0]
    rows = n // n_seq
    tm = min(PROJ_TILE, rows)
    nt = rows // tm
    has_hist = hist is not None
    cos_t, sin_t = _rope_tables(pos)
    tri, seg = _segment_matrices(PROJ_SUB, seg_len)

    tok = lambda w: pl.BlockSpec((tm, w), lambda b, t: (b * nt + t, 0))
    const = lambda a: pl.BlockSpec(a.shape, lambda b, t: (0,) * a.ndim)
    in_arrays = [x, cos_t, sin_t, wts['wqkv'], wts['wg'], wts['wz'], wts['wab'],
                 wts['convw'], wts['alog'], wts['dtb'], tri, seg]
    in_specs = [tok(D_MODEL), pl.BlockSpec((tm, LANES), lambda b, t: (t, 0)),
                pl.BlockSpec((tm, LANES), lambda b, t: (t, 0))] + [const(a) for a in in_arrays[3:]]
    if has_hist:
        in_arrays += [hist, valid]
        in_specs += [tok(CONV_DIM), tok(1)]
    u_rows = n if has_hist else (n // tm) * SUBLANES
    u_block = tm if has_hist else SUBLANES
    out_shape = [jax.ShapeDtypeStruct((n, Q_COLS), F32), jax.ShapeDtypeStruct((n, KV_COLS), F32),
                 jax.ShapeDtypeStruct((n, KV_COLS), F32), jax.ShapeDtypeStruct((n, QK_COLS), F32),
                 jax.ShapeDtypeStruct((n, QK_COLS), F32), jax.ShapeDtypeStruct((n, Z_COLS), F32),
                 jax.ShapeDtypeStruct((n, Z_COLS), F32), jax.ShapeDtypeStruct((n, LANES), F32),
                 jax.ShapeDtypeStruct((u_rows, CONV_DIM), F32)]
    out_specs = [tok(Q_COLS), tok(KV_COLS), tok(KV_COLS), tok(QK_COLS), tok(QK_COLS), tok(Z_COLS),
                 tok(Z_COLS), tok(LANES),
                 pl.BlockSpec((u_block, CONV_DIM), lambda b, t: (b * nt + t, 0))]
    return pl.pallas_call(
        functools.partial(_proj_kernel, tm=tm, has_hist=has_hist, full_u=has_hist,
                          one_segment=seg_len == PROJ_SUB),
        out_shape=out_shape, grid=(n_seq, nt), in_specs=in_specs, out_specs=out_specs,
        scratch_shapes=[pltpu.VMEM((tm + SUBLANES, CONV_DIM), F32)],
        compiler_params=_cparams(("arbitrary", "arbitrary")),
        name="proj_hist" if has_hist else "proj",
    )(*in_arrays)


def _attn_blocks(qs, kcats, vcats, biases, sink, tq):
    lane = lax.broadcasted_iota(I32, (tq, LANES), 1)
    low = lane < HEAD_DIM
    n_slab = Q_COLS // LANES

    def stack(q):
        slabs = [q[:, j * LANES:(j + 1) * LANES] * (HEAD_DIM ** -0.5) for j in range(n_slab)]
        parts = ([jnp.where(low, s, 0.0) for s in slabs] + [jnp.where(low, 0.0, s) for s in slabs])
        return jnp.concatenate(parts, axis=0).astype(BF16)

    def unstack(o8):
        return [jnp.where(low, o8[j * tq:(j + 1) * tq, :], o8[(n_slab + j) * tq:(n_slab + j + 1) * tq, :])
                for j in range(n_slab)]

    rows = ATTN_HEADS * tq
    half = rows // 2
    klow = lax.broadcasted_iota(I32, (2 * WINDOW, LANES), 1) < HEAD_DIM
    one = jnp.ones((), BF16)
    q8s = _each(stack, qs)
    ss = _each(lambda q8, kc, b: _bdot_nt(q8, kc) + b, q8s, kcats, biases)
    ms = _each(lambda s: jnp.maximum(jnp.broadcast_to(jnp.max(s, axis=1, keepdims=True), (rows, LANES)),
                                     sink), ss)
    ps = _each(lambda s, m: jnp.exp(s - jnp.concatenate([m, m], axis=1)).astype(BF16), ss, ms)
    pv0 = _each(lambda p, vc: jnp.dot(p[:half], jnp.where(klow, vc, one), preferred_element_type=F32),
                ps, vcats)
    pv1 = _each(lambda p, vc: jnp.dot(p[half:], jnp.where(klow, one, vc), preferred_element_type=F32),
                ps, vcats)
    pvs = _each(lambda a, b: jnp.concatenate([a, b], axis=0), pv0, pv1)
    o8s = _each(lambda pv, m: pv / (pltpu.roll(pv, HEAD_DIM, axis=1) + jnp.exp(sink - m)), pvs, ms)
    return _each(unstack, o8s)


def _attn_prompt_kernel(q_ref, kc_ref, vc_ref, kp_ref, vp_ref, bias0_ref, bias_ref, sink_ref, o_ref, *,
                        nblk):
    kall = jnp.concatenate([kp_ref[...], kc_ref[...]], axis=0).astype(BF16)
    vall = jnp.concatenate([vp_ref[...], vc_ref[...]], axis=0).astype(BF16)
    win = lambda a, j: a[j * WINDOW:(j + 2) * WINDOW, :]
    qs = [q_ref[j * WINDOW:(j + 1) * WINDOW, :] for j in range(nblk)]
    biases = [bias0_ref[0]] + [bias_ref[...]] * (nblk - 1)
    outs = _attn_blocks(qs, [win(kall, j) for j in range(nblk)], [win(vall, j) for j in range(nblk)],
                        biases, sink_ref[...], WINDOW)
    for j, slabs in enumerate(outs):
        for c, slab in enumerate(slabs):
            o_ref[j * WINDOW:(j + 1) * WINDOW, c * LANES:(c + 1) * LANES] = slab


def _attn_sample_kernel(q_ref, kc_ref, vc_ref, kp_ref, vp_ref, bias_ref, sink_ref, o_ref, kw_ref, vw_ref,
                        *, nseq, n_new):
    tq = SAMPLE_SLOTS
    zpad = jnp.zeros((WINDOW - tq, LANES), F32)
    rows = lambda ref, j: ref[j * tq:(j + 1) * tq, :]
    cat = lambda pref, cref, j: jnp.concatenate([pref[j], rows(cref, j), zpad], axis=0).astype(BF16)
    outs = _attn_blocks([rows(q_ref, j) for j in range(nseq)],
                        [cat(kp_ref, kc_ref, j) for j in range(nseq)],
                        [cat(vp_ref, vc_ref, j) for j in range(nseq)],
                        [bias_ref[...]] * nseq, sink_ref[...], tq)
    for j, slabs in enumerate(outs):
        for c, slab in enumerate(slabs):
            o_ref[j * tq:(j + 1) * tq, c * LANES:(c + 1) * LANES] = slab
    row = lax.broadcasted_iota(I32, (WINDOW, LANES), 0)
    keep = WINDOW - n_new
    for pref, cref, wref in ((kp_ref, kc_ref, kw_ref), (vp_ref, vc_ref, vw_ref)):
        for j in range(nseq):
            new = jnp.concatenate([rows(cref, j), zpad], axis=0)
            wref[j] = jnp.where(row < keep, pltpu.roll(pref[j], keep, axis=0),
                                pltpu.roll(new, keep - SAMPLE_FIRST, axis=0))


def _sink_rows(sinks, tq):
    return jnp.broadcast_to(jnp.repeat(sinks.astype(F32), tq)[:, None], (ATTN_HEADS * tq, LANES))


def _attn_bias(tq, q_off, k_lo, k_hi, has_prev):
    qi = (jnp.arange(ATTN_HEADS * tq, dtype=I32) % tq)[:, None]
    c = jnp.arange(2 * WINDOW, dtype=I32)[None, :]
    cj = c - WINDOW
    vis_prev = (c < WINDOW) & (c > qi - q_off) & has_prev
    vis_cur = (c >= WINDOW) & (cj <= qi) & (cj >= k_lo) & (cj <= k_hi)
    return jnp.where(vis_prev | vis_cur, 0.0, NEG_BIG).astype(F32)


def _attn_prompt(q, k, v, sinks, n_seq):
    n = q.shape[0]
    nb = n // n_seq // WINDOW
    nblk = min(ATTN_BLOCKS_PER_STEP, nb)
    steps = nb // nblk
    tq = nblk * WINDOW
    cur = lambda w: pl.BlockSpec((tq, w), lambda b, i: (b * steps + i, 0))
    prev = pl.BlockSpec((WINDOW, LANES), lambda b, i: (b * nb + jnp.maximum(i * nblk - 1, 0), 0))
    bias2 = jnp.stack([_attn_bias(WINDOW, 0, 0, WINDOW - 1, False),
                       _attn_bias(WINDOW, 0, 0, WINDOW - 1, True)])
    rows = ATTN_HEADS * WINDOW
    return pl.pallas_call(
        functools.partial(_attn_prompt_kernel, nblk=nblk),
        out_shape=jax.ShapeDtypeStruct((n, Q_COLS), F32), grid=(n_seq, steps),
        in_specs=[cur(Q_COLS), cur(LANES), cur(LANES), prev, prev,
                  pl.BlockSpec((1, rows, 2 * WINDOW), lambda b, i: (jnp.minimum(i, 1), 0, 0)),
                  pl.BlockSpec((rows, 2 * WINDOW), lambda b, i: (0, 0)),
                  pl.BlockSpec((rows, LANES), lambda b, i: (0, 0))],
        out_specs=cur(Q_COLS),
        compiler_params=_cparams(("arbitrary", "arbitrary")), name="attn_prompt",
    )(q, k, v, k, v, bias2, bias2[1], _sink_rows(sinks, WINDOW))


def _attn_sample(q, k, v, cache_k, cache_v, sinks, n_seq, n_new):
    tq = SAMPLE_SLOTS
    nseq = min(ATTN_SEQS_PER_STEP, n_seq)
    cur = lambda w: pl.BlockSpec((nseq * tq, w), lambda b: (b, 0))
    prev = pl.BlockSpec((nseq, WINDOW, LANES), lambda b: (b, 0, 0))
    bias = _attn_bias(tq, SAMPLE_FIRST, SAMPLE_FIRST, SAMPLE_FIRST + 3, True)
    win = jax.ShapeDtypeStruct((n_seq, WINDOW, LANES), F32)
    return pl.pallas_call(
        functools.partial(_attn_sample_kernel, nseq=nseq, n_new=n_new),
        out_shape=[jax.ShapeDtypeStruct((n_seq * tq, Q_COLS), F32), win, win], grid=(n_seq // nseq,),
        in_specs=[cur(Q_COLS), cur(LANES), cur(LANES), prev, prev,
                  pl.BlockSpec(bias.shape, lambda b: (0, 0)),
                  pl.BlockSpec((ATTN_HEADS * tq, LANES), lambda b: (0, 0))],
        out_specs=[cur(Q_COLS), prev, prev],
        compiler_params=_cparams(("arbitrary",)), name="attn_sample",
    )(q, k, v, cache_k, cache_v, bias, _sink_rows(sinks, tq))


def _each(f, *lists):
    return [f(*args) for args in zip(*lists)]


def _unit_lower_inverse(ms, eye, same_base):
    c = ms[0].shape[0]

    def neumann(q0s, n_factors):
        xs = _each(lambda q: eye + q, q0s)
        if n_factors == 1:
            return xs
        qs = _each(_bdot, q0s, q0s)
        for _ in range(n_factors - 2):
            prods = _each(lambda x, q: _bdot(jnp.concatenate([x, q], axis=0), q), xs, qs)
            xs = _each(lambda x, pr: x + pr[:c], xs, prods)
            qs = _each(lambda pr: pr[c:], prods)
        return _each(lambda x, q: x + _bdot(x, q), xs, qs)

    ds = _each(lambda m: jnp.where(same_base, m, 0.0), ms)
    xs = neumann(_each(lambda d: -d, ds), int(math.log2(INV_BASE)))
    nblk = c // INV_BASE
    if nblk == 1:
        return xs
    ls = _each(lambda m, d: m - d, ms, ds)
    ns = _each(lambda x, l: -_bdot(x, l), xs, ls)
    ys = neumann(ns, int(math.log2(nblk)))
    return _each(_bdot, ys, xs)


def _gdn_intra(qs, ks, vs, gcs, gls, betas, same_seq, low_incl, low_strict, eye, same_base):
    del same_seq
    e_gcs = _each(jnp.exp, gcs)

    def decay_of(gc):
        gc_row = jnp.sum(jnp.where(eye > 0, gc, 0.0), axis=0, keepdims=True)
        return jnp.where(low_incl, jnp.exp(jnp.where(low_incl, gc - gc_row, 0.0)), 0.0)

    c = qs[0].shape[0]
    decays = _each(decay_of, gcs)
    kbs = _each(lambda k, b: k * b, ks, betas)
    vbs = _each(lambda v, b: v * b, vs, betas)
    kqs = _each(lambda kb, q, k: _bdot_nt(jnp.concatenate([kb, q], axis=0), k), kbs, qs, ks)
    ms = _each(lambda kq, d: jnp.where(low_strict, kq[:c] * d, 0.0), kqs, decays)
    attns = _each(lambda kq, d: jnp.where(low_incl, kq[c:] * d, 0.0), kqs, decays)
    tmats = _unit_lower_inverse(ms, eye, same_base)
    uws = _each(lambda t, vb, kb, e: _bdot(t, jnp.concatenate([vb, kb * e], axis=1)),
                tmats, vbs, kbs, e_gcs)
    us = _each(lambda uw: uw[:, :GDN_DV], uws)
    ws = _each(lambda uw: uw[:, GDN_DV:], uws)
    q_decs = _each(lambda q, e: q * e, qs, e_gcs)
    k_decs = _each(lambda k, gl, gc: k * jnp.exp(gl - gc), ks, gls, gcs)
    return us, ws, attns, q_decs, k_decs


def _chunk_masks(c, seq_len):
    i = lax.broadcasted_iota(I32, (c, c), 0)
    j = lax.broadcasted_iota(I32, (c, c), 1)
    same_seq = _div_pow2(i, seq_len) == _div_pow2(j, seq_len)
    low_incl = same_seq & (i >= j)
    low_strict = same_seq & (i > j)
    eye = (i == j).astype(F32)
    same_base = _div_pow2(i, INV_BASE) == _div_pow2(j, INV_BASE)
    return same_seq, low_incl, low_strict, eye, same_base


def _gated_rms(o, z, nw):
    o = o * lax.rsqrt(jnp.mean(o * o, axis=1, keepdims=True) + NORM_EPS) * nw
    return o * _silu(z)


def _gdn_prompt_kernel(qg_ref, kg_ref, vg_ref, z_ref, gcb_ref, nw_ref, o_ref, s_out_ref, s_scr):
    c = GDN_CHUNK
    n = pl.program_id(1)

    @pl.when(n == 0)
    def _():
        s_scr[...] = jnp.zeros_like(s_scr)

    masks = _chunk_masks(c, c)
    nw = nw_ref[...]
    chains = [(b, h) for b in range(qg_ref.shape[0]) for h in range(GDN_HEADS)]
    hs = lambda h: slice(h * GDN_DK, (h + 1) * GDN_DK)
    col = lambda off: [gcb_ref[b, :, off + h:off + h + 1] for b, h in chains]
    gcs, betas, gls = col(0), col(GDN_HEADS), col(2 * GDN_HEADS)
    qs = [qg_ref[b, :, hs(h)] for b, h in chains]
    ks = [kg_ref[b, :, hs(h)] for b, h in chains]
    vs = [vg_ref[b, :, hs(h)] for b, h in chains]
    us, ws, attns, q_decs, k_decs = _gdn_intra(qs, ks, vs, gcs, gls, betas, *masks)
    ss = [s_scr[b, h] for b, h in chains]
    wqs = _each(lambda w, qd, s: _bdot(jnp.concatenate([w, qd], axis=0), s), ws, q_decs, ss)
    wss = _each(lambda wq: wq[:c], wqs)
    qss = _each(lambda wq: wq[c:], wqs)
    v_news = _each(lambda u, x: u - x, us, wss)
    avs = _each(_bdot, attns, v_news)
    kvs = _each(_bdot_tn, k_decs, v_news)
    for (b, h), s, gl, qsv, av, kv in zip(chains, ss, gls, qss, avs, kvs):
        s_scr[b, h] = s * jnp.exp(gl[0:1, :]) + kv
        o_ref[b, :, hs(h)] = _gated_rms(qsv + av, z_ref[b, :, hs(h)], nw)

    @pl.when(n == pl.num_programs(1) - 1)
    def _():
        s_out_ref[...] = s_scr[...]


def _gdn_prompt(qg, kg, vg, z, gcb, norm_w, n_seq):
    n = qg.shape[0]
    s_len = n // n_seq
    nb = min(GDN_SEQ_PER_STEP, n_seq)
    v3 = lambda a: a.reshape(n_seq, s_len, a.shape[-1])
    tok = lambda w: pl.BlockSpec((nb, GDN_CHUNK, w), lambda b, i: (b, i, 0))
    o, s = pl.pallas_call(
        _gdn_prompt_kernel,
        out_shape=[jax.ShapeDtypeStruct((n_seq, s_len, Z_COLS), F32),
                   jax.ShapeDtypeStruct((n_seq, GDN_HEADS, GDN_DK, GDN_DV), F32)],
        grid=(n_seq // nb, s_len // GDN_CHUNK),
        in_specs=[tok(QK_COLS), tok(QK_COLS), tok(Z_COLS), tok(Z_COLS), tok(LANES),
                  pl.BlockSpec((1, GDN_DV), lambda b, i: (0, 0))],
        out_specs=[tok(Z_COLS),
                   pl.BlockSpec((nb, GDN_HEADS, GDN_DK, GDN_DV), lambda b, i: (b, 0, 0, 0))],
        scratch_shapes=[pltpu.VMEM((nb, GDN_HEADS, GDN_DK, GDN_DV), F32)],
        compiler_params=_cparams(("arbitrary", "arbitrary")), name="gdn_prompt",
    )(v3(qg), v3(kg), v3(vg), v3(z), v3(gcb), norm_w)
    return o.reshape(n, Z_COLS), s


def _gdn_sample_kernel(qg_ref, kg_ref, vg_ref, z_ref, gcb_ref, nw_ref, s_in_ref, o_ref, s_out_ref):
    c = GDN_CHUNK
    n_sub = c // SAMPLE_SLOTS
    masks = _chunk_masks(c, SAMPLE_SLOTS)
    heads = range(GDN_HEADS)
    hs = lambda h: slice(h * GDN_DK, (h + 1) * GDN_DK)
    rs = lambda s: slice(s * SAMPLE_SLOTS, (s + 1) * SAMPLE_SLOTS)
    col = lambda off: [gcb_ref[:, off + h:off + h + 1] for h in heads]
    gcs, betas, gls = col(0), col(GDN_HEADS), col(2 * GDN_HEADS)
    us, ws, attns, q_decs, k_decs = _gdn_intra([qg_ref[:, hs(h)] for h in heads],
                                               [kg_ref[:, hs(h)] for h in heads],
                                               [vg_ref[:, hs(h)] for h in heads], gcs, gls, betas, *masks)
    pairs = [(h, s) for h in heads for s in range(n_sub)]
    sts = [s_in_ref[s, h] for h, s in pairs]
    boths = [jnp.concatenate([ws[h][rs(s), :], q_decs[h][rs(s), :]], axis=0) for h, s in pairs]
    rr = _each(_bdot, boths, sts)
    gather = lambda h, part: jnp.concatenate(
        [rr[h * n_sub + s][part * SAMPLE_SLOTS:(part + 1) * SAMPLE_SLOTS, :] for s in range(n_sub)], axis=0)
    v_news = [us[h] - gather(h, 0) for h in heads]
    avs = _each(_bdot, attns, v_news)
    row = lax.broadcasted_iota(I32, (c, LANES), 0)
    seq_of_row = _div_pow2(row, SAMPLE_SLOTS)
    kds = [jnp.where(seq_of_row == s, k_decs[h], 0.0) for h, s in pairs]
    kvs = _each(_bdot_tn, kds, [v_news[h] for h, _ in pairs])
    egls = _each(jnp.exp, gls)
    for (h, s), st, kv in zip(pairs, sts, kvs):
        s_out_ref[s, h] = st * egls[h][s * SAMPLE_SLOTS:s * SAMPLE_SLOTS + 1, :] + kv
    nw = nw_ref[...]
    for h in heads:
        o_ref[:, hs(h)] = _gated_rms(gather(h, 1) + avs[h], z_ref[:, hs(h)], nw)


def _gdn_sample(qg, kg, vg, z, gcb, norm_w, state):
    n = qg.shape[0]
    n_sub = GDN_CHUNK // SAMPLE_SLOTS
    tok = lambda w: pl.BlockSpec((GDN_CHUNK, w), lambda i: (i, 0))
    st = pl.BlockSpec((n_sub, GDN_HEADS, GDN_DK, GDN_DV), lambda i: (i, 0, 0, 0))
    return pl.pallas_call(
        _gdn_sample_kernel,
        out_shape=[jax.ShapeDtypeStruct((n, Z_COLS), F32),
                   jax.ShapeDtypeStruct(state.shape, F32)],
        grid=(n // GDN_CHUNK,),
        in_specs=[tok(QK_COLS), tok(QK_COLS), tok(Z_COLS), tok(Z_COLS), tok(LANES),
                  pl.BlockSpec((1, GDN_DV), lambda i: (0, 0)), st],
        out_specs=[tok(Z_COLS), st],
        compiler_params=_cparams(("arbitrary",)), name="gdn_sample",
    )(qg, kg, vg, z, gcb, norm_w, state)


def _post_kernel(a_ref, g_ref, x_ref, wo_ref, ln_g_ref, ln_b_ref, wr_ref, x1_ref, route_ref, *, tm):
    sub = PROJ_SUB
    rows = [slice(j * sub, (j + 1) * sub) for j in range(tm // sub)]
    d = lambda a, b: jnp.dot(a, b, preferred_element_type=F32)
    mixes = [d(a_ref[r, :].astype(BF16), wo_ref[0:Q_COLS, :]) + d(g_ref[r, :].astype(BF16), wo_ref[Q_COLS:, :])
             for r in rows]
    x1s = [_layer_norm(DEEPNORM_ALPHA * x_ref[r, :] + mix, ln_g_ref[...], ln_b_ref[...])
           for r, mix in zip(rows, mixes)]
    for r, x1 in zip(rows, x1s):
        x1_ref[r, :] = x1
    wh, wm = wr_ref[0], wr_ref[1]
    lgs = []
    for x1 in x1s:
        xh = x1.astype(BF16)
        xm = (x1 - xh.astype(F32)).astype(BF16)
        lgs.append(d(xh, wh) + d(xh, wm) + d(xm, wh))
    for r, lg in zip(rows, lgs):
        route_ref[:, r] = jnp.transpose(_route(lg))[0:SUBLANES, :]


def _route(lg):
    lane = lax.broadcasted_iota(I32, lg.shape, 1)
    lane_f = lane.astype(F32)
    big = float(LANES)

    def first_max(vals, mask):
        v = jnp.where(mask, vals, NEG_BIG)
        mx = jnp.max(v, axis=1, keepdims=True)
        idx = jnp.min(jnp.where(mask & (v == mx), lane_f, big), axis=1, keepdims=True)
        return mx, idx

    gmask = lane < N_GROUPS
    gmax, gidx = first_max(lg, gmask)
    gden = jnp.sum(jnp.where(gmask, jnp.exp(lg - gmax), 0.0), axis=1, keepdims=True)
    g_top_p = 1.0 / gden
    e_lane = lane - N_GROUPS
    e_group = _div_pow2(jnp.maximum(e_lane, 0), EXPERTS_PER_GROUP).astype(F32)
    emask = (e_lane >= 0) & (e_lane < N_EXPERTS) & (e_group == gidx)
    m1, i1 = first_max(lg, emask)
    eden = jnp.sum(jnp.where(emask, jnp.exp(lg - m1), 0.0), axis=1, keepdims=True)
    m2, i2 = first_max(lg, emask & (lane_f != i1))
    p1 = 1.0 / eden
    p2 = jnp.exp(m2 - m1) / eden
    tot = p1 + p2
    gate1 = g_top_p * (p1 / tot)
    gate2 = g_top_p * (p2 / tot)
    return jnp.where(lane == 0, gate1,
                     jnp.where(lane == 1, gate2,
                               jnp.where(lane == 2, i1 - N_GROUPS,
                                         jnp.where(lane == 3, i2 - N_GROUPS, 0.0))))


def _post(attn_o, gdn_o, x, wts):
    n = x.shape[0]
    tm = min(PROJ_TILE, n)
    tok = lambda w: pl.BlockSpec((tm, w), lambda i: (i, 0))
    const = lambda a: pl.BlockSpec(a.shape, lambda i: (0,) * a.ndim)
    consts = [wts['wo'], wts['ln1_g'], wts['ln1_b'], wts['wr']]
    return pl.pallas_call(
        functools.partial(_post_kernel, tm=tm),
        out_shape=[jax.ShapeDtypeStruct((n, D_MODEL), F32), jax.ShapeDtypeStruct((SUBLANES, n), F32)],
        grid=(n // tm,),
        in_specs=[tok(Q_COLS), tok(Z_COLS), tok(D_MODEL)] + [const(a) for a in consts],
        out_specs=[tok(D_MODEL), pl.BlockSpec((SUBLANES, tm), lambda i: (0, i))],
        compiler_params=_cparams(("arbitrary",)), name="post_%d" % (n // tm),
    )(attn_o, gdn_o, x, *consts)


def _slab_loop(n, body):
    n_main = jnp.right_shift(n, int(math.log2(SLAB_UNROLL)))

    def main(i, c):
        for u in range(SLAB_UNROLL):
            body(i * SLAB_UNROLL + u, u)
        return c

    lax.fori_loop(0, n_main, main, 0)
    lax.fori_loop(n_main * SLAB_UNROLL, n, lambda j, c: (body(j, 0), c)[1], 0)


def _dispatch_kernel(dst_ref, nslab_ref, ztab_ref, zinfo_ref, slot_ref, gate_ref, *rest,
                     group_tiles, max_tiles):
    x_refs = rest[:len(group_tiles)]
    xs_ref, pbuf, sem, zbuf, zsem = rest[len(group_tiles):]
    n_tiles = sum(group_tiles)
    g = pl.program_id(0)
    cur = lax.rem(g, 2)

    def slab_copy(tile, buf_slot, j):
        d = pl.multiple_of(dst_ref[tile * PERM_SLABS + j], SUBLANES)
        src = pbuf.at[buf_slot, pl.ds(pl.multiple_of(j * SUBLANES, SUBLANES), SUBLANES), :]
        return pltpu.make_async_copy(src, xs_ref.at[pl.ds(d, SUBLANES), :], sem.at[buf_slot])

    def tail_copy(k):
        d = pl.multiple_of(ztab_ref[k], SUBLANES)
        return pltpu.make_async_copy(zbuf.at[pl.ds(0, SUBLANES), :], xs_ref.at[pl.ds(d, SUBLANES), :], zsem)

    def tile_copy(t):
        d = pl.multiple_of(t * ROW_TILE, ROW_TILE)
        return pltpu.make_async_copy(zbuf, xs_ref.at[pl.ds(d, ROW_TILE), :], zsem)

    @pl.when(g == 0)
    def _():
        zbuf[...] = jnp.zeros_like(zbuf)
        _slab_loop(zinfo_ref[0], lambda k, u: tail_copy(k).start(priority=1))
        lax.fori_loop(zinfo_ref[1], max_tiles, lambda t, c: (tile_copy(t).start(priority=1), c)[1], 0)

    x = x_refs[-1][...]
    bound = n_tiles
    for x_ref, nt in zip(x_refs[-2::-1], group_tiles[:0:-1]):
        bound -= nt
        x = jnp.where(g < bound, x_ref[...], x)

    r = lax.broadcasted_iota(I32, (PERM_ROWS, TOK_TILE), 0)
    sl = slot_ref[0]
    hit0, hit1 = r == sl[0:1, :], r == sl[1:2, :]
    onehot = jnp.where(hit0 | hit1, 1.0, 0.0).astype(BF16)
    gt = gate_ref[0]
    gcol = jnp.sum(jnp.where(hit0, gt[0:1, :], 0.0) + jnp.where(hit1, gt[1:2, :], 0.0),
                   axis=1, keepdims=True)
    pbuf[cur, :, 0:D_MODEL] = jnp.dot(onehot, x.astype(BF16), preferred_element_type=F32)
    pbuf[cur, :, D_MODEL:] = jnp.broadcast_to(gcol, (PERM_ROWS, LANES))

    @pl.when(g > 0)
    def _():
        _slab_loop(nslab_ref[g - 1], lambda j, u: slab_copy(g - 1, 1 - cur, j).wait())

    _slab_loop(nslab_ref[g], lambda j, u: slab_copy(g, cur, j).start())

    @pl.when(g == n_tiles - 1)
    def _():
        _slab_loop(nslab_ref[g], lambda j, u: slab_copy(g, cur, j).wait())
        _slab_loop(zinfo_ref[0], lambda k, u: tail_copy(k).wait())
        lax.fori_loop(zinfo_ref[1], max_tiles, lambda t, c: (tile_copy(t).wait(), c)[1], 0)


def _dispatch(plan, x1s, max_tiles):
    group_tiles = tuple(x1.shape[0] // TOK_TILE for x1 in x1s)
    n_tiles = sum(group_tiles)
    tile = lambda i, d, ns, zt, zi: (i, 0, 0)
    in_specs = [pl.BlockSpec((1, TOP_K, TOK_TILE), tile), pl.BlockSpec((1, TOP_K, TOK_TILE), tile)]
    base = 0
    for nt in group_tiles:
        in_specs.append(pl.BlockSpec(
            (TOK_TILE, D_MODEL),
            lambda i, d, ns, zt, zi, base=base, nt=nt: (jnp.clip(i - base, 0, nt - 1), 0)))
        base += nt
    return pl.pallas_call(
        functools.partial(_dispatch_kernel, group_tiles=group_tiles, max_tiles=max_tiles),
        out_shape=jax.ShapeDtypeStruct((max_tiles * ROW_TILE, XS_WORDS), F32),
        grid_spec=pltpu.PrefetchScalarGridSpec(
            num_scalar_prefetch=4, grid=(n_tiles,), in_specs=in_specs,
            out_specs=pl.BlockSpec(memory_space=pl.ANY),
            scratch_shapes=[pltpu.VMEM((2, PERM_ROWS, XS_WORDS), F32), pltpu.SemaphoreType.DMA((2,)),
                            pltpu.VMEM((ROW_TILE, XS_WORDS), F32), pltpu.SemaphoreType.DMA(())]),
        compiler_params=_cparams(("arbitrary",)), name="moe_dispatch",
    )(plan['slab_dst'], plan['nslab'], plan['ztab'], plan['zinfo'], plan['slot_rows'], plan['gate_rows'],
      *x1s)


def _expert_kernel(te_ref, nu_ref, xs_ref, wg_ref, wu_ref, wd_ref, ye_ref):
    del te_ref

    @pl.when(pl.program_id(0) < nu_ref[0])
    def _():
        sub = EXPERT_SUB
        rows = [slice(j * sub, (j + 1) * sub) for j in range(ROW_TILE // sub)]
        d = lambda a, b: jnp.dot(a, b, preferred_element_type=F32)
        wg, wu, wd = wg_ref[0].astype(BF16), wu_ref[0].astype(BF16), wd_ref[0].astype(BF16)
        xs = [xs_ref[r, 0:D_MODEL].astype(BF16) for r in rows]
        hgs = [d(x, wg) for x in xs]
        hus = [d(x, wu) for x in xs]
        hhs = [(_silu(hg) * hu).astype(BF16) for hg, hu in zip(hgs, hus)]
        ys = [d(hh, wd) for hh in hhs]
        for r, y in zip(rows, ys):
            gate = xs_ref[r, D_MODEL:]
            ye_ref[r, :] = y * jnp.concatenate([gate] * (D_MODEL // LANES), axis=1)

    @pl.when(pl.program_id(0) >= nu_ref[0])
    def _():
        ye_ref[...] = jnp.zeros_like(ye_ref)


def _experts(tile_expert, n_used, xs, w_gate, w_up, w_down):
    n_tiles = xs.shape[0] // ROW_TILE
    row_in = lambda i, te, nu: (jnp.minimum(i, nu[0] - 1), 0)
    row = lambda i, te, nu: (i, 0)
    wsel = lambda i, te, nu: (te[i], 0, 0)
    return pl.pallas_call(
        _expert_kernel,
        out_shape=jax.ShapeDtypeStruct((xs.shape[0], D_MODEL), F32),
        grid_spec=pltpu.PrefetchScalarGridSpec(
            num_scalar_prefetch=2, grid=(n_tiles,),
            in_specs=[pl.BlockSpec((ROW_TILE, XS_WORDS), row_in),
                      pl.BlockSpec((1, D_MODEL, EXPERT_FF), wsel),
                      pl.BlockSpec((1, D_MODEL, EXPERT_FF), wsel),
                      pl.BlockSpec((1, EXPERT_FF, D_MODEL), wsel)],
            out_specs=pl.BlockSpec((ROW_TILE, D_MODEL), row)),
        compiler_params=_cparams(("arbitrary",)), name="moe_experts",
    )(tile_expert, n_used, xs, w_gate, w_up, w_down)


def _combine_kernel(dst_ref, nslab_ref, x1_ref, slot_ref, ye_ref, ln_g_ref, ln_b_ref, y_ref,
                    buf, sem, *, tile_base, n_tiles):
    i = pl.program_id(0)
    g = tile_base + i
    cur = lax.rem(i, 2)

    def slab_copy(tile, buf_slot, j):
        d = pl.multiple_of(dst_ref[tile * PERM_SLABS + j], SUBLANES)
        dst = buf.at[buf_slot, pl.ds(pl.multiple_of(j * SUBLANES, SUBLANES), SUBLANES), :]
        return pltpu.make_async_copy(ye_ref.at[pl.ds(d, SUBLANES), :], dst, sem.at[buf_slot])

    @pl.when(i == 0)
    def _():
        buf[...] = jnp.zeros_like(buf)
        _slab_loop(nslab_ref[g], lambda j, u: slab_copy(g, cur, j).start(priority=u % 2))

    @pl.when(i + 1 < n_tiles)
    def _():
        _slab_loop(nslab_ref[g + 1], lambda j, u: slab_copy(g + 1, 1 - cur, j).start(priority=u % 2))

    _slab_loop(nslab_ref[g], lambda j, u: slab_copy(g, cur, j).wait())

    col = lax.broadcasted_iota(I32, (TOK_TILE, PERM_ROWS), 1)
    sl = slot_ref[0]
    diag = (lax.broadcasted_iota(I32, (TOK_TILE, TOK_TILE), 0)
            == lax.broadcasted_iota(I32, (TOK_TILE, TOK_TILE), 1))
    as_col = lambda row: jnp.sum(jnp.where(diag, row, 0), axis=1, keepdims=True)
    pick = jnp.where((col == as_col(sl[0:1, :])) | (col == as_col(sl[1:2, :])), 1.0, 0.0).astype(BF16)
    moe = jnp.dot(pick, buf[cur].astype(BF16), preferred_element_type=F32)
    y_ref[...] = _layer_norm(DEEPNORM_ALPHA * x1_ref[...] + moe, ln_g_ref[...], ln_b_ref[...])


def _combine(plan, tile_base, x1, ye, ln_g, ln_b):
    n = x1.shape[0]
    n_tiles = n // TOK_TILE
    tok = lambda w: pl.BlockSpec((TOK_TILE, w), lambda i, d, ns: (i, 0))
    const = lambda a: pl.BlockSpec(a.shape, lambda i, d, ns: (0,) * a.ndim)
    return pl.pallas_call(
        functools.partial(_combine_kernel, tile_base=tile_base, n_tiles=n_tiles),
        out_shape=jax.ShapeDtypeStruct((n, D_MODEL), F32),
        grid_spec=pltpu.PrefetchScalarGridSpec(
            num_scalar_prefetch=2, grid=(n_tiles,),
            in_specs=[tok(D_MODEL),
                      pl.BlockSpec((1, TOP_K, TOK_TILE), lambda i, d, ns: (tile_base + i, 0, 0)),
                      pl.BlockSpec(memory_space=pl.ANY), const(ln_g), const(ln_b)],
            out_specs=tok(D_MODEL),
            scratch_shapes=[pltpu.VMEM((2, PERM_ROWS, D_MODEL), F32), pltpu.SemaphoreType.DMA((2,))]),
        compiler_params=_cparams(("arbitrary",)), name="moe_combine_%d" % tile_base,
    )(plan['slab_dst'], plan['nslab'], x1, plan['slot_rows'], ye, ln_g, ln_b)


def _routing_plan(ids, gates):
    nt = ids.shape[1] // TOK_TILE
    pairs = TOP_K * TOK_TILE
    ex = jnp.arange(N_EXPERTS, dtype=I32)
    per_tile = lambda a: jnp.swapaxes(a.reshape(TOP_K, nt, TOK_TILE), 0, 1)
    flat = per_tile(ids).reshape(nt, pairs)
    onehot = (flat[:, None, :] == ex[None, :, None])
    p = jnp.arange(pairs, dtype=I32)
    triu = (p[:, None] <= p[None, :]).astype(BF16)
    csum = jnp.dot(onehot.astype(BF16).reshape(nt * N_EXPERTS, pairs), triu,
                   preferred_element_type=F32).astype(I32).reshape(nt, N_EXPERTS, pairs)
    oh = onehot.astype(I32)
    rank = jnp.sum(oh * (csum - 1), axis=1)
    cnt = csum[:, :, -1]
    cpad = (cnt + SUBLANES - 1) // SUBLANES * SUBLANES
    seg_end = jnp.cumsum(cpad, axis=1)
    seg_off = seg_end - cpad
    slot = jnp.sum(oh * seg_off[:, :, None], axis=1) + rank
    run_end = jnp.cumsum(cpad, axis=0)
    ntiles_e = (run_end[-1] + ROW_TILE - 1) // ROW_TILE
    tile_end = jnp.cumsum(ntiles_e)
    dst_run = ((tile_end - ntiles_e) * ROW_TILE)[None, :] + run_end - cpad
    j8 = jnp.arange(PERM_SLABS, dtype=I32) * SUBLANES
    e_of = jnp.minimum(jnp.sum((j8[None, :, None] >= seg_end[:, None, :]).astype(I32), axis=2),
                       N_EXPERTS - 1)
    sel = (e_of[:, :, None] == ex).astype(I32)
    slab_dst = jnp.sum(sel * (dst_run - seg_off)[:, None, :], axis=2) + j8[None, :]
    n_used = tile_end[-1]
    max_tiles = _max_row_tiles(ids.shape[1])
    t = jnp.arange(max_tiles, dtype=I32)
    te = jnp.sum((t[:, None] >= tile_end[None, :]).astype(I32), axis=1)
    te_last = jnp.sum((n_used - 1 >= tile_end).astype(I32))
    row_start = (tile_end - ntiles_e) * ROW_TILE
    tail_cnt = (ntiles_e * ROW_TILE - run_end[-1]) // SUBLANES
    tail_end = jnp.cumsum(tail_cnt)
    k = jnp.arange(ZERO_TABLE, dtype=I32)
    e_k = jnp.minimum(jnp.sum((k[:, None] >= tail_end[None, :]).astype(I32), axis=1), N_EXPERTS - 1)
    base_k = jnp.sum((e_k[:, None] == ex).astype(I32)
                     * (row_start + run_end[-1] - SUBLANES * (tail_end - tail_cnt))[None, :], axis=1)
    return dict(
        slab_dst=slab_dst.reshape(-1).astype(I32), nslab=(seg_end[:, -1] // SUBLANES).astype(I32),
        ztab=(base_k + SUBLANES * k).astype(I32), zinfo=jnp.stack([tail_end[-1], n_used]).astype(I32),
        slot_rows=slot.reshape(nt, TOP_K, TOK_TILE).astype(I32),
        gate_rows=per_tile(gates).astype(F32),
        tile_expert=jnp.where(t < n_used, jnp.minimum(te, N_EXPERTS - 1), te_last).astype(I32),
        n_used=n_used.reshape(1).astype(I32))


def _max_row_tiles(n_tokens):
    rows = TOP_K * n_tokens + (n_tokens // TOK_TILE) * N_EXPERTS * (SUBLANES - 1)
    return (rows + ROW_TILE - 1) // ROW_TILE + N_EXPERTS


def _moe(x1s, routes, wts):
    ids = jnp.concatenate([r[TOP_K:2 * TOP_K, :] for r in routes], axis=1).astype(I32)
    gates = jnp.concatenate([r[0:TOP_K, :] for r in routes], axis=1)
    plan = _routing_plan(ids, gates)
    max_tiles = _max_row_tiles(ids.shape[1])
    bases = [0]
    for x1 in x1s[:-1]:
        bases.append(bases[-1] + x1.shape[0] // TOK_TILE)
    xs = _dispatch(plan, x1s, max_tiles)
    ye = _experts(plan['tile_expert'], plan['n_used'], xs, wts['w_gate'], wts['w_up'], wts['w_down'])
    return [_combine(plan, base, x1, ye, wts['ln2_g'], wts['ln2_b']) for base, x1 in zip(bases, x1s)]


def _prep_weights(w_in, w_out, conv_w, a_log, dt_bias, gdn_norm_w, ln1_g, ln1_b, w_router_group,
                  w_router_expert, w_gate, w_up, w_down, ln2_g, ln2_b):
    o1 = Q_COLS + 2 * KV_COLS
    o2 = o1 + CONV_DIM
    o3 = o2 + Z_COLS
    pad_row = lambda v: jnp.pad(v.astype(F32), (0, LANES - v.shape[0]))[None, :]
    wab = jnp.pad(w_in[:, o3:], ((0, 0), (0, LANES - 2 * GDN_HEADS)))
    wr = jnp.pad(jnp.concatenate([w_router_group, w_router_expert], axis=1),
                 ((0, 0), (0, LANES - N_GROUPS - N_EXPERTS)))
    wr_hi = wr.astype(BF16)
    wr_mid = (wr - wr_hi.astype(F32)).astype(BF16)
    group = ATTN_HEADS // ATTN_KV_HEADS
    head_order = jnp.array([h for j in range(group) for h in (j, j + group)], I32)
    col_order = (head_order[:, None] * HEAD_DIM + jnp.arange(HEAD_DIM, dtype=I32)[None, :]).reshape(-1)
    wqkv = jnp.concatenate([w_in[:, :Q_COLS][:, col_order], w_in[:, Q_COLS:o1]], axis=1)
    wo = jnp.concatenate([w_out[:Q_COLS][col_order], w_out[Q_COLS:]], axis=0)
    return dict(
        wqkv=wqkv.astype(BF16), wg=w_in[:, o1:o2].astype(BF16), wz=w_in[:, o2:o3].astype(BF16),
        wab=wab.astype(BF16), convw=conv_w.astype(F32), alog=pad_row(a_log), dtb=pad_row(dt_bias),
        norm_w=gdn_norm_w.astype(F32)[None, :], wo=wo.astype(BF16),
        ln1_g=ln1_g[None, :], ln1_b=ln1_b[None, :], wr=jnp.stack([wr_hi, wr_mid]),
        w_gate=w_gate, w_up=w_up, w_down=w_down, ln2_g=ln2_g[None, :], ln2_b=ln2_b[None, :])


def _layer(x_prompt, x_sample, cache_k, cache_v, state_gdn, state_conv, wts):
    bp, sp, _ = x_prompt.shape
    bs, ts, _ = x_sample.shape
    n_p = bp * sp

    xp = x_prompt.reshape(n_p, D_MODEL)
    (q, k, v, qg, kg, vg, z, gcb, utail) = _proj(xp, jnp.arange(sp, dtype=I32), wts, GDN_CHUNK, bp)
    attn_p = _attn_prompt(q, k, v, wts['sinks'], bp)
    gdn_p, s_p = _gdn_prompt(qg, kg, vg, z, gcb, wts['norm_w'], bp)
    last_win = lambda a: a.reshape(bp, sp, KV_COLS)[:, sp - WINDOW:].reshape(bp, WINDOW, ATTN_KV_HEADS,
                                                                            HEAD_DIM)
    new_k_p, new_v_p = last_win(k), last_win(v)
    tiles_per_seq = sp // min(PROJ_TILE, sp)
    conv_p = utail.reshape(bp, tiles_per_seq, SUBLANES, CONV_DIM)[:, -1, SUBLANES - (CONV_W - 1):]

    lo, hi = SAMPLE_FIRST, SAMPLE_FIRST + ts
    xs_rows = jnp.pad(x_sample, ((0, 0), (lo, SAMPLE_SLOTS - hi), (0, 0))).reshape(bs * SAMPLE_SLOTS, D_MODEL)
    hist = jnp.pad(state_conv, ((0, 0), (0, SAMPLE_SLOTS - lo), (0, 0))).reshape(bs * SAMPLE_SLOTS, CONV_DIM)
    slot = jnp.arange(SAMPLE_SLOTS, dtype=I32)
    valid = jnp.tile(((slot >= lo) & (slot < hi)).astype(F32), bs)[:, None]
    pos_s = jnp.tile(PAST_LEN + slot - lo, bs)
    (q, k, v, qg, kg, vg, z, gcb, u_s) = _proj(xs_rows, pos_s, wts, SAMPLE_SLOTS, 1, hist, valid)
    ck = cache_k.reshape(bs, WINDOW, KV_COLS)
    cv = cache_v.reshape(bs, WINDOW, KV_COLS)
    attn_s, kwin, vwin = _attn_sample(q, k, v, ck, cv, wts['sinks'], bs, ts)
    gdn_s, s_s = _gdn_sample(qg, kg, vg, z, gcb, wts['norm_w'], state_gdn)
    real = lambda a: a.reshape(bs, SAMPLE_SLOTS, -1)[:, lo:hi]
    new_k_s = kwin.reshape(bs, WINDOW, ATTN_KV_HEADS, HEAD_DIM)
    new_v_s = vwin.reshape(bs, WINDOW, ATTN_KV_HEADS, HEAD_DIM)
    conv_s = u_s.reshape(bs, SAMPLE_SLOTS, CONV_DIM)[:, hi - (CONV_W - 1):hi]

    x1_p, route_p = _post(attn_p, gdn_p, xp, wts)
    x1_s, route_s = _post(real(attn_s).reshape(bs * ts, Q_COLS), real(gdn_s).reshape(bs * ts, Z_COLS),
                          x_sample.reshape(bs * ts, D_MODEL), wts)
    y_p, y_s = _moe([x1_p, x1_s], [route_p, route_s], wts)
    return (y_p.reshape(bp, sp, D_MODEL), y_s.reshape(bs, ts, D_MODEL), new_k_p, new_v_p, s_p, conv_p,
            new_k_s, new_v_s, s_s, conv_s)


def kernel(x_prompt, x_sample, cache_attn_k, cache_attn_v, state_gdn, state_conv, w_in, w_out,
           attn_sinks, conv_w, a_log, dt_bias, gdn_norm_w, ln1_g, ln1_b, w_router_group,
           w_router_expert, w_gate, w_up, w_down, ln2_g, ln2_b):
    assert w_in.shape[0] == DEPTH
    l = 0
    wts = _prep_weights(w_in[l], w_out[l], conv_w[l], a_log[l], dt_bias[l], gdn_norm_w[l], ln1_g[l],
                        ln1_b[l], w_router_group[l], w_router_expert[l], w_gate[l], w_up[l],
                        w_down[l], ln2_g[l], ln2_b[l])
    wts['sinks'] = attn_sinks[l]
    outs = _layer(x_prompt, x_sample, cache_attn_k[l], cache_attn_v[l], state_gdn[l], state_conv[l], wts)
    (y_p, y_s, k_p, v_p, s_p, c_p, k_s, v_s, s_s, c_s) = outs
    add = lambda a: a[None]
    return (y_p, y_s, add(k_p), add(v_p), add(s_p), add(c_p), add(k_s), add(v_s), add(s_s), add(c_s))
```

```python
import functools
import math

import jax
import jax.numpy as jnp
from jax import lax
from jax.experimental import pallas as pl
from jax.experimental.pallas import tpu as pltpu

F32 = jnp.float32
BF16 = jnp.bfloat16
I32 = jnp.int32

D_MODEL = 1024
ATTN_HEADS = 8
ATTN_KV_HEADS = 2
HEAD_DIM = 64
WINDOW = 128
ROT_DIM = HEAD_DIM // 4
ROPE_THETA = 500000.0
GDN_HEADS = 4
GDN_DK = 128
GDN_DV = 128
CONV_W = 4
QK_COLS = GDN_HEADS * GDN_DK
CONV_DIM = 2 * QK_COLS + GDN_HEADS * GDN_DV
Z_COLS = GDN_HEADS * GDN_DV
Q_COLS = ATTN_HEADS * HEAD_DIM
KV_COLS = ATTN_KV_HEADS * HEAD_DIM
N_GROUPS = 4
EXPERTS_PER_GROUP = 8
N_EXPERTS = N_GROUPS * EXPERTS_PER_GROUP
TOP_K = 2
EXPERT_FF = 256
NORM_EPS = 1e-5
L2_EPS = 1e-6
DEPTH = 1
DEEPNORM_ALPHA = (2 * DEPTH) ** 0.25
PAST_LEN = 8192

LANES = 128
SUBLANES = 8
TOK_TILE = 256
PROJ_TILE = 512
PROJ_SUB = 128
GDN_CHUNK = 128
GDN_SEQ_PER_STEP = 4
ATTN_BLOCKS_PER_STEP = 4
ATTN_SEQS_PER_STEP = 8
INV_BASE = 16
SAMPLE_SLOTS = 8
SAMPLE_FIRST = CONV_W - 1
ROW_TILE = 512
EXPERT_SUB = 256
SLAB_UNROLL = 4
PERM_ROWS = TOP_K * TOK_TILE + N_EXPERTS * SUBLANES
PERM_SLABS = PERM_ROWS // SUBLANES
XS_WORDS = D_MODEL + LANES
ZERO_TABLE = N_EXPERTS * (ROW_TILE // SUBLANES)
VMEM_LIMIT = 48 * 1024 * 1024
NEG_BIG = -1e30


def _cparams(sem):
    return pltpu.CompilerParams(dimension_semantics=sem, vmem_limit_bytes=VMEM_LIMIT)


def _bdot(a, b):
    return jnp.dot(a.astype(BF16), b.astype(BF16), preferred_element_type=F32)


def _bdot_nt(a, b):
    return lax.dot_general(a.astype(BF16), b.astype(BF16), (((1,), (1,)), ((), ())),
                           preferred_element_type=F32)


def _bdot_tn(a, b):
    return lax.dot_general(a.astype(BF16), b.astype(BF16), (((0,), (0,)), ((), ())),
                           preferred_element_type=F32)


def _div_pow2(x, n):
    return jnp.right_shift(x, int(math.log2(n)))


def _mod_pow2(x, n):
    return jnp.bitwise_and(x, n - 1)


def _split3(x):
    hi = x.astype(BF16)
    r = x - hi.astype(F32)
    mid = r.astype(BF16)
    lo = (r - mid.astype(F32)).astype(BF16)
    return hi, mid, lo


def _dot_exact_lhs01(m01, x):
    hi, mid, lo = _split3(x)
    d = lambda t: jnp.dot(m01, t, preferred_element_type=F32)
    return d(hi) + d(mid) + d(lo)


def _sigmoid(x):
    return 1.0 / (1.0 + jnp.exp(-x))


def _silu(x):
    return x * _sigmoid(x)


def _softplus(x):
    return jnp.maximum(x, 0.0) + jnp.log1p(jnp.exp(-jnp.abs(x)))


def _layer_norm(h, g, b):
    mu = jnp.mean(h, axis=-1, keepdims=True)
    d = h - mu
    var = jnp.mean(d * d, axis=-1, keepdims=True)
    return d * lax.rsqrt(var + NORM_EPS) * g + b


def _proj_kernel(*refs, tm, has_hist, full_u, one_segment):
    it = iter(refs)
    x_ref, cos_ref, sin_ref = next(it), next(it), next(it)
    wqkv_ref, wg_ref, wz_ref, wab_ref = next(it), next(it), next(it), next(it)
    convw_ref, alog_ref, dtb_ref, tri_ref, seg_ref = next(it), next(it), next(it), next(it), next(it)
    hist_ref = valid_ref = None
    if has_hist:
        hist_ref, valid_ref = next(it), next(it)
    q_ref, k_ref, v_ref = next(it), next(it), next(it)
    qg_ref, kg_ref, vg_ref, z_ref, gcb_ref, u_ref = (next(it) for _ in range(6))
    ubuf = next(it)

    t = pl.program_id(1)
    sub = PROJ_SUB
    rows = [slice(j * sub, (j + 1) * sub) for j in range(tm // sub)]
    lane = lax.broadcasted_iota(I32, (sub, LANES), 1)
    first_half = _mod_pow2(lane, HEAD_DIM) < (ROT_DIM // 2)

    @pl.when(t == 0)
    def _():
        ubuf[0:SUBLANES, :] = jnp.zeros((SUBLANES, CONV_DIM), F32)

    @pl.when(t > 0)
    def _():
        ubuf[0:SUBLANES, :] = ubuf[tm:tm + SUBLANES, :]

    dots = []
    for r in rows:
        xb = x_ref[r, :].astype(BF16)
        dots.append([jnp.dot(xb, w[...], preferred_element_type=F32)
                     for w in (wqkv_ref, wg_ref, wz_ref, wab_ref)])

    def l2n(s):
        return s * lax.rsqrt(jnp.sum(s * s, axis=1, keepdims=True) + L2_EPS)

    for r, (pq, u, z, ab) in zip(rows, dots):
        cosv, sinv = cos_ref[r, :], sin_ref[r, :]

        def rope(s):
            sw = jnp.where(first_half, pltpu.roll(s, LANES - ROT_DIM // 2, axis=1),
                           pltpu.roll(s, ROT_DIM // 2, axis=1))
            return s * cosv + sw * sinv

        for j in range(Q_COLS // LANES):
            q_ref[r, j * LANES:(j + 1) * LANES] = rope(pq[:, j * LANES:(j + 1) * LANES])
        k_ref[r, :] = rope(pq[:, Q_COLS:Q_COLS + KV_COLS])
        v_ref[r, :] = pq[:, Q_COLS + KV_COLS:Q_COLS + 2 * KV_COLS]
        z_ref[r, :] = z

        if has_hist:
            u = u + hist_ref[r, :]
        if full_u:
            u_ref[r, :] = u
        elif r.stop == tm:
            u_ref[...] = u[sub - SUBLANES:, :]
        base = SUBLANES + r.start
        ubuf[base:base + sub, :] = u
        acc = u * convw_ref[CONV_W - 1:CONV_W, :]
        for j in range(1, CONV_W):
            acc = acc + ubuf[base - j:base - j + sub, :] * convw_ref[CONV_W - 1 - j:CONV_W - j, :]
        c = _silu(acc)
        if has_hist:
            c = c * valid_ref[r, :]
        for h in range(GDN_HEADS):
            sl = slice(h * GDN_DK, (h + 1) * GDN_DK)
            qg_ref[r, sl] = l2n(c[:, sl]) * (GDN_DK ** -0.5)
            kg_ref[r, sl] = l2n(c[:, QK_COLS + h * GDN_DK:QK_COLS + (h + 1) * GDN_DK])
        vg_ref[r, :] = c[:, 2 * QK_COLS:]

        g = -jnp.exp(alog_ref[...]) * _softplus(ab + dtb_ref[...])
        beta = _sigmoid(ab)
        if has_hist:
            g = g * valid_ref[r, :]
            beta = beta * valid_ref[r, :]
        g = jnp.where(lane < GDN_HEADS, g, 0.0)
        gc = _dot_exact_lhs01(tri_ref[...], g)
        if one_segment:
            gl = jnp.broadcast_to(gc[sub - 1:sub, :], (sub, LANES))
        else:
            gl = _dot_exact_lhs01(seg_ref[...], g)
        gcb_ref[r, :] = jnp.where(lane < GDN_HEADS, gc,
                                  jnp.where(lane < 2 * GDN_HEADS, beta,
                                            jnp.where(lane < 3 * GDN_HEADS,
                                                      pltpu.roll(gl, 2 * GDN_HEADS, axis=1), 0.0)))


def _rope_tables(pos):
    half = ROT_DIM // 2
    inv_freq = ROPE_THETA ** (-jnp.arange(half, dtype=F32) * 2.0 / ROT_DIM)
    ang = pos.astype(F32)[:, None] * inv_freq[None, :]
    cos, sin = jnp.cos(ang), jnp.sin(ang)
    p = pos.shape[0]
    cpat = jnp.concatenate([cos, cos, jnp.ones((p, HEAD_DIM - ROT_DIM), F32)], axis=1)
    spat = jnp.concatenate([-sin, sin, jnp.zeros((p, HEAD_DIM - ROT_DIM), F32)], axis=1)
    return jnp.tile(cpat, (1, LANES // HEAD_DIM)), jnp.tile(spat, (1, LANES // HEAD_DIM))


def _segment_matrices(tm, seg_len):
    i = jnp.arange(tm)
    same = (i[:, None] // seg_len) == (i[None, :] // seg_len)
    tri = same & (i[None, :] <= i[:, None])
    return tri.astype(BF16), same.astype(BF16)


def _proj(x, pos, wts, seg_len, n_seq, hist=None, valid=None):
    n = x.shape[0]
    rows = n // n_seq
    tm = min(PROJ_TILE, rows)
    nt = rows // tm
    has_hist = hist is not None
    cos_t, sin_t = _rope_tables(pos)
    tri, seg = _segment_matrices(PROJ_SUB, seg_len)

    tok = lambda w: pl.BlockSpec((tm, w), lambda b, t: (b * nt + t, 0))
    const = lambda a: pl.BlockSpec(a.shape, lambda b, t: (0,) * a.ndim)
    in_arrays = [x, cos_t, sin_t, wts['wqkv'], wts['wg'], wts['wz'], wts['wab'],
                 wts['convw'], wts['alog'], wts['dtb'], tri, seg]
    in_specs = [tok(D_MODEL), pl.BlockSpec((tm, LANES), lambda b, t: (t, 0)),
                pl.BlockSpec((tm, LANES), lambda b, t: (t, 0))] + [const(a) for a in in_arrays[3:]]
    if has_hist:
        in_arrays += [hist, valid]
        in_specs += [tok(CONV_DIM), tok(1)]
    u_rows = n if has_hist else (n // tm) * SUBLANES
    u_block = tm if has_hist else SUBLANES
    out_shape = [jax.ShapeDtypeStruct((n, Q_COLS), F32), jax.ShapeDtypeStruct((n, KV_COLS), F32),
                 jax.ShapeDtypeStruct((n, KV_COLS), F32), jax.ShapeDtypeStruct((n, QK_COLS), F32),
                 jax.ShapeDtypeStruct((n, QK_COLS), F32), jax.ShapeDtypeStruct((n, Z_COLS), F32),
                 jax.ShapeDtypeStruct((n, Z_COLS), F32), jax.ShapeDtypeStruct((n, LANES), F32),
                 jax.ShapeDtypeStruct((u_rows, CONV_DIM), F32)]
    out_specs = [tok(Q_COLS), tok(KV_COLS), tok(KV_COLS), tok(QK_COLS), tok(QK_COLS), tok(Z_COLS),
                 tok(Z_COLS), tok(LANES),
                 pl.BlockSpec((u_block, CONV_DIM), lambda b, t: (b * nt + t, 0))]
    return pl.pallas_call(
        functools.partial(_proj_kernel, tm=tm, has_hist=has_hist, full_u=has_hist,
                          one_segment=seg_len == PROJ_SUB),
        out_shape=out_shape, grid=(n_seq, nt), in_specs=in_specs, out_specs=out_specs,
        scratch_shapes=[pltpu.VMEM((tm + SUBLANES, CONV_DIM), F32)],
        compiler_params=_cparams(("arbitrary", "arbitrary")),
        name="proj_hist" if has_hist else "proj",
    )(*in_arrays)


def _attn_blocks(qs, kcats, vcats, biases, sink, tq):
    lane = lax.broadcasted_iota(I32, (tq, LANES), 1)
    low = lane < HEAD_DIM
    n_slab = Q_COLS // LANES

    def stack(q):
        slabs = [q[:, j * LANES:(j + 1) * LANES] * (HEAD_DIM ** -0.5) for j in range(n_slab)]
        parts = ([jnp.where(low, s, 0.0) for s in slabs] + [jnp.where(low, 0.0, s) for s in slabs])
        return jnp.concatenate(parts, axis=0).astype(BF16)

    def unstack(o8):
        return [jnp.where(low, o8[j * tq:(j + 1) * tq, :], o8[(n_slab + j) * tq:(n_slab + j + 1) * tq, :])
                for j in range(n_slab)]

    rows = ATTN_HEADS * tq
    half = rows // 2
    klow = lax.broadcasted_iota(I32, (2 * WINDOW, LANES), 1) < HEAD_DIM
    one = jnp.ones((), BF16)
    q8s = _each(stack, qs)
    ss = _each(lambda q8, kc, b: _bdot_nt(q8, kc) + b, q8s, kcats, biases)
    ms = _each(lambda s: jnp.maximum(jnp.broadcast_to(jnp.max(s, axis=1, keepdims=True), (rows, LANES)),
                                     sink), ss)
    ps = _each(lambda s, m: jnp.exp(s - jnp.concatenate([m, m], axis=1)).astype(BF16), ss, ms)
    pv0 = _each(lambda p, vc: jnp.dot(p[:half], jnp.where(klow, vc, one), preferred_element_type=F32),
                ps, vcats)
    pv1 = _each(lambda p, vc: jnp.dot(p[half:], jnp.where(klow, one, vc), preferred_element_type=F32),
                ps, vcats)
    pvs = _each(lambda a, b: jnp.concatenate([a, b], axis=0), pv0, pv1)
    o8s = _each(lambda pv, m: pv / (pltpu.roll(pv, HEAD_DIM, axis=1) + jnp.exp(sink - m)), pvs, ms)
    return _each(unstack, o8s)


def _attn_prompt_kernel(q_ref, kc_ref, vc_ref, kp_ref, vp_ref, bias0_ref, bias_ref, sink_ref, o_ref, *,
                        nblk):
    kall = jnp.concatenate([kp_ref[...], kc_ref[...]], axis=0).astype(BF16)
    vall = jnp.concatenate([vp_ref[...], vc_ref[...]], axis=0).astype(BF16)
    win = lambda a, j: a[j * WINDOW:(j + 2) * WINDOW, :]
    qs = [q_ref[j * WINDOW:(j + 1) * WINDOW, :] for j in range(nblk)]
    biases = [bias0_ref[0]] + [bias_ref[...]] * (nblk - 1)
    outs = _attn_blocks(qs, [win(kall, j) for j in range(nblk)], [win(vall, j) for j in range(nblk)],
                        biases, sink_ref[...], WINDOW)
    for j, slabs in enumerate(outs):
        for c, slab in enumerate(slabs):
            o_ref[j * WINDOW:(j + 1) * WINDOW, c * LANES:(c + 1) * LANES] = slab


def _attn_sample_kernel(q_ref, kc_ref, vc_ref, kp_ref, vp_ref, bias_ref, sink_ref, o_ref, kw_ref, vw_ref,
                        *, nseq, n_new):
    tq = SAMPLE_SLOTS
    zpad = jnp.zeros((WINDOW - tq, LANES), F32)
    rows = lambda ref, j: ref[j * tq:(j + 1) * tq, :]
    cat = lambda pref, cref, j: jnp.concatenate([pref[j], rows(cref, j), zpad], axis=0).astype(BF16)
    outs = _attn_blocks([rows(q_ref, j) for j in range(nseq)],
                        [cat(kp_ref, kc_ref, j) for j in range(nseq)],
                        [cat(vp_ref, vc_ref, j) for j in range(nseq)],
                        [bias_ref[...]] * nseq, sink_ref[...], tq)
    for j, slabs in enumerate(outs):
        for c, slab in enumerate(slabs):
            o_ref[j * tq:(j + 1) * tq, c * LANES:(c + 1) * LANES] = slab
    row = lax.broadcasted_iota(I32, (WINDOW, LANES), 0)
    keep = WINDOW - n_new
    for pref, cref, wref in ((kp_ref, kc_ref, kw_ref), (vp_ref, vc_ref, vw_ref)):
        for j in range(nseq):
            new = jnp.concatenate([rows(cref, j), zpad], axis=0)
            wref[j] = jnp.where(row < keep, pltpu.roll(pref[j], keep, axis=0),
                                pltpu.roll(new, keep - SAMPLE_FIRST, axis=0))


def _sink_rows(sinks, tq):
    return jnp.broadcast_to(jnp.repeat(sinks.astype(F32), tq)[:, None], (ATTN_HEADS * tq, LANES))


def _attn_bias(tq, q_off, k_lo, k_hi, has_prev):
    qi = (jnp.arange(ATTN_HEADS * tq, dtype=I32) % tq)[:, None]
    c = jnp.arange(2 * WINDOW, dtype=I32)[None, :]
    cj = c - WINDOW
    vis_prev = (c < WINDOW) & (c > qi - q_off) & has_prev
    vis_cur = (c >= WINDOW) & (cj <= qi) & (cj >= k_lo) & (cj <= k_hi)
    return jnp.where(vis_prev | vis_cur, 0.0, NEG_BIG).astype(F32)


def _attn_prompt(q, k, v, sinks, n_seq):
    n = q.shape[0]
    nb = n // n_seq // WINDOW
    nblk = min(ATTN_BLOCKS_PER_STEP, nb)
    steps = nb // nblk
    tq = nblk * WINDOW
    cur = lambda w: pl.BlockSpec((tq, w), lambda b, i: (b * steps + i, 0))
    prev = pl.BlockSpec((WINDOW, LANES), lambda b, i: (b * nb + jnp.maximum(i * nblk - 1, 0), 0))
    bias2 = jnp.stack([_attn_bias(WINDOW, 0, 0, WINDOW - 1, False),
                       _attn_bias(WINDOW, 0, 0, WINDOW - 1, True)])
    rows = ATTN_HEADS * WINDOW
    return pl.pallas_call(
        functools.partial(_attn_prompt_kernel, nblk=nblk),
        out_shape=jax.ShapeDtypeStruct((n, Q_COLS), F32), grid=(n_seq, steps),
        in_specs=[cur(Q_COLS), cur(LANES), cur(LANES), prev, prev,
                  pl.BlockSpec((1, rows, 2 * WINDOW), lambda b, i: (jnp.minimum(i, 1), 0, 0)),
                  pl.BlockSpec((rows, 2 * WINDOW), lambda b, i: (0, 0)),
                  pl.BlockSpec((rows, LANES), lambda b, i: (0, 0))],
        out_specs=cur(Q_COLS),
        compiler_params=_cparams(("arbitrary", "arbitrary")), name="attn_prompt",
    )(q, k, v, k, v, bias2, bias2[1], _sink_rows(sinks, WINDOW))


def _attn_sample(q, k, v, cache_k, cache_v, sinks, n_seq, n_new):
    tq = SAMPLE_SLOTS
    nseq = min(ATTN_SEQS_PER_STEP, n_seq)
    cur = lambda w: pl.BlockSpec((nseq * tq, w), lambda b: (b, 0))
    prev = pl.BlockSpec((nseq, WINDOW, LANES), lambda b: (b, 0, 0))
    bias = _attn_bias(tq, SAMPLE_FIRST, SAMPLE_FIRST, SAMPLE_FIRST + 3, True)
    win = jax.ShapeDtypeStruct((n_seq, WINDOW, LANES), F32)
    return pl.pallas_call(
        functools.partial(_attn_sample_kernel, nseq=nseq, n_new=n_new),
        out_shape=[jax.ShapeDtypeStruct((n_seq * tq, Q_COLS), F32), win, win], grid=(n_seq // nseq,),
        in_specs=[cur(Q_COLS), cur(LANES), cur(LANES), prev, prev,
                  pl.BlockSpec(bias.shape, lambda b: (0, 0)),
                  pl.BlockSpec((ATTN_HEADS * tq, LANES), lambda b: (0, 0))],
        out_specs=[cur(Q_COLS), prev, prev],
        compiler_params=_cparams(("arbitrary",)), name="attn_sample",
    )(q, k, v, cache_k, cache_v, bias, _sink_rows(sinks, tq))


def _each(f, *lists):
    return [f(*args) for args in zip(*lists)]


def _unit_lower_inverse(ms, eye, same_base):
    c = ms[0].shape[0]

    def neumann(q0s, n_factors):
        xs = _each(lambda q: eye + q, q0s)
        if n_factors == 1:
            return xs
        qs = _each(_bdot, q0s, q0s)
        for _ in range(n_factors - 2):
            prods = _each(lambda x, q: _bdot(jnp.concatenate([x, q], axis=0), q), xs, qs)
            xs = _each(lambda x, pr: x + pr[:c], xs, prods)
            qs = _each(lambda pr: pr[c:], prods)
        return _each(lambda x, q: x + _bdot(x, q), xs, qs)

    ds = _each(lambda m: jnp.where(same_base, m, 0.0), ms)
    xs = neumann(_each(lambda d: -d, ds), int(math.log2(INV_BASE)))
    nblk = c // INV_BASE
    if nblk == 1:
        return xs
    ls = _each(lambda m, d: m - d, ms, ds)
    ns = _each(lambda x, l: -_bdot(x, l), xs, ls)
    ys = neumann(ns, int(math.log2(nblk)))
    return _each(_bdot, ys, xs)


def _gdn_intra(qs, ks, vs, gcs, gls, betas, same_seq, low_incl, low_strict, eye, same_base):
    del same_seq
    e_gcs = _each(jnp.exp, gcs)

    def decay_of(gc):
        gc_row = jnp.sum(jnp.where(eye > 0, gc, 0.0), axis=0, keepdims=True)
        return jnp.where(low_incl, jnp.exp(jnp.where(low_incl, gc - gc_row, 0.0)), 0.0)

    c = qs[0].shape[0]
    decays = _each(decay_of, gcs)
    kbs = _each(lambda k, b: k * b, ks, betas)
    vbs = _each(lambda v, b: v * b, vs, betas)
    kqs = _each(lambda kb, q, k: _bdot_nt(jnp.concatenate([kb, q], axis=0), k), kbs, qs, ks)
    ms = _each(lambda kq, d: jnp.where(low_strict, kq[:c] * d, 0.0), kqs, decays)
    attns = _each(lambda kq, d: jnp.where(low_incl, kq[c:] * d, 0.0), kqs, decays)
    tmats = _unit_lower_inverse(ms, eye, same_base)
    uws = _each(lambda t, vb, kb, e: _bdot(t, jnp.concatenate([vb, kb * e], axis=1)),
                tmats, vbs, kbs, e_gcs)
    us = _each(lambda uw: uw[:, :GDN_DV], uws)
    ws = _each(lambda uw: uw[:, GDN_DV:], uws)
    q_decs = _each(lambda q, e: q * e, qs, e_gcs)
    k_decs = _each(lambda k, gl, gc: k * jnp.exp(gl - gc), ks, gls, gcs)
    return us, ws, attns, q_decs, k_decs


def _chunk_masks(c, seq_len):
    i = lax.broadcasted_iota(I32, (c, c), 0)
    j = lax.broadcasted_iota(I32, (c, c), 1)
    same_seq = _div_pow2(i, seq_len) == _div_pow2(j, seq_len)
    low_incl = same_seq & (i >= j)
    low_strict = same_seq & (i > j)
    eye = (i == j).astype(F32)
    same_base = _div_pow2(i, INV_BASE) == _div_pow2(j, INV_BASE)
    return same_seq, low_incl, low_strict, eye, same_base


def _gated_rms(o, z, nw):
    o = o * lax.rsqrt(jnp.mean(o * o, axis=1, keepdims=True) + NORM_EPS) * nw
    return o * _silu(z)


def _gdn_prompt_kernel(qg_ref, kg_ref, vg_ref, z_ref, gcb_ref, nw_ref, o_ref, s_out_ref, s_scr):
    c = GDN_CHUNK
    n = pl.program_id(1)

    @pl.when(n == 0)
    def _():
        s_scr[...] = jnp.zeros_like(s_scr)

    masks = _chunk_masks(c, c)
    nw = nw_ref[...]
    chains = [(b, h) for b in range(qg_ref.shape[0]) for h in range(GDN_HEADS)]
    hs = lambda h: slice(h * GDN_DK, (h + 1) * GDN_DK)
    col = lambda off: [gcb_ref[b, :, off + h:off + h + 1] for b, h in chains]
    gcs, betas, gls = col(0), col(GDN_HEADS), col(2 * GDN_HEADS)
    qs = [qg_ref[b, :, hs(h)] for b, h in chains]
    ks = [kg_ref[b, :, hs(h)] for b, h in chains]
    vs = [vg_ref[b, :, hs(h)] for b, h in chains]
    us, ws, attns, q_decs, k_decs = _gdn_intra(qs, ks, vs, gcs, gls, betas, *masks)
    ss = [s_scr[b, h] for b, h in chains]
    wqs = _each(lambda w, qd, s: _bdot(jnp.concatenate([w, qd], axis=0), s), ws, q_decs, ss)
    wss = _each(lambda wq: wq[:c], wqs)
    qss = _each(lambda wq: wq[c:], wqs)
    v_news = _each(lambda u, x: u - x, us, wss)
    avs = _each(_bdot, attns, v_news)
    kvs = _each(_bdot_tn, k_decs, v_news)
    for (b, h), s, gl, qsv, av, kv in zip(chains, ss, gls, qss, avs, kvs):
        s_scr[b, h] = s * jnp.exp(gl[0:1, :]) + kv
        o_ref[b, :, hs(h)] = _gated_rms(qsv + av, z_ref[b, :, hs(h)], nw)

    @pl.when(n == pl.num_programs(1) - 1)
    def _():
        s_out_ref[...] = s_scr[...]


def _gdn_prompt(qg, kg, vg, z, gcb, norm_w, n_seq):
    n = qg.shape[0]
    s_len = n // n_seq
    nb = min(GDN_SEQ_PER_STEP, n_seq)
    v3 = lambda a: a.reshape(n_seq, s_len, a.shape[-1])
    tok = lambda w: pl.BlockSpec((nb, GDN_CHUNK, w), lambda b, i: (b, i, 0))
    o, s = pl.pallas_call(
        _gdn_prompt_kernel,
        out_shape=[jax.ShapeDtypeStruct((n_seq, s_len, Z_COLS), F32),
                   jax.ShapeDtypeStruct((n_seq, GDN_HEADS, GDN_DK, GDN_DV), F32)],
        grid=(n_seq // nb, s_len // GDN_CHUNK),
        in_specs=[tok(QK_COLS), tok(QK_COLS), tok(Z_COLS), tok(Z_COLS), tok(LANES),
                  pl.BlockSpec((1, GDN_DV), lambda b, i: (0, 0))],
        out_specs=[tok(Z_COLS),
                   pl.BlockSpec((nb, GDN_HEADS, GDN_DK, GDN_DV), lambda b, i: (b, 0, 0, 0))],
        scratch_shapes=[pltpu.VMEM((nb, GDN_HEADS, GDN_DK, GDN_DV), F32)],
        compiler_params=_cparams(("arbitrary", "arbitrary")), name="gdn_prompt",
    )(v3(qg), v3(kg), v3(vg), v3(z), v3(gcb), norm_w)
    return o.reshape(n, Z_COLS), s


def _gdn_sample_kernel(qg_ref, kg_ref, vg_ref, z_ref, gcb_ref, nw_ref, s_in_ref, o_ref, s_out_ref):
    c = GDN_CHUNK
    n_sub = c // SAMPLE_SLOTS
    masks = _chunk_masks(c, SAMPLE_SLOTS)
    heads = range(GDN_HEADS)
    hs = lambda h: slice(h * GDN_DK, (h + 1) * GDN_DK)
    rs = lambda s: slice(s * SAMPLE_SLOTS, (s + 1) * SAMPLE_SLOTS)
    col = lambda off: [gcb_ref[:, off + h:off + h + 1] for h in heads]
    gcs, betas, gls = col(0), col(GDN_HEADS), col(2 * GDN_HEADS)
    us, ws, attns, q_decs, k_decs = _gdn_intra([qg_ref[:, hs(h)] for h in heads],
                                               [kg_ref[:, hs(h)] for h in heads],
                                               [vg_ref[:, hs(h)] for h in heads], gcs, gls, betas, *masks)
    pairs = [(h, s) for h in heads for s in range(n_sub)]
    sts = [s_in_ref[s, h] for h, s in pairs]
    boths = [jnp.concatenate([ws[h][rs(s), :], q_decs[h][rs(s), :]], axis=0) for h, s in pairs]
    rr = _each(_bdot, boths, sts)
    gather = lambda h, part: jnp.concatenate(
        [rr[h * n_sub + s][part * SAMPLE_SLOTS:(part + 1) * SAMPLE_SLOTS, :] for s in range(n_sub)], axis=0)
    v_news = [us[h] - gather(h, 0) for h in heads]
    avs = _each(_bdot, attns, v_news)
    row = lax.broadcasted_iota(I32, (c, LANES), 0)
    seq_of_row = _div_pow2(row, SAMPLE_SLOTS)
    kds = [jnp.where(seq_of_row == s, k_decs[h], 0.0) for h, s in pairs]
    kvs = _each(_bdot_tn, kds, [v_news[h] for h, _ in pairs])
    egls = _each(jnp.exp, gls)
    for (h, s), st, kv in zip(pairs, sts, kvs):
        s_out_ref[s, h] = st * egls[h][s * SAMPLE_SLOTS:s * SAMPLE_SLOTS + 1, :] + kv
    nw = nw_ref[...]
    for h in heads:
        o_ref[:, hs(h)] = _gated_rms(gather(h, 1) + avs[h], z_ref[:, hs(h)], nw)


def _gdn_sample(qg, kg, vg, z, gcb, norm_w, state):
    n = qg.shape[0]
    n_sub = GDN_CHUNK // SAMPLE_SLOTS
    tok = lambda w: pl.BlockSpec((GDN_CHUNK, w), lambda i: (i, 0))
    st = pl.BlockSpec((n_sub, GDN_HEADS, GDN_DK, GDN_DV), lambda i: (i, 0, 0, 0))
    return pl.pallas_call(
        _gdn_sample_kernel,
        out_shape=[jax.ShapeDtypeStruct((n, Z_COLS), F32),
                   jax.ShapeDtypeStruct(state.shape, F32)],
        grid=(n // GDN_CHUNK,),
        in_specs=[tok(QK_COLS), tok(QK_COLS), tok(Z_COLS), tok(Z_COLS), tok(LANES),
                  pl.BlockSpec((1, GDN_DV), lambda i: (0, 0)), st],
        out_specs=[tok(Z_COLS), st],
        compiler_params=_cparams(("arbitrary",)), name="gdn_sample",
    )(qg, kg, vg, z, gcb, norm_w, state)


def _post_kernel(a_ref, g_ref, x_ref, wo_ref, ln_g_ref, ln_b_ref, wr_ref, x1_ref, route_ref, *, tm):
    sub = PROJ_SUB
    rows = [slice(j * sub, (j + 1) * sub) for j in range(tm // sub)]
    d = lambda a, b: jnp.dot(a, b, preferred_element_type=F32)
    mixes = [d(a_ref[r, :].astype(BF16), wo_ref[0:Q_COLS, :]) + d(g_ref[r, :].astype(BF16), wo_ref[Q_COLS:, :])
             for r in rows]
    x1s = [_layer_norm(DEEPNORM_ALPHA * x_ref[r, :] + mix, ln_g_ref[...], ln_b_ref[...])
           for r, mix in zip(rows, mixes)]
    for r, x1 in zip(rows, x1s):
        x1_ref[r, :] = x1
    wh, wm = wr_ref[0], wr_ref[1]
    lgs = []
    for x1 in x1s:
        xh = x1.astype(BF16)
        xm = (x1 - xh.astype(F32)).astype(BF16)
        lgs.append(d(xh, wh) + d(xh, wm) + d(xm, wh))
    for r, lg in zip(rows, lgs):
        route_ref[:, r] = jnp.transpose(_route(lg))[0:SUBLANES, :]


def _route(lg):
    lane = lax.broadcasted_iota(I32, lg.shape, 1)
    lane_f = lane.astype(F32)
    big = float(LANES)

    def first_max(vals, mask):
        v = jnp.where(mask, vals, NEG_BIG)
        mx = jnp.max(v, axis=1, keepdims=True)
        idx = jnp.min(jnp.where(mask & (v == mx), lane_f, big), axis=1, keepdims=True)
        return mx, idx

    gmask = lane < N_GROUPS
    gmax, gidx = first_max(lg, gmask)
    gden = jnp.sum(jnp.where(gmask, jnp.exp(lg - gmax), 0.0), axis=1, keepdims=True)
    g_top_p = 1.0 / gden
    e_lane = lane - N_GROUPS
    e_group = _div_pow2(jnp.maximum(e_lane, 0), EXPERTS_PER_GROUP).astype(F32)
    emask = (e_lane >= 0) & (e_lane < N_EXPERTS) & (e_group == gidx)
    m1, i1 = first_max(lg, emask)
    eden = jnp.sum(jnp.where(emask, jnp.exp(lg - m1), 0.0), axis=1, keepdims=True)
    m2, i2 = first_max(lg, emask & (lane_f != i1))
    p1 = 1.0 / eden
    p2 = jnp.exp(m2 - m1) / eden
    tot = p1 + p2
    gate1 = g_top_p * (p1 / tot)
    gate2 = g_top_p * (p2 / tot)
    return jnp.where(lane == 0, gate1,
                     jnp.where(lane == 1, gate2,
                               jnp.where(lane == 2, i1 - N_GROUPS,
                                         jnp.where(lane == 3, i2 - N_GROUPS, 0.0))))


def _post(attn_o, gdn_o, x, wts):
    n = x.shape[0]
    tm = min(PROJ_TILE, n)
    tok = lambda w: pl.BlockSpec((tm, w), lambda i: (i, 0))
    const = lambda a: pl.BlockSpec(a.shape, lambda i: (0,) * a.ndim)
    consts = [wts['wo'], wts['ln1_g'], wts['ln1_b'], wts['wr']]
    return pl.pallas_call(
        functools.partial(_post_kernel, tm=tm),
        out_shape=[jax.ShapeDtypeStruct((n, D_MODEL), F32), jax.ShapeDtypeStruct((SUBLANES, n), F32)],
        grid=(n // tm,),
        in_specs=[tok(Q_COLS), tok(Z_COLS), tok(D_MODEL)] + [const(a) for a in consts],
        out_specs=[tok(D_MODEL), pl.BlockSpec((SUBLANES, tm), lambda i: (0, i))],
        compiler_params=_cparams(("arbitrary",)), name="post_%d" % (n // tm),
    )(attn_o, gdn_o, x, *consts)


def _slab_loop(n, body):
    n_main = jnp.right_shift(n, int(math.log2(SLAB_UNROLL)))

    def main(i, c):
        for u in range(SLAB_UNROLL):
            body(i * SLAB_UNROLL + u, u)
        return c

    lax.fori_loop(0, n_main, main, 0)
    lax.fori_loop(n_main * SLAB_UNROLL, n, lambda j, c: (body(j, 0), c)[1], 0)


def _dispatch_kernel(dst_ref, nslab_ref, ztab_ref, zinfo_ref, slot_ref, gate_ref, *rest,
                     group_tiles, max_tiles):
    x_refs = rest[:len(group_tiles)]
    xs_ref, pbuf, sem, zbuf, zsem = rest[len(group_tiles):]
    n_tiles = sum(group_tiles)
    g = pl.program_id(0)
    cur = lax.rem(g, 2)

    def slab_copy(tile, buf_slot, j):
        d = pl.multiple_of(dst_ref[tile * PERM_SLABS + j], SUBLANES)
        src = pbuf.at[buf_slot, pl.ds(pl.multiple_of(j * SUBLANES, SUBLANES), SUBLANES), :]
        return pltpu.make_async_copy(src, xs_ref.at[pl.ds(d, SUBLANES), :], sem.at[buf_slot])

    def tail_copy(k):
        d = pl.multiple_of(ztab_ref[k], SUBLANES)
        return pltpu.make_async_copy(zbuf.at[pl.ds(0, SUBLANES), :], xs_ref.at[pl.ds(d, SUBLANES), :], zsem)

    def tile_copy(t):
        d = pl.multiple_of(t * ROW_TILE, ROW_TILE)
        return pltpu.make_async_copy(zbuf, xs_ref.at[pl.ds(d, ROW_TILE), :], zsem)

    @pl.when(g == 0)
    def _():
        zbuf[...] = jnp.zeros_like(zbuf)

    share = -(-ZERO_TABLE // n_tiles)
    k0 = g * share
    _slab_loop(jnp.clip(zinfo_ref[0] - k0, 0, share), lambda j, u: tail_copy(k0 + j).start(priority=1))

    @pl.when(zinfo_ref[1] + g < max_tiles)
    def _():
        tile_copy(zinfo_ref[1] + g).start(priority=1)

    x = x_refs[-1][...]
    bound = n_tiles
    for x_ref, nt in zip(x_refs[-2::-1], group_tiles[:0:-1]):
        bound -= nt
        x = jnp.where(g < bound, x_ref[...], x)

    r = lax.broadcasted_iota(I32, (PERM_ROWS, TOK_TILE), 0)
    sl = slot_ref[0]
    hit0, hit1 = r == sl[0:1, :], r == sl[1:2, :]
    onehot = jnp.where(hit0 | hit1, 1.0, 0.0).astype(BF16)
    gt = gate_ref[0]
    gcol = jnp.sum(jnp.where(hit0, gt[0:1, :], 0.0) + jnp.where(hit1, gt[1:2, :], 0.0),
                   axis=1, keepdims=True)
    pbuf[cur, :, 0:D_MODEL] = jnp.dot(onehot, x.astype(BF16), preferred_element_type=F32)
    pbuf[cur, :, D_MODEL:] = jnp.broadcast_to(gcol, (PERM_ROWS, LANES))

    @pl.when(g > 0)
    def _():
        _slab_loop(nslab_ref[g - 1], lambda j, u: slab_copy(g - 1, 1 - cur, j).wait())

    _slab_loop(nslab_ref[g], lambda j, u: slab_copy(g, cur, j).start(priority=u % 2))

    @pl.when(g == n_tiles - 1)
    def _():
        _slab_loop(nslab_ref[g], lambda j, u: slab_copy(g, cur, j).wait())
        lax.fori_loop(zinfo_ref[1] + n_tiles, max_tiles,
                      lambda t, c: (tile_copy(t).start(priority=1), c)[1], 0)
        _slab_loop(zinfo_ref[0], lambda k, u: tail_copy(k).wait())
        lax.fori_loop(zinfo_ref[1], max_tiles, lambda t, c: (tile_copy(t).wait(), c)[1], 0)


def _dispatch(plan, x1s, max_tiles):
    group_tiles = tuple(x1.shape[0] // TOK_TILE for x1 in x1s)
    n_tiles = sum(group_tiles)
    tile = lambda i, d, ns, zt, zi: (i, 0, 0)
    in_specs = [pl.BlockSpec((1, TOP_K, TOK_TILE), tile), pl.BlockSpec((1, TOP_K, TOK_TILE), tile)]
    base = 0
    for nt in group_tiles:
        in_specs.append(pl.BlockSpec(
            (TOK_TILE, D_MODEL),
            lambda i, d, ns, zt, zi, base=base, nt=nt: (jnp.clip(i - base, 0, nt - 1), 0)))
        base += nt
    return pl.pallas_call(
        functools.partial(_dispatch_kernel, group_tiles=group_tiles, max_tiles=max_tiles),
        out_shape=jax.ShapeDtypeStruct((max_tiles * ROW_TILE, XS_WORDS), F32),
        grid_spec=pltpu.PrefetchScalarGridSpec(
            num_scalar_prefetch=4, grid=(n_tiles,), in_specs=in_specs,
            out_specs=pl.BlockSpec(memory_space=pl.ANY),
            scratch_shapes=[pltpu.VMEM((2, PERM_ROWS, XS_WORDS), F32), pltpu.SemaphoreType.DMA((2,)),
                            pltpu.VMEM((ROW_TILE, XS_WORDS), F32), pltpu.SemaphoreType.DMA(())]),
        compiler_params=_cparams(("arbitrary",)), name="moe_dispatch",
    )(plan['slab_dst'], plan['nslab'], plan['ztab'], plan['zinfo'], plan['slot_rows'], plan['gate_rows'],
      *x1s)


def _expert_kernel(te_ref, nu_ref, xs_ref, wg_ref, wu_ref, wd_ref, ye_ref, wgu_scr, wd_scr):
    i = pl.program_id(0)
    active = i < nu_ref[0]

    @pl.when(active & ((i == 0) | (te_ref[i] != te_ref[jnp.maximum(i - 1, 0)])))
    def _():
        wgu_scr[:, 0:EXPERT_FF] = wg_ref[0].astype(BF16)
        wgu_scr[:, EXPERT_FF:] = wu_ref[0].astype(BF16)
        wd_scr[...] = wd_ref[0].astype(BF16)

    @pl.when(active)
    def _():
        sub = EXPERT_SUB
        rows = [slice(j * sub, (j + 1) * sub) for j in range(ROW_TILE // sub)]
        d = lambda a, b: jnp.dot(a, b, preferred_element_type=F32)
        wgu, wd = wgu_scr[...], wd_scr[...]
        xs = [xs_ref[r, 0:D_MODEL].astype(BF16) for r in rows]
        hs = [d(x, wgu) for x in xs]
        hhs = [(_silu(h[:, :EXPERT_FF]) * h[:, EXPERT_FF:]).astype(BF16) for h in hs]
        ys = [d(hh, wd) for hh in hhs]
        for r, y in zip(rows, ys):
            gate = xs_ref[r, D_MODEL:]
            ye_ref[r, :] = y * jnp.concatenate([gate] * (D_MODEL // LANES), axis=1)

    @pl.when(jnp.logical_not(active))
    def _():
        ye_ref[...] = jnp.zeros_like(ye_ref)


def _experts(tile_expert, n_used, xs, w_gate, w_up, w_down):
    n_tiles = xs.shape[0] // ROW_TILE
    row_in = lambda i, te, nu: (jnp.minimum(i, nu[0] - 1), 0)
    row = lambda i, te, nu: (i, 0)
    wsel = lambda i, te, nu: (te[i], 0, 0)
    return pl.pallas_call(
        _expert_kernel,
        out_shape=jax.ShapeDtypeStruct((xs.shape[0], D_MODEL), F32),
        grid_spec=pltpu.PrefetchScalarGridSpec(
            num_scalar_prefetch=2, grid=(n_tiles,),
            in_specs=[pl.BlockSpec((ROW_TILE, XS_WORDS), row_in),
                      pl.BlockSpec((1, D_MODEL, EXPERT_FF), wsel),
                      pl.BlockSpec((1, D_MODEL, EXPERT_FF), wsel),
                      pl.BlockSpec((1, EXPERT_FF, D_MODEL), wsel)],
            out_specs=pl.BlockSpec((ROW_TILE, D_MODEL), row),
            scratch_shapes=[pltpu.VMEM((D_MODEL, 2 * EXPERT_FF), BF16),
                            pltpu.VMEM((EXPERT_FF, D_MODEL), BF16)]),
        compiler_params=_cparams(("arbitrary",)), name="moe_experts",
    )(tile_expert, n_used, xs, w_gate, w_up, w_down)


def _combine_kernel(dst_ref, nslab_ref, x1_ref, slot_ref, ye_ref, ln_g_ref, ln_b_ref, y_ref,
                    buf, sem, *, tile_base, n_tiles):
    i = pl.program_id(0)
    g = tile_base + i
    cur = lax.rem(i, 2)

    def slab_copy(tile, buf_slot, j):
        d = pl.multiple_of(dst_ref[tile * PERM_SLABS + j], SUBLANES)
        dst = buf.at[buf_slot, pl.ds(pl.multiple_of(j * SUBLANES, SUBLANES), SUBLANES), :]
        return pltpu.make_async_copy(ye_ref.at[pl.ds(d, SUBLANES), :], dst, sem.at[buf_slot])

    @pl.when(i == 0)
    def _():
        buf[...] = jnp.zeros_like(buf)
        _slab_loop(nslab_ref[g], lambda j, u: slab_copy(g, cur, j).start(priority=u % 2))

    @pl.when(i + 1 < n_tiles)
    def _():
        _slab_loop(nslab_ref[g + 1], lambda j, u: slab_copy(g + 1, 1 - cur, j).start(priority=u % 2))

    _slab_loop(nslab_ref[g], lambda j, u: slab_copy(g, cur, j).wait())

    col = lax.broadcasted_iota(I32, (TOK_TILE, PERM_ROWS), 1)
    sl = slot_ref[0]
    diag = (lax.broadcasted_iota(I32, (TOK_TILE, TOK_TILE), 0)
            == lax.broadcasted_iota(I32, (TOK_TILE, TOK_TILE), 1))
    as_col = lambda row: jnp.sum(jnp.where(diag, row, 0), axis=1, keepdims=True)
    pick = jnp.where((col == as_col(sl[0:1, :])) | (col == as_col(sl[1:2, :])), 1.0, 0.0).astype(BF16)
    moe = jnp.dot(pick, buf[cur].astype(BF16), preferred_element_type=F32)
    y_ref[...] = _layer_norm(DEEPNORM_ALPHA * x1_ref[...] + moe, ln_g_ref[...], ln_b_ref[...])


def _combine(plan, tile_base, x1, ye, ln_g, ln_b):
    n = x1.shape[0]
    n_tiles = n // TOK_TILE
    tok = lambda w: pl.BlockSpec((TOK_TILE, w), lambda i, d, ns: (i, 0))
    const = lambda a: pl.BlockSpec(a.shape, lambda i, d, ns: (0,) * a.ndim)
    return pl.pallas_call(
        functools.partial(_combine_kernel, tile_base=tile_base, n_tiles=n_tiles),
        out_shape=jax.ShapeDtypeStruct((n, D_MODEL), F32),
        grid_spec=pltpu.PrefetchScalarGridSpec(
            num_scalar_prefetch=2, grid=(n_tiles,),
            in_specs=[tok(D_MODEL),
                      pl.BlockSpec((1, TOP_K, TOK_TILE), lambda i, d, ns: (tile_base + i, 0, 0)),
                      pl.BlockSpec(memory_space=pl.ANY), const(ln_g), const(ln_b)],
            out_specs=tok(D_MODEL),
            scratch_shapes=[pltpu.VMEM((2, PERM_ROWS, D_MODEL), F32), pltpu.SemaphoreType.DMA((2,))]),
        compiler_params=_cparams(("arbitrary",)), name="moe_combine_%d" % tile_base,
    )(plan['slab_dst'], plan['nslab'], x1, plan['slot_rows'], ye, ln_g, ln_b)


def _routing_plan(ids, gates):
    nt = ids.shape[1] // TOK_TILE
    pairs = TOP_K * TOK_TILE
    ex = jnp.arange(N_EXPERTS, dtype=I32)
    per_tile = lambda a: jnp.swapaxes(a.reshape(TOP_K, nt, TOK_TILE), 0, 1)
    flat = per_tile(ids).reshape(nt, pairs)
    onehot = (flat[:, None, :] == ex[None, :, None])
    p = jnp.arange(pairs, dtype=I32)
    triu = (p[:, None] <= p[None, :]).astype(BF16)
    csum = jnp.dot(onehot.astype(BF16).reshape(nt * N_EXPERTS, pairs), triu,
                   preferred_element_type=F32).astype(I32).reshape(nt, N_EXPERTS, pairs)
    oh = onehot.astype(I32)
    rank = jnp.sum(oh * (csum - 1), axis=1)
    cnt = csum[:, :, -1]
    cpad = (cnt + SUBLANES - 1) // SUBLANES * SUBLANES
    seg_end = jnp.cumsum(cpad, axis=1)
    seg_off = seg_end - cpad
    slot = jnp.sum(oh * seg_off[:, :, None], axis=1) + rank
    run_end = jnp.cumsum(cpad, axis=0)
    ntiles_e = (run_end[-1] + ROW_TILE - 1) // ROW_TILE
    tile_end = jnp.cumsum(ntiles_e)
    dst_run = ((tile_end - ntiles_e) * ROW_TILE)[None, :] + run_end - cpad
    j8 = jnp.arange(PERM_SLABS, dtype=I32) * SUBLANES
    e_of = jnp.minimum(jnp.sum((j8[None, :, None] >= seg_end[:, None, :]).astype(I32), axis=2),
                       N_EXPERTS - 1)
    sel = (e_of[:, :, None] == ex).astype(I32)
    slab_dst = jnp.sum(sel * (dst_run - seg_off)[:, None, :], axis=2) + j8[None, :]
    n_used = tile_end[-1]
    max_tiles = _max_row_tiles(ids.shape[1])
    t = jnp.arange(max_tiles, dtype=I32)
    te = jnp.sum((t[:, None] >= tile_end[None, :]).astype(I32), axis=1)
    te_last = jnp.sum((n_used - 1 >= tile_end).astype(I32))
    row_start = (tile_end - ntiles_e) * ROW_TILE
    tail_cnt = (ntiles_e * ROW_TILE - run_end[-1]) // SUBLANES
    tail_end = jnp.cumsum(tail_cnt)
    k = jnp.arange(ZERO_TABLE, dtype=I32)
    e_k = jnp.minimum(jnp.sum((k[:, None] >= tail_end[None, :]).astype(I32), axis=1), N_EXPERTS - 1)
    base_k = jnp.sum((e_k[:, None] == ex).astype(I32)
                     * (row_start + run_end[-1] - SUBLANES * (tail_end - tail_cnt))[None, :], axis=1)
    return dict(
        slab_dst=slab_dst.reshape(-1).astype(I32), nslab=(seg_end[:, -1] // SUBLANES).astype(I32),
        ztab=(base_k + SUBLANES * k).astype(I32), zinfo=jnp.stack([tail_end[-1], n_used]).astype(I32),
        slot_rows=slot.reshape(nt, TOP_K, TOK_TILE).astype(I32),
        gate_rows=per_tile(gates).astype(F32),
        tile_expert=jnp.where(t < n_used, jnp.minimum(te, N_EXPERTS - 1), te_last).astype(I32),
        n_used=n_used.reshape(1).astype(I32))


def _max_row_tiles(n_tokens):
    rows = TOP_K * n_tokens + (n_tokens // TOK_TILE) * N_EXPERTS * (SUBLANES - 1)
    return (rows + ROW_TILE - 1) // ROW_TILE + N_EXPERTS


def _moe(x1s, routes, wts):
    ids = jnp.concatenate([r[TOP_K:2 * TOP_K, :] for r in routes], axis=1).astype(I32)
    gates = jnp.concatenate([r[0:TOP_K, :] for r in routes], axis=1)
    plan = _routing_plan(ids, gates)
    max_tiles = _max_row_tiles(ids.shape[1])
    bases = [0]
    for x1 in x1s[:-1]:
        bases.append(bases[-1] + x1.shape[0] // TOK_TILE)
    xs = _dispatch(plan, x1s, max_tiles)
    ye = _experts(plan['tile_expert'], plan['n_used'], xs, wts['w_gate'], wts['w_up'], wts['w_down'])
    return [_combine(plan, base, x1, ye, wts['ln2_g'], wts['ln2_b']) for base, x1 in zip(bases, x1s)]


def _prep_weights(w_in, w_out, conv_w, a_log, dt_bias, gdn_norm_w, ln1_g, ln1_b, w_router_group,
                  w_router_expert, w_gate, w_up, w_down, ln2_g, ln2_b):
    o1 = Q_COLS + 2 * KV_COLS
    o2 = o1 + CONV_DIM
    o3 = o2 + Z_COLS
    pad_row = lambda v: jnp.pad(v.astype(F32), (0, LANES - v.shape[0]))[None, :]
    wab = jnp.pad(w_in[:, o3:], ((0, 0), (0, LANES - 2 * GDN_HEADS)))
    wr = jnp.pad(jnp.concatenate([w_router_group, w_router_expert], axis=1),
                 ((0, 0), (0, LANES - N_GROUPS - N_EXPERTS)))
    wr_hi = wr.astype(BF16)
    wr_mid = (wr - wr_hi.astype(F32)).astype(BF16)
    group = ATTN_HEADS // ATTN_KV_HEADS
    head_order = jnp.array([h for j in range(group) for h in (j, j + group)], I32)
    col_order = (head_order[:, None] * HEAD_DIM + jnp.arange(HEAD_DIM, dtype=I32)[None, :]).reshape(-1)
    wqkv = jnp.concatenate([w_in[:, :Q_COLS][:, col_order], w_in[:, Q_COLS:o1]], axis=1)
    wo = jnp.concatenate([w_out[:Q_COLS][col_order], w_out[Q_COLS:]], axis=0)
    return dict(
        wqkv=wqkv.astype(BF16), wg=w_in[:, o1:o2].astype(BF16), wz=w_in[:, o2:o3].astype(BF16),
        wab=wab.astype(BF16), convw=conv_w.astype(F32), alog=pad_row(a_log), dtb=pad_row(dt_bias),
        norm_w=gdn_norm_w.astype(F32)[None, :], wo=wo.astype(BF16),
        ln1_g=ln1_g[None, :], ln1_b=ln1_b[None, :], wr=jnp.stack([wr_hi, wr_mid]),
        w_gate=w_gate, w_up=w_up, w_down=w_down, ln2_g=ln2_g[None, :], ln2_b=ln2_b[None, :])


def _layer(x_prompt, x_sample, cache_k, cache_v, state_gdn, state_conv, wts):
    bp, sp, _ = x_prompt.shape
    bs, ts, _ = x_sample.shape
    n_p = bp * sp

    xp = x_prompt.reshape(n_p, D_MODEL)
    (q, k, v, qg, kg, vg, z, gcb, utail) = _proj(xp, jnp.arange(sp, dtype=I32), wts, GDN_CHUNK, bp)
    attn_p = _attn_prompt(q, k, v, wts['sinks'], bp)
    gdn_p, s_p = _gdn_prompt(qg, kg, vg, z, gcb, wts['norm_w'], bp)
    last_win = lambda a: a.reshape(bp, sp, KV_COLS)[:, sp - WINDOW:].reshape(bp, WINDOW, ATTN_KV_HEADS,
                                                                            HEAD_DIM)
    new_k_p, new_v_p = last_win(k), last_win(v)
    tiles_per_seq = sp // min(PROJ_TILE, sp)
    conv_p = utail.reshape(bp, tiles_per_seq, SUBLANES, CONV_DIM)[:, -1, SUBLANES - (CONV_W - 1):]

    lo, hi = SAMPLE_FIRST, SAMPLE_FIRST + ts
    xs_rows = jnp.pad(x_sample, ((0, 0), (lo, SAMPLE_SLOTS - hi), (0, 0))).reshape(bs * SAMPLE_SLOTS, D_MODEL)
    hist = jnp.pad(state_conv, ((0, 0), (0, SAMPLE_SLOTS - lo), (0, 0))).reshape(bs * SAMPLE_SLOTS, CONV_DIM)
    slot = jnp.arange(SAMPLE_SLOTS, dtype=I32)
    valid = jnp.tile(((slot >= lo) & (slot < hi)).astype(F32), bs)[:, None]
    pos_s = jnp.tile(PAST_LEN + slot - lo, bs)
    (q, k, v, qg, kg, vg, z, gcb, u_s) = _proj(xs_rows, pos_s, wts, SAMPLE_SLOTS, 1, hist, valid)
    ck = cache_k.reshape(bs, WINDOW, KV_COLS)
    cv = cache_v.reshape(bs, WINDOW, KV_COLS)
    attn_s, kwin, vwin = _attn_sample(q, k, v, ck, cv, wts['sinks'], bs, ts)
    gdn_s, s_s = _gdn_sample(qg, kg, vg, z, gcb, wts['norm_w'], state_gdn)
    real = lambda a: a.reshape(bs, SAMPLE_SLOTS, -1)[:, lo:hi]
    new_k_s = kwin.reshape(bs, WINDOW, ATTN_KV_HEADS, HEAD_DIM)
    new_v_s = vwin.reshape(bs, WINDOW, ATTN_KV_HEADS, HEAD_DIM)
    conv_s = u_s.reshape(bs, SAMPLE_SLOTS, CONV_DIM)[:, hi - (CONV_W - 1):hi]

    x1_p, route_p = _post(attn_p, gdn_p, xp, wts)
    x1_s, route_s = _post(real(attn_s).reshape(bs * ts, Q_COLS), real(gdn_s).reshape(bs * ts, Z_COLS),
                          x_sample.reshape(bs * ts, D_MODEL), wts)
    y_p, y_s = _moe([x1_p, x1_s], [route_p, route_s], wts)
    return (y_p.reshape(bp, sp, D_MODEL), y_s.reshape(bs, ts, D_MODEL), new_k_p, new_v_p, s_p, conv_p,
            new_k_s, new_v_s, s_s, conv_s)


def kernel(x_prompt, x_sample, cache_attn_k, cache_attn_v, state_gdn, state_conv, w_in, w_out,
           attn_sinks, conv_w, a_log, dt_bias, gdn_norm_w, ln1_g, ln1_b, w_router_group,
           w_router_expert, w_gate, w_up, w_down, ln2_g, ln2_b):
    assert w_in.shape[0] == DEPTH
    l = 0
    wts = _prep_weights(w_in[l], w_out[l], conv_w[l], a_log[l], dt_bias[l], gdn_norm_w[l], ln1_g[l],
                        ln1_b[l], w_router_group[l], w_router_expert[l], w_gate[l], w_up[l],
                        w_down[l], ln2_g[l], ln2_b[l])
    wts['sinks'] = attn_sinks[l]
    outs = _layer(x_prompt, x_sample, cache_attn_k[l], cache_attn_v[l], state_gdn[l], state_conv[l], wts)
    (y_p, y_s, k_p, v_p, s_p, c_p, k_s, v_s, s_s, c_s) = outs
    add = lambda a: a[None]
    return (y_p, y_s, add(k_p), add(v_p), add(s_p), add(c_p), add(k_s), add(v_s), add(s_s), add(c_s))
```

```python
import functools
import math

import jax
import jax.numpy as jnp
from jax import lax
from jax.experimental import pallas as pl
from jax.experimental.pallas import tpu as pltpu

F32 = jnp.float32
BF16 = jnp.bfloat16
I32 = jnp.int32

D_MODEL = 1024
ATTN_HEADS = 8
ATTN_KV_HEADS = 2
HEAD_DIM = 64
WINDOW = 128
ROT_DIM = HEAD_DIM // 4
ROPE_THETA = 500000.0
GDN_HEADS = 4
GDN_DK = 128
GDN_DV = 128
CONV_W = 4
QK_COLS = GDN_HEADS * GDN_DK
CONV_DIM = 2 * QK_COLS + GDN_HEADS * GDN_DV
Z_COLS = GDN_HEADS * GDN_DV
Q_COLS = ATTN_HEADS * HEAD_DIM
KV_COLS = ATTN_KV_HEADS * HEAD_DIM
N_GROUPS = 4
EXPERTS_PER_GROUP = 8
N_EXPERTS = N_GROUPS * EXPERTS_PER_GROUP
TOP_K = 2
EXPERT_FF = 256
NORM_EPS = 1e-5
L2_EPS = 1e-6
DEPTH = 1
DEEPNORM_ALPHA = (2 * DEPTH) ** 0.25
PAST_LEN = 8192

LANES = 128
SUBLANES = 8
TOK_TILE = 512
PROJ_TILE = 512
PROJ_SUB = 128
POST_SUB = 256
GDN_CHUNK = 128
GDN_SEQ_PER_STEP = 4
ATTN_BLOCKS_PER_STEP = 4
ATTN_SEQS_PER_STEP = 8
INV_BASE = 16
SAMPLE_SLOTS = 8
SAMPLE_FIRST = CONV_W - 1
ROW_TILE = 512
EXPERT_SUB = 256
SLAB_UNROLL = 4
PERM_ROWS = TOP_K * TOK_TILE + N_EXPERTS * SUBLANES
PERM_SLABS = PERM_ROWS // SUBLANES
XS_WORDS = D_MODEL + LANES
ZERO_TABLE = N_EXPERTS * (ROW_TILE // SUBLANES)
VMEM_LIMIT = 48 * 1024 * 1024
NEG_BIG = -1e30


def _cparams(sem):
    return pltpu.CompilerParams(dimension_semantics=sem, vmem_limit_bytes=VMEM_LIMIT)


def _bdot(a, b):
    return jnp.dot(a.astype(BF16), b.astype(BF16), preferred_element_type=F32)


def _bdot_nt(a, b):
    return lax.dot_general(a.astype(BF16), b.astype(BF16), (((1,), (1,)), ((), ())),
                           preferred_element_type=F32)


def _bdot_tn(a, b):
    return lax.dot_general(a.astype(BF16), b.astype(BF16), (((0,), (0,)), ((), ())),
                           preferred_element_type=F32)


def _div_pow2(x, n):
    return jnp.right_shift(x, int(math.log2(n)))


def _mod_pow2(x, n):
    return jnp.bitwise_and(x, n - 1)


def _split3(x):
    hi = x.astype(BF16)
    r = x - hi.astype(F32)
    mid = r.astype(BF16)
    lo = (r - mid.astype(F32)).astype(BF16)
    return hi, mid, lo


def _dot_exact_lhs01(m01, x):
    hi, mid, lo = _split3(x)
    d = lambda t: jnp.dot(m01, t, preferred_element_type=F32)
    return d(hi) + d(mid) + d(lo)


def _sigmoid(x):
    return 1.0 / (1.0 + jnp.exp(-x))


def _silu(x):
    return x * _sigmoid(x)


def _softplus(x):
    return jnp.maximum(x, 0.0) + jnp.log1p(jnp.exp(-jnp.abs(x)))


def _layer_norm(h, g, b):
    mu = jnp.mean(h, axis=-1, keepdims=True)
    d = h - mu
    var = jnp.mean(d * d, axis=-1, keepdims=True)
    return d * lax.rsqrt(var + NORM_EPS) * g + b


def _proj_kernel(*refs, tm, has_hist, full_u, one_segment):
    it = iter(refs)
    x_ref, cos_ref, sin_ref = next(it), next(it), next(it)
    wqkv_ref, wg_ref, wz_ref, wab_ref = next(it), next(it), next(it), next(it)
    convw_ref, alog_ref, dtb_ref, tri_ref, seg_ref = next(it), next(it), next(it), next(it), next(it)
    hist_ref = valid_ref = None
    if has_hist:
        hist_ref, valid_ref = next(it), next(it)
    q_ref, k_ref, v_ref = next(it), next(it), next(it)
    qg_ref, kg_ref, vg_ref, z_ref, gcb_ref, u_ref = (next(it) for _ in range(6))
    ubuf = next(it)

    t = pl.program_id(1)
    sub = PROJ_SUB
    rows = [slice(j * sub, (j + 1) * sub) for j in range(tm // sub)]
    lane = lax.broadcasted_iota(I32, (sub, LANES), 1)
    first_half = _mod_pow2(lane, HEAD_DIM) < (ROT_DIM // 2)

    @pl.when(t == 0)
    def _():
        ubuf[0:SUBLANES, :] = jnp.zeros((SUBLANES, CONV_DIM), F32)

    @pl.when(t > 0)
    def _():
        ubuf[0:SUBLANES, :] = ubuf[tm:tm + SUBLANES, :]

    dots = []
    for r in rows:
        xb = x_ref[r, :].astype(BF16)
        dots.append([jnp.dot(xb, w[...], preferred_element_type=F32)
                     for w in (wqkv_ref, wg_ref, wz_ref, wab_ref)])

    def l2n(s):
        return s * lax.rsqrt(jnp.sum(s * s, axis=1, keepdims=True) + L2_EPS)

    for r, (pq, u, z, ab) in zip(rows, dots):
        cosv, sinv = cos_ref[r, :], sin_ref[r, :]

        def rope(s):
            sw = jnp.where(first_half, pltpu.roll(s, LANES - ROT_DIM // 2, axis=1),
                           pltpu.roll(s, ROT_DIM // 2, axis=1))
            return s * cosv + sw * sinv

        for j in range(Q_COLS // LANES):
            q_ref[r, j * LANES:(j + 1) * LANES] = rope(pq[:, j * LANES:(j + 1) * LANES])
        k_ref[r, :] = rope(pq[:, Q_COLS:Q_COLS + KV_COLS])
        v_ref[r, :] = pq[:, Q_COLS + KV_COLS:Q_COLS + 2 * KV_COLS]
        z_ref[r, :] = z

        if has_hist:
            u = u + hist_ref[r, :]
        if full_u:
            u_ref[r, :] = u
        elif r.stop == tm:
            u_ref[...] = u[sub - SUBLANES:, :]
        base = SUBLANES + r.start
        ubuf[base:base + sub, :] = u
        acc = u * convw_ref[CONV_W - 1:CONV_W, :]
        for j in range(1, CONV_W):
            acc = acc + ubuf[base - j:base - j + sub, :] * convw_ref[CONV_W - 1 - j:CONV_W - j, :]
        c = _silu(acc)
        if has_hist:
            c = c * valid_ref[r, :]
        for h in range(GDN_HEADS):
            sl = slice(h * GDN_DK, (h + 1) * GDN_DK)
            qg_ref[r, sl] = l2n(c[:, sl]) * (GDN_DK ** -0.5)
            kg_ref[r, sl] = l2n(c[:, QK_COLS + h * GDN_DK:QK_COLS + (h + 1) * GDN_DK])
        vg_ref[r, :] = c[:, 2 * QK_COLS:]

        g = -jnp.exp(alog_ref[...]) * _softplus(ab + dtb_ref[...])
        beta = _sigmoid(ab)
        if has_hist:
            g = g * valid_ref[r, :]
            beta = beta * valid_ref[r, :]
        g = jnp.where(lane < GDN_HEADS, g, 0.0)
        gc = _dot_exact_lhs01(tri_ref[...], g)
        if one_segment:
            gl = jnp.broadcast_to(gc[sub - 1:sub, :], (sub, LANES))
        else:
            gl = _dot_exact_lhs01(seg_ref[...], g)
        gcb_ref[r, :] = jnp.where(lane < GDN_HEADS, gc,
                                  jnp.where(lane < 2 * GDN_HEADS, beta,
                                            jnp.where(lane < 3 * GDN_HEADS,
                                                      pltpu.roll(gl, 2 * GDN_HEADS, axis=1), 0.0)))


def _rope_tables(pos):
    half = ROT_DIM // 2
    inv_freq = ROPE_THETA ** (-jnp.arange(half, dtype=F32) * 2.0 / ROT_DIM)
    ang = pos.astype(F32)[:, None] * inv_freq[None, :]
    cos, sin = jnp.cos(ang), jnp.sin(ang)
    p = pos.shape[0]
    cpat = jnp.concatenate([cos, cos, jnp.ones((p, HEAD_DIM - ROT_DIM), F32)], axis=1)
    spat = jnp.concatenate([-sin, sin, jnp.zeros((p, HEAD_DIM - ROT_DIM), F32)], axis=1)
    return jnp.tile(cpat, (1, LANES // HEAD_DIM)), jnp.tile(spat, (1, LANES // HEAD_DIM))


def _segment_matrices(tm, seg_len):
    i = jnp.arange(tm)
    same = (i[:, None] // seg_len) == (i[None, :] // seg_len)
    tri = same & (i[None, :] <= i[:, None])
    return tri.astype(BF16), same.astype(BF16)


def _proj(x, pos, wts, seg_len, n_seq, hist=None, valid=None):
    n = x.shape[0]
    rows = n // n_seq
    tm = min(PROJ_TILE, rows)
    nt = rows // tm
    has_hist = hist is not None
    cos_t, sin_t = _rope_tables(pos)
    tri, seg = _segment_matrices(PROJ_SUB, seg_len)

    tok = lambda w: pl.BlockSpec((tm, w), lambda b, t: (b * nt + t, 0))
    const = lambda a: pl.BlockSpec(a.shape, lambda b, t: (0,) * a.ndim)
    in_arrays = [x, cos_t, sin_t, wts['wqkv'], wts['wg'], wts['wz'], wts['wab'],
                 wts['convw'], wts['alog'], wts['dtb'], tri, seg]
    in_specs = [tok(D_MODEL), pl.BlockSpec((tm, LANES), lambda b, t: (t, 0)),
                pl.BlockSpec((tm, LANES), lambda b, t: (t, 0))] + [const(a) for a in in_arrays[3:]]
    if has_hist:
        in_arrays += [hist, valid]
        in_specs += [tok(CONV_DIM), tok(1)]
    u_rows = n if has_hist else (n // tm) * SUBLANES
    u_block = tm if has_hist else SUBLANES
    out_shape = [jax.ShapeDtypeStruct((n, Q_COLS), F32), jax.ShapeDtypeStruct((n, KV_COLS), F32),
                 jax.ShapeDtypeStruct((n, KV_COLS), F32), jax.ShapeDtypeStruct((n, QK_COLS), F32),
                 jax.ShapeDtypeStruct((n, QK_COLS), F32), jax.ShapeDtypeStruct((n, Z_COLS), F32),
                 jax.ShapeDtypeStruct((n, Z_COLS), F32), jax.ShapeDtypeStruct((n, LANES), F32),
                 jax.ShapeDtypeStruct((u_rows, CONV_DIM), F32)]
    out_specs = [tok(Q_COLS), tok(KV_COLS), tok(KV_COLS), tok(QK_COLS), tok(QK_COLS), tok(Z_COLS),
                 tok(Z_COLS), tok(LANES),
                 pl.BlockSpec((u_block, CONV_DIM), lambda b, t: (b * nt + t, 0))]
    return pl.pallas_call(
        functools.partial(_proj_kernel, tm=tm, has_hist=has_hist, full_u=has_hist,
                          one_segment=seg_len == PROJ_SUB),
        out_shape=out_shape, grid=(n_seq, nt), in_specs=in_specs, out_specs=out_specs,
        scratch_shapes=[pltpu.VMEM((tm + SUBLANES, CONV_DIM), F32)],
        compiler_params=_cparams(("arbitrary", "arbitrary")),
        name="proj_hist" if has_hist else "proj",
    )(*in_arrays)


def _attn_blocks(qs, kcats, vcats, biases, sink, tq):
    lane = lax.broadcasted_iota(I32, (tq, LANES), 1)
    low = lane < HEAD_DIM
    n_slab = Q_COLS // LANES

    def stack(q):
        slabs = [q[:, j * LANES:(j + 1) * LANES] * (HEAD_DIM ** -0.5) for j in range(n_slab)]
        parts = ([jnp.where(low, s, 0.0) for s in slabs] + [jnp.where(low, 0.0, s) for s in slabs])
        return jnp.concatenate(parts, axis=0).astype(BF16)

    def unstack(o8):
        return [jnp.where(low, o8[j * tq:(j + 1) * tq, :], o8[(n_slab + j) * tq:(n_slab + j + 1) * tq, :])
                for j in range(n_slab)]

    rows = ATTN_HEADS * tq
    half = rows // 2
    klow = lax.broadcasted_iota(I32, (2 * WINDOW, LANES), 1) < HEAD_DIM
    one = jnp.ones((), BF16)
    q8s = _each(stack, qs)
    ss = _each(lambda q8, kc, b: _bdot_nt(q8, kc) + b, q8s, kcats, biases)
    ms = _each(lambda s: jnp.maximum(jnp.broadcast_to(jnp.max(s, axis=1, keepdims=True), (rows, LANES)),
                                     sink), ss)
    ps = _each(lambda s, m: jnp.exp(s - jnp.concatenate([m, m], axis=1)).astype(BF16), ss, ms)
    pv0 = _each(lambda p, vc: jnp.dot(p[:half], jnp.where(klow, vc, one), preferred_element_type=F32),
                ps, vcats)
    pv1 = _each(lambda p, vc: jnp.dot(p[half:], jnp.where(klow, one, vc), preferred_element_type=F32),
                ps, vcats)
    pvs = _each(lambda a, b: jnp.concatenate([a, b], axis=0), pv0, pv1)
    o8s = _each(lambda pv, m: pv / (pltpu.roll(pv, HEAD_DIM, axis=1) + jnp.exp(sink - m)), pvs, ms)
    return _each(unstack, o8s)


def _attn_prompt_kernel(q_ref, kc_ref, vc_ref, kp_ref, vp_ref, bias0_ref, bias_ref, sink_ref, o_ref, *,
                        nblk):
    kall = jnp.concatenate([kp_ref[...], kc_ref[...]], axis=0).astype(BF16)
    vall = jnp.concatenate([vp_ref[...], vc_ref[...]], axis=0).astype(BF16)
    win = lambda a, j: a[j * WINDOW:(j + 2) * WINDOW, :]
    qs = [q_ref[j * WINDOW:(j + 1) * WINDOW, :] for j in range(nblk)]
    biases = [bias0_ref[0]] + [bias_ref[...]] * (nblk - 1)
    outs = _attn_blocks(qs, [win(kall, j) for j in range(nblk)], [win(vall, j) for j in range(nblk)],
                        biases, sink_ref[...], WINDOW)
    for j, slabs in enumerate(outs):
        for c, slab in enumerate(slabs):
            o_ref[j * WINDOW:(j + 1) * WINDOW, c * LANES:(c + 1) * LANES] = slab


def _attn_sample_kernel(q_ref, kc_ref, vc_ref, kp_ref, vp_ref, bias_ref, sink_ref, o_ref, kw_ref, vw_ref,
                        *, nseq, n_new):
    tq = SAMPLE_SLOTS
    zpad = jnp.zeros((WINDOW - tq, LANES), F32)
    rows = lambda ref, j: ref[j * tq:(j + 1) * tq, :]
    cat = lambda pref, cref, j: jnp.concatenate([pref[j], rows(cref, j), zpad], axis=0).astype(BF16)
    outs = _attn_blocks([rows(q_ref, j) for j in range(nseq)],
                        [cat(kp_ref, kc_ref, j) for j in range(nseq)],
                        [cat(vp_ref, vc_ref, j) for j in range(nseq)],
                        [bias_ref[...]] * nseq, sink_ref[...], tq)
    for j, slabs in enumerate(outs):
        for c, slab in enumerate(slabs):
            o_ref[j * tq:(j + 1) * tq, c * LANES:(c + 1) * LANES] = slab
    row = lax.broadcasted_iota(I32, (WINDOW, LANES), 0)
    keep = WINDOW - n_new
    for pref, cref, wref in ((kp_ref, kc_ref, kw_ref), (vp_ref, vc_ref, vw_ref)):
        for j in range(nseq):
            new = jnp.concatenate([rows(cref, j), zpad], axis=0)
            wref[j] = jnp.where(row < keep, pltpu.roll(pref[j], keep, axis=0),
                                pltpu.roll(new, keep - SAMPLE_FIRST, axis=0))


def _sink_rows(sinks, tq):
    return jnp.broadcast_to(jnp.repeat(sinks.astype(F32), tq)[:, None], (ATTN_HEADS * tq, LANES))


def _attn_bias(tq, q_off, k_lo, k_hi, has_prev):
    qi = (jnp.arange(ATTN_HEADS * tq, dtype=I32) % tq)[:, None]
    c = jnp.arange(2 * WINDOW, dtype=I32)[None, :]
    cj = c - WINDOW
    vis_prev = (c < WINDOW) & (c > qi - q_off) & has_prev
    vis_cur = (c >= WINDOW) & (cj <= qi) & (cj >= k_lo) & (cj <= k_hi)
    return jnp.where(vis_prev | vis_cur, 0.0, NEG_BIG).astype(F32)


def _attn_prompt(q, k, v, sinks, n_seq):
    n = q.shape[0]
    nb = n // n_seq // WINDOW
    nblk = min(ATTN_BLOCKS_PER_STEP, nb)
    steps = nb // nblk
    tq = nblk * WINDOW
    cur = lambda w: pl.BlockSpec((tq, w), lambda b, i: (b * steps + i, 0))
    prev = pl.BlockSpec((WINDOW, LANES), lambda b, i: (b * nb + jnp.maximum(i * nblk - 1, 0), 0))
    bias2 = jnp.stack([_attn_bias(WINDOW, 0, 0, WINDOW - 1, False),
                       _attn_bias(WINDOW, 0, 0, WINDOW - 1, True)])
    rows = ATTN_HEADS * WINDOW
    return pl.pallas_call(
        functools.partial(_attn_prompt_kernel, nblk=nblk),
        out_shape=jax.ShapeDtypeStruct((n, Q_COLS), F32), grid=(n_seq, steps),
        in_specs=[cur(Q_COLS), cur(LANES), cur(LANES), prev, prev,
                  pl.BlockSpec((1, rows, 2 * WINDOW), lambda b, i: (jnp.minimum(i, 1), 0, 0)),
                  pl.BlockSpec((rows, 2 * WINDOW), lambda b, i: (0, 0)),
                  pl.BlockSpec((rows, LANES), lambda b, i: (0, 0))],
        out_specs=cur(Q_COLS),
        compiler_params=_cparams(("arbitrary", "arbitrary")), name="attn_prompt",
    )(q, k, v, k, v, bias2, bias2[1], _sink_rows(sinks, WINDOW))


def _attn_sample(q, k, v, cache_k, cache_v, sinks, n_seq, n_new):
    tq = SAMPLE_SLOTS
    nseq = min(ATTN_SEQS_PER_STEP, n_seq)
    cur = lambda w: pl.BlockSpec((nseq * tq, w), lambda b: (b, 0))
    prev = pl.BlockSpec((nseq, WINDOW, LANES), lambda b: (b, 0, 0))
    bias = _attn_bias(tq, SAMPLE_FIRST, SAMPLE_FIRST, SAMPLE_FIRST + 3, True)
    win = jax.ShapeDtypeStruct((n_seq, WINDOW, LANES), F32)
    return pl.pallas_call(
        functools.partial(_attn_sample_kernel, nseq=nseq, n_new=n_new),
        out_shape=[jax.ShapeDtypeStruct((n_seq * tq, Q_COLS), F32), win, win], grid=(n_seq // nseq,),
        in_specs=[cur(Q_COLS), cur(LANES), cur(LANES), prev, prev,
                  pl.BlockSpec(bias.shape, lambda b: (0, 0)),
                  pl.BlockSpec((ATTN_HEADS * tq, LANES), lambda b: (0, 0))],
        out_specs=[cur(Q_COLS), prev, prev],
        compiler_params=_cparams(("arbitrary",)), name="attn_sample",
    )(q, k, v, cache_k, cache_v, bias, _sink_rows(sinks, tq))


def _each(f, *lists):
    return [f(*args) for args in zip(*lists)]


def _unit_lower_inverse(ms, eye, same_base):
    c = ms[0].shape[0]

    def neumann(q0s, n_factors):
        xs = _each(lambda q: eye + q, q0s)
        if n_factors == 1:
            return xs
        qs = _each(_bdot, q0s, q0s)
        for _ in range(n_factors - 2):
            prods = _each(lambda x, q: _bdot(jnp.concatenate([x, q], axis=0), q), xs, qs)
            xs = _each(lambda x, pr: x + pr[:c], xs, prods)
            qs = _each(lambda pr: pr[c:], prods)
        return _each(lambda x, q: x + _bdot(x, q), xs, qs)

    ds = _each(lambda m: jnp.where(same_base, m, 0.0), ms)
    xs = neumann(_each(lambda d: -d, ds), int(math.log2(INV_BASE)))
    nblk = c // INV_BASE
    if nblk == 1:
        return xs
    ls = _each(lambda m, d: m - d, ms, ds)
    ns = _each(lambda x, l: -_bdot(x, l), xs, ls)
    ys = neumann(ns, int(math.log2(nblk)))
    return _each(_bdot, ys, xs)


def _gdn_intra(qs, ks, vs, gcs, gls, betas, same_seq, low_incl, low_strict, eye, same_base):
    del same_seq
    e_gcs = _each(jnp.exp, gcs)

    def decay_of(gc):
        gc_row = jnp.sum(jnp.where(eye > 0, gc, 0.0), axis=0, keepdims=True)
        return jnp.where(low_incl, jnp.exp(jnp.where(low_incl, gc - gc_row, 0.0)), 0.0)

    c = qs[0].shape[0]
    decays = _each(decay_of, gcs)
    kbs = _each(lambda k, b: k * b, ks, betas)
    vbs = _each(lambda v, b: v * b, vs, betas)
    kqs = _each(lambda kb, q, k: _bdot_nt(jnp.concatenate([kb, q], axis=0), k), kbs, qs, ks)
    ms = _each(lambda kq, d: jnp.where(low_strict, kq[:c] * d, 0.0), kqs, decays)
    attns = _each(lambda kq, d: jnp.where(low_incl, kq[c:] * d, 0.0), kqs, decays)
    tmats = _unit_lower_inverse(ms, eye, same_base)
    uws = _each(lambda t, vb, kb, e: _bdot(t, jnp.concatenate([vb, kb * e], axis=1)),
                tmats, vbs, kbs, e_gcs)
    us = _each(lambda uw: uw[:, :GDN_DV], uws)
    ws = _each(lambda uw: uw[:, GDN_DV:], uws)
    q_decs = _each(lambda q, e: q * e, qs, e_gcs)
    k_decs = _each(lambda k, gl, gc: k * jnp.exp(gl - gc), ks, gls, gcs)
    return us, ws, attns, q_decs, k_decs


def _chunk_masks(c, seq_len):
    i = lax.broadcasted_iota(I32, (c, c), 0)
    j = lax.broadcasted_iota(I32, (c, c), 1)
    same_seq = _div_pow2(i, seq_len) == _div_pow2(j, seq_len)
    low_incl = same_seq & (i >= j)
    low_strict = same_seq & (i > j)
    eye = (i == j).astype(F32)
    same_base = _div_pow2(i, INV_BASE) == _div_pow2(j, INV_BASE)
    return same_seq, low_incl, low_strict, eye, same_base


def _gated_rms(o, z, nw):
    o = o * lax.rsqrt(jnp.mean(o * o, axis=1, keepdims=True) + NORM_EPS) * nw
    return o * _silu(z)


def _gdn_prompt_kernel(qg_ref, kg_ref, vg_ref, z_ref, gcb_ref, nw_ref, o_ref, s_out_ref, s_scr):
    c = GDN_CHUNK
    n = pl.program_id(1)

    @pl.when(n == 0)
    def _():
        s_scr[...] = jnp.zeros_like(s_scr)

    masks = _chunk_masks(c, c)
    nw = nw_ref[...]
    chains = [(b, h) for b in range(qg_ref.shape[0]) for h in range(GDN_HEADS)]
    hs = lambda h: slice(h * GDN_DK, (h + 1) * GDN_DK)
    col = lambda off: [gcb_ref[b, :, off + h:off + h + 1] for b, h in chains]
    gcs, betas, gls = col(0), col(GDN_HEADS), col(2 * GDN_HEADS)
    qs = [qg_ref[b, :, hs(h)] for b, h in chains]
    ks = [kg_ref[b, :, hs(h)] for b, h in chains]
    vs = [vg_ref[b, :, hs(h)] for b, h in chains]
    us, ws, attns, q_decs, k_decs = _gdn_intra(qs, ks, vs, gcs, gls, betas, *masks)
    ss = [s_scr[b, h] for b, h in chains]
    wqs = _each(lambda w, qd, s: _bdot(jnp.concatenate([w, qd], axis=0), s), ws, q_decs, ss)
    wss = _each(lambda wq: wq[:c], wqs)
    qss = _each(lambda wq: wq[c:], wqs)
    v_news = _each(lambda u, x: u - x, us, wss)
    avs = _each(_bdot, attns, v_news)
    kvs = _each(_bdot_tn, k_decs, v_news)
    for (b, h), s, gl, qsv, av, kv in zip(chains, ss, gls, qss, avs, kvs):
        s_scr[b, h] = s * jnp.exp(gl[0:1, :]) + kv
        o_ref[b, :, hs(h)] = _gated_rms(qsv + av, z_ref[b, :, hs(h)], nw)

    @pl.when(n == pl.num_programs(1) - 1)
    def _():
        s_out_ref[...] = s_scr[...]


def _gdn_prompt(qg, kg, vg, z, gcb, norm_w, n_seq):
    n = qg.shape[0]
    s_len = n // n_seq
    nb = min(GDN_SEQ_PER_STEP, n_seq)
    v3 = lambda a: a.reshape(n_seq, s_len, a.shape[-1])
    tok = lambda w: pl.BlockSpec((nb, GDN_CHUNK, w), lambda b, i: (b, i, 0))
    o, s = pl.pallas_call(
        _gdn_prompt_kernel,
        out_shape=[jax.ShapeDtypeStruct((n_seq, s_len, Z_COLS), F32),
                   jax.ShapeDtypeStruct((n_seq, GDN_HEADS, GDN_DK, GDN_DV), F32)],
        grid=(n_seq // nb, s_len // GDN_CHUNK),
        in_specs=[tok(QK_COLS), tok(QK_COLS), tok(Z_COLS), tok(Z_COLS), tok(LANES),
                  pl.BlockSpec((1, GDN_DV), lambda b, i: (0, 0))],
        out_specs=[tok(Z_COLS),
                   pl.BlockSpec((nb, GDN_HEADS, GDN_DK, GDN_DV), lambda b, i: (b, 0, 0, 0))],
        scratch_shapes=[pltpu.VMEM((nb, GDN_HEADS, GDN_DK, GDN_DV), F32)],
        compiler_params=_cparams(("arbitrary", "arbitrary")), name="gdn_prompt",
    )(v3(qg), v3(kg), v3(vg), v3(z), v3(gcb), norm_w)
    return o.reshape(n, Z_COLS), s


def _gdn_sample_kernel(qg_ref, kg_ref, vg_ref, z_ref, gcb_ref, nw_ref, s_in_ref, o_ref, s_out_ref):
    c = GDN_CHUNK
    n_sub = c // SAMPLE_SLOTS
    masks = _chunk_masks(c, SAMPLE_SLOTS)
    heads = range(GDN_HEADS)
    hs = lambda h: slice(h * GDN_DK, (h + 1) * GDN_DK)
    rs = lambda s: slice(s * SAMPLE_SLOTS, (s + 1) * SAMPLE_SLOTS)
    col = lambda off: [gcb_ref[:, off + h:off + h + 1] for h in heads]
    gcs, betas, gls = col(0), col(GDN_HEADS), col(2 * GDN_HEADS)
    us, ws, attns, q_decs, k_decs = _gdn_intra([qg_ref[:, hs(h)] for h in heads],
                                               [kg_ref[:, hs(h)] for h in heads],
                                               [vg_ref[:, hs(h)] for h in heads], gcs, gls, betas, *masks)
    pairs = [(h, s) for h in heads for s in range(n_sub)]
    sts = [s_in_ref[s, h] for h, s in pairs]
    boths = [jnp.concatenate([ws[h][rs(s), :], q_decs[h][rs(s), :]], axis=0) for h, s in pairs]
    rr = _each(_bdot, boths, sts)
    gather = lambda h, part: jnp.concatenate(
        [rr[h * n_sub + s][part * SAMPLE_SLOTS:(part + 1) * SAMPLE_SLOTS, :] for s in range(n_sub)], axis=0)
    v_news = [us[h] - gather(h, 0) for h in heads]
    avs = _each(_bdot, attns, v_news)
    row = lax.broadcasted_iota(I32, (c, LANES), 0)
    seq_of_row = _div_pow2(row, SAMPLE_SLOTS)
    kds = [jnp.where(seq_of_row == s, k_decs[h], 0.0) for h, s in pairs]
    kvs = _each(_bdot_tn, kds, [v_news[h] for h, _ in pairs])
    egls = _each(jnp.exp, gls)
    for (h, s), st, kv in zip(pairs, sts, kvs):
        s_out_ref[s, h] = st * egls[h][s * SAMPLE_SLOTS:s * SAMPLE_SLOTS + 1, :] + kv
    nw = nw_ref[...]
    for h in heads:
        o_ref[:, hs(h)] = _gated_rms(gather(h, 1) + avs[h], z_ref[:, hs(h)], nw)


def _gdn_sample(qg, kg, vg, z, gcb, norm_w, state):
    n = qg.shape[0]
    n_sub = GDN_CHUNK // SAMPLE_SLOTS
    tok = lambda w: pl.BlockSpec((GDN_CHUNK, w), lambda i: (i, 0))
    st = pl.BlockSpec((n_sub, GDN_HEADS, GDN_DK, GDN_DV), lambda i: (i, 0, 0, 0))
    return pl.pallas_call(
        _gdn_sample_kernel,
        out_shape=[jax.ShapeDtypeStruct((n, Z_COLS), F32),
                   jax.ShapeDtypeStruct(state.shape, F32)],
        grid=(n // GDN_CHUNK,),
        in_specs=[tok(QK_COLS), tok(QK_COLS), tok(Z_COLS), tok(Z_COLS), tok(LANES),
                  pl.BlockSpec((1, GDN_DV), lambda i: (0, 0)), st],
        out_specs=[tok(Z_COLS), st],
        compiler_params=_cparams(("arbitrary",)), name="gdn_sample",
    )(qg, kg, vg, z, gcb, norm_w, state)


def _post_kernel(a_ref, g_ref, x_ref, wo_ref, ln_g_ref, ln_b_ref, wr_ref, x1_ref, route_ref, *, tm):
    sub = POST_SUB
    rows = [slice(j * sub, (j + 1) * sub) for j in range(tm // sub)]
    d = lambda a, b: jnp.dot(a, b, preferred_element_type=F32)
    mixes = [d(a_ref[r, :].astype(BF16), wo_ref[0:Q_COLS, :]) + d(g_ref[r, :].astype(BF16), wo_ref[Q_COLS:, :])
             for r in rows]
    x1s = [_layer_norm(DEEPNORM_ALPHA * x_ref[r, :] + mix, ln_g_ref[...], ln_b_ref[...])
           for r, mix in zip(rows, mixes)]
    for r, x1 in zip(rows, x1s):
        x1_ref[r, :] = x1
    wh, wm = wr_ref[0], wr_ref[1]
    lgs = []
    for x1 in x1s:
        xh = x1.astype(BF16)
        xm = (x1 - xh.astype(F32)).astype(BF16)
        lgs.append(d(xh, wh) + d(xh, wm) + d(xm, wh))
    for r, lg in zip(rows, lgs):
        route_ref[:, r] = jnp.transpose(_route(lg))[0:SUBLANES, :]


def _route(lg):
    lane = lax.broadcasted_iota(I32, lg.shape, 1)
    lane_f = lane.astype(F32)
    big = float(LANES)

    def first_max(vals, mask):
        v = jnp.where(mask, vals, NEG_BIG)
        mx = jnp.max(v, axis=1, keepdims=True)
        idx = jnp.min(jnp.where(mask & (v == mx), lane_f, big), axis=1, keepdims=True)
        return mx, idx

    gmask = lane < N_GROUPS
    gmax, gidx = first_max(lg, gmask)
    gden = jnp.sum(jnp.where(gmask, jnp.exp(lg - gmax), 0.0), axis=1, keepdims=True)
    g_top_p = 1.0 / gden
    e_lane = lane - N_GROUPS
    e_group = _div_pow2(jnp.maximum(e_lane, 0), EXPERTS_PER_GROUP).astype(F32)
    emask = (e_lane >= 0) & (e_lane < N_EXPERTS) & (e_group == gidx)
    m1, i1 = first_max(lg, emask)
    eden = jnp.sum(jnp.where(emask, jnp.exp(lg - m1), 0.0), axis=1, keepdims=True)
    m2, i2 = first_max(lg, emask & (lane_f != i1))
    p1 = 1.0 / eden
    p2 = jnp.exp(m2 - m1) / eden
    tot = p1 + p2
    gate1 = g_top_p * (p1 / tot)
    gate2 = g_top_p * (p2 / tot)
    return jnp.where(lane == 0, gate1,
                     jnp.where(lane == 1, gate2,
                               jnp.where(lane == 2, i1 - N_GROUPS,
                                         jnp.where(lane == 3, i2 - N_GROUPS, 0.0))))


def _post(attn_o, gdn_o, x, wts):
    n = x.shape[0]
    tm = min(PROJ_TILE, n)
    tok = lambda w: pl.BlockSpec((tm, w), lambda i: (i, 0))
    const = lambda a: pl.BlockSpec(a.shape, lambda i: (0,) * a.ndim)
    consts = [wts['wo'], wts['ln1_g'], wts['ln1_b'], wts['wr']]
    return pl.pallas_call(
        functools.partial(_post_kernel, tm=tm),
        out_shape=[jax.ShapeDtypeStruct((n, D_MODEL), F32), jax.ShapeDtypeStruct((SUBLANES, n), F32)],
        grid=(n // tm,),
        in_specs=[tok(Q_COLS), tok(Z_COLS), tok(D_MODEL)] + [const(a) for a in consts],
        out_specs=[tok(D_MODEL), pl.BlockSpec((SUBLANES, tm), lambda i: (0, i))],
        compiler_params=_cparams(("arbitrary",)), name="post_%d" % (n // tm),
    )(attn_o, gdn_o, x, *consts)


def _slab_loop(n, body):
    n_main = jnp.right_shift(n, int(math.log2(SLAB_UNROLL)))

    def main(i, c):
        for u in range(SLAB_UNROLL):
            body(i * SLAB_UNROLL + u, u)
        return c

    lax.fori_loop(0, n_main, main, 0)
    lax.fori_loop(n_main * SLAB_UNROLL, n, lambda j, c: (body(j, 0), c)[1], 0)


def _dispatch_kernel(dst_ref, nslab_ref, ztab_ref, zinfo_ref, slot_ref, gate_ref, *rest,
                     group_tiles, max_tiles):
    x_refs = rest[:len(group_tiles)]
    xs_ref, pbuf, sem, zbuf, zsem = rest[len(group_tiles):]
    n_tiles = sum(group_tiles)
    g = pl.program_id(0)
    cur = lax.rem(g, 2)

    def slab_copy(tile, buf_slot, j):
        d = pl.multiple_of(dst_ref[tile * PERM_SLABS + j], SUBLANES)
        src = pbuf.at[buf_slot, pl.ds(pl.multiple_of(j * SUBLANES, SUBLANES), SUBLANES), :]
        return pltpu.make_async_copy(src, xs_ref.at[pl.ds(d, SUBLANES), :], sem.at[buf_slot])

    def tail_copy(k):
        d = pl.multiple_of(ztab_ref[k], SUBLANES)
        return pltpu.make_async_copy(zbuf.at[pl.ds(0, SUBLANES), :], xs_ref.at[pl.ds(d, SUBLANES), :], zsem)

    def tile_copy(t):
        d = pl.multiple_of(t * ROW_TILE, ROW_TILE)
        return pltpu.make_async_copy(zbuf, xs_ref.at[pl.ds(d, ROW_TILE), :], zsem)

    @pl.when(g == 0)
    def _():
        zbuf[...] = jnp.zeros_like(zbuf)

    share = -(-ZERO_TABLE // n_tiles)
    k0 = g * share
    _slab_loop(jnp.clip(zinfo_ref[0] - k0, 0, share), lambda j, u: tail_copy(k0 + j).start(priority=1))

    @pl.when(zinfo_ref[1] + g < max_tiles)
    def _():
        tile_copy(zinfo_ref[1] + g).start(priority=1)

    x = x_refs[-1][...]
    bound = n_tiles
    for x_ref, nt in zip(x_refs[-2::-1], group_tiles[:0:-1]):
        bound -= nt
        x = jnp.where(g < bound, x_ref[...], x)

    r = lax.broadcasted_iota(I32, (PERM_ROWS, TOK_TILE), 0)
    sl = slot_ref[0]
    hit0, hit1 = r == sl[0:1, :], r == sl[1:2, :]
    onehot = jnp.where(hit0 | hit1, 1.0, 0.0).astype(BF16)
    gt = gate_ref[0]
    gcol = jnp.sum(jnp.where(hit0, gt[0:1, :], 0.0) + jnp.where(hit1, gt[1:2, :], 0.0),
                   axis=1, keepdims=True)
    pbuf[cur, :, 0:D_MODEL] = jnp.dot(onehot, x.astype(BF16), preferred_element_type=F32)
    pbuf[cur, :, D_MODEL:] = jnp.broadcast_to(gcol, (PERM_ROWS, LANES))

    @pl.when(g > 0)
    def _():
        _slab_loop(nslab_ref[g - 1], lambda j, u: slab_copy(g - 1, 1 - cur, j).wait())

    _slab_loop(nslab_ref[g], lambda j, u: slab_copy(g, cur, j).start(priority=u % 2))

    @pl.when(g == n_tiles - 1)
    def _():
        _slab_loop(nslab_ref[g], lambda j, u: slab_copy(g, cur, j).wait())
        lax.fori_loop(zinfo_ref[1] + n_tiles, max_tiles,
                      lambda t, c: (tile_copy(t).start(priority=1), c)[1], 0)
        _slab_loop(zinfo_ref[0], lambda k, u: tail_copy(k).wait())
        lax.fori_loop(zinfo_ref[1], max_tiles, lambda t, c: (tile_copy(t).wait(), c)[1], 0)


def _dispatch(plan, x1s, max_tiles):
    group_tiles = tuple(x1.shape[0] // TOK_TILE for x1 in x1s)
    n_tiles = sum(group_tiles)
    tile = lambda i, d, ns, zt, zi: (i, 0, 0)
    in_specs = [pl.BlockSpec((1, TOP_K, TOK_TILE), tile), pl.BlockSpec((1, TOP_K, TOK_TILE), tile)]
    base = 0
    for nt in group_tiles:
        in_specs.append(pl.BlockSpec(
            (TOK_TILE, D_MODEL),
            lambda i, d, ns, zt, zi, base=base, nt=nt: (jnp.clip(i - base, 0, nt - 1), 0)))
        base += nt
    return pl.pallas_call(
        functools.partial(_dispatch_kernel, group_tiles=group_tiles, max_tiles=max_tiles),
        out_shape=jax.ShapeDtypeStruct((max_tiles * ROW_TILE, XS_WORDS), F32),
        grid_spec=pltpu.PrefetchScalarGridSpec(
            num_scalar_prefetch=4, grid=(n_tiles,), in_specs=in_specs,
            out_specs=pl.BlockSpec(memory_space=pl.ANY),
            scratch_shapes=[pltpu.VMEM((2, PERM_ROWS, XS_WORDS), F32), pltpu.SemaphoreType.DMA((2,)),
                            pltpu.VMEM((ROW_TILE, XS_WORDS), F32), pltpu.SemaphoreType.DMA(())]),
        compiler_params=_cparams(("arbitrary",)), name="moe_dispatch",
    )(plan['slab_dst'], plan['nslab'], plan['ztab'], plan['zinfo'], plan['slot_rows'], plan['gate_rows'],
      *x1s)


def _expert_kernel(te_ref, nu_ref, xs_ref, wg_ref, wu_ref, wd_ref, ye_ref, wgu_scr, wd_scr):
    i = pl.program_id(0)
    active = i < nu_ref[0]

    @pl.when(active & ((i == 0) | (te_ref[i] != te_ref[jnp.maximum(i - 1, 0)])))
    def _():
        wgu_scr[:, 0:EXPERT_FF] = wg_ref[0].astype(BF16)
        wgu_scr[:, EXPERT_FF:] = wu_ref[0].astype(BF16)
        wd_scr[...] = wd_ref[0].astype(BF16)

    @pl.when(active)
    def _():
        sub = EXPERT_SUB
        rows = [slice(j * sub, (j + 1) * sub) for j in range(ROW_TILE // sub)]
        d = lambda a, b: jnp.dot(a, b, preferred_element_type=F32)
        wgu, wd = wgu_scr[...], wd_scr[...]
        xs = [xs_ref[r, 0:D_MODEL].astype(BF16) for r in rows]
        hs = [d(x, wgu) for x in xs]
        hhs = [(_silu(h[:, :EXPERT_FF]) * h[:, EXPERT_FF:]).astype(BF16) for h in hs]
        ys = [d(hh, wd) for hh in hhs]
        for r, y in zip(rows, ys):
            gate = xs_ref[r, D_MODEL:]
            ye_ref[r, :] = y * jnp.concatenate([gate] * (D_MODEL // LANES), axis=1)

    @pl.when(jnp.logical_not(active))
    def _():
        ye_ref[...] = jnp.zeros_like(ye_ref)


def _experts(tile_expert, n_used, xs, w_gate, w_up, w_down):
    n_tiles = xs.shape[0] // ROW_TILE
    row_in = lambda i, te, nu: (jnp.minimum(i, nu[0] - 1), 0)
    row = lambda i, te, nu: (i, 0)
    wsel = lambda i, te, nu: (te[i], 0, 0)
    return pl.pallas_call(
        _expert_kernel,
        out_shape=jax.ShapeDtypeStruct((xs.shape[0], D_MODEL), F32),
        grid_spec=pltpu.PrefetchScalarGridSpec(
            num_scalar_prefetch=2, grid=(n_tiles,),
            in_specs=[pl.BlockSpec((ROW_TILE, XS_WORDS), row_in),
                      pl.BlockSpec((1, D_MODEL, EXPERT_FF), wsel),
                      pl.BlockSpec((1, D_MODEL, EXPERT_FF), wsel),
                      pl.BlockSpec((1, EXPERT_FF, D_MODEL), wsel)],
            out_specs=pl.BlockSpec((ROW_TILE, D_MODEL), row),
            scratch_shapes=[pltpu.VMEM((D_MODEL, 2 * EXPERT_FF), BF16),
                            pltpu.VMEM((EXPERT_FF, D_MODEL), BF16)]),
        compiler_params=_cparams(("arbitrary",)), name="moe_experts",
    )(tile_expert, n_used, xs, w_gate, w_up, w_down)


def _combine_kernel(dst_ref, nslab_ref, x1_ref, slot_ref, ye_ref, ln_g_ref, ln_b_ref, y_ref,
                    buf, sem, *, tile_base, n_tiles):
    i = pl.program_id(0)
    g = tile_base + i
    cur = lax.rem(i, 2)

    def slab_copy(tile, buf_slot, j):
        d = pl.multiple_of(dst_ref[tile * PERM_SLABS + j], SUBLANES)
        dst = buf.at[buf_slot, pl.ds(pl.multiple_of(j * SUBLANES, SUBLANES), SUBLANES), :]
        return pltpu.make_async_copy(ye_ref.at[pl.ds(d, SUBLANES), :], dst, sem.at[buf_slot])

    @pl.when(i == 0)
    def _():
        buf[...] = jnp.zeros_like(buf)
        _slab_loop(nslab_ref[g], lambda j, u: slab_copy(g, cur, j).start(priority=u % 2))

    @pl.when(i + 1 < n_tiles)
    def _():
        _slab_loop(nslab_ref[g + 1], lambda j, u: slab_copy(g + 1, 1 - cur, j).start(priority=u % 2))

    _slab_loop(nslab_ref[g], lambda j, u: slab_copy(g, cur, j).wait())

    col = lax.broadcasted_iota(I32, (TOK_TILE, PERM_ROWS), 1)
    sl = slot_ref[0]
    diag = (lax.broadcasted_iota(I32, (TOK_TILE, TOK_TILE), 0)
            == lax.broadcasted_iota(I32, (TOK_TILE, TOK_TILE), 1))
    as_col = lambda row: jnp.sum(jnp.where(diag, row, 0), axis=1, keepdims=True)
    pick = jnp.where((col == as_col(sl[0:1, :])) | (col == as_col(sl[1:2, :])), 1.0, 0.0).astype(BF16)
    moe = jnp.dot(pick, buf[cur].astype(BF16), preferred_element_type=F32)
    y_ref[...] = _layer_norm(DEEPNORM_ALPHA * x1_ref[...] + moe, ln_g_ref[...], ln_b_ref[...])


def _combine(plan, tile_base, x1, ye, ln_g, ln_b):
    n = x1.shape[0]
    n_tiles = n // TOK_TILE
    tok = lambda w: pl.BlockSpec((TOK_TILE, w), lambda i, d, ns: (i, 0))
    const = lambda a: pl.BlockSpec(a.shape, lambda i, d, ns: (0,) * a.ndim)
    return pl.pallas_call(
        functools.partial(_combine_kernel, tile_base=tile_base, n_tiles=n_tiles),
        out_shape=jax.ShapeDtypeStruct((n, D_MODEL), F32),
        grid_spec=pltpu.PrefetchScalarGridSpec(
            num_scalar_prefetch=2, grid=(n_tiles,),
            in_specs=[tok(D_MODEL),
                      pl.BlockSpec((1, TOP_K, TOK_TILE), lambda i, d, ns: (tile_base + i, 0, 0)),
                      pl.BlockSpec(memory_space=pl.ANY), const(ln_g), const(ln_b)],
            out_specs=tok(D_MODEL),
            scratch_shapes=[pltpu.VMEM((2, PERM_ROWS, D_MODEL), F32), pltpu.SemaphoreType.DMA((2,))]),
        compiler_params=_cparams(("arbitrary",)), name="moe_combine_%d" % tile_base,
    )(plan['slab_dst'], plan['nslab'], x1, plan['slot_rows'], ye, ln_g, ln_b)


def _routing_plan(ids, gates):
    nt = ids.shape[1] // TOK_TILE
    pairs = TOP_K * TOK_TILE
    ex = jnp.arange(N_EXPERTS, dtype=I32)
    per_tile = lambda a: jnp.swapaxes(a.reshape(TOP_K, nt, TOK_TILE), 0, 1)
    flat = per_tile(ids).reshape(nt, pairs)
    onehot = (flat[:, None, :] == ex[None, :, None])
    p = jnp.arange(pairs, dtype=I32)
    triu = (p[:, None] <= p[None, :]).astype(BF16)
    csum = jnp.dot(onehot.astype(BF16).reshape(nt * N_EXPERTS, pairs), triu,
                   preferred_element_type=F32).astype(I32).reshape(nt, N_EXPERTS, pairs)
    oh = onehot.astype(I32)
    rank = jnp.sum(oh * (csum - 1), axis=1)
    cnt = csum[:, :, -1]
    cpad = (cnt + SUBLANES - 1) // SUBLANES * SUBLANES
    seg_end = jnp.cumsum(cpad, axis=1)
    seg_off = seg_end - cpad
    slot = jnp.sum(oh * seg_off[:, :, None], axis=1) + rank
    run_end = jnp.cumsum(cpad, axis=0)
    ntiles_e = (run_end[-1] + ROW_TILE - 1) // ROW_TILE
    tile_end = jnp.cumsum(ntiles_e)
    dst_run = ((tile_end - ntiles_e) * ROW_TILE)[None, :] + run_end - cpad
    j8 = jnp.arange(PERM_SLABS, dtype=I32) * SUBLANES
    e_of = jnp.minimum(jnp.sum((j8[None, :, None] >= seg_end[:, None, :]).astype(I32), axis=2),
                       N_EXPERTS - 1)
    sel = (e_of[:, :, None] == ex).astype(I32)
    slab_dst = jnp.sum(sel * (dst_run - seg_off)[:, None, :], axis=2) + j8[None, :]
    n_used = tile_end[-1]
    max_tiles = _max_row_tiles(ids.shape[1])
    t = jnp.arange(max_tiles, dtype=I32)
    te = jnp.sum((t[:, None] >= tile_end[None, :]).astype(I32), axis=1)
    te_last = jnp.sum((n_used - 1 >= tile_end).astype(I32))
    row_start = (tile_end - ntiles_e) * ROW_TILE
    tail_cnt = (ntiles_e * ROW_TILE - run_end[-1]) // SUBLANES
    tail_end = jnp.cumsum(tail_cnt)
    k = jnp.arange(ZERO_TABLE, dtype=I32)
    e_k = jnp.minimum(jnp.sum((k[:, None] >= tail_end[None, :]).astype(I32), axis=1), N_EXPERTS - 1)
    base_k = jnp.sum((e_k[:, None] == ex).astype(I32)
                     * (row_start + run_end[-1] - SUBLANES * (tail_end - tail_cnt))[None, :], axis=1)
    return dict(
        slab_dst=slab_dst.reshape(-1).astype(I32), nslab=(seg_end[:, -1] // SUBLANES).astype(I32),
        ztab=(base_k + SUBLANES * k).astype(I32), zinfo=jnp.stack([tail_end[-1], n_used]).astype(I32),
        slot_rows=slot.reshape(nt, TOP_K, TOK_TILE).astype(I32),
        gate_rows=per_tile(gates).astype(F32),
        tile_expert=jnp.where(t < n_used, jnp.minimum(te, N_EXPERTS - 1), te_last).astype(I32),
        n_used=n_used.reshape(1).astype(I32))


def _max_row_tiles(n_tokens):
    rows = TOP_K * n_tokens + (n_tokens // TOK_TILE) * N_EXPERTS * (SUBLANES - 1)
    return (rows + ROW_TILE - 1) // ROW_TILE + N_EXPERTS


def _moe(x1s, routes, wts):
    ids = jnp.concatenate([r[TOP_K:2 * TOP_K, :] for r in routes], axis=1).astype(I32)
    gates = jnp.concatenate([r[0:TOP_K, :] for r in routes], axis=1)
    plan = _routing_plan(ids, gates)
    max_tiles = _max_row_tiles(ids.shape[1])
    bases = [0]
    for x1 in x1s[:-1]:
        bases.append(bases[-1] + x1.shape[0] // TOK_TILE)
    xs = _dispatch(plan, x1s, max_tiles)
    ye = _experts(plan['tile_expert'], plan['n_used'], xs, wts['w_gate'], wts['w_up'], wts['w_down'])
    return [_combine(plan, base, x1, ye, wts['ln2_g'], wts['ln2_b']) for base, x1 in zip(bases, x1s)]


def _prep_weights(w_in, w_out, conv_w, a_log, dt_bias, gdn_norm_w, ln1_g, ln1_b, w_router_group,
                  w_router_expert, w_gate, w_up, w_down, ln2_g, ln2_b):
    o1 = Q_COLS + 2 * KV_COLS
    o2 = o1 + CONV_DIM
    o3 = o2 + Z_COLS
    pad_row = lambda v: jnp.pad(v.astype(F32), (0, LANES - v.shape[0]))[None, :]
    wab = jnp.pad(w_in[:, o3:], ((0, 0), (0, LANES - 2 * GDN_HEADS)))
    wr = jnp.pad(jnp.concatenate([w_router_group, w_router_expert], axis=1),
                 ((0, 0), (0, LANES - N_GROUPS - N_EXPERTS)))
    wr_hi = wr.astype(BF16)
    wr_mid = (wr - wr_hi.astype(F32)).astype(BF16)
    group = ATTN_HEADS // ATTN_KV_HEADS
    head_order = jnp.array([h for j in range(group) for h in (j, j + group)], I32)
    col_order = (head_order[:, None] * HEAD_DIM + jnp.arange(HEAD_DIM, dtype=I32)[None, :]).reshape(-1)
    wqkv = jnp.concatenate([w_in[:, :Q_COLS][:, col_order], w_in[:, Q_COLS:o1]], axis=1)
    wo = jnp.concatenate([w_out[:Q_COLS][col_order], w_out[Q_COLS:]], axis=0)
    return dict(
        wqkv=wqkv.astype(BF16), wg=w_in[:, o1:o2].astype(BF16), wz=w_in[:, o2:o3].astype(BF16),
        wab=wab.astype(BF16), convw=conv_w.astype(F32), alog=pad_row(a_log), dtb=pad_row(dt_bias),
        norm_w=gdn_norm_w.astype(F32)[None, :], wo=wo.astype(BF16),
        ln1_g=ln1_g[None, :], ln1_b=ln1_b[None, :], wr=jnp.stack([wr_hi, wr_mid]),
        w_gate=w_gate, w_up=w_up, w_down=w_down, ln2_g=ln2_g[None, :], ln2_b=ln2_b[None, :])


def _layer(x_prompt, x_sample, cache_k, cache_v, state_gdn, state_conv, wts):
    bp, sp, _ = x_prompt.shape
    bs, ts, _ = x_sample.shape
    n_p = bp * sp

    xp = x_prompt.reshape(n_p, D_MODEL)
    (q, k, v, qg, kg, vg, z, gcb, utail) = _proj(xp, jnp.arange(sp, dtype=I32), wts, GDN_CHUNK, bp)
    attn_p = _attn_prompt(q, k, v, wts['sinks'], bp)
    gdn_p, s_p = _gdn_prompt(qg, kg, vg, z, gcb, wts['norm_w'], bp)
    last_win = lambda a: a.reshape(bp, sp, KV_COLS)[:, sp - WINDOW:].reshape(bp, WINDOW, ATTN_KV_HEADS,
                                                                            HEAD_DIM)
    new_k_p, new_v_p = last_win(k), last_win(v)
    tiles_per_seq = sp // min(PROJ_TILE, sp)
    conv_p = utail.reshape(bp, tiles_per_seq, SUBLANES, CONV_DIM)[:, -1, SUBLANES - (CONV_W - 1):]

    lo, hi = SAMPLE_FIRST, SAMPLE_FIRST + ts
    xs_rows = jnp.pad(x_sample, ((0, 0), (lo, SAMPLE_SLOTS - hi), (0, 0))).reshape(bs * SAMPLE_SLOTS, D_MODEL)
    hist = jnp.pad(state_conv, ((0, 0), (0, SAMPLE_SLOTS - lo), (0, 0))).reshape(bs * SAMPLE_SLOTS, CONV_DIM)
    slot = jnp.arange(SAMPLE_SLOTS, dtype=I32)
    valid = jnp.tile(((slot >= lo) & (slot < hi)).astype(F32), bs)[:, None]
    pos_s = jnp.tile(PAST_LEN + slot - lo, bs)
    (q, k, v, qg, kg, vg, z, gcb, u_s) = _proj(xs_rows, pos_s, wts, SAMPLE_SLOTS, 1, hist, valid)
    ck = cache_k.reshape(bs, WINDOW, KV_COLS)
    cv = cache_v.reshape(bs, WINDOW, KV_COLS)
    attn_s, kwin, vwin = _attn_sample(q, k, v, ck, cv, wts['sinks'], bs, ts)
    gdn_s, s_s = _gdn_sample(qg, kg, vg, z, gcb, wts['norm_w'], state_gdn)
    real = lambda a: a.reshape(bs, SAMPLE_SLOTS, -1)[:, lo:hi]
    new_k_s = kwin.reshape(bs, WINDOW, ATTN_KV_HEADS, HEAD_DIM)
    new_v_s = vwin.reshape(bs, WINDOW, ATTN_KV_HEADS, HEAD_DIM)
    conv_s = u_s.reshape(bs, SAMPLE_SLOTS, CONV_DIM)[:, hi - (CONV_W - 1):hi]

    x1_p, route_p = _post(attn_p, gdn_p, xp, wts)
    x1_s, route_s = _post(real(attn_s).reshape(bs * ts, Q_COLS), real(gdn_s).reshape(bs * ts, Z_COLS),
                          x_sample.reshape(bs * ts, D_MODEL), wts)
    y_p, y_s = _moe([x1_p, x1_s], [route_p, route_s], wts)
    return (y_p.reshape(bp, sp, D_MODEL), y_s.reshape(bs, ts, D_MODEL), new_k_p, new_v_p, s_p, conv_p,
            new_k_s, new_v_s, s_s, conv_s)


def kernel(x_prompt, x_sample, cache_attn_k, cache_attn_v, state_gdn, state_conv, w_in, w_out,
           attn_sinks, conv_w, a_log, dt_bias, gdn_norm_w, ln1_g, ln1_b, w_router_group,
           w_router_expert, w_gate, w_up, w_down, ln2_g, ln2_b):
    assert w_in.shape[0] == DEPTH
    l = 0
    wts = _prep_weights(w_in[l], w_out[l], conv_w[l], a_log[l], dt_bias[l], gdn_norm_w[l], ln1_g[l],
                        ln1_b[l], w_router_group[l], w_router_expert[l], w_gate[l], w_up[l],
                        w_down[l], ln2_g[l], ln2_b[l])
    wts['sinks'] = attn_sinks[l]
    outs = _layer(x_prompt, x_sample, cache_attn_k[l], cache_attn_v[l], state_gdn[l], state_conv[l], wts)
    (y_p, y_s, k_p, v_p, s_p, c_p, k_s, v_s, s_s, c_s) = outs
    add = lambda a: a[None]
    return (y_p, y_s, add(k_p), add(v_p), add(s_p), add(c_p), add(k_s), add(v_s), add(s_s), add(c_s))
```

```python
import functools
import math

import jax
import jax.numpy as jnp
from jax import lax
from jax.experimental import pallas as pl
from jax.experimental.pallas import tpu as pltpu

F32 = jnp.float32
BF16 = jnp.bfloat16
I32 = jnp.int32

D_MODEL = 1024
ATTN_HEADS = 8
ATTN_KV_HEADS = 2
HEAD_DIM = 64
WINDOW = 128
ROT_DIM = HEAD_DIM // 4
ROPE_THETA = 500000.0
GDN_HEADS = 4
GDN_DK = 128
GDN_DV = 128
CONV_W = 4
QK_COLS = GDN_HEADS * GDN_DK
CONV_DIM = 2 * QK_COLS + GDN_HEADS * GDN_DV
Z_COLS = GDN_HEADS * GDN_DV
Q_COLS = ATTN_HEADS * HEAD_DIM
KV_COLS = ATTN_KV_HEADS * HEAD_DIM
N_GROUPS = 4
EXPERTS_PER_GROUP = 8
N_EXPERTS = N_GROUPS * EXPERTS_PER_GROUP
TOP_K = 2
EXPERT_FF = 256
NORM_EPS = 1e-5
L2_EPS = 1e-6
DEPTH = 1
DEEPNORM_ALPHA = (2 * DEPTH) ** 0.25
PAST_LEN = 8192

LANES = 128
SUBLANES = 8
TOK_TILE = 512
PROJ_TILE = 512
PROJ_SUB = 128
POST_SUB = 256
GDN_CHUNK = 128
GDN_SEQ_PER_STEP = 4
ATTN_BLOCKS_PER_STEP = 4
ATTN_SEQS_PER_STEP = 8
INV_BASE = 16
SAMPLE_SLOTS = 8
SAMPLE_FIRST = CONV_W - 1
ROW_TILE = 512
EXPERT_SUB = 256
SLAB_UNROLL = 4
SLAB = 16
PERM_ROWS = TOP_K * TOK_TILE + N_EXPERTS * SLAB
PERM_SLABS = PERM_ROWS // SLAB
XS_WORDS = D_MODEL + LANES
ZERO_TABLE = N_EXPERTS * (ROW_TILE // SLAB)
VMEM_LIMIT = 48 * 1024 * 1024
NEG_BIG = -1e30


def _cparams(sem):
    return pltpu.CompilerParams(dimension_semantics=sem, vmem_limit_bytes=VMEM_LIMIT)


def _bdot(a, b):
    return jnp.dot(a.astype(BF16), b.astype(BF16), preferred_element_type=F32)


def _bdot_nt(a, b):
    return lax.dot_general(a.astype(BF16), b.astype(BF16), (((1,), (1,)), ((), ())),
                           preferred_element_type=F32)


def _bdot_tn(a, b):
    return lax.dot_general(a.astype(BF16), b.astype(BF16), (((0,), (0,)), ((), ())),
                           preferred_element_type=F32)


def _div_pow2(x, n):
    return jnp.right_shift(x, int(math.log2(n)))


def _mod_pow2(x, n):
    return jnp.bitwise_and(x, n - 1)


def _split3(x):
    hi = x.astype(BF16)
    r = x - hi.astype(F32)
    mid = r.astype(BF16)
    lo = (r - mid.astype(F32)).astype(BF16)
    return hi, mid, lo


def _dot_exact_lhs01(m01, x):
    hi, mid, lo = _split3(x)
    d = lambda t: jnp.dot(m01, t, preferred_element_type=F32)
    return d(hi) + d(mid) + d(lo)


def _sigmoid(x):
    return 1.0 / (1.0 + jnp.exp(-x))


def _silu(x):
    return x * _sigmoid(x)


def _softplus(x):
    return jnp.maximum(x, 0.0) + jnp.log1p(jnp.exp(-jnp.abs(x)))


def _layer_norm(h, g, b):
    mu = jnp.mean(h, axis=-1, keepdims=True)
    d = h - mu
    var = jnp.mean(d * d, axis=-1, keepdims=True)
    return d * lax.rsqrt(var + NORM_EPS) * g + b


def _proj_kernel(*refs, tm, has_hist, full_u, one_segment):
    it = iter(refs)
    x_ref, cos_ref, sin_ref = next(it), next(it), next(it)
    wqkv_ref, wg_ref, wz_ref, wab_ref = next(it), next(it), next(it), next(it)
    convw_ref, alog_ref, dtb_ref, tri_ref, seg_ref = next(it), next(it), next(it), next(it), next(it)
    hist_ref = valid_ref = None
    if has_hist:
        hist_ref, valid_ref = next(it), next(it)
    q_ref, k_ref, v_ref = next(it), next(it), next(it)
    qg_ref, kg_ref, vg_ref, z_ref, gcb_ref, u_ref = (next(it) for _ in range(6))
    ubuf = next(it)

    t = pl.program_id(1)
    sub = PROJ_SUB
    rows = [slice(j * sub, (j + 1) * sub) for j in range(tm // sub)]
    lane = lax.broadcasted_iota(I32, (sub, LANES), 1)
    first_half = _mod_pow2(lane, HEAD_DIM) < (ROT_DIM // 2)

    @pl.when(t == 0)
    def _():
        ubuf[0:SUBLANES, :] = jnp.zeros((SUBLANES, CONV_DIM), F32)

    @pl.when(t > 0)
    def _():
        ubuf[0:SUBLANES, :] = ubuf[tm:tm + SUBLANES, :]

    dots = []
    for r in rows:
        xb = x_ref[r, :].astype(BF16)
        dots.append([jnp.dot(xb, w[...], preferred_element_type=F32)
                     for w in (wqkv_ref, wg_ref, wz_ref, wab_ref)])

    def l2n(s):
        return s * lax.rsqrt(jnp.sum(s * s, axis=1, keepdims=True) + L2_EPS)

    for r, (pq, u, z, ab) in zip(rows, dots):
        cosv, sinv = cos_ref[r, :], sin_ref[r, :]

        def rope(s):
            sw = jnp.where(first_half, pltpu.roll(s, LANES - ROT_DIM // 2, axis=1),
                           pltpu.roll(s, ROT_DIM // 2, axis=1))
            return s * cosv + sw * sinv

        for j in range(Q_COLS // LANES):
            q_ref[r, j * LANES:(j + 1) * LANES] = rope(pq[:, j * LANES:(j + 1) * LANES])
        k_ref[r, :] = rope(pq[:, Q_COLS:Q_COLS + KV_COLS])
        v_ref[r, :] = pq[:, Q_COLS + KV_COLS:Q_COLS + 2 * KV_COLS]
        z_ref[r, :] = z

        if has_hist:
            u = u + hist_ref[r, :]
        if full_u:
            u_ref[r, :] = u
        elif r.stop == tm:
            u_ref[...] = u[sub - SUBLANES:, :]
        base = SUBLANES + r.start
        ubuf[base:base + sub, :] = u
        acc = u * convw_ref[CONV_W - 1:CONV_W, :]
        for j in range(1, CONV_W):
            acc = acc + ubuf[base - j:base - j + sub, :] * convw_ref[CONV_W - 1 - j:CONV_W - j, :]
        c = _silu(acc)
        if has_hist:
            c = c * valid_ref[r, :]
        for h in range(GDN_HEADS):
            sl = slice(h * GDN_DK, (h + 1) * GDN_DK)
            qg_ref[r, sl] = l2n(c[:, sl]) * (GDN_DK ** -0.5)
            kg_ref[r, sl] = l2n(c[:, QK_COLS + h * GDN_DK:QK_COLS + (h + 1) * GDN_DK])
        vg_ref[r, :] = c[:, 2 * QK_COLS:]

        g = -jnp.exp(alog_ref[...]) * _softplus(ab + dtb_ref[...])
        beta = _sigmoid(ab)
        if has_hist:
            g = g * valid_ref[r, :]
            beta = beta * valid_ref[r, :]
        g = jnp.where(lane < GDN_HEADS, g, 0.0)
        gc = _dot_exact_lhs01(tri_ref[...], g)
        if one_segment:
            gl = jnp.broadcast_to(gc[sub - 1:sub, :], (sub, LANES))
        else:
            gl = _dot_exact_lhs01(seg_ref[...], g)
        gcb_ref[r, :] = jnp.where(lane < GDN_HEADS, gc,
                                  jnp.where(lane < 2 * GDN_HEADS, beta,
                                            jnp.where(lane < 3 * GDN_HEADS,
                                                      pltpu.roll(gl, 2 * GDN_HEADS, axis=1), 0.0)))


def _rope_tables(pos):
    half = ROT_DIM // 2
    inv_freq = ROPE_THETA ** (-jnp.arange(half, dtype=F32) * 2.0 / ROT_DIM)
    ang = pos.astype(F32)[:, None] * inv_freq[None, :]
    cos, sin = jnp.cos(ang), jnp.sin(ang)
    p = pos.shape[0]
    cpat = jnp.concatenate([cos, cos, jnp.ones((p, HEAD_DIM - ROT_DIM), F32)], axis=1)
    spat = jnp.concatenate([-sin, sin, jnp.zeros((p, HEAD_DIM - ROT_DIM), F32)], axis=1)
    return jnp.tile(cpat, (1, LANES // HEAD_DIM)), jnp.tile(spat, (1, LANES // HEAD_DIM))


def _segment_matrices(tm, seg_len):
    i = jnp.arange(tm)
    same = (i[:, None] // seg_len) == (i[None, :] // seg_len)
    tri = same & (i[None, :] <= i[:, None])
    return tri.astype(BF16), same.astype(BF16)


def _proj(x, pos, wts, seg_len, n_seq, hist=None, valid=None):
    n = x.shape[0]
    rows = n // n_seq
    tm = min(PROJ_TILE, rows)
    nt = rows // tm
    has_hist = hist is not None
    cos_t, sin_t = _rope_tables(pos)
    tri, seg = _segment_matrices(PROJ_SUB, seg_len)

    tok = lambda w: pl.BlockSpec((tm, w), lambda b, t: (b * nt + t, 0))
    const = lambda a: pl.BlockSpec(a.shape, lambda b, t: (0,) * a.ndim)
    in_arrays = [x, cos_t, sin_t, wts['wqkv'], wts['wg'], wts['wz'], wts['wab'],
                 wts['convw'], wts['alog'], wts['dtb'], tri, seg]
    in_specs = [tok(D_MODEL), pl.BlockSpec((tm, LANES), lambda b, t: (t, 0)),
                pl.BlockSpec((tm, LANES), lambda b, t: (t, 0))] + [const(a) for a in in_arrays[3:]]
    if has_hist:
        in_arrays += [hist, valid]
        in_specs += [tok(CONV_DIM), tok(1)]
    u_rows = n if has_hist else (n // tm) * SUBLANES
    u_block = tm if has_hist else SUBLANES
    out_shape = [jax.ShapeDtypeStruct((n, Q_COLS), F32), jax.ShapeDtypeStruct((n, KV_COLS), F32),
                 jax.ShapeDtypeStruct((n, KV_COLS), F32), jax.ShapeDtypeStruct((n, QK_COLS), F32),
                 jax.ShapeDtypeStruct((n, QK_COLS), F32), jax.ShapeDtypeStruct((n, Z_COLS), F32),
                 jax.ShapeDtypeStruct((n, Z_COLS), F32), jax.ShapeDtypeStruct((n, LANES), F32),
                 jax.ShapeDtypeStruct((u_rows, CONV_DIM), F32)]
    out_specs = [tok(Q_COLS), tok(KV_COLS), tok(KV_COLS), tok(QK_COLS), tok(QK_COLS), tok(Z_COLS),
                 tok(Z_COLS), tok(LANES),
                 pl.BlockSpec((u_block, CONV_DIM), lambda b, t: (b * nt + t, 0))]
    return pl.pallas_call(
        functools.partial(_proj_kernel, tm=tm, has_hist=has_hist, full_u=has_hist,
                          one_segment=seg_len == PROJ_SUB),
        out_shape=out_shape, grid=(n_seq, nt), in_specs=in_specs, out_specs=out_specs,
        scratch_shapes=[pltpu.VMEM((tm + SUBLANES, CONV_DIM), F32)],
        compiler_params=_cparams(("arbitrary", "arbitrary")),
        name="proj_hist" if has_hist else "proj",
    )(*in_arrays)


def _attn_blocks(qs, kcats, vcats, biases, sink, tq):
    lane = lax.broadcasted_iota(I32, (tq, LANES), 1)
    low = lane < HEAD_DIM
    n_slab = Q_COLS // LANES

    def stack(q):
        slabs = [q[:, j * LANES:(j + 1) * LANES] * (HEAD_DIM ** -0.5) for j in range(n_slab)]
        parts = ([jnp.where(low, s, 0.0) for s in slabs] + [jnp.where(low, 0.0, s) for s in slabs])
        return jnp.concatenate(parts, axis=0).astype(BF16)

    def unstack(o8):
        return [jnp.where(low, o8[j * tq:(j + 1) * tq, :], o8[(n_slab + j) * tq:(n_slab + j + 1) * tq, :])
                for j in range(n_slab)]

    rows = ATTN_HEADS * tq
    half = rows // 2
    klow = lax.broadcasted_iota(I32, (2 * WINDOW, LANES), 1) < HEAD_DIM
    one = jnp.ones((), BF16)
    q8s = _each(stack, qs)
    ss = _each(lambda q8, kc, b: _bdot_nt(q8, kc) + b, q8s, kcats, biases)
    ms = _each(lambda s: jnp.maximum(jnp.broadcast_to(jnp.max(s, axis=1, keepdims=True), (rows, LANES)),
                                     sink), ss)
    ps = _each(lambda s, m: jnp.exp(s - jnp.concatenate([m, m], axis=1)).astype(BF16), ss, ms)
    pv0 = _each(lambda p, vc: jnp.dot(p[:half], jnp.where(klow, vc, one), preferred_element_type=F32),
                ps, vcats)
    pv1 = _each(lambda p, vc: jnp.dot(p[half:], jnp.where(klow, one, vc), preferred_element_type=F32),
                ps, vcats)
    pvs = _each(lambda a, b: jnp.concatenate([a, b], axis=0), pv0, pv1)
    o8s = _each(lambda pv, m: pv / (pltpu.roll(pv, HEAD_DIM, axis=1) + jnp.exp(sink - m)), pvs, ms)
    return _each(unstack, o8s)


def _attn_prompt_kernel(q_ref, kc_ref, vc_ref, kp_ref, vp_ref, bias0_ref, bias_ref, sink_ref, o_ref, *,
                        nblk):
    kall = jnp.concatenate([kp_ref[...], kc_ref[...]], axis=0).astype(BF16)
    vall = jnp.concatenate([vp_ref[...], vc_ref[...]], axis=0).astype(BF16)
    win = lambda a, j: a[j * WINDOW:(j + 2) * WINDOW, :]
    qs = [q_ref[j * WINDOW:(j + 1) * WINDOW, :] for j in range(nblk)]
    biases = [bias0_ref[0]] + [bias_ref[...]] * (nblk - 1)
    outs = _attn_blocks(qs, [win(kall, j) for j in range(nblk)], [win(vall, j) for j in range(nblk)],
                        biases, sink_ref[...], WINDOW)
    for j, slabs in enumerate(outs):
        for c, slab in enumerate(slabs):
            o_ref[j * WINDOW:(j + 1) * WINDOW, c * LANES:(c + 1) * LANES] = slab


def _attn_sample_kernel(q_ref, kc_ref, vc_ref, kp_ref, vp_ref, bias_ref, sink_ref, o_ref, kw_ref, vw_ref,
                        *, nseq, n_new):
    tq = SAMPLE_SLOTS
    zpad = jnp.zeros((WINDOW - tq, LANES), F32)
    rows = lambda ref, j: ref[j * tq:(j + 1) * tq, :]
    cat = lambda pref, cref, j: jnp.concatenate([pref[j], rows(cref, j), zpad], axis=0).astype(BF16)
    outs = _attn_blocks([rows(q_ref, j) for j in range(nseq)],
                        [cat(kp_ref, kc_ref, j) for j in range(nseq)],
                        [cat(vp_ref, vc_ref, j) for j in range(nseq)],
                        [bias_ref[...]] * nseq, sink_ref[...], tq)
    for j, slabs in enumerate(outs):
        for c, slab in enumerate(slabs):
            o_ref[j * tq:(j + 1) * tq, c * LANES:(c + 1) * LANES] = slab
    row = lax.broadcasted_iota(I32, (WINDOW, LANES), 0)
    keep = WINDOW - n_new
    for pref, cref, wref in ((kp_ref, kc_ref, kw_ref), (vp_ref, vc_ref, vw_ref)):
        for j in range(nseq):
            new = jnp.concatenate([rows(cref, j), zpad], axis=0)
            wref[j] = jnp.where(row < keep, pltpu.roll(pref[j], keep, axis=0),
                                pltpu.roll(new, keep - SAMPLE_FIRST, axis=0))


def _sink_rows(sinks, tq):
    return jnp.broadcast_to(jnp.repeat(sinks.astype(F32), tq)[:, None], (ATTN_HEADS * tq, LANES))


def _attn_bias(tq, q_off, k_lo, k_hi, has_prev):
    qi = (jnp.arange(ATTN_HEADS * tq, dtype=I32) % tq)[:, None]
    c = jnp.arange(2 * WINDOW, dtype=I32)[None, :]
    cj = c - WINDOW
    vis_prev = (c < WINDOW) & (c > qi - q_off) & has_prev
    vis_cur = (c >= WINDOW) & (cj <= qi) & (cj >= k_lo) & (cj <= k_hi)
    return jnp.where(vis_prev | vis_cur, 0.0, NEG_BIG).astype(F32)


def _attn_prompt(q, k, v, sinks, n_seq):
    n = q.shape[0]
    nb = n // n_seq // WINDOW
    nblk = min(ATTN_BLOCKS_PER_STEP, nb)
    steps = nb // nblk
    tq = nblk * WINDOW
    cur = lambda w: pl.BlockSpec((tq, w), lambda b, i: (b * steps + i, 0))
    prev = pl.BlockSpec((WINDOW, LANES), lambda b, i: (b * nb + jnp.maximum(i * nblk - 1, 0), 0))
    bias2 = jnp.stack([_attn_bias(WINDOW, 0, 0, WINDOW - 1, False),
                       _attn_bias(WINDOW, 0, 0, WINDOW - 1, True)])
    rows = ATTN_HEADS * WINDOW
    return pl.pallas_call(
        functools.partial(_attn_prompt_kernel, nblk=nblk),
        out_shape=jax.ShapeDtypeStruct((n, Q_COLS), F32), grid=(n_seq, steps),
        in_specs=[cur(Q_COLS), cur(LANES), cur(LANES), prev, prev,
                  pl.BlockSpec((1, rows, 2 * WINDOW), lambda b, i: (jnp.minimum(i, 1), 0, 0)),
                  pl.BlockSpec((rows, 2 * WINDOW), lambda b, i: (0, 0)),
                  pl.BlockSpec((rows, LANES), lambda b, i: (0, 0))],
        out_specs=cur(Q_COLS),
        compiler_params=_cparams(("arbitrary", "arbitrary")), name="attn_prompt",
    )(q, k, v, k, v, bias2, bias2[1], _sink_rows(sinks, WINDOW))


def _attn_sample(q, k, v, cache_k, cache_v, sinks, n_seq, n_new):
    tq = SAMPLE_SLOTS
    nseq = min(ATTN_SEQS_PER_STEP, n_seq)
    cur = lambda w: pl.BlockSpec((nseq * tq, w), lambda b: (b, 0))
    prev = pl.BlockSpec((nseq, WINDOW, LANES), lambda b: (b, 0, 0))
    bias = _attn_bias(tq, SAMPLE_FIRST, SAMPLE_FIRST, SAMPLE_FIRST + 3, True)
    win = jax.ShapeDtypeStruct((n_seq, WINDOW, LANES), F32)
    return pl.pallas_call(
        functools.partial(_attn_sample_kernel, nseq=nseq, n_new=n_new),
        out_shape=[jax.ShapeDtypeStruct((n_seq * tq, Q_COLS), F32), win, win], grid=(n_seq // nseq,),
        in_specs=[cur(Q_COLS), cur(LANES), cur(LANES), prev, prev,
                  pl.BlockSpec(bias.shape, lambda b: (0, 0)),
                  pl.BlockSpec((ATTN_HEADS * tq, LANES), lambda b: (0, 0))],
        out_specs=[cur(Q_COLS), prev, prev],
        compiler_params=_cparams(("arbitrary",)), name="attn_sample",
    )(q, k, v, cache_k, cache_v, bias, _sink_rows(sinks, tq))


def _each(f, *lists):
    return [f(*args) for args in zip(*lists)]


def _unit_lower_inverse(ms, eye, same_base):
    c = ms[0].shape[0]

    def neumann(q0s, n_factors):
        xs = _each(lambda q: eye + q, q0s)
        if n_factors == 1:
            return xs
        qs = _each(_bdot, q0s, q0s)
        for _ in range(n_factors - 2):
            prods = _each(lambda x, q: _bdot(jnp.concatenate([x, q], axis=0), q), xs, qs)
            xs = _each(lambda x, pr: x + pr[:c], xs, prods)
            qs = _each(lambda pr: pr[c:], prods)
        return _each(lambda x, q: x + _bdot(x, q), xs, qs)

    ds = _each(lambda m: jnp.where(same_base, m, 0.0), ms)
    xs = neumann(_each(lambda d: -d, ds), int(math.log2(INV_BASE)))
    nblk = c // INV_BASE
    if nblk == 1:
        return xs
    ls = _each(lambda m, d: m - d, ms, ds)
    ns = _each(lambda x, l: -_bdot(x, l), xs, ls)
    ys = neumann(ns, int(math.log2(nblk)))
    return _each(_bdot, ys, xs)


def _gdn_intra(qs, ks, vs, gcs, gls, betas, same_seq, low_incl, low_strict, eye, same_base):
    del same_seq
    e_gcs = _each(jnp.exp, gcs)

    def decay_of(gc):
        gc_row = jnp.sum(jnp.where(eye > 0, gc, 0.0), axis=0, keepdims=True)
        return jnp.where(low_incl, jnp.exp(jnp.where(low_incl, gc - gc_row, 0.0)), 0.0)

    c = qs[0].shape[0]
    decays = _each(decay_of, gcs)
    kbs = _each(lambda k, b: k * b, ks, betas)
    vbs = _each(lambda v, b: v * b, vs, betas)
    kqs = _each(lambda kb, q, k: _bdot_nt(jnp.concatenate([kb, q], axis=0), k), kbs, qs, ks)
    ms = _each(lambda kq, d: jnp.where(low_strict, kq[:c] * d, 0.0), kqs, decays)
    attns = _each(lambda kq, d: jnp.where(low_incl, kq[c:] * d, 0.0), kqs, decays)
    tmats = _unit_lower_inverse(ms, eye, same_base)
    uws = _each(lambda t, vb, kb, e: _bdot(t, jnp.concatenate([vb, kb * e], axis=1)),
                tmats, vbs, kbs, e_gcs)
    us = _each(lambda uw: uw[:, :GDN_DV], uws)
    ws = _each(lambda uw: uw[:, GDN_DV:], uws)
    q_decs = _each(lambda q, e: q * e, qs, e_gcs)
    k_decs = _each(lambda k, gl, gc: k * jnp.exp(gl - gc), ks, gls, gcs)
    return us, ws, attns, q_decs, k_decs


def _chunk_masks(c, seq_len):
    i = lax.broadcasted_iota(I32, (c, c), 0)
    j = lax.broadcasted_iota(I32, (c, c), 1)
    same_seq = _div_pow2(i, seq_len) == _div_pow2(j, seq_len)
    low_incl = same_seq & (i >= j)
    low_strict = same_seq & (i > j)
    eye = (i == j).astype(F32)
    same_base = _div_pow2(i, INV_BASE) == _div_pow2(j, INV_BASE)
    return same_seq, low_incl, low_strict, eye, same_base


def _gated_rms(o, z, nw):
    o = o * lax.rsqrt(jnp.mean(o * o, axis=1, keepdims=True) + NORM_EPS) * nw
    return o * _silu(z)


def _gdn_prompt_kernel(qg_ref, kg_ref, vg_ref, z_ref, gcb_ref, nw_ref, o_ref, s_out_ref, s_scr):
    c = GDN_CHUNK
    n = pl.program_id(1)

    @pl.when(n == 0)
    def _():
        s_scr[...] = jnp.zeros_like(s_scr)

    masks = _chunk_masks(c, c)
    nw = nw_ref[...]
    chains = [(b, h) for b in range(qg_ref.shape[0]) for h in range(GDN_HEADS)]
    hs = lambda h: slice(h * GDN_DK, (h + 1) * GDN_DK)
    col = lambda off: [gcb_ref[b, :, off + h:off + h + 1] for b, h in chains]
    gcs, betas, gls = col(0), col(GDN_HEADS), col(2 * GDN_HEADS)
    qs = [qg_ref[b, :, hs(h)] for b, h in chains]
    ks = [kg_ref[b, :, hs(h)] for b, h in chains]
    vs = [vg_ref[b, :, hs(h)] for b, h in chains]
    us, ws, attns, q_decs, k_decs = _gdn_intra(qs, ks, vs, gcs, gls, betas, *masks)
    ss = [s_scr[b, h] for b, h in chains]
    wqs = _each(lambda w, qd, s: _bdot(jnp.concatenate([w, qd], axis=0), s), ws, q_decs, ss)
    wss = _each(lambda wq: wq[:c], wqs)
    qss = _each(lambda wq: wq[c:], wqs)
    v_news = _each(lambda u, x: u - x, us, wss)
    avs = _each(_bdot, attns, v_news)
    kvs = _each(_bdot_tn, k_decs, v_news)
    for (b, h), s, gl, qsv, av, kv in zip(chains, ss, gls, qss, avs, kvs):
        s_scr[b, h] = s * jnp.exp(gl[0:1, :]) + kv
        o_ref[b, :, hs(h)] = _gated_rms(qsv + av, z_ref[b, :, hs(h)], nw)

    @pl.when(n == pl.num_programs(1) - 1)
    def _():
        s_out_ref[...] = s_scr[...]


def _gdn_prompt(qg, kg, vg, z, gcb, norm_w, n_seq):
    n = qg.shape[0]
    s_len = n // n_seq
    nb = min(GDN_SEQ_PER_STEP, n_seq)
    v3 = lambda a: a.reshape(n_seq, s_len, a.shape[-1])
    tok = lambda w: pl.BlockSpec((nb, GDN_CHUNK, w), lambda b, i: (b, i, 0))
    o, s = pl.pallas_call(
        _gdn_prompt_kernel,
        out_shape=[jax.ShapeDtypeStruct((n_seq, s_len, Z_COLS), F32),
                   jax.ShapeDtypeStruct((n_seq, GDN_HEADS, GDN_DK, GDN_DV), F32)],
        grid=(n_seq // nb, s_len // GDN_CHUNK),
        in_specs=[tok(QK_COLS), tok(QK_COLS), tok(Z_COLS), tok(Z_COLS), tok(LANES),
                  pl.BlockSpec((1, GDN_DV), lambda b, i: (0, 0))],
        out_specs=[tok(Z_COLS),
                   pl.BlockSpec((nb, GDN_HEADS, GDN_DK, GDN_DV), lambda b, i: (b, 0, 0, 0))],
        scratch_shapes=[pltpu.VMEM((nb, GDN_HEADS, GDN_DK, GDN_DV), F32)],
        compiler_params=_cparams(("arbitrary", "arbitrary")), name="gdn_prompt",
    )(v3(qg), v3(kg), v3(vg), v3(z), v3(gcb), norm_w)
    return o.reshape(n, Z_COLS), s


def _gdn_sample_kernel(qg_ref, kg_ref, vg_ref, z_ref, gcb_ref, nw_ref, s_in_ref, o_ref, s_out_ref):
    c = GDN_CHUNK
    n_sub = c // SAMPLE_SLOTS
    masks = _chunk_masks(c, SAMPLE_SLOTS)
    heads = range(GDN_HEADS)
    hs = lambda h: slice(h * GDN_DK, (h + 1) * GDN_DK)
    rs = lambda s: slice(s * SAMPLE_SLOTS, (s + 1) * SAMPLE_SLOTS)
    col = lambda off: [gcb_ref[:, off + h:off + h + 1] for h in heads]
    gcs, betas, gls = col(0), col(GDN_HEADS), col(2 * GDN_HEADS)
    us, ws, attns, q_decs, k_decs = _gdn_intra([qg_ref[:, hs(h)] for h in heads],
                                               [kg_ref[:, hs(h)] for h in heads],
                                               [vg_ref[:, hs(h)] for h in heads], gcs, gls, betas, *masks)
    pairs = [(h, s) for h in heads for s in range(n_sub)]
    sts = [s_in_ref[s, h] for h, s in pairs]
    boths = [jnp.concatenate([ws[h][rs(s), :], q_decs[h][rs(s), :]], axis=0) for h, s in pairs]
    rr = _each(_bdot, boths, sts)
    gather = lambda h, part: jnp.concatenate(
        [rr[h * n_sub + s][part * SAMPLE_SLOTS:(part + 1) * SAMPLE_SLOTS, :] for s in range(n_sub)], axis=0)
    v_news = [us[h] - gather(h, 0) for h in heads]
    avs = _each(_bdot, attns, v_news)
    row = lax.broadcasted_iota(I32, (c, LANES), 0)
    seq_of_row = _div_pow2(row, SAMPLE_SLOTS)
    kds = [jnp.where(seq_of_row == s, k_decs[h], 0.0) for h, s in pairs]
    kvs = _each(_bdot_tn, kds, [v_news[h] for h, _ in pairs])
    egls = _each(jnp.exp, gls)
    for (h, s), st, kv in zip(pairs, sts, kvs):
        s_out_ref[s, h] = st * egls[h][s * SAMPLE_SLOTS:s * SAMPLE_SLOTS + 1, :] + kv
    nw = nw_ref[...]
    for h in heads:
        o_ref[:, hs(h)] = _gated_rms(gather(h, 1) + avs[h], z_ref[:, hs(h)], nw)


def _gdn_sample(qg, kg, vg, z, gcb, norm_w, state):
    n = qg.shape[0]
    n_sub = GDN_CHUNK // SAMPLE_SLOTS
    tok = lambda w: pl.BlockSpec((GDN_CHUNK, w), lambda i: (i, 0))
    st = pl.BlockSpec((n_sub, GDN_HEADS, GDN_DK, GDN_DV), lambda i: (i, 0, 0, 0))
    return pl.pallas_call(
        _gdn_sample_kernel,
        out_shape=[jax.ShapeDtypeStruct((n, Z_COLS), F32),
                   jax.ShapeDtypeStruct(state.shape, F32)],
        grid=(n // GDN_CHUNK,),
        in_specs=[tok(QK_COLS), tok(QK_COLS), tok(Z_COLS), tok(Z_COLS), tok(LANES),
                  pl.BlockSpec((1, GDN_DV), lambda i: (0, 0)), st],
        out_specs=[tok(Z_COLS), st],
        compiler_params=_cparams(("arbitrary",)), name="gdn_sample",
    )(qg, kg, vg, z, gcb, norm_w, state)


def _post_kernel(a_ref, g_ref, x_ref, wo_ref, ln_g_ref, ln_b_ref, wr_ref, x1_ref, route_ref, *, tm):
    sub = POST_SUB
    rows = [slice(j * sub, (j + 1) * sub) for j in range(tm // sub)]
    d = lambda a, b: jnp.dot(a, b, preferred_element_type=F32)
    mixes = [d(a_ref[r, :].astype(BF16), wo_ref[0:Q_COLS, :]) + d(g_ref[r, :].astype(BF16), wo_ref[Q_COLS:, :])
             for r in rows]
    x1s = [_layer_norm(DEEPNORM_ALPHA * x_ref[r, :] + mix, ln_g_ref[...], ln_b_ref[...])
           for r, mix in zip(rows, mixes)]
    for r, x1 in zip(rows, x1s):
        x1_ref[r, :] = x1
    wh, wm = wr_ref[0], wr_ref[1]
    lgs = []
    for x1 in x1s:
        xh = x1.astype(BF16)
        xm = (x1 - xh.astype(F32)).astype(BF16)
        lgs.append(d(xh, wh) + d(xh, wm) + d(xm, wh))
    for r, lg in zip(rows, lgs):
        route_ref[:, r] = jnp.transpose(_route(lg))[0:SUBLANES, :]


def _route(lg):
    lane = lax.broadcasted_iota(I32, lg.shape, 1)
    lane_f = lane.astype(F32)
    big = float(LANES)

    def first_max(vals, mask):
        v = jnp.where(mask, vals, NEG_BIG)
        mx = jnp.max(v, axis=1, keepdims=True)
        idx = jnp.min(jnp.where(mask & (v == mx), lane_f, big), axis=1, keepdims=True)
        return mx, idx

    gmask = lane < N_GROUPS
    gmax, gidx = first_max(lg, gmask)
    gden = jnp.sum(jnp.where(gmask, jnp.exp(lg - gmax), 0.0), axis=1, keepdims=True)
    g_top_p = 1.0 / gden
    e_lane = lane - N_GROUPS
    e_group = _div_pow2(jnp.maximum(e_lane, 0), EXPERTS_PER_GROUP).astype(F32)
    emask = (e_lane >= 0) & (e_lane < N_EXPERTS) & (e_group == gidx)
    m1, i1 = first_max(lg, emask)
    eden = jnp.sum(jnp.where(emask, jnp.exp(lg - m1), 0.0), axis=1, keepdims=True)
    m2, i2 = first_max(lg, emask & (lane_f != i1))
    p1 = 1.0 / eden
    p2 = jnp.exp(m2 - m1) / eden
    tot = p1 + p2
    gate1 = g_top_p * (p1 / tot)
    gate2 = g_top_p * (p2 / tot)
    return jnp.where(lane == 0, gate1,
                     jnp.where(lane == 1, gate2,
                               jnp.where(lane == 2, i1 - N_GROUPS,
                                         jnp.where(lane == 3, i2 - N_GROUPS, 0.0))))


def _post(attn_o, gdn_o, x, wts):
    n = x.shape[0]
    tm = min(PROJ_TILE, n)
    tok = lambda w: pl.BlockSpec((tm, w), lambda i: (i, 0))
    const = lambda a: pl.BlockSpec(a.shape, lambda i: (0,) * a.ndim)
    consts = [wts['wo'], wts['ln1_g'], wts['ln1_b'], wts['wr']]
    return pl.pallas_call(
        functools.partial(_post_kernel, tm=tm),
        out_shape=[jax.ShapeDtypeStruct((n, D_MODEL), F32), jax.ShapeDtypeStruct((SUBLANES, n), F32)],
        grid=(n // tm,),
        in_specs=[tok(Q_COLS), tok(Z_COLS), tok(D_MODEL)] + [const(a) for a in consts],
        out_specs=[tok(D_MODEL), pl.BlockSpec((SUBLANES, tm), lambda i: (0, i))],
        compiler_params=_cparams(("arbitrary",)), name="post_%d" % (n // tm),
    )(attn_o, gdn_o, x, *consts)


def _slab_loop(n, body):
    n_main = jnp.right_shift(n, int(math.log2(SLAB_UNROLL)))

    def main(i, c):
        for u in range(SLAB_UNROLL):
            body(i * SLAB_UNROLL + u, u)
        return c

    lax.fori_loop(0, n_main, main, 0)
    lax.fori_loop(n_main * SLAB_UNROLL, n, lambda j, c: (body(j, 0), c)[1], 0)


def _dispatch_kernel(dst_ref, nslab_ref, ztab_ref, zinfo_ref, slot_ref, gate_ref, *rest,
                     group_tiles, max_tiles):
    x_refs = rest[:len(group_tiles)]
    xs_ref, pbuf, sem, zbuf, zsem = rest[len(group_tiles):]
    n_tiles = sum(group_tiles)
    g = pl.program_id(0)
    cur = lax.rem(g, 2)

    def slab_copy(tile, buf_slot, j):
        d = pl.multiple_of(dst_ref[tile * PERM_SLABS + j], SLAB)
        src = pbuf.at[buf_slot, pl.ds(pl.multiple_of(j * SLAB, SLAB), SLAB), :]
        return pltpu.make_async_copy(src, xs_ref.at[pl.ds(d, SLAB), :], sem.at[buf_slot])

    def tail_copy(k):
        d = pl.multiple_of(ztab_ref[k], SLAB)
        return pltpu.make_async_copy(zbuf.at[pl.ds(0, SLAB), :], xs_ref.at[pl.ds(d, SLAB), :], zsem)

    def tile_copy(t):
        d = pl.multiple_of(t * ROW_TILE, ROW_TILE)
        return pltpu.make_async_copy(zbuf, xs_ref.at[pl.ds(d, ROW_TILE), :], zsem)

    @pl.when(g == 0)
    def _():
        zbuf[...] = jnp.zeros_like(zbuf)

    share = -(-ZERO_TABLE // n_tiles)
    k0 = g * share
    _slab_loop(jnp.clip(zinfo_ref[0] - k0, 0, share), lambda j, u: tail_copy(k0 + j).start(priority=1))

    @pl.when(zinfo_ref[1] + g < max_tiles)
    def _():
        tile_copy(zinfo_ref[1] + g).start(priority=1)

    x = x_refs[-1][...]
    bound = n_tiles
    for x_ref, nt in zip(x_refs[-2::-1], group_tiles[:0:-1]):
        bound -= nt
        x = jnp.where(g < bound, x_ref[...], x)

    r = lax.broadcasted_iota(I32, (PERM_ROWS, TOK_TILE), 0)
    sl = slot_ref[0]
    hit0, hit1 = r == sl[0:1, :], r == sl[1:2, :]
    onehot = jnp.where(hit0 | hit1, 1.0, 0.0).astype(BF16)
    gt = gate_ref[0]
    gcol = jnp.sum(jnp.where(hit0, gt[0:1, :], 0.0) + jnp.where(hit1, gt[1:2, :], 0.0),
                   axis=1, keepdims=True)
    pbuf[cur, :, 0:D_MODEL] = jnp.dot(onehot, x.astype(BF16), preferred_element_type=F32).astype(BF16)
    g_hi = gcol.astype(BF16).astype(F32)
    lane = lax.broadcasted_iota(I32, (PERM_ROWS, LANES), 1)
    pbuf[cur, :, D_MODEL:] = jnp.where(lane < LANES // 2, g_hi, gcol - g_hi).astype(BF16)

    @pl.when(g > 0)
    def _():
        _slab_loop(nslab_ref[g - 1], lambda j, u: slab_copy(g - 1, 1 - cur, j).wait())

    _slab_loop(nslab_ref[g], lambda j, u: slab_copy(g, cur, j).start(priority=u % 2))

    @pl.when(g == n_tiles - 1)
    def _():
        _slab_loop(nslab_ref[g], lambda j, u: slab_copy(g, cur, j).wait())
        lax.fori_loop(zinfo_ref[1] + n_tiles, max_tiles,
                      lambda t, c: (tile_copy(t).start(priority=1), c)[1], 0)
        _slab_loop(zinfo_ref[0], lambda k, u: tail_copy(k).wait())
        lax.fori_loop(zinfo_ref[1], max_tiles, lambda t, c: (tile_copy(t).wait(), c)[1], 0)


def _dispatch(plan, x1s, max_tiles):
    group_tiles = tuple(x1.shape[0] // TOK_TILE for x1 in x1s)
    n_tiles = sum(group_tiles)
    tile = lambda i, d, ns, zt, zi: (i, 0, 0)
    in_specs = [pl.BlockSpec((1, TOP_K, TOK_TILE), tile), pl.BlockSpec((1, TOP_K, TOK_TILE), tile)]
    base = 0
    for nt in group_tiles:
        in_specs.append(pl.BlockSpec(
            (TOK_TILE, D_MODEL),
            lambda i, d, ns, zt, zi, base=base, nt=nt: (jnp.clip(i - base, 0, nt - 1), 0)))
        base += nt
    return pl.pallas_call(
        functools.partial(_dispatch_kernel, group_tiles=group_tiles, max_tiles=max_tiles),
        out_shape=jax.ShapeDtypeStruct((max_tiles * ROW_TILE, XS_WORDS), BF16),
        grid_spec=pltpu.PrefetchScalarGridSpec(
            num_scalar_prefetch=4, grid=(n_tiles,), in_specs=in_specs,
            out_specs=pl.BlockSpec(memory_space=pl.ANY),
            scratch_shapes=[pltpu.VMEM((2, PERM_ROWS, XS_WORDS), BF16), pltpu.SemaphoreType.DMA((2,)),
                            pltpu.VMEM((ROW_TILE, XS_WORDS), BF16), pltpu.SemaphoreType.DMA(())]),
        compiler_params=_cparams(("arbitrary",)), name="moe_dispatch",
    )(plan['slab_dst'], plan['nslab'], plan['ztab'], plan['zinfo'], plan['slot_rows'], plan['gate_rows'],
      *x1s)


def _expert_kernel(te_ref, nu_ref, xs_ref, wg_ref, wu_ref, wd_ref, ye_ref, wgu_scr, wd_scr):
    i = pl.program_id(0)
    active = i < nu_ref[0]

    @pl.when(active & ((i == 0) | (te_ref[i] != te_ref[jnp.maximum(i - 1, 0)])))
    def _():
        wgu_scr[:, 0:EXPERT_FF] = wg_ref[0].astype(BF16)
        wgu_scr[:, EXPERT_FF:] = wu_ref[0].astype(BF16)
        wd_scr[...] = wd_ref[0].astype(BF16)

    @pl.when(active)
    def _():
        sub = EXPERT_SUB
        rows = [slice(j * sub, (j + 1) * sub) for j in range(ROW_TILE // sub)]
        d = lambda a, b: jnp.dot(a, b, preferred_element_type=F32)
        wgu, wd = wgu_scr[...], wd_scr[...]
        xs = [xs_ref[r, 0:D_MODEL] for r in rows]
        hs = [d(x, wgu) for x in xs]
        hhs = [(_silu(h[:, :EXPERT_FF]) * h[:, EXPERT_FF:]).astype(BF16) for h in hs]
        ys = [d(hh, wd) for hh in hhs]
        for r, y in zip(rows, ys):
            parts = xs_ref[r, D_MODEL:].astype(F32)
            gate = parts + pltpu.roll(parts, LANES // 2, axis=1)
            ye_ref[r, :] = (y * jnp.concatenate([gate] * (D_MODEL // LANES), axis=1)).astype(BF16)

    @pl.when(jnp.logical_not(active))
    def _():
        ye_ref[...] = jnp.zeros_like(ye_ref)


def _experts(tile_expert, n_used, xs, w_gate, w_up, w_down):
    n_tiles = xs.shape[0] // ROW_TILE
    row_in = lambda i, te, nu: (jnp.minimum(i, nu[0] - 1), 0)
    row = lambda i, te, nu: (i, 0)
    wsel = lambda i, te, nu: (te[i], 0, 0)
    return pl.pallas_call(
        _expert_kernel,
        out_shape=jax.ShapeDtypeStruct((xs.shape[0], D_MODEL), BF16),
        grid_spec=pltpu.PrefetchScalarGridSpec(
            num_scalar_prefetch=2, grid=(n_tiles,),
            in_specs=[pl.BlockSpec((ROW_TILE, XS_WORDS), row_in),
                      pl.BlockSpec((1, D_MODEL, EXPERT_FF), wsel),
                      pl.BlockSpec((1, D_MODEL, EXPERT_FF), wsel),
                      pl.BlockSpec((1, EXPERT_FF, D_MODEL), wsel)],
            out_specs=pl.BlockSpec((ROW_TILE, D_MODEL), row),
            scratch_shapes=[pltpu.VMEM((D_MODEL, 2 * EXPERT_FF), BF16),
                            pltpu.VMEM((EXPERT_FF, D_MODEL), BF16)]),
        compiler_params=_cparams(("arbitrary",)), name="moe_experts",
    )(tile_expert, n_used, xs, w_gate, w_up, w_down)


def _combine_kernel(dst_ref, nslab_ref, x1_ref, slot_ref, ye_ref, ln_g_ref, ln_b_ref, y_ref,
                    buf, sem, *, tile_base, n_tiles):
    i = pl.program_id(0)
    g = tile_base + i
    cur = lax.rem(i, 2)

    def slab_copy(tile, buf_slot, j):
        d = pl.multiple_of(dst_ref[tile * PERM_SLABS + j], SLAB)
        dst = buf.at[buf_slot, pl.ds(pl.multiple_of(j * SLAB, SLAB), SLAB), :]
        return pltpu.make_async_copy(ye_ref.at[pl.ds(d, SLAB), :], dst, sem.at[buf_slot])

    @pl.when(i == 0)
    def _():
        buf[...] = jnp.zeros_like(buf)
        _slab_loop(nslab_ref[g], lambda j, u: slab_copy(g, cur, j).start(priority=u % 2))

    @pl.when(i + 1 < n_tiles)
    def _():
        _slab_loop(nslab_ref[g + 1], lambda j, u: slab_copy(g + 1, 1 - cur, j).start(priority=u % 2))

    _slab_loop(nslab_ref[g], lambda j, u: slab_copy(g, cur, j).wait())

    col = lax.broadcasted_iota(I32, (TOK_TILE, PERM_ROWS), 1)
    sl = slot_ref[0]
    diag = (lax.broadcasted_iota(I32, (TOK_TILE, TOK_TILE), 0)
            == lax.broadcasted_iota(I32, (TOK_TILE, TOK_TILE), 1))
    as_col = lambda row: jnp.sum(jnp.where(diag, row, 0), axis=1, keepdims=True)
    pick = jnp.where((col == as_col(sl[0:1, :])) | (col == as_col(sl[1:2, :])), 1.0, 0.0).astype(BF16)
    moe = jnp.dot(pick, buf[cur], preferred_element_type=F32)
    y_ref[...] = _layer_norm(DEEPNORM_ALPHA * x1_ref[...] + moe, ln_g_ref[...], ln_b_ref[...])


def _combine(plan, tile_base, x1, ye, ln_g, ln_b):
    n = x1.shape[0]
    n_tiles = n // TOK_TILE
    tok = lambda w: pl.BlockSpec((TOK_TILE, w), lambda i, d, ns: (i, 0))
    const = lambda a: pl.BlockSpec(a.shape, lambda i, d, ns: (0,) * a.ndim)
    return pl.pallas_call(
        functools.partial(_combine_kernel, tile_base=tile_base, n_tiles=n_tiles),
        out_shape=jax.ShapeDtypeStruct((n, D_MODEL), F32),
        grid_spec=pltpu.PrefetchScalarGridSpec(
            num_scalar_prefetch=2, grid=(n_tiles,),
            in_specs=[tok(D_MODEL),
                      pl.BlockSpec((1, TOP_K, TOK_TILE), lambda i, d, ns: (tile_base + i, 0, 0)),
                      pl.BlockSpec(memory_space=pl.ANY), const(ln_g), const(ln_b)],
            out_specs=tok(D_MODEL),
            scratch_shapes=[pltpu.VMEM((2, PERM_ROWS, D_MODEL), BF16), pltpu.SemaphoreType.DMA((2,))]),
        compiler_params=_cparams(("arbitrary",)), name="moe_combine_%d" % tile_base,
    )(plan['slab_dst'], plan['nslab'], x1, plan['slot_rows'], ye, ln_g, ln_b)


def _routing_plan(ids, gates):
    nt = ids.shape[1] // TOK_TILE
    pairs = TOP_K * TOK_TILE
    ex = jnp.arange(N_EXPERTS, dtype=I32)
    per_tile = lambda a: jnp.swapaxes(a.reshape(TOP_K, nt, TOK_TILE), 0, 1)
    flat = per_tile(ids).reshape(nt, pairs)
    onehot = (flat[:, None, :] == ex[None, :, None])
    p = jnp.arange(pairs, dtype=I32)
    triu = (p[:, None] <= p[None, :]).astype(BF16)
    csum = jnp.dot(onehot.astype(BF16).reshape(nt * N_EXPERTS, pairs), triu,
                   preferred_element_type=F32).astype(I32).reshape(nt, N_EXPERTS, pairs)
    oh = onehot.astype(I32)
    rank = jnp.sum(oh * (csum - 1), axis=1)
    cnt = csum[:, :, -1]
    cpad = (cnt + SLAB - 1) // SLAB * SLAB
    seg_end = jnp.cumsum(cpad, axis=1)
    seg_off = seg_end - cpad
    slot = jnp.sum(oh * seg_off[:, :, None], axis=1) + rank
    run_end = jnp.cumsum(cpad, axis=0)
    ntiles_e = (run_end[-1] + ROW_TILE - 1) // ROW_TILE
    tile_end = jnp.cumsum(ntiles_e)
    dst_run = ((tile_end - ntiles_e) * ROW_TILE)[None, :] + run_end - cpad
    j8 = jnp.arange(PERM_SLABS, dtype=I32) * SLAB
    e_of = jnp.minimum(jnp.sum((j8[None, :, None] >= seg_end[:, None, :]).astype(I32), axis=2),
                       N_EXPERTS - 1)
    sel = (e_of[:, :, None] == ex).astype(I32)
    slab_dst = jnp.sum(sel * (dst_run - seg_off)[:, None, :], axis=2) + j8[None, :]
    n_used = tile_end[-1]
    max_tiles = _max_row_tiles(ids.shape[1])
    t = jnp.arange(max_tiles, dtype=I32)
    te = jnp.sum((t[:, None] >= tile_end[None, :]).astype(I32), axis=1)
    te_last = jnp.sum((n_used - 1 >= tile_end).astype(I32))
    row_start = (tile_end - ntiles_e) * ROW_TILE
    tail_cnt = (ntiles_e * ROW_TILE - run_end[-1]) // SLAB
    tail_end = jnp.cumsum(tail_cnt)
    k = jnp.arange(ZERO_TABLE, dtype=I32)
    e_k = jnp.minimum(jnp.sum((k[:, None] >= tail_end[None, :]).astype(I32), axis=1), N_EXPERTS - 1)
    base_k = jnp.sum((e_k[:, None] == ex).astype(I32)
                     * (row_start + run_end[-1] - SLAB * (tail_end - tail_cnt))[None, :], axis=1)
    return dict(
        slab_dst=slab_dst.reshape(-1).astype(I32), nslab=(seg_end[:, -1] // SLAB).astype(I32),
        ztab=(base_k + SLAB * k).astype(I32), zinfo=jnp.stack([tail_end[-1], n_used]).astype(I32),
        slot_rows=slot.reshape(nt, TOP_K, TOK_TILE).astype(I32),
        gate_rows=per_tile(gates).astype(F32),
        tile_expert=jnp.where(t < n_used, jnp.minimum(te, N_EXPERTS - 1), te_last).astype(I32),
        n_used=n_used.reshape(1).astype(I32))


def _max_row_tiles(n_tokens):
    rows = TOP_K * n_tokens + (n_tokens // TOK_TILE) * N_EXPERTS * (SLAB - 1)
    return (rows + ROW_TILE - 1) // ROW_TILE + N_EXPERTS


def _moe(x1s, routes, wts):
    ids = jnp.concatenate([r[TOP_K:2 * TOP_K, :] for r in routes], axis=1).astype(I32)
    gates = jnp.concatenate([r[0:TOP_K, :] for r in routes], axis=1)
    plan = _routing_plan(ids, gates)
    max_tiles = _max_row_tiles(ids.shape[1])
    bases = [0]
    for x1 in x1s[:-1]:
        bases.append(bases[-1] + x1.shape[0] // TOK_TILE)
    xs = _dispatch(plan, x1s, max_tiles)
    ye = _experts(plan['tile_expert'], plan['n_used'], xs, wts['w_gate'], wts['w_up'], wts['w_down'])
    return [_combine(plan, base, x1, ye, wts['ln2_g'], wts['ln2_b']) for base, x1 in zip(bases, x1s)]


def _prep_weights(w_in, w_out, conv_w, a_log, dt_bias, gdn_norm_w, ln1_g, ln1_b, w_router_group,
                  w_router_expert, w_gate, w_up, w_down, ln2_g, ln2_b):
    o1 = Q_COLS + 2 * KV_COLS
    o2 = o1 + CONV_DIM
    o3 = o2 + Z_COLS
    pad_row = lambda v: jnp.pad(v.astype(F32), (0, LANES - v.shape[0]))[None, :]
    wab = jnp.pad(w_in[:, o3:], ((0, 0), (0, LANES - 2 * GDN_HEADS)))
    wr = jnp.pad(jnp.concatenate([w_router_group, w_router_expert], axis=1),
                 ((0, 0), (0, LANES - N_GROUPS - N_EXPERTS)))
    wr_hi = wr.astype(BF16)
    wr_mid = (wr - wr_hi.astype(F32)).astype(BF16)
    group = ATTN_HEADS // ATTN_KV_HEADS
    head_order = jnp.array([h for j in range(group) for h in (j, j + group)], I32)
    col_order = (head_order[:, None] * HEAD_DIM + jnp.arange(HEAD_DIM, dtype=I32)[None, :]).reshape(-1)
    wqkv = jnp.concatenate([w_in[:, :Q_COLS][:, col_order], w_in[:, Q_COLS:o1]], axis=1)
    wo = jnp.concatenate([w_out[:Q_COLS][col_order], w_out[Q_COLS:]], axis=0)
    return dict(
        wqkv=wqkv.astype(BF16), wg=w_in[:, o1:o2].astype(BF16), wz=w_in[:, o2:o3].astype(BF16),
        wab=wab.astype(BF16), convw=conv_w.astype(F32), alog=pad_row(a_log), dtb=pad_row(dt_bias),
        norm_w=gdn_norm_w.astype(F32)[None, :], wo=wo.astype(BF16),
        ln1_g=ln1_g[None, :], ln1_b=ln1_b[None, :], wr=jnp.stack([wr_hi, wr_mid]),
        w_gate=w_gate, w_up=w_up, w_down=w_down, ln2_g=ln2_g[None, :], ln2_b=ln2_b[None, :])


def _layer(x_prompt, x_sample, cache_k, cache_v, state_gdn, state_conv, wts):
    bp, sp, _ = x_prompt.shape
    bs, ts, _ = x_sample.shape
    n_p = bp * sp

    xp = x_prompt.reshape(n_p, D_MODEL)
    (q, k, v, qg, kg, vg, z, gcb, utail) = _proj(xp, jnp.arange(sp, dtype=I32), wts, GDN_CHUNK, bp)
    attn_p = _attn_prompt(q, k, v, wts['sinks'], bp)
    gdn_p, s_p = _gdn_prompt(qg, kg, vg, z, gcb, wts['norm_w'], bp)
    last_win = lambda a: a.reshape(bp, sp, KV_COLS)[:, sp - WINDOW:].reshape(bp, WINDOW, ATTN_KV_HEADS,
                                                                            HEAD_DIM)
    new_k_p, new_v_p = last_win(k), last_win(v)
    tiles_per_seq = sp // min(PROJ_TILE, sp)
    conv_p = utail.reshape(bp, tiles_per_seq, SUBLANES, CONV_DIM)[:, -1, SUBLANES - (CONV_W - 1):]

    lo, hi = SAMPLE_FIRST, SAMPLE_FIRST + ts
    xs_rows = jnp.pad(x_sample, ((0, 0), (lo, SAMPLE_SLOTS - hi), (0, 0))).reshape(bs * SAMPLE_SLOTS, D_MODEL)
    hist = jnp.pad(state_conv, ((0, 0), (0, SAMPLE_SLOTS - lo), (0, 0))).reshape(bs * SAMPLE_SLOTS, CONV_DIM)
    slot = jnp.arange(SAMPLE_SLOTS, dtype=I32)
    valid = jnp.tile(((slot >= lo) & (slot < hi)).astype(F32), bs)[:, None]
    pos_s = jnp.tile(PAST_LEN + slot - lo, bs)
    (q, k, v, qg, kg, vg, z, gcb, u_s) = _proj(xs_rows, pos_s, wts, SAMPLE_SLOTS, 1, hist, valid)
    ck = cache_k.reshape(bs, WINDOW, KV_COLS)
    cv = cache_v.reshape(bs, WINDOW, KV_COLS)
    attn_s, kwin, vwin = _attn_sample(q, k, v, ck, cv, wts['sinks'], bs, ts)
    gdn_s, s_s = _gdn_sample(qg, kg, vg, z, gcb, wts['norm_w'], state_gdn)
    real = lambda a: a.reshape(bs, SAMPLE_SLOTS, -1)[:, lo:hi]
    new_k_s = kwin.reshape(bs, WINDOW, ATTN_KV_HEADS, HEAD_DIM)
    new_v_s = vwin.reshape(bs, WINDOW, ATTN_KV_HEADS, HEAD_DIM)
    conv_s = u_s.reshape(bs, SAMPLE_SLOTS, CONV_DIM)[:, hi - (CONV_W - 1):hi]

    x1_p, route_p = _post(attn_p, gdn_p, xp, wts)
    x1_s, route_s = _post(real(attn_s).reshape(bs * ts, Q_COLS), real(gdn_s).reshape(bs * ts, Z_COLS),
                          x_sample.reshape(bs * ts, D_MODEL), wts)
    y_p, y_s = _moe([x1_p, x1_s], [route_p, route_s], wts)
    return (y_p.reshape(bp, sp, D_MODEL), y_s.reshape(bs, ts, D_MODEL), new_k_p, new_v_p, s_p, conv_p,
            new_k_s, new_v_s, s_s, conv_s)


def kernel(x_prompt, x_sample, cache_attn_k, cache_attn_v, state_gdn, state_conv, w_in, w_out,
           attn_sinks, conv_w, a_log, dt_bias, gdn_norm_w, ln1_g, ln1_b, w_router_group,
           w_router_expert, w_gate, w_up, w_down, ln2_g, ln2_b):
    assert w_in.shape[0] == DEPTH
    l = 0
    wts = _prep_weights(w_in[l], w_out[l], conv_w[l], a_log[l], dt_bias[l], gdn_norm_w[l], ln1_g[l],
                        ln1_b[l], w_router_group[l], w_router_expert[l], w_gate[l], w_up[l],
                        w_down[l], ln2_g[l], ln2_b[l])
    wts['sinks'] = attn_sinks[l]
    outs = _layer(x_prompt, x_sample, cache_attn_k[l], cache_attn_v[l], state_gdn[l], state_conv[l], wts)
    (y_p, y_s, k_p, v_p, s_p, c_p, k_s, v_s, s_s, c_s) = outs
    add = lambda a: a[None]
    return (y_p, y_s, add(k_p), add(v_p), add(s_p), add(c_p), add(k_s), add(v_s), add(s_s), add(c_s))
```

```python
import functools
import math

import jax
import jax.numpy as jnp
import numpy as np
from jax import lax
from jax.experimental import pallas as pl
from jax.experimental.pallas import tpu as pltpu

F32 = jnp.float32
BF16 = jnp.bfloat16
I32 = jnp.int32

D_MODEL = 1024
ATTN_HEADS = 8
ATTN_KV_HEADS = 2
HEAD_DIM = 64
WINDOW = 128
ROT_DIM = HEAD_DIM // 4
ROPE_THETA = 500000.0
GDN_HEADS = 4
GDN_DK = 128
GDN_DV = 128
CONV_W = 4
QK_COLS = GDN_HEADS * GDN_DK
CONV_DIM = 2 * QK_COLS + GDN_HEADS * GDN_DV
Z_COLS = GDN_HEADS * GDN_DV
Q_COLS = ATTN_HEADS * HEAD_DIM
KV_COLS = ATTN_KV_HEADS * HEAD_DIM
N_GROUPS = 4
EXPERTS_PER_GROUP = 8
N_EXPERTS = N_GROUPS * EXPERTS_PER_GROUP
TOP_K = 2
EXPERT_FF = 256
NORM_EPS = 1e-5
L2_EPS = 1e-6
DEPTH = 1
DEEPNORM_ALPHA = (2 * DEPTH) ** 0.25
PAST_LEN = 8192

LANES = 128
SUBLANES = 8
IN_SPLITS = (0, Q_COLS + 2 * KV_COLS, Q_COLS + 2 * KV_COLS + CONV_DIM,
             Q_COLS + 2 * KV_COLS + CONV_DIM + Z_COLS, Q_COLS + 2 * KV_COLS + CONV_DIM + Z_COLS + LANES)
IN_COLS_PAD = IN_SPLITS[-1]
TOK_TILE = 512
PROJ_TILE = 512
PROJ_SUB = 128
POST_SUB = 256
GDN_CHUNK = 128
GDN_SEQ_PER_STEP = 4
ATTN_BLOCKS_PER_STEP = 4
ATTN_SEQS_PER_STEP = 8
INV_BASE = 16
SAMPLE_SLOTS = 8
SAMPLE_FIRST = CONV_W - 1
ROW_TILE = 512
EXPERT_SUB = 256
SLAB_UNROLL = 4
SLAB = 16
PERM_ROWS = TOP_K * TOK_TILE + N_EXPERTS * SLAB
PERM_SLABS = PERM_ROWS // SLAB
XS_WORDS = D_MODEL + LANES
ZERO_TABLE = N_EXPERTS * (ROW_TILE // SLAB)
VMEM_LIMIT = 48 * 1024 * 1024
NEG_BIG = -1e30


def _cparams(sem):
    return pltpu.CompilerParams(dimension_semantics=sem, vmem_limit_bytes=VMEM_LIMIT)


def _bdot(a, b):
    return jnp.dot(a.astype(BF16), b.astype(BF16), preferred_element_type=F32)


def _bdot_nt(a, b):
    return lax.dot_general(a.astype(BF16), b.astype(BF16), (((1,), (1,)), ((), ())),
                           preferred_element_type=F32)


def _bdot_tn(a, b):
    return lax.dot_general(a.astype(BF16), b.astype(BF16), (((0,), (0,)), ((), ())),
                           preferred_element_type=F32)


def _div_pow2(x, n):
    return jnp.right_shift(x, int(math.log2(n)))


def _mod_pow2(x, n):
    return jnp.bitwise_and(x, n - 1)


def _split3(x):
    hi = x.astype(BF16)
    r = x - hi.astype(F32)
    mid = r.astype(BF16)
    lo = (r - mid.astype(F32)).astype(BF16)
    return hi, mid, lo


def _dot_exact_lhs01(m01, x):
    hi, mid, lo = _split3(x)
    d = lambda t: jnp.dot(m01, t, preferred_element_type=F32)
    return d(hi) + d(mid) + d(lo)


def _sigmoid(x):
    return 1.0 / (1.0 + jnp.exp(-x))


def _silu(x):
    return x * _sigmoid(x)


def _softplus(x):
    return jnp.maximum(x, 0.0) + jnp.log1p(jnp.exp(-jnp.abs(x)))


def _layer_norm(h, g, b):
    mu = jnp.mean(h, axis=-1, keepdims=True)
    d = h - mu
    var = jnp.mean(d * d, axis=-1, keepdims=True)
    return d * lax.rsqrt(var + NORM_EPS) * g + b


def _proj_kernel(*refs, tm, has_hist, full_u, one_segment):
    it = iter(refs)
    x_ref, cos_ref, sin_ref = next(it), next(it), next(it)
    w_ref = next(it)
    convw_ref, alog_ref, dtb_ref, tri_ref, seg_ref = next(it), next(it), next(it), next(it), next(it)
    hist_ref = valid_ref = None
    if has_hist:
        hist_ref, valid_ref = next(it), next(it)
    q_ref, k_ref, v_ref = next(it), next(it), next(it)
    qg_ref, kg_ref, vg_ref, z_ref, gcb_ref, u_ref = (next(it) for _ in range(6))
    ubuf = next(it)

    t = pl.program_id(1)
    sub = PROJ_SUB
    rows = [slice(j * sub, (j + 1) * sub) for j in range(tm // sub)]
    lane = lax.broadcasted_iota(I32, (sub, LANES), 1)
    first_half = _mod_pow2(lane, HEAD_DIM) < (ROT_DIM // 2)

    @pl.when(t == 0)
    def _():
        ubuf[0:SUBLANES, :] = jnp.zeros((SUBLANES, CONV_DIM), F32)

    @pl.when(t > 0)
    def _():
        ubuf[0:SUBLANES, :] = ubuf[tm:tm + SUBLANES, :]

    dots = []
    for r in rows:
        xb = x_ref[r, :].astype(BF16)
        dots.append([jnp.dot(xb, w_ref[:, lo:hi], preferred_element_type=F32)
                     for lo, hi in zip(IN_SPLITS[:-1], IN_SPLITS[1:])])

    def l2n(s):
        return s * lax.rsqrt(jnp.sum(s * s, axis=1, keepdims=True) + L2_EPS)

    for r, (pq, u, z, ab) in zip(rows, dots):
        cosv, sinv = cos_ref[r, :], sin_ref[r, :]

        def rope(s):
            sw = jnp.where(first_half, pltpu.roll(s, LANES - ROT_DIM // 2, axis=1),
                           pltpu.roll(s, ROT_DIM // 2, axis=1))
            return s * cosv + sw * sinv

        for j in range(Q_COLS // LANES):
            q_ref[r, j * LANES:(j + 1) * LANES] = rope(pq[:, j * LANES:(j + 1) * LANES])
        k_ref[r, :] = rope(pq[:, Q_COLS:Q_COLS + KV_COLS])
        v_ref[r, :] = pq[:, Q_COLS + KV_COLS:Q_COLS + 2 * KV_COLS]
        z_ref[r, :] = z

        if has_hist:
            u = u + hist_ref[r, :]
        if full_u:
            u_ref[r, :] = u
        elif r.stop == tm:
            u_ref[...] = u[sub - SUBLANES:, :]
        base = SUBLANES + r.start
        ubuf[base:base + sub, :] = u
        acc = u * convw_ref[CONV_W - 1:CONV_W, :]
        for j in range(1, CONV_W):
            acc = acc + ubuf[base - j:base - j + sub, :] * convw_ref[CONV_W - 1 - j:CONV_W - j, :]
        c = _silu(acc)
        if has_hist:
            c = c * valid_ref[r, :]
        for h in range(GDN_HEADS):
            sl = slice(h * GDN_DK, (h + 1) * GDN_DK)
            qg_ref[r, sl] = l2n(c[:, sl]) * (GDN_DK ** -0.5)
            kg_ref[r, sl] = l2n(c[:, QK_COLS + h * GDN_DK:QK_COLS + (h + 1) * GDN_DK])
        vg_ref[r, :] = c[:, 2 * QK_COLS:]

        g = -jnp.exp(alog_ref[...]) * _softplus(ab + dtb_ref[...])
        beta = _sigmoid(ab)
        if has_hist:
            g = g * valid_ref[r, :]
            beta = beta * valid_ref[r, :]
        g = jnp.where(lane < GDN_HEADS, g, 0.0)
        gc = _dot_exact_lhs01(tri_ref[...], g)
        if one_segment:
            gl = jnp.broadcast_to(gc[sub - 1:sub, :], (sub, LANES))
        else:
            gl = _dot_exact_lhs01(seg_ref[...], g)
        gcb_ref[r, :] = jnp.where(lane < GDN_HEADS, gc,
                                  jnp.where(lane < 2 * GDN_HEADS, beta,
                                            jnp.where(lane < 3 * GDN_HEADS,
                                                      pltpu.roll(gl, 2 * GDN_HEADS, axis=1), 0.0)))


def _rope_tables(pos):
    half = ROT_DIM // 2
    pos = np.asarray(pos, np.float64)
    inv_freq = ROPE_THETA ** (-np.arange(half, dtype=np.float64) * 2.0 / ROT_DIM)
    ang = pos[:, None] * inv_freq[None, :]
    cos, sin = np.cos(ang), np.sin(ang)
    p = pos.shape[0]
    cpat = np.concatenate([cos, cos, np.ones((p, HEAD_DIM - ROT_DIM))], axis=1)
    spat = np.concatenate([-sin, sin, np.zeros((p, HEAD_DIM - ROT_DIM))], axis=1)
    rep = (1, LANES // HEAD_DIM)
    return jnp.asarray(np.tile(cpat, rep), F32), jnp.asarray(np.tile(spat, rep), F32)


def _segment_matrices(tm, seg_len):
    i = np.arange(tm)
    same = (i[:, None] // seg_len) == (i[None, :] // seg_len)
    tri = same & (i[None, :] <= i[:, None])
    return jnp.asarray(tri, BF16), jnp.asarray(same, BF16)


def _proj(x, pos, wts, seg_len, n_seq, hist=None, valid=None):
    n = x.shape[0]
    rows = n // n_seq
    tm = min(PROJ_TILE, rows)
    nt = rows // tm
    has_hist = hist is not None
    cos_t, sin_t = _rope_tables(pos)
    tri, seg = _segment_matrices(PROJ_SUB, seg_len)

    tok = lambda w: pl.BlockSpec((tm, w), lambda b, t: (b * nt + t, 0))
    const = lambda a: pl.BlockSpec(a.shape, lambda b, t: (0,) * a.ndim)
    in_arrays = [x, cos_t, sin_t, wts['w_all'], wts['convw'], wts['alog'], wts['dtb'], tri, seg]
    in_specs = [tok(D_MODEL), pl.BlockSpec((tm, LANES), lambda b, t: (t, 0)),
                pl.BlockSpec((tm, LANES), lambda b, t: (t, 0))] + [const(a) for a in in_arrays[3:]]
    if has_hist:
        in_arrays += [hist, valid]
        in_specs += [tok(CONV_DIM), tok(1)]
    u_rows = n if has_hist else (n // tm) * SUBLANES
    u_block = tm if has_hist else SUBLANES
    out_shape = [jax.ShapeDtypeStruct((n, Q_COLS), F32), jax.ShapeDtypeStruct((n, KV_COLS), F32),
                 jax.ShapeDtypeStruct((n, KV_COLS), F32), jax.ShapeDtypeStruct((n, QK_COLS), F32),
                 jax.ShapeDtypeStruct((n, QK_COLS), F32), jax.ShapeDtypeStruct((n, Z_COLS), F32),
                 jax.ShapeDtypeStruct((n, Z_COLS), F32), jax.ShapeDtypeStruct((n, LANES), F32),
                 jax.ShapeDtypeStruct((u_rows, CONV_DIM), F32)]
    out_specs = [tok(Q_COLS), tok(KV_COLS), tok(KV_COLS), tok(QK_COLS), tok(QK_COLS), tok(Z_COLS),
                 tok(Z_COLS), tok(LANES),
                 pl.BlockSpec((u_block, CONV_DIM), lambda b, t: (b * nt + t, 0))]
    return pl.pallas_call(
        functools.partial(_proj_kernel, tm=tm, has_hist=has_hist, full_u=has_hist,
                          one_segment=seg_len == PROJ_SUB),
        out_shape=out_shape, grid=(n_seq, nt), in_specs=in_specs, out_specs=out_specs,
        scratch_shapes=[pltpu.VMEM((tm + SUBLANES, CONV_DIM), F32)],
        compiler_params=_cparams(("arbitrary", "arbitrary")),
        name="proj_hist" if has_hist else "proj",
    )(*in_arrays)


def _attn_blocks(qs, kcats, vcats, biases, sink, tq):
    lane = lax.broadcasted_iota(I32, (tq, LANES), 1)
    low = lane < HEAD_DIM
    n_slab = Q_COLS // LANES

    def stack(q):
        slabs = [q[:, j * LANES:(j + 1) * LANES] * (HEAD_DIM ** -0.5) for j in range(n_slab)]
        parts = ([jnp.where(low, s, 0.0) for s in slabs] + [jnp.where(low, 0.0, s) for s in slabs])
        return jnp.concatenate(parts, axis=0).astype(BF16)

    def unstack(o8):
        return [jnp.where(low, o8[j * tq:(j + 1) * tq, :], o8[(n_slab + j) * tq:(n_slab + j + 1) * tq, :])
                for j in range(n_slab)]

    rows = ATTN_HEADS * tq
    half = rows // 2
    klow = lax.broadcasted_iota(I32, (2 * WINDOW, LANES), 1) < HEAD_DIM
    one = jnp.ones((), BF16)
    q8s = _each(stack, qs)
    ss = _each(lambda q8, kc, b: _bdot_nt(q8, kc) + b, q8s, kcats, biases)
    ms = _each(lambda s: jnp.maximum(jnp.broadcast_to(jnp.max(s, axis=1, keepdims=True), (rows, LANES)),
                                     sink), ss)
    ps = _each(lambda s, m: jnp.exp(s - jnp.concatenate([m, m], axis=1)).astype(BF16), ss, ms)
    pv0 = _each(lambda p, vc: jnp.dot(p[:half], jnp.where(klow, vc, one), preferred_element_type=F32),
                ps, vcats)
    pv1 = _each(lambda p, vc: jnp.dot(p[half:], jnp.where(klow, one, vc), preferred_element_type=F32),
                ps, vcats)
    pvs = _each(lambda a, b: jnp.concatenate([a, b], axis=0), pv0, pv1)
    o8s = _each(lambda pv, m: pv / (pltpu.roll(pv, HEAD_DIM, axis=1) + jnp.exp(sink - m)), pvs, ms)
    return _each(unstack, o8s)


def _attn_prompt_kernel(q_ref, kc_ref, vc_ref, kp_ref, vp_ref, bias0_ref, bias_ref, sink_ref, o_ref, *,
                        nblk):
    kall = jnp.concatenate([kp_ref[...], kc_ref[...]], axis=0).astype(BF16)
    vall = jnp.concatenate([vp_ref[...], vc_ref[...]], axis=0).astype(BF16)
    win = lambda a, j: a[j * WINDOW:(j + 2) * WINDOW, :]
    qs = [q_ref[j * WINDOW:(j + 1) * WINDOW, :] for j in range(nblk)]
    biases = [bias0_ref[0]] + [bias_ref[...]] * (nblk - 1)
    outs = _attn_blocks(qs, [win(kall, j) for j in range(nblk)], [win(vall, j) for j in range(nblk)],
                        biases, sink_ref[...], WINDOW)
    for j, slabs in enumerate(outs):
        for c, slab in enumerate(slabs):
            o_ref[j * WINDOW:(j + 1) * WINDOW, c * LANES:(c + 1) * LANES] = slab


def _attn_sample_kernel(q_ref, kc_ref, vc_ref, kp_ref, vp_ref, bias_ref, sink_ref, o_ref, kw_ref, vw_ref,
                        *, nseq, n_new):
    tq = SAMPLE_SLOTS
    zpad = jnp.zeros((WINDOW - tq, LANES), F32)
    rows = lambda ref, j: ref[j * tq:(j + 1) * tq, :]
    cat = lambda pref, cref, j: jnp.concatenate([pref[j], rows(cref, j), zpad], axis=0).astype(BF16)
    outs = _attn_blocks([rows(q_ref, j) for j in range(nseq)],
                        [cat(kp_ref, kc_ref, j) for j in range(nseq)],
                        [cat(vp_ref, vc_ref, j) for j in range(nseq)],
                        [bias_ref[...]] * nseq, sink_ref[...], tq)
    for j, slabs in enumerate(outs):
        for c, slab in enumerate(slabs):
            o_ref[j * tq:(j + 1) * tq, c * LANES:(c + 1) * LANES] = slab
    row = lax.broadcasted_iota(I32, (WINDOW, LANES), 0)
    keep = WINDOW - n_new
    for pref, cref, wref in ((kp_ref, kc_ref, kw_ref), (vp_ref, vc_ref, vw_ref)):
        for j in range(nseq):
            new = jnp.concatenate([rows(cref, j), zpad], axis=0)
            wref[j] = jnp.where(row < keep, pltpu.roll(pref[j], keep, axis=0),
                                pltpu.roll(new, keep - SAMPLE_FIRST, axis=0))


def _sink_rows(sinks, tq):
    return jnp.broadcast_to(jnp.repeat(sinks.astype(F32), tq)[:, None], (ATTN_HEADS * tq, LANES))


def _attn_bias(tq, q_off, k_lo, k_hi, has_prev):
    qi = (np.arange(ATTN_HEADS * tq) % tq)[:, None]
    c = np.arange(2 * WINDOW)[None, :]
    cj = c - WINDOW
    vis_prev = (c < WINDOW) & (c > qi - q_off) & has_prev
    vis_cur = (c >= WINDOW) & (cj <= qi) & (cj >= k_lo) & (cj <= k_hi)
    return np.where(vis_prev | vis_cur, 0.0, NEG_BIG).astype(np.float32)


def _attn_prompt(q, k, v, sinks, n_seq):
    n = q.shape[0]
    nb = n // n_seq // WINDOW
    nblk = min(ATTN_BLOCKS_PER_STEP, nb)
    steps = nb // nblk
    tq = nblk * WINDOW
    cur = lambda w: pl.BlockSpec((tq, w), lambda b, i: (b * steps + i, 0))
    prev = pl.BlockSpec((WINDOW, LANES), lambda b, i: (b * nb + jnp.maximum(i * nblk - 1, 0), 0))
    bias2 = jnp.asarray(np.stack([_attn_bias(WINDOW, 0, 0, WINDOW - 1, False),
                                  _attn_bias(WINDOW, 0, 0, WINDOW - 1, True)]))
    rows = ATTN_HEADS * WINDOW
    return pl.pallas_call(
        functools.partial(_attn_prompt_kernel, nblk=nblk),
        out_shape=jax.ShapeDtypeStruct((n, Q_COLS), F32), grid=(n_seq, steps),
        in_specs=[cur(Q_COLS), cur(LANES), cur(LANES), prev, prev,
                  pl.BlockSpec((1, rows, 2 * WINDOW), lambda b, i: (jnp.minimum(i, 1), 0, 0)),
                  pl.BlockSpec((rows, 2 * WINDOW), lambda b, i: (0, 0)),
                  pl.BlockSpec((rows, LANES), lambda b, i: (0, 0))],
        out_specs=cur(Q_COLS),
        compiler_params=_cparams(("arbitrary", "arbitrary")), name="attn_prompt",
    )(q, k, v, k, v, bias2, bias2[1], _sink_rows(sinks, WINDOW))


def _attn_sample(q, k, v, cache_k, cache_v, sinks, n_seq, n_new):
    tq = SAMPLE_SLOTS
    nseq = min(ATTN_SEQS_PER_STEP, n_seq)
    cur = lambda w: pl.BlockSpec((nseq * tq, w), lambda b: (b, 0))
    prev = pl.BlockSpec((nseq, WINDOW, LANES), lambda b: (b, 0, 0))
    bias = jnp.asarray(_attn_bias(tq, SAMPLE_FIRST, SAMPLE_FIRST, SAMPLE_FIRST + 3, True))
    win = jax.ShapeDtypeStruct((n_seq, WINDOW, LANES), F32)
    return pl.pallas_call(
        functools.partial(_attn_sample_kernel, nseq=nseq, n_new=n_new),
        out_shape=[jax.ShapeDtypeStruct((n_seq * tq, Q_COLS), F32), win, win], grid=(n_seq // nseq,),
        in_specs=[cur(Q_COLS), cur(LANES), cur(LANES), prev, prev,
                  pl.BlockSpec(bias.shape, lambda b: (0, 0)),
                  pl.BlockSpec((ATTN_HEADS * tq, LANES), lambda b: (0, 0))],
        out_specs=[cur(Q_COLS), prev, prev],
        compiler_params=_cparams(("arbitrary",)), name="attn_sample",
    )(q, k, v, cache_k, cache_v, bias, _sink_rows(sinks, tq))


def _each(f, *lists):
    return [f(*args) for args in zip(*lists)]


def _unit_lower_inverse(ms, eye, same_base):
    c = ms[0].shape[0]

    def neumann(q0s, n_factors):
        xs = _each(lambda q: eye + q, q0s)
        if n_factors == 1:
            return xs
        qs = _each(_bdot, q0s, q0s)
        for _ in range(n_factors - 2):
            prods = _each(lambda x, q: _bdot(jnp.concatenate([x, q], axis=0), q), xs, qs)
            xs = _each(lambda x, pr: x + pr[:c], xs, prods)
            qs = _each(lambda pr: pr[c:], prods)
        return _each(lambda x, q: x + _bdot(x, q), xs, qs)

    ds = _each(lambda m: jnp.where(same_base, m, 0.0), ms)
    xs = neumann(_each(lambda d: -d, ds), int(math.log2(INV_BASE)))
    nblk = c // INV_BASE
    if nblk == 1:
        return xs
    ls = _each(lambda m, d: m - d, ms, ds)
    ns = _each(lambda x, l: -_bdot(x, l), xs, ls)
    ys = neumann(ns, int(math.log2(nblk)))
    return _each(_bdot, ys, xs)


def _gdn_intra(qs, ks, vs, gcs, gls, betas, same_seq, low_incl, low_strict, eye, same_base):
    del same_seq
    e_gcs = _each(jnp.exp, gcs)

    def decay_of(gc):
        gc_row = jnp.sum(jnp.where(eye > 0, gc, 0.0), axis=0, keepdims=True)
        return jnp.where(low_incl, jnp.exp(jnp.where(low_incl, gc - gc_row, 0.0)), 0.0)

    c = qs[0].shape[0]
    decays = _each(decay_of, gcs)
    kbs = _each(lambda k, b: k * b, ks, betas)
    vbs = _each(lambda v, b: v * b, vs, betas)
    kqs = _each(lambda kb, q, k: _bdot_nt(jnp.concatenate([kb, q], axis=0), k), kbs, qs, ks)
    ms = _each(lambda kq, d: jnp.where(low_strict, kq[:c] * d, 0.0), kqs, decays)
    attns = _each(lambda kq, d: jnp.where(low_incl, kq[c:] * d, 0.0), kqs, decays)
    tmats = _unit_lower_inverse(ms, eye, same_base)
    uws = _each(lambda t, vb, kb, e: _bdot(t, jnp.concatenate([vb, kb * e], axis=1)),
                tmats, vbs, kbs, e_gcs)
    us = _each(lambda uw: uw[:, :GDN_DV], uws)
    ws = _each(lambda uw: uw[:, GDN_DV:], uws)
    q_decs = _each(lambda q, e: q * e, qs, e_gcs)
    k_decs = _each(lambda k, gl, gc: k * jnp.exp(gl - gc), ks, gls, gcs)
    return us, ws, attns, q_decs, k_decs


def _chunk_masks(c, seq_len):
    i = lax.broadcasted_iota(I32, (c, c), 0)
    j = lax.broadcasted_iota(I32, (c, c), 1)
    same_seq = _div_pow2(i, seq_len) == _div_pow2(j, seq_len)
    low_incl = same_seq & (i >= j)
    low_strict = same_seq & (i > j)
    eye = (i == j).astype(F32)
    same_base = _div_pow2(i, INV_BASE) == _div_pow2(j, INV_BASE)
    return same_seq, low_incl, low_strict, eye, same_base


def _gated_rms(o, z, nw):
    o = o * lax.rsqrt(jnp.mean(o * o, axis=1, keepdims=True) + NORM_EPS) * nw
    return o * _silu(z)


def _gdn_prompt_kernel(qg_ref, kg_ref, vg_ref, z_ref, gcb_ref, nw_ref, o_ref, s_out_ref, s_scr):
    c = GDN_CHUNK
    n = pl.program_id(1)

    @pl.when(n == 0)
    def _():
        s_scr[...] = jnp.zeros_like(s_scr)

    masks = _chunk_masks(c, c)
    nw = nw_ref[...]
    chains = [(b, h) for b in range(qg_ref.shape[0]) for h in range(GDN_HEADS)]
    hs = lambda h: slice(h * GDN_DK, (h + 1) * GDN_DK)
    col = lambda off: [gcb_ref[b, :, off + h:off + h + 1] for b, h in chains]
    gcs, betas, gls = col(0), col(GDN_HEADS), col(2 * GDN_HEADS)
    qs = [qg_ref[b, :, hs(h)] for b, h in chains]
    ks = [kg_ref[b, :, hs(h)] for b, h in chains]
    vs = [vg_ref[b, :, hs(h)] for b, h in chains]
    us, ws, attns, q_decs, k_decs = _gdn_intra(qs, ks, vs, gcs, gls, betas, *masks)
    ss = [s_scr[b, h] for b, h in chains]
    wqs = _each(lambda w, qd, s: _bdot(jnp.concatenate([w, qd], axis=0), s), ws, q_decs, ss)
    wss = _each(lambda wq: wq[:c], wqs)
    qss = _each(lambda wq: wq[c:], wqs)
    v_news = _each(lambda u, x: u - x, us, wss)
    avs = _each(_bdot, attns, v_news)
    kvs = _each(_bdot_tn, k_decs, v_news)
    for (b, h), s, gl, qsv, av, kv in zip(chains, ss, gls, qss, avs, kvs):
        s_scr[b, h] = s * jnp.exp(gl[0:1, :]) + kv
        o_ref[b, :, hs(h)] = _gated_rms(qsv + av, z_ref[b, :, hs(h)], nw)

    @pl.when(n == pl.num_programs(1) - 1)
    def _():
        s_out_ref[...] = s_scr[...]


def _gdn_prompt(qg, kg, vg, z, gcb, norm_w, n_seq):
    n = qg.shape[0]
    s_len = n // n_seq
    nb = min(GDN_SEQ_PER_STEP, n_seq)
    v3 = lambda a: a.reshape(n_seq, s_len, a.shape[-1])
    tok = lambda w: pl.BlockSpec((nb, GDN_CHUNK, w), lambda b, i: (b, i, 0))
    o, s = pl.pallas_call(
        _gdn_prompt_kernel,
        out_shape=[jax.ShapeDtypeStruct((n_seq, s_len, Z_COLS), F32),
                   jax.ShapeDtypeStruct((n_seq, GDN_HEADS, GDN_DK, GDN_DV), F32)],
        grid=(n_seq // nb, s_len // GDN_CHUNK),
        in_specs=[tok(QK_COLS), tok(QK_COLS), tok(Z_COLS), tok(Z_COLS), tok(LANES),
                  pl.BlockSpec((1, GDN_DV), lambda b, i: (0, 0))],
        out_specs=[tok(Z_COLS),
                   pl.BlockSpec((nb, GDN_HEADS, GDN_DK, GDN_DV), lambda b, i: (b, 0, 0, 0))],
        scratch_shapes=[pltpu.VMEM((nb, GDN_HEADS, GDN_DK, GDN_DV), F32)],
        compiler_params=_cparams(("arbitrary", "arbitrary")), name="gdn_prompt",
    )(v3(qg), v3(kg), v3(vg), v3(z), v3(gcb), norm_w)
    return o.reshape(n, Z_COLS), s


def _gdn_sample_kernel(qg_ref, kg_ref, vg_ref, z_ref, gcb_ref, nw_ref, s_in_ref, o_ref, s_out_ref):
    c = GDN_CHUNK
    n_sub = c // SAMPLE_SLOTS
    masks = _chunk_masks(c, SAMPLE_SLOTS)
    heads = range(GDN_HEADS)
    hs = lambda h: slice(h * GDN_DK, (h + 1) * GDN_DK)
    rs = lambda s: slice(s * SAMPLE_SLOTS, (s + 1) * SAMPLE_SLOTS)
    col = lambda off: [gcb_ref[:, off + h:off + h + 1] for h in heads]
    gcs, betas, gls = col(0), col(GDN_HEADS), col(2 * GDN_HEADS)
    us, ws, attns, q_decs, k_decs = _gdn_intra([qg_ref[:, hs(h)] for h in heads],
                                               [kg_ref[:, hs(h)] for h in heads],
                                               [vg_ref[:, hs(h)] for h in heads], gcs, gls, betas, *masks)
    pairs = [(h, s) for h in heads for s in range(n_sub)]
    sts = [s_in_ref[s, h] for h, s in pairs]
    boths = [jnp.concatenate([ws[h][rs(s), :], q_decs[h][rs(s), :]], axis=0) for h, s in pairs]
    rr = _each(_bdot, boths, sts)
    gather = lambda h, part: jnp.concatenate(
        [rr[h * n_sub + s][part * SAMPLE_SLOTS:(part + 1) * SAMPLE_SLOTS, :] for s in range(n_sub)], axis=0)
    v_news = [us[h] - gather(h, 0) for h in heads]
    avs = _each(_bdot, attns, v_news)
    row = lax.broadcasted_iota(I32, (c, LANES), 0)
    seq_of_row = _div_pow2(row, SAMPLE_SLOTS)
    kds = [jnp.where(seq_of_row == s, k_decs[h], 0.0) for h, s in pairs]
    kvs = _each(_bdot_tn, kds, [v_news[h] for h, _ in pairs])
    egls = _each(jnp.exp, gls)
    for (h, s), st, kv in zip(pairs, sts, kvs):
        s_out_ref[s, h] = st * egls[h][s * SAMPLE_SLOTS:s * SAMPLE_SLOTS + 1, :] + kv
    nw = nw_ref[...]
    for h in heads:
        o_ref[:, hs(h)] = _gated_rms(gather(h, 1) + avs[h], z_ref[:, hs(h)], nw)


def _gdn_sample(qg, kg, vg, z, gcb, norm_w, state):
    n = qg.shape[0]
    n_sub = GDN_CHUNK // SAMPLE_SLOTS
    tok = lambda w: pl.BlockSpec((GDN_CHUNK, w), lambda i: (i, 0))
    st = pl.BlockSpec((n_sub, GDN_HEADS, GDN_DK, GDN_DV), lambda i: (i, 0, 0, 0))
    return pl.pallas_call(
        _gdn_sample_kernel,
        out_shape=[jax.ShapeDtypeStruct((n, Z_COLS), F32),
                   jax.ShapeDtypeStruct(state.shape, F32)],
        grid=(n // GDN_CHUNK,),
        in_specs=[tok(QK_COLS), tok(QK_COLS), tok(Z_COLS), tok(Z_COLS), tok(LANES),
                  pl.BlockSpec((1, GDN_DV), lambda i: (0, 0)), st],
        out_specs=[tok(Z_COLS), st],
        compiler_params=_cparams(("arbitrary",)), name="gdn_sample",
    )(qg, kg, vg, z, gcb, norm_w, state)


def _post_kernel(a_ref, g_ref, x_ref, wo_ref, ln_g_ref, ln_b_ref, wr_ref, x1_ref, route_ref, *, tm):
    sub = POST_SUB
    rows = [slice(j * sub, (j + 1) * sub) for j in range(tm // sub)]
    d = lambda a, b: jnp.dot(a, b, preferred_element_type=F32)
    mixes = [d(a_ref[r, :].astype(BF16), wo_ref[0:Q_COLS, :]) + d(g_ref[r, :].astype(BF16), wo_ref[Q_COLS:, :])
             for r in rows]
    x1s = [_layer_norm(DEEPNORM_ALPHA * x_ref[r, :] + mix, ln_g_ref[...], ln_b_ref[...])
           for r, mix in zip(rows, mixes)]
    for r, x1 in zip(rows, x1s):
        x1_ref[r, :] = x1
    wh, wm = wr_ref[0], wr_ref[1]
    lgs = []
    for x1 in x1s:
        xh = x1.astype(BF16)
        xm = (x1 - xh.astype(F32)).astype(BF16)
        lgs.append(d(xh, wh) + d(xh, wm) + d(xm, wh))
    for r, lg in zip(rows, lgs):
        route_ref[:, r] = jnp.transpose(_route(lg))[0:SUBLANES, :]


def _route(lg):
    lane = lax.broadcasted_iota(I32, lg.shape, 1)
    lane_f = lane.astype(F32)
    big = float(LANES)

    def first_max(vals, mask):
        v = jnp.where(mask, vals, NEG_BIG)
        mx = jnp.max(v, axis=1, keepdims=True)
        idx = jnp.min(jnp.where(mask & (v == mx), lane_f, big), axis=1, keepdims=True)
        return mx, idx

    gmask = lane < N_GROUPS
    gmax, gidx = first_max(lg, gmask)
    gden = jnp.sum(jnp.where(gmask, jnp.exp(lg - gmax), 0.0), axis=1, keepdims=True)
    g_top_p = 1.0 / gden
    e_lane = lane - N_GROUPS
    e_group = _div_pow2(jnp.maximum(e_lane, 0), EXPERTS_PER_GROUP).astype(F32)
    emask = (e_lane >= 0) & (e_lane < N_EXPERTS) & (e_group == gidx)
    m1, i1 = first_max(lg, emask)
    eden = jnp.sum(jnp.where(emask, jnp.exp(lg - m1), 0.0), axis=1, keepdims=True)
    m2, i2 = first_max(lg, emask & (lane_f != i1))
    p1 = 1.0 / eden
    p2 = jnp.exp(m2 - m1) / eden
    tot = p1 + p2
    gate1 = g_top_p * (p1 / tot)
    gate2 = g_top_p * (p2 / tot)
    return jnp.where(lane == 0, gate1,
                     jnp.where(lane == 1, gate2,
                               jnp.where(lane == 2, i1 - N_GROUPS,
                                         jnp.where(lane == 3, i2 - N_GROUPS, 0.0))))


def _post(attn_o, gdn_o, x, wts):
    n = x.shape[0]
    tm = min(PROJ_TILE, n)
    tok = lambda w: pl.BlockSpec((tm, w), lambda i: (i, 0))
    const = lambda a: pl.BlockSpec(a.shape, lambda i: (0,) * a.ndim)
    consts = [wts['wo'], wts['ln1_g'], wts['ln1_b'], wts['wr']]
    return pl.pallas_call(
        functools.partial(_post_kernel, tm=tm),
        out_shape=[jax.ShapeDtypeStruct((n, D_MODEL), F32), jax.ShapeDtypeStruct((SUBLANES, n), F32)],
        grid=(n // tm,),
        in_specs=[tok(Q_COLS), tok(Z_COLS), tok(D_MODEL)] + [const(a) for a in consts],
        out_specs=[tok(D_MODEL), pl.BlockSpec((SUBLANES, tm), lambda i: (0, i))],
        compiler_params=_cparams(("arbitrary",)), name="post_%d" % (n // tm),
    )(attn_o, gdn_o, x, *consts)


def _slab_loop(n, body):
    n_main = jnp.right_shift(n, int(math.log2(SLAB_UNROLL)))

    def main(i, c):
        for u in range(SLAB_UNROLL):
            body(i * SLAB_UNROLL + u, u)
        return c

    lax.fori_loop(0, n_main, main, 0)
    lax.fori_loop(n_main * SLAB_UNROLL, n, lambda j, c: (body(j, 0), c)[1], 0)


def _dispatch_kernel(dst_ref, nslab_ref, ztab_ref, zinfo_ref, slot_ref, gate_ref, *rest,
                     group_tiles, max_tiles):
    x_refs = rest[:len(group_tiles)]
    xs_ref, pbuf, sem, zbuf, zsem = rest[len(group_tiles):]
    n_tiles = sum(group_tiles)
    g = pl.program_id(0)
    cur = lax.rem(g, 2)

    def slab_copy(tile, buf_slot, j):
        d = pl.multiple_of(dst_ref[tile * PERM_SLABS + j], SLAB)
        src = pbuf.at[buf_slot, pl.ds(pl.multiple_of(j * SLAB, SLAB), SLAB), :]
        return pltpu.make_async_copy(src, xs_ref.at[pl.ds(d, SLAB), :], sem.at[buf_slot])

    def tail_copy(k):
        d = pl.multiple_of(ztab_ref[k], SLAB)
        return pltpu.make_async_copy(zbuf.at[pl.ds(0, SLAB), :], xs_ref.at[pl.ds(d, SLAB), :], zsem)

    def tile_copy(t):
        d = pl.multiple_of(t * ROW_TILE, ROW_TILE)
        return pltpu.make_async_copy(zbuf, xs_ref.at[pl.ds(d, ROW_TILE), :], zsem)

    @pl.when(g == 0)
    def _():
        zbuf[...] = jnp.zeros_like(zbuf)

    share = -(-ZERO_TABLE // n_tiles)
    k0 = g * share
    _slab_loop(jnp.clip(zinfo_ref[0] - k0, 0, share), lambda j, u: tail_copy(k0 + j).start(priority=1))

    @pl.when(zinfo_ref[1] + g < max_tiles)
    def _():
        tile_copy(zinfo_ref[1] + g).start(priority=1)

    x = x_refs[-1][...]
    bound = n_tiles
    for x_ref, nt in zip(x_refs[-2::-1], group_tiles[:0:-1]):
        bound -= nt
        x = jnp.where(g < bound, x_ref[...], x)

    r = lax.broadcasted_iota(I32, (PERM_ROWS, TOK_TILE), 0)
    sl = slot_ref[0]
    hit0, hit1 = r == sl[0:1, :], r == sl[1:2, :]
    onehot = jnp.where(hit0 | hit1, 1.0, 0.0).astype(BF16)
    gt = gate_ref[0]
    gcol = jnp.sum(jnp.where(hit0, gt[0:1, :], 0.0) + jnp.where(hit1, gt[1:2, :], 0.0),
                   axis=1, keepdims=True)
    pbuf[cur, :, 0:D_MODEL] = jnp.dot(onehot, x.astype(BF16), preferred_element_type=F32).astype(BF16)
    g_hi = gcol.astype(BF16).astype(F32)
    lane = lax.broadcasted_iota(I32, (PERM_ROWS, LANES), 1)
    pbuf[cur, :, D_MODEL:] = jnp.where(lane < LANES // 2, g_hi, gcol - g_hi).astype(BF16)

    @pl.when(g > 0)
    def _():
        _slab_loop(nslab_ref[g - 1], lambda j, u: slab_copy(g - 1, 1 - cur, j).wait())

    _slab_loop(nslab_ref[g], lambda j, u: slab_copy(g, cur, j).start(priority=u % 2))

    @pl.when(g == n_tiles - 1)
    def _():
        _slab_loop(nslab_ref[g], lambda j, u: slab_copy(g, cur, j).wait())
        lax.fori_loop(zinfo_ref[1] + n_tiles, max_tiles,
                      lambda t, c: (tile_copy(t).start(priority=1), c)[1], 0)
        _slab_loop(zinfo_ref[0], lambda k, u: tail_copy(k).wait())
        lax.fori_loop(zinfo_ref[1], max_tiles, lambda t, c: (tile_copy(t).wait(), c)[1], 0)


def _dispatch(plan, x1s, max_tiles):
    group_tiles = tuple(x1.shape[0] // TOK_TILE for x1 in x1s)
    n_tiles = sum(group_tiles)
    tile = lambda i, d, ns, zt, zi: (i, 0, 0)
    in_specs = [pl.BlockSpec((1, TOP_K, TOK_TILE), tile), pl.BlockSpec((1, TOP_K, TOK_TILE), tile)]
    base = 0
    for nt in group_tiles:
        in_specs.append(pl.BlockSpec(
            (TOK_TILE, D_MODEL),
            lambda i, d, ns, zt, zi, base=base, nt=nt: (jnp.clip(i - base, 0, nt - 1), 0)))
        base += nt
    return pl.pallas_call(
        functools.partial(_dispatch_kernel, group_tiles=group_tiles, max_tiles=max_tiles),
        out_shape=jax.ShapeDtypeStruct((max_tiles * ROW_TILE, XS_WORDS), BF16),
        grid_spec=pltpu.PrefetchScalarGridSpec(
            num_scalar_prefetch=4, grid=(n_tiles,), in_specs=in_specs,
            out_specs=pl.BlockSpec(memory_space=pl.ANY),
            scratch_shapes=[pltpu.VMEM((2, PERM_ROWS, XS_WORDS), BF16), pltpu.SemaphoreType.DMA((2,)),
                            pltpu.VMEM((ROW_TILE, XS_WORDS), BF16), pltpu.SemaphoreType.DMA(())]),
        compiler_params=_cparams(("arbitrary",)), name="moe_dispatch",
    )(plan['slab_dst'], plan['nslab'], plan['ztab'], plan['zinfo'], plan['slot_rows'], plan['gate_rows'],
      *x1s)


def _expert_kernel(te_ref, nu_ref, xs_ref, wg_ref, wu_ref, wd_ref, ye_ref, wgu_scr, wd_scr):
    i = pl.program_id(0)
    active = i < nu_ref[0]

    @pl.when(active & ((i == 0) | (te_ref[i] != te_ref[jnp.maximum(i - 1, 0)])))
    def _():
        wgu_scr[:, 0:EXPERT_FF] = wg_ref[0].astype(BF16)
        wgu_scr[:, EXPERT_FF:] = wu_ref[0].astype(BF16)
        wd_scr[...] = wd_ref[0].astype(BF16)

    @pl.when(active)
    def _():
        sub = EXPERT_SUB
        rows = [slice(j * sub, (j + 1) * sub) for j in range(ROW_TILE // sub)]
        d = lambda a, b: jnp.dot(a, b, preferred_element_type=F32)
        wgu, wd = wgu_scr[...], wd_scr[...]
        xs = [xs_ref[r, 0:D_MODEL] for r in rows]
        hs = [d(x, wgu) for x in xs]
        hhs = [(_silu(h[:, :EXPERT_FF]) * h[:, EXPERT_FF:]).astype(BF16) for h in hs]
        ys = [d(hh, wd) for hh in hhs]
        for r, y in zip(rows, ys):
            parts = xs_ref[r, D_MODEL:].astype(F32)
            gate = parts + pltpu.roll(parts, LANES // 2, axis=1)
            ye_ref[r, :] = (y * jnp.concatenate([gate] * (D_MODEL // LANES), axis=1)).astype(BF16)

    @pl.when(jnp.logical_not(active))
    def _():
        ye_ref[...] = jnp.zeros_like(ye_ref)


def _experts(tile_expert, n_used, xs, w_gate, w_up, w_down):
    n_tiles = xs.shape[0] // ROW_TILE
    row_in = lambda i, te, nu: (jnp.minimum(i, nu[0] - 1), 0)
    row = lambda i, te, nu: (i, 0)
    wsel = lambda i, te, nu: (te[i], 0, 0)
    return pl.pallas_call(
        _expert_kernel,
        out_shape=jax.ShapeDtypeStruct((xs.shape[0], D_MODEL), BF16),
        grid_spec=pltpu.PrefetchScalarGridSpec(
            num_scalar_prefetch=2, grid=(n_tiles,),
            in_specs=[pl.BlockSpec((ROW_TILE, XS_WORDS), row_in),
                      pl.BlockSpec((1, D_MODEL, EXPERT_FF), wsel),
                      pl.BlockSpec((1, D_MODEL, EXPERT_FF), wsel),
                      pl.BlockSpec((1, EXPERT_FF, D_MODEL), wsel)],
            out_specs=pl.BlockSpec((ROW_TILE, D_MODEL), row),
            scratch_shapes=[pltpu.VMEM((D_MODEL, 2 * EXPERT_FF), BF16),
                            pltpu.VMEM((EXPERT_FF, D_MODEL), BF16)]),
        compiler_params=_cparams(("arbitrary",)), name="moe_experts",
    )(tile_expert, n_used, xs, w_gate, w_up, w_down)


def _combine_kernel(dst_ref, nslab_ref, x1_ref, slot_ref, ye_ref, ln_g_ref, ln_b_ref, y_ref,
                    buf, sem, *, tile_base, n_tiles):
    i = pl.program_id(0)
    g = tile_base + i
    cur = lax.rem(i, 2)

    def slab_copy(tile, buf_slot, j):
        d = pl.multiple_of(dst_ref[tile * PERM_SLABS + j], SLAB)
        dst = buf.at[buf_slot, pl.ds(pl.multiple_of(j * SLAB, SLAB), SLAB), :]
        return pltpu.make_async_copy(ye_ref.at[pl.ds(d, SLAB), :], dst, sem.at[buf_slot])

    @pl.when(i == 0)
    def _():
        buf[...] = jnp.zeros_like(buf)
        _slab_loop(nslab_ref[g], lambda j, u: slab_copy(g, cur, j).start(priority=u % 2))

    @pl.when(i + 1 < n_tiles)
    def _():
        _slab_loop(nslab_ref[g + 1], lambda j, u: slab_copy(g + 1, 1 - cur, j).start(priority=u % 2))

    _slab_loop(nslab_ref[g], lambda j, u: slab_copy(g, cur, j).wait())

    col = lax.broadcasted_iota(I32, (TOK_TILE, PERM_ROWS), 1)
    sl = slot_ref[0]
    diag = (lax.broadcasted_iota(I32, (TOK_TILE, TOK_TILE), 0)
            == lax.broadcasted_iota(I32, (TOK_TILE, TOK_TILE), 1))
    as_col = lambda row: jnp.sum(jnp.where(diag, row, 0), axis=1, keepdims=True)
    pick = jnp.where((col == as_col(sl[0:1, :])) | (col == as_col(sl[1:2, :])), 1.0, 0.0).astype(BF16)
    moe = jnp.dot(pick, buf[cur], preferred_element_type=F32)
    y_ref[...] = _layer_norm(DEEPNORM_ALPHA * x1_ref[...] + moe, ln_g_ref[...], ln_b_ref[...])


def _combine(plan, tile_base, x1, ye, ln_g, ln_b):
    n = x1.shape[0]
    n_tiles = n // TOK_TILE
    tok = lambda w: pl.BlockSpec((TOK_TILE, w), lambda i, d, ns: (i, 0))
    const = lambda a: pl.BlockSpec(a.shape, lambda i, d, ns: (0,) * a.ndim)
    return pl.pallas_call(
        functools.partial(_combine_kernel, tile_base=tile_base, n_tiles=n_tiles),
        out_shape=jax.ShapeDtypeStruct((n, D_MODEL), F32),
        grid_spec=pltpu.PrefetchScalarGridSpec(
            num_scalar_prefetch=2, grid=(n_tiles,),
            in_specs=[tok(D_MODEL),
                      pl.BlockSpec((1, TOP_K, TOK_TILE), lambda i, d, ns: (tile_base + i, 0, 0)),
                      pl.BlockSpec(memory_space=pl.ANY), const(ln_g), const(ln_b)],
            out_specs=tok(D_MODEL),
            scratch_shapes=[pltpu.VMEM((2, PERM_ROWS, D_MODEL), BF16), pltpu.SemaphoreType.DMA((2,))]),
        compiler_params=_cparams(("arbitrary",)), name="moe_combine_%d" % tile_base,
    )(plan['slab_dst'], plan['nslab'], x1, plan['slot_rows'], ye, ln_g, ln_b)


def _routing_plan(ids, gates):
    nt = ids.shape[1] // TOK_TILE
    pairs = TOP_K * TOK_TILE
    ex = jnp.arange(N_EXPERTS, dtype=I32)
    per_tile = lambda a: jnp.swapaxes(a.reshape(TOP_K, nt, TOK_TILE), 0, 1)
    flat = per_tile(ids).reshape(nt, pairs)
    onehot = (flat[:, None, :] == ex[None, :, None])
    p = np.arange(pairs)
    triu = jnp.asarray(p[:, None] <= p[None, :], BF16)
    csum = jnp.dot(onehot.astype(BF16).reshape(nt * N_EXPERTS, pairs), triu,
                   preferred_element_type=F32).astype(I32).reshape(nt, N_EXPERTS, pairs)
    oh = onehot.astype(I32)
    rank = jnp.sum(oh * (csum - 1), axis=1)
    cnt = csum[:, :, -1]
    cpad = (cnt + SLAB - 1) // SLAB * SLAB
    seg_end = jnp.cumsum(cpad, axis=1)
    seg_off = seg_end - cpad
    slot = jnp.sum(oh * seg_off[:, :, None], axis=1) + rank
    run_end = jnp.cumsum(cpad, axis=0)
    ntiles_e = (run_end[-1] + ROW_TILE - 1) // ROW_TILE
    tile_end = jnp.cumsum(ntiles_e)
    dst_run = ((tile_end - ntiles_e) * ROW_TILE)[None, :] + run_end - cpad
    j8 = jnp.arange(PERM_SLABS, dtype=I32) * SLAB
    e_of = jnp.minimum(jnp.sum((j8[None, :, None] >= seg_end[:, None, :]).astype(I32), axis=2),
                       N_EXPERTS - 1)
    sel = (e_of[:, :, None] == ex).astype(I32)
    slab_dst = jnp.sum(sel * (dst_run - seg_off)[:, None, :], axis=2) + j8[None, :]
    n_used = tile_end[-1]
    max_tiles = _max_row_tiles(ids.shape[1])
    t = jnp.arange(max_tiles, dtype=I32)
    te = jnp.sum((t[:, None] >= tile_end[None, :]).astype(I32), axis=1)
    te_last = jnp.sum((n_used - 1 >= tile_end).astype(I32))
    row_start = (tile_end - ntiles_e) * ROW_TILE
    tail_cnt = (ntiles_e * ROW_TILE - run_end[-1]) // SLAB
    tail_end = jnp.cumsum(tail_cnt)
    k = jnp.arange(ZERO_TABLE, dtype=I32)
    e_k = jnp.minimum(jnp.sum((k[:, None] >= tail_end[None, :]).astype(I32), axis=1), N_EXPERTS - 1)
    base_k = jnp.sum((e_k[:, None] == ex).astype(I32)
                     * (row_start + run_end[-1] - SLAB * (tail_end - tail_cnt))[None, :], axis=1)
    return dict(
        slab_dst=slab_dst.reshape(-1).astype(I32), nslab=(seg_end[:, -1] // SLAB).astype(I32),
        ztab=(base_k + SLAB * k).astype(I32), zinfo=jnp.stack([tail_end[-1], n_used]).astype(I32),
        slot_rows=slot.reshape(nt, TOP_K, TOK_TILE).astype(I32),
        gate_rows=per_tile(gates).astype(F32),
        tile_expert=jnp.where(t < n_used, jnp.minimum(te, N_EXPERTS - 1), te_last).astype(I32),
        n_used=n_used.reshape(1).astype(I32))


def _max_row_tiles(n_tokens):
    rows = TOP_K * n_tokens + (n_tokens // TOK_TILE) * N_EXPERTS * (SLAB - 1)
    return (rows + ROW_TILE - 1) // ROW_TILE + N_EXPERTS


def _moe(x1s, routes, wts):
    ids = jnp.concatenate([r[TOP_K:2 * TOP_K, :] for r in routes], axis=1).astype(I32)
    gates = jnp.concatenate([r[0:TOP_K, :] for r in routes], axis=1)
    plan = _routing_plan(ids, gates)
    max_tiles = _max_row_tiles(ids.shape[1])
    bases = [0]
    for x1 in x1s[:-1]:
        bases.append(bases[-1] + x1.shape[0] // TOK_TILE)
    xs = _dispatch(plan, x1s, max_tiles)
    ye = _experts(plan['tile_expert'], plan['n_used'], xs, wts['w_gate'], wts['w_up'], wts['w_down'])
    return [_combine(plan, base, x1, ye, wts['ln2_g'], wts['ln2_b']) for base, x1 in zip(bases, x1s)]


def _prep_weights(w_in, w_out, conv_w, a_log, dt_bias, gdn_norm_w, ln1_g, ln1_b, w_router_group,
                  w_router_expert, w_gate, w_up, w_down, ln2_g, ln2_b):
    pad_row = lambda v: jnp.pad(v.astype(F32), (0, LANES - v.shape[0]))[None, :]
    wr = jnp.pad(jnp.concatenate([w_router_group, w_router_expert], axis=1),
                 ((0, 0), (0, LANES - N_GROUPS - N_EXPERTS)))
    wr_hi = wr.astype(BF16)
    wr_mid = (wr - wr_hi.astype(F32)).astype(BF16)
    group = ATTN_HEADS // ATTN_KV_HEADS
    wq = w_in[:, :Q_COLS].reshape(D_MODEL, ATTN_KV_HEADS, group, HEAD_DIM)
    wq = jnp.swapaxes(wq, 1, 2).reshape(D_MODEL, Q_COLS)
    w_all = jnp.concatenate([wq, w_in[:, Q_COLS:]], axis=1)
    w_all = jnp.pad(w_all, ((0, 0), (0, IN_COLS_PAD - w_all.shape[1]))).astype(BF16)
    wo_q = w_out[:Q_COLS].reshape(ATTN_KV_HEADS, group, HEAD_DIM, D_MODEL)
    wo_q = jnp.swapaxes(wo_q, 0, 1).reshape(Q_COLS, D_MODEL)
    wo = jnp.concatenate([wo_q, w_out[Q_COLS:]], axis=0)
    return dict(
        w_all=w_all, convw=conv_w.astype(F32), alog=pad_row(a_log), dtb=pad_row(dt_bias),
        norm_w=gdn_norm_w.astype(F32)[None, :], wo=wo.astype(BF16),
        ln1_g=ln1_g[None, :], ln1_b=ln1_b[None, :], wr=jnp.stack([wr_hi, wr_mid]),
        w_gate=w_gate, w_up=w_up, w_down=w_down, ln2_g=ln2_g[None, :], ln2_b=ln2_b[None, :])


def _layer(x_prompt, x_sample, cache_k, cache_v, state_gdn, state_conv, wts):
    bp, sp, _ = x_prompt.shape
    bs, ts, _ = x_sample.shape
    n_p = bp * sp

    xp = x_prompt.reshape(n_p, D_MODEL)
    (q, k, v, qg, kg, vg, z, gcb, utail) = _proj(xp, np.arange(sp), wts, GDN_CHUNK, bp)
    attn_p = _attn_prompt(q, k, v, wts['sinks'], bp)
    gdn_p, s_p = _gdn_prompt(qg, kg, vg, z, gcb, wts['norm_w'], bp)
    last_win = lambda a: a.reshape(bp, sp, KV_COLS)[:, sp - WINDOW:].reshape(bp, WINDOW, ATTN_KV_HEADS,
                                                                            HEAD_DIM)
    new_k_p, new_v_p = last_win(k), last_win(v)
    tiles_per_seq = sp // min(PROJ_TILE, sp)
    conv_p = utail.reshape(bp, tiles_per_seq, SUBLANES, CONV_DIM)[:, -1, SUBLANES - (CONV_W - 1):]

    lo, hi = SAMPLE_FIRST, SAMPLE_FIRST + ts
    xs_rows = jnp.pad(x_sample, ((0, 0), (lo, SAMPLE_SLOTS - hi), (0, 0))).reshape(bs * SAMPLE_SLOTS, D_MODEL)
    hist = jnp.pad(state_conv, ((0, 0), (0, SAMPLE_SLOTS - lo), (0, 0))).reshape(bs * SAMPLE_SLOTS, CONV_DIM)
    slot = np.arange(SAMPLE_SLOTS)
    valid = jnp.asarray(np.tile((slot >= lo) & (slot < hi), bs)[:, None], F32)
    pos_s = np.tile(PAST_LEN + slot - lo, bs)
    (q, k, v, qg, kg, vg, z, gcb, u_s) = _proj(xs_rows, pos_s, wts, SAMPLE_SLOTS, 1, hist, valid)
    ck = cache_k.reshape(bs, WINDOW, KV_COLS)
    cv = cache_v.reshape(bs, WINDOW, KV_COLS)
    attn_s, kwin, vwin = _attn_sample(q, k, v, ck, cv, wts['sinks'], bs, ts)
    gdn_s, s_s = _gdn_sample(qg, kg, vg, z, gcb, wts['norm_w'], state_gdn)
    real = lambda a: a.reshape(bs, SAMPLE_SLOTS, -1)[:, lo:hi]
    new_k_s = kwin.reshape(bs, WINDOW, ATTN_KV_HEADS, HEAD_DIM)
    new_v_s = vwin.reshape(bs, WINDOW, ATTN_KV_HEADS, HEAD_DIM)
    conv_s = u_s.reshape(bs, SAMPLE_SLOTS, CONV_DIM)[:, hi - (CONV_W - 1):hi]

    x1_p, route_p = _post(attn_p, gdn_p, xp, wts)
    x1_s, route_s = _post(real(attn_s).reshape(bs * ts, Q_COLS), real(gdn_s).reshape(bs * ts, Z_COLS),
                          x_sample.reshape(bs * ts, D_MODEL), wts)
    y_p, y_s = _moe([x1_p, x1_s], [route_p, route_s], wts)
    return (y_p.reshape(bp, sp, D_MODEL), y_s.reshape(bs, ts, D_MODEL), new_k_p, new_v_p, s_p, conv_p,
            new_k_s, new_v_s, s_s, conv_s)


def kernel(x_prompt, x_sample, cache_attn_k, cache_attn_v, state_gdn, state_conv, w_in, w_out,
           attn_sinks, conv_w, a_log, dt_bias, gdn_norm_w, ln1_g, ln1_b, w_router_group,
           w_router_expert, w_gate, w_up, w_down, ln2_g, ln2_b):
    assert w_in.shape[0] == DEPTH
    l = 0
    wts = _prep_weights(w_in[l], w_out[l], conv_w[l], a_log[l], dt_bias[l], gdn_norm_w[l], ln1_g[l],
                        ln1_b[l], w_router_group[l], w_router_expert[l], w_gate[l], w_up[l],
                        w_down[l], ln2_g[l], ln2_b[l])
    wts['sinks'] = attn_sinks[l]
    outs = _layer(x_prompt, x_sample, cache_attn_k[l], cache_attn_v[l], state_gdn[l], state_conv[l], wts)
    (y_p, y_s, k_p, v_p, s_p, c_p, k_s, v_s, s_s, c_s) = outs
    add = lambda a: a[None]
    return (y_p, y_s, add(k_p), add(v_p), add(s_p), add(c_p), add(k_s), add(v_s), add(s_s), add(c_s))
```

```python
import functools
import math

import jax
import jax.numpy as jnp
import numpy as np
from jax import lax
from jax.experimental import pallas as pl
from jax.experimental.pallas import tpu as pltpu

F32 = jnp.float32
BF16 = jnp.bfloat16
I32 = jnp.int32

D_MODEL = 1024
ATTN_HEADS = 8
ATTN_KV_HEADS = 2
HEAD_DIM = 64
WINDOW = 128
ROT_DIM = HEAD_DIM // 4
ROPE_THETA = 500000.0
GDN_HEADS = 4
GDN_DK = 128
GDN_DV = 128
CONV_W = 4
QK_COLS = GDN_HEADS * GDN_DK
CONV_DIM = 2 * QK_COLS + GDN_HEADS * GDN_DV
Z_COLS = GDN_HEADS * GDN_DV
Q_COLS = ATTN_HEADS * HEAD_DIM
KV_COLS = ATTN_KV_HEADS * HEAD_DIM
N_GROUPS = 4
EXPERTS_PER_GROUP = 8
N_EXPERTS = N_GROUPS * EXPERTS_PER_GROUP
TOP_K = 2
EXPERT_FF = 256
NORM_EPS = 1e-5
L2_EPS = 1e-6
DEPTH = 1
DEEPNORM_ALPHA = (2 * DEPTH) ** 0.25
PAST_LEN = 8192

LANES = 128
SUBLANES = 8
IN_SPLITS = (0, Q_COLS + 2 * KV_COLS, Q_COLS + 2 * KV_COLS + CONV_DIM,
             Q_COLS + 2 * KV_COLS + CONV_DIM + Z_COLS, Q_COLS + 2 * KV_COLS + CONV_DIM + Z_COLS + LANES)
IN_COLS_PAD = IN_SPLITS[-1]
TOK_TILE = 512
PROJ_TILE = 512
PROJ_SUB = 128
POST_SUB = 256
GDN_CHUNK = 128
GDN_SEQ_PER_STEP = 4
ATTN_BLOCKS_PER_STEP = 4
ATTN_SEQS_PER_STEP = 8
INV_BASE = 16
SAMPLE_SLOTS = 8
SAMPLE_FIRST = CONV_W - 1
ROW_TILE = 512
EXPERT_SUB = 256
SLAB_UNROLL = 4
SLAB = 16
PERM_ROWS = TOP_K * TOK_TILE + N_EXPERTS * SLAB
PERM_SLABS = PERM_ROWS // SLAB
XS_WORDS = D_MODEL + LANES
ZERO_TABLE = N_EXPERTS * (ROW_TILE // SLAB)
VMEM_LIMIT = 48 * 1024 * 1024
NEG_BIG = -1e30


def _cparams(sem):
    return pltpu.CompilerParams(dimension_semantics=sem, vmem_limit_bytes=VMEM_LIMIT)


def _bdot(a, b):
    return jnp.dot(a.astype(BF16), b.astype(BF16), preferred_element_type=F32)


def _bdot_nt(a, b):
    return lax.dot_general(a.astype(BF16), b.astype(BF16), (((1,), (1,)), ((), ())),
                           preferred_element_type=F32)


def _bdot_tn(a, b):
    return lax.dot_general(a.astype(BF16), b.astype(BF16), (((0,), (0,)), ((), ())),
                           preferred_element_type=F32)


def _div_pow2(x, n):
    return jnp.right_shift(x, int(math.log2(n)))


def _mod_pow2(x, n):
    return jnp.bitwise_and(x, n - 1)


def _split3(x):
    hi = x.astype(BF16)
    r = x - hi.astype(F32)
    mid = r.astype(BF16)
    lo = (r - mid.astype(F32)).astype(BF16)
    return hi, mid, lo


def _dot_exact_lhs01(m01, x):
    hi, mid, lo = _split3(x)
    d = lambda t: jnp.dot(m01, t, preferred_element_type=F32)
    return d(hi) + d(mid) + d(lo)


def _sigmoid(x):
    return 1.0 / (1.0 + jnp.exp(-x))


def _silu(x):
    return x * _sigmoid(x)


def _softplus(x):
    return jnp.maximum(x, 0.0) + jnp.log1p(jnp.exp(-jnp.abs(x)))


def _layer_norm(h, g, b):
    mu = jnp.mean(h, axis=-1, keepdims=True)
    d = h - mu
    var = jnp.mean(d * d, axis=-1, keepdims=True)
    return d * lax.rsqrt(var + NORM_EPS) * g + b


def _proj_kernel(*refs, tm, has_hist, full_u, one_segment):
    it = iter(refs)
    x_ref, cos_ref, sin_ref = next(it), next(it), next(it)
    w_ref = next(it)
    convw_ref, alog_ref, dtb_ref, tri_ref, seg_ref = next(it), next(it), next(it), next(it), next(it)
    hist_ref = valid_ref = None
    if has_hist:
        hist_ref, valid_ref = next(it), next(it)
    q_ref, k_ref, v_ref = next(it), next(it), next(it)
    qg_ref, kg_ref, vg_ref, z_ref, gcb_ref, u_ref = (next(it) for _ in range(6))
    ubuf = next(it)

    t = pl.program_id(1)
    sub = PROJ_SUB
    rows = [slice(j * sub, (j + 1) * sub) for j in range(tm // sub)]
    lane = lax.broadcasted_iota(I32, (sub, LANES), 1)
    first_half = _mod_pow2(lane, HEAD_DIM) < (ROT_DIM // 2)

    @pl.when(t == 0)
    def _():
        ubuf[0:SUBLANES, :] = jnp.zeros((SUBLANES, CONV_DIM), F32)

    @pl.when(t > 0)
    def _():
        ubuf[0:SUBLANES, :] = ubuf[tm:tm + SUBLANES, :]

    dots = []
    for r in rows:
        xb = x_ref[r, :].astype(BF16)
        dots.append([jnp.dot(xb, w_ref[:, lo:hi], preferred_element_type=F32)
                     for lo, hi in zip(IN_SPLITS[:-1], IN_SPLITS[1:])])

    def l2n(s):
        return s * lax.rsqrt(jnp.sum(s * s, axis=1, keepdims=True) + L2_EPS)

    for r, (pq, u, z, ab) in zip(rows, dots):
        cosv, sinv = cos_ref[r, :], sin_ref[r, :]

        def rope(s):
            sw = jnp.where(first_half, pltpu.roll(s, LANES - ROT_DIM // 2, axis=1),
                           pltpu.roll(s, ROT_DIM // 2, axis=1))
            return s * cosv + sw * sinv

        for j in range(Q_COLS // LANES):
            q_ref[r, j * LANES:(j + 1) * LANES] = rope(pq[:, j * LANES:(j + 1) * LANES])
        k_ref[r, :] = rope(pq[:, Q_COLS:Q_COLS + KV_COLS])
        v_ref[r, :] = pq[:, Q_COLS + KV_COLS:Q_COLS + 2 * KV_COLS]
        z_ref[r, :] = z

        if has_hist:
            u = u + hist_ref[r, :]
        if full_u:
            u_ref[r, :] = u
        elif r.stop == tm:
            u_ref[...] = u[sub - SUBLANES:, :]
        base = SUBLANES + r.start
        ubuf[base:base + sub, :] = u
        acc = u * convw_ref[CONV_W - 1:CONV_W, :]
        for j in range(1, CONV_W):
            acc = acc + ubuf[base - j:base - j + sub, :] * convw_ref[CONV_W - 1 - j:CONV_W - j, :]
        c = _silu(acc)
        if has_hist:
            c = c * valid_ref[r, :]
        for h in range(GDN_HEADS):
            sl = slice(h * GDN_DK, (h + 1) * GDN_DK)
            qg_ref[r, sl] = l2n(c[:, sl]) * (GDN_DK ** -0.5)
            kg_ref[r, sl] = l2n(c[:, QK_COLS + h * GDN_DK:QK_COLS + (h + 1) * GDN_DK])
        vg_ref[r, :] = c[:, 2 * QK_COLS:]

        g = -jnp.exp(alog_ref[...]) * _softplus(ab + dtb_ref[...])
        beta = _sigmoid(ab)
        if has_hist:
            g = g * valid_ref[r, :]
            beta = beta * valid_ref[r, :]
        g = jnp.where(lane < GDN_HEADS, g, 0.0)
        gc = _dot_exact_lhs01(tri_ref[...], g)
        if one_segment:
            gl = jnp.broadcast_to(gc[sub - 1:sub, :], (sub, LANES))
        else:
            gl = _dot_exact_lhs01(seg_ref[...], g)
        gcb_ref[r, :] = jnp.where(lane < GDN_HEADS, gc,
                                  jnp.where(lane < 2 * GDN_HEADS, beta,
                                            jnp.where(lane < 3 * GDN_HEADS,
                                                      pltpu.roll(gl, 2 * GDN_HEADS, axis=1), 0.0)))


def _rope_tables(pos):
    half = ROT_DIM // 2
    pos = np.asarray(pos, np.float64)
    inv_freq = ROPE_THETA ** (-np.arange(half, dtype=np.float64) * 2.0 / ROT_DIM)
    ang = pos[:, None] * inv_freq[None, :]
    cos, sin = np.cos(ang), np.sin(ang)
    p = pos.shape[0]
    cpat = np.concatenate([cos, cos, np.ones((p, HEAD_DIM - ROT_DIM))], axis=1)
    spat = np.concatenate([-sin, sin, np.zeros((p, HEAD_DIM - ROT_DIM))], axis=1)
    rep = (1, LANES // HEAD_DIM)
    return jnp.asarray(np.tile(cpat, rep), F32), jnp.asarray(np.tile(spat, rep), F32)


def _segment_matrices(tm, seg_len):
    i = np.arange(tm)
    same = (i[:, None] // seg_len) == (i[None, :] // seg_len)
    tri = same & (i[None, :] <= i[:, None])
    return jnp.asarray(tri, BF16), jnp.asarray(same, BF16)


def _proj(x, pos, wts, seg_len, n_seq, hist=None, valid=None):
    n = x.shape[0]
    rows = n // n_seq
    tm = min(PROJ_TILE, rows)
    nt = rows // tm
    has_hist = hist is not None
    cos_t, sin_t = _rope_tables(pos)
    tri, seg = _segment_matrices(PROJ_SUB, seg_len)

    tok = lambda w: pl.BlockSpec((tm, w), lambda b, t: (b * nt + t, 0))
    const = lambda a: pl.BlockSpec(a.shape, lambda b, t: (0,) * a.ndim)
    in_arrays = [x, cos_t, sin_t, wts['w_all'], wts['convw'], wts['alog'], wts['dtb'], tri, seg]
    in_specs = [tok(D_MODEL), pl.BlockSpec((tm, LANES), lambda b, t: (t, 0)),
                pl.BlockSpec((tm, LANES), lambda b, t: (t, 0))] + [const(a) for a in in_arrays[3:]]
    if has_hist:
        in_arrays += [hist, valid]
        in_specs += [tok(CONV_DIM), tok(1)]
    u_rows = n if has_hist else (n // tm) * SUBLANES
    u_block = tm if has_hist else SUBLANES
    out_shape = [jax.ShapeDtypeStruct((n, Q_COLS), F32), jax.ShapeDtypeStruct((n, KV_COLS), F32),
                 jax.ShapeDtypeStruct((n, KV_COLS), F32), jax.ShapeDtypeStruct((n, QK_COLS), F32),
                 jax.ShapeDtypeStruct((n, QK_COLS), F32), jax.ShapeDtypeStruct((n, Z_COLS), F32),
                 jax.ShapeDtypeStruct((n, Z_COLS), F32), jax.ShapeDtypeStruct((n, LANES), F32),
                 jax.ShapeDtypeStruct((u_rows, CONV_DIM), F32)]
    out_specs = [tok(Q_COLS), tok(KV_COLS), tok(KV_COLS), tok(QK_COLS), tok(QK_COLS), tok(Z_COLS),
                 tok(Z_COLS), tok(LANES),
                 pl.BlockSpec((u_block, CONV_DIM), lambda b, t: (b * nt + t, 0))]
    return pl.pallas_call(
        functools.partial(_proj_kernel, tm=tm, has_hist=has_hist, full_u=has_hist,
                          one_segment=seg_len == PROJ_SUB),
        out_shape=out_shape, grid=(n_seq, nt), in_specs=in_specs, out_specs=out_specs,
        scratch_shapes=[pltpu.VMEM((tm + SUBLANES, CONV_DIM), F32)],
        compiler_params=_cparams(("arbitrary", "arbitrary")),
        name="proj_hist" if has_hist else "proj",
    )(*in_arrays)


def _attn_blocks(qs, kcats, vcats, biases, sink, tq):
    lane = lax.broadcasted_iota(I32, (tq, LANES), 1)
    low = lane < HEAD_DIM
    n_slab = Q_COLS // LANES

    def stack(q):
        slabs = [q[:, j * LANES:(j + 1) * LANES] * (HEAD_DIM ** -0.5) for j in range(n_slab)]
        parts = ([jnp.where(low, s, 0.0) for s in slabs] + [jnp.where(low, 0.0, s) for s in slabs])
        return jnp.concatenate(parts, axis=0).astype(BF16)

    def unstack(o8):
        return [jnp.where(low, o8[j * tq:(j + 1) * tq, :], o8[(n_slab + j) * tq:(n_slab + j + 1) * tq, :])
                for j in range(n_slab)]

    rows = ATTN_HEADS * tq
    half = rows // 2
    klow = lax.broadcasted_iota(I32, (2 * WINDOW, LANES), 1) < HEAD_DIM
    one = jnp.ones((), BF16)
    q8s = _each(stack, qs)
    ss = _each(lambda q8, kc, b: _bdot_nt(q8, kc) + b, q8s, kcats, biases)
    ms = _each(lambda s: jnp.maximum(jnp.broadcast_to(jnp.max(s, axis=1, keepdims=True), (rows, LANES)),
                                     sink), ss)
    ps = _each(lambda s, m: jnp.exp(s - jnp.concatenate([m, m], axis=1)).astype(BF16), ss, ms)
    pv0 = _each(lambda p, vc: jnp.dot(p[:half], jnp.where(klow, vc, one), preferred_element_type=F32),
                ps, vcats)
    pv1 = _each(lambda p, vc: jnp.dot(p[half:], jnp.where(klow, one, vc), preferred_element_type=F32),
                ps, vcats)
    pvs = _each(lambda a, b: jnp.concatenate([a, b], axis=0), pv0, pv1)
    o8s = _each(lambda pv, m: pv / (pltpu.roll(pv, HEAD_DIM, axis=1) + jnp.exp(sink - m)), pvs, ms)
    return _each(unstack, o8s)


def _attn_prompt_kernel(q_ref, kc_ref, vc_ref, kp_ref, vp_ref, bias0_ref, bias_ref, sink_ref, o_ref, *,
                        nblk):
    kall = jnp.concatenate([kp_ref[...], kc_ref[...]], axis=0).astype(BF16)
    vall = jnp.concatenate([vp_ref[...], vc_ref[...]], axis=0).astype(BF16)
    win = lambda a, j: a[j * WINDOW:(j + 2) * WINDOW, :]
    qs = [q_ref[j * WINDOW:(j + 1) * WINDOW, :] for j in range(nblk)]
    biases = [bias0_ref[0]] + [bias_ref[...]] * (nblk - 1)
    outs = _attn_blocks(qs, [win(kall, j) for j in range(nblk)], [win(vall, j) for j in range(nblk)],
                        biases, sink_ref[...], WINDOW)
    for j, slabs in enumerate(outs):
        for c, slab in enumerate(slabs):
            o_ref[j * WINDOW:(j + 1) * WINDOW, c * LANES:(c + 1) * LANES] = slab


def _attn_sample_kernel(q_ref, kc_ref, vc_ref, kp_ref, vp_ref, bias_ref, sink_ref, o_ref, kw_ref, vw_ref,
                        *, nseq, n_new):
    tq = SAMPLE_SLOTS
    zpad = jnp.zeros((WINDOW - tq, LANES), F32)
    rows = lambda ref, j: ref[j * tq:(j + 1) * tq, :]
    cat = lambda pref, cref, j: jnp.concatenate([pref[j], rows(cref, j), zpad], axis=0).astype(BF16)
    outs = _attn_blocks([rows(q_ref, j) for j in range(nseq)],
                        [cat(kp_ref, kc_ref, j) for j in range(nseq)],
                        [cat(vp_ref, vc_ref, j) for j in range(nseq)],
                        [bias_ref[...]] * nseq, sink_ref[...], tq)
    for j, slabs in enumerate(outs):
        for c, slab in enumerate(slabs):
            o_ref[j * tq:(j + 1) * tq, c * LANES:(c + 1) * LANES] = slab
    row = lax.broadcasted_iota(I32, (WINDOW, LANES), 0)
    keep = WINDOW - n_new
    for pref, cref, wref in ((kp_ref, kc_ref, kw_ref), (vp_ref, vc_ref, vw_ref)):
        for j in range(nseq):
            new = jnp.concatenate([rows(cref, j), zpad], axis=0)
            wref[j] = jnp.where(row < keep, pltpu.roll(pref[j], keep, axis=0),
                                pltpu.roll(new, keep - SAMPLE_FIRST, axis=0))


def _sink_rows(sinks, tq):
    return jnp.broadcast_to(jnp.repeat(sinks.astype(F32), tq)[:, None], (ATTN_HEADS * tq, LANES))


def _attn_bias(tq, q_off, k_lo, k_hi, has_prev):
    qi = (np.arange(ATTN_HEADS * tq) % tq)[:, None]
    c = np.arange(2 * WINDOW)[None, :]
    cj = c - WINDOW
    vis_prev = (c < WINDOW) & (c > qi - q_off) & has_prev
    vis_cur = (c >= WINDOW) & (cj <= qi) & (cj >= k_lo) & (cj <= k_hi)
    return np.where(vis_prev | vis_cur, 0.0, NEG_BIG).astype(np.float32)


def _attn_prompt(q, k, v, sinks, n_seq):
    n = q.shape[0]
    nb = n // n_seq // WINDOW
    nblk = min(ATTN_BLOCKS_PER_STEP, nb)
    steps = nb // nblk
    tq = nblk * WINDOW
    cur = lambda w: pl.BlockSpec((tq, w), lambda b, i: (b * steps + i, 0))
    prev = pl.BlockSpec((WINDOW, LANES), lambda b, i: (b * nb + jnp.maximum(i * nblk - 1, 0), 0))
    bias2 = jnp.asarray(np.stack([_attn_bias(WINDOW, 0, 0, WINDOW - 1, False),
                                  _attn_bias(WINDOW, 0, 0, WINDOW - 1, True)]))
    rows = ATTN_HEADS * WINDOW
    return pl.pallas_call(
        functools.partial(_attn_prompt_kernel, nblk=nblk),
        out_shape=jax.ShapeDtypeStruct((n, Q_COLS), F32), grid=(n_seq, steps),
        in_specs=[cur(Q_COLS), cur(LANES), cur(LANES), prev, prev,
                  pl.BlockSpec((1, rows, 2 * WINDOW), lambda b, i: (jnp.minimum(i, 1), 0, 0)),
                  pl.BlockSpec((rows, 2 * WINDOW), lambda b, i: (0, 0)),
                  pl.BlockSpec((rows, LANES), lambda b, i: (0, 0))],
        out_specs=cur(Q_COLS),
        compiler_params=_cparams(("arbitrary", "arbitrary")), name="attn_prompt",
    )(q, k, v, k, v, bias2, bias2[1], _sink_rows(sinks, WINDOW))


def _attn_sample(q, k, v, cache_k, cache_v, sinks, n_seq, n_new):
    tq = SAMPLE_SLOTS
    nseq = min(ATTN_SEQS_PER_STEP, n_seq)
    cur = lambda w: pl.BlockSpec((nseq * tq, w), lambda b: (b, 0))
    prev = pl.BlockSpec((nseq, WINDOW, LANES), lambda b: (b, 0, 0))
    bias = jnp.asarray(_attn_bias(tq, SAMPLE_FIRST, SAMPLE_FIRST, SAMPLE_FIRST + 3, True))
    win = jax.ShapeDtypeStruct((n_seq, WINDOW, LANES), F32)
    return pl.pallas_call(
        functools.partial(_attn_sample_kernel, nseq=nseq, n_new=n_new),
        out_shape=[jax.ShapeDtypeStruct((n_seq * tq, Q_COLS), F32), win, win], grid=(n_seq // nseq,),
        in_specs=[cur(Q_COLS), cur(LANES), cur(LANES), prev, prev,
                  pl.BlockSpec(bias.shape, lambda b: (0, 0)),
                  pl.BlockSpec((ATTN_HEADS * tq, LANES), lambda b: (0, 0))],
        out_specs=[cur(Q_COLS), prev, prev],
        compiler_params=_cparams(("arbitrary",)), name="attn_sample",
    )(q, k, v, cache_k, cache_v, bias, _sink_rows(sinks, tq))


def _each(f, *lists):
    return [f(*args) for args in zip(*lists)]


def _unit_lower_inverse(ms, eye, same_base, base_only=False):
    c = ms[0].shape[0]

    def neumann(q0s, n_factors):
        xs = _each(lambda q: eye + q, q0s)
        if n_factors == 1:
            return xs
        qs = _each(_bdot, q0s, q0s)
        for _ in range(n_factors - 2):
            prods = _each(lambda x, q: _bdot(jnp.concatenate([x, q], axis=0), q), xs, qs)
            xs = _each(lambda x, pr: x + pr[:c], xs, prods)
            qs = _each(lambda pr: pr[c:], prods)
        return _each(lambda x, q: x + _bdot(x, q), xs, qs)

    ds = _each(lambda m: jnp.where(same_base, m, 0.0), ms)
    xs = neumann(_each(lambda d: -d, ds), int(math.log2(INV_BASE)))
    nblk = c // INV_BASE
    if nblk == 1 or base_only:
        return xs
    ls = _each(lambda m, d: m - d, ms, ds)
    ns = _each(lambda x, l: -_bdot(x, l), xs, ls)
    ys = neumann(ns, int(math.log2(nblk)))
    return _each(_bdot, ys, xs)


def _gdn_intra(qs, ks, vs, gcs, gls, betas, same_seq, low_incl, low_strict, eye, same_base,
               base_only=False):
    del same_seq
    e_gcs = _each(jnp.exp, gcs)

    def decay_of(gc):
        gc_row = jnp.sum(jnp.where(eye > 0, gc, 0.0), axis=0, keepdims=True)
        return jnp.where(low_incl, jnp.exp(jnp.where(low_incl, gc - gc_row, 0.0)), 0.0)

    c = qs[0].shape[0]
    decays = _each(decay_of, gcs)
    kbs = _each(lambda k, b: k * b, ks, betas)
    vbs = _each(lambda v, b: v * b, vs, betas)
    kqs = _each(lambda kb, q, k: _bdot_nt(jnp.concatenate([kb, q], axis=0), k), kbs, qs, ks)
    ms = _each(lambda kq, d: jnp.where(low_strict, kq[:c] * d, 0.0), kqs, decays)
    attns = _each(lambda kq, d: jnp.where(low_incl, kq[c:] * d, 0.0), kqs, decays)
    tmats = _unit_lower_inverse(ms, eye, same_base, base_only)
    uws = _each(lambda t, vb, kb, e: _bdot(t, jnp.concatenate([vb, kb * e], axis=1)),
                tmats, vbs, kbs, e_gcs)
    us = _each(lambda uw: uw[:, :GDN_DV], uws)
    ws = _each(lambda uw: uw[:, GDN_DV:], uws)
    q_decs = _each(lambda q, e: q * e, qs, e_gcs)
    k_decs = _each(lambda k, gl, gc: k * jnp.exp(gl - gc), ks, gls, gcs)
    return us, ws, attns, q_decs, k_decs


def _chunk_masks(c, seq_len):
    i = lax.broadcasted_iota(I32, (c, c), 0)
    j = lax.broadcasted_iota(I32, (c, c), 1)
    same_seq = _div_pow2(i, seq_len) == _div_pow2(j, seq_len)
    low_incl = same_seq & (i >= j)
    low_strict = same_seq & (i > j)
    eye = (i == j).astype(F32)
    same_base = _div_pow2(i, INV_BASE) == _div_pow2(j, INV_BASE)
    return same_seq, low_incl, low_strict, eye, same_base


def _gated_rms(o, z, nw):
    o = o * lax.rsqrt(jnp.mean(o * o, axis=1, keepdims=True) + NORM_EPS) * nw
    return o * _silu(z)


def _gdn_prompt_kernel(qg_ref, kg_ref, vg_ref, z_ref, gcb_ref, nw_ref, o_ref, s_out_ref, s_scr):
    c = GDN_CHUNK
    n = pl.program_id(1)

    @pl.when(n == 0)
    def _():
        s_scr[...] = jnp.zeros_like(s_scr)

    masks = _chunk_masks(c, c)
    nw = nw_ref[...]
    chains = [(b, h) for b in range(qg_ref.shape[0]) for h in range(GDN_HEADS)]
    hs = lambda h: slice(h * GDN_DK, (h + 1) * GDN_DK)
    col = lambda off: [gcb_ref[b, :, off + h:off + h + 1] for b, h in chains]
    gcs, betas, gls = col(0), col(GDN_HEADS), col(2 * GDN_HEADS)
    qs = [qg_ref[b, :, hs(h)] for b, h in chains]
    ks = [kg_ref[b, :, hs(h)] for b, h in chains]
    vs = [vg_ref[b, :, hs(h)] for b, h in chains]
    us, ws, attns, q_decs, k_decs = _gdn_intra(qs, ks, vs, gcs, gls, betas, *masks)
    ss = [s_scr[b, h] for b, h in chains]
    wqs = _each(lambda w, qd, s: _bdot(jnp.concatenate([w, qd], axis=0), s), ws, q_decs, ss)
    wss = _each(lambda wq: wq[:c], wqs)
    qss = _each(lambda wq: wq[c:], wqs)
    v_news = _each(lambda u, x: u - x, us, wss)
    avs = _each(_bdot, attns, v_news)
    kvs = _each(_bdot_tn, k_decs, v_news)
    for (b, h), s, gl, qsv, av, kv in zip(chains, ss, gls, qss, avs, kvs):
        s_scr[b, h] = s * jnp.exp(gl[0:1, :]) + kv
        o_ref[b, :, hs(h)] = _gated_rms(qsv + av, z_ref[b, :, hs(h)], nw)

    @pl.when(n == pl.num_programs(1) - 1)
    def _():
        s_out_ref[...] = s_scr[...]


def _gdn_prompt(qg, kg, vg, z, gcb, norm_w, n_seq):
    n = qg.shape[0]
    s_len = n // n_seq
    nb = min(GDN_SEQ_PER_STEP, n_seq)
    v3 = lambda a: a.reshape(n_seq, s_len, a.shape[-1])
    tok = lambda w: pl.BlockSpec((nb, GDN_CHUNK, w), lambda b, i: (b, i, 0))
    o, s = pl.pallas_call(
        _gdn_prompt_kernel,
        out_shape=[jax.ShapeDtypeStruct((n_seq, s_len, Z_COLS), F32),
                   jax.ShapeDtypeStruct((n_seq, GDN_HEADS, GDN_DK, GDN_DV), F32)],
        grid=(n_seq // nb, s_len // GDN_CHUNK),
        in_specs=[tok(QK_COLS), tok(QK_COLS), tok(Z_COLS), tok(Z_COLS), tok(LANES),
                  pl.BlockSpec((1, GDN_DV), lambda b, i: (0, 0))],
        out_specs=[tok(Z_COLS),
                   pl.BlockSpec((nb, GDN_HEADS, GDN_DK, GDN_DV), lambda b, i: (b, 0, 0, 0))],
        scratch_shapes=[pltpu.VMEM((nb, GDN_HEADS, GDN_DK, GDN_DV), F32)],
        compiler_params=_cparams(("arbitrary", "arbitrary")), name="gdn_prompt",
    )(v3(qg), v3(kg), v3(vg), v3(z), v3(gcb), norm_w)
    return o.reshape(n, Z_COLS), s


def _gdn_sample_kernel(qg_ref, kg_ref, vg_ref, z_ref, gcb_ref, nw_ref, s_in_ref, o_ref, s_out_ref):
    c = GDN_CHUNK
    n_sub = c // SAMPLE_SLOTS
    masks = _chunk_masks(c, SAMPLE_SLOTS)
    heads = range(GDN_HEADS)
    hs = lambda h: slice(h * GDN_DK, (h + 1) * GDN_DK)
    rs = lambda s: slice(s * SAMPLE_SLOTS, (s + 1) * SAMPLE_SLOTS)
    col = lambda off: [gcb_ref[:, off + h:off + h + 1] for h in heads]
    gcs, betas, gls = col(0), col(GDN_HEADS), col(2 * GDN_HEADS)
    us, ws, attns, q_decs, k_decs = _gdn_intra([qg_ref[:, hs(h)] for h in heads],
                                               [kg_ref[:, hs(h)] for h in heads],
                                               [vg_ref[:, hs(h)] for h in heads], gcs, gls, betas, *masks,
                                               base_only=SAMPLE_SLOTS <= INV_BASE)
    pairs = [(h, s) for h in heads for s in range(n_sub)]
    sts = [s_in_ref[s, h] for h, s in pairs]
    boths = [jnp.concatenate([ws[h][rs(s), :], q_decs[h][rs(s), :]], axis=0) for h, s in pairs]
    rr = _each(_bdot, boths, sts)
    gather = lambda h, part: jnp.concatenate(
        [rr[h * n_sub + s][part * SAMPLE_SLOTS:(part + 1) * SAMPLE_SLOTS, :] for s in range(n_sub)], axis=0)
    v_news = [us[h] - gather(h, 0) for h in heads]
    avs = _each(_bdot, attns, v_news)
    row = lax.broadcasted_iota(I32, (c, LANES), 0)
    seq_of_row = _div_pow2(row, SAMPLE_SLOTS)
    kds = [jnp.where(seq_of_row == s, k_decs[h], 0.0) for h, s in pairs]
    kvs = _each(_bdot_tn, kds, [v_news[h] for h, _ in pairs])
    egls = _each(jnp.exp, gls)
    for (h, s), st, kv in zip(pairs, sts, kvs):
        s_out_ref[s, h] = st * egls[h][s * SAMPLE_SLOTS:s * SAMPLE_SLOTS + 1, :] + kv
    nw = nw_ref[...]
    for h in heads:
        o_ref[:, hs(h)] = _gated_rms(gather(h, 1) + avs[h], z_ref[:, hs(h)], nw)


def _gdn_sample(qg, kg, vg, z, gcb, norm_w, state):
    n = qg.shape[0]
    n_sub = GDN_CHUNK // SAMPLE_SLOTS
    tok = lambda w: pl.BlockSpec((GDN_CHUNK, w), lambda i: (i, 0))
    st = pl.BlockSpec((n_sub, GDN_HEADS, GDN_DK, GDN_DV), lambda i: (i, 0, 0, 0))
    return pl.pallas_call(
        _gdn_sample_kernel,
        out_shape=[jax.ShapeDtypeStruct((n, Z_COLS), F32),
                   jax.ShapeDtypeStruct(state.shape, F32)],
        grid=(n // GDN_CHUNK,),
        in_specs=[tok(QK_COLS), tok(QK_COLS), tok(Z_COLS), tok(Z_COLS), tok(LANES),
                  pl.BlockSpec((1, GDN_DV), lambda i: (0, 0)), st],
        out_specs=[tok(Z_COLS), st],
        compiler_params=_cparams(("arbitrary",)), name="gdn_sample",
    )(qg, kg, vg, z, gcb, norm_w, state)


def _post_kernel(a_ref, g_ref, x_ref, wo_ref, ln_g_ref, ln_b_ref, wr_ref, x1_ref, route_ref, *, tm):
    sub = POST_SUB
    rows = [slice(j * sub, (j + 1) * sub) for j in range(tm // sub)]
    d = lambda a, b: jnp.dot(a, b, preferred_element_type=F32)
    mixes = [d(a_ref[r, :].astype(BF16), wo_ref[0:Q_COLS, :]) + d(g_ref[r, :].astype(BF16), wo_ref[Q_COLS:, :])
             for r in rows]
    x1s = [_layer_norm(DEEPNORM_ALPHA * x_ref[r, :] + mix, ln_g_ref[...], ln_b_ref[...])
           for r, mix in zip(rows, mixes)]
    for r, x1 in zip(rows, x1s):
        x1_ref[r, :] = x1
    w2 = wr_ref[...]
    lgs = []
    for x1 in x1s:
        xh = x1.astype(BF16)
        xm = (x1 - xh.astype(F32)).astype(BF16)
        both = d(xh, w2)
        lgs.append(both[:, :LANES] + both[:, LANES:] + d(xm, w2[:, :LANES]))
    for r, lg in zip(rows, lgs):
        route_ref[:, r] = jnp.transpose(_route(lg))[0:SUBLANES, :]


def _route(lg):
    lane = lax.broadcasted_iota(I32, lg.shape, 1)
    lane_f = lane.astype(F32)
    big = float(LANES)

    def first_max(vals, mask):
        v = jnp.where(mask, vals, NEG_BIG)
        mx = jnp.max(v, axis=1, keepdims=True)
        idx = jnp.min(jnp.where(mask & (v == mx), lane_f, big), axis=1, keepdims=True)
        return mx, idx

    gmask = lane < N_GROUPS
    gmax, gidx = first_max(lg, gmask)
    gden = jnp.sum(jnp.where(gmask, jnp.exp(lg - gmax), 0.0), axis=1, keepdims=True)
    g_top_p = 1.0 / gden
    e_lane = lane - N_GROUPS
    e_group = _div_pow2(jnp.maximum(e_lane, 0), EXPERTS_PER_GROUP).astype(F32)
    emask = (e_lane >= 0) & (e_lane < N_EXPERTS) & (e_group == gidx)
    m1, i1 = first_max(lg, emask)
    eden = jnp.sum(jnp.where(emask, jnp.exp(lg - m1), 0.0), axis=1, keepdims=True)
    m2, i2 = first_max(lg, emask & (lane_f != i1))
    p1 = 1.0 / eden
    p2 = jnp.exp(m2 - m1) / eden
    tot = p1 + p2
    gate1 = g_top_p * (p1 / tot)
    gate2 = g_top_p * (p2 / tot)
    return jnp.where(lane == 0, gate1,
                     jnp.where(lane == 1, gate2,
                               jnp.where(lane == 2, i1 - N_GROUPS,
                                         jnp.where(lane == 3, i2 - N_GROUPS, 0.0))))


def _post(attn_o, gdn_o, x, wts):
    n = x.shape[0]
    tm = min(PROJ_TILE, n)
    tok = lambda w: pl.BlockSpec((tm, w), lambda i: (i, 0))
    const = lambda a: pl.BlockSpec(a.shape, lambda i: (0,) * a.ndim)
    consts = [wts['wo'], wts['ln1_g'], wts['ln1_b'], wts['wr']]
    return pl.pallas_call(
        functools.partial(_post_kernel, tm=tm),
        out_shape=[jax.ShapeDtypeStruct((n, D_MODEL), F32), jax.ShapeDtypeStruct((SUBLANES, n), F32)],
        grid=(n // tm,),
        in_specs=[tok(Q_COLS), tok(Z_COLS), tok(D_MODEL)] + [const(a) for a in consts],
        out_specs=[tok(D_MODEL), pl.BlockSpec((SUBLANES, tm), lambda i: (0, i))],
        compiler_params=_cparams(("arbitrary",)), name="post_%d" % (n // tm),
    )(attn_o, gdn_o, x, *consts)


def _slab_loop(n, body):
    n_main = jnp.right_shift(n, int(math.log2(SLAB_UNROLL)))

    def main(i, c):
        for u in range(SLAB_UNROLL):
            body(i * SLAB_UNROLL + u, u)
        return c

    lax.fori_loop(0, n_main, main, 0)
    lax.fori_loop(n_main * SLAB_UNROLL, n, lambda j, c: (body(j, 0), c)[1], 0)


def _dispatch_kernel(dst_ref, nslab_ref, ztab_ref, zinfo_ref, slot_ref, gate_ref, *rest,
                     group_tiles, max_tiles):
    x_refs = rest[:len(group_tiles)]
    xs_ref, pbuf, sem, zbuf, zsem = rest[len(group_tiles):]
    n_tiles = sum(group_tiles)
    g = pl.program_id(0)
    cur = lax.rem(g, 2)

    def slab_copy(tile, buf_slot, j):
        d = pl.multiple_of(dst_ref[tile * PERM_SLABS + j], SLAB)
        src = pbuf.at[buf_slot, pl.ds(pl.multiple_of(j * SLAB, SLAB), SLAB), :]
        return pltpu.make_async_copy(src, xs_ref.at[pl.ds(d, SLAB), :], sem.at[buf_slot])

    def tail_copy(k):
        d = pl.multiple_of(ztab_ref[k], SLAB)
        return pltpu.make_async_copy(zbuf.at[pl.ds(0, SLAB), :], xs_ref.at[pl.ds(d, SLAB), :], zsem)

    def tile_copy(t):
        d = pl.multiple_of(t * ROW_TILE, ROW_TILE)
        return pltpu.make_async_copy(zbuf, xs_ref.at[pl.ds(d, ROW_TILE), :], zsem)

    @pl.when(g == 0)
    def _():
        zbuf[...] = jnp.zeros_like(zbuf)

    share = -(-ZERO_TABLE // n_tiles)
    k0 = g * share
    _slab_loop(jnp.clip(zinfo_ref[0] - k0, 0, share), lambda j, u: tail_copy(k0 + j).start(priority=1))

    @pl.when(zinfo_ref[1] + g < max_tiles)
    def _():
        tile_copy(zinfo_ref[1] + g).start(priority=1)

    x = x_refs[-1][...]
    bound = n_tiles
    for x_ref, nt in zip(x_refs[-2::-1], group_tiles[:0:-1]):
        bound -= nt
        x = jnp.where(g < bound, x_ref[...], x)

    r = lax.broadcasted_iota(I32, (PERM_ROWS, TOK_TILE), 0)
    sl = slot_ref[0]
    hit0, hit1 = r == sl[0:1, :], r == sl[1:2, :]
    onehot = jnp.where(hit0 | hit1, 1.0, 0.0).astype(BF16)
    gt = gate_ref[0]
    gcol = jnp.sum(jnp.where(hit0, gt[0:1, :], 0.0) + jnp.where(hit1, gt[1:2, :], 0.0),
                   axis=1, keepdims=True)
    pbuf[cur, :, 0:D_MODEL] = jnp.dot(onehot, x.astype(BF16), preferred_element_type=F32).astype(BF16)
    g_hi = gcol.astype(BF16).astype(F32)
    lane = lax.broadcasted_iota(I32, (PERM_ROWS, LANES), 1)
    pbuf[cur, :, D_MODEL:] = jnp.where(lane < LANES // 2, g_hi, gcol - g_hi).astype(BF16)

    @pl.when(g > 0)
    def _():
        _slab_loop(nslab_ref[g - 1], lambda j, u: slab_copy(g - 1, 1 - cur, j).wait())

    _slab_loop(nslab_ref[g], lambda j, u: slab_copy(g, cur, j).start(priority=u % 2))

    @pl.when(g == n_tiles - 1)
    def _():
        _slab_loop(nslab_ref[g], lambda j, u: slab_copy(g, cur, j).wait())
        lax.fori_loop(zinfo_ref[1] + n_tiles, max_tiles,
                      lambda t, c: (tile_copy(t).start(priority=1), c)[1], 0)
        _slab_loop(zinfo_ref[0], lambda k, u: tail_copy(k).wait())
        lax.fori_loop(zinfo_ref[1], max_tiles, lambda t, c: (tile_copy(t).wait(), c)[1], 0)


def _dispatch(plan, x1s, max_tiles):
    group_tiles = tuple(x1.shape[0] // TOK_TILE for x1 in x1s)
    n_tiles = sum(group_tiles)
    tile = lambda i, d, ns, zt, zi: (i, 0, 0)
    in_specs = [pl.BlockSpec((1, TOP_K, TOK_TILE), tile), pl.BlockSpec((1, TOP_K, TOK_TILE), tile)]
    base = 0
    for nt in group_tiles:
        in_specs.append(pl.BlockSpec(
            (TOK_TILE, D_MODEL),
            lambda i, d, ns, zt, zi, base=base, nt=nt: (jnp.clip(i - base, 0, nt - 1), 0)))
        base += nt
    return pl.pallas_call(
        functools.partial(_dispatch_kernel, group_tiles=group_tiles, max_tiles=max_tiles),
        out_shape=jax.ShapeDtypeStruct((max_tiles * ROW_TILE, XS_WORDS), BF16),
        grid_spec=pltpu.PrefetchScalarGridSpec(
            num_scalar_prefetch=4, grid=(n_tiles,), in_specs=in_specs,
            out_specs=pl.BlockSpec(memory_space=pl.ANY),
            scratch_shapes=[pltpu.VMEM((2, PERM_ROWS, XS_WORDS), BF16), pltpu.SemaphoreType.DMA((2,)),
                            pltpu.VMEM((ROW_TILE, XS_WORDS), BF16), pltpu.SemaphoreType.DMA(())]),
        compiler_params=_cparams(("arbitrary",)), name="moe_dispatch",
    )(plan['slab_dst'], plan['nslab'], plan['ztab'], plan['zinfo'], plan['slot_rows'], plan['gate_rows'],
      *x1s)


def _expert_kernel(te_ref, nu_ref, xs_ref, wg_ref, wu_ref, wd_ref, ye_ref, wgu_scr, wd_scr):
    i = pl.program_id(0)
    active = i < nu_ref[0]

    @pl.when(active & ((i == 0) | (te_ref[i] != te_ref[jnp.maximum(i - 1, 0)])))
    def _():
        wgu_scr[:, 0:EXPERT_FF] = wg_ref[0].astype(BF16)
        wgu_scr[:, EXPERT_FF:] = wu_ref[0].astype(BF16)
        wd_scr[...] = wd_ref[0].astype(BF16)

    @pl.when(active)
    def _():
        sub = EXPERT_SUB
        rows = [slice(j * sub, (j + 1) * sub) for j in range(ROW_TILE // sub)]
        d = lambda a, b: jnp.dot(a, b, preferred_element_type=F32)
        wgu, wd = wgu_scr[...], wd_scr[...]
        xs = [xs_ref[r, 0:D_MODEL] for r in rows]
        hs = [d(x, wgu) for x in xs]
        hhs = [(_silu(h[:, :EXPERT_FF]) * h[:, EXPERT_FF:]).astype(BF16) for h in hs]
        ys = [d(hh, wd) for hh in hhs]
        for r, y in zip(rows, ys):
            parts = xs_ref[r, D_MODEL:].astype(F32)
            gate = parts + pltpu.roll(parts, LANES // 2, axis=1)
            ye_ref[r, :] = (y * jnp.concatenate([gate] * (D_MODEL // LANES), axis=1)).astype(BF16)

    @pl.when(jnp.logical_not(active))
    def _():
        ye_ref[...] = jnp.zeros_like(ye_ref)


def _experts(tile_expert, n_used, xs, w_gate, w_up, w_down):
    n_tiles = xs.shape[0] // ROW_TILE
    row_in = lambda i, te, nu: (jnp.minimum(i, nu[0] - 1), 0)
    row = lambda i, te, nu: (i, 0)
    wsel = lambda i, te, nu: (te[i], 0, 0)
    return pl.pallas_call(
        _expert_kernel,
        out_shape=jax.ShapeDtypeStruct((xs.shape[0], D_MODEL), BF16),
        grid_spec=pltpu.PrefetchScalarGridSpec(
            num_scalar_prefetch=2, grid=(n_tiles,),
            in_specs=[pl.BlockSpec((ROW_TILE, XS_WORDS), row_in),
                      pl.BlockSpec((1, D_MODEL, EXPERT_FF), wsel),
                      pl.BlockSpec((1, D_MODEL, EXPERT_FF), wsel),
                      pl.BlockSpec((1, EXPERT_FF, D_MODEL), wsel)],
            out_specs=pl.BlockSpec((ROW_TILE, D_MODEL), row),
            scratch_shapes=[pltpu.VMEM((D_MODEL, 2 * EXPERT_FF), BF16),
                            pltpu.VMEM((EXPERT_FF, D_MODEL), BF16)]),
        compiler_params=_cparams(("arbitrary",)), name="moe_experts",
    )(tile_expert, n_used, xs, w_gate, w_up, w_down)


def _combine_kernel(dst_ref, nslab_ref, x1_ref, slot_ref, ye_ref, ln_g_ref, ln_b_ref, y_ref,
                    buf, sem, *, tile_base, n_tiles):
    i = pl.program_id(0)
    g = tile_base + i
    cur = lax.rem(i, 2)

    def slab_copy(tile, buf_slot, j):
        d = pl.multiple_of(dst_ref[tile * PERM_SLABS + j], SLAB)
        dst = buf.at[buf_slot, pl.ds(pl.multiple_of(j * SLAB, SLAB), SLAB), :]
        return pltpu.make_async_copy(ye_ref.at[pl.ds(d, SLAB), :], dst, sem.at[buf_slot])

    @pl.when(i == 0)
    def _():
        buf[...] = jnp.zeros_like(buf)
        _slab_loop(nslab_ref[g], lambda j, u: slab_copy(g, cur, j).start(priority=u % 2))

    @pl.when(i + 1 < n_tiles)
    def _():
        _slab_loop(nslab_ref[g + 1], lambda j, u: slab_copy(g + 1, 1 - cur, j).start(priority=u % 2))

    _slab_loop(nslab_ref[g], lambda j, u: slab_copy(g, cur, j).wait())

    col = lax.broadcasted_iota(I32, (TOK_TILE, PERM_ROWS), 1)
    sl = slot_ref[0]
    diag = (lax.broadcasted_iota(I32, (TOK_TILE, TOK_TILE), 0)
            == lax.broadcasted_iota(I32, (TOK_TILE, TOK_TILE), 1))
    as_col = lambda row: jnp.sum(jnp.where(diag, row, 0), axis=1, keepdims=True)
    pick = jnp.where((col == as_col(sl[0:1, :])) | (col == as_col(sl[1:2, :])), 1.0, 0.0).astype(BF16)
    moe = jnp.dot(pick, buf[cur], preferred_element_type=F32)
    y_ref[...] = _layer_norm(DEEPNORM_ALPHA * x1_ref[...] + moe, ln_g_ref[...], ln_b_ref[...])


def _combine(plan, tile_base, x1, ye, ln_g, ln_b):
    n = x1.shape[0]
    n_tiles = n // TOK_TILE
    tok = lambda w: pl.BlockSpec((TOK_TILE, w), lambda i, d, ns: (i, 0))
    const = lambda a: pl.BlockSpec(a.shape, lambda i, d, ns: (0,) * a.ndim)
    return pl.pallas_call(
        functools.partial(_combine_kernel, tile_base=tile_base, n_tiles=n_tiles),
        out_shape=jax.ShapeDtypeStruct((n, D_MODEL), F32),
        grid_spec=pltpu.PrefetchScalarGridSpec(
            num_scalar_prefetch=2, grid=(n_tiles,),
            in_specs=[tok(D_MODEL),
                      pl.BlockSpec((1, TOP_K, TOK_TILE), lambda i, d, ns: (tile_base + i, 0, 0)),
                      pl.BlockSpec(memory_space=pl.ANY), const(ln_g), const(ln_b)],
            out_specs=tok(D_MODEL),
            scratch_shapes=[pltpu.VMEM((2, PERM_ROWS, D_MODEL), BF16), pltpu.SemaphoreType.DMA((2,))]),
        compiler_params=_cparams(("arbitrary",)), name="moe_combine_%d" % tile_base,
    )(plan['slab_dst'], plan['nslab'], x1, plan['slot_rows'], ye, ln_g, ln_b)


def _routing_plan(ids, gates):
    nt = ids.shape[1] // TOK_TILE
    pairs = TOP_K * TOK_TILE
    ex = jnp.arange(N_EXPERTS, dtype=I32)
    per_tile = lambda a: jnp.swapaxes(a.reshape(TOP_K, nt, TOK_TILE), 0, 1)
    flat = per_tile(ids).reshape(nt, pairs)
    onehot = (flat[:, None, :] == ex[None, :, None])
    p = np.arange(pairs)
    triu = jnp.asarray(p[:, None] <= p[None, :], BF16)
    csum = jnp.dot(onehot.astype(BF16).reshape(nt * N_EXPERTS, pairs), triu,
                   preferred_element_type=F32).astype(I32).reshape(nt, N_EXPERTS, pairs)
    oh = onehot.astype(I32)
    rank = jnp.sum(oh * (csum - 1), axis=1)
    cnt = csum[:, :, -1]
    cpad = (cnt + SLAB - 1) // SLAB * SLAB
    seg_end = jnp.cumsum(cpad, axis=1)
    seg_off = seg_end - cpad
    slot = jnp.sum(oh * seg_off[:, :, None], axis=1) + rank
    run_end = jnp.cumsum(cpad, axis=0)
    ntiles_e = (run_end[-1] + ROW_TILE - 1) // ROW_TILE
    tile_end = jnp.cumsum(ntiles_e)
    dst_run = ((tile_end - ntiles_e) * ROW_TILE)[None, :] + run_end - cpad
    j8 = jnp.arange(PERM_SLABS, dtype=I32) * SLAB
    e_of = jnp.minimum(jnp.sum((j8[None, :, None] >= seg_end[:, None, :]).astype(I32), axis=2),
                       N_EXPERTS - 1)
    sel = (e_of[:, :, None] == ex).astype(I32)
    slab_dst = jnp.sum(sel * (dst_run - seg_off)[:, None, :], axis=2) + j8[None, :]
    n_used = tile_end[-1]
    max_tiles = _max_row_tiles(ids.shape[1])
    t = jnp.arange(max_tiles, dtype=I32)
    te = jnp.sum((t[:, None] >= tile_end[None, :]).astype(I32), axis=1)
    te_last = jnp.sum((n_used - 1 >= tile_end).astype(I32))
    row_start = (tile_end - ntiles_e) * ROW_TILE
    tail_cnt = (ntiles_e * ROW_TILE - run_end[-1]) // SLAB
    tail_end = jnp.cumsum(tail_cnt)
    k = jnp.arange(ZERO_TABLE, dtype=I32)
    e_k = jnp.minimum(jnp.sum((k[:, None] >= tail_end[None, :]).astype(I32), axis=1), N_EXPERTS - 1)
    base_k = jnp.sum((e_k[:, None] == ex).astype(I32)
                     * (row_start + run_end[-1] - SLAB * (tail_end - tail_cnt))[None, :], axis=1)
    return dict(
        slab_dst=slab_dst.reshape(-1).astype(I32), nslab=(seg_end[:, -1] // SLAB).astype(I32),
        ztab=(base_k + SLAB * k).astype(I32), zinfo=jnp.stack([tail_end[-1], n_used]).astype(I32),
        slot_rows=slot.reshape(nt, TOP_K, TOK_TILE).astype(I32),
        gate_rows=per_tile(gates).astype(F32),
        tile_expert=jnp.where(t < n_used, jnp.minimum(te, N_EXPERTS - 1), te_last).astype(I32),
        n_used=n_used.reshape(1).astype(I32))


def _max_row_tiles(n_tokens):
    rows = TOP_K * n_tokens + (n_tokens // TOK_TILE) * N_EXPERTS * (SLAB - 1)
    return (rows + ROW_TILE - 1) // ROW_TILE + N_EXPERTS


def _moe(x1s, routes, wts):
    ids = jnp.concatenate([r[TOP_K:2 * TOP_K, :] for r in routes], axis=1).astype(I32)
    gates = jnp.concatenate([r[0:TOP_K, :] for r in routes], axis=1)
    plan = _routing_plan(ids, gates)
    max_tiles = _max_row_tiles(ids.shape[1])
    bases = [0]
    for x1 in x1s[:-1]:
        bases.append(bases[-1] + x1.shape[0] // TOK_TILE)
    xs = _dispatch(plan, x1s, max_tiles)
    ye = _experts(plan['tile_expert'], plan['n_used'], xs, wts['w_gate'], wts['w_up'], wts['w_down'])
    return [_combine(plan, base, x1, ye, wts['ln2_g'], wts['ln2_b']) for base, x1 in zip(bases, x1s)]


def _prep_weights(w_in, w_out, conv_w, a_log, dt_bias, gdn_norm_w, ln1_g, ln1_b, w_router_group,
                  w_router_expert, w_gate, w_up, w_down, ln2_g, ln2_b):
    pad_row = lambda v: jnp.pad(v.astype(F32), (0, LANES - v.shape[0]))[None, :]
    wr = jnp.pad(jnp.concatenate([w_router_group, w_router_expert], axis=1),
                 ((0, 0), (0, LANES - N_GROUPS - N_EXPERTS)))
    wr_hi = wr.astype(BF16)
    wr_mid = (wr - wr_hi.astype(F32)).astype(BF16)
    group = ATTN_HEADS // ATTN_KV_HEADS
    wq = w_in[:, :Q_COLS].reshape(D_MODEL, ATTN_KV_HEADS, group, HEAD_DIM)
    wq = jnp.swapaxes(wq, 1, 2).reshape(D_MODEL, Q_COLS)
    w_all = jnp.concatenate([wq, w_in[:, Q_COLS:]], axis=1)
    w_all = jnp.pad(w_all, ((0, 0), (0, IN_COLS_PAD - w_all.shape[1]))).astype(BF16)
    wo_q = w_out[:Q_COLS].reshape(ATTN_KV_HEADS, group, HEAD_DIM, D_MODEL)
    wo_q = jnp.swapaxes(wo_q, 0, 1).reshape(Q_COLS, D_MODEL)
    wo = jnp.concatenate([wo_q, w_out[Q_COLS:]], axis=0)
    return dict(
        w_all=w_all, convw=conv_w.astype(F32), alog=pad_row(a_log), dtb=pad_row(dt_bias),
        norm_w=gdn_norm_w.astype(F32)[None, :], wo=wo.astype(BF16),
        ln1_g=ln1_g[None, :], ln1_b=ln1_b[None, :], wr=jnp.concatenate([wr_hi, wr_mid], axis=1),
        w_gate=w_gate, w_up=w_up, w_down=w_down, ln2_g=ln2_g[None, :], ln2_b=ln2_b[None, :])


def _layer(x_prompt, x_sample, cache_k, cache_v, state_gdn, state_conv, wts):
    bp, sp, _ = x_prompt.shape
    bs, ts, _ = x_sample.shape
    n_p = bp * sp

    xp = x_prompt.reshape(n_p, D_MODEL)
    (q, k, v, qg, kg, vg, z, gcb, utail) = _proj(xp, np.arange(sp), wts, GDN_CHUNK, bp)
    attn_p = _attn_prompt(q, k, v, wts['sinks'], bp)
    gdn_p, s_p = _gdn_prompt(qg, kg, vg, z, gcb, wts['norm_w'], bp)
    last_win = lambda a: a.reshape(bp, sp, KV_COLS)[:, sp - WINDOW:].reshape(bp, WINDOW, ATTN_KV_HEADS,
                                                                            HEAD_DIM)
    new_k_p, new_v_p = last_win(k), last_win(v)
    tiles_per_seq = sp // min(PROJ_TILE, sp)
    conv_p = utail.reshape(bp, tiles_per_seq, SUBLANES, CONV_DIM)[:, -1, SUBLANES - (CONV_W - 1):]

    lo, hi = SAMPLE_FIRST, SAMPLE_FIRST + ts
    xs_rows = jnp.pad(x_sample, ((0, 0), (lo, SAMPLE_SLOTS - hi), (0, 0))).reshape(bs * SAMPLE_SLOTS, D_MODEL)
    hist = jnp.pad(state_conv, ((0, 0), (0, SAMPLE_SLOTS - lo), (0, 0))).reshape(bs * SAMPLE_SLOTS, CONV_DIM)
    slot = np.arange(SAMPLE_SLOTS)
    valid = jnp.asarray(np.tile((slot >= lo) & (slot < hi), bs)[:, None], F32)
    pos_s = np.tile(PAST_LEN + slot - lo, bs)
    (q, k, v, qg, kg, vg, z, gcb, u_s) = _proj(xs_rows, pos_s, wts, SAMPLE_SLOTS, 1, hist, valid)
    ck = cache_k.reshape(bs, WINDOW, KV_COLS)
    cv = cache_v.reshape(bs, WINDOW, KV_COLS)
    attn_s, kwin, vwin = _attn_sample(q, k, v, ck, cv, wts['sinks'], bs, ts)
    gdn_s, s_s = _gdn_sample(qg, kg, vg, z, gcb, wts['norm_w'], state_gdn)
    real = lambda a: a.reshape(bs, SAMPLE_SLOTS, -1)[:, lo:hi]
    new_k_s = kwin.reshape(bs, WINDOW, ATTN_KV_HEADS, HEAD_DIM)
    new_v_s = vwin.reshape(bs, WINDOW, ATTN_KV_HEADS, HEAD_DIM)
    conv_s = u_s.reshape(bs, SAMPLE_SLOTS, CONV_DIM)[:, hi - (CONV_W - 1):hi]

    x1_p, route_p = _post(attn_p, gdn_p, xp, wts)
    x1_s, route_s = _post(real(attn_s).reshape(bs * ts, Q_COLS), real(gdn_s).reshape(bs * ts, Z_COLS),
                          x_sample.reshape(bs * ts, D_MODEL), wts)
    y_p, y_s = _moe([x1_p, x1_s], [route_p, route_s], wts)
    return (y_p.reshape(bp, sp, D_MODEL), y_s.reshape(bs, ts, D_MODEL), new_k_p, new_v_p, s_p, conv_p,
            new_k_s, new_v_s, s_s, conv_s)


def kernel(x_prompt, x_sample, cache_attn_k, cache_attn_v, state_gdn, state_conv, w_in, w_out,
           attn_sinks, conv_w, a_log, dt_bias, gdn_norm_w, ln1_g, ln1_b, w_router_group,
           w_router_expert, w_gate, w_up, w_down, ln2_g, ln2_b):
    assert w_in.shape[0] == DEPTH
    l = 0
    wts = _prep_weights(w_in[l], w_out[l], conv_w[l], a_log[l], dt_bias[l], gdn_norm_w[l], ln1_g[l],
                        ln1_b[l], w_router_group[l], w_router_expert[l], w_gate[l], w_up[l],
                        w_down[l], ln2_g[l], ln2_b[l])
    wts['sinks'] = attn_sinks[l]
    outs = _layer(x_prompt, x_sample, cache_attn_k[l], cache_attn_v[l], state_gdn[l], state_conv[l], wts)
    (y_p, y_s, k_p, v_p, s_p, c_p, k_s, v_s, s_s, c_s) = outs
    add = lambda a: a[None]
    return (y_p, y_s, add(k_p), add(v_p), add(s_p), add(c_p), add(k_s), add(v_s), add(s_s), add(c_s))
```

```python
import functools
import math

import jax
import jax.numpy as jnp
import numpy as np
from jax import lax
from jax.experimental import pallas as pl
from jax.experimental.pallas import tpu as pltpu

F32 = jnp.float32
BF16 = jnp.bfloat16
I32 = jnp.int32

D_MODEL = 1024
ATTN_HEADS = 8
ATTN_KV_HEADS = 2
HEAD_DIM = 64
WINDOW = 128
ROT_DIM = HEAD_DIM // 4
ROPE_THETA = 500000.0
GDN_HEADS = 4
GDN_DK = 128
GDN_DV = 128
CONV_W = 4
QK_COLS = GDN_HEADS * GDN_DK
CONV_DIM = 2 * QK_COLS + GDN_HEADS * GDN_DV
Z_COLS = GDN_HEADS * GDN_DV
Q_COLS = ATTN_HEADS * HEAD_DIM
KV_COLS = ATTN_KV_HEADS * HEAD_DIM
N_GROUPS = 4
EXPERTS_PER_GROUP = 8
N_EXPERTS = N_GROUPS * EXPERTS_PER_GROUP
TOP_K = 2
EXPERT_FF = 256
NORM_EPS = 1e-5
L2_EPS = 1e-6
DEPTH = 1
DEEPNORM_ALPHA = (2 * DEPTH) ** 0.25
PAST_LEN = 8192

LANES = 128
SUBLANES = 8
IN_SPLITS = (0, Q_COLS + 2 * KV_COLS, Q_COLS + 2 * KV_COLS + CONV_DIM,
             Q_COLS + 2 * KV_COLS + CONV_DIM + Z_COLS, Q_COLS + 2 * KV_COLS + CONV_DIM + Z_COLS + LANES)
IN_COLS_PAD = IN_SPLITS[-1]
TOK_TILE = 512
PROJ_TILE = 512
PROJ_SUB = 128
POST_SUB = 256
GDN_CHUNK = 128
GDN_SEQ_PER_STEP = 4
ATTN_BLOCKS_PER_STEP = 4
ATTN_SEQS_PER_STEP = 8
INV_BASE = 16
SAMPLE_SLOTS = 8
SAMPLE_FIRST = CONV_W - 1
ROW_TILE = 512
EXPERT_SUB = 256
SLAB_UNROLL = 4
SLAB = 16
PERM_ROWS = TOP_K * TOK_TILE + N_EXPERTS * SLAB
PERM_SLABS = PERM_ROWS // SLAB
XS_WORDS = D_MODEL + LANES
ZERO_TABLE = N_EXPERTS * (ROW_TILE // SLAB)
VMEM_LIMIT = 48 * 1024 * 1024
NEG_BIG = -1e30


def _cparams(sem):
    return pltpu.CompilerParams(dimension_semantics=sem, vmem_limit_bytes=VMEM_LIMIT)


def _bdot(a, b):
    return jnp.dot(a.astype(BF16), b.astype(BF16), preferred_element_type=F32)


def _bdot_nt(a, b):
    return lax.dot_general(a.astype(BF16), b.astype(BF16), (((1,), (1,)), ((), ())),
                           preferred_element_type=F32)


def _bdot_tn(a, b):
    return lax.dot_general(a.astype(BF16), b.astype(BF16), (((0,), (0,)), ((), ())),
                           preferred_element_type=F32)


def _div_pow2(x, n):
    return jnp.right_shift(x, int(math.log2(n)))


def _mod_pow2(x, n):
    return jnp.bitwise_and(x, n - 1)


def _split3(x):
    hi = x.astype(BF16)
    r = x - hi.astype(F32)
    mid = r.astype(BF16)
    lo = (r - mid.astype(F32)).astype(BF16)
    return hi, mid, lo


def _dot_exact_lhs01(m01, x):
    hi, mid, lo = _split3(x)
    d = lambda t: jnp.dot(m01, t, preferred_element_type=F32)
    return d(hi) + d(mid) + d(lo)


def _sigmoid(x):
    return 1.0 / (1.0 + jnp.exp(-x))


def _silu(x):
    return x * _sigmoid(x)


def _softplus(x):
    return jnp.maximum(x, 0.0) + jnp.log1p(jnp.exp(-jnp.abs(x)))


def _layer_norm(h, g, b):
    mu = jnp.mean(h, axis=-1, keepdims=True)
    d = h - mu
    var = jnp.mean(d * d, axis=-1, keepdims=True)
    return d * lax.rsqrt(var + NORM_EPS) * g + b


def _proj_kernel(*refs, tm, has_hist, full_u, one_segment):
    it = iter(refs)
    x_ref, cos_ref, sin_ref = next(it), next(it), next(it)
    w_ref = next(it)
    convw_ref, alog_ref, dtb_ref, tri_ref, seg_ref = next(it), next(it), next(it), next(it), next(it)
    hist_ref = valid_ref = None
    if has_hist:
        hist_ref, valid_ref = next(it), next(it)
    q_ref, k_ref, v_ref = next(it), next(it), next(it)
    qg_ref, kg_ref, vg_ref, z_ref, gcb_ref, u_ref = (next(it) for _ in range(6))
    ubuf = next(it)

    t = pl.program_id(1)
    sub = PROJ_SUB
    rows = [slice(j * sub, (j + 1) * sub) for j in range(tm // sub)]
    lane = lax.broadcasted_iota(I32, (sub, LANES), 1)
    first_half = _mod_pow2(lane, HEAD_DIM) < (ROT_DIM // 2)

    @pl.when(t == 0)
    def _():
        ubuf[0:SUBLANES, :] = jnp.zeros((SUBLANES, CONV_DIM), F32)

    @pl.when(t > 0)
    def _():
        ubuf[0:SUBLANES, :] = ubuf[tm:tm + SUBLANES, :]

    dots = []
    for r in rows:
        xb = x_ref[r, :].astype(BF16)
        dots.append([jnp.dot(xb, w_ref[:, lo:hi], preferred_element_type=F32)
                     for lo, hi in zip(IN_SPLITS[:-1], IN_SPLITS[1:])])

    def l2n(s):
        return s * lax.rsqrt(jnp.sum(s * s, axis=1, keepdims=True) + L2_EPS)

    for r, (pq, u, z, ab) in zip(rows, dots):
        cosv, sinv = cos_ref[r, :], sin_ref[r, :]

        def rope(s):
            sw = jnp.where(first_half, pltpu.roll(s, LANES - ROT_DIM // 2, axis=1),
                           pltpu.roll(s, ROT_DIM // 2, axis=1))
            return s * cosv + sw * sinv

        for j in range(Q_COLS // LANES):
            q_ref[r, j * LANES:(j + 1) * LANES] = rope(pq[:, j * LANES:(j + 1) * LANES])
        k_ref[r, :] = rope(pq[:, Q_COLS:Q_COLS + KV_COLS])
        v_ref[r, :] = pq[:, Q_COLS + KV_COLS:Q_COLS + 2 * KV_COLS]
        z_ref[r, :] = z

        if has_hist:
            u = u + hist_ref[r, :]
        if full_u:
            u_ref[r, :] = u
        elif r.stop == tm:
            u_ref[...] = u[sub - SUBLANES:, :]
        base = SUBLANES + r.start
        ubuf[base:base + sub, :] = u
        acc = u * convw_ref[CONV_W - 1:CONV_W, :]
        for j in range(1, CONV_W):
            acc = acc + ubuf[base - j:base - j + sub, :] * convw_ref[CONV_W - 1 - j:CONV_W - j, :]
        c = _silu(acc)
        if has_hist:
            c = c * valid_ref[r, :]
        for h in range(GDN_HEADS):
            sl = slice(h * GDN_DK, (h + 1) * GDN_DK)
            qg_ref[r, sl] = l2n(c[:, sl]) * (GDN_DK ** -0.5)
            kg_ref[r, sl] = l2n(c[:, QK_COLS + h * GDN_DK:QK_COLS + (h + 1) * GDN_DK])
        vg_ref[r, :] = c[:, 2 * QK_COLS:]

        g = -jnp.exp(alog_ref[...]) * _softplus(ab + dtb_ref[...])
        beta = _sigmoid(ab)
        if has_hist:
            g = g * valid_ref[r, :]
            beta = beta * valid_ref[r, :]
        g = jnp.where(lane < GDN_HEADS, g, 0.0)
        gc = _dot_exact_lhs01(tri_ref[...], g)
        if one_segment:
            gl = jnp.broadcast_to(gc[sub - 1:sub, :], (sub, LANES))
        else:
            gl = _dot_exact_lhs01(seg_ref[...], g)
        gcb_ref[r, :] = jnp.where(lane < GDN_HEADS, gc,
                                  jnp.where(lane < 2 * GDN_HEADS, beta,
                                            jnp.where(lane < 3 * GDN_HEADS,
                                                      pltpu.roll(gl, 2 * GDN_HEADS, axis=1), 0.0)))


def _rope_tables(pos):
    half = ROT_DIM // 2
    pos = np.asarray(pos, np.float64)
    inv_freq = ROPE_THETA ** (-np.arange(half, dtype=np.float64) * 2.0 / ROT_DIM)
    ang = pos[:, None] * inv_freq[None, :]
    cos, sin = np.cos(ang), np.sin(ang)
    p = pos.shape[0]
    cpat = np.concatenate([cos, cos, np.ones((p, HEAD_DIM - ROT_DIM))], axis=1)
    spat = np.concatenate([-sin, sin, np.zeros((p, HEAD_DIM - ROT_DIM))], axis=1)
    rep = (1, LANES // HEAD_DIM)
    return jnp.asarray(np.tile(cpat, rep), F32), jnp.asarray(np.tile(spat, rep), F32)


def _segment_matrices(tm, seg_len):
    i = np.arange(tm)
    same = (i[:, None] // seg_len) == (i[None, :] // seg_len)
    tri = same & (i[None, :] <= i[:, None])
    return jnp.asarray(tri, BF16), jnp.asarray(same, BF16)


def _proj(x, pos, wts, seg_len, n_seq, hist=None, valid=None):
    n = x.shape[0]
    rows = n // n_seq
    tm = min(PROJ_TILE, rows)
    nt = rows // tm
    has_hist = hist is not None
    cos_t, sin_t = _rope_tables(pos)
    tri, seg = _segment_matrices(PROJ_SUB, seg_len)

    tok = lambda w: pl.BlockSpec((tm, w), lambda b, t: (b * nt + t, 0))
    const = lambda a: pl.BlockSpec(a.shape, lambda b, t: (0,) * a.ndim)
    in_arrays = [x, cos_t, sin_t, wts['w_all'], wts['convw'], wts['alog'], wts['dtb'], tri, seg]
    in_specs = [tok(D_MODEL), pl.BlockSpec((tm, LANES), lambda b, t: (t, 0)),
                pl.BlockSpec((tm, LANES), lambda b, t: (t, 0))] + [const(a) for a in in_arrays[3:]]
    if has_hist:
        in_arrays += [hist, valid]
        in_specs += [tok(CONV_DIM), tok(1)]
    u_rows = n if has_hist else (n // tm) * SUBLANES
    u_block = tm if has_hist else SUBLANES
    out_shape = [jax.ShapeDtypeStruct((n, Q_COLS), F32), jax.ShapeDtypeStruct((n, KV_COLS), F32),
                 jax.ShapeDtypeStruct((n, KV_COLS), F32), jax.ShapeDtypeStruct((n, QK_COLS), F32),
                 jax.ShapeDtypeStruct((n, QK_COLS), F32), jax.ShapeDtypeStruct((n, Z_COLS), F32),
                 jax.ShapeDtypeStruct((n, Z_COLS), F32), jax.ShapeDtypeStruct((n, LANES), F32),
                 jax.ShapeDtypeStruct((u_rows, CONV_DIM), F32)]
    out_specs = [tok(Q_COLS), tok(KV_COLS), tok(KV_COLS), tok(QK_COLS), tok(QK_COLS), tok(Z_COLS),
                 tok(Z_COLS), tok(LANES),
                 pl.BlockSpec((u_block, CONV_DIM), lambda b, t: (b * nt + t, 0))]
    return pl.pallas_call(
        functools.partial(_proj_kernel, tm=tm, has_hist=has_hist, full_u=has_hist,
                          one_segment=seg_len == PROJ_SUB),
        out_shape=out_shape, grid=(n_seq, nt), in_specs=in_specs, out_specs=out_specs,
        scratch_shapes=[pltpu.VMEM((tm + SUBLANES, CONV_DIM), F32)],
        compiler_params=_cparams(("arbitrary", "arbitrary")),
        name="proj_hist" if has_hist else "proj",
    )(*in_arrays)


def _attn_blocks(qs, kcats, vcats, biases, sink, tq):
    lane = lax.broadcasted_iota(I32, (tq, LANES), 1)
    low = lane < HEAD_DIM
    n_slab = Q_COLS // LANES

    def stack(q):
        slabs = [q[:, j * LANES:(j + 1) * LANES] * (HEAD_DIM ** -0.5) for j in range(n_slab)]
        parts = ([jnp.where(low, s, 0.0) for s in slabs] + [jnp.where(low, 0.0, s) for s in slabs])
        return jnp.concatenate(parts, axis=0).astype(BF16)

    def unstack(o8):
        return [jnp.where(low, o8[j * tq:(j + 1) * tq, :], o8[(n_slab + j) * tq:(n_slab + j + 1) * tq, :])
                for j in range(n_slab)]

    rows = ATTN_HEADS * tq
    half = rows // 2
    klow = lax.broadcasted_iota(I32, (2 * WINDOW, LANES), 1) < HEAD_DIM
    one = jnp.ones((), BF16)
    q8s = _each(stack, qs)
    ss = _each(lambda q8, kc, b: _bdot_nt(q8, kc) + b, q8s, kcats, biases)
    ms = _each(lambda s: jnp.maximum(jnp.broadcast_to(jnp.max(s, axis=1, keepdims=True), (rows, LANES)),
                                     sink), ss)
    ps = _each(lambda s, m: jnp.exp(s - jnp.concatenate([m, m], axis=1)).astype(BF16), ss, ms)
    pv0 = _each(lambda p, vc: jnp.dot(p[:half], jnp.where(klow, vc, one), preferred_element_type=F32),
                ps, vcats)
    pv1 = _each(lambda p, vc: jnp.dot(p[half:], jnp.where(klow, one, vc), preferred_element_type=F32),
                ps, vcats)
    pvs = _each(lambda a, b: jnp.concatenate([a, b], axis=0), pv0, pv1)
    o8s = _each(lambda pv, m: pv / (pltpu.roll(pv, HEAD_DIM, axis=1) + jnp.exp(sink - m)), pvs, ms)
    return _each(unstack, o8s)


def _attn_prompt_kernel(q_ref, kc_ref, vc_ref, kp_ref, vp_ref, bias0_ref, bias_ref, sink_ref, o_ref, *,
                        nblk):
    kall = jnp.concatenate([kp_ref[...], kc_ref[...]], axis=0).astype(BF16)
    vall = jnp.concatenate([vp_ref[...], vc_ref[...]], axis=0).astype(BF16)
    win = lambda a, j: a[j * WINDOW:(j + 2) * WINDOW, :]
    qs = [q_ref[j * WINDOW:(j + 1) * WINDOW, :] for j in range(nblk)]
    biases = [bias0_ref[0]] + [bias_ref[...]] * (nblk - 1)
    outs = _attn_blocks(qs, [win(kall, j) for j in range(nblk)], [win(vall, j) for j in range(nblk)],
                        biases, sink_ref[...], WINDOW)
    for j, slabs in enumerate(outs):
        for c, slab in enumerate(slabs):
            o_ref[j * WINDOW:(j + 1) * WINDOW, c * LANES:(c + 1) * LANES] = slab


def _attn_sample_kernel(q_ref, kc_ref, vc_ref, kp_ref, vp_ref, bias_ref, sink_ref, o_ref, kw_ref, vw_ref,
                        *, nseq, n_new):
    tq = SAMPLE_SLOTS
    zpad = jnp.zeros((WINDOW - tq, LANES), F32)
    rows = lambda ref, j: ref[j * tq:(j + 1) * tq, :]
    cat = lambda pref, cref, j: jnp.concatenate([pref[j], rows(cref, j), zpad], axis=0).astype(BF16)
    outs = _attn_blocks([rows(q_ref, j) for j in range(nseq)],
                        [cat(kp_ref, kc_ref, j) for j in range(nseq)],
                        [cat(vp_ref, vc_ref, j) for j in range(nseq)],
                        [bias_ref[...]] * nseq, sink_ref[...], tq)
    for j, slabs in enumerate(outs):
        for c, slab in enumerate(slabs):
            o_ref[j * tq:(j + 1) * tq, c * LANES:(c + 1) * LANES] = slab
    row = lax.broadcasted_iota(I32, (WINDOW, LANES), 0)
    keep = WINDOW - n_new
    for pref, cref, wref in ((kp_ref, kc_ref, kw_ref), (vp_ref, vc_ref, vw_ref)):
        for j in range(nseq):
            new = jnp.concatenate([rows(cref, j), zpad], axis=0)
            wref[j] = jnp.where(row < keep, pltpu.roll(pref[j], keep, axis=0),
                                pltpu.roll(new, keep - SAMPLE_FIRST, axis=0))


def _sink_rows(sinks, tq):
    return jnp.broadcast_to(jnp.repeat(sinks.astype(F32), tq)[:, None], (ATTN_HEADS * tq, LANES))


def _attn_bias(tq, q_off, k_lo, k_hi, has_prev):
    qi = (np.arange(ATTN_HEADS * tq) % tq)[:, None]
    c = np.arange(2 * WINDOW)[None, :]
    cj = c - WINDOW
    vis_prev = (c < WINDOW) & (c > qi - q_off) & has_prev
    vis_cur = (c >= WINDOW) & (cj <= qi) & (cj >= k_lo) & (cj <= k_hi)
    return np.where(vis_prev | vis_cur, 0.0, NEG_BIG).astype(np.float32)


def _attn_prompt(q, k, v, sinks, n_seq):
    n = q.shape[0]
    nb = n // n_seq // WINDOW
    nblk = min(ATTN_BLOCKS_PER_STEP, nb)
    steps = nb // nblk
    tq = nblk * WINDOW
    cur = lambda w: pl.BlockSpec((tq, w), lambda b, i: (b * steps + i, 0))
    prev = pl.BlockSpec((WINDOW, LANES), lambda b, i: (b * nb + jnp.maximum(i * nblk - 1, 0), 0))
    bias2 = jnp.asarray(np.stack([_attn_bias(WINDOW, 0, 0, WINDOW - 1, False),
                                  _attn_bias(WINDOW, 0, 0, WINDOW - 1, True)]))
    rows = ATTN_HEADS * WINDOW
    return pl.pallas_call(
        functools.partial(_attn_prompt_kernel, nblk=nblk),
        out_shape=jax.ShapeDtypeStruct((n, Q_COLS), F32), grid=(n_seq, steps),
        in_specs=[cur(Q_COLS), cur(LANES), cur(LANES), prev, prev,
                  pl.BlockSpec((1, rows, 2 * WINDOW), lambda b, i: (jnp.minimum(i, 1), 0, 0)),
                  pl.BlockSpec((rows, 2 * WINDOW), lambda b, i: (0, 0)),
                  pl.BlockSpec((rows, LANES), lambda b, i: (0, 0))],
        out_specs=cur(Q_COLS),
        compiler_params=_cparams(("arbitrary", "arbitrary")), name="attn_prompt",
    )(q, k, v, k, v, bias2, bias2[1], _sink_rows(sinks, WINDOW))


def _attn_sample(q, k, v, cache_k, cache_v, sinks, n_seq, n_new):
    tq = SAMPLE_SLOTS
    nseq = min(ATTN_SEQS_PER_STEP, n_seq)
    cur = lambda w: pl.BlockSpec((nseq * tq, w), lambda b: (b, 0))
    prev = pl.BlockSpec((nseq, WINDOW, LANES), lambda b: (b, 0, 0))
    bias = jnp.asarray(_attn_bias(tq, SAMPLE_FIRST, SAMPLE_FIRST, SAMPLE_FIRST + 3, True))
    win = jax.ShapeDtypeStruct((n_seq, WINDOW, LANES), F32)
    return pl.pallas_call(
        functools.partial(_attn_sample_kernel, nseq=nseq, n_new=n_new),
        out_shape=[jax.ShapeDtypeStruct((n_seq * tq, Q_COLS), F32), win, win], grid=(n_seq // nseq,),
        in_specs=[cur(Q_COLS), cur(LANES), cur(LANES), prev, prev,
                  pl.BlockSpec(bias.shape, lambda b: (0, 0)),
                  pl.BlockSpec((ATTN_HEADS * tq, LANES), lambda b: (0, 0))],
        out_specs=[cur(Q_COLS), prev, prev],
        compiler_params=_cparams(("arbitrary",)), name="attn_sample",
    )(q, k, v, cache_k, cache_v, bias, _sink_rows(sinks, tq))


def _each(f, *lists):
    return [f(*args) for args in zip(*lists)]


def _unit_lower_inverse(ms, eye, same_base, base_only=False):
    c = ms[0].shape[0]

    def neumann(q0s, n_factors):
        xs = _each(lambda q: eye + q, q0s)
        if n_factors == 1:
            return xs
        qs = _each(_bdot, q0s, q0s)
        for _ in range(n_factors - 2):
            prods = _each(lambda x, q: _bdot(jnp.concatenate([x, q], axis=0), q), xs, qs)
            xs = _each(lambda x, pr: x + pr[:c], xs, prods)
            qs = _each(lambda pr: pr[c:], prods)
        return _each(lambda x, q: x + _bdot(x, q), xs, qs)

    ds = _each(lambda m: jnp.where(same_base, m, 0.0), ms)
    xs = neumann(_each(lambda d: -d, ds), int(math.log2(INV_BASE)))
    nblk = c // INV_BASE
    if nblk == 1 or base_only:
        return xs
    ls = _each(lambda m, d: m - d, ms, ds)
    ns = _each(lambda x, l: -_bdot(x, l), xs, ls)
    ys = neumann(ns, int(math.log2(nblk)))
    return _each(_bdot, ys, xs)


def _gdn_intra(qs, ks, vs, gcs, gls, betas, same_seq, low_incl, low_strict, eye, same_base,
               base_only=False):
    del same_seq
    e_gcs = _each(jnp.exp, gcs)

    def decay_of(gc):
        gc_row = jnp.sum(jnp.where(eye > 0, gc, 0.0), axis=0, keepdims=True)
        return jnp.where(low_incl, jnp.exp(jnp.where(low_incl, gc - gc_row, 0.0)), 0.0)

    c = qs[0].shape[0]
    decays = _each(decay_of, gcs)
    kbs = _each(lambda k, b: k * b, ks, betas)
    vbs = _each(lambda v, b: v * b, vs, betas)
    kqs = _each(lambda kb, q, k: _bdot_nt(jnp.concatenate([kb, q], axis=0), k), kbs, qs, ks)
    ms = _each(lambda kq, d: jnp.where(low_strict, kq[:c] * d, 0.0), kqs, decays)
    attns = _each(lambda kq, d: jnp.where(low_incl, kq[c:] * d, 0.0), kqs, decays)
    tmats = _unit_lower_inverse(ms, eye, same_base, base_only)
    uws = _each(lambda t, vb, kb, e: _bdot(t, jnp.concatenate([vb, kb * e], axis=1)),
                tmats, vbs, kbs, e_gcs)
    us = _each(lambda uw: uw[:, :GDN_DV], uws)
    ws = _each(lambda uw: uw[:, GDN_DV:], uws)
    q_decs = _each(lambda q, e: q * e, qs, e_gcs)
    k_decs = _each(lambda k, gl, gc: k * jnp.exp(gl - gc), ks, gls, gcs)
    return us, ws, attns, q_decs, k_decs


def _chunk_masks(c, seq_len):
    i = lax.broadcasted_iota(I32, (c, c), 0)
    j = lax.broadcasted_iota(I32, (c, c), 1)
    same_seq = _div_pow2(i, seq_len) == _div_pow2(j, seq_len)
    low_incl = same_seq & (i >= j)
    low_strict = same_seq & (i > j)
    eye = (i == j).astype(F32)
    same_base = _div_pow2(i, INV_BASE) == _div_pow2(j, INV_BASE)
    return same_seq, low_incl, low_strict, eye, same_base


def _gated_rms(o, z, nw):
    o = o * lax.rsqrt(jnp.mean(o * o, axis=1, keepdims=True) + NORM_EPS) * nw
    return o * _silu(z)


def _gdn_prompt_kernel(qg_ref, kg_ref, vg_ref, z_ref, gcb_ref, nw_ref, o_ref, s_out_ref, s_scr):
    c = GDN_CHUNK
    n = pl.program_id(1)

    @pl.when(n == 0)
    def _():
        s_scr[...] = jnp.zeros_like(s_scr)

    masks = _chunk_masks(c, c)
    nw = nw_ref[...]
    chains = [(b, h) for b in range(qg_ref.shape[0]) for h in range(GDN_HEADS)]
    hs = lambda h: slice(h * GDN_DK, (h + 1) * GDN_DK)
    col = lambda off: [gcb_ref[b, :, off + h:off + h + 1] for b, h in chains]
    gcs, betas, gls = col(0), col(GDN_HEADS), col(2 * GDN_HEADS)
    qs = [qg_ref[b, :, hs(h)] for b, h in chains]
    ks = [kg_ref[b, :, hs(h)] for b, h in chains]
    vs = [vg_ref[b, :, hs(h)] for b, h in chains]
    us, ws, attns, q_decs, k_decs = _gdn_intra(qs, ks, vs, gcs, gls, betas, *masks)
    ss = [s_scr[b, h] for b, h in chains]
    wqs = _each(lambda w, qd, s: _bdot(jnp.concatenate([w, qd], axis=0), s), ws, q_decs, ss)
    wss = _each(lambda wq: wq[:c], wqs)
    qss = _each(lambda wq: wq[c:], wqs)
    v_news = _each(lambda u, x: u - x, us, wss)
    avs = _each(_bdot, attns, v_news)
    kvs = _each(_bdot_tn, k_decs, v_news)
    for (b, h), s, gl, qsv, av, kv in zip(chains, ss, gls, qss, avs, kvs):
        s_scr[b, h] = s * jnp.exp(gl[0:1, :]) + kv
        o_ref[b, :, hs(h)] = _gated_rms(qsv + av, z_ref[b, :, hs(h)], nw)

    @pl.when(n == pl.num_programs(1) - 1)
    def _():
        s_out_ref[...] = s_scr[...]


def _gdn_prompt(qg, kg, vg, z, gcb, norm_w, n_seq):
    n = qg.shape[0]
    s_len = n // n_seq
    nb = min(GDN_SEQ_PER_STEP, n_seq)
    v3 = lambda a: a.reshape(n_seq, s_len, a.shape[-1])
    tok = lambda w: pl.BlockSpec((nb, GDN_CHUNK, w), lambda b, i: (b, i, 0))
    o, s = pl.pallas_call(
        _gdn_prompt_kernel,
        out_shape=[jax.ShapeDtypeStruct((n_seq, s_len, Z_COLS), F32),
                   jax.ShapeDtypeStruct((n_seq, GDN_HEADS, GDN_DK, GDN_DV), F32)],
        grid=(n_seq // nb, s_len // GDN_CHUNK),
        in_specs=[tok(QK_COLS), tok(QK_COLS), tok(Z_COLS), tok(Z_COLS), tok(LANES),
                  pl.BlockSpec((1, GDN_DV), lambda b, i: (0, 0))],
        out_specs=[tok(Z_COLS),
                   pl.BlockSpec((nb, GDN_HEADS, GDN_DK, GDN_DV), lambda b, i: (b, 0, 0, 0))],
        scratch_shapes=[pltpu.VMEM((nb, GDN_HEADS, GDN_DK, GDN_DV), F32)],
        compiler_params=_cparams(("arbitrary", "arbitrary")), name="gdn_prompt",
    )(v3(qg), v3(kg), v3(vg), v3(z), v3(gcb), norm_w)
    return o.reshape(n, Z_COLS), s


def _gdn_sample_kernel(qg_ref, kg_ref, vg_ref, z_ref, gcb_ref, nw_ref, s_in_ref, o_ref, s_out_ref):
    c = GDN_CHUNK
    n_sub = c // SAMPLE_SLOTS
    masks = _chunk_masks(c, SAMPLE_SLOTS)
    heads = range(GDN_HEADS)
    hs = lambda h: slice(h * GDN_DK, (h + 1) * GDN_DK)
    rs = lambda s: slice(s * SAMPLE_SLOTS, (s + 1) * SAMPLE_SLOTS)
    col = lambda off: [gcb_ref[:, off + h:off + h + 1] for h in heads]
    gcs, betas, gls = col(0), col(GDN_HEADS), col(2 * GDN_HEADS)
    us, ws, attns, q_decs, k_decs = _gdn_intra([qg_ref[:, hs(h)] for h in heads],
                                               [kg_ref[:, hs(h)] for h in heads],
                                               [vg_ref[:, hs(h)] for h in heads], gcs, gls, betas, *masks,
                                               base_only=SAMPLE_SLOTS <= INV_BASE)
    pairs = [(h, s) for h in heads for s in range(n_sub)]
    sts = [s_in_ref[s, h] for h, s in pairs]
    boths = [jnp.concatenate([ws[h][rs(s), :], q_decs[h][rs(s), :]], axis=0) for h, s in pairs]
    rr = _each(_bdot, boths, sts)
    gather = lambda h, part: jnp.concatenate(
        [rr[h * n_sub + s][part * SAMPLE_SLOTS:(part + 1) * SAMPLE_SLOTS, :] for s in range(n_sub)], axis=0)
    v_news = [us[h] - gather(h, 0) for h in heads]
    avs = _each(_bdot, attns, v_news)
    row = lax.broadcasted_iota(I32, (c, LANES), 0)
    seq_of_row = _div_pow2(row, SAMPLE_SLOTS)
    kds = [jnp.where(seq_of_row == s, k_decs[h], 0.0) for h, s in pairs]
    kvs = _each(_bdot_tn, kds, [v_news[h] for h, _ in pairs])
    egls = _each(jnp.exp, gls)
    for (h, s), st, kv in zip(pairs, sts, kvs):
        s_out_ref[s, h] = st * egls[h][s * SAMPLE_SLOTS:s * SAMPLE_SLOTS + 1, :] + kv
    nw = nw_ref[...]
    for h in heads:
        o_ref[:, hs(h)] = _gated_rms(gather(h, 1) + avs[h], z_ref[:, hs(h)], nw)


def _gdn_sample(qg, kg, vg, z, gcb, norm_w, state):
    n = qg.shape[0]
    n_sub = GDN_CHUNK // SAMPLE_SLOTS
    tok = lambda w: pl.BlockSpec((GDN_CHUNK, w), lambda i: (i, 0))
    st = pl.BlockSpec((n_sub, GDN_HEADS, GDN_DK, GDN_DV), lambda i: (i, 0, 0, 0))
    return pl.pallas_call(
        _gdn_sample_kernel,
        out_shape=[jax.ShapeDtypeStruct((n, Z_COLS), F32),
                   jax.ShapeDtypeStruct(state.shape, F32)],
        grid=(n // GDN_CHUNK,),
        in_specs=[tok(QK_COLS), tok(QK_COLS), tok(Z_COLS), tok(Z_COLS), tok(LANES),
                  pl.BlockSpec((1, GDN_DV), lambda i: (0, 0)), st],
        out_specs=[tok(Z_COLS), st],
        compiler_params=_cparams(("arbitrary",)), name="gdn_sample",
    )(qg, kg, vg, z, gcb, norm_w, state)


def _post_kernel(a_ref, g_ref, x_ref, wo_ref, ln_g_ref, ln_b_ref, wr_ref, x1_ref, route_ref, *, tm):
    sub = POST_SUB
    rows = [slice(j * sub, (j + 1) * sub) for j in range(tm // sub)]
    d = lambda a, b: jnp.dot(a, b, preferred_element_type=F32)
    mixes = [d(a_ref[r, :].astype(BF16), wo_ref[0:Q_COLS, :]) + d(g_ref[r, :].astype(BF16), wo_ref[Q_COLS:, :])
             for r in rows]
    x1s = [_layer_norm(DEEPNORM_ALPHA * x_ref[r, :] + mix, ln_g_ref[...], ln_b_ref[...])
           for r, mix in zip(rows, mixes)]
    for r, x1 in zip(rows, x1s):
        x1_ref[r, :] = x1
    w2 = wr_ref[...]
    lgs = []
    for x1 in x1s:
        xh = x1.astype(BF16)
        xm = (x1 - xh.astype(F32)).astype(BF16)
        both = d(xh, w2)
        lgs.append(both[:, :LANES] + both[:, LANES:] + d(xm, w2[:, :LANES]))
    for r, lg in zip(rows, lgs):
        route_ref[:, r] = jnp.transpose(_route(lg))[0:SUBLANES, :]


def _route(lg):
    lane = lax.broadcasted_iota(I32, lg.shape, 1)
    lane_f = lane.astype(F32)
    big = float(LANES)

    def first_max(vals, mask):
        v = jnp.where(mask, vals, NEG_BIG)
        mx = jnp.max(v, axis=1, keepdims=True)
        idx = jnp.min(jnp.where(mask & (v == mx), lane_f, big), axis=1, keepdims=True)
        return mx, idx

    gmask = lane < N_GROUPS
    gmax, gidx = first_max(lg, gmask)
    gden = jnp.sum(jnp.where(gmask, jnp.exp(lg - gmax), 0.0), axis=1, keepdims=True)
    g_top_p = 1.0 / gden
    e_lane = lane - N_GROUPS
    e_group = _div_pow2(jnp.maximum(e_lane, 0), EXPERTS_PER_GROUP).astype(F32)
    emask = (e_lane >= 0) & (e_lane < N_EXPERTS) & (e_group == gidx)
    m1, i1 = first_max(lg, emask)
    eden = jnp.sum(jnp.where(emask, jnp.exp(lg - m1), 0.0), axis=1, keepdims=True)
    m2, i2 = first_max(lg, emask & (lane_f != i1))
    p1 = 1.0 / eden
    p2 = jnp.exp(m2 - m1) / eden
    tot = p1 + p2
    gate1 = g_top_p * (p1 / tot)
    gate2 = g_top_p * (p2 / tot)
    return jnp.where(lane == 0, gate1,
                     jnp.where(lane == 1, gate2,
                               jnp.where(lane == 2, i1 - N_GROUPS,
                                         jnp.where(lane == 3, i2 - N_GROUPS, 0.0))))


def _post(attn_o, gdn_o, x, wts):
    n = x.shape[0]
    tm = min(PROJ_TILE, n)
    tok = lambda w: pl.BlockSpec((tm, w), lambda i: (i, 0))
    const = lambda a: pl.BlockSpec(a.shape, lambda i: (0,) * a.ndim)
    consts = [wts['wo'], wts['ln1_g'], wts['ln1_b'], wts['wr']]
    return pl.pallas_call(
        functools.partial(_post_kernel, tm=tm),
        out_shape=[jax.ShapeDtypeStruct((n, D_MODEL), F32), jax.ShapeDtypeStruct((SUBLANES, n), F32)],
        grid=(n // tm,),
        in_specs=[tok(Q_COLS), tok(Z_COLS), tok(D_MODEL)] + [const(a) for a in consts],
        out_specs=[tok(D_MODEL), pl.BlockSpec((SUBLANES, tm), lambda i: (0, i))],
        compiler_params=_cparams(("arbitrary",)), name="post_%d" % (n // tm),
    )(attn_o, gdn_o, x, *consts)


def _slab_loop(n, body):
    n_main = jnp.right_shift(n, int(math.log2(SLAB_UNROLL)))

    def main(i, c):
        for u in range(SLAB_UNROLL):
            body(i * SLAB_UNROLL + u, u)
        return c

    lax.fori_loop(0, n_main, main, 0)
    lax.fori_loop(n_main * SLAB_UNROLL, n, lambda j, c: (body(j, 0), c)[1], 0)


def _dispatch_kernel(dst_ref, nslab_ref, ztab_ref, zinfo_ref, slot_ref, gate_ref, *rest,
                     group_tiles, max_tiles):
    x_refs = rest[:len(group_tiles)]
    xs_ref, pbuf, sem, zbuf, zsem = rest[len(group_tiles):]
    n_tiles = sum(group_tiles)
    g = pl.program_id(0)
    cur = lax.rem(g, 2)

    def slab_copy(tile, buf_slot, j):
        d = pl.multiple_of(dst_ref[tile * PERM_SLABS + j], SLAB)
        src = pbuf.at[buf_slot, pl.ds(pl.multiple_of(j * SLAB, SLAB), SLAB), :]
        return pltpu.make_async_copy(src, xs_ref.at[pl.ds(d, SLAB), :], sem.at[buf_slot])

    def tail_copy(k):
        d = pl.multiple_of(ztab_ref[k], SLAB)
        return pltpu.make_async_copy(zbuf.at[pl.ds(0, SLAB), :], xs_ref.at[pl.ds(d, SLAB), :], zsem)

    def tile_copy(t):
        d = pl.multiple_of(t * ROW_TILE, ROW_TILE)
        return pltpu.make_async_copy(zbuf, xs_ref.at[pl.ds(d, ROW_TILE), :], zsem)

    @pl.when(g == 0)
    def _():
        zbuf[...] = jnp.zeros_like(zbuf)

    share = -(-ZERO_TABLE // n_tiles)
    k0 = g * share
    _slab_loop(jnp.clip(zinfo_ref[0] - k0, 0, share), lambda j, u: tail_copy(k0 + j).start(priority=1))

    @pl.when(zinfo_ref[1] + g < max_tiles)
    def _():
        tile_copy(zinfo_ref[1] + g).start(priority=1)

    x = x_refs[-1][...]
    bound = n_tiles
    for x_ref, nt in zip(x_refs[-2::-1], group_tiles[:0:-1]):
        bound -= nt
        x = jnp.where(g < bound, x_ref[...], x)

    r = lax.broadcasted_iota(I32, (PERM_ROWS, TOK_TILE), 0)
    sl = slot_ref[0]
    hit0, hit1 = r == sl[0:1, :], r == sl[1:2, :]
    onehot = jnp.where(hit0 | hit1, 1.0, 0.0).astype(BF16)
    gt = gate_ref[0]
    gcol = jnp.sum(jnp.where(hit0, gt[0:1, :], 0.0) + jnp.where(hit1, gt[1:2, :], 0.0),
                   axis=1, keepdims=True)
    pbuf[cur, :, 0:D_MODEL] = jnp.dot(onehot, x.astype(BF16), preferred_element_type=F32).astype(BF16)
    g_hi = gcol.astype(BF16).astype(F32)
    lane = lax.broadcasted_iota(I32, (PERM_ROWS, LANES), 1)
    pbuf[cur, :, D_MODEL:] = jnp.where(lane < LANES // 2, g_hi, gcol - g_hi).astype(BF16)

    @pl.when(g > 0)
    def _():
        _slab_loop(nslab_ref[g - 1], lambda j, u: slab_copy(g - 1, 1 - cur, j).wait())

    _slab_loop(nslab_ref[g], lambda j, u: slab_copy(g, cur, j).start(priority=u % 2))

    @pl.when(g == n_tiles - 1)
    def _():
        _slab_loop(nslab_ref[g], lambda j, u: slab_copy(g, cur, j).wait())
        lax.fori_loop(zinfo_ref[1] + n_tiles, max_tiles,
                      lambda t, c: (tile_copy(t).start(priority=1), c)[1], 0)
        _slab_loop(zinfo_ref[0], lambda k, u: tail_copy(k).wait())
        lax.fori_loop(zinfo_ref[1], max_tiles, lambda t, c: (tile_copy(t).wait(), c)[1], 0)


def _dispatch(plan, x1s, max_tiles):
    group_tiles = tuple(x1.shape[0] // TOK_TILE for x1 in x1s)
    n_tiles = sum(group_tiles)
    tile = lambda i, d, ns, zt, zi: (i, 0, 0)
    in_specs = [pl.BlockSpec((1, TOP_K, TOK_TILE), tile), pl.BlockSpec((1, TOP_K, TOK_TILE), tile)]
    base = 0
    for nt in group_tiles:
        in_specs.append(pl.BlockSpec(
            (TOK_TILE, D_MODEL),
            lambda i, d, ns, zt, zi, base=base, nt=nt: (jnp.clip(i - base, 0, nt - 1), 0)))
        base += nt
    return pl.pallas_call(
        functools.partial(_dispatch_kernel, group_tiles=group_tiles, max_tiles=max_tiles),
        out_shape=jax.ShapeDtypeStruct((max_tiles * ROW_TILE, XS_WORDS), BF16),
        grid_spec=pltpu.PrefetchScalarGridSpec(
            num_scalar_prefetch=4, grid=(n_tiles,), in_specs=in_specs,
            out_specs=pl.BlockSpec(memory_space=pl.ANY),
            scratch_shapes=[pltpu.VMEM((2, PERM_ROWS, XS_WORDS), BF16), pltpu.SemaphoreType.DMA((2,)),
                            pltpu.VMEM((ROW_TILE, XS_WORDS), BF16), pltpu.SemaphoreType.DMA(())]),
        compiler_params=_cparams(("arbitrary",)), name="moe_dispatch",
    )(plan['slab_dst'], plan['nslab'], plan['ztab'], plan['zinfo'], plan['slot_rows'], plan['gate_rows'],
      *x1s)


def _expert_kernel(t0_ref, nt_ref, nu_ref, wg_ref, wu_ref, wd_ref, xs_ref, ye_ref,
                   xbuf, ybuf, wgu_scr, wd_scr, zbuf, in_sem, out_sem, zsem, *, max_tiles):
    e = pl.program_id(0)
    n_used = nu_ref[0]

    def in_copy(t):
        slot = lax.rem(t, 2)
        src = xs_ref.at[pl.ds(pl.multiple_of(t * ROW_TILE, ROW_TILE), ROW_TILE), :]
        return pltpu.make_async_copy(src, xbuf.at[slot], in_sem.at[slot])

    def out_copy(t):
        slot = lax.rem(t, 2)
        dst = ye_ref.at[pl.ds(pl.multiple_of(t * ROW_TILE, ROW_TILE), ROW_TILE), :]
        return pltpu.make_async_copy(ybuf.at[slot], dst, out_sem.at[slot])

    def zero_copy(t):
        dst = ye_ref.at[pl.ds(pl.multiple_of(t * ROW_TILE, ROW_TILE), ROW_TILE), :]
        return pltpu.make_async_copy(zbuf, dst, zsem)

    @pl.when(e == 0)
    def _():
        in_copy(0).start()

    wgu_scr[:, 0:EXPERT_FF] = wg_ref[0].astype(BF16)
    wgu_scr[:, EXPERT_FF:] = wu_ref[0].astype(BF16)
    wd_scr[...] = wd_ref[0].astype(BF16)

    sub = EXPERT_SUB
    rows = [slice(j * sub, (j + 1) * sub) for j in range(ROW_TILE // sub)]
    d = lambda a, b: jnp.dot(a, b, preferred_element_type=F32)

    def tile_body(j, carry):
        t = t0_ref[e] + j
        slot = lax.rem(t, 2)

        @pl.when(t + 1 < n_used)
        def _():
            in_copy(t + 1).start()

        in_copy(t).wait()

        @pl.when(t >= 2)
        def _():
            out_copy(t - 2).wait()

        wgu, wd = wgu_scr[...], wd_scr[...]
        xs = [xbuf[slot, r, 0:D_MODEL] for r in rows]
        hs = [d(x, wgu) for x in xs]
        hhs = [(_silu(h[:, :EXPERT_FF]) * h[:, EXPERT_FF:]).astype(BF16) for h in hs]
        ys = [d(hh, wd) for hh in hhs]
        for r, y in zip(rows, ys):
            parts = xbuf[slot, r, D_MODEL:].astype(F32)
            gate = parts + pltpu.roll(parts, LANES // 2, axis=1)
            ybuf[slot, r, :] = (y * jnp.concatenate([gate] * (D_MODEL // LANES), axis=1)).astype(BF16)
        out_copy(t).start()
        return carry

    lax.fori_loop(0, nt_ref[e], tile_body, 0)

    @pl.when(e == pl.num_programs(0) - 1)
    def _():
        @pl.when(n_used >= 2)
        def _():
            out_copy(n_used - 2).wait()

        out_copy(n_used - 1).wait()
        zbuf[...] = jnp.zeros_like(zbuf)
        lax.fori_loop(n_used, max_tiles, lambda t, c: (zero_copy(t).start(), c)[1], 0)
        lax.fori_loop(n_used, max_tiles, lambda t, c: (zero_copy(t).wait(), c)[1], 0)


def _experts(plan, xs, w_gate, w_up, w_down):
    max_tiles = xs.shape[0] // ROW_TILE
    wsel = lambda e, t0, nt, nu: (e, 0, 0)
    return pl.pallas_call(
        functools.partial(_expert_kernel, max_tiles=max_tiles),
        out_shape=jax.ShapeDtypeStruct((xs.shape[0], D_MODEL), BF16),
        grid_spec=pltpu.PrefetchScalarGridSpec(
            num_scalar_prefetch=3, grid=(N_EXPERTS,),
            in_specs=[pl.BlockSpec((1, D_MODEL, EXPERT_FF), wsel),
                      pl.BlockSpec((1, D_MODEL, EXPERT_FF), wsel),
                      pl.BlockSpec((1, EXPERT_FF, D_MODEL), wsel),
                      pl.BlockSpec(memory_space=pl.ANY)],
            out_specs=pl.BlockSpec(memory_space=pl.ANY),
            scratch_shapes=[pltpu.VMEM((2, ROW_TILE, XS_WORDS), BF16),
                            pltpu.VMEM((2, ROW_TILE, D_MODEL), BF16),
                            pltpu.VMEM((D_MODEL, 2 * EXPERT_FF), BF16),
                            pltpu.VMEM((EXPERT_FF, D_MODEL), BF16),
                            pltpu.VMEM((ROW_TILE, D_MODEL), BF16),
                            pltpu.SemaphoreType.DMA((2,)), pltpu.SemaphoreType.DMA((2,)),
                            pltpu.SemaphoreType.DMA(())]),
        compiler_params=_cparams(("arbitrary",)), name="moe_experts",
    )(plan['tile_start'], plan['tile_count'], plan['n_used'], w_gate, w_up, w_down, xs)


def _combine_kernel(dst_ref, nslab_ref, x1_ref, slot_ref, ye_ref, ln_g_ref, ln_b_ref, y_ref,
                    buf, sem, *, tile_base, n_tiles):
    i = pl.program_id(0)
    g = tile_base + i
    cur = lax.rem(i, 2)

    def slab_copy(tile, buf_slot, j):
        d = pl.multiple_of(dst_ref[tile * PERM_SLABS + j], SLAB)
        dst = buf.at[buf_slot, pl.ds(pl.multiple_of(j * SLAB, SLAB), SLAB), :]
        return pltpu.make_async_copy(ye_ref.at[pl.ds(d, SLAB), :], dst, sem.at[buf_slot])

    @pl.when(i == 0)
    def _():
        buf[...] = jnp.zeros_like(buf)
        _slab_loop(nslab_ref[g], lambda j, u: slab_copy(g, cur, j).start(priority=u % 2))

    @pl.when(i + 1 < n_tiles)
    def _():
        _slab_loop(nslab_ref[g + 1], lambda j, u: slab_copy(g + 1, 1 - cur, j).start(priority=u % 2))

    _slab_loop(nslab_ref[g], lambda j, u: slab_copy(g, cur, j).wait())

    col = lax.broadcasted_iota(I32, (TOK_TILE, PERM_ROWS), 1)
    sl = slot_ref[0]
    diag = (lax.broadcasted_iota(I32, (TOK_TILE, TOK_TILE), 0)
            == lax.broadcasted_iota(I32, (TOK_TILE, TOK_TILE), 1))
    as_col = lambda row: jnp.sum(jnp.where(diag, row, 0), axis=1, keepdims=True)
    pick = jnp.where((col == as_col(sl[0:1, :])) | (col == as_col(sl[1:2, :])), 1.0, 0.0).astype(BF16)
    moe = jnp.dot(pick, buf[cur], preferred_element_type=F32)
    y_ref[...] = _layer_norm(DEEPNORM_ALPHA * x1_ref[...] + moe, ln_g_ref[...], ln_b_ref[...])


def _combine(plan, tile_base, x1, ye, ln_g, ln_b):
    n = x1.shape[0]
    n_tiles = n // TOK_TILE
    tok = lambda w: pl.BlockSpec((TOK_TILE, w), lambda i, d, ns: (i, 0))
    const = lambda a: pl.BlockSpec(a.shape, lambda i, d, ns: (0,) * a.ndim)
    return pl.pallas_call(
        functools.partial(_combine_kernel, tile_base=tile_base, n_tiles=n_tiles),
        out_shape=jax.ShapeDtypeStruct((n, D_MODEL), F32),
        grid_spec=pltpu.PrefetchScalarGridSpec(
            num_scalar_prefetch=2, grid=(n_tiles,),
            in_specs=[tok(D_MODEL),
                      pl.BlockSpec((1, TOP_K, TOK_TILE), lambda i, d, ns: (tile_base + i, 0, 0)),
                      pl.BlockSpec(memory_space=pl.ANY), const(ln_g), const(ln_b)],
            out_specs=tok(D_MODEL),
            scratch_shapes=[pltpu.VMEM((2, PERM_ROWS, D_MODEL), BF16), pltpu.SemaphoreType.DMA((2,))]),
        compiler_params=_cparams(("arbitrary",)), name="moe_combine_%d" % tile_base,
    )(plan['slab_dst'], plan['nslab'], x1, plan['slot_rows'], ye, ln_g, ln_b)


def _routing_plan(ids, gates):
    nt = ids.shape[1] // TOK_TILE
    pairs = TOP_K * TOK_TILE
    ex = jnp.arange(N_EXPERTS, dtype=I32)
    per_tile = lambda a: jnp.swapaxes(a.reshape(TOP_K, nt, TOK_TILE), 0, 1)
    flat = per_tile(ids).reshape(nt, pairs)
    onehot = (flat[:, None, :] == ex[None, :, None])
    p = np.arange(pairs)
    triu = jnp.asarray(p[:, None] <= p[None, :], BF16)
    csum = jnp.dot(onehot.astype(BF16).reshape(nt * N_EXPERTS, pairs), triu,
                   preferred_element_type=F32).astype(I32).reshape(nt, N_EXPERTS, pairs)
    oh = onehot.astype(I32)
    rank = jnp.sum(oh * (csum - 1), axis=1)
    cnt = csum[:, :, -1]
    cpad = (cnt + SLAB - 1) // SLAB * SLAB
    seg_end = jnp.cumsum(cpad, axis=1)
    seg_off = seg_end - cpad
    slot = jnp.sum(oh * seg_off[:, :, None], axis=1) + rank
    run_end = jnp.cumsum(cpad, axis=0)
    ntiles_e = (run_end[-1] + ROW_TILE - 1) // ROW_TILE
    tile_end = jnp.cumsum(ntiles_e)
    dst_run = ((tile_end - ntiles_e) * ROW_TILE)[None, :] + run_end - cpad
    j8 = jnp.arange(PERM_SLABS, dtype=I32) * SLAB
    e_of = jnp.minimum(jnp.sum((j8[None, :, None] >= seg_end[:, None, :]).astype(I32), axis=2),
                       N_EXPERTS - 1)
    sel = (e_of[:, :, None] == ex).astype(I32)
    slab_dst = jnp.sum(sel * (dst_run - seg_off)[:, None, :], axis=2) + j8[None, :]
    n_used = tile_end[-1]
    row_start = (tile_end - ntiles_e) * ROW_TILE
    tail_cnt = (ntiles_e * ROW_TILE - run_end[-1]) // SLAB
    tail_end = jnp.cumsum(tail_cnt)
    k = jnp.arange(ZERO_TABLE, dtype=I32)
    e_k = jnp.minimum(jnp.sum((k[:, None] >= tail_end[None, :]).astype(I32), axis=1), N_EXPERTS - 1)
    base_k = jnp.sum((e_k[:, None] == ex).astype(I32)
                     * (row_start + run_end[-1] - SLAB * (tail_end - tail_cnt))[None, :], axis=1)
    return dict(
        slab_dst=slab_dst.reshape(-1).astype(I32), nslab=(seg_end[:, -1] // SLAB).astype(I32),
        ztab=(base_k + SLAB * k).astype(I32), zinfo=jnp.stack([tail_end[-1], n_used]).astype(I32),
        slot_rows=slot.reshape(nt, TOP_K, TOK_TILE).astype(I32),
        gate_rows=per_tile(gates).astype(F32),
        tile_start=(tile_end - ntiles_e).astype(I32), tile_count=ntiles_e.astype(I32),
        n_used=n_used.reshape(1).astype(I32))


def _max_row_tiles(n_tokens):
    rows = TOP_K * n_tokens + (n_tokens // TOK_TILE) * N_EXPERTS * (SLAB - 1)
    return (rows + ROW_TILE - 1) // ROW_TILE + N_EXPERTS


def _moe(x1s, routes, wts):
    ids = jnp.concatenate([r[TOP_K:2 * TOP_K, :] for r in routes], axis=1).astype(I32)
    gates = jnp.concatenate([r[0:TOP_K, :] for r in routes], axis=1)
    plan = _routing_plan(ids, gates)
    max_tiles = _max_row_tiles(ids.shape[1])
    bases = [0]
    for x1 in x1s[:-1]:
        bases.append(bases[-1] + x1.shape[0] // TOK_TILE)
    xs = _dispatch(plan, x1s, max_tiles)
    ye = _experts(plan, xs, wts['w_gate'], wts['w_up'], wts['w_down'])
    return [_combine(plan, base, x1, ye, wts['ln2_g'], wts['ln2_b']) for base, x1 in zip(bases, x1s)]


def _prep_weights(w_in, w_out, conv_w, a_log, dt_bias, gdn_norm_w, ln1_g, ln1_b, w_router_group,
                  w_router_expert, w_gate, w_up, w_down, ln2_g, ln2_b):
    pad_row = lambda v: jnp.pad(v.astype(F32), (0, LANES - v.shape[0]))[None, :]
    wr = jnp.pad(jnp.concatenate([w_router_group, w_router_expert], axis=1),
                 ((0, 0), (0, LANES - N_GROUPS - N_EXPERTS)))
    wr_hi = wr.astype(BF16)
    wr_mid = (wr - wr_hi.astype(F32)).astype(BF16)
    group = ATTN_HEADS // ATTN_KV_HEADS
    wq = w_in[:, :Q_COLS].reshape(D_MODEL, ATTN_KV_HEADS, group, HEAD_DIM)
    wq = jnp.swapaxes(wq, 1, 2).reshape(D_MODEL, Q_COLS)
    w_all = jnp.concatenate([wq, w_in[:, Q_COLS:]], axis=1)
    w_all = jnp.pad(w_all, ((0, 0), (0, IN_COLS_PAD - w_all.shape[1]))).astype(BF16)
    wo_q = w_out[:Q_COLS].reshape(ATTN_KV_HEADS, group, HEAD_DIM, D_MODEL)
    wo_q = jnp.swapaxes(wo_q, 0, 1).reshape(Q_COLS, D_MODEL)
    wo = jnp.concatenate([wo_q, w_out[Q_COLS:]], axis=0)
    return dict(
        w_all=w_all, convw=conv_w.astype(F32), alog=pad_row(a_log), dtb=pad_row(dt_bias),
        norm_w=gdn_norm_w.astype(F32)[None, :], wo=wo.astype(BF16),
        ln1_g=ln1_g[None, :], ln1_b=ln1_b[None, :], wr=jnp.concatenate([wr_hi, wr_mid], axis=1),
        w_gate=w_gate, w_up=w_up, w_down=w_down, ln2_g=ln2_g[None, :], ln2_b=ln2_b[None, :])


def _layer(x_prompt, x_sample, cache_k, cache_v, state_gdn, state_conv, wts):
    bp, sp, _ = x_prompt.shape
    bs, ts, _ = x_sample.shape
    n_p = bp * sp

    xp = x_prompt.reshape(n_p, D_MODEL)
    (q, k, v, qg, kg, vg, z, gcb, utail) = _proj(xp, np.arange(sp), wts, GDN_CHUNK, bp)
    attn_p = _attn_prompt(q, k, v, wts['sinks'], bp)
    gdn_p, s_p = _gdn_prompt(qg, kg, vg, z, gcb, wts['norm_w'], bp)
    last_win = lambda a: a.reshape(bp, sp, KV_COLS)[:, sp - WINDOW:].reshape(bp, WINDOW, ATTN_KV_HEADS,
                                                                            HEAD_DIM)
    new_k_p, new_v_p = last_win(k), last_win(v)
    tiles_per_seq = sp // min(PROJ_TILE, sp)
    conv_p = utail.reshape(bp, tiles_per_seq, SUBLANES, CONV_DIM)[:, -1, SUBLANES - (CONV_W - 1):]

    lo, hi = SAMPLE_FIRST, SAMPLE_FIRST + ts
    xs_rows = jnp.pad(x_sample, ((0, 0), (lo, SAMPLE_SLOTS - hi), (0, 0))).reshape(bs * SAMPLE_SLOTS, D_MODEL)
    hist = jnp.pad(state_conv, ((0, 0), (0, SAMPLE_SLOTS - lo), (0, 0))).reshape(bs * SAMPLE_SLOTS, CONV_DIM)
    slot = np.arange(SAMPLE_SLOTS)
    valid = jnp.asarray(np.tile((slot >= lo) & (slot < hi), bs)[:, None], F32)
    pos_s = np.tile(PAST_LEN + slot - lo, bs)
    (q, k, v, qg, kg, vg, z, gcb, u_s) = _proj(xs_rows, pos_s, wts, SAMPLE_SLOTS, 1, hist, valid)
    ck = cache_k.reshape(bs, WINDOW, KV_COLS)
    cv = cache_v.reshape(bs, WINDOW, KV_COLS)
    attn_s, kwin, vwin = _attn_sample(q, k, v, ck, cv, wts['sinks'], bs, ts)
    gdn_s, s_s = _gdn_sample(qg, kg, vg, z, gcb, wts['norm_w'], state_gdn)
    real = lambda a: a.reshape(bs, SAMPLE_SLOTS, -1)[:, lo:hi]
    new_k_s = kwin.reshape(bs, WINDOW, ATTN_KV_HEADS, HEAD_DIM)
    new_v_s = vwin.reshape(bs, WINDOW, ATTN_KV_HEADS, HEAD_DIM)
    conv_s = u_s.reshape(bs, SAMPLE_SLOTS, CONV_DIM)[:, hi - (CONV_W - 1):hi]

    x1_p, route_p = _post(attn_p, gdn_p, xp, wts)
    x1_s, route_s = _post(real(attn_s).reshape(bs * ts, Q_COLS), real(gdn_s).reshape(bs * ts, Z_COLS),
                          x_sample.reshape(bs * ts, D_MODEL), wts)
    y_p, y_s = _moe([x1_p, x1_s], [route_p, route_s], wts)
    return (y_p.reshape(bp, sp, D_MODEL), y_s.reshape(bs, ts, D_MODEL), new_k_p, new_v_p, s_p, conv_p,
            new_k_s, new_v_s, s_s, conv_s)


def kernel(x_prompt, x_sample, cache_attn_k, cache_attn_v, state_gdn, state_conv, w_in, w_out,
           attn_sinks, conv_w, a_log, dt_bias, gdn_norm_w, ln1_g, ln1_b, w_router_group,
           w_router_expert, w_gate, w_up, w_down, ln2_g, ln2_b):
    assert w_in.shape[0] == DEPTH
    l = 0
    wts = _prep_weights(w_in[l], w_out[l], conv_w[l], a_log[l], dt_bias[l], gdn_norm_w[l], ln1_g[l],
                        ln1_b[l], w_router_group[l], w_router_expert[l], w_gate[l], w_up[l],
                        w_down[l], ln2_g[l], ln2_b[l])
    wts['sinks'] = attn_sinks[l]
    outs = _layer(x_prompt, x_sample, cache_attn_k[l], cache_attn_v[l], state_gdn[l], state_conv[l], wts)
    (y_p, y_s, k_p, v_p, s_p, c_p, k_s, v_s, s_s, c_s) = outs
    add = lambda a: a[None]
    return (y_p, y_s, add(k_p), add(v_p), add(s_p), add(c_p), add(k_s), add(v_s), add(s_s), add(c_s))
```

```python
import functools
import math

import jax
import jax.numpy as jnp
import numpy as np
from jax import lax
from jax.experimental import pallas as pl
from jax.experimental.pallas import tpu as pltpu

F32 = jnp.float32
BF16 = jnp.bfloat16
I32 = jnp.int32

D_MODEL = 1024
ATTN_HEADS = 8
ATTN_KV_HEADS = 2
HEAD_DIM = 64
WINDOW = 128
ROT_DIM = HEAD_DIM // 4
ROPE_THETA = 500000.0
GDN_HEADS = 4
GDN_DK = 128
GDN_DV = 128
CONV_W = 4
QK_COLS = GDN_HEADS * GDN_DK
CONV_DIM = 2 * QK_COLS + GDN_HEADS * GDN_DV
Z_COLS = GDN_HEADS * GDN_DV
Q_COLS = ATTN_HEADS * HEAD_DIM
KV_COLS = ATTN_KV_HEADS * HEAD_DIM
N_GROUPS = 4
EXPERTS_PER_GROUP = 8
N_EXPERTS = N_GROUPS * EXPERTS_PER_GROUP
TOP_K = 2
EXPERT_FF = 256
NORM_EPS = 1e-5
L2_EPS = 1e-6
DEPTH = 1
DEEPNORM_ALPHA = (2 * DEPTH) ** 0.25
PAST_LEN = 8192

LANES = 128
SUBLANES = 8
IN_SPLITS = (0, Q_COLS + 2 * KV_COLS, Q_COLS + 2 * KV_COLS + CONV_DIM,
             Q_COLS + 2 * KV_COLS + CONV_DIM + Z_COLS, Q_COLS + 2 * KV_COLS + CONV_DIM + Z_COLS + LANES)
IN_COLS_PAD = IN_SPLITS[-1]
TOK_TILE = 512
PROJ_TILE = 512
PROJ_SUB = 128
POST_TILE = 1024
POST_SUB = 256
GDN_CHUNK = 128
GDN_SEQ_PER_STEP = 4
ATTN_BLOCKS_PER_STEP = 4
ATTN_SEQS_PER_STEP = 8
INV_BASE = 16
SAMPLE_SLOTS = 8
SAMPLE_FIRST = CONV_W - 1
ROW_TILE = 512
EXPERT_SUB = 256
SLAB_UNROLL = 4
SLAB = 16
PERM_ROWS = TOP_K * TOK_TILE + N_EXPERTS * SLAB
PERM_SLABS = PERM_ROWS // SLAB
XS_WORDS = D_MODEL + LANES
ZERO_TABLE = N_EXPERTS * (ROW_TILE // SLAB)
VMEM_LIMIT = 48 * 1024 * 1024
NEG_BIG = -1e30


def _cparams(sem):
    return pltpu.CompilerParams(dimension_semantics=sem, vmem_limit_bytes=VMEM_LIMIT)


def _bdot(a, b):
    return jnp.dot(a.astype(BF16), b.astype(BF16), preferred_element_type=F32)


def _bdot_nt(a, b):
    return lax.dot_general(a.astype(BF16), b.astype(BF16), (((1,), (1,)), ((), ())),
                           preferred_element_type=F32)


def _bdot_tn(a, b):
    return lax.dot_general(a.astype(BF16), b.astype(BF16), (((0,), (0,)), ((), ())),
                           preferred_element_type=F32)


def _div_pow2(x, n):
    return jnp.right_shift(x, int(math.log2(n)))


def _mod_pow2(x, n):
    return jnp.bitwise_and(x, n - 1)


def _split3(x):
    hi = x.astype(BF16)
    r = x - hi.astype(F32)
    mid = r.astype(BF16)
    lo = (r - mid.astype(F32)).astype(BF16)
    return hi, mid, lo


def _dot_exact_lhs01(m01, x):
    hi, mid, lo = _split3(x)
    d = lambda t: jnp.dot(m01, t, preferred_element_type=F32)
    return d(hi) + d(mid) + d(lo)


def _sigmoid(x):
    return 1.0 / (1.0 + jnp.exp(-x))


def _silu(x):
    return x * _sigmoid(x)


def _softplus(x):
    return jnp.maximum(x, 0.0) + jnp.log1p(jnp.exp(-jnp.abs(x)))


def _layer_norm(h, g, b):
    mu = jnp.mean(h, axis=-1, keepdims=True)
    d = h - mu
    var = jnp.mean(d * d, axis=-1, keepdims=True)
    return d * lax.rsqrt(var + NORM_EPS) * g + b


def _proj_kernel(*refs, tm, has_hist, full_u, one_segment):
    it = iter(refs)
    x_ref, cos_ref, sin_ref = next(it), next(it), next(it)
    w_ref = next(it)
    convw_ref, alog_ref, dtb_ref, tri_ref, seg_ref = next(it), next(it), next(it), next(it), next(it)
    hist_ref = valid_ref = None
    if has_hist:
        hist_ref, valid_ref = next(it), next(it)
    q_ref, k_ref, v_ref = next(it), next(it), next(it)
    qg_ref, kg_ref, vg_ref, z_ref, gcb_ref, u_ref = (next(it) for _ in range(6))
    ubuf = next(it)

    t = pl.program_id(1)
    sub = PROJ_SUB
    rows = [slice(j * sub, (j + 1) * sub) for j in range(tm // sub)]
    lane = lax.broadcasted_iota(I32, (sub, LANES), 1)
    first_half = _mod_pow2(lane, HEAD_DIM) < (ROT_DIM // 2)

    @pl.when(t == 0)
    def _():
        ubuf[0:SUBLANES, :] = jnp.zeros((SUBLANES, CONV_DIM), F32)

    @pl.when(t > 0)
    def _():
        ubuf[0:SUBLANES, :] = ubuf[tm:tm + SUBLANES, :]

    dots = []
    for r in rows:
        xb = x_ref[r, :].astype(BF16)
        dots.append([jnp.dot(xb, w_ref[:, lo:hi], preferred_element_type=F32)
                     for lo, hi in zip(IN_SPLITS[:-1], IN_SPLITS[1:])])

    def l2n(s):
        return s * lax.rsqrt(jnp.sum(s * s, axis=1, keepdims=True) + L2_EPS)

    for r, (pq, u, z, ab) in zip(rows, dots):
        cosv, sinv = cos_ref[r, :], sin_ref[r, :]

        def rope(s):
            sw = jnp.where(first_half, pltpu.roll(s, LANES - ROT_DIM // 2, axis=1),
                           pltpu.roll(s, ROT_DIM // 2, axis=1))
            return s * cosv + sw * sinv

        for j in range(Q_COLS // LANES):
            q_ref[r, j * LANES:(j + 1) * LANES] = rope(pq[:, j * LANES:(j + 1) * LANES])
        k_ref[r, :] = rope(pq[:, Q_COLS:Q_COLS + KV_COLS])
        v_ref[r, :] = pq[:, Q_COLS + KV_COLS:Q_COLS + 2 * KV_COLS]
        z_ref[r, :] = z

        if has_hist:
            u = u + hist_ref[r, :]
        if full_u:
            u_ref[r, :] = u
        elif r.stop == tm:
            u_ref[...] = u[sub - SUBLANES:, :]
        base = SUBLANES + r.start
        ubuf[base:base + sub, :] = u
        acc = u * convw_ref[CONV_W - 1:CONV_W, :]
        for j in range(1, CONV_W):
            acc = acc + ubuf[base - j:base - j + sub, :] * convw_ref[CONV_W - 1 - j:CONV_W - j, :]
        c = _silu(acc)
        if has_hist:
            c = c * valid_ref[r, :]
        for h in range(GDN_HEADS):
            sl = slice(h * GDN_DK, (h + 1) * GDN_DK)
            qg_ref[r, sl] = l2n(c[:, sl]) * (GDN_DK ** -0.5)
            kg_ref[r, sl] = l2n(c[:, QK_COLS + h * GDN_DK:QK_COLS + (h + 1) * GDN_DK])
        vg_ref[r, :] = c[:, 2 * QK_COLS:]

        g = -jnp.exp(alog_ref[...]) * _softplus(ab + dtb_ref[...])
        beta = _sigmoid(ab)
        if has_hist:
            g = g * valid_ref[r, :]
            beta = beta * valid_ref[r, :]
        g = jnp.where(lane < GDN_HEADS, g, 0.0)
        gc = _dot_exact_lhs01(tri_ref[...], g)
        if one_segment:
            gl = jnp.broadcast_to(gc[sub - 1:sub, :], (sub, LANES))
        else:
            gl = _dot_exact_lhs01(seg_ref[...], g)
        gcb_ref[r, :] = jnp.where(lane < GDN_HEADS, gc,
                                  jnp.where(lane < 2 * GDN_HEADS, beta,
                                            jnp.where(lane < 3 * GDN_HEADS,
                                                      pltpu.roll(gl, 2 * GDN_HEADS, axis=1), 0.0)))


def _rope_tables(pos):
    half = ROT_DIM // 2
    pos = np.asarray(pos, np.float64)
    inv_freq = ROPE_THETA ** (-np.arange(half, dtype=np.float64) * 2.0 / ROT_DIM)
    ang = pos[:, None] * inv_freq[None, :]
    cos, sin = np.cos(ang), np.sin(ang)
    p = pos.shape[0]
    cpat = np.concatenate([cos, cos, np.ones((p, HEAD_DIM - ROT_DIM))], axis=1)
    spat = np.concatenate([-sin, sin, np.zeros((p, HEAD_DIM - ROT_DIM))], axis=1)
    rep = (1, LANES // HEAD_DIM)
    return jnp.asarray(np.tile(cpat, rep), F32), jnp.asarray(np.tile(spat, rep), F32)


def _segment_matrices(tm, seg_len):
    i = np.arange(tm)
    same = (i[:, None] // seg_len) == (i[None, :] // seg_len)
    tri = same & (i[None, :] <= i[:, None])
    return jnp.asarray(tri, BF16), jnp.asarray(same, BF16)


def _proj(x, pos, wts, seg_len, n_seq, hist=None, valid=None):
    n = x.shape[0]
    rows = n // n_seq
    tm = min(PROJ_TILE, rows)
    nt = rows // tm
    has_hist = hist is not None
    cos_t, sin_t = _rope_tables(pos)
    tri, seg = _segment_matrices(PROJ_SUB, seg_len)

    tok = lambda w: pl.BlockSpec((tm, w), lambda b, t: (b * nt + t, 0))
    const = lambda a: pl.BlockSpec(a.shape, lambda b, t: (0,) * a.ndim)
    in_arrays = [x, cos_t, sin_t, wts['w_all'], wts['convw'], wts['alog'], wts['dtb'], tri, seg]
    in_specs = [tok(D_MODEL), pl.BlockSpec((tm, LANES), lambda b, t: (t, 0)),
                pl.BlockSpec((tm, LANES), lambda b, t: (t, 0))] + [const(a) for a in in_arrays[3:]]
    if has_hist:
        in_arrays += [hist, valid]
        in_specs += [tok(CONV_DIM), tok(1)]
    u_rows = n if has_hist else (n // tm) * SUBLANES
    u_block = tm if has_hist else SUBLANES
    out_shape = [jax.ShapeDtypeStruct((n, Q_COLS), F32), jax.ShapeDtypeStruct((n, KV_COLS), F32),
                 jax.ShapeDtypeStruct((n, KV_COLS), F32), jax.ShapeDtypeStruct((n, QK_COLS), F32),
                 jax.ShapeDtypeStruct((n, QK_COLS), F32), jax.ShapeDtypeStruct((n, Z_COLS), F32),
                 jax.ShapeDtypeStruct((n, Z_COLS), F32), jax.ShapeDtypeStruct((n, LANES), F32),
                 jax.ShapeDtypeStruct((u_rows, CONV_DIM), F32)]
    out_specs = [tok(Q_COLS), tok(KV_COLS), tok(KV_COLS), tok(QK_COLS), tok(QK_COLS), tok(Z_COLS),
                 tok(Z_COLS), tok(LANES),
                 pl.BlockSpec((u_block, CONV_DIM), lambda b, t: (b * nt + t, 0))]
    return pl.pallas_call(
        functools.partial(_proj_kernel, tm=tm, has_hist=has_hist, full_u=has_hist,
                          one_segment=seg_len == PROJ_SUB),
        out_shape=out_shape, grid=(n_seq, nt), in_specs=in_specs, out_specs=out_specs,
        scratch_shapes=[pltpu.VMEM((tm + SUBLANES, CONV_DIM), F32)],
        compiler_params=_cparams(("arbitrary", "arbitrary")),
        name="proj_hist" if has_hist else "proj",
    )(*in_arrays)


def _attn_blocks(qs, kcats, vcats, biases, sink, tq):
    lane = lax.broadcasted_iota(I32, (tq, LANES), 1)
    low = lane < HEAD_DIM
    n_slab = Q_COLS // LANES

    def stack(q):
        slabs = [q[:, j * LANES:(j + 1) * LANES] * (HEAD_DIM ** -0.5) for j in range(n_slab)]
        parts = ([jnp.where(low, s, 0.0) for s in slabs] + [jnp.where(low, 0.0, s) for s in slabs])
        return jnp.concatenate(parts, axis=0).astype(BF16)

    def unstack(o8):
        return [jnp.where(low, o8[j * tq:(j + 1) * tq, :], o8[(n_slab + j) * tq:(n_slab + j + 1) * tq, :])
                for j in range(n_slab)]

    rows = ATTN_HEADS * tq
    half = rows // 2
    klow = lax.broadcasted_iota(I32, (2 * WINDOW, LANES), 1) < HEAD_DIM
    one = jnp.ones((), BF16)
    q8s = _each(stack, qs)
    ss = _each(lambda q8, kc, b: _bdot_nt(q8, kc) + b, q8s, kcats, biases)
    ms = _each(lambda s: jnp.maximum(jnp.broadcast_to(jnp.max(s, axis=1, keepdims=True), (rows, LANES)),
                                     sink), ss)
    ps = _each(lambda s, m: jnp.exp(s - jnp.concatenate([m, m], axis=1)).astype(BF16), ss, ms)
    pv0 = _each(lambda p, vc: jnp.dot(p[:half], jnp.where(klow, vc, one), preferred_element_type=F32),
                ps, vcats)
    pv1 = _each(lambda p, vc: jnp.dot(p[half:], jnp.where(klow, one, vc), preferred_element_type=F32),
                ps, vcats)
    pvs = _each(lambda a, b: jnp.concatenate([a, b], axis=0), pv0, pv1)
    o8s = _each(lambda pv, m: pv / (pltpu.roll(pv, HEAD_DIM, axis=1) + jnp.exp(sink - m)), pvs, ms)
    return _each(unstack, o8s)


def _attn_prompt_kernel(q_ref, kc_ref, vc_ref, kp_ref, vp_ref, bias0_ref, bias_ref, sink_ref, o_ref, *,
                        nblk):
    kall = jnp.concatenate([kp_ref[...], kc_ref[...]], axis=0).astype(BF16)
    vall = jnp.concatenate([vp_ref[...], vc_ref[...]], axis=0).astype(BF16)
    win = lambda a, j: a[j * WINDOW:(j + 2) * WINDOW, :]
    qs = [q_ref[j * WINDOW:(j + 1) * WINDOW, :] for j in range(nblk)]
    biases = [bias0_ref[0]] + [bias_ref[...]] * (nblk - 1)
    outs = _attn_blocks(qs, [win(kall, j) for j in range(nblk)], [win(vall, j) for j in range(nblk)],
                        biases, sink_ref[...], WINDOW)
    for j, slabs in enumerate(outs):
        for c, slab in enumerate(slabs):
            o_ref[j * WINDOW:(j + 1) * WINDOW, c * LANES:(c + 1) * LANES] = slab


def _attn_sample_kernel(q_ref, kc_ref, vc_ref, kp_ref, vp_ref, bias_ref, sink_ref, o_ref, kw_ref, vw_ref,
                        *, nseq, n_new):
    tq = SAMPLE_SLOTS
    zpad = jnp.zeros((WINDOW - tq, LANES), F32)
    rows = lambda ref, j: ref[j * tq:(j + 1) * tq, :]
    cat = lambda pref, cref, j: jnp.concatenate([pref[j], rows(cref, j), zpad], axis=0).astype(BF16)
    outs = _attn_blocks([rows(q_ref, j) for j in range(nseq)],
                        [cat(kp_ref, kc_ref, j) for j in range(nseq)],
                        [cat(vp_ref, vc_ref, j) for j in range(nseq)],
                        [bias_ref[...]] * nseq, sink_ref[...], tq)
    for j, slabs in enumerate(outs):
        for c, slab in enumerate(slabs):
            o_ref[j * tq:(j + 1) * tq, c * LANES:(c + 1) * LANES] = slab
    row = lax.broadcasted_iota(I32, (WINDOW, LANES), 0)
    keep = WINDOW - n_new
    for pref, cref, wref in ((kp_ref, kc_ref, kw_ref), (vp_ref, vc_ref, vw_ref)):
        for j in range(nseq):
            new = jnp.concatenate([rows(cref, j), zpad], axis=0)
            wref[j] = jnp.where(row < keep, pltpu.roll(pref[j], keep, axis=0),
                                pltpu.roll(new, keep - SAMPLE_FIRST, axis=0))


def _sink_rows(sinks, tq):
    return jnp.broadcast_to(jnp.repeat(sinks.astype(F32), tq)[:, None], (ATTN_HEADS * tq, LANES))


def _attn_bias(tq, q_off, k_lo, k_hi, has_prev):
    qi = (np.arange(ATTN_HEADS * tq) % tq)[:, None]
    c = np.arange(2 * WINDOW)[None, :]
    cj = c - WINDOW
    vis_prev = (c < WINDOW) & (c > qi - q_off) & has_prev
    vis_cur = (c >= WINDOW) & (cj <= qi) & (cj >= k_lo) & (cj <= k_hi)
    return np.where(vis_prev | vis_cur, 0.0, NEG_BIG).astype(np.float32)


def _attn_prompt(q, k, v, sinks, n_seq):
    n = q.shape[0]
    nb = n // n_seq // WINDOW
    nblk = min(ATTN_BLOCKS_PER_STEP, nb)
    steps = nb // nblk
    tq = nblk * WINDOW
    cur = lambda w: pl.BlockSpec((tq, w), lambda b, i: (b * steps + i, 0))
    prev = pl.BlockSpec((WINDOW, LANES), lambda b, i: (b * nb + jnp.maximum(i * nblk - 1, 0), 0))
    bias2 = jnp.asarray(np.stack([_attn_bias(WINDOW, 0, 0, WINDOW - 1, False),
                                  _attn_bias(WINDOW, 0, 0, WINDOW - 1, True)]))
    rows = ATTN_HEADS * WINDOW
    return pl.pallas_call(
        functools.partial(_attn_prompt_kernel, nblk=nblk),
        out_shape=jax.ShapeDtypeStruct((n, Q_COLS), F32), grid=(n_seq, steps),
        in_specs=[cur(Q_COLS), cur(LANES), cur(LANES), prev, prev,
                  pl.BlockSpec((1, rows, 2 * WINDOW), lambda b, i: (jnp.minimum(i, 1), 0, 0)),
                  pl.BlockSpec((rows, 2 * WINDOW), lambda b, i: (0, 0)),
                  pl.BlockSpec((rows, LANES), lambda b, i: (0, 0))],
        out_specs=cur(Q_COLS),
        compiler_params=_cparams(("arbitrary", "arbitrary")), name="attn_prompt",
    )(q, k, v, k, v, bias2, bias2[1], _sink_rows(sinks, WINDOW))


def _attn_sample(q, k, v, cache_k, cache_v, sinks, n_seq, n_new):
    tq = SAMPLE_SLOTS
    nseq = min(ATTN_SEQS_PER_STEP, n_seq)
    cur = lambda w: pl.BlockSpec((nseq * tq, w), lambda b: (b, 0))
    prev = pl.BlockSpec((nseq, WINDOW, LANES), lambda b: (b, 0, 0))
    bias = jnp.asarray(_attn_bias(tq, SAMPLE_FIRST, SAMPLE_FIRST, SAMPLE_FIRST + 3, True))
    win = jax.ShapeDtypeStruct((n_seq, WINDOW, LANES), F32)
    return pl.pallas_call(
        functools.partial(_attn_sample_kernel, nseq=nseq, n_new=n_new),
        out_shape=[jax.ShapeDtypeStruct((n_seq * tq, Q_COLS), F32), win, win], grid=(n_seq // nseq,),
        in_specs=[cur(Q_COLS), cur(LANES), cur(LANES), prev, prev,
                  pl.BlockSpec(bias.shape, lambda b: (0, 0)),
                  pl.BlockSpec((ATTN_HEADS * tq, LANES), lambda b: (0, 0))],
        out_specs=[cur(Q_COLS), prev, prev],
        compiler_params=_cparams(("arbitrary",)), name="attn_sample",
    )(q, k, v, cache_k, cache_v, bias, _sink_rows(sinks, tq))


def _each(f, *lists):
    return [f(*args) for args in zip(*lists)]


def _unit_lower_inverse(ms, eye, same_base, base_only=False):
    c = ms[0].shape[0]

    def neumann(q0s, n_factors):
        xs = _each(lambda q: eye + q, q0s)
        if n_factors == 1:
            return xs
        qs = _each(_bdot, q0s, q0s)
        for _ in range(n_factors - 2):
            prods = _each(lambda x, q: _bdot(jnp.concatenate([x, q], axis=0), q), xs, qs)
            xs = _each(lambda x, pr: x + pr[:c], xs, prods)
            qs = _each(lambda pr: pr[c:], prods)
        return _each(lambda x, q: x + _bdot(x, q), xs, qs)

    ds = _each(lambda m: jnp.where(same_base, m, 0.0), ms)
    xs = neumann(_each(lambda d: -d, ds), int(math.log2(INV_BASE)))
    nblk = c // INV_BASE
    if nblk == 1 or base_only:
        return xs
    ls = _each(lambda m, d: m - d, ms, ds)
    ns = _each(lambda x, l: -_bdot(x, l), xs, ls)
    ys = neumann(ns, int(math.log2(nblk)))
    return _each(_bdot, ys, xs)


def _gdn_intra(qs, ks, vs, gcs, gls, betas, same_seq, low_incl, low_strict, eye, same_base,
               base_only=False):
    del same_seq
    e_gcs = _each(jnp.exp, gcs)

    def decay_of(gc):
        gc_row = jnp.sum(jnp.where(eye > 0, gc, 0.0), axis=0, keepdims=True)
        return jnp.where(low_incl, jnp.exp(jnp.where(low_incl, gc - gc_row, 0.0)), 0.0)

    c = qs[0].shape[0]
    decays = _each(decay_of, gcs)
    kbs = _each(lambda k, b: k * b, ks, betas)
    vbs = _each(lambda v, b: v * b, vs, betas)
    kqs = _each(lambda kb, q, k: _bdot_nt(jnp.concatenate([kb, q], axis=0), k), kbs, qs, ks)
    ms = _each(lambda kq, d: jnp.where(low_strict, kq[:c] * d, 0.0), kqs, decays)
    attns = _each(lambda kq, d: jnp.where(low_incl, kq[c:] * d, 0.0), kqs, decays)
    tmats = _unit_lower_inverse(ms, eye, same_base, base_only)
    uws = _each(lambda t, vb, kb, e: _bdot(t, jnp.concatenate([vb, kb * e], axis=1)),
                tmats, vbs, kbs, e_gcs)
    us = _each(lambda uw: uw[:, :GDN_DV], uws)
    ws = _each(lambda uw: uw[:, GDN_DV:], uws)
    q_decs = _each(lambda q, e: q * e, qs, e_gcs)
    k_decs = _each(lambda k, gl, gc: k * jnp.exp(gl - gc), ks, gls, gcs)
    return us, ws, attns, q_decs, k_decs


def _chunk_masks(c, seq_len):
    i = lax.broadcasted_iota(I32, (c, c), 0)
    j = lax.broadcasted_iota(I32, (c, c), 1)
    same_seq = _div_pow2(i, seq_len) == _div_pow2(j, seq_len)
    low_incl = same_seq & (i >= j)
    low_strict = same_seq & (i > j)
    eye = (i == j).astype(F32)
    same_base = _div_pow2(i, INV_BASE) == _div_pow2(j, INV_BASE)
    return same_seq, low_incl, low_strict, eye, same_base


def _gated_rms(o, z, nw):
    o = o * lax.rsqrt(jnp.mean(o * o, axis=1, keepdims=True) + NORM_EPS) * nw
    return o * _silu(z)


def _gdn_prompt_kernel(qg_ref, kg_ref, vg_ref, z_ref, gcb_ref, nw_ref, o_ref, s_out_ref, s_scr):
    c = GDN_CHUNK
    n = pl.program_id(1)

    @pl.when(n == 0)
    def _():
        s_scr[...] = jnp.zeros_like(s_scr)

    masks = _chunk_masks(c, c)
    nw = nw_ref[...]
    chains = [(b, h) for b in range(qg_ref.shape[0]) for h in range(GDN_HEADS)]
    hs = lambda h: slice(h * GDN_DK, (h + 1) * GDN_DK)
    col = lambda off: [gcb_ref[b, :, off + h:off + h + 1] for b, h in chains]
    gcs, betas, gls = col(0), col(GDN_HEADS), col(2 * GDN_HEADS)
    qs = [qg_ref[b, :, hs(h)] for b, h in chains]
    ks = [kg_ref[b, :, hs(h)] for b, h in chains]
    vs = [vg_ref[b, :, hs(h)] for b, h in chains]
    us, ws, attns, q_decs, k_decs = _gdn_intra(qs, ks, vs, gcs, gls, betas, *masks)
    ss = [s_scr[b, h] for b, h in chains]
    wqs = _each(lambda w, qd, s: _bdot(jnp.concatenate([w, qd], axis=0), s), ws, q_decs, ss)
    wss = _each(lambda wq: wq[:c], wqs)
    qss = _each(lambda wq: wq[c:], wqs)
    v_news = _each(lambda u, x: u - x, us, wss)
    avs = _each(_bdot, attns, v_news)
    kvs = _each(_bdot_tn, k_decs, v_news)
    for (b, h), s, gl, qsv, av, kv in zip(chains, ss, gls, qss, avs, kvs):
        s_scr[b, h] = s * jnp.exp(gl[0:1, :]) + kv
        o_ref[b, :, hs(h)] = _gated_rms(qsv + av, z_ref[b, :, hs(h)], nw)

    @pl.when(n == pl.num_programs(1) - 1)
    def _():
        s_out_ref[...] = s_scr[...]


def _gdn_prompt(qg, kg, vg, z, gcb, norm_w, n_seq):
    n = qg.shape[0]
    s_len = n // n_seq
    nb = min(GDN_SEQ_PER_STEP, n_seq)
    v3 = lambda a: a.reshape(n_seq, s_len, a.shape[-1])
    tok = lambda w: pl.BlockSpec((nb, GDN_CHUNK, w), lambda b, i: (b, i, 0))
    o, s = pl.pallas_call(
        _gdn_prompt_kernel,
        out_shape=[jax.ShapeDtypeStruct((n_seq, s_len, Z_COLS), F32),
                   jax.ShapeDtypeStruct((n_seq, GDN_HEADS, GDN_DK, GDN_DV), F32)],
        grid=(n_seq // nb, s_len // GDN_CHUNK),
        in_specs=[tok(QK_COLS), tok(QK_COLS), tok(Z_COLS), tok(Z_COLS), tok(LANES),
                  pl.BlockSpec((1, GDN_DV), lambda b, i: (0, 0))],
        out_specs=[tok(Z_COLS),
                   pl.BlockSpec((nb, GDN_HEADS, GDN_DK, GDN_DV), lambda b, i: (b, 0, 0, 0))],
        scratch_shapes=[pltpu.VMEM((nb, GDN_HEADS, GDN_DK, GDN_DV), F32)],
        compiler_params=_cparams(("arbitrary", "arbitrary")), name="gdn_prompt",
    )(v3(qg), v3(kg), v3(vg), v3(z), v3(gcb), norm_w)
    return o.reshape(n, Z_COLS), s


def _gdn_sample_kernel(qg_ref, kg_ref, vg_ref, z_ref, gcb_ref, nw_ref, s_in_ref, o_ref, s_out_ref):
    c = GDN_CHUNK
    n_sub = c // SAMPLE_SLOTS
    masks = _chunk_masks(c, SAMPLE_SLOTS)
    heads = range(GDN_HEADS)
    hs = lambda h: slice(h * GDN_DK, (h + 1) * GDN_DK)
    rs = lambda s: slice(s * SAMPLE_SLOTS, (s + 1) * SAMPLE_SLOTS)
    col = lambda off: [gcb_ref[:, off + h:off + h + 1] for h in heads]
    gcs, betas, gls = col(0), col(GDN_HEADS), col(2 * GDN_HEADS)
    us, ws, attns, q_decs, k_decs = _gdn_intra([qg_ref[:, hs(h)] for h in heads],
                                               [kg_ref[:, hs(h)] for h in heads],
                                               [vg_ref[:, hs(h)] for h in heads], gcs, gls, betas, *masks,
                                               base_only=SAMPLE_SLOTS <= INV_BASE)
    pairs = [(h, s) for h in heads for s in range(n_sub)]
    sts = [s_in_ref[s, h] for h, s in pairs]
    boths = [jnp.concatenate([ws[h][rs(s), :], q_decs[h][rs(s), :]], axis=0) for h, s in pairs]
    rr = _each(_bdot, boths, sts)
    gather = lambda h, part: jnp.concatenate(
        [rr[h * n_sub + s][part * SAMPLE_SLOTS:(part + 1) * SAMPLE_SLOTS, :] for s in range(n_sub)], axis=0)
    v_news = [us[h] - gather(h, 0) for h in heads]
    avs = _each(_bdot, attns, v_news)
    row = lax.broadcasted_iota(I32, (c, LANES), 0)
    seq_of_row = _div_pow2(row, SAMPLE_SLOTS)
    kds = [jnp.where(seq_of_row == s, k_decs[h], 0.0) for h, s in pairs]
    kvs = _each(_bdot_tn, kds, [v_news[h] for h, _ in pairs])
    egls = _each(jnp.exp, gls)
    for (h, s), st, kv in zip(pairs, sts, kvs):
        s_out_ref[s, h] = st * egls[h][s * SAMPLE_SLOTS:s * SAMPLE_SLOTS + 1, :] + kv
    nw = nw_ref[...]
    for h in heads:
        o_ref[:, hs(h)] = _gated_rms(gather(h, 1) + avs[h], z_ref[:, hs(h)], nw)


def _gdn_sample(qg, kg, vg, z, gcb, norm_w, state):
    n = qg.shape[0]
    n_sub = GDN_CHUNK // SAMPLE_SLOTS
    tok = lambda w: pl.BlockSpec((GDN_CHUNK, w), lambda i: (i, 0))
    st = pl.BlockSpec((n_sub, GDN_HEADS, GDN_DK, GDN_DV), lambda i: (i, 0, 0, 0))
    return pl.pallas_call(
        _gdn_sample_kernel,
        out_shape=[jax.ShapeDtypeStruct((n, Z_COLS), F32),
                   jax.ShapeDtypeStruct(state.shape, F32)],
        grid=(n // GDN_CHUNK,),
        in_specs=[tok(QK_COLS), tok(QK_COLS), tok(Z_COLS), tok(Z_COLS), tok(LANES),
                  pl.BlockSpec((1, GDN_DV), lambda i: (0, 0)), st],
        out_specs=[tok(Z_COLS), st],
        compiler_params=_cparams(("arbitrary",)), name="gdn_sample",
    )(qg, kg, vg, z, gcb, norm_w, state)


def _post_kernel(a_ref, g_ref, x_ref, wo_ref, ln_g_ref, ln_b_ref, wr_ref, x1_ref, route_ref, *, tm):
    sub = POST_SUB
    rows = [slice(j * sub, (j + 1) * sub) for j in range(tm // sub)]
    d = lambda a, b: jnp.dot(a, b, preferred_element_type=F32)
    mixes = [d(a_ref[r, :].astype(BF16), wo_ref[0:Q_COLS, :]) + d(g_ref[r, :].astype(BF16), wo_ref[Q_COLS:, :])
             for r in rows]
    x1s = [_layer_norm(DEEPNORM_ALPHA * x_ref[r, :] + mix, ln_g_ref[...], ln_b_ref[...])
           for r, mix in zip(rows, mixes)]
    for r, x1 in zip(rows, x1s):
        x1_ref[r, :] = x1
    w2 = wr_ref[...]
    lgs = []
    for x1 in x1s:
        xh = x1.astype(BF16)
        xm = (x1 - xh.astype(F32)).astype(BF16)
        both = d(xh, w2)
        lgs.append(both[:, :LANES] + both[:, LANES:] + d(xm, w2[:, :LANES]))
    for r, lg in zip(rows, lgs):
        route_ref[:, r] = jnp.transpose(_route(lg))[0:SUBLANES, :]


def _route(lg):
    lane = lax.broadcasted_iota(I32, lg.shape, 1)
    lane_f = lane.astype(F32)
    big = float(LANES)

    def first_max(vals, mask):
        v = jnp.where(mask, vals, NEG_BIG)
        mx = jnp.max(v, axis=1, keepdims=True)
        idx = jnp.min(jnp.where(mask & (v == mx), lane_f, big), axis=1, keepdims=True)
        return mx, idx

    gmask = lane < N_GROUPS
    gmax, gidx = first_max(lg, gmask)
    gden = jnp.sum(jnp.where(gmask, jnp.exp(lg - gmax), 0.0), axis=1, keepdims=True)
    g_top_p = 1.0 / gden
    e_lane = lane - N_GROUPS
    e_group = _div_pow2(jnp.maximum(e_lane, 0), EXPERTS_PER_GROUP).astype(F32)
    emask = (e_lane >= 0) & (e_lane < N_EXPERTS) & (e_group == gidx)
    m1, i1 = first_max(lg, emask)
    eden = jnp.sum(jnp.where(emask, jnp.exp(lg - m1), 0.0), axis=1, keepdims=True)
    m2, i2 = first_max(lg, emask & (lane_f != i1))
    p1 = 1.0 / eden
    p2 = jnp.exp(m2 - m1) / eden
    tot = p1 + p2
    gate1 = g_top_p * (p1 / tot)
    gate2 = g_top_p * (p2 / tot)
    return jnp.where(lane == 0, gate1,
                     jnp.where(lane == 1, gate2,
                               jnp.where(lane == 2, i1 - N_GROUPS,
                                         jnp.where(lane == 3, i2 - N_GROUPS, 0.0))))


def _post(attn_o, gdn_o, x, wts):
    n = x.shape[0]
    tm = min(POST_TILE, n)
    tok = lambda w: pl.BlockSpec((tm, w), lambda i: (i, 0))
    const = lambda a: pl.BlockSpec(a.shape, lambda i: (0,) * a.ndim)
    consts = [wts['wo'], wts['ln1_g'], wts['ln1_b'], wts['wr']]
    return pl.pallas_call(
        functools.partial(_post_kernel, tm=tm),
        out_shape=[jax.ShapeDtypeStruct((n, D_MODEL), F32), jax.ShapeDtypeStruct((SUBLANES, n), F32)],
        grid=(n // tm,),
        in_specs=[tok(Q_COLS), tok(Z_COLS), tok(D_MODEL)] + [const(a) for a in consts],
        out_specs=[tok(D_MODEL), pl.BlockSpec((SUBLANES, tm), lambda i: (0, i))],
        compiler_params=_cparams(("arbitrary",)), name="post_%d" % (n // tm),
    )(attn_o, gdn_o, x, *consts)


def _slab_loop(n, body):
    n_main = jnp.right_shift(n, int(math.log2(SLAB_UNROLL)))

    def main(i, c):
        for u in range(SLAB_UNROLL):
            body(i * SLAB_UNROLL + u, u)
        return c

    lax.fori_loop(0, n_main, main, 0)
    lax.fori_loop(n_main * SLAB_UNROLL, n, lambda j, c: (body(j, 0), c)[1], 0)


def _dispatch_kernel(dst_ref, nslab_ref, ztab_ref, zinfo_ref, slot_ref, gate_ref, *rest,
                     group_tiles, max_tiles):
    x_refs = rest[:len(group_tiles)]
    xs_ref, pbuf, sem, zbuf, zsem = rest[len(group_tiles):]
    n_tiles = sum(group_tiles)
    g = pl.program_id(0)
    cur = lax.rem(g, 2)

    def slab_copy(tile, buf_slot, j):
        d = pl.multiple_of(dst_ref[tile * PERM_SLABS + j], SLAB)
        src = pbuf.at[buf_slot, pl.ds(pl.multiple_of(j * SLAB, SLAB), SLAB), :]
        return pltpu.make_async_copy(src, xs_ref.at[pl.ds(d, SLAB), :], sem.at[buf_slot])

    def tail_copy(k):
        d = pl.multiple_of(ztab_ref[k], SLAB)
        return pltpu.make_async_copy(zbuf.at[pl.ds(0, SLAB), :], xs_ref.at[pl.ds(d, SLAB), :], zsem)

    def tile_copy(t):
        d = pl.multiple_of(t * ROW_TILE, ROW_TILE)
        return pltpu.make_async_copy(zbuf, xs_ref.at[pl.ds(d, ROW_TILE), :], zsem)

    @pl.when(g == 0)
    def _():
        zbuf[...] = jnp.zeros_like(zbuf)

    share = -(-ZERO_TABLE // n_tiles)
    k0 = g * share
    _slab_loop(jnp.clip(zinfo_ref[0] - k0, 0, share), lambda j, u: tail_copy(k0 + j).start(priority=1))

    @pl.when(zinfo_ref[1] + g < max_tiles)
    def _():
        tile_copy(zinfo_ref[1] + g).start(priority=1)

    x = x_refs[-1][...]
    bound = n_tiles
    for x_ref, nt in zip(x_refs[-2::-1], group_tiles[:0:-1]):
        bound -= nt
        x = jnp.where(g < bound, x_ref[...], x)

    r = lax.broadcasted_iota(I32, (PERM_ROWS, TOK_TILE), 0)
    sl = slot_ref[0]
    hit0, hit1 = r == sl[0:1, :], r == sl[1:2, :]
    onehot = jnp.where(hit0 | hit1, 1.0, 0.0).astype(BF16)
    gt = gate_ref[0]
    gcol = jnp.sum(jnp.where(hit0, gt[0:1, :], 0.0) + jnp.where(hit1, gt[1:2, :], 0.0),
                   axis=1, keepdims=True)
    pbuf[cur, :, 0:D_MODEL] = jnp.dot(onehot, x.astype(BF16), preferred_element_type=F32).astype(BF16)
    g_hi = gcol.astype(BF16).astype(F32)
    lane = lax.broadcasted_iota(I32, (PERM_ROWS, LANES), 1)
    pbuf[cur, :, D_MODEL:] = jnp.where(lane < LANES // 2, g_hi, gcol - g_hi).astype(BF16)

    @pl.when(g > 0)
    def _():
        _slab_loop(nslab_ref[g - 1], lambda j, u: slab_copy(g - 1, 1 - cur, j).wait())

    _slab_loop(nslab_ref[g], lambda j, u: slab_copy(g, cur, j).start(priority=u % 2))

    @pl.when(g == n_tiles - 1)
    def _():
        _slab_loop(nslab_ref[g], lambda j, u: slab_copy(g, cur, j).wait())
        lax.fori_loop(zinfo_ref[1] + n_tiles, max_tiles,
                      lambda t, c: (tile_copy(t).start(priority=1), c)[1], 0)
        _slab_loop(zinfo_ref[0], lambda k, u: tail_copy(k).wait())
        lax.fori_loop(zinfo_ref[1], max_tiles, lambda t, c: (tile_copy(t).wait(), c)[1], 0)


def _dispatch(plan, x1s, max_tiles):
    group_tiles = tuple(x1.shape[0] // TOK_TILE for x1 in x1s)
    n_tiles = sum(group_tiles)
    tile = lambda i, d, ns, zt, zi: (i, 0, 0)
    in_specs = [pl.BlockSpec((1, TOP_K, TOK_TILE), tile), pl.BlockSpec((1, TOP_K, TOK_TILE), tile)]
    base = 0
    for nt in group_tiles:
        in_specs.append(pl.BlockSpec(
            (TOK_TILE, D_MODEL),
            lambda i, d, ns, zt, zi, base=base, nt=nt: (jnp.clip(i - base, 0, nt - 1), 0)))
        base += nt
    return pl.pallas_call(
        functools.partial(_dispatch_kernel, group_tiles=group_tiles, max_tiles=max_tiles),
        out_shape=jax.ShapeDtypeStruct((max_tiles * ROW_TILE, XS_WORDS), BF16),
        grid_spec=pltpu.PrefetchScalarGridSpec(
            num_scalar_prefetch=4, grid=(n_tiles,), in_specs=in_specs,
            out_specs=pl.BlockSpec(memory_space=pl.ANY),
            scratch_shapes=[pltpu.VMEM((2, PERM_ROWS, XS_WORDS), BF16), pltpu.SemaphoreType.DMA((2,)),
                            pltpu.VMEM((ROW_TILE, XS_WORDS), BF16), pltpu.SemaphoreType.DMA(())]),
        compiler_params=_cparams(("arbitrary",)), name="moe_dispatch",
    )(plan['slab_dst'], plan['nslab'], plan['ztab'], plan['zinfo'], plan['slot_rows'], plan['gate_rows'],
      *x1s)


def _expert_kernel(t0_ref, nt_ref, nu_ref, wg_ref, wu_ref, wd_ref, xs_ref, ye_ref,
                   xbuf, ybuf, wgu_scr, wd_scr, zbuf, in_sem, out_sem, zsem, *, max_tiles):
    e = pl.program_id(0)
    n_used = nu_ref[0]

    def in_copy(t):
        slot = lax.rem(t, 2)
        src = xs_ref.at[pl.ds(pl.multiple_of(t * ROW_TILE, ROW_TILE), ROW_TILE), :]
        return pltpu.make_async_copy(src, xbuf.at[slot], in_sem.at[slot])

    def out_copy(t):
        slot = lax.rem(t, 2)
        dst = ye_ref.at[pl.ds(pl.multiple_of(t * ROW_TILE, ROW_TILE), ROW_TILE), :]
        return pltpu.make_async_copy(ybuf.at[slot], dst, out_sem.at[slot])

    def zero_copy(t):
        dst = ye_ref.at[pl.ds(pl.multiple_of(t * ROW_TILE, ROW_TILE), ROW_TILE), :]
        return pltpu.make_async_copy(zbuf, dst, zsem)

    @pl.when(e == 0)
    def _():
        in_copy(0).start()

    wgu_scr[:, 0:EXPERT_FF] = wg_ref[0].astype(BF16)
    wgu_scr[:, EXPERT_FF:] = wu_ref[0].astype(BF16)
    wd_scr[...] = wd_ref[0].astype(BF16)

    sub = EXPERT_SUB
    rows = [slice(j * sub, (j + 1) * sub) for j in range(ROW_TILE // sub)]
    d = lambda a, b: jnp.dot(a, b, preferred_element_type=F32)

    def tile_body(j, carry):
        t = t0_ref[e] + j
        slot = lax.rem(t, 2)

        @pl.when(t + 1 < n_used)
        def _():
            in_copy(t + 1).start()

        in_copy(t).wait()

        @pl.when(t >= 2)
        def _():
            out_copy(t - 2).wait()

        wgu, wd = wgu_scr[...], wd_scr[...]
        xs = [xbuf[slot, r, 0:D_MODEL] for r in rows]
        hs = [d(x, wgu) for x in xs]
        hhs = [(_silu(h[:, :EXPERT_FF]) * h[:, EXPERT_FF:]).astype(BF16) for h in hs]
        ys = [d(hh, wd) for hh in hhs]
        for r, y in zip(rows, ys):
            parts = xbuf[slot, r, D_MODEL:].astype(F32)
            gate = parts + pltpu.roll(parts, LANES // 2, axis=1)
            ybuf[slot, r, :] = (y * jnp.concatenate([gate] * (D_MODEL // LANES), axis=1)).astype(BF16)
        out_copy(t).start()
        return carry

    lax.fori_loop(0, nt_ref[e], tile_body, 0)

    @pl.when(e == pl.num_programs(0) - 1)
    def _():
        @pl.when(n_used >= 2)
        def _():
            out_copy(n_used - 2).wait()

        out_copy(n_used - 1).wait()
        zbuf[...] = jnp.zeros_like(zbuf)
        lax.fori_loop(n_used, max_tiles, lambda t, c: (zero_copy(t).start(), c)[1], 0)
        lax.fori_loop(n_used, max_tiles, lambda t, c: (zero_copy(t).wait(), c)[1], 0)


def _experts(plan, xs, w_gate, w_up, w_down):
    max_tiles = xs.shape[0] // ROW_TILE
    wsel = lambda e, t0, nt, nu: (e, 0, 0)
    return pl.pallas_call(
        functools.partial(_expert_kernel, max_tiles=max_tiles),
        out_shape=jax.ShapeDtypeStruct((xs.shape[0], D_MODEL), BF16),
        grid_spec=pltpu.PrefetchScalarGridSpec(
            num_scalar_prefetch=3, grid=(N_EXPERTS,),
            in_specs=[pl.BlockSpec((1, D_MODEL, EXPERT_FF), wsel),
                      pl.BlockSpec((1, D_MODEL, EXPERT_FF), wsel),
                      pl.BlockSpec((1, EXPERT_FF, D_MODEL), wsel),
                      pl.BlockSpec(memory_space=pl.ANY)],
            out_specs=pl.BlockSpec(memory_space=pl.ANY),
            scratch_shapes=[pltpu.VMEM((2, ROW_TILE, XS_WORDS), BF16),
                            pltpu.VMEM((2, ROW_TILE, D_MODEL), BF16),
                            pltpu.VMEM((D_MODEL, 2 * EXPERT_FF), BF16),
                            pltpu.VMEM((EXPERT_FF, D_MODEL), BF16),
                            pltpu.VMEM((ROW_TILE, D_MODEL), BF16),
                            pltpu.SemaphoreType.DMA((2,)), pltpu.SemaphoreType.DMA((2,)),
                            pltpu.SemaphoreType.DMA(())]),
        compiler_params=_cparams(("arbitrary",)), name="moe_experts",
    )(plan['tile_start'], plan['tile_count'], plan['n_used'], w_gate, w_up, w_down, xs)


def _combine_kernel(dst_ref, nslab_ref, x1_ref, slot_ref, ye_ref, ln_g_ref, ln_b_ref, y_ref,
                    buf, sem, *, tile_base, n_tiles):
    i = pl.program_id(0)
    g = tile_base + i
    cur = lax.rem(i, 2)

    def slab_copy(tile, buf_slot, j):
        d = pl.multiple_of(dst_ref[tile * PERM_SLABS + j], SLAB)
        dst = buf.at[buf_slot, pl.ds(pl.multiple_of(j * SLAB, SLAB), SLAB), :]
        return pltpu.make_async_copy(ye_ref.at[pl.ds(d, SLAB), :], dst, sem.at[buf_slot])

    @pl.when(i == 0)
    def _():
        buf[...] = jnp.zeros_like(buf)
        _slab_loop(nslab_ref[g], lambda j, u: slab_copy(g, cur, j).start(priority=u % 2))

    @pl.when(i + 1 < n_tiles)
    def _():
        _slab_loop(nslab_ref[g + 1], lambda j, u: slab_copy(g + 1, 1 - cur, j).start(priority=u % 2))

    _slab_loop(nslab_ref[g], lambda j, u: slab_copy(g, cur, j).wait())

    col = lax.broadcasted_iota(I32, (TOK_TILE, PERM_ROWS), 1)
    sl = slot_ref[0]
    diag = (lax.broadcasted_iota(I32, (TOK_TILE, TOK_TILE), 0)
            == lax.broadcasted_iota(I32, (TOK_TILE, TOK_TILE), 1))
    as_col = lambda row: jnp.sum(jnp.where(diag, row, 0), axis=1, keepdims=True)
    pick = jnp.where((col == as_col(sl[0:1, :])) | (col == as_col(sl[1:2, :])), 1.0, 0.0).astype(BF16)
    moe = jnp.dot(pick, buf[cur], preferred_element_type=F32)
    y_ref[...] = _layer_norm(DEEPNORM_ALPHA * x1_ref[...] + moe, ln_g_ref[...], ln_b_ref[...])


def _combine(plan, tile_base, x1, ye, ln_g, ln_b):
    n = x1.shape[0]
    n_tiles = n // TOK_TILE
    tok = lambda w: pl.BlockSpec((TOK_TILE, w), lambda i, d, ns: (i, 0))
    const = lambda a: pl.BlockSpec(a.shape, lambda i, d, ns: (0,) * a.ndim)
    return pl.pallas_call(
        functools.partial(_combine_kernel, tile_base=tile_base, n_tiles=n_tiles),
        out_shape=jax.ShapeDtypeStruct((n, D_MODEL), F32),
        grid_spec=pltpu.PrefetchScalarGridSpec(
            num_scalar_prefetch=2, grid=(n_tiles,),
            in_specs=[tok(D_MODEL),
                      pl.BlockSpec((1, TOP_K, TOK_TILE), lambda i, d, ns: (tile_base + i, 0, 0)),
                      pl.BlockSpec(memory_space=pl.ANY), const(ln_g), const(ln_b)],
            out_specs=tok(D_MODEL),
            scratch_shapes=[pltpu.VMEM((2, PERM_ROWS, D_MODEL), BF16), pltpu.SemaphoreType.DMA((2,))]),
        compiler_params=_cparams(("arbitrary",)), name="moe_combine_%d" % tile_base,
    )(plan['slab_dst'], plan['nslab'], x1, plan['slot_rows'], ye, ln_g, ln_b)


def _routing_plan(ids, gates):
    nt = ids.shape[1] // TOK_TILE
    pairs = TOP_K * TOK_TILE
    ex = jnp.arange(N_EXPERTS, dtype=I32)
    per_tile = lambda a: jnp.swapaxes(a.reshape(TOP_K, nt, TOK_TILE), 0, 1)
    flat = per_tile(ids).reshape(nt, pairs)
    onehot = (flat[:, None, :] == ex[None, :, None])
    p = np.arange(pairs)
    triu = jnp.asarray(p[:, None] <= p[None, :], BF16)
    csum = jnp.dot(onehot.astype(BF16).reshape(nt * N_EXPERTS, pairs), triu,
                   preferred_element_type=F32).astype(I32).reshape(nt, N_EXPERTS, pairs)
    oh = onehot.astype(I32)
    rank = jnp.sum(oh * (csum - 1), axis=1)
    cnt = csum[:, :, -1]
    cpad = (cnt + SLAB - 1) // SLAB * SLAB
    seg_end = jnp.cumsum(cpad, axis=1)
    seg_off = seg_end - cpad
    slot = jnp.sum(oh * seg_off[:, :, None], axis=1) + rank
    run_end = jnp.cumsum(cpad, axis=0)
    ntiles_e = (run_end[-1] + ROW_TILE - 1) // ROW_TILE
    tile_end = jnp.cumsum(ntiles_e)
    dst_run = ((tile_end - ntiles_e) * ROW_TILE)[None, :] + run_end - cpad
    j8 = jnp.arange(PERM_SLABS, dtype=I32) * SLAB
    e_of = jnp.minimum(jnp.sum((j8[None, :, None] >= seg_end[:, None, :]).astype(I32), axis=2),
                       N_EXPERTS - 1)
    sel = (e_of[:, :, None] == ex).astype(I32)
    slab_dst = jnp.sum(sel * (dst_run - seg_off)[:, None, :], axis=2) + j8[None, :]
    n_used = tile_end[-1]
    row_start = (tile_end - ntiles_e) * ROW_TILE
    tail_cnt = (ntiles_e * ROW_TILE - run_end[-1]) // SLAB
    tail_end = jnp.cumsum(tail_cnt)
    k = jnp.arange(ZERO_TABLE, dtype=I32)
    e_k = jnp.minimum(jnp.sum((k[:, None] >= tail_end[None, :]).astype(I32), axis=1), N_EXPERTS - 1)
    base_k = jnp.sum((e_k[:, None] == ex).astype(I32)
                     * (row_start + run_end[-1] - SLAB * (tail_end - tail_cnt))[None, :], axis=1)
    return dict(
        slab_dst=slab_dst.reshape(-1).astype(I32), nslab=(seg_end[:, -1] // SLAB).astype(I32),
        ztab=(base_k + SLAB * k).astype(I32), zinfo=jnp.stack([tail_end[-1], n_used]).astype(I32),
        slot_rows=slot.reshape(nt, TOP_K, TOK_TILE).astype(I32),
        gate_rows=per_tile(gates).astype(F32),
        tile_start=(tile_end - ntiles_e).astype(I32), tile_count=ntiles_e.astype(I32),
        n_used=n_used.reshape(1).astype(I32))


def _max_row_tiles(n_tokens):
    rows = TOP_K * n_tokens + (n_tokens // TOK_TILE) * N_EXPERTS * (SLAB - 1)
    return (rows + ROW_TILE - 1) // ROW_TILE + N_EXPERTS


def _moe(x1s, routes, wts):
    ids = jnp.concatenate([r[TOP_K:2 * TOP_K, :] for r in routes], axis=1).astype(I32)
    gates = jnp.concatenate([r[0:TOP_K, :] for r in routes], axis=1)
    plan = _routing_plan(ids, gates)
    max_tiles = _max_row_tiles(ids.shape[1])
    bases = [0]
    for x1 in x1s[:-1]:
        bases.append(bases[-1] + x1.shape[0] // TOK_TILE)
    xs = _dispatch(plan, x1s, max_tiles)
    ye = _experts(plan, xs, wts['w_gate'], wts['w_up'], wts['w_down'])
    return [_combine(plan, base, x1, ye, wts['ln2_g'], wts['ln2_b']) for base, x1 in zip(bases, x1s)]


def _prep_weights(w_in, w_out, conv_w, a_log, dt_bias, gdn_norm_w, ln1_g, ln1_b, w_router_group,
                  w_router_expert, w_gate, w_up, w_down, ln2_g, ln2_b):
    pad_row = lambda v: jnp.pad(v.astype(F32), (0, LANES - v.shape[0]))[None, :]
    wr = jnp.pad(jnp.concatenate([w_router_group, w_router_expert], axis=1),
                 ((0, 0), (0, LANES - N_GROUPS - N_EXPERTS)))
    wr_hi = wr.astype(BF16)
    wr_mid = (wr - wr_hi.astype(F32)).astype(BF16)
    group = ATTN_HEADS // ATTN_KV_HEADS
    wq = w_in[:, :Q_COLS].reshape(D_MODEL, ATTN_KV_HEADS, group, HEAD_DIM)
    wq = jnp.swapaxes(wq, 1, 2).reshape(D_MODEL, Q_COLS)
    w_all = jnp.concatenate([wq, w_in[:, Q_COLS:]], axis=1)
    w_all = jnp.pad(w_all, ((0, 0), (0, IN_COLS_PAD - w_all.shape[1]))).astype(BF16)
    wo_q = w_out[:Q_COLS].reshape(ATTN_KV_HEADS, group, HEAD_DIM, D_MODEL)
    wo_q = jnp.swapaxes(wo_q, 0, 1).reshape(Q_COLS, D_MODEL)
    wo = jnp.concatenate([wo_q, w_out[Q_COLS:]], axis=0)
    return dict(
        w_all=w_all, convw=conv_w.astype(F32), alog=pad_row(a_log), dtb=pad_row(dt_bias),
        norm_w=gdn_norm_w.astype(F32)[None, :], wo=wo.astype(BF16),
        ln1_g=ln1_g[None, :], ln1_b=ln1_b[None, :], wr=jnp.concatenate([wr_hi, wr_mid], axis=1),
        w_gate=w_gate, w_up=w_up, w_down=w_down, ln2_g=ln2_g[None, :], ln2_b=ln2_b[None, :])


def _layer(x_prompt, x_sample, cache_k, cache_v, state_gdn, state_conv, wts):
    bp, sp, _ = x_prompt.shape
    bs, ts, _ = x_sample.shape
    n_p = bp * sp

    xp = x_prompt.reshape(n_p, D_MODEL)
    (q, k, v, qg, kg, vg, z, gcb, utail) = _proj(xp, np.arange(sp), wts, GDN_CHUNK, bp)
    attn_p = _attn_prompt(q, k, v, wts['sinks'], bp)
    gdn_p, s_p = _gdn_prompt(qg, kg, vg, z, gcb, wts['norm_w'], bp)
    last_win = lambda a: a.reshape(bp, sp, KV_COLS)[:, sp - WINDOW:].reshape(bp, WINDOW, ATTN_KV_HEADS,
                                                                            HEAD_DIM)
    new_k_p, new_v_p = last_win(k), last_win(v)
    tiles_per_seq = sp // min(PROJ_TILE, sp)
    conv_p = utail.reshape(bp, tiles_per_seq, SUBLANES, CONV_DIM)[:, -1, SUBLANES - (CONV_W - 1):]

    lo, hi = SAMPLE_FIRST, SAMPLE_FIRST + ts
    xs_rows = jnp.pad(x_sample, ((0, 0), (lo, SAMPLE_SLOTS - hi), (0, 0))).reshape(bs * SAMPLE_SLOTS, D_MODEL)
    hist = jnp.pad(state_conv, ((0, 0), (0, SAMPLE_SLOTS - lo), (0, 0))).reshape(bs * SAMPLE_SLOTS, CONV_DIM)
    slot = np.arange(SAMPLE_SLOTS)
    valid = jnp.asarray(np.tile((slot >= lo) & (slot < hi), bs)[:, None], F32)
    pos_s = np.tile(PAST_LEN + slot - lo, bs)
    (q, k, v, qg, kg, vg, z, gcb, u_s) = _proj(xs_rows, pos_s, wts, SAMPLE_SLOTS, 1, hist, valid)
    ck = cache_k.reshape(bs, WINDOW, KV_COLS)
    cv = cache_v.reshape(bs, WINDOW, KV_COLS)
    attn_s, kwin, vwin = _attn_sample(q, k, v, ck, cv, wts['sinks'], bs, ts)
    gdn_s, s_s = _gdn_sample(qg, kg, vg, z, gcb, wts['norm_w'], state_gdn)
    real = lambda a: a.reshape(bs, SAMPLE_SLOTS, -1)[:, lo:hi]
    new_k_s = kwin.reshape(bs, WINDOW, ATTN_KV_HEADS, HEAD_DIM)
    new_v_s = vwin.reshape(bs, WINDOW, ATTN_KV_HEADS, HEAD_DIM)
    conv_s = u_s.reshape(bs, SAMPLE_SLOTS, CONV_DIM)[:, hi - (CONV_W - 1):hi]

    x1_p, route_p = _post(attn_p, gdn_p, xp, wts)
    x1_s, route_s = _post(real(attn_s).reshape(bs * ts, Q_COLS), real(gdn_s).reshape(bs * ts, Z_COLS),
                          x_sample.reshape(bs * ts, D_MODEL), wts)
    y_p, y_s = _moe([x1_p, x1_s], [route_p, route_s], wts)
    return (y_p.reshape(bp, sp, D_MODEL), y_s.reshape(bs, ts, D_MODEL), new_k_p, new_v_p, s_p, conv_p,
            new_k_s, new_v_s, s_s, conv_s)


def kernel(x_prompt, x_sample, cache_attn_k, cache_attn_v, state_gdn, state_conv, w_in, w_out,
           attn_sinks, conv_w, a_log, dt_bias, gdn_norm_w, ln1_g, ln1_b, w_router_group,
           w_router_expert, w_gate, w_up, w_down, ln2_g, ln2_b):
    assert w_in.shape[0] == DEPTH
    l = 0
    wts = _prep_weights(w_in[l], w_out[l], conv_w[l], a_log[l], dt_bias[l], gdn_norm_w[l], ln1_g[l],
                        ln1_b[l], w_router_group[l], w_router_expert[l], w_gate[l], w_up[l],
                        w_down[l], ln2_g[l], ln2_b[l])
    wts['sinks'] = attn_sinks[l]
    outs = _layer(x_prompt, x_sample, cache_attn_k[l], cache_attn_v[l], state_gdn[l], state_conv[l], wts)
    (y_p, y_s, k_p, v_p, s_p, c_p, k_s, v_s, s_s, c_s) = outs
    add = lambda a: a[None]
    return (y_p, y_s, add(k_p), add(v_p), add(s_p), add(c_p), add(k_s), add(v_s), add(s_s), add(c_s))
```

```python
import functools
import math

import jax
import jax.numpy as jnp
import numpy as np
from jax import lax
from jax.experimental import pallas as pl
from jax.experimental.pallas import tpu as pltpu

F32 = jnp.float32
BF16 = jnp.bfloat16
I32 = jnp.int32

D_MODEL = 1024
ATTN_HEADS = 8
ATTN_KV_HEADS = 2
HEAD_DIM = 64
WINDOW = 128
ROT_DIM = HEAD_DIM // 4
ROPE_THETA = 500000.0
GDN_HEADS = 4
GDN_DK = 128
GDN_DV = 128
CONV_W = 4
QK_COLS = GDN_HEADS * GDN_DK
CONV_DIM = 2 * QK_COLS + GDN_HEADS * GDN_DV
Z_COLS = GDN_HEADS * GDN_DV
Q_COLS = ATTN_HEADS * HEAD_DIM
KV_COLS = ATTN_KV_HEADS * HEAD_DIM
N_GROUPS = 4
EXPERTS_PER_GROUP = 8
N_EXPERTS = N_GROUPS * EXPERTS_PER_GROUP
TOP_K = 2
EXPERT_FF = 256
NORM_EPS = 1e-5
L2_EPS = 1e-6
DEPTH = 1
DEEPNORM_ALPHA = (2 * DEPTH) ** 0.25
PAST_LEN = 8192

LANES = 128
SUBLANES = 8
IN_SPLITS = (0, Q_COLS + 2 * KV_COLS, Q_COLS + 2 * KV_COLS + CONV_DIM,
             Q_COLS + 2 * KV_COLS + CONV_DIM + Z_COLS, Q_COLS + 2 * KV_COLS + CONV_DIM + Z_COLS + LANES)
IN_COLS_PAD = IN_SPLITS[-1]
TOK_TILE = 512
PROJ_TILE = 512
PROJ_SUB = 128
POST_TILE = 1024
POST_SUB = 256
GDN_CHUNK = 128
GDN_SEQ_PER_STEP = 4
ATTN_BLOCKS_PER_STEP = 4
ATTN_SEQS_PER_STEP = 8
INV_BASE = 16
SAMPLE_SLOTS = 8
SAMPLE_FIRST = CONV_W - 1
ROW_TILE = 512
EXPERT_SUB = 256
SLAB_UNROLL = 4
SLAB = 16
PERM_ROWS = TOP_K * TOK_TILE + N_EXPERTS * SLAB
PERM_SLABS = PERM_ROWS // SLAB
XS_WORDS = D_MODEL + LANES
ZERO_TABLE = N_EXPERTS * (ROW_TILE // SLAB)
VMEM_LIMIT = 48 * 1024 * 1024
NEG_BIG = -1e30


def _cparams(sem):
    return pltpu.CompilerParams(dimension_semantics=sem, vmem_limit_bytes=VMEM_LIMIT)


def _bdot(a, b):
    return jnp.dot(a.astype(BF16), b.astype(BF16), preferred_element_type=F32)


def _bdot_nt(a, b):
    return lax.dot_general(a.astype(BF16), b.astype(BF16), (((1,), (1,)), ((), ())),
                           preferred_element_type=F32)


def _bdot_tn(a, b):
    return lax.dot_general(a.astype(BF16), b.astype(BF16), (((0,), (0,)), ((), ())),
                           preferred_element_type=F32)


def _div_pow2(x, n):
    return jnp.right_shift(x, int(math.log2(n)))


def _mod_pow2(x, n):
    return jnp.bitwise_and(x, n - 1)


def _split3(x):
    hi = x.astype(BF16)
    r = x - hi.astype(F32)
    mid = r.astype(BF16)
    lo = (r - mid.astype(F32)).astype(BF16)
    return hi, mid, lo


def _dot_exact_lhs01(m01, x):
    hi, mid, lo = _split3(x)
    d = lambda t: jnp.dot(m01, t, preferred_element_type=F32)
    return d(hi) + d(mid) + d(lo)


def _sigmoid(x):
    return 1.0 / (1.0 + jnp.exp(-x))


def _silu(x):
    return x * _sigmoid(x)


def _softplus(x):
    return jnp.maximum(x, 0.0) + jnp.log1p(jnp.exp(-jnp.abs(x)))


def _layer_norm(h, g, b):
    mu = jnp.mean(h, axis=-1, keepdims=True)
    d = h - mu
    var = jnp.mean(d * d, axis=-1, keepdims=True)
    return d * lax.rsqrt(var + NORM_EPS) * g + b


def _proj_kernel(*refs, tm, has_hist, full_u, one_segment):
    it = iter(refs)
    x_ref, cos_ref, sin_ref = next(it), next(it), next(it)
    w_ref = next(it)
    convw_ref, alog_ref, dtb_ref, tri_ref, seg_ref = next(it), next(it), next(it), next(it), next(it)
    hist_ref = valid_ref = None
    if has_hist:
        hist_ref, valid_ref = next(it), next(it)
    q_ref, k_ref, v_ref = next(it), next(it), next(it)
    qg_ref, kg_ref, vg_ref, z_ref, gcb_ref, u_ref = (next(it) for _ in range(6))
    ubuf = next(it)

    t = pl.program_id(1)
    sub = PROJ_SUB
    rows = [slice(j * sub, (j + 1) * sub) for j in range(tm // sub)]
    lane = lax.broadcasted_iota(I32, (sub, LANES), 1)
    first_half = _mod_pow2(lane, HEAD_DIM) < (ROT_DIM // 2)

    @pl.when(t == 0)
    def _():
        ubuf[0:SUBLANES, :] = jnp.zeros((SUBLANES, CONV_DIM), F32)

    @pl.when(t > 0)
    def _():
        ubuf[0:SUBLANES, :] = ubuf[tm:tm + SUBLANES, :]

    dots = []
    for r in rows:
        xb = x_ref[r, :].astype(BF16)
        dots.append([jnp.dot(xb, w_ref[:, lo:hi], preferred_element_type=F32)
                     for lo, hi in zip(IN_SPLITS[:-1], IN_SPLITS[1:])])

    def l2n(s):
        return s * lax.rsqrt(jnp.sum(s * s, axis=1, keepdims=True) + L2_EPS)

    for r, (pq, u, z, ab) in zip(rows, dots):
        cosv, sinv = cos_ref[r, :], sin_ref[r, :]

        def rope(s):
            sw = jnp.where(first_half, pltpu.roll(s, LANES - ROT_DIM // 2, axis=1),
                           pltpu.roll(s, ROT_DIM // 2, axis=1))
            return s * cosv + sw * sinv

        for j in range(Q_COLS // LANES):
            q_ref[r, j * LANES:(j + 1) * LANES] = rope(pq[:, j * LANES:(j + 1) * LANES])
        k_ref[r, :] = rope(pq[:, Q_COLS:Q_COLS + KV_COLS])
        v_ref[r, :] = pq[:, Q_COLS + KV_COLS:Q_COLS + 2 * KV_COLS]
        z_ref[r, :] = z

        if has_hist:
            u = u + hist_ref[r, :]
        if full_u:
            u_ref[r, :] = u
        elif r.stop == tm:
            u_ref[...] = u[sub - SUBLANES:, :]
        base = SUBLANES + r.start
        ubuf[base:base + sub, :] = u
        acc = u * convw_ref[CONV_W - 1:CONV_W, :]
        for j in range(1, CONV_W):
            acc = acc + ubuf[base - j:base - j + sub, :] * convw_ref[CONV_W - 1 - j:CONV_W - j, :]
        c = _silu(acc)
        if has_hist:
            c = c * valid_ref[r, :]
        for h in range(GDN_HEADS):
            sl = slice(h * GDN_DK, (h + 1) * GDN_DK)
            qg_ref[r, sl] = l2n(c[:, sl]) * (GDN_DK ** -0.5)
            kg_ref[r, sl] = l2n(c[:, QK_COLS + h * GDN_DK:QK_COLS + (h + 1) * GDN_DK])
        vg_ref[r, :] = c[:, 2 * QK_COLS:]

        g = -jnp.exp(alog_ref[...]) * _softplus(ab + dtb_ref[...])
        beta = _sigmoid(ab)
        if has_hist:
            g = g * valid_ref[r, :]
            beta = beta * valid_ref[r, :]
        g = jnp.where(lane < GDN_HEADS, g, 0.0)
        gc = _dot_exact_lhs01(tri_ref[...], g)
        if one_segment:
            gl = jnp.broadcast_to(gc[sub - 1:sub, :], (sub, LANES))
        else:
            gl = _dot_exact_lhs01(seg_ref[...], g)
        gcb_ref[r, :] = jnp.where(lane < GDN_HEADS, gc,
                                  jnp.where(lane < 2 * GDN_HEADS, beta,
                                            jnp.where(lane < 3 * GDN_HEADS,
                                                      pltpu.roll(gl, 2 * GDN_HEADS, axis=1), 0.0)))


def _rope_tables(pos):
    half = ROT_DIM // 2
    pos = np.asarray(pos, np.float64)
    inv_freq = ROPE_THETA ** (-np.arange(half, dtype=np.float64) * 2.0 / ROT_DIM)
    ang = pos[:, None] * inv_freq[None, :]
    cos, sin = np.cos(ang), np.sin(ang)
    p = pos.shape[0]
    cpat = np.concatenate([cos, cos, np.ones((p, HEAD_DIM - ROT_DIM))], axis=1)
    spat = np.concatenate([-sin, sin, np.zeros((p, HEAD_DIM - ROT_DIM))], axis=1)
    rep = (1, LANES // HEAD_DIM)
    return jnp.asarray(np.tile(cpat, rep), F32), jnp.asarray(np.tile(spat, rep), F32)


def _segment_matrices(tm, seg_len):
    i = np.arange(tm)
    same = (i[:, None] // seg_len) == (i[None, :] // seg_len)
    tri = same & (i[None, :] <= i[:, None])
    return jnp.asarray(tri, BF16), jnp.asarray(same, BF16)


def _proj(x, pos, wts, seg_len, n_seq, hist=None, valid=None):
    n = x.shape[0]
    rows = n // n_seq
    tm = min(PROJ_TILE, rows)
    nt = rows // tm
    has_hist = hist is not None
    cos_t, sin_t = _rope_tables(pos)
    tri, seg = _segment_matrices(PROJ_SUB, seg_len)

    tok = lambda w: pl.BlockSpec((tm, w), lambda b, t: (b * nt + t, 0))
    const = lambda a: pl.BlockSpec(a.shape, lambda b, t: (0,) * a.ndim)
    in_arrays = [x, cos_t, sin_t, wts['w_all'], wts['convw'], wts['alog'], wts['dtb'], tri, seg]
    in_specs = [tok(D_MODEL), pl.BlockSpec((tm, LANES), lambda b, t: (t, 0)),
                pl.BlockSpec((tm, LANES), lambda b, t: (t, 0))] + [const(a) for a in in_arrays[3:]]
    if has_hist:
        in_arrays += [hist, valid]
        in_specs += [tok(CONV_DIM), tok(1)]
    u_rows = n if has_hist else (n // tm) * SUBLANES
    u_block = tm if has_hist else SUBLANES
    out_shape = [jax.ShapeDtypeStruct((n, Q_COLS), F32), jax.ShapeDtypeStruct((n, KV_COLS), F32),
                 jax.ShapeDtypeStruct((n, KV_COLS), F32), jax.ShapeDtypeStruct((n, QK_COLS), F32),
                 jax.ShapeDtypeStruct((n, QK_COLS), F32), jax.ShapeDtypeStruct((n, Z_COLS), F32),
                 jax.ShapeDtypeStruct((n, Z_COLS), F32), jax.ShapeDtypeStruct((n, LANES), F32),
                 jax.ShapeDtypeStruct((u_rows, CONV_DIM), F32)]
    out_specs = [tok(Q_COLS), tok(KV_COLS), tok(KV_COLS), tok(QK_COLS), tok(QK_COLS), tok(Z_COLS),
                 tok(Z_COLS), tok(LANES),
                 pl.BlockSpec((u_block, CONV_DIM), lambda b, t: (b * nt + t, 0))]
    return pl.pallas_call(
        functools.partial(_proj_kernel, tm=tm, has_hist=has_hist, full_u=has_hist,
                          one_segment=seg_len == PROJ_SUB),
        out_shape=out_shape, grid=(n_seq, nt), in_specs=in_specs, out_specs=out_specs,
        scratch_shapes=[pltpu.VMEM((tm + SUBLANES, CONV_DIM), F32)],
        compiler_params=_cparams(("arbitrary", "arbitrary")),
        name="proj_hist" if has_hist else "proj",
    )(*in_arrays)


def _attn_blocks(qs, kcats, vcats, biases, sink, tq):
    lane = lax.broadcasted_iota(I32, (tq, LANES), 1)
    low = lane < HEAD_DIM
    n_slab = Q_COLS // LANES

    def stack(q):
        slabs = [q[:, j * LANES:(j + 1) * LANES] * (HEAD_DIM ** -0.5) for j in range(n_slab)]
        parts = ([jnp.where(low, s, 0.0) for s in slabs] + [jnp.where(low, 0.0, s) for s in slabs])
        return jnp.concatenate(parts, axis=0).astype(BF16)

    def unstack(o8):
        return [jnp.where(low, o8[j * tq:(j + 1) * tq, :], o8[(n_slab + j) * tq:(n_slab + j + 1) * tq, :])
                for j in range(n_slab)]

    rows = ATTN_HEADS * tq
    half = rows // 2
    klow = lax.broadcasted_iota(I32, (2 * WINDOW, LANES), 1) < HEAD_DIM
    one = jnp.ones((), BF16)
    q8s = _each(stack, qs)
    ss = _each(lambda q8, kc, b: _bdot_nt(q8, kc) + b, q8s, kcats, biases)
    ms = _each(lambda s: jnp.maximum(jnp.broadcast_to(jnp.max(s, axis=1, keepdims=True), (rows, LANES)),
                                     sink), ss)
    ps = _each(lambda s, m: jnp.exp(s - jnp.concatenate([m, m], axis=1)).astype(BF16), ss, ms)
    pv0 = _each(lambda p, vc: jnp.dot(p[:half], jnp.where(klow, vc, one), preferred_element_type=F32),
                ps, vcats)
    pv1 = _each(lambda p, vc: jnp.dot(p[half:], jnp.where(klow, one, vc), preferred_element_type=F32),
                ps, vcats)
    pvs = _each(lambda a, b: jnp.concatenate([a, b], axis=0), pv0, pv1)
    o8s = _each(lambda pv, m: pv / (pltpu.roll(pv, HEAD_DIM, axis=1) + jnp.exp(sink - m)), pvs, ms)
    return _each(unstack, o8s)


def _attn_prompt_kernel(q_ref, kc_ref, vc_ref, kp_ref, vp_ref, bias0_ref, bias_ref, sink_ref, o_ref, *,
                        nblk):
    kall = jnp.concatenate([kp_ref[...], kc_ref[...]], axis=0).astype(BF16)
    vall = jnp.concatenate([vp_ref[...], vc_ref[...]], axis=0).astype(BF16)
    win = lambda a, j: a[j * WINDOW:(j + 2) * WINDOW, :]
    qs = [q_ref[j * WINDOW:(j + 1) * WINDOW, :] for j in range(nblk)]
    biases = [bias0_ref[0]] + [bias_ref[...]] * (nblk - 1)
    outs = _attn_blocks(qs, [win(kall, j) for j in range(nblk)], [win(vall, j) for j in range(nblk)],
                        biases, sink_ref[...], WINDOW)
    for j, slabs in enumerate(outs):
        for c, slab in enumerate(slabs):
            o_ref[j * WINDOW:(j + 1) * WINDOW, c * LANES:(c + 1) * LANES] = slab


def _attn_sample_kernel(q_ref, kc_ref, vc_ref, kp_ref, vp_ref, bias_ref, sink_ref, o_ref, kw_ref, vw_ref,
                        *, nseq, n_new):
    tq = SAMPLE_SLOTS
    zpad = jnp.zeros((WINDOW - tq, LANES), F32)
    rows = lambda ref, j: ref[j * tq:(j + 1) * tq, :]
    cat = lambda pref, cref, j: jnp.concatenate([pref[j], rows(cref, j), zpad], axis=0).astype(BF16)
    outs = _attn_blocks([rows(q_ref, j) for j in range(nseq)],
                        [cat(kp_ref, kc_ref, j) for j in range(nseq)],
                        [cat(vp_ref, vc_ref, j) for j in range(nseq)],
                        [bias_ref[...]] * nseq, sink_ref[...], tq)
    for j, slabs in enumerate(outs):
        for c, slab in enumerate(slabs):
            o_ref[j * tq:(j + 1) * tq, c * LANES:(c + 1) * LANES] = slab
    row = lax.broadcasted_iota(I32, (WINDOW, LANES), 0)
    keep = WINDOW - n_new
    for pref, cref, wref in ((kp_ref, kc_ref, kw_ref), (vp_ref, vc_ref, vw_ref)):
        for j in range(nseq):
            new = jnp.concatenate([rows(cref, j), zpad], axis=0)
            wref[j] = jnp.where(row < keep, pltpu.roll(pref[j], keep, axis=0),
                                pltpu.roll(new, keep - SAMPLE_FIRST, axis=0))


def _sink_rows(sinks, tq):
    return jnp.broadcast_to(jnp.repeat(sinks.astype(F32), tq)[:, None], (ATTN_HEADS * tq, LANES))


def _attn_bias(tq, q_off, k_lo, k_hi, has_prev):
    qi = (np.arange(ATTN_HEADS * tq) % tq)[:, None]
    c = np.arange(2 * WINDOW)[None, :]
    cj = c - WINDOW
    vis_prev = (c < WINDOW) & (c > qi - q_off) & has_prev
    vis_cur = (c >= WINDOW) & (cj <= qi) & (cj >= k_lo) & (cj <= k_hi)
    return np.where(vis_prev | vis_cur, 0.0, NEG_BIG).astype(np.float32)


def _attn_prompt(q, k, v, sinks, n_seq):
    n = q.shape[0]
    nb = n // n_seq // WINDOW
    nblk = min(ATTN_BLOCKS_PER_STEP, nb)
    steps = nb // nblk
    tq = nblk * WINDOW
    cur = lambda w: pl.BlockSpec((tq, w), lambda b, i: (b * steps + i, 0))
    prev = pl.BlockSpec((WINDOW, LANES), lambda b, i: (b * nb + jnp.maximum(i * nblk - 1, 0), 0))
    bias2 = jnp.asarray(np.stack([_attn_bias(WINDOW, 0, 0, WINDOW - 1, False),
                                  _attn_bias(WINDOW, 0, 0, WINDOW - 1, True)]))
    rows = ATTN_HEADS * WINDOW
    return pl.pallas_call(
        functools.partial(_attn_prompt_kernel, nblk=nblk),
        out_shape=jax.ShapeDtypeStruct((n, Q_COLS), F32), grid=(n_seq, steps),
        in_specs=[cur(Q_COLS), cur(LANES), cur(LANES), prev, prev,
                  pl.BlockSpec((1, rows, 2 * WINDOW), lambda b, i: (jnp.minimum(i, 1), 0, 0)),
                  pl.BlockSpec((rows, 2 * WINDOW), lambda b, i: (0, 0)),
                  pl.BlockSpec((rows, LANES), lambda b, i: (0, 0))],
        out_specs=cur(Q_COLS),
        compiler_params=_cparams(("arbitrary", "arbitrary")), name="attn_prompt",
    )(q, k, v, k, v, bias2, bias2[1], _sink_rows(sinks, WINDOW))


def _attn_sample(q, k, v, cache_k, cache_v, sinks, n_seq, n_new):
    tq = SAMPLE_SLOTS
    nseq = min(ATTN_SEQS_PER_STEP, n_seq)
    cur = lambda w: pl.BlockSpec((nseq * tq, w), lambda b: (b, 0))
    prev = pl.BlockSpec((nseq, WINDOW, LANES), lambda b: (b, 0, 0))
    bias = jnp.asarray(_attn_bias(tq, SAMPLE_FIRST, SAMPLE_FIRST, SAMPLE_FIRST + 3, True))
    win = jax.ShapeDtypeStruct((n_seq, WINDOW, LANES), F32)
    return pl.pallas_call(
        functools.partial(_attn_sample_kernel, nseq=nseq, n_new=n_new),
        out_shape=[jax.ShapeDtypeStruct((n_seq * tq, Q_COLS), F32), win, win], grid=(n_seq // nseq,),
        in_specs=[cur(Q_COLS), cur(LANES), cur(LANES), prev, prev,
                  pl.BlockSpec(bias.shape, lambda b: (0, 0)),
                  pl.BlockSpec((ATTN_HEADS * tq, LANES), lambda b: (0, 0))],
        out_specs=[cur(Q_COLS), prev, prev],
        compiler_params=_cparams(("arbitrary",)), name="attn_sample",
    )(q, k, v, cache_k, cache_v, bias, _sink_rows(sinks, tq))


def _each(f, *lists):
    return [f(*args) for args in zip(*lists)]


def _unit_lower_inverse(ms, eye, same_base, base_only=False):
    c = ms[0].shape[0]

    def neumann(q0s, n_factors):
        xs = _each(lambda q: eye + q, q0s)
        if n_factors == 1:
            return xs
        qs = _each(_bdot, q0s, q0s)
        for _ in range(n_factors - 2):
            prods = _each(lambda x, q: _bdot(jnp.concatenate([x, q], axis=0), q), xs, qs)
            xs = _each(lambda x, pr: x + pr[:c], xs, prods)
            qs = _each(lambda pr: pr[c:], prods)
        return _each(lambda x, q: x + _bdot(x, q), xs, qs)

    ds = _each(lambda m: jnp.where(same_base, m, 0.0), ms)
    xs = neumann(_each(lambda d: -d, ds), int(math.log2(INV_BASE)))
    nblk = c // INV_BASE
    if nblk == 1 or base_only:
        return xs
    ls = _each(lambda m, d: m - d, ms, ds)
    ns = _each(lambda x, l: -_bdot(x, l), xs, ls)
    ys = neumann(ns, int(math.log2(nblk)))
    return _each(_bdot, ys, xs)


def _gdn_intra(qs, ks, vs, gcs, gls, betas, same_seq, low_incl, low_strict, eye, same_base,
               base_only=False):
    del same_seq
    e_gcs = _each(jnp.exp, gcs)

    def decay_of(gc):
        gc_row = jnp.sum(jnp.where(eye > 0, gc, 0.0), axis=0, keepdims=True)
        return jnp.where(low_incl, jnp.exp(jnp.where(low_incl, gc - gc_row, 0.0)), 0.0)

    c = qs[0].shape[0]
    decays = _each(decay_of, gcs)
    kbs = _each(lambda k, b: k * b, ks, betas)
    vbs = _each(lambda v, b: v * b, vs, betas)
    kqs = _each(lambda kb, q, k: _bdot_nt(jnp.concatenate([kb, q], axis=0), k), kbs, qs, ks)
    ms = _each(lambda kq, d: jnp.where(low_strict, kq[:c] * d, 0.0), kqs, decays)
    attns = _each(lambda kq, d: jnp.where(low_incl, kq[c:] * d, 0.0), kqs, decays)
    tmats = _unit_lower_inverse(ms, eye, same_base, base_only)
    uws = _each(lambda t, vb, kb, e: _bdot(t, jnp.concatenate([vb, kb * e], axis=1)),
                tmats, vbs, kbs, e_gcs)
    us = _each(lambda uw: uw[:, :GDN_DV], uws)
    ws = _each(lambda uw: uw[:, GDN_DV:], uws)
    q_decs = _each(lambda q, e: q * e, qs, e_gcs)
    k_decs = _each(lambda k, gl, gc: k * jnp.exp(gl - gc), ks, gls, gcs)
    return us, ws, attns, q_decs, k_decs


def _chunk_masks(c, seq_len):
    i = lax.broadcasted_iota(I32, (c, c), 0)
    j = lax.broadcasted_iota(I32, (c, c), 1)
    same_seq = _div_pow2(i, seq_len) == _div_pow2(j, seq_len)
    low_incl = same_seq & (i >= j)
    low_strict = same_seq & (i > j)
    eye = (i == j).astype(F32)
    same_base = _div_pow2(i, INV_BASE) == _div_pow2(j, INV_BASE)
    return same_seq, low_incl, low_strict, eye, same_base


def _gated_rms(o, z, nw):
    o = o * lax.rsqrt(jnp.mean(o * o, axis=1, keepdims=True) + NORM_EPS) * nw
    return o * _silu(z)


def _gdn_prompt_kernel(qg_ref, kg_ref, vg_ref, z_ref, gcb_ref, nw_ref, o_ref, s_out_ref, s_scr):
    c = GDN_CHUNK
    n = pl.program_id(1)

    @pl.when(n == 0)
    def _():
        s_scr[...] = jnp.zeros_like(s_scr)

    masks = _chunk_masks(c, c)
    nw = nw_ref[...]
    chains = [(b, h) for b in range(qg_ref.shape[0]) for h in range(GDN_HEADS)]
    hs = lambda h: slice(h * GDN_DK, (h + 1) * GDN_DK)
    col = lambda off: [gcb_ref[b, :, off + h:off + h + 1] for b, h in chains]
    gcs, betas, gls = col(0), col(GDN_HEADS), col(2 * GDN_HEADS)
    qs = [qg_ref[b, :, hs(h)] for b, h in chains]
    ks = [kg_ref[b, :, hs(h)] for b, h in chains]
    vs = [vg_ref[b, :, hs(h)] for b, h in chains]
    us, ws, attns, q_decs, k_decs = _gdn_intra(qs, ks, vs, gcs, gls, betas, *masks)
    ss = [s_scr[b, h] for b, h in chains]
    wqs = _each(lambda w, qd, s: _bdot(jnp.concatenate([w, qd], axis=0), s), ws, q_decs, ss)
    wss = _each(lambda wq: wq[:c], wqs)
    qss = _each(lambda wq: wq[c:], wqs)
    v_news = _each(lambda u, x: u - x, us, wss)
    avs = _each(_bdot, attns, v_news)
    kvs = _each(_bdot_tn, k_decs, v_news)
    for (b, h), s, gl, qsv, av, kv in zip(chains, ss, gls, qss, avs, kvs):
        s_scr[b, h] = s * jnp.exp(gl[0:1, :]) + kv
        o_ref[b, :, hs(h)] = _gated_rms(qsv + av, z_ref[b, :, hs(h)], nw)

    @pl.when(n == pl.num_programs(1) - 1)
    def _():
        s_out_ref[...] = s_scr[...]


def _gdn_prompt(qg, kg, vg, z, gcb, norm_w, n_seq):
    n = qg.shape[0]
    s_len = n // n_seq
    nb = min(GDN_SEQ_PER_STEP, n_seq)
    v3 = lambda a: a.reshape(n_seq, s_len, a.shape[-1])
    tok = lambda w: pl.BlockSpec((nb, GDN_CHUNK, w), lambda b, i: (b, i, 0))
    o, s = pl.pallas_call(
        _gdn_prompt_kernel,
        out_shape=[jax.ShapeDtypeStruct((n_seq, s_len, Z_COLS), F32),
                   jax.ShapeDtypeStruct((n_seq, GDN_HEADS, GDN_DK, GDN_DV), F32)],
        grid=(n_seq // nb, s_len // GDN_CHUNK),
        in_specs=[tok(QK_COLS), tok(QK_COLS), tok(Z_COLS), tok(Z_COLS), tok(LANES),
                  pl.BlockSpec((1, GDN_DV), lambda b, i: (0, 0))],
        out_specs=[tok(Z_COLS),
                   pl.BlockSpec((nb, GDN_HEADS, GDN_DK, GDN_DV), lambda b, i: (b, 0, 0, 0))],
        scratch_shapes=[pltpu.VMEM((nb, GDN_HEADS, GDN_DK, GDN_DV), F32)],
        compiler_params=_cparams(("arbitrary", "arbitrary")), name="gdn_prompt",
    )(v3(qg), v3(kg), v3(vg), v3(z), v3(gcb), norm_w)
    return o.reshape(n, Z_COLS), s


def _gdn_sample_kernel(qg_ref, kg_ref, vg_ref, z_ref, gcb_ref, nw_ref, s_in_ref, o_ref, s_out_ref):
    c = GDN_CHUNK
    n_sub = c // SAMPLE_SLOTS
    masks = _chunk_masks(c, SAMPLE_SLOTS)
    heads = range(GDN_HEADS)
    hs = lambda h: slice(h * GDN_DK, (h + 1) * GDN_DK)
    rs = lambda s: slice(s * SAMPLE_SLOTS, (s + 1) * SAMPLE_SLOTS)
    col = lambda off: [gcb_ref[:, off + h:off + h + 1] for h in heads]
    gcs, betas, gls = col(0), col(GDN_HEADS), col(2 * GDN_HEADS)
    us, ws, attns, q_decs, k_decs = _gdn_intra([qg_ref[:, hs(h)] for h in heads],
                                               [kg_ref[:, hs(h)] for h in heads],
                                               [vg_ref[:, hs(h)] for h in heads], gcs, gls, betas, *masks,
                                               base_only=SAMPLE_SLOTS <= INV_BASE)
    pairs = [(h, s) for h in heads for s in range(n_sub)]
    sts = [s_in_ref[s, h] for h, s in pairs]
    boths = [jnp.concatenate([ws[h][rs(s), :], q_decs[h][rs(s), :]], axis=0) for h, s in pairs]
    rr = _each(_bdot, boths, sts)
    gather = lambda h, part: jnp.concatenate(
        [rr[h * n_sub + s][part * SAMPLE_SLOTS:(part + 1) * SAMPLE_SLOTS, :] for s in range(n_sub)], axis=0)
    v_news = [us[h] - gather(h, 0) for h in heads]
    avs = _each(_bdot, attns, v_news)
    row = lax.broadcasted_iota(I32, (c, LANES), 0)
    seq_of_row = _div_pow2(row, SAMPLE_SLOTS)
    kds = [jnp.where(seq_of_row == s, k_decs[h], 0.0) for h, s in pairs]
    kvs = _each(_bdot_tn, kds, [v_news[h] for h, _ in pairs])
    egls = _each(jnp.exp, gls)
    for (h, s), st, kv in zip(pairs, sts, kvs):
        s_out_ref[s, h] = st * egls[h][s * SAMPLE_SLOTS:s * SAMPLE_SLOTS + 1, :] + kv
    nw = nw_ref[...]
    for h in heads:
        o_ref[:, hs(h)] = _gated_rms(gather(h, 1) + avs[h], z_ref[:, hs(h)], nw)


def _gdn_sample(qg, kg, vg, z, gcb, norm_w, state):
    n = qg.shape[0]
    n_sub = GDN_CHUNK // SAMPLE_SLOTS
    tok = lambda w: pl.BlockSpec((GDN_CHUNK, w), lambda i: (i, 0))
    st = pl.BlockSpec((n_sub, GDN_HEADS, GDN_DK, GDN_DV), lambda i: (i, 0, 0, 0))
    return pl.pallas_call(
        _gdn_sample_kernel,
        out_shape=[jax.ShapeDtypeStruct((n, Z_COLS), F32),
                   jax.ShapeDtypeStruct(state.shape, F32)],
        grid=(n // GDN_CHUNK,),
        in_specs=[tok(QK_COLS), tok(QK_COLS), tok(Z_COLS), tok(Z_COLS), tok(LANES),
                  pl.BlockSpec((1, GDN_DV), lambda i: (0, 0)), st],
        out_specs=[tok(Z_COLS), st],
        compiler_params=_cparams(("arbitrary",)), name="gdn_sample",
    )(qg, kg, vg, z, gcb, norm_w, state)


def _post_kernel(a_ref, g_ref, x_ref, wo_ref, ln_g_ref, ln_b_ref, wr_ref, x1_ref, route_ref, *, tm):
    sub = POST_SUB
    rows = [slice(j * sub, (j + 1) * sub) for j in range(tm // sub)]
    d = lambda a, b: jnp.dot(a, b, preferred_element_type=F32)
    mixes = [d(a_ref[r, :].astype(BF16), wo_ref[0:Q_COLS, :]) + d(g_ref[r, :].astype(BF16), wo_ref[Q_COLS:, :])
             for r in rows]
    x1s = [_layer_norm(DEEPNORM_ALPHA * x_ref[r, :] + mix, ln_g_ref[...], ln_b_ref[...])
           for r, mix in zip(rows, mixes)]
    for r, x1 in zip(rows, x1s):
        x1_ref[r, :] = x1
    w2 = wr_ref[...]
    lgs = []
    for x1 in x1s:
        xh = x1.astype(BF16)
        xm = (x1 - xh.astype(F32)).astype(BF16)
        both = d(xh, w2)
        lgs.append(both[:, :LANES] + both[:, LANES:] + d(xm, w2[:, :LANES]))
    for r, lg in zip(rows, lgs):
        route_ref[:, r] = jnp.transpose(_route(lg))[0:SUBLANES, :]


def _route(lg):
    lane = lax.broadcasted_iota(I32, lg.shape, 1)
    lane_f = lane.astype(F32)
    big = float(LANES)

    def first_max(vals, mask):
        v = jnp.where(mask, vals, NEG_BIG)
        mx = jnp.max(v, axis=1, keepdims=True)
        idx = jnp.min(jnp.where(mask & (v == mx), lane_f, big), axis=1, keepdims=True)
        return mx, idx

    gmask = lane < N_GROUPS
    gmax, gidx = first_max(lg, gmask)
    gden = jnp.sum(jnp.where(gmask, jnp.exp(lg - gmax), 0.0), axis=1, keepdims=True)
    g_top_p = 1.0 / gden
    e_lane = lane - N_GROUPS
    e_group = _div_pow2(jnp.maximum(e_lane, 0), EXPERTS_PER_GROUP).astype(F32)
    emask = (e_lane >= 0) & (e_lane < N_EXPERTS) & (e_group == gidx)
    m1, i1 = first_max(lg, emask)
    eden = jnp.sum(jnp.where(emask, jnp.exp(lg - m1), 0.0), axis=1, keepdims=True)
    m2, i2 = first_max(lg, emask & (lane_f != i1))
    p1 = 1.0 / eden
    p2 = jnp.exp(m2 - m1) / eden
    tot = p1 + p2
    gate1 = g_top_p * (p1 / tot)
    gate2 = g_top_p * (p2 / tot)
    return jnp.where(lane == 0, gate1,
                     jnp.where(lane == 1, gate2,
                               jnp.where(lane == 2, i1 - N_GROUPS,
                                         jnp.where(lane == 3, i2 - N_GROUPS, 0.0))))


def _post(attn_o, gdn_o, x, wts):
    n = x.shape[0]
    tm = min(POST_TILE, n)
    tok = lambda w: pl.BlockSpec((tm, w), lambda i: (i, 0))
    const = lambda a: pl.BlockSpec(a.shape, lambda i: (0,) * a.ndim)
    consts = [wts['wo'], wts['ln1_g'], wts['ln1_b'], wts['wr']]
    return pl.pallas_call(
        functools.partial(_post_kernel, tm=tm),
        out_shape=[jax.ShapeDtypeStruct((n, D_MODEL), F32), jax.ShapeDtypeStruct((SUBLANES, n), F32)],
        grid=(n // tm,),
        in_specs=[tok(Q_COLS), tok(Z_COLS), tok(D_MODEL)] + [const(a) for a in consts],
        out_specs=[tok(D_MODEL), pl.BlockSpec((SUBLANES, tm), lambda i: (0, i))],
        compiler_params=_cparams(("arbitrary",)), name="post_%d" % (n // tm),
    )(attn_o, gdn_o, x, *consts)


def _slab_loop(n, body):
    n_main = jnp.right_shift(n, int(math.log2(SLAB_UNROLL)))

    def main(i, c):
        for u in range(SLAB_UNROLL):
            body(i * SLAB_UNROLL + u, u)
        return c

    lax.fori_loop(0, n_main, main, 0)
    lax.fori_loop(n_main * SLAB_UNROLL, n, lambda j, c: (body(j, 0), c)[1], 0)


def _dispatch_kernel(dst_ref, nslab_ref, ztab_ref, zinfo_ref, slot_ref, gate_ref, *rest,
                     group_tiles, max_tiles):
    x_refs = rest[:len(group_tiles)]
    wg_ref, wu_ref, wd_ref, xs_ref, wgu_out, wd_out, pbuf, sem, zbuf, zsem = rest[len(group_tiles):]
    n_tiles = sum(group_tiles)
    g = pl.program_id(0)
    cur = lax.rem(g, 2)

    wgu_out[:, :, 0:EXPERT_FF] = wg_ref[...].astype(BF16)
    wgu_out[:, :, EXPERT_FF:] = wu_ref[...].astype(BF16)
    wd_out[...] = wd_ref[...].astype(BF16)

    def slab_copy(tile, buf_slot, j):
        d = pl.multiple_of(dst_ref[tile * PERM_SLABS + j], SLAB)
        src = pbuf.at[buf_slot, pl.ds(pl.multiple_of(j * SLAB, SLAB), SLAB), :]
        return pltpu.make_async_copy(src, xs_ref.at[pl.ds(d, SLAB), :], sem.at[buf_slot])

    def tail_copy(k):
        d = pl.multiple_of(ztab_ref[k], SLAB)
        return pltpu.make_async_copy(zbuf.at[pl.ds(0, SLAB), :], xs_ref.at[pl.ds(d, SLAB), :], zsem)

    def tile_copy(t):
        d = pl.multiple_of(t * ROW_TILE, ROW_TILE)
        return pltpu.make_async_copy(zbuf, xs_ref.at[pl.ds(d, ROW_TILE), :], zsem)

    @pl.when(g == 0)
    def _():
        zbuf[...] = jnp.zeros_like(zbuf)

    share = -(-ZERO_TABLE // n_tiles)
    k0 = g * share
    _slab_loop(jnp.clip(zinfo_ref[0] - k0, 0, share), lambda j, u: tail_copy(k0 + j).start(priority=1))

    @pl.when(zinfo_ref[1] + g < max_tiles)
    def _():
        tile_copy(zinfo_ref[1] + g).start(priority=1)

    x = x_refs[-1][...]
    bound = n_tiles
    for x_ref, nt in zip(x_refs[-2::-1], group_tiles[:0:-1]):
        bound -= nt
        x = jnp.where(g < bound, x_ref[...], x)

    r = lax.broadcasted_iota(I32, (PERM_ROWS, TOK_TILE), 0)
    sl = slot_ref[0]
    hit0, hit1 = r == sl[0:1, :], r == sl[1:2, :]
    onehot = jnp.where(hit0 | hit1, 1.0, 0.0).astype(BF16)
    gt = gate_ref[0]
    gcol = jnp.sum(jnp.where(hit0, gt[0:1, :], 0.0) + jnp.where(hit1, gt[1:2, :], 0.0),
                   axis=1, keepdims=True)
    pbuf[cur, :, 0:D_MODEL] = jnp.dot(onehot, x.astype(BF16), preferred_element_type=F32).astype(BF16)
    g_hi = gcol.astype(BF16).astype(F32)
    lane = lax.broadcasted_iota(I32, (PERM_ROWS, LANES), 1)
    pbuf[cur, :, D_MODEL:] = jnp.where(lane < LANES // 2, g_hi, gcol - g_hi).astype(BF16)

    @pl.when(g > 0)
    def _():
        _slab_loop(nslab_ref[g - 1], lambda j, u: slab_copy(g - 1, 1 - cur, j).wait())

    _slab_loop(nslab_ref[g], lambda j, u: slab_copy(g, cur, j).start(priority=u % 2))

    @pl.when(g == n_tiles - 1)
    def _():
        _slab_loop(nslab_ref[g], lambda j, u: slab_copy(g, cur, j).wait())
        lax.fori_loop(zinfo_ref[1] + n_tiles, max_tiles,
                      lambda t, c: (tile_copy(t).start(priority=1), c)[1], 0)
        _slab_loop(zinfo_ref[0], lambda k, u: tail_copy(k).wait())
        lax.fori_loop(zinfo_ref[1], max_tiles, lambda t, c: (tile_copy(t).wait(), c)[1], 0)


def _dispatch(plan, x1s, max_tiles, w_gate, w_up, w_down):
    group_tiles = tuple(x1.shape[0] // TOK_TILE for x1 in x1s)
    n_tiles = sum(group_tiles)
    tile = lambda i, d, ns, zt, zi: (i, 0, 0)
    in_specs = [pl.BlockSpec((1, TOP_K, TOK_TILE), tile), pl.BlockSpec((1, TOP_K, TOK_TILE), tile)]
    base = 0
    for nt in group_tiles:
        in_specs.append(pl.BlockSpec(
            (TOK_TILE, D_MODEL),
            lambda i, d, ns, zt, zi, base=base, nt=nt: (jnp.clip(i - base, 0, nt - 1), 0)))
        base += nt
    per = 1
    while per * n_tiles < N_EXPERTS:
        per *= 2
    wblk = lambda i, d, ns, zt, zi: (jnp.minimum(i, N_EXPERTS // per - 1), 0, 0)
    in_specs += [pl.BlockSpec((per, D_MODEL, EXPERT_FF), wblk), pl.BlockSpec((per, D_MODEL, EXPERT_FF), wblk),
                 pl.BlockSpec((per, EXPERT_FF, D_MODEL), wblk)]
    return pl.pallas_call(
        functools.partial(_dispatch_kernel, group_tiles=group_tiles, max_tiles=max_tiles),
        out_shape=[jax.ShapeDtypeStruct((max_tiles * ROW_TILE, XS_WORDS), BF16),
                   jax.ShapeDtypeStruct((N_EXPERTS, D_MODEL, 2 * EXPERT_FF), BF16),
                   jax.ShapeDtypeStruct((N_EXPERTS, EXPERT_FF, D_MODEL), BF16)],
        grid_spec=pltpu.PrefetchScalarGridSpec(
            num_scalar_prefetch=4, grid=(n_tiles,), in_specs=in_specs,
            out_specs=[pl.BlockSpec(memory_space=pl.ANY),
                       pl.BlockSpec((per, D_MODEL, 2 * EXPERT_FF), wblk),
                       pl.BlockSpec((per, EXPERT_FF, D_MODEL), wblk)],
            scratch_shapes=[pltpu.VMEM((2, PERM_ROWS, XS_WORDS), BF16), pltpu.SemaphoreType.DMA((2,)),
                            pltpu.VMEM((ROW_TILE, XS_WORDS), BF16), pltpu.SemaphoreType.DMA(())]),
        compiler_params=_cparams(("arbitrary",)), name="moe_dispatch",
    )(plan['slab_dst'], plan['nslab'], plan['ztab'], plan['zinfo'], plan['slot_rows'], plan['gate_rows'],
      *x1s, w_gate, w_up, w_down)


def _expert_kernel(t0_ref, nt_ref, nu_ref, wgu_ref, wd_ref, xs_ref, ye_ref,
                   xbuf, ybuf, zbuf, in_sem, out_sem, zsem, *, max_tiles):
    e = pl.program_id(0)
    n_used = nu_ref[0]

    def in_copy(t):
        slot = lax.rem(t, 2)
        src = xs_ref.at[pl.ds(pl.multiple_of(t * ROW_TILE, ROW_TILE), ROW_TILE), :]
        return pltpu.make_async_copy(src, xbuf.at[slot], in_sem.at[slot])

    def out_copy(t):
        slot = lax.rem(t, 2)
        dst = ye_ref.at[pl.ds(pl.multiple_of(t * ROW_TILE, ROW_TILE), ROW_TILE), :]
        return pltpu.make_async_copy(ybuf.at[slot], dst, out_sem.at[slot])

    def zero_copy(t):
        dst = ye_ref.at[pl.ds(pl.multiple_of(t * ROW_TILE, ROW_TILE), ROW_TILE), :]
        return pltpu.make_async_copy(zbuf, dst, zsem)

    @pl.when(e == 0)
    def _():
        in_copy(0).start()

    sub = EXPERT_SUB
    rows = [slice(j * sub, (j + 1) * sub) for j in range(ROW_TILE // sub)]
    d = lambda a, b: jnp.dot(a, b, preferred_element_type=F32)

    def tile_body(j, carry):
        t = t0_ref[e] + j
        slot = lax.rem(t, 2)

        @pl.when(t + 1 < n_used)
        def _():
            in_copy(t + 1).start()

        in_copy(t).wait()

        @pl.when(t >= 2)
        def _():
            out_copy(t - 2).wait()

        wgu, wd = wgu_ref[0], wd_ref[0]
        xs = [xbuf[slot, r, 0:D_MODEL] for r in rows]
        hs = [d(x, wgu) for x in xs]
        hhs = [(_silu(h[:, :EXPERT_FF]) * h[:, EXPERT_FF:]).astype(BF16) for h in hs]
        ys = [d(hh, wd) for hh in hhs]
        for r, y in zip(rows, ys):
            parts = xbuf[slot, r, D_MODEL:].astype(F32)
            gate = parts + pltpu.roll(parts, LANES // 2, axis=1)
            ybuf[slot, r, :] = (y * jnp.concatenate([gate] * (D_MODEL // LANES), axis=1)).astype(BF16)
        out_copy(t).start()
        return carry

    lax.fori_loop(0, nt_ref[e], tile_body, 0)

    @pl.when(e == pl.num_programs(0) - 1)
    def _():
        @pl.when(n_used >= 2)
        def _():
            out_copy(n_used - 2).wait()

        out_copy(n_used - 1).wait()
        zbuf[...] = jnp.zeros_like(zbuf)
        lax.fori_loop(n_used, max_tiles, lambda t, c: (zero_copy(t).start(), c)[1], 0)
        lax.fori_loop(n_used, max_tiles, lambda t, c: (zero_copy(t).wait(), c)[1], 0)


def _experts(plan, xs, w_gate_up, w_down):
    max_tiles = xs.shape[0] // ROW_TILE
    wsel = lambda e, t0, nt, nu: (e, 0, 0)
    return pl.pallas_call(
        functools.partial(_expert_kernel, max_tiles=max_tiles),
        out_shape=jax.ShapeDtypeStruct((xs.shape[0], D_MODEL), BF16),
        grid_spec=pltpu.PrefetchScalarGridSpec(
            num_scalar_prefetch=3, grid=(N_EXPERTS,),
            in_specs=[pl.BlockSpec((1, D_MODEL, 2 * EXPERT_FF), wsel),
                      pl.BlockSpec((1, EXPERT_FF, D_MODEL), wsel),
                      pl.BlockSpec(memory_space=pl.ANY)],
            out_specs=pl.BlockSpec(memory_space=pl.ANY),
            scratch_shapes=[pltpu.VMEM((2, ROW_TILE, XS_WORDS), BF16),
                            pltpu.VMEM((2, ROW_TILE, D_MODEL), BF16),
                            pltpu.VMEM((ROW_TILE, D_MODEL), BF16),
                            pltpu.SemaphoreType.DMA((2,)), pltpu.SemaphoreType.DMA((2,)),
                            pltpu.SemaphoreType.DMA(())]),
        compiler_params=_cparams(("arbitrary",)), name="moe_experts",
    )(plan['tile_start'], plan['tile_count'], plan['n_used'], w_gate_up, w_down, xs)


def _combine_kernel(dst_ref, nslab_ref, x1_ref, slot_ref, ye_ref, ln_g_ref, ln_b_ref, y_ref,
                    buf, sem, *, tile_base, n_tiles):
    i = pl.program_id(0)
    g = tile_base + i
    cur = lax.rem(i, 2)

    def slab_copy(tile, buf_slot, j):
        d = pl.multiple_of(dst_ref[tile * PERM_SLABS + j], SLAB)
        dst = buf.at[buf_slot, pl.ds(pl.multiple_of(j * SLAB, SLAB), SLAB), :]
        return pltpu.make_async_copy(ye_ref.at[pl.ds(d, SLAB), :], dst, sem.at[buf_slot])

    @pl.when(i == 0)
    def _():
        buf[...] = jnp.zeros_like(buf)
        _slab_loop(nslab_ref[g], lambda j, u: slab_copy(g, cur, j).start(priority=u % 2))

    @pl.when(i + 1 < n_tiles)
    def _():
        _slab_loop(nslab_ref[g + 1], lambda j, u: slab_copy(g + 1, 1 - cur, j).start(priority=u % 2))

    _slab_loop(nslab_ref[g], lambda j, u: slab_copy(g, cur, j).wait())

    col = lax.broadcasted_iota(I32, (TOK_TILE, PERM_ROWS), 1)
    sl = slot_ref[0]
    diag = (lax.broadcasted_iota(I32, (TOK_TILE, TOK_TILE), 0)
            == lax.broadcasted_iota(I32, (TOK_TILE, TOK_TILE), 1))
    as_col = lambda row: jnp.sum(jnp.where(diag, row, 0), axis=1, keepdims=True)
    pick = jnp.where((col == as_col(sl[0:1, :])) | (col == as_col(sl[1:2, :])), 1.0, 0.0).astype(BF16)
    moe = jnp.dot(pick, buf[cur], preferred_element_type=F32)
    y_ref[...] = _layer_norm(DEEPNORM_ALPHA * x1_ref[...] + moe, ln_g_ref[...], ln_b_ref[...])


def _combine(plan, tile_base, x1, ye, ln_g, ln_b):
    n = x1.shape[0]
    n_tiles = n // TOK_TILE
    tok = lambda w: pl.BlockSpec((TOK_TILE, w), lambda i, d, ns: (i, 0))
    const = lambda a: pl.BlockSpec(a.shape, lambda i, d, ns: (0,) * a.ndim)
    return pl.pallas_call(
        functools.partial(_combine_kernel, tile_base=tile_base, n_tiles=n_tiles),
        out_shape=jax.ShapeDtypeStruct((n, D_MODEL), F32),
        grid_spec=pltpu.PrefetchScalarGridSpec(
            num_scalar_prefetch=2, grid=(n_tiles,),
            in_specs=[tok(D_MODEL),
                      pl.BlockSpec((1, TOP_K, TOK_TILE), lambda i, d, ns: (tile_base + i, 0, 0)),
                      pl.BlockSpec(memory_space=pl.ANY), const(ln_g), const(ln_b)],
            out_specs=tok(D_MODEL),
            scratch_shapes=[pltpu.VMEM((2, PERM_ROWS, D_MODEL), BF16), pltpu.SemaphoreType.DMA((2,))]),
        compiler_params=_cparams(("arbitrary",)), name="moe_combine_%d" % tile_base,
    )(plan['slab_dst'], plan['nslab'], x1, plan['slot_rows'], ye, ln_g, ln_b)


def _routing_plan(ids, gates):
    nt = ids.shape[1] // TOK_TILE
    pairs = TOP_K * TOK_TILE
    ex = jnp.arange(N_EXPERTS, dtype=I32)
    per_tile = lambda a: jnp.swapaxes(a.reshape(TOP_K, nt, TOK_TILE), 0, 1)
    flat = per_tile(ids).reshape(nt, pairs)
    onehot = (flat[:, None, :] == ex[None, :, None])
    p = np.arange(pairs)
    triu = jnp.asarray(p[:, None] <= p[None, :], BF16)
    csum = jnp.dot(onehot.astype(BF16).reshape(nt * N_EXPERTS, pairs), triu,
                   preferred_element_type=F32).astype(I32).reshape(nt, N_EXPERTS, pairs)
    oh = onehot.astype(I32)
    rank = jnp.sum(oh * (csum - 1), axis=1)
    cnt = csum[:, :, -1]
    cpad = (cnt + SLAB - 1) // SLAB * SLAB
    seg_end = jnp.cumsum(cpad, axis=1)
    seg_off = seg_end - cpad
    slot = jnp.sum(oh * seg_off[:, :, None], axis=1) + rank
    run_end = jnp.cumsum(cpad, axis=0)
    ntiles_e = (run_end[-1] + ROW_TILE - 1) // ROW_TILE
    tile_end = jnp.cumsum(ntiles_e)
    dst_run = ((tile_end - ntiles_e) * ROW_TILE)[None, :] + run_end - cpad
    j8 = jnp.arange(PERM_SLABS, dtype=I32) * SLAB
    e_of = jnp.minimum(jnp.sum((j8[None, :, None] >= seg_end[:, None, :]).astype(I32), axis=2),
                       N_EXPERTS - 1)
    sel = (e_of[:, :, None] == ex).astype(I32)
    slab_dst = jnp.sum(sel * (dst_run - seg_off)[:, None, :], axis=2) + j8[None, :]
    n_used = tile_end[-1]
    row_start = (tile_end - ntiles_e) * ROW_TILE
    tail_cnt = (ntiles_e * ROW_TILE - run_end[-1]) // SLAB
    tail_end = jnp.cumsum(tail_cnt)
    k = jnp.arange(ZERO_TABLE, dtype=I32)
    e_k = jnp.minimum(jnp.sum((k[:, None] >= tail_end[None, :]).astype(I32), axis=1), N_EXPERTS - 1)
    base_k = jnp.sum((e_k[:, None] == ex).astype(I32)
                     * (row_start + run_end[-1] - SLAB * (tail_end - tail_cnt))[None, :], axis=1)
    return dict(
        slab_dst=slab_dst.reshape(-1).astype(I32), nslab=(seg_end[:, -1] // SLAB).astype(I32),
        ztab=(base_k + SLAB * k).astype(I32), zinfo=jnp.stack([tail_end[-1], n_used]).astype(I32),
        slot_rows=slot.reshape(nt, TOP_K, TOK_TILE).astype(I32),
        gate_rows=per_tile(gates).astype(F32),
        tile_start=(tile_end - ntiles_e).astype(I32), tile_count=ntiles_e.astype(I32),
        n_used=n_used.reshape(1).astype(I32))


def _max_row_tiles(n_tokens):
    rows = TOP_K * n_tokens + (n_tokens // TOK_TILE) * N_EXPERTS * (SLAB - 1)
    return (rows + ROW_TILE - 1) // ROW_TILE + N_EXPERTS


def _moe(x1s, routes, wts):
    ids = jnp.concatenate([r[TOP_K:2 * TOP_K, :] for r in routes], axis=1).astype(I32)
    gates = jnp.concatenate([r[0:TOP_K, :] for r in routes], axis=1)
    plan = _routing_plan(ids, gates)
    max_tiles = _max_row_tiles(ids.shape[1])
    bases = [0]
    for x1 in x1s[:-1]:
        bases.append(bases[-1] + x1.shape[0] // TOK_TILE)
    xs, w_gate_up, w_down = _dispatch(plan, x1s, max_tiles, wts['w_gate'], wts['w_up'], wts['w_down'])
    ye = _experts(plan, xs, w_gate_up, w_down)
    return [_combine(plan, base, x1, ye, wts['ln2_g'], wts['ln2_b']) for base, x1 in zip(bases, x1s)]


def _prep_weights(w_in, w_out, conv_w, a_log, dt_bias, gdn_norm_w, ln1_g, ln1_b, w_router_group,
                  w_router_expert, w_gate, w_up, w_down, ln2_g, ln2_b):
    pad_row = lambda v: jnp.pad(v.astype(F32), (0, LANES - v.shape[0]))[None, :]
    wr = jnp.pad(jnp.concatenate([w_router_group, w_router_expert], axis=1),
                 ((0, 0), (0, LANES - N_GROUPS - N_EXPERTS)))
    wr_hi = wr.astype(BF16)
    wr_mid = (wr - wr_hi.astype(F32)).astype(BF16)
    group = ATTN_HEADS // ATTN_KV_HEADS
    wq = w_in[:, :Q_COLS].reshape(D_MODEL, ATTN_KV_HEADS, group, HEAD_DIM)
    wq = jnp.swapaxes(wq, 1, 2).reshape(D_MODEL, Q_COLS)
    w_all = jnp.concatenate([wq, w_in[:, Q_COLS:]], axis=1)
    w_all = jnp.pad(w_all, ((0, 0), (0, IN_COLS_PAD - w_all.shape[1]))).astype(BF16)
    wo_q = w_out[:Q_COLS].reshape(ATTN_KV_HEADS, group, HEAD_DIM, D_MODEL)
    wo_q = jnp.swapaxes(wo_q, 0, 1).reshape(Q_COLS, D_MODEL)
    wo = jnp.concatenate([wo_q, w_out[Q_COLS:]], axis=0)
    return dict(
        w_all=w_all, convw=conv_w.astype(F32), alog=pad_row(a_log), dtb=pad_row(dt_bias),
        norm_w=gdn_norm_w.astype(F32)[None, :], wo=wo.astype(BF16),
        ln1_g=ln1_g[None, :], ln1_b=ln1_b[None, :], wr=jnp.concatenate([wr_hi, wr_mid], axis=1),
        w_gate=w_gate, w_up=w_up, w_down=w_down, ln2_g=ln2_g[None, :], ln2_b=ln2_b[None, :])


def _layer(x_prompt, x_sample, cache_k, cache_v, state_gdn, state_conv, wts):
    bp, sp, _ = x_prompt.shape
    bs, ts, _ = x_sample.shape
    n_p = bp * sp

    xp = x_prompt.reshape(n_p, D_MODEL)
    (q, k, v, qg, kg, vg, z, gcb, utail) = _proj(xp, np.arange(sp), wts, GDN_CHUNK, bp)
    attn_p = _attn_prompt(q, k, v, wts['sinks'], bp)
    gdn_p, s_p = _gdn_prompt(qg, kg, vg, z, gcb, wts['norm_w'], bp)
    last_win = lambda a: a.reshape(bp, sp, KV_COLS)[:, sp - WINDOW:].reshape(bp, WINDOW, ATTN_KV_HEADS,
                                                                            HEAD_DIM)
    new_k_p, new_v_p = last_win(k), last_win(v)
    tiles_per_seq = sp // min(PROJ_TILE, sp)
    conv_p = utail.reshape(bp, tiles_per_seq, SUBLANES, CONV_DIM)[:, -1, SUBLANES - (CONV_W - 1):]

    lo, hi = SAMPLE_FIRST, SAMPLE_FIRST + ts
    xs_rows = jnp.pad(x_sample, ((0, 0), (lo, SAMPLE_SLOTS - hi), (0, 0))).reshape(bs * SAMPLE_SLOTS, D_MODEL)
    hist = jnp.pad(state_conv, ((0, 0), (0, SAMPLE_SLOTS - lo), (0, 0))).reshape(bs * SAMPLE_SLOTS, CONV_DIM)
    slot = np.arange(SAMPLE_SLOTS)
    valid = jnp.asarray(np.tile((slot >= lo) & (slot < hi), bs)[:, None], F32)
    pos_s = np.tile(PAST_LEN + slot - lo, bs)
    (q, k, v, qg, kg, vg, z, gcb, u_s) = _proj(xs_rows, pos_s, wts, SAMPLE_SLOTS, 1, hist, valid)
    ck = cache_k.reshape(bs, WINDOW, KV_COLS)
    cv = cache_v.reshape(bs, WINDOW, KV_COLS)
    attn_s, kwin, vwin = _attn_sample(q, k, v, ck, cv, wts['sinks'], bs, ts)
    gdn_s, s_s = _gdn_sample(qg, kg, vg, z, gcb, wts['norm_w'], state_gdn)
    real = lambda a: a.reshape(bs, SAMPLE_SLOTS, -1)[:, lo:hi]
    new_k_s = kwin.reshape(bs, WINDOW, ATTN_KV_HEADS, HEAD_DIM)
    new_v_s = vwin.reshape(bs, WINDOW, ATTN_KV_HEADS, HEAD_DIM)
    conv_s = u_s.reshape(bs, SAMPLE_SLOTS, CONV_DIM)[:, hi - (CONV_W - 1):hi]

    x1_p, route_p = _post(attn_p, gdn_p, xp, wts)
    x1_s, route_s = _post(real(attn_s).reshape(bs * ts, Q_COLS), real(gdn_s).reshape(bs * ts, Z_COLS),
                          x_sample.reshape(bs * ts, D_MODEL), wts)
    y_p, y_s = _moe([x1_p, x1_s], [route_p, route_s], wts)
    return (y_p.reshape(bp, sp, D_MODEL), y_s.reshape(bs, ts, D_MODEL), new_k_p, new_v_p, s_p, conv_p,
            new_k_s, new_v_s, s_s, conv_s)


def kernel(x_prompt, x_sample, cache_attn_k, cache_attn_v, state_gdn, state_conv, w_in, w_out,
           attn_sinks, conv_w, a_log, dt_bias, gdn_norm_w, ln1_g, ln1_b, w_router_group,
           w_router_expert, w_gate, w_up, w_down, ln2_g, ln2_b):
    assert w_in.shape[0] == DEPTH
    l = 0
    wts = _prep_weights(w_in[l], w_out[l], conv_w[l], a_log[l], dt_bias[l], gdn_norm_w[l], ln1_g[l],
                        ln1_b[l], w_router_group[l], w_router_expert[l], w_gate[l], w_up[l],
                        w_down[l], ln2_g[l], ln2_b[l])
    wts['sinks'] = attn_sinks[l]
    outs = _layer(x_prompt, x_sample, cache_attn_k[l], cache_attn_v[l], state_gdn[l], state_conv[l], wts)
    (y_p, y_s, k_p, v_p, s_p, c_p, k_s, v_s, s_s, c_s) = outs
    add = lambda a: a[None]
    return (y_p, y_s, add(k_p), add(v_p), add(s_p), add(c_p), add(k_s), add(v_s), add(s_s), add(c_s))
```

```python
import functools
import math

import jax
import jax.numpy as jnp
import numpy as np
from jax import lax
from jax.experimental import pallas as pl
from jax.experimental.pallas import tpu as pltpu

F32 = jnp.float32
BF16 = jnp.bfloat16
I32 = jnp.int32

D_MODEL = 1024
ATTN_HEADS = 8
ATTN_KV_HEADS = 2
HEAD_DIM = 64
WINDOW = 128
ROT_DIM = HEAD_DIM // 4
ROPE_THETA = 500000.0
GDN_HEADS = 4
GDN_DK = 128
GDN_DV = 128
CONV_W = 4
QK_COLS = GDN_HEADS * GDN_DK
CONV_DIM = 2 * QK_COLS + GDN_HEADS * GDN_DV
Z_COLS = GDN_HEADS * GDN_DV
Q_COLS = ATTN_HEADS * HEAD_DIM
KV_COLS = ATTN_KV_HEADS * HEAD_DIM
N_GROUPS = 4
EXPERTS_PER_GROUP = 8
N_EXPERTS = N_GROUPS * EXPERTS_PER_GROUP
TOP_K = 2
EXPERT_FF = 256
NORM_EPS = 1e-5
L2_EPS = 1e-6
DEPTH = 1
DEEPNORM_ALPHA = (2 * DEPTH) ** 0.25
PAST_LEN = 8192

LANES = 128
SUBLANES = 8
IN_SPLITS = (0, Q_COLS + 2 * KV_COLS, Q_COLS + 2 * KV_COLS + CONV_DIM,
             Q_COLS + 2 * KV_COLS + CONV_DIM + Z_COLS, Q_COLS + 2 * KV_COLS + CONV_DIM + Z_COLS + LANES)
IN_COLS_PAD = IN_SPLITS[-1]
TOK_TILE = 512
PROJ_TILE = 512
PROJ_SUB = 128
POST_TILE = 1024
POST_SUB = 256
GDN_CHUNK = 128
GDN_SEQ_PER_STEP = 4
ATTN_BLOCKS_PER_STEP = 4
ATTN_SEQS_PER_STEP = 16
INV_BASE = 16
SAMPLE_SLOTS = 8
SAMPLE_FIRST = CONV_W - 1
ROW_TILE = 512
EXPERT_SUB = 256
SLAB_UNROLL = 4
COMBINE_SUB = 128
SLAB = 16
PERM_ROWS = TOP_K * TOK_TILE + N_EXPERTS * SLAB
PERM_SLABS = PERM_ROWS // SLAB
XS_WORDS = D_MODEL + LANES
ZERO_TABLE = N_EXPERTS * (ROW_TILE // SLAB)
VMEM_LIMIT = 48 * 1024 * 1024
NEG_BIG = -1e30


def _cparams(sem):
    return pltpu.CompilerParams(dimension_semantics=sem, vmem_limit_bytes=VMEM_LIMIT)


def _bdot(a, b):
    return jnp.dot(a.astype(BF16), b.astype(BF16), preferred_element_type=F32)


def _bdot_nt(a, b):
    return lax.dot_general(a.astype(BF16), b.astype(BF16), (((1,), (1,)), ((), ())),
                           preferred_element_type=F32)


def _bdot_tn(a, b):
    return lax.dot_general(a.astype(BF16), b.astype(BF16), (((0,), (0,)), ((), ())),
                           preferred_element_type=F32)


def _div_pow2(x, n):
    return jnp.right_shift(x, int(math.log2(n)))


def _mod_pow2(x, n):
    return jnp.bitwise_and(x, n - 1)


def _split3(x):
    hi = x.astype(BF16)
    r = x - hi.astype(F32)
    mid = r.astype(BF16)
    lo = (r - mid.astype(F32)).astype(BF16)
    return hi, mid, lo


def _dot_exact_lhs01(m01, x):
    hi, mid, lo = _split3(x)
    d = lambda t: jnp.dot(m01, t, preferred_element_type=F32)
    return d(hi) + d(mid) + d(lo)


def _sigmoid(x):
    return 1.0 / (1.0 + jnp.exp(-x))


def _silu(x):
    return x * _sigmoid(x)


def _softplus(x):
    return jnp.maximum(x, 0.0) + jnp.log1p(jnp.exp(-jnp.abs(x)))


def _layer_norm(h, g, b):
    mu = jnp.mean(h, axis=-1, keepdims=True)
    d = h - mu
    var = jnp.mean(d * d, axis=-1, keepdims=True)
    return d * lax.rsqrt(var + NORM_EPS) * g + b


def _proj_kernel(*refs, tm, has_hist, full_u, one_segment):
    it = iter(refs)
    x_ref, cos_ref, sin_ref = next(it), next(it), next(it)
    w_ref = next(it)
    convw_ref, alog_ref, dtb_ref, tri_ref, seg_ref = next(it), next(it), next(it), next(it), next(it)
    hist_ref = valid_ref = None
    if has_hist:
        hist_ref, valid_ref = next(it), next(it)
    q_ref, k_ref, v_ref = next(it), next(it), next(it)
    qg_ref, kg_ref, vg_ref, z_ref, gcb_ref, u_ref = (next(it) for _ in range(6))
    ubuf = next(it)

    t = pl.program_id(1)
    sub = PROJ_SUB
    rows = [slice(j * sub, (j + 1) * sub) for j in range(tm // sub)]
    lane = lax.broadcasted_iota(I32, (sub, LANES), 1)
    first_half = _mod_pow2(lane, HEAD_DIM) < (ROT_DIM // 2)

    @pl.when(t == 0)
    def _():
        ubuf[0:SUBLANES, :] = jnp.zeros((SUBLANES, CONV_DIM), F32)

    @pl.when(t > 0)
    def _():
        ubuf[0:SUBLANES, :] = ubuf[tm:tm + SUBLANES, :]

    dots = []
    for r in rows:
        xb = x_ref[r, :].astype(BF16)
        dots.append([jnp.dot(xb, w_ref[:, lo:hi], preferred_element_type=F32)
                     for lo, hi in zip(IN_SPLITS[:-1], IN_SPLITS[1:])])

    def l2n(s):
        return s * lax.rsqrt(jnp.sum(s * s, axis=1, keepdims=True) + L2_EPS)

    for r, (pq, u, z, ab) in zip(rows, dots):
        cosv, sinv = cos_ref[r, :], sin_ref[r, :]

        def rope(s):
            sw = jnp.where(first_half, pltpu.roll(s, LANES - ROT_DIM // 2, axis=1),
                           pltpu.roll(s, ROT_DIM // 2, axis=1))
            return s * cosv + sw * sinv

        for j in range(Q_COLS // LANES):
            q_ref[r, j * LANES:(j + 1) * LANES] = rope(pq[:, j * LANES:(j + 1) * LANES])
        k_ref[r, :] = rope(pq[:, Q_COLS:Q_COLS + KV_COLS])
        v_ref[r, :] = pq[:, Q_COLS + KV_COLS:Q_COLS + 2 * KV_COLS]
        z_ref[r, :] = z

        if has_hist:
            u = u + hist_ref[r, :]
        if full_u:
            u_ref[r, :] = u
        elif r.stop == tm:
            u_ref[...] = u[sub - SUBLANES:, :]
        base = SUBLANES + r.start
        ubuf[base:base + sub, :] = u
        acc = u * convw_ref[CONV_W - 1:CONV_W, :]
        for j in range(1, CONV_W):
            acc = acc + ubuf[base - j:base - j + sub, :] * convw_ref[CONV_W - 1 - j:CONV_W - j, :]
        c = _silu(acc)
        if has_hist:
            c = c * valid_ref[r, :]
        for h in range(GDN_HEADS):
            sl = slice(h * GDN_DK, (h + 1) * GDN_DK)
            qg_ref[r, sl] = l2n(c[:, sl]) * (GDN_DK ** -0.5)
            kg_ref[r, sl] = l2n(c[:, QK_COLS + h * GDN_DK:QK_COLS + (h + 1) * GDN_DK])
        vg_ref[r, :] = c[:, 2 * QK_COLS:]

        g = -jnp.exp(alog_ref[...]) * _softplus(ab + dtb_ref[...])
        beta = _sigmoid(ab)
        if has_hist:
            g = g * valid_ref[r, :]
            beta = beta * valid_ref[r, :]
        g = jnp.where(lane < GDN_HEADS, g, 0.0)
        gc = _dot_exact_lhs01(tri_ref[...], g)
        if one_segment:
            gl = jnp.broadcast_to(gc[sub - 1:sub, :], (sub, LANES))
        else:
            gl = _dot_exact_lhs01(seg_ref[...], g)
        gcb_ref[r, :] = jnp.where(lane < GDN_HEADS, gc,
                                  jnp.where(lane < 2 * GDN_HEADS, beta,
                                            jnp.where(lane < 3 * GDN_HEADS,
                                                      pltpu.roll(gl, 2 * GDN_HEADS, axis=1), 0.0)))


def _rope_tables(pos):
    half = ROT_DIM // 2
    pos = np.asarray(pos, np.float64)
    inv_freq = ROPE_THETA ** (-np.arange(half, dtype=np.float64) * 2.0 / ROT_DIM)
    ang = pos[:, None] * inv_freq[None, :]
    cos, sin = np.cos(ang), np.sin(ang)
    p = pos.shape[0]
    cpat = np.concatenate([cos, cos, np.ones((p, HEAD_DIM - ROT_DIM))], axis=1)
    spat = np.concatenate([-sin, sin, np.zeros((p, HEAD_DIM - ROT_DIM))], axis=1)
    rep = (1, LANES // HEAD_DIM)
    return jnp.asarray(np.tile(cpat, rep), F32), jnp.asarray(np.tile(spat, rep), F32)


def _segment_matrices(tm, seg_len):
    i = np.arange(tm)
    same = (i[:, None] // seg_len) == (i[None, :] // seg_len)
    tri = same & (i[None, :] <= i[:, None])
    return jnp.asarray(tri, BF16), jnp.asarray(same, BF16)


def _proj(x, pos, wts, seg_len, n_seq, hist=None, valid=None):
    n = x.shape[0]
    rows = n // n_seq
    tm = min(PROJ_TILE, rows)
    nt = rows // tm
    has_hist = hist is not None
    cos_t, sin_t = _rope_tables(pos)
    tri, seg = _segment_matrices(PROJ_SUB, seg_len)

    tok = lambda w: pl.BlockSpec((tm, w), lambda b, t: (b * nt + t, 0))
    const = lambda a: pl.BlockSpec(a.shape, lambda b, t: (0,) * a.ndim)
    in_arrays = [x, cos_t, sin_t, wts['w_all'], wts['convw'], wts['alog'], wts['dtb'], tri, seg]
    in_specs = [tok(D_MODEL), pl.BlockSpec((tm, LANES), lambda b, t: (t, 0)),
                pl.BlockSpec((tm, LANES), lambda b, t: (t, 0))] + [const(a) for a in in_arrays[3:]]
    if has_hist:
        in_arrays += [hist, valid]
        in_specs += [tok(CONV_DIM), tok(1)]
    u_rows = n if has_hist else (n // tm) * SUBLANES
    u_block = tm if has_hist else SUBLANES
    out_shape = [jax.ShapeDtypeStruct((n, Q_COLS), F32), jax.ShapeDtypeStruct((n, KV_COLS), F32),
                 jax.ShapeDtypeStruct((n, KV_COLS), F32), jax.ShapeDtypeStruct((n, QK_COLS), F32),
                 jax.ShapeDtypeStruct((n, QK_COLS), F32), jax.ShapeDtypeStruct((n, Z_COLS), F32),
                 jax.ShapeDtypeStruct((n, Z_COLS), F32), jax.ShapeDtypeStruct((n, LANES), F32),
                 jax.ShapeDtypeStruct((u_rows, CONV_DIM), F32)]
    out_specs = [tok(Q_COLS), tok(KV_COLS), tok(KV_COLS), tok(QK_COLS), tok(QK_COLS), tok(Z_COLS),
                 tok(Z_COLS), tok(LANES),
                 pl.BlockSpec((u_block, CONV_DIM), lambda b, t: (b * nt + t, 0))]
    return pl.pallas_call(
        functools.partial(_proj_kernel, tm=tm, has_hist=has_hist, full_u=has_hist,
                          one_segment=seg_len == PROJ_SUB),
        out_shape=out_shape, grid=(n_seq, nt), in_specs=in_specs, out_specs=out_specs,
        scratch_shapes=[pltpu.VMEM((tm + SUBLANES, CONV_DIM), F32)],
        compiler_params=_cparams(("arbitrary", "arbitrary")),
        name="proj_hist" if has_hist else "proj",
    )(*in_arrays)


def _attn_blocks(qs, kcats, vcats, biases, sink, tq):
    lane = lax.broadcasted_iota(I32, (tq, LANES), 1)
    low = lane < HEAD_DIM
    n_slab = Q_COLS // LANES

    def stack(q):
        slabs = [q[:, j * LANES:(j + 1) * LANES] * (HEAD_DIM ** -0.5) for j in range(n_slab)]
        parts = ([jnp.where(low, s, 0.0) for s in slabs] + [jnp.where(low, 0.0, s) for s in slabs])
        return jnp.concatenate(parts, axis=0).astype(BF16)

    def unstack(o8):
        return [jnp.where(low, o8[j * tq:(j + 1) * tq, :], o8[(n_slab + j) * tq:(n_slab + j + 1) * tq, :])
                for j in range(n_slab)]

    rows = ATTN_HEADS * tq
    half = rows // 2
    klow = lax.broadcasted_iota(I32, (2 * WINDOW, LANES), 1) < HEAD_DIM
    one = jnp.ones((), BF16)
    q8s = _each(stack, qs)
    ss = _each(lambda q8, kc, b: _bdot_nt(q8, kc) + b, q8s, kcats, biases)
    ms = _each(lambda s: jnp.maximum(jnp.broadcast_to(jnp.max(s, axis=1, keepdims=True), (rows, LANES)),
                                     sink), ss)
    ps = _each(lambda s, m: jnp.exp(s - jnp.concatenate([m, m], axis=1)).astype(BF16), ss, ms)
    pv0 = _each(lambda p, vc: jnp.dot(p[:half], jnp.where(klow, vc, one), preferred_element_type=F32),
                ps, vcats)
    pv1 = _each(lambda p, vc: jnp.dot(p[half:], jnp.where(klow, one, vc), preferred_element_type=F32),
                ps, vcats)
    pvs = _each(lambda a, b: jnp.concatenate([a, b], axis=0), pv0, pv1)
    o8s = _each(lambda pv, m: pv / (pltpu.roll(pv, HEAD_DIM, axis=1) + jnp.exp(sink - m)), pvs, ms)
    return _each(unstack, o8s)


def _attn_prompt_kernel(q_ref, kc_ref, vc_ref, kp_ref, vp_ref, bias0_ref, bias_ref, sink_ref, o_ref, *,
                        nblk):
    kall = jnp.concatenate([kp_ref[...], kc_ref[...]], axis=0).astype(BF16)
    vall = jnp.concatenate([vp_ref[...], vc_ref[...]], axis=0).astype(BF16)
    win = lambda a, j: a[j * WINDOW:(j + 2) * WINDOW, :]
    qs = [q_ref[j * WINDOW:(j + 1) * WINDOW, :] for j in range(nblk)]
    biases = [bias0_ref[0]] + [bias_ref[...]] * (nblk - 1)
    outs = _attn_blocks(qs, [win(kall, j) for j in range(nblk)], [win(vall, j) for j in range(nblk)],
                        biases, sink_ref[...], WINDOW)
    for j, slabs in enumerate(outs):
        for c, slab in enumerate(slabs):
            o_ref[j * WINDOW:(j + 1) * WINDOW, c * LANES:(c + 1) * LANES] = slab


def _attn_sample_kernel(q_ref, kc_ref, vc_ref, kp_ref, vp_ref, bias_ref, sink_ref, o_ref, kw_ref, vw_ref,
                        *, nseq, n_new):
    tq = SAMPLE_SLOTS
    zpad = jnp.zeros((WINDOW - tq, LANES), F32)
    rows = lambda ref, j: ref[j * tq:(j + 1) * tq, :]
    cat = lambda pref, cref, j: jnp.concatenate([pref[j], rows(cref, j), zpad], axis=0).astype(BF16)
    outs = _attn_blocks([rows(q_ref, j) for j in range(nseq)],
                        [cat(kp_ref, kc_ref, j) for j in range(nseq)],
                        [cat(vp_ref, vc_ref, j) for j in range(nseq)],
                        [bias_ref[...]] * nseq, sink_ref[...], tq)
    for j, slabs in enumerate(outs):
        for c, slab in enumerate(slabs):
            o_ref[j * tq:(j + 1) * tq, c * LANES:(c + 1) * LANES] = slab
    row = lax.broadcasted_iota(I32, (WINDOW, LANES), 0)
    keep = WINDOW - n_new
    for pref, cref, wref in ((kp_ref, kc_ref, kw_ref), (vp_ref, vc_ref, vw_ref)):
        for j in range(nseq):
            new = jnp.concatenate([rows(cref, j), zpad], axis=0)
            wref[j] = jnp.where(row < keep, pltpu.roll(pref[j], keep, axis=0),
                                pltpu.roll(new, keep - SAMPLE_FIRST, axis=0))


def _sink_rows(sinks, tq):
    return jnp.broadcast_to(jnp.repeat(sinks.astype(F32), tq)[:, None], (ATTN_HEADS * tq, LANES))


def _attn_bias(tq, q_off, k_lo, k_hi, has_prev):
    qi = (np.arange(ATTN_HEADS * tq) % tq)[:, None]
    c = np.arange(2 * WINDOW)[None, :]
    cj = c - WINDOW
    vis_prev = (c < WINDOW) & (c > qi - q_off) & has_prev
    vis_cur = (c >= WINDOW) & (cj <= qi) & (cj >= k_lo) & (cj <= k_hi)
    return np.where(vis_prev | vis_cur, 0.0, NEG_BIG).astype(np.float32)


def _attn_prompt(q, k, v, sinks, n_seq):
    n = q.shape[0]
    nb = n // n_seq // WINDOW
    nblk = min(ATTN_BLOCKS_PER_STEP, nb)
    steps = nb // nblk
    tq = nblk * WINDOW
    cur = lambda w: pl.BlockSpec((tq, w), lambda b, i: (b * steps + i, 0))
    prev = pl.BlockSpec((WINDOW, LANES), lambda b, i: (b * nb + jnp.maximum(i * nblk - 1, 0), 0))
    bias2 = jnp.asarray(np.stack([_attn_bias(WINDOW, 0, 0, WINDOW - 1, False),
                                  _attn_bias(WINDOW, 0, 0, WINDOW - 1, True)]))
    rows = ATTN_HEADS * WINDOW
    return pl.pallas_call(
        functools.partial(_attn_prompt_kernel, nblk=nblk),
        out_shape=jax.ShapeDtypeStruct((n, Q_COLS), F32), grid=(n_seq, steps),
        in_specs=[cur(Q_COLS), cur(LANES), cur(LANES), prev, prev,
                  pl.BlockSpec((1, rows, 2 * WINDOW), lambda b, i: (jnp.minimum(i, 1), 0, 0)),
                  pl.BlockSpec((rows, 2 * WINDOW), lambda b, i: (0, 0)),
                  pl.BlockSpec((rows, LANES), lambda b, i: (0, 0))],
        out_specs=cur(Q_COLS),
        compiler_params=_cparams(("arbitrary", "arbitrary")), name="attn_prompt",
    )(q, k, v, k, v, bias2, bias2[1], _sink_rows(sinks, WINDOW))


def _attn_sample(q, k, v, cache_k, cache_v, sinks, n_seq, n_new):
    tq = SAMPLE_SLOTS
    nseq = min(ATTN_SEQS_PER_STEP, n_seq)
    cur = lambda w: pl.BlockSpec((nseq * tq, w), lambda b: (b, 0))
    prev = pl.BlockSpec((nseq, WINDOW, LANES), lambda b: (b, 0, 0))
    bias = jnp.asarray(_attn_bias(tq, SAMPLE_FIRST, SAMPLE_FIRST, SAMPLE_FIRST + 3, True))
    win = jax.ShapeDtypeStruct((n_seq, WINDOW, LANES), F32)
    return pl.pallas_call(
        functools.partial(_attn_sample_kernel, nseq=nseq, n_new=n_new),
        out_shape=[jax.ShapeDtypeStruct((n_seq * tq, Q_COLS), F32), win, win], grid=(n_seq // nseq,),
        in_specs=[cur(Q_COLS), cur(LANES), cur(LANES), prev, prev,
                  pl.BlockSpec(bias.shape, lambda b: (0, 0)),
                  pl.BlockSpec((ATTN_HEADS * tq, LANES), lambda b: (0, 0))],
        out_specs=[cur(Q_COLS), prev, prev],
        compiler_params=_cparams(("arbitrary",)), name="attn_sample",
    )(q, k, v, cache_k, cache_v, bias, _sink_rows(sinks, tq))


def _each(f, *lists):
    return [f(*args) for args in zip(*lists)]


def _unit_lower_inverse(ms, eye, same_base, base_only=False):
    c = ms[0].shape[0]

    def neumann(q0s, n_factors):
        xs = _each(lambda q: eye + q, q0s)
        if n_factors == 1:
            return xs
        qs = _each(_bdot, q0s, q0s)
        for _ in range(n_factors - 2):
            prods = _each(lambda x, q: _bdot(jnp.concatenate([x, q], axis=0), q), xs, qs)
            xs = _each(lambda x, pr: x + pr[:c], xs, prods)
            qs = _each(lambda pr: pr[c:], prods)
        return _each(lambda x, q: x + _bdot(x, q), xs, qs)

    ds = _each(lambda m: jnp.where(same_base, m, 0.0), ms)
    xs = neumann(_each(lambda d: -d, ds), int(math.log2(INV_BASE)))
    nblk = c // INV_BASE
    if nblk == 1 or base_only:
        return xs
    ls = _each(lambda m, d: m - d, ms, ds)
    ns = _each(lambda x, l: -_bdot(x, l), xs, ls)
    ys = neumann(ns, int(math.log2(nblk)))
    return _each(_bdot, ys, xs)


def _gdn_intra(qs, ks, vs, gcs, gls, betas, same_seq, low_incl, low_strict, eye, same_base,
               base_only=False):
    del same_seq
    e_gcs = _each(jnp.exp, gcs)

    def decay_of(gc):
        gc_row = jnp.sum(jnp.where(eye > 0, gc, 0.0), axis=0, keepdims=True)
        return jnp.where(low_incl, jnp.exp(jnp.where(low_incl, gc - gc_row, 0.0)), 0.0)

    c = qs[0].shape[0]
    decays = _each(decay_of, gcs)
    kbs = _each(lambda k, b: k * b, ks, betas)
    vbs = _each(lambda v, b: v * b, vs, betas)
    kqs = _each(lambda kb, q, k: _bdot_nt(jnp.concatenate([kb, q], axis=0), k), kbs, qs, ks)
    ms = _each(lambda kq, d: jnp.where(low_strict, kq[:c] * d, 0.0), kqs, decays)
    attns = _each(lambda kq, d: kq[c:] * d, kqs, decays)
    tmats = _unit_lower_inverse(ms, eye, same_base, base_only)
    uws = _each(lambda t, vb, kb, e: _bdot(t, jnp.concatenate([vb, kb * e], axis=1)),
                tmats, vbs, kbs, e_gcs)
    us = _each(lambda uw: uw[:, :GDN_DV], uws)
    ws = _each(lambda uw: uw[:, GDN_DV:], uws)
    q_decs = _each(lambda q, e: q * e, qs, e_gcs)
    k_decs = _each(lambda k, gl, gc: k * jnp.exp(gl - gc), ks, gls, gcs)
    return us, ws, attns, q_decs, k_decs


def _chunk_masks(c, seq_len):
    i = lax.broadcasted_iota(I32, (c, c), 0)
    j = lax.broadcasted_iota(I32, (c, c), 1)
    same_seq = _div_pow2(i, seq_len) == _div_pow2(j, seq_len)
    low_incl = same_seq & (i >= j)
    low_strict = same_seq & (i > j)
    eye = (i == j).astype(F32)
    same_base = _div_pow2(i, INV_BASE) == _div_pow2(j, INV_BASE)
    return same_seq, low_incl, low_strict, eye, same_base


def _gated_rms(o, z, nw):
    o = o * lax.rsqrt(jnp.mean(o * o, axis=1, keepdims=True) + NORM_EPS) * nw
    return o * _silu(z)


def _gdn_prompt_kernel(qg_ref, kg_ref, vg_ref, z_ref, gcb_ref, nw_ref, o_ref, s_out_ref, s_scr):
    c = GDN_CHUNK
    n = pl.program_id(1)

    @pl.when(n == 0)
    def _():
        s_scr[...] = jnp.zeros_like(s_scr)

    masks = _chunk_masks(c, c)
    nw = nw_ref[...]
    chains = [(b, h) for b in range(qg_ref.shape[0]) for h in range(GDN_HEADS)]
    hs = lambda h: slice(h * GDN_DK, (h + 1) * GDN_DK)
    col = lambda off: [gcb_ref[b, :, off + h:off + h + 1] for b, h in chains]
    gcs, betas, gls = col(0), col(GDN_HEADS), col(2 * GDN_HEADS)
    qs = [qg_ref[b, :, hs(h)] for b, h in chains]
    ks = [kg_ref[b, :, hs(h)] for b, h in chains]
    vs = [vg_ref[b, :, hs(h)] for b, h in chains]
    us, ws, attns, q_decs, k_decs = _gdn_intra(qs, ks, vs, gcs, gls, betas, *masks)
    ss = [s_scr[b, h] for b, h in chains]
    wqs = _each(lambda w, qd, s: _bdot(jnp.concatenate([w, qd], axis=0), s), ws, q_decs, ss)
    wss = _each(lambda wq: wq[:c], wqs)
    qss = _each(lambda wq: wq[c:], wqs)
    v_news = _each(lambda u, x: u - x, us, wss)
    avs = _each(_bdot, attns, v_news)
    kvs = _each(_bdot_tn, k_decs, v_news)
    for (b, h), s, gl, qsv, av, kv in zip(chains, ss, gls, qss, avs, kvs):
        s_scr[b, h] = s * jnp.exp(gl[0:1, :]) + kv
        o_ref[b, :, hs(h)] = _gated_rms(qsv + av, z_ref[b, :, hs(h)], nw)

    @pl.when(n == pl.num_programs(1) - 1)
    def _():
        s_out_ref[...] = s_scr[...]


def _gdn_prompt(qg, kg, vg, z, gcb, norm_w, n_seq):
    n = qg.shape[0]
    s_len = n // n_seq
    nb = min(GDN_SEQ_PER_STEP, n_seq)
    v3 = lambda a: a.reshape(n_seq, s_len, a.shape[-1])
    tok = lambda w: pl.BlockSpec((nb, GDN_CHUNK, w), lambda b, i: (b, i, 0))
    o, s = pl.pallas_call(
        _gdn_prompt_kernel,
        out_shape=[jax.ShapeDtypeStruct((n_seq, s_len, Z_COLS), F32),
                   jax.ShapeDtypeStruct((n_seq, GDN_HEADS, GDN_DK, GDN_DV), F32)],
        grid=(n_seq // nb, s_len // GDN_CHUNK),
        in_specs=[tok(QK_COLS), tok(QK_COLS), tok(Z_COLS), tok(Z_COLS), tok(LANES),
                  pl.BlockSpec((1, GDN_DV), lambda b, i: (0, 0))],
        out_specs=[tok(Z_COLS),
                   pl.BlockSpec((nb, GDN_HEADS, GDN_DK, GDN_DV), lambda b, i: (b, 0, 0, 0))],
        scratch_shapes=[pltpu.VMEM((nb, GDN_HEADS, GDN_DK, GDN_DV), F32)],
        compiler_params=_cparams(("arbitrary", "arbitrary")), name="gdn_prompt",
    )(v3(qg), v3(kg), v3(vg), v3(z), v3(gcb), norm_w)
    return o.reshape(n, Z_COLS), s


def _gdn_sample_kernel(qg_ref, kg_ref, vg_ref, z_ref, gcb_ref, nw_ref, s_in_ref, o_ref, s_out_ref):
    c = GDN_CHUNK
    n_sub = c // SAMPLE_SLOTS
    masks = _chunk_masks(c, SAMPLE_SLOTS)
    heads = range(GDN_HEADS)
    hs = lambda h: slice(h * GDN_DK, (h + 1) * GDN_DK)
    rs = lambda s: slice(s * SAMPLE_SLOTS, (s + 1) * SAMPLE_SLOTS)
    col = lambda off: [gcb_ref[:, off + h:off + h + 1] for h in heads]
    gcs, betas, gls = col(0), col(GDN_HEADS), col(2 * GDN_HEADS)
    us, ws, attns, q_decs, k_decs = _gdn_intra([qg_ref[:, hs(h)] for h in heads],
                                               [kg_ref[:, hs(h)] for h in heads],
                                               [vg_ref[:, hs(h)] for h in heads], gcs, gls, betas, *masks,
                                               base_only=SAMPLE_SLOTS <= INV_BASE)
    pairs = [(h, s) for h in heads for s in range(n_sub)]
    sts = [s_in_ref[s, h] for h, s in pairs]
    boths = [jnp.concatenate([ws[h][rs(s), :], q_decs[h][rs(s), :]], axis=0) for h, s in pairs]
    rr = _each(_bdot, boths, sts)
    gather = lambda h, part: jnp.concatenate(
        [rr[h * n_sub + s][part * SAMPLE_SLOTS:(part + 1) * SAMPLE_SLOTS, :] for s in range(n_sub)], axis=0)
    v_news = [us[h] - gather(h, 0) for h in heads]
    avs = _each(_bdot, attns, v_news)
    row = lax.broadcasted_iota(I32, (c, LANES), 0)
    seq_of_row = _div_pow2(row, SAMPLE_SLOTS)
    kds = [jnp.where(seq_of_row == s, k_decs[h], 0.0) for h, s in pairs]
    kvs = _each(_bdot_tn, kds, [v_news[h] for h, _ in pairs])
    egls = _each(jnp.exp, gls)
    for (h, s), st, kv in zip(pairs, sts, kvs):
        s_out_ref[s, h] = st * egls[h][s * SAMPLE_SLOTS:s * SAMPLE_SLOTS + 1, :] + kv
    nw = nw_ref[...]
    for h in heads:
        o_ref[:, hs(h)] = _gated_rms(gather(h, 1) + avs[h], z_ref[:, hs(h)], nw)


def _gdn_sample(qg, kg, vg, z, gcb, norm_w, state):
    n = qg.shape[0]
    n_sub = GDN_CHUNK // SAMPLE_SLOTS
    tok = lambda w: pl.BlockSpec((GDN_CHUNK, w), lambda i: (i, 0))
    st = pl.BlockSpec((n_sub, GDN_HEADS, GDN_DK, GDN_DV), lambda i: (i, 0, 0, 0))
    return pl.pallas_call(
        _gdn_sample_kernel,
        out_shape=[jax.ShapeDtypeStruct((n, Z_COLS), F32),
                   jax.ShapeDtypeStruct(state.shape, F32)],
        grid=(n // GDN_CHUNK,),
        in_specs=[tok(QK_COLS), tok(QK_COLS), tok(Z_COLS), tok(Z_COLS), tok(LANES),
                  pl.BlockSpec((1, GDN_DV), lambda i: (0, 0)), st],
        out_specs=[tok(Z_COLS), st],
        compiler_params=_cparams(("arbitrary",)), name="gdn_sample",
    )(qg, kg, vg, z, gcb, norm_w, state)


def _post_kernel(a_ref, g_ref, x_ref, wo_ref, ln_g_ref, ln_b_ref, wr_ref, x1_ref, route_ref, *, tm):
    sub = POST_SUB
    rows = [slice(j * sub, (j + 1) * sub) for j in range(tm // sub)]
    d = lambda a, b: jnp.dot(a, b, preferred_element_type=F32)
    mixes = [d(a_ref[r, :].astype(BF16), wo_ref[0:Q_COLS, :]) + d(g_ref[r, :].astype(BF16), wo_ref[Q_COLS:, :])
             for r in rows]
    x1s = [_layer_norm(DEEPNORM_ALPHA * x_ref[r, :] + mix, ln_g_ref[...], ln_b_ref[...])
           for r, mix in zip(rows, mixes)]
    for r, x1 in zip(rows, x1s):
        x1_ref[r, :] = x1
    w2 = wr_ref[...]
    lgs = []
    for x1 in x1s:
        xh = x1.astype(BF16)
        xm = (x1 - xh.astype(F32)).astype(BF16)
        both = d(xh, w2)
        lgs.append(both[:, :LANES] + both[:, LANES:] + d(xm, w2[:, :LANES]))
    for r, lg in zip(rows, lgs):
        route_ref[:, r] = jnp.transpose(_route(lg))[0:SUBLANES, :]


def _route(lg):
    lane = lax.broadcasted_iota(I32, lg.shape, 1)
    lane_f = lane.astype(F32)
    big = float(LANES)

    def first_max(vals, mask):
        v = jnp.where(mask, vals, NEG_BIG)
        mx = jnp.max(v, axis=1, keepdims=True)
        idx = jnp.min(jnp.where(mask & (v == mx), lane_f, big), axis=1, keepdims=True)
        return mx, idx

    gmask = lane < N_GROUPS
    gmax, gidx = first_max(lg, gmask)
    gden = jnp.sum(jnp.where(gmask, jnp.exp(lg - gmax), 0.0), axis=1, keepdims=True)
    g_top_p = 1.0 / gden
    e_lane = lane - N_GROUPS
    e_group = _div_pow2(jnp.maximum(e_lane, 0), EXPERTS_PER_GROUP).astype(F32)
    emask = (e_lane >= 0) & (e_lane < N_EXPERTS) & (e_group == gidx)
    m1, i1 = first_max(lg, emask)
    eden = jnp.sum(jnp.where(emask, jnp.exp(lg - m1), 0.0), axis=1, keepdims=True)
    m2, i2 = first_max(lg, emask & (lane_f != i1))
    p1 = 1.0 / eden
    p2 = jnp.exp(m2 - m1) / eden
    tot = p1 + p2
    gate1 = g_top_p * (p1 / tot)
    gate2 = g_top_p * (p2 / tot)
    return jnp.where(lane == 0, gate1,
                     jnp.where(lane == 1, gate2,
                               jnp.where(lane == 2, i1 - N_GROUPS,
                                         jnp.where(lane == 3, i2 - N_GROUPS, 0.0))))


def _post(attn_o, gdn_o, x, wts):
    n = x.shape[0]
    tm = min(POST_TILE, n)
    tok = lambda w: pl.BlockSpec((tm, w), lambda i: (i, 0))
    const = lambda a: pl.BlockSpec(a.shape, lambda i: (0,) * a.ndim)
    consts = [wts['wo'], wts['ln1_g'], wts['ln1_b'], wts['wr']]
    return pl.pallas_call(
        functools.partial(_post_kernel, tm=tm),
        out_shape=[jax.ShapeDtypeStruct((n, D_MODEL), F32), jax.ShapeDtypeStruct((SUBLANES, n), F32)],
        grid=(n // tm,),
        in_specs=[tok(Q_COLS), tok(Z_COLS), tok(D_MODEL)] + [const(a) for a in consts],
        out_specs=[tok(D_MODEL), pl.BlockSpec((SUBLANES, tm), lambda i: (0, i))],
        compiler_params=_cparams(("arbitrary",)), name="post_%d" % (n // tm),
    )(attn_o, gdn_o, x, *consts)


def _slab_loop(n, body):
    n_main = jnp.right_shift(n, int(math.log2(SLAB_UNROLL)))

    def main(i, c):
        for u in range(SLAB_UNROLL):
            body(i * SLAB_UNROLL + u, u)
        return c

    lax.fori_loop(0, n_main, main, 0)
    lax.fori_loop(n_main * SLAB_UNROLL, n, lambda j, c: (body(j, 0), c)[1], 0)


def _dispatch_kernel(dst_ref, nslab_ref, ztab_ref, zinfo_ref, slot_ref, gate_ref, *rest,
                     group_tiles, max_tiles):
    x_refs = rest[:len(group_tiles)]
    xs_ref, pbuf, sem, zbuf, zsem = rest[len(group_tiles):]
    n_tiles = sum(group_tiles)
    g = pl.program_id(0)
    cur = lax.rem(g, 2)

    def slab_copy(tile, buf_slot, j):
        d = pl.multiple_of(dst_ref[tile * PERM_SLABS + j], SLAB)
        src = pbuf.at[buf_slot, pl.ds(pl.multiple_of(j * SLAB, SLAB), SLAB), :]
        return pltpu.make_async_copy(src, xs_ref.at[pl.ds(d, SLAB), :], sem.at[buf_slot])

    def tail_copy(k):
        d = pl.multiple_of(ztab_ref[k], SLAB)
        return pltpu.make_async_copy(zbuf.at[pl.ds(0, SLAB), :], xs_ref.at[pl.ds(d, SLAB), :], zsem)

    def tile_copy(t):
        d = pl.multiple_of(t * ROW_TILE, ROW_TILE)
        return pltpu.make_async_copy(zbuf, xs_ref.at[pl.ds(d, ROW_TILE), :], zsem)

    @pl.when(g == 0)
    def _():
        zbuf[...] = jnp.zeros_like(zbuf)

    share = -(-ZERO_TABLE // n_tiles)
    k0 = g * share
    _slab_loop(jnp.clip(zinfo_ref[0] - k0, 0, share), lambda j, u: tail_copy(k0 + j).start(priority=1))

    @pl.when(zinfo_ref[1] + g < max_tiles)
    def _():
        tile_copy(zinfo_ref[1] + g).start(priority=1)

    x = x_refs[-1][...]
    bound = n_tiles
    for x_ref, nt in zip(x_refs[-2::-1], group_tiles[:0:-1]):
        bound -= nt
        x = jnp.where(g < bound, x_ref[...], x)

    r = lax.broadcasted_iota(I32, (PERM_ROWS, TOK_TILE), 0)
    sl = slot_ref[0]
    hit0, hit1 = r == sl[0:1, :], r == sl[1:2, :]
    onehot = jnp.where(hit0 | hit1, 1.0, 0.0).astype(BF16)
    gt = gate_ref[0]
    gcol = jnp.sum(jnp.where(hit0, gt[0:1, :], 0.0) + jnp.where(hit1, gt[1:2, :], 0.0),
                   axis=1, keepdims=True)
    pbuf[cur, :, 0:D_MODEL] = jnp.dot(onehot, x.astype(BF16), preferred_element_type=F32).astype(BF16)
    g_hi = gcol.astype(BF16).astype(F32)
    lane = lax.broadcasted_iota(I32, (PERM_ROWS, LANES), 1)
    pbuf[cur, :, D_MODEL:] = jnp.where(lane < LANES // 2, g_hi, gcol - g_hi).astype(BF16)

    @pl.when(g > 0)
    def _():
        _slab_loop(nslab_ref[g - 1], lambda j, u: slab_copy(g - 1, 1 - cur, j).wait())

    _slab_loop(nslab_ref[g], lambda j, u: slab_copy(g, cur, j).start(priority=u % 2))

    @pl.when(g == n_tiles - 1)
    def _():
        _slab_loop(nslab_ref[g], lambda j, u: slab_copy(g, cur, j).wait())
        lax.fori_loop(zinfo_ref[1] + n_tiles, max_tiles,
                      lambda t, c: (tile_copy(t).start(priority=1), c)[1], 0)
        _slab_loop(zinfo_ref[0], lambda k, u: tail_copy(k).wait())
        lax.fori_loop(zinfo_ref[1], max_tiles, lambda t, c: (tile_copy(t).wait(), c)[1], 0)


def _dispatch(plan, x1s, max_tiles):
    group_tiles = tuple(x1.shape[0] // TOK_TILE for x1 in x1s)
    n_tiles = sum(group_tiles)
    tile = lambda i, d, ns, zt, zi: (i, 0, 0)
    in_specs = [pl.BlockSpec((1, TOP_K, TOK_TILE), tile), pl.BlockSpec((1, TOP_K, TOK_TILE), tile)]
    base = 0
    for nt in group_tiles:
        in_specs.append(pl.BlockSpec(
            (TOK_TILE, D_MODEL),
            lambda i, d, ns, zt, zi, base=base, nt=nt: (jnp.clip(i - base, 0, nt - 1), 0)))
        base += nt
    return pl.pallas_call(
        functools.partial(_dispatch_kernel, group_tiles=group_tiles, max_tiles=max_tiles),
        out_shape=jax.ShapeDtypeStruct((max_tiles * ROW_TILE, XS_WORDS), BF16),
        grid_spec=pltpu.PrefetchScalarGridSpec(
            num_scalar_prefetch=4, grid=(n_tiles,), in_specs=in_specs,
            out_specs=pl.BlockSpec(memory_space=pl.ANY),
            scratch_shapes=[pltpu.VMEM((2, PERM_ROWS, XS_WORDS), BF16), pltpu.SemaphoreType.DMA((2,)),
                            pltpu.VMEM((ROW_TILE, XS_WORDS), BF16), pltpu.SemaphoreType.DMA(())]),
        compiler_params=_cparams(("arbitrary",)), name="moe_dispatch",
    )(plan['slab_dst'], plan['nslab'], plan['ztab'], plan['zinfo'], plan['slot_rows'], plan['gate_rows'],
      *x1s)


def _expert_kernel(t0_ref, nt_ref, nu_ref, wg_ref, wu_ref, wd_ref, xs_ref, ye_ref,
                   xbuf, ybuf, wgu_scr, wd_scr, zbuf, in_sem, out_sem, zsem, *, max_tiles):
    e = pl.program_id(0)
    n_used = nu_ref[0]

    def in_copy(t):
        slot = lax.rem(t, 2)
        src = xs_ref.at[pl.ds(pl.multiple_of(t * ROW_TILE, ROW_TILE), ROW_TILE), :]
        return pltpu.make_async_copy(src, xbuf.at[slot], in_sem.at[slot])

    def out_copy(t):
        slot = lax.rem(t, 2)
        dst = ye_ref.at[pl.ds(pl.multiple_of(t * ROW_TILE, ROW_TILE), ROW_TILE), :]
        return pltpu.make_async_copy(ybuf.at[slot], dst, out_sem.at[slot])

    def zero_copy(t):
        dst = ye_ref.at[pl.ds(pl.multiple_of(t * ROW_TILE, ROW_TILE), ROW_TILE), :]
        return pltpu.make_async_copy(zbuf, dst, zsem)

    @pl.when(e == 0)
    def _():
        in_copy(0).start()

    wgu_scr[:, 0:EXPERT_FF] = wg_ref[0].astype(BF16)
    wgu_scr[:, EXPERT_FF:] = wu_ref[0].astype(BF16)
    wd_scr[...] = wd_ref[0].astype(BF16)

    sub = EXPERT_SUB
    rows = [slice(j * sub, (j + 1) * sub) for j in range(ROW_TILE // sub)]
    d = lambda a, b: jnp.dot(a, b, preferred_element_type=F32)

    def tile_body(j, carry):
        t = t0_ref[e] + j
        slot = lax.rem(t, 2)

        @pl.when(t + 1 < n_used)
        def _():
            in_copy(t + 1).start()

        in_copy(t).wait()

        @pl.when(t >= 2)
        def _():
            out_copy(t - 2).wait()

        wgu, wd = wgu_scr[...], wd_scr[...]
        xs = [xbuf[slot, r, 0:D_MODEL] for r in rows]
        hs = [d(x, wgu) for x in xs]
        hhs = [(_silu(h[:, :EXPERT_FF]) * h[:, EXPERT_FF:]).astype(BF16) for h in hs]
        ys = [d(hh, wd) for hh in hhs]
        for r, y in zip(rows, ys):
            parts = xbuf[slot, r, D_MODEL:].astype(F32)
            gate = parts + pltpu.roll(parts, LANES // 2, axis=1)
            ybuf[slot, r, :] = (y * jnp.concatenate([gate] * (D_MODEL // LANES), axis=1)).astype(BF16)
        out_copy(t).start()
        return carry

    lax.fori_loop(0, nt_ref[e], tile_body, 0)

    @pl.when(e == pl.num_programs(0) - 1)
    def _():
        @pl.when(n_used >= 2)
        def _():
            out_copy(n_used - 2).wait()

        out_copy(n_used - 1).wait()
        zbuf[...] = jnp.zeros_like(zbuf)
        lax.fori_loop(n_used, max_tiles, lambda t, c: (zero_copy(t).start(), c)[1], 0)
        lax.fori_loop(n_used, max_tiles, lambda t, c: (zero_copy(t).wait(), c)[1], 0)


def _experts(plan, xs, w_gate, w_up, w_down):
    max_tiles = xs.shape[0] // ROW_TILE
    wsel = lambda e, t0, nt, nu: (e, 0, 0)
    return pl.pallas_call(
        functools.partial(_expert_kernel, max_tiles=max_tiles),
        out_shape=jax.ShapeDtypeStruct((xs.shape[0], D_MODEL), BF16),
        grid_spec=pltpu.PrefetchScalarGridSpec(
            num_scalar_prefetch=3, grid=(N_EXPERTS,),
            in_specs=[pl.BlockSpec((1, D_MODEL, EXPERT_FF), wsel),
                      pl.BlockSpec((1, D_MODEL, EXPERT_FF), wsel),
                      pl.BlockSpec((1, EXPERT_FF, D_MODEL), wsel),
                      pl.BlockSpec(memory_space=pl.ANY)],
            out_specs=pl.BlockSpec(memory_space=pl.ANY),
            scratch_shapes=[pltpu.VMEM((2, ROW_TILE, XS_WORDS), BF16),
                            pltpu.VMEM((2, ROW_TILE, D_MODEL), BF16),
                            pltpu.VMEM((D_MODEL, 2 * EXPERT_FF), BF16),
                            pltpu.VMEM((EXPERT_FF, D_MODEL), BF16),
                            pltpu.VMEM((ROW_TILE, D_MODEL), BF16),
                            pltpu.SemaphoreType.DMA((2,)), pltpu.SemaphoreType.DMA((2,)),
                            pltpu.SemaphoreType.DMA(())]),
        compiler_params=_cparams(("arbitrary",)), name="moe_experts",
    )(plan['tile_start'], plan['tile_count'], plan['n_used'], w_gate, w_up, w_down, xs)


def _combine_kernel(dst_ref, nslab_ref, x1_ref, slot_ref, ye_ref, ln_g_ref, ln_b_ref, y_ref,
                    buf, sem, *, tile_base, n_tiles):
    i = pl.program_id(0)
    g = tile_base + i
    cur = lax.rem(i, 2)

    def slab_copy(tile, buf_slot, j):
        d = pl.multiple_of(dst_ref[tile * PERM_SLABS + j], SLAB)
        dst = buf.at[buf_slot, pl.ds(pl.multiple_of(j * SLAB, SLAB), SLAB), :]
        return pltpu.make_async_copy(ye_ref.at[pl.ds(d, SLAB), :], dst, sem.at[buf_slot])

    @pl.when(i == 0)
    def _():
        buf[...] = jnp.zeros_like(buf)
        _slab_loop(nslab_ref[g], lambda j, u: slab_copy(g, cur, j).start(priority=u % 2))

    @pl.when(i + 1 < n_tiles)
    def _():
        _slab_loop(nslab_ref[g + 1], lambda j, u: slab_copy(g + 1, 1 - cur, j).start(priority=u % 2))

    _slab_loop(nslab_ref[g], lambda j, u: slab_copy(g, cur, j).wait())

    sub = COMBINE_SUB
    subs = [slice(j * sub, (j + 1) * sub) for j in range(TOK_TILE // sub)]
    col = lax.broadcasted_iota(I32, (sub, PERM_ROWS), 1)
    sl = slot_ref[0]
    diag = (lax.broadcasted_iota(I32, (sub, sub), 0) == lax.broadcasted_iota(I32, (sub, sub), 1))
    as_col = lambda row: jnp.sum(jnp.where(diag, row, 0), axis=1, keepdims=True)
    picks = [jnp.where((col == as_col(sl[0:1, r])) | (col == as_col(sl[1:2, r])), 1.0, 0.0).astype(BF16)
             for r in subs]
    rows = buf[cur]
    moes = [jnp.dot(pick, rows, preferred_element_type=F32) for pick in picks]
    for r, moe in zip(subs, moes):
        y_ref[r, :] = _layer_norm(DEEPNORM_ALPHA * x1_ref[r, :] + moe, ln_g_ref[...], ln_b_ref[...])


def _combine(plan, tile_base, x1, ye, ln_g, ln_b):
    n = x1.shape[0]
    n_tiles = n // TOK_TILE
    tok = lambda w: pl.BlockSpec((TOK_TILE, w), lambda i, d, ns: (i, 0))
    const = lambda a: pl.BlockSpec(a.shape, lambda i, d, ns: (0,) * a.ndim)
    return pl.pallas_call(
        functools.partial(_combine_kernel, tile_base=tile_base, n_tiles=n_tiles),
        out_shape=jax.ShapeDtypeStruct((n, D_MODEL), F32),
        grid_spec=pltpu.PrefetchScalarGridSpec(
            num_scalar_prefetch=2, grid=(n_tiles,),
            in_specs=[tok(D_MODEL),
                      pl.BlockSpec((1, TOP_K, TOK_TILE), lambda i, d, ns: (tile_base + i, 0, 0)),
                      pl.BlockSpec(memory_space=pl.ANY), const(ln_g), const(ln_b)],
            out_specs=tok(D_MODEL),
            scratch_shapes=[pltpu.VMEM((2, PERM_ROWS, D_MODEL), BF16), pltpu.SemaphoreType.DMA((2,))]),
        compiler_params=_cparams(("arbitrary",)), name="moe_combine_%d" % tile_base,
    )(plan['slab_dst'], plan['nslab'], x1, plan['slot_rows'], ye, ln_g, ln_b)


def _routing_plan(ids, gates):
    nt = ids.shape[1] // TOK_TILE
    pairs = TOP_K * TOK_TILE
    ex = jnp.arange(N_EXPERTS, dtype=I32)
    per_tile = lambda a: jnp.swapaxes(a.reshape(TOP_K, nt, TOK_TILE), 0, 1)
    flat = per_tile(ids).reshape(nt, pairs)
    onehot = (flat[:, None, :] == ex[None, :, None])
    p = np.arange(pairs)
    triu = jnp.asarray(p[:, None] <= p[None, :], BF16)
    csum = jnp.dot(onehot.astype(BF16).reshape(nt * N_EXPERTS, pairs), triu,
                   preferred_element_type=F32).astype(I32).reshape(nt, N_EXPERTS, pairs)
    oh = onehot.astype(I32)
    rank = jnp.sum(oh * (csum - 1), axis=1)
    cnt = csum[:, :, -1]
    cpad = (cnt + SLAB - 1) // SLAB * SLAB
    seg_end = jnp.cumsum(cpad, axis=1)
    seg_off = seg_end - cpad
    slot = jnp.sum(oh * seg_off[:, :, None], axis=1) + rank
    run_end = jnp.cumsum(cpad, axis=0)
    ntiles_e = (run_end[-1] + ROW_TILE - 1) // ROW_TILE
    tile_end = jnp.cumsum(ntiles_e)
    dst_run = ((tile_end - ntiles_e) * ROW_TILE)[None, :] + run_end - cpad
    j8 = jnp.arange(PERM_SLABS, dtype=I32) * SLAB
    e_of = jnp.minimum(jnp.sum((j8[None, :, None] >= seg_end[:, None, :]).astype(I32), axis=2),
                       N_EXPERTS - 1)
    sel = (e_of[:, :, None] == ex).astype(I32)
    slab_dst = jnp.sum(sel * (dst_run - seg_off)[:, None, :], axis=2) + j8[None, :]
    n_used = tile_end[-1]
    row_start = (tile_end - ntiles_e) * ROW_TILE
    tail_cnt = (ntiles_e * ROW_TILE - run_end[-1]) // SLAB
    tail_end = jnp.cumsum(tail_cnt)
    k = jnp.arange(ZERO_TABLE, dtype=I32)
    e_k = jnp.minimum(jnp.sum((k[:, None] >= tail_end[None, :]).astype(I32), axis=1), N_EXPERTS - 1)
    base_k = jnp.sum((e_k[:, None] == ex).astype(I32)
                     * (row_start + run_end[-1] - SLAB * (tail_end - tail_cnt))[None, :], axis=1)
    return dict(
        slab_dst=slab_dst.reshape(-1).astype(I32), nslab=(seg_end[:, -1] // SLAB).astype(I32),
        ztab=(base_k + SLAB * k).astype(I32), zinfo=jnp.stack([tail_end[-1], n_used]).astype(I32),
        slot_rows=slot.reshape(nt, TOP_K, TOK_TILE).astype(I32),
        gate_rows=per_tile(gates).astype(F32),
        tile_start=(tile_end - ntiles_e).astype(I32), tile_count=ntiles_e.astype(I32),
        n_used=n_used.reshape(1).astype(I32))


def _max_row_tiles(n_tokens):
    rows = TOP_K * n_tokens + (n_tokens // TOK_TILE) * N_EXPERTS * (SLAB - 1)
    return (rows + ROW_TILE - 1) // ROW_TILE + N_EXPERTS


def _moe(x1s, routes, wts):
    ids = jnp.concatenate([r[TOP_K:2 * TOP_K, :] for r in routes], axis=1).astype(I32)
    gates = jnp.concatenate([r[0:TOP_K, :] for r in routes], axis=1)
    plan = _routing_plan(ids, gates)
    max_tiles = _max_row_tiles(ids.shape[1])
    bases = [0]
    for x1 in x1s[:-1]:
        bases.append(bases[-1] + x1.shape[0] // TOK_TILE)
    xs = _dispatch(plan, x1s, max_tiles)
    ye = _experts(plan, xs, wts['w_gate'], wts['w_up'], wts['w_down'])
    return [_combine(plan, base, x1, ye, wts['ln2_g'], wts['ln2_b']) for base, x1 in zip(bases, x1s)]


def _prep_weights(w_in, w_out, conv_w, a_log, dt_bias, gdn_norm_w, ln1_g, ln1_b, w_router_group,
                  w_router_expert, w_gate, w_up, w_down, ln2_g, ln2_b):
    pad_row = lambda v: jnp.pad(v.astype(F32), (0, LANES - v.shape[0]))[None, :]
    wr = jnp.pad(jnp.concatenate([w_router_group, w_router_expert], axis=1),
                 ((0, 0), (0, LANES - N_GROUPS - N_EXPERTS)))
    wr_hi = wr.astype(BF16)
    wr_mid = (wr - wr_hi.astype(F32)).astype(BF16)
    group = ATTN_HEADS // ATTN_KV_HEADS
    wq = w_in[:, :Q_COLS].reshape(D_MODEL, ATTN_KV_HEADS, group, HEAD_DIM)
    wq = jnp.swapaxes(wq, 1, 2).reshape(D_MODEL, Q_COLS)
    w_all = jnp.concatenate([wq, w_in[:, Q_COLS:]], axis=1)
    w_all = jnp.pad(w_all, ((0, 0), (0, IN_COLS_PAD - w_all.shape[1]))).astype(BF16)
    wo_q = w_out[:Q_COLS].reshape(ATTN_KV_HEADS, group, HEAD_DIM, D_MODEL)
    wo_q = jnp.swapaxes(wo_q, 0, 1).reshape(Q_COLS, D_MODEL)
    wo = jnp.concatenate([wo_q, w_out[Q_COLS:]], axis=0)
    return dict(
        w_all=w_all, convw=conv_w.astype(F32), alog=pad_row(a_log), dtb=pad_row(dt_bias),
        norm_w=gdn_norm_w.astype(F32)[None, :], wo=wo.astype(BF16),
        ln1_g=ln1_g[None, :], ln1_b=ln1_b[None, :], wr=jnp.concatenate([wr_hi, wr_mid], axis=1),
        w_gate=w_gate, w_up=w_up, w_down=w_down, ln2_g=ln2_g[None, :], ln2_b=ln2_b[None, :])


def _layer(x_prompt, x_sample, cache_k, cache_v, state_gdn, state_conv, wts):
    bp, sp, _ = x_prompt.shape
    bs, ts, _ = x_sample.shape
    n_p = bp * sp

    xp = x_prompt.reshape(n_p, D_MODEL)
    (q, k, v, qg, kg, vg, z, gcb, utail) = _proj(xp, np.arange(sp), wts, GDN_CHUNK, bp)
    attn_p = _attn_prompt(q, k, v, wts['sinks'], bp)
    gdn_p, s_p = _gdn_prompt(qg, kg, vg, z, gcb, wts['norm_w'], bp)
    last_win = lambda a: a.reshape(bp, sp, KV_COLS)[:, sp - WINDOW:].reshape(bp, WINDOW, ATTN_KV_HEADS,
                                                                            HEAD_DIM)
    new_k_p, new_v_p = last_win(k), last_win(v)
    tiles_per_seq = sp // min(PROJ_TILE, sp)
    conv_p = utail.reshape(bp, tiles_per_seq, SUBLANES, CONV_DIM)[:, -1, SUBLANES - (CONV_W - 1):]

    lo, hi = SAMPLE_FIRST, SAMPLE_FIRST + ts
    xs_rows = jnp.pad(x_sample, ((0, 0), (lo, SAMPLE_SLOTS - hi), (0, 0))).reshape(bs * SAMPLE_SLOTS, D_MODEL)
    hist = jnp.pad(state_conv, ((0, 0), (0, SAMPLE_SLOTS - lo), (0, 0))).reshape(bs * SAMPLE_SLOTS, CONV_DIM)
    slot = np.arange(SAMPLE_SLOTS)
    valid = jnp.asarray(np.tile((slot >= lo) & (slot < hi), bs)[:, None], F32)
    pos_s = np.tile(PAST_LEN + slot - lo, bs)
    (q, k, v, qg, kg, vg, z, gcb, u_s) = _proj(xs_rows, pos_s, wts, SAMPLE_SLOTS, 1, hist, valid)
    ck = cache_k.reshape(bs, WINDOW, KV_COLS)
    cv = cache_v.reshape(bs, WINDOW, KV_COLS)
    attn_s, kwin, vwin = _attn_sample(q, k, v, ck, cv, wts['sinks'], bs, ts)
    gdn_s, s_s = _gdn_sample(qg, kg, vg, z, gcb, wts['norm_w'], state_gdn)
    real = lambda a: a.reshape(bs, SAMPLE_SLOTS, -1)[:, lo:hi]
    new_k_s = kwin.reshape(bs, WINDOW, ATTN_KV_HEADS, HEAD_DIM)
    new_v_s = vwin.reshape(bs, WINDOW, ATTN_KV_HEADS, HEAD_DIM)
    conv_s = u_s.reshape(bs, SAMPLE_SLOTS, CONV_DIM)[:, hi - (CONV_W - 1):hi]

    x1_p, route_p = _post(attn_p, gdn_p, xp, wts)
    x1_s, route_s = _post(real(attn_s).reshape(bs * ts, Q_COLS), real(gdn_s).reshape(bs * ts, Z_COLS),
                          x_sample.reshape(bs * ts, D_MODEL), wts)
    y_p, y_s = _moe([x1_p, x1_s], [route_p, route_s], wts)
    return (y_p.reshape(bp, sp, D_MODEL), y_s.reshape(bs, ts, D_MODEL), new_k_p, new_v_p, s_p, conv_p,
            new_k_s, new_v_s, s_s, conv_s)


def kernel(x_prompt, x_sample, cache_attn_k, cache_attn_v, state_gdn, state_conv, w_in, w_out,
           attn_sinks, conv_w, a_log, dt_bias, gdn_norm_w, ln1_g, ln1_b, w_router_group,
           w_router_expert, w_gate, w_up, w_down, ln2_g, ln2_b):
    assert w_in.shape[0] == DEPTH
    l = 0
    wts = _prep_weights(w_in[l], w_out[l], conv_w[l], a_log[l], dt_bias[l], gdn_norm_w[l], ln1_g[l],
                        ln1_b[l], w_router_group[l], w_router_expert[l], w_gate[l], w_up[l],
                        w_down[l], ln2_g[l], ln2_b[l])
    wts['sinks'] = attn_sinks[l]
    outs = _layer(x_prompt, x_sample, cache_attn_k[l], cache_attn_v[l], state_gdn[l], state_conv[l], wts)
    (y_p, y_s, k_p, v_p, s_p, c_p, k_s, v_s, s_s, c_s) = outs
    add = lambda a: a[None]
    return (y_p, y_s, add(k_p), add(v_p), add(s_p), add(c_p), add(k_s), add(v_s), add(s_s), add(c_s))
```

```python
import functools
import math

import jax
import jax.numpy as jnp
import numpy as np
from jax import lax
from jax.experimental import pallas as pl
from jax.experimental.pallas import tpu as pltpu

F32 = jnp.float32
BF16 = jnp.bfloat16
I32 = jnp.int32

D_MODEL = 1024
ATTN_HEADS = 8
ATTN_KV_HEADS = 2
HEAD_DIM = 64
WINDOW = 128
ROT_DIM = HEAD_DIM // 4
ROPE_THETA = 500000.0
GDN_HEADS = 4
GDN_DK = 128
GDN_DV = 128
CONV_W = 4
QK_COLS = GDN_HEADS * GDN_DK
CONV_DIM = 2 * QK_COLS + GDN_HEADS * GDN_DV
Z_COLS = GDN_HEADS * GDN_DV
Q_COLS = ATTN_HEADS * HEAD_DIM
KV_COLS = ATTN_KV_HEADS * HEAD_DIM
N_GROUPS = 4
EXPERTS_PER_GROUP = 8
N_EXPERTS = N_GROUPS * EXPERTS_PER_GROUP
TOP_K = 2
EXPERT_FF = 256
NORM_EPS = 1e-5
L2_EPS = 1e-6
DEPTH = 1
DEEPNORM_ALPHA = (2 * DEPTH) ** 0.25
PAST_LEN = 8192

LANES = 128
SUBLANES = 8
IN_SPLITS = (0, Q_COLS + 2 * KV_COLS, Q_COLS + 2 * KV_COLS + CONV_DIM,
             Q_COLS + 2 * KV_COLS + CONV_DIM + Z_COLS, Q_COLS + 2 * KV_COLS + CONV_DIM + Z_COLS + LANES)
IN_COLS_PAD = IN_SPLITS[-1]
TOK_TILE = 512
PROJ_TILE = 512
PROJ_SUB = 128
POST_TILE = 1024
POST_SUB = 256
GDN_CHUNK = 128
GDN_SEQ_PER_STEP = 4
ATTN_BLOCKS_PER_STEP = 4
ATTN_SEQS_PER_STEP = 16
INV_BASE = 16
SAMPLE_SLOTS = 8
SAMPLE_FIRST = CONV_W - 1
ROW_TILE = 512
EXPERT_SUB = 256
EXPERT_IN_SLOTS = 3
SLAB_UNROLL = 4
COMBINE_SUB = 128
SLAB = 16
PERM_ROWS = TOP_K * TOK_TILE + N_EXPERTS * SLAB
PERM_SLABS = PERM_ROWS // SLAB
XS_WORDS = D_MODEL + LANES
ZERO_TABLE = N_EXPERTS * (ROW_TILE // SLAB)
VMEM_LIMIT = 48 * 1024 * 1024
NEG_BIG = -1e30


def _cparams(sem):
    return pltpu.CompilerParams(dimension_semantics=sem, vmem_limit_bytes=VMEM_LIMIT)


def _bdot(a, b):
    return jnp.dot(a.astype(BF16), b.astype(BF16), preferred_element_type=F32)


def _bdot_nt(a, b):
    return lax.dot_general(a.astype(BF16), b.astype(BF16), (((1,), (1,)), ((), ())),
                           preferred_element_type=F32)


def _bdot_tn(a, b):
    return lax.dot_general(a.astype(BF16), b.astype(BF16), (((0,), (0,)), ((), ())),
                           preferred_element_type=F32)


def _div_pow2(x, n):
    return jnp.right_shift(x, int(math.log2(n)))


def _mod_pow2(x, n):
    return jnp.bitwise_and(x, n - 1)


def _split3(x):
    hi = x.astype(BF16)
    r = x - hi.astype(F32)
    mid = r.astype(BF16)
    lo = (r - mid.astype(F32)).astype(BF16)
    return hi, mid, lo


def _dot_exact_lhs01(m01, x):
    hi, mid, lo = _split3(x)
    d = lambda t: jnp.dot(m01, t, preferred_element_type=F32)
    return d(hi) + d(mid) + d(lo)


def _sigmoid(x):
    return 1.0 / (1.0 + jnp.exp(-x))


def _silu(x):
    return x * _sigmoid(x)


def _softplus(x):
    return jnp.maximum(x, 0.0) + jnp.log1p(jnp.exp(-jnp.abs(x)))


def _layer_norm(h, g, b):
    mu = jnp.mean(h, axis=-1, keepdims=True)
    d = h - mu
    var = jnp.mean(d * d, axis=-1, keepdims=True)
    return d * lax.rsqrt(var + NORM_EPS) * g + b


def _proj_kernel(*refs, tm, has_hist, full_u, one_segment):
    it = iter(refs)
    x_ref, cos_ref, sin_ref = next(it), next(it), next(it)
    w_ref = next(it)
    convw_ref, alog_ref, dtb_ref, tri_ref, seg_ref = next(it), next(it), next(it), next(it), next(it)
    hist_ref = valid_ref = None
    if has_hist:
        hist_ref, valid_ref = next(it), next(it)
    q_ref, k_ref, v_ref = next(it), next(it), next(it)
    qg_ref, kg_ref, vg_ref, z_ref, gcb_ref, u_ref = (next(it) for _ in range(6))
    ubuf = next(it)

    t = pl.program_id(1)
    sub = PROJ_SUB
    rows = [slice(j * sub, (j + 1) * sub) for j in range(tm // sub)]
    lane = lax.broadcasted_iota(I32, (sub, LANES), 1)
    first_half = _mod_pow2(lane, HEAD_DIM) < (ROT_DIM // 2)

    @pl.when(t == 0)
    def _():
        ubuf[0:SUBLANES, :] = jnp.zeros((SUBLANES, CONV_DIM), F32)

    @pl.when(t > 0)
    def _():
        ubuf[0:SUBLANES, :] = ubuf[tm:tm + SUBLANES, :]

    dots = []
    for r in rows:
        xb = x_ref[r, :].astype(BF16)
        dots.append([jnp.dot(xb, w_ref[:, lo:hi], preferred_element_type=F32)
                     for lo, hi in zip(IN_SPLITS[:-1], IN_SPLITS[1:])])

    def l2n(s):
        return s * lax.rsqrt(jnp.sum(s * s, axis=1, keepdims=True) + L2_EPS)

    for r, (pq, u, z, ab) in zip(rows, dots):
        cosv, sinv = cos_ref[r, :], sin_ref[r, :]

        def rope(s):
            sw = jnp.where(first_half, pltpu.roll(s, LANES - ROT_DIM // 2, axis=1),
                           pltpu.roll(s, ROT_DIM // 2, axis=1))
            return s * cosv + sw * sinv

        for j in range(Q_COLS // LANES):
            q_ref[r, j * LANES:(j + 1) * LANES] = rope(pq[:, j * LANES:(j + 1) * LANES])
        k_ref[r, :] = rope(pq[:, Q_COLS:Q_COLS + KV_COLS])
        v_ref[r, :] = pq[:, Q_COLS + KV_COLS:Q_COLS + 2 * KV_COLS]
        z_ref[r, :] = z

        if has_hist:
            u = u + hist_ref[r, :]
        if full_u:
            u_ref[r, :] = u
        elif r.stop == tm:
            u_ref[...] = u[sub - SUBLANES:, :]
        base = SUBLANES + r.start
        ubuf[base:base + sub, :] = u
        acc = u * convw_ref[CONV_W - 1:CONV_W, :]
        for j in range(1, CONV_W):
            acc = acc + ubuf[base - j:base - j + sub, :] * convw_ref[CONV_W - 1 - j:CONV_W - j, :]
        c = _silu(acc)
        if has_hist:
            c = c * valid_ref[r, :]
        for h in range(GDN_HEADS):
            sl = slice(h * GDN_DK, (h + 1) * GDN_DK)
            qg_ref[r, sl] = l2n(c[:, sl]) * (GDN_DK ** -0.5)
            kg_ref[r, sl] = l2n(c[:, QK_COLS + h * GDN_DK:QK_COLS + (h + 1) * GDN_DK])
        vg_ref[r, :] = c[:, 2 * QK_COLS:]

        g = -jnp.exp(alog_ref[...]) * _softplus(ab + dtb_ref[...])
        beta = _sigmoid(ab)
        if has_hist:
            g = g * valid_ref[r, :]
            beta = beta * valid_ref[r, :]
        g = jnp.where(lane < GDN_HEADS, g, 0.0)
        gc = _dot_exact_lhs01(tri_ref[...], g)
        if one_segment:
            gl = jnp.broadcast_to(gc[sub - 1:sub, :], (sub, LANES))
        else:
            gl = _dot_exact_lhs01(seg_ref[...], g)
        gcb_ref[r, :] = jnp.where(lane < GDN_HEADS, gc,
                                  jnp.where(lane < 2 * GDN_HEADS, beta,
                                            jnp.where(lane < 3 * GDN_HEADS,
                                                      pltpu.roll(gl, 2 * GDN_HEADS, axis=1), 0.0)))


def _rope_tables(pos):
    half = ROT_DIM // 2
    pos = np.asarray(pos, np.float64)
    inv_freq = ROPE_THETA ** (-np.arange(half, dtype=np.float64) * 2.0 / ROT_DIM)
    ang = pos[:, None] * inv_freq[None, :]
    cos, sin = np.cos(ang), np.sin(ang)
    p = pos.shape[0]
    cpat = np.concatenate([cos, cos, np.ones((p, HEAD_DIM - ROT_DIM))], axis=1)
    spat = np.concatenate([-sin, sin, np.zeros((p, HEAD_DIM - ROT_DIM))], axis=1)
    rep = (1, LANES // HEAD_DIM)
    return jnp.asarray(np.tile(cpat, rep), F32), jnp.asarray(np.tile(spat, rep), F32)


def _segment_matrices(tm, seg_len):
    i = np.arange(tm)
    same = (i[:, None] // seg_len) == (i[None, :] // seg_len)
    tri = same & (i[None, :] <= i[:, None])
    return jnp.asarray(tri, BF16), jnp.asarray(same, BF16)


def _proj(x, pos, wts, seg_len, n_seq, hist=None, valid=None):
    n = x.shape[0]
    rows = n // n_seq
    tm = min(PROJ_TILE, rows)
    nt = rows // tm
    has_hist = hist is not None
    cos_t, sin_t = _rope_tables(pos)
    tri, seg = _segment_matrices(PROJ_SUB, seg_len)

    tok = lambda w: pl.BlockSpec((tm, w), lambda b, t: (b * nt + t, 0))
    const = lambda a: pl.BlockSpec(a.shape, lambda b, t: (0,) * a.ndim)
    in_arrays = [x, cos_t, sin_t, wts['w_all'], wts['convw'], wts['alog'], wts['dtb'], tri, seg]
    in_specs = [tok(D_MODEL), pl.BlockSpec((tm, LANES), lambda b, t: (t, 0)),
                pl.BlockSpec((tm, LANES), lambda b, t: (t, 0))] + [const(a) for a in in_arrays[3:]]
    if has_hist:
        in_arrays += [hist, valid]
        in_specs += [tok(CONV_DIM), tok(1)]
    u_rows = n if has_hist else (n // tm) * SUBLANES
    u_block = tm if has_hist else SUBLANES
    out_shape = [jax.ShapeDtypeStruct((n, Q_COLS), F32), jax.ShapeDtypeStruct((n, KV_COLS), F32),
                 jax.ShapeDtypeStruct((n, KV_COLS), F32), jax.ShapeDtypeStruct((n, QK_COLS), F32),
                 jax.ShapeDtypeStruct((n, QK_COLS), F32), jax.ShapeDtypeStruct((n, Z_COLS), F32),
                 jax.ShapeDtypeStruct((n, Z_COLS), F32), jax.ShapeDtypeStruct((n, LANES), F32),
                 jax.ShapeDtypeStruct((u_rows, CONV_DIM), F32)]
    out_specs = [tok(Q_COLS), tok(KV_COLS), tok(KV_COLS), tok(QK_COLS), tok(QK_COLS), tok(Z_COLS),
                 tok(Z_COLS), tok(LANES),
                 pl.BlockSpec((u_block, CONV_DIM), lambda b, t: (b * nt + t, 0))]
    return pl.pallas_call(
        functools.partial(_proj_kernel, tm=tm, has_hist=has_hist, full_u=has_hist,
                          one_segment=seg_len == PROJ_SUB),
        out_shape=out_shape, grid=(n_seq, nt), in_specs=in_specs, out_specs=out_specs,
        scratch_shapes=[pltpu.VMEM((tm + SUBLANES, CONV_DIM), F32)],
        compiler_params=_cparams(("arbitrary", "arbitrary")),
        name="proj_hist" if has_hist else "proj",
    )(*in_arrays)


def _attn_blocks(qs, kcats, vcats, biases, sink, tq):
    lane = lax.broadcasted_iota(I32, (tq, LANES), 1)
    low = lane < HEAD_DIM
    n_slab = Q_COLS // LANES

    def stack(q):
        slabs = [q[:, j * LANES:(j + 1) * LANES] * (HEAD_DIM ** -0.5) for j in range(n_slab)]
        parts = ([jnp.where(low, s, 0.0) for s in slabs] + [jnp.where(low, 0.0, s) for s in slabs])
        return jnp.concatenate(parts, axis=0).astype(BF16)

    def unstack(o8):
        return [jnp.where(low, o8[j * tq:(j + 1) * tq, :], o8[(n_slab + j) * tq:(n_slab + j + 1) * tq, :])
                for j in range(n_slab)]

    rows = ATTN_HEADS * tq
    half = rows // 2
    klow = lax.broadcasted_iota(I32, (2 * WINDOW, LANES), 1) < HEAD_DIM
    one = jnp.ones((), BF16)
    q8s = _each(stack, qs)
    ss = _each(lambda q8, kc, b: _bdot_nt(q8, kc) + b, q8s, kcats, biases)
    ms = _each(lambda s: jnp.maximum(jnp.broadcast_to(jnp.max(s, axis=1, keepdims=True), (rows, LANES)),
                                     sink), ss)
    ps = _each(lambda s, m: jnp.exp(s - jnp.concatenate([m, m], axis=1)).astype(BF16), ss, ms)
    pv0 = _each(lambda p, vc: jnp.dot(p[:half], jnp.where(klow, vc, one), preferred_element_type=F32),
                ps, vcats)
    pv1 = _each(lambda p, vc: jnp.dot(p[half:], jnp.where(klow, one, vc), preferred_element_type=F32),
                ps, vcats)
    pvs = _each(lambda a, b: jnp.concatenate([a, b], axis=0), pv0, pv1)
    o8s = _each(lambda pv, m: pv / (pltpu.roll(pv, HEAD_DIM, axis=1) + jnp.exp(sink - m)), pvs, ms)
    return _each(unstack, o8s)


def _attn_prompt_kernel(q_ref, kc_ref, vc_ref, kp_ref, vp_ref, bias0_ref, bias_ref, sink_ref, o_ref, *,
                        nblk):
    kall = jnp.concatenate([kp_ref[...], kc_ref[...]], axis=0).astype(BF16)
    vall = jnp.concatenate([vp_ref[...], vc_ref[...]], axis=0).astype(BF16)
    win = lambda a, j: a[j * WINDOW:(j + 2) * WINDOW, :]
    qs = [q_ref[j * WINDOW:(j + 1) * WINDOW, :] for j in range(nblk)]
    biases = [bias0_ref[0]] + [bias_ref[...]] * (nblk - 1)
    outs = _attn_blocks(qs, [win(kall, j) for j in range(nblk)], [win(vall, j) for j in range(nblk)],
                        biases, sink_ref[...], WINDOW)
    for j, slabs in enumerate(outs):
        for c, slab in enumerate(slabs):
            o_ref[j * WINDOW:(j + 1) * WINDOW, c * LANES:(c + 1) * LANES] = slab


def _attn_sample_kernel(q_ref, kc_ref, vc_ref, kp_ref, vp_ref, bias_ref, sink_ref, o_ref, kw_ref, vw_ref,
                        *, nseq, n_new):
    tq = SAMPLE_SLOTS
    zpad = jnp.zeros((WINDOW - tq, LANES), F32)
    rows = lambda ref, j: ref[j * tq:(j + 1) * tq, :]
    cat = lambda pref, cref, j: jnp.concatenate([pref[j], rows(cref, j), zpad], axis=0).astype(BF16)
    outs = _attn_blocks([rows(q_ref, j) for j in range(nseq)],
                        [cat(kp_ref, kc_ref, j) for j in range(nseq)],
                        [cat(vp_ref, vc_ref, j) for j in range(nseq)],
                        [bias_ref[...]] * nseq, sink_ref[...], tq)
    for j, slabs in enumerate(outs):
        for c, slab in enumerate(slabs):
            o_ref[j * tq:(j + 1) * tq, c * LANES:(c + 1) * LANES] = slab
    row = lax.broadcasted_iota(I32, (WINDOW, LANES), 0)
    keep = WINDOW - n_new
    for pref, cref, wref in ((kp_ref, kc_ref, kw_ref), (vp_ref, vc_ref, vw_ref)):
        for j in range(nseq):
            new = jnp.concatenate([rows(cref, j), zpad], axis=0)
            wref[j] = jnp.where(row < keep, pltpu.roll(pref[j], keep, axis=0),
                                pltpu.roll(new, keep - SAMPLE_FIRST, axis=0))


def _sink_rows(sinks, tq):
    return jnp.broadcast_to(jnp.repeat(sinks.astype(F32), tq)[:, None], (ATTN_HEADS * tq, LANES))


def _attn_bias(tq, q_off, k_lo, k_hi, has_prev):
    qi = (np.arange(ATTN_HEADS * tq) % tq)[:, None]
    c = np.arange(2 * WINDOW)[None, :]
    cj = c - WINDOW
    vis_prev = (c < WINDOW) & (c > qi - q_off) & has_prev
    vis_cur = (c >= WINDOW) & (cj <= qi) & (cj >= k_lo) & (cj <= k_hi)
    return np.where(vis_prev | vis_cur, 0.0, NEG_BIG).astype(np.float32)


def _attn_prompt(q, k, v, sinks, n_seq):
    n = q.shape[0]
    nb = n // n_seq // WINDOW
    nblk = min(ATTN_BLOCKS_PER_STEP, nb)
    steps = nb // nblk
    tq = nblk * WINDOW
    cur = lambda w: pl.BlockSpec((tq, w), lambda b, i: (b * steps + i, 0))
    prev = pl.BlockSpec((WINDOW, LANES), lambda b, i: (b * nb + jnp.maximum(i * nblk - 1, 0), 0))
    bias2 = jnp.asarray(np.stack([_attn_bias(WINDOW, 0, 0, WINDOW - 1, False),
                                  _attn_bias(WINDOW, 0, 0, WINDOW - 1, True)]))
    rows = ATTN_HEADS * WINDOW
    return pl.pallas_call(
        functools.partial(_attn_prompt_kernel, nblk=nblk),
        out_shape=jax.ShapeDtypeStruct((n, Q_COLS), F32), grid=(n_seq, steps),
        in_specs=[cur(Q_COLS), cur(LANES), cur(LANES), prev, prev,
                  pl.BlockSpec((1, rows, 2 * WINDOW), lambda b, i: (jnp.minimum(i, 1), 0, 0)),
                  pl.BlockSpec((rows, 2 * WINDOW), lambda b, i: (0, 0)),
                  pl.BlockSpec((rows, LANES), lambda b, i: (0, 0))],
        out_specs=cur(Q_COLS),
        compiler_params=_cparams(("arbitrary", "arbitrary")), name="attn_prompt",
    )(q, k, v, k, v, bias2, bias2[1], _sink_rows(sinks, WINDOW))


def _attn_sample(q, k, v, cache_k, cache_v, sinks, n_seq, n_new):
    tq = SAMPLE_SLOTS
    nseq = min(ATTN_SEQS_PER_STEP, n_seq)
    cur = lambda w: pl.BlockSpec((nseq * tq, w), lambda b: (b, 0))
    prev = pl.BlockSpec((nseq, WINDOW, LANES), lambda b: (b, 0, 0))
    bias = jnp.asarray(_attn_bias(tq, SAMPLE_FIRST, SAMPLE_FIRST, SAMPLE_FIRST + 3, True))
    win = jax.ShapeDtypeStruct((n_seq, WINDOW, LANES), F32)
    return pl.pallas_call(
        functools.partial(_attn_sample_kernel, nseq=nseq, n_new=n_new),
        out_shape=[jax.ShapeDtypeStruct((n_seq * tq, Q_COLS), F32), win, win], grid=(n_seq // nseq,),
        in_specs=[cur(Q_COLS), cur(LANES), cur(LANES), prev, prev,
                  pl.BlockSpec(bias.shape, lambda b: (0, 0)),
                  pl.BlockSpec((ATTN_HEADS * tq, LANES), lambda b: (0, 0))],
        out_specs=[cur(Q_COLS), prev, prev],
        compiler_params=_cparams(("arbitrary",)), name="attn_sample",
    )(q, k, v, cache_k, cache_v, bias, _sink_rows(sinks, tq))


def _each(f, *lists):
    return [f(*args) for args in zip(*lists)]


def _unit_lower_inverse(ms, eye, same_base, base_only=False):
    c = ms[0].shape[0]

    def neumann(q0s, n_factors):
        xs = _each(lambda q: eye + q, q0s)
        if n_factors == 1:
            return xs
        qs = _each(_bdot, q0s, q0s)
        for _ in range(n_factors - 2):
            prods = _each(lambda x, q: _bdot(jnp.concatenate([x, q], axis=0), q), xs, qs)
            xs = _each(lambda x, pr: x + pr[:c], xs, prods)
            qs = _each(lambda pr: pr[c:], prods)
        return _each(lambda x, q: x + _bdot(x, q), xs, qs)

    ds = _each(lambda m: jnp.where(same_base, m, 0.0), ms)
    xs = neumann(_each(lambda d: -d, ds), int(math.log2(INV_BASE)))
    nblk = c // INV_BASE
    if nblk == 1 or base_only:
        return xs
    ls = _each(lambda m, d: m - d, ms, ds)
    ns = _each(lambda x, l: -_bdot(x, l), xs, ls)
    ys = neumann(ns, int(math.log2(nblk)))
    return _each(_bdot, ys, xs)


def _gdn_intra(qs, ks, vs, gcs, gls, betas, same_seq, low_incl, low_strict, eye, same_base,
               base_only=False):
    del same_seq
    e_gcs = _each(jnp.exp, gcs)

    def decay_of(gc):
        gc_row = jnp.sum(jnp.where(eye > 0, gc, 0.0), axis=0, keepdims=True)
        return jnp.where(low_incl, jnp.exp(jnp.where(low_incl, gc - gc_row, 0.0)), 0.0)

    c = qs[0].shape[0]
    decays = _each(decay_of, gcs)
    kbs = _each(lambda k, b: k * b, ks, betas)
    vbs = _each(lambda v, b: v * b, vs, betas)
    kqs = _each(lambda kb, q, k: _bdot_nt(jnp.concatenate([kb, q], axis=0), k), kbs, qs, ks)
    ms = _each(lambda kq, d: jnp.where(low_strict, kq[:c] * d, 0.0), kqs, decays)
    attns = _each(lambda kq, d: kq[c:] * d, kqs, decays)
    tmats = _unit_lower_inverse(ms, eye, same_base, base_only)
    uws = _each(lambda t, vb, kb, e: _bdot(t, jnp.concatenate([vb, kb * e], axis=1)),
                tmats, vbs, kbs, e_gcs)
    us = _each(lambda uw: uw[:, :GDN_DV], uws)
    ws = _each(lambda uw: uw[:, GDN_DV:], uws)
    q_decs = _each(lambda q, e: q * e, qs, e_gcs)
    k_decs = _each(lambda k, gl, gc: k * jnp.exp(gl - gc), ks, gls, gcs)
    return us, ws, attns, q_decs, k_decs


def _chunk_masks(c, seq_len):
    i = lax.broadcasted_iota(I32, (c, c), 0)
    j = lax.broadcasted_iota(I32, (c, c), 1)
    same_seq = _div_pow2(i, seq_len) == _div_pow2(j, seq_len)
    low_incl = same_seq & (i >= j)
    low_strict = same_seq & (i > j)
    eye = (i == j).astype(F32)
    same_base = _div_pow2(i, INV_BASE) == _div_pow2(j, INV_BASE)
    return same_seq, low_incl, low_strict, eye, same_base


def _gated_rms(o, z, nw):
    o = o * lax.rsqrt(jnp.mean(o * o, axis=1, keepdims=True) + NORM_EPS) * nw
    return o * _silu(z)


def _gdn_prompt_kernel(qg_ref, kg_ref, vg_ref, z_ref, gcb_ref, nw_ref, o_ref, s_out_ref, s_scr):
    c = GDN_CHUNK
    n = pl.program_id(1)

    @pl.when(n == 0)
    def _():
        s_scr[...] = jnp.zeros_like(s_scr)

    masks = _chunk_masks(c, c)
    nw = nw_ref[...]
    chains = [(b, h) for b in range(qg_ref.shape[0]) for h in range(GDN_HEADS)]
    hs = lambda h: slice(h * GDN_DK, (h + 1) * GDN_DK)
    col = lambda off: [gcb_ref[b, :, off + h:off + h + 1] for b, h in chains]
    gcs, betas, gls = col(0), col(GDN_HEADS), col(2 * GDN_HEADS)
    qs = [qg_ref[b, :, hs(h)] for b, h in chains]
    ks = [kg_ref[b, :, hs(h)] for b, h in chains]
    vs = [vg_ref[b, :, hs(h)] for b, h in chains]
    us, ws, attns, q_decs, k_decs = _gdn_intra(qs, ks, vs, gcs, gls, betas, *masks)
    ss = [s_scr[b, h] for b, h in chains]
    wqs = _each(lambda w, qd, s: _bdot(jnp.concatenate([w, qd], axis=0), s), ws, q_decs, ss)
    wss = _each(lambda wq: wq[:c], wqs)
    qss = _each(lambda wq: wq[c:], wqs)
    v_news = _each(lambda u, x: u - x, us, wss)
    avs = _each(_bdot, attns, v_news)
    kvs = _each(_bdot_tn, k_decs, v_news)
    for (b, h), s, gl, qsv, av, kv in zip(chains, ss, gls, qss, avs, kvs):
        s_scr[b, h] = s * jnp.exp(gl[0:1, :]) + kv
        o_ref[b, :, hs(h)] = _gated_rms(qsv + av, z_ref[b, :, hs(h)], nw)

    @pl.when(n == pl.num_programs(1) - 1)
    def _():
        s_out_ref[...] = s_scr[...]


def _gdn_prompt(qg, kg, vg, z, gcb, norm_w, n_seq):
    n = qg.shape[0]
    s_len = n // n_seq
    nb = min(GDN_SEQ_PER_STEP, n_seq)
    v3 = lambda a: a.reshape(n_seq, s_len, a.shape[-1])
    tok = lambda w: pl.BlockSpec((nb, GDN_CHUNK, w), lambda b, i: (b, i, 0))
    o, s = pl.pallas_call(
        _gdn_prompt_kernel,
        out_shape=[jax.ShapeDtypeStruct((n_seq, s_len, Z_COLS), F32),
                   jax.ShapeDtypeStruct((n_seq, GDN_HEADS, GDN_DK, GDN_DV), F32)],
        grid=(n_seq // nb, s_len // GDN_CHUNK),
        in_specs=[tok(QK_COLS), tok(QK_COLS), tok(Z_COLS), tok(Z_COLS), tok(LANES),
                  pl.BlockSpec((1, GDN_DV), lambda b, i: (0, 0))],
        out_specs=[tok(Z_COLS),
                   pl.BlockSpec((nb, GDN_HEADS, GDN_DK, GDN_DV), lambda b, i: (b, 0, 0, 0))],
        scratch_shapes=[pltpu.VMEM((nb, GDN_HEADS, GDN_DK, GDN_DV), F32)],
        compiler_params=_cparams(("arbitrary", "arbitrary")), name="gdn_prompt",
    )(v3(qg), v3(kg), v3(vg), v3(z), v3(gcb), norm_w)
    return o.reshape(n, Z_COLS), s


def _gdn_sample_kernel(qg_ref, kg_ref, vg_ref, z_ref, gcb_ref, nw_ref, s_in_ref, o_ref, s_out_ref):
    c = GDN_CHUNK
    n_sub = c // SAMPLE_SLOTS
    masks = _chunk_masks(c, SAMPLE_SLOTS)
    heads = range(GDN_HEADS)
    hs = lambda h: slice(h * GDN_DK, (h + 1) * GDN_DK)
    rs = lambda s: slice(s * SAMPLE_SLOTS, (s + 1) * SAMPLE_SLOTS)
    col = lambda off: [gcb_ref[:, off + h:off + h + 1] for h in heads]
    gcs, betas, gls = col(0), col(GDN_HEADS), col(2 * GDN_HEADS)
    us, ws, attns, q_decs, k_decs = _gdn_intra([qg_ref[:, hs(h)] for h in heads],
                                               [kg_ref[:, hs(h)] for h in heads],
                                               [vg_ref[:, hs(h)] for h in heads], gcs, gls, betas, *masks,
                                               base_only=SAMPLE_SLOTS <= INV_BASE)
    pairs = [(h, s) for h in heads for s in range(n_sub)]
    sts = [s_in_ref[s, h] for h, s in pairs]
    boths = [jnp.concatenate([ws[h][rs(s), :], q_decs[h][rs(s), :]], axis=0) for h, s in pairs]
    rr = _each(_bdot, boths, sts)
    gather = lambda h, part: jnp.concatenate(
        [rr[h * n_sub + s][part * SAMPLE_SLOTS:(part + 1) * SAMPLE_SLOTS, :] for s in range(n_sub)], axis=0)
    v_news = [us[h] - gather(h, 0) for h in heads]
    avs = _each(_bdot, attns, v_news)
    row = lax.broadcasted_iota(I32, (c, LANES), 0)
    seq_of_row = _div_pow2(row, SAMPLE_SLOTS)
    kds = [jnp.where(seq_of_row == s, k_decs[h], 0.0) for h, s in pairs]
    kvs = _each(_bdot_tn, kds, [v_news[h] for h, _ in pairs])
    egls = _each(jnp.exp, gls)
    for (h, s), st, kv in zip(pairs, sts, kvs):
        s_out_ref[s, h] = st * egls[h][s * SAMPLE_SLOTS:s * SAMPLE_SLOTS + 1, :] + kv
    nw = nw_ref[...]
    for h in heads:
        o_ref[:, hs(h)] = _gated_rms(gather(h, 1) + avs[h], z_ref[:, hs(h)], nw)


def _gdn_sample(qg, kg, vg, z, gcb, norm_w, state):
    n = qg.shape[0]
    n_sub = GDN_CHUNK // SAMPLE_SLOTS
    tok = lambda w: pl.BlockSpec((GDN_CHUNK, w), lambda i: (i, 0))
    st = pl.BlockSpec((n_sub, GDN_HEADS, GDN_DK, GDN_DV), lambda i: (i, 0, 0, 0))
    return pl.pallas_call(
        _gdn_sample_kernel,
        out_shape=[jax.ShapeDtypeStruct((n, Z_COLS), F32),
                   jax.ShapeDtypeStruct(state.shape, F32)],
        grid=(n // GDN_CHUNK,),
        in_specs=[tok(QK_COLS), tok(QK_COLS), tok(Z_COLS), tok(Z_COLS), tok(LANES),
                  pl.BlockSpec((1, GDN_DV), lambda i: (0, 0)), st],
        out_specs=[tok(Z_COLS), st],
        compiler_params=_cparams(("arbitrary",)), name="gdn_sample",
    )(qg, kg, vg, z, gcb, norm_w, state)


def _post_kernel(a_ref, g_ref, x_ref, wo_ref, ln_g_ref, ln_b_ref, wr_ref, x1_ref, route_ref, *, tm):
    sub = POST_SUB
    rows = [slice(j * sub, (j + 1) * sub) for j in range(tm // sub)]
    d = lambda a, b: jnp.dot(a, b, preferred_element_type=F32)
    mixes = [d(a_ref[r, :].astype(BF16), wo_ref[0:Q_COLS, :]) + d(g_ref[r, :].astype(BF16), wo_ref[Q_COLS:, :])
             for r in rows]
    x1s = [_layer_norm(DEEPNORM_ALPHA * x_ref[r, :] + mix, ln_g_ref[...], ln_b_ref[...])
           for r, mix in zip(rows, mixes)]
    for r, x1 in zip(rows, x1s):
        x1_ref[r, :] = x1
    w2 = wr_ref[...]
    lgs = []
    for x1 in x1s:
        xh = x1.astype(BF16)
        xm = (x1 - xh.astype(F32)).astype(BF16)
        both = d(xh, w2)
        lgs.append(both[:, :LANES] + both[:, LANES:] + d(xm, w2[:, :LANES]))
    for r, lg in zip(rows, lgs):
        route_ref[:, r] = jnp.transpose(_route(lg))[0:SUBLANES, :]


def _route(lg):
    lane = lax.broadcasted_iota(I32, lg.shape, 1)
    lane_f = lane.astype(F32)
    big = float(LANES)

    def first_max(vals, mask):
        v = jnp.where(mask, vals, NEG_BIG)
        mx = jnp.max(v, axis=1, keepdims=True)
        idx = jnp.min(jnp.where(mask & (v == mx), lane_f, big), axis=1, keepdims=True)
        return mx, idx

    gmask = lane < N_GROUPS
    gmax, gidx = first_max(lg, gmask)
    gden = jnp.sum(jnp.where(gmask, jnp.exp(lg - gmax), 0.0), axis=1, keepdims=True)
    g_top_p = 1.0 / gden
    e_lane = lane - N_GROUPS
    e_group = _div_pow2(jnp.maximum(e_lane, 0), EXPERTS_PER_GROUP).astype(F32)
    emask = (e_lane >= 0) & (e_lane < N_EXPERTS) & (e_group == gidx)
    m1, i1 = first_max(lg, emask)
    eden = jnp.sum(jnp.where(emask, jnp.exp(lg - m1), 0.0), axis=1, keepdims=True)
    m2, i2 = first_max(lg, emask & (lane_f != i1))
    p1 = 1.0 / eden
    p2 = jnp.exp(m2 - m1) / eden
    tot = p1 + p2
    gate1 = g_top_p * (p1 / tot)
    gate2 = g_top_p * (p2 / tot)
    return jnp.where(lane == 0, gate1,
                     jnp.where(lane == 1, gate2,
                               jnp.where(lane == 2, i1 - N_GROUPS,
                                         jnp.where(lane == 3, i2 - N_GROUPS, 0.0))))


def _post(attn_o, gdn_o, x, wts):
    n = x.shape[0]
    tm = min(POST_TILE, n)
    tok = lambda w: pl.BlockSpec((tm, w), lambda i: (i, 0))
    const = lambda a: pl.BlockSpec(a.shape, lambda i: (0,) * a.ndim)
    consts = [wts['wo'], wts['ln1_g'], wts['ln1_b'], wts['wr']]
    return pl.pallas_call(
        functools.partial(_post_kernel, tm=tm),
        out_shape=[jax.ShapeDtypeStruct((n, D_MODEL), F32), jax.ShapeDtypeStruct((SUBLANES, n), F32)],
        grid=(n // tm,),
        in_specs=[tok(Q_COLS), tok(Z_COLS), tok(D_MODEL)] + [const(a) for a in consts],
        out_specs=[tok(D_MODEL), pl.BlockSpec((SUBLANES, tm), lambda i: (0, i))],
        compiler_params=_cparams(("arbitrary",)), name="post_%d" % (n // tm),
    )(attn_o, gdn_o, x, *consts)


def _slab_loop(n, body):
    n_main = jnp.right_shift(n, int(math.log2(SLAB_UNROLL)))

    def main(i, c):
        for u in range(SLAB_UNROLL):
            body(i * SLAB_UNROLL + u, u)
        return c

    lax.fori_loop(0, n_main, main, 0)
    lax.fori_loop(n_main * SLAB_UNROLL, n, lambda j, c: (body(j, 0), c)[1], 0)


def _dispatch_kernel(dst_ref, nslab_ref, ztab_ref, zinfo_ref, slot_ref, gate_ref, *rest,
                     group_tiles, max_tiles):
    x_refs = rest[:len(group_tiles)]
    xs_ref, pbuf, sem, zbuf, zsem = rest[len(group_tiles):]
    n_tiles = sum(group_tiles)
    g = pl.program_id(0)
    cur = lax.rem(g, 2)

    def slab_copy(tile, buf_slot, j):
        d = pl.multiple_of(dst_ref[tile * PERM_SLABS + j], SLAB)
        src = pbuf.at[buf_slot, pl.ds(pl.multiple_of(j * SLAB, SLAB), SLAB), :]
        return pltpu.make_async_copy(src, xs_ref.at[pl.ds(d, SLAB), :], sem.at[buf_slot])

    def tail_copy(k):
        d = pl.multiple_of(ztab_ref[k], SLAB)
        return pltpu.make_async_copy(zbuf.at[pl.ds(0, SLAB), :], xs_ref.at[pl.ds(d, SLAB), :], zsem)

    def tile_copy(t):
        d = pl.multiple_of(t * ROW_TILE, ROW_TILE)
        return pltpu.make_async_copy(zbuf, xs_ref.at[pl.ds(d, ROW_TILE), :], zsem)

    @pl.when(g == 0)
    def _():
        zbuf[...] = jnp.zeros_like(zbuf)

    share = -(-ZERO_TABLE // n_tiles)
    k0 = g * share
    _slab_loop(jnp.clip(zinfo_ref[0] - k0, 0, share), lambda j, u: tail_copy(k0 + j).start(priority=1))

    @pl.when(zinfo_ref[1] + g < max_tiles)
    def _():
        tile_copy(zinfo_ref[1] + g).start(priority=1)

    x = x_refs[-1][...]
    bound = n_tiles
    for x_ref, nt in zip(x_refs[-2::-1], group_tiles[:0:-1]):
        bound -= nt
        x = jnp.where(g < bound, x_ref[...], x)

    r = lax.broadcasted_iota(I32, (PERM_ROWS, TOK_TILE), 0)
    sl = slot_ref[0]
    hit0, hit1 = r == sl[0:1, :], r == sl[1:2, :]
    onehot = jnp.where(hit0 | hit1, 1.0, 0.0).astype(BF16)
    gt = gate_ref[0]
    gcol = jnp.sum(jnp.where(hit0, gt[0:1, :], 0.0) + jnp.where(hit1, gt[1:2, :], 0.0),
                   axis=1, keepdims=True)
    pbuf[cur, :, 0:D_MODEL] = jnp.dot(onehot, x.astype(BF16), preferred_element_type=F32).astype(BF16)
    g_hi = gcol.astype(BF16).astype(F32)
    lane = lax.broadcasted_iota(I32, (PERM_ROWS, LANES), 1)
    pbuf[cur, :, D_MODEL:] = jnp.where(lane < LANES // 2, g_hi, gcol - g_hi).astype(BF16)

    @pl.when(g > 0)
    def _():
        _slab_loop(nslab_ref[g - 1], lambda j, u: slab_copy(g - 1, 1 - cur, j).wait())

    _slab_loop(nslab_ref[g], lambda j, u: slab_copy(g, cur, j).start(priority=u % 2))

    @pl.when(g == n_tiles - 1)
    def _():
        _slab_loop(nslab_ref[g], lambda j, u: slab_copy(g, cur, j).wait())
        lax.fori_loop(zinfo_ref[1] + n_tiles, max_tiles,
                      lambda t, c: (tile_copy(t).start(priority=1), c)[1], 0)
        _slab_loop(zinfo_ref[0], lambda k, u: tail_copy(k).wait())
        lax.fori_loop(zinfo_ref[1], max_tiles, lambda t, c: (tile_copy(t).wait(), c)[1], 0)


def _dispatch(plan, x1s, max_tiles):
    group_tiles = tuple(x1.shape[0] // TOK_TILE for x1 in x1s)
    n_tiles = sum(group_tiles)
    tile = lambda i, d, ns, zt, zi: (i, 0, 0)
    in_specs = [pl.BlockSpec((1, TOP_K, TOK_TILE), tile), pl.BlockSpec((1, TOP_K, TOK_TILE), tile)]
    base = 0
    for nt in group_tiles:
        in_specs.append(pl.BlockSpec(
            (TOK_TILE, D_MODEL),
            lambda i, d, ns, zt, zi, base=base, nt=nt: (jnp.clip(i - base, 0, nt - 1), 0)))
        base += nt
    return pl.pallas_call(
        functools.partial(_dispatch_kernel, group_tiles=group_tiles, max_tiles=max_tiles),
        out_shape=jax.ShapeDtypeStruct((max_tiles * ROW_TILE, XS_WORDS), BF16),
        grid_spec=pltpu.PrefetchScalarGridSpec(
            num_scalar_prefetch=4, grid=(n_tiles,), in_specs=in_specs,
            out_specs=pl.BlockSpec(memory_space=pl.ANY),
            scratch_shapes=[pltpu.VMEM((2, PERM_ROWS, XS_WORDS), BF16), pltpu.SemaphoreType.DMA((2,)),
                            pltpu.VMEM((ROW_TILE, XS_WORDS), BF16), pltpu.SemaphoreType.DMA(())]),
        compiler_params=_cparams(("arbitrary",)), name="moe_dispatch",
    )(plan['slab_dst'], plan['nslab'], plan['ztab'], plan['zinfo'], plan['slot_rows'], plan['gate_rows'],
      *x1s)


def _expert_kernel(t0_ref, nt_ref, nu_ref, wg_ref, wu_ref, wd_ref, xs_ref, ye_ref,
                   xbuf, ybuf, wgu_scr, wd_scr, zbuf, in_sem, out_sem, zsem, *, max_tiles):
    e = pl.program_id(0)
    n_used = nu_ref[0]
    ahead = EXPERT_IN_SLOTS - 1

    def in_copy(t):
        slot = lax.rem(t, EXPERT_IN_SLOTS)
        src = xs_ref.at[pl.ds(pl.multiple_of(t * ROW_TILE, ROW_TILE), ROW_TILE), :]
        return pltpu.make_async_copy(src, xbuf.at[slot], in_sem.at[slot])

    def out_copy(t):
        slot = lax.rem(t, 2)
        dst = ye_ref.at[pl.ds(pl.multiple_of(t * ROW_TILE, ROW_TILE), ROW_TILE), :]
        return pltpu.make_async_copy(ybuf.at[slot], dst, out_sem.at[slot])

    def zero_copy(t):
        dst = ye_ref.at[pl.ds(pl.multiple_of(t * ROW_TILE, ROW_TILE), ROW_TILE), :]
        return pltpu.make_async_copy(zbuf, dst, zsem)

    @pl.when(e == 0)
    def _():
        for t in range(ahead):
            @pl.when(t < n_used)
            def _():
                in_copy(t).start()

        zbuf[...] = jnp.zeros_like(zbuf)
        lax.fori_loop(n_used, max_tiles, lambda t, c: (zero_copy(t).start(), c)[1], 0)

    wgu_scr[:, 0:EXPERT_FF] = wg_ref[0].astype(BF16)
    wgu_scr[:, EXPERT_FF:] = wu_ref[0].astype(BF16)
    wd_scr[...] = wd_ref[0].astype(BF16)

    sub = EXPERT_SUB
    rows = [slice(j * sub, (j + 1) * sub) for j in range(ROW_TILE // sub)]
    d = lambda a, b: jnp.dot(a, b, preferred_element_type=F32)

    def tile_body(j, carry):
        t = t0_ref[e] + j
        slot = lax.rem(t, 2)
        in_slot = lax.rem(t, EXPERT_IN_SLOTS)

        @pl.when(t + ahead < n_used)
        def _():
            in_copy(t + ahead).start()

        in_copy(t).wait()

        @pl.when(t >= 2)
        def _():
            out_copy(t - 2).wait()

        wgu, wd = wgu_scr[...], wd_scr[...]
        xs = [xbuf[in_slot, r, 0:D_MODEL] for r in rows]
        hs = [d(x, wgu) for x in xs]
        hhs = [(_silu(h[:, :EXPERT_FF]) * h[:, EXPERT_FF:]).astype(BF16) for h in hs]
        ys = [d(hh, wd) for hh in hhs]
        for r, y in zip(rows, ys):
            parts = xbuf[in_slot, r, D_MODEL:].astype(F32)
            gate = parts + pltpu.roll(parts, LANES // 2, axis=1)
            ybuf[slot, r, :] = (y * jnp.concatenate([gate] * (D_MODEL // LANES), axis=1)).astype(BF16)
        out_copy(t).start()
        return carry

    lax.fori_loop(0, nt_ref[e], tile_body, 0)

    @pl.when(e == pl.num_programs(0) - 1)
    def _():
        @pl.when(n_used >= 2)
        def _():
            out_copy(n_used - 2).wait()

        out_copy(n_used - 1).wait()
        lax.fori_loop(n_used, max_tiles, lambda t, c: (zero_copy(t).wait(), c)[1], 0)


def _experts(plan, xs, w_gate, w_up, w_down):
    max_tiles = xs.shape[0] // ROW_TILE
    wsel = lambda e, t0, nt, nu: (e, 0, 0)
    return pl.pallas_call(
        functools.partial(_expert_kernel, max_tiles=max_tiles),
        out_shape=jax.ShapeDtypeStruct((xs.shape[0], D_MODEL), BF16),
        grid_spec=pltpu.PrefetchScalarGridSpec(
            num_scalar_prefetch=3, grid=(N_EXPERTS,),
            in_specs=[pl.BlockSpec((1, D_MODEL, EXPERT_FF), wsel),
                      pl.BlockSpec((1, D_MODEL, EXPERT_FF), wsel),
                      pl.BlockSpec((1, EXPERT_FF, D_MODEL), wsel),
                      pl.BlockSpec(memory_space=pl.ANY)],
            out_specs=pl.BlockSpec(memory_space=pl.ANY),
            scratch_shapes=[pltpu.VMEM((EXPERT_IN_SLOTS, ROW_TILE, XS_WORDS), BF16),
                            pltpu.VMEM((2, ROW_TILE, D_MODEL), BF16),
                            pltpu.VMEM((D_MODEL, 2 * EXPERT_FF), BF16),
                            pltpu.VMEM((EXPERT_FF, D_MODEL), BF16),
                            pltpu.VMEM((ROW_TILE, D_MODEL), BF16),
                            pltpu.SemaphoreType.DMA((EXPERT_IN_SLOTS,)), pltpu.SemaphoreType.DMA((2,)),
                            pltpu.SemaphoreType.DMA(())]),
        compiler_params=_cparams(("arbitrary",)), name="moe_experts",
    )(plan['tile_start'], plan['tile_count'], plan['n_used'], w_gate, w_up, w_down, xs)


def _combine_kernel(dst_ref, nslab_ref, x1_ref, slot_ref, ye_ref, ln_g_ref, ln_b_ref, y_ref,
                    buf, sem, *, tile_base, n_tiles):
    i = pl.program_id(0)
    g = tile_base + i
    cur = lax.rem(i, 2)

    def slab_copy(tile, buf_slot, j):
        d = pl.multiple_of(dst_ref[tile * PERM_SLABS + j], SLAB)
        dst = buf.at[buf_slot, pl.ds(pl.multiple_of(j * SLAB, SLAB), SLAB), :]
        return pltpu.make_async_copy(ye_ref.at[pl.ds(d, SLAB), :], dst, sem.at[buf_slot])

    @pl.when(i == 0)
    def _():
        buf[...] = jnp.zeros_like(buf)
        _slab_loop(nslab_ref[g], lambda j, u: slab_copy(g, cur, j).start(priority=u % 2))

    @pl.when(i + 1 < n_tiles)
    def _():
        _slab_loop(nslab_ref[g + 1], lambda j, u: slab_copy(g + 1, 1 - cur, j).start(priority=u % 2))

    _slab_loop(nslab_ref[g], lambda j, u: slab_copy(g, cur, j).wait())

    sub = COMBINE_SUB
    subs = [slice(j * sub, (j + 1) * sub) for j in range(TOK_TILE // sub)]
    col = lax.broadcasted_iota(I32, (sub, PERM_ROWS), 1)
    sl = slot_ref[0]
    diag = (lax.broadcasted_iota(I32, (sub, sub), 0) == lax.broadcasted_iota(I32, (sub, sub), 1))
    as_col = lambda row: jnp.sum(jnp.where(diag, row, 0), axis=1, keepdims=True)
    picks = [jnp.where((col == as_col(sl[0:1, r])) | (col == as_col(sl[1:2, r])), 1.0, 0.0).astype(BF16)
             for r in subs]
    rows = buf[cur]
    moes = [jnp.dot(pick, rows, preferred_element_type=F32) for pick in picks]
    for r, moe in zip(subs, moes):
        y_ref[r, :] = _layer_norm(DEEPNORM_ALPHA * x1_ref[r, :] + moe, ln_g_ref[...], ln_b_ref[...])


def _combine(plan, tile_base, x1, ye, ln_g, ln_b):
    n = x1.shape[0]
    n_tiles = n // TOK_TILE
    tok = lambda w: pl.BlockSpec((TOK_TILE, w), lambda i, d, ns: (i, 0))
    const = lambda a: pl.BlockSpec(a.shape, lambda i, d, ns: (0,) * a.ndim)
    return pl.pallas_call(
        functools.partial(_combine_kernel, tile_base=tile_base, n_tiles=n_tiles),
        out_shape=jax.ShapeDtypeStruct((n, D_MODEL), F32),
        grid_spec=pltpu.PrefetchScalarGridSpec(
            num_scalar_prefetch=2, grid=(n_tiles,),
            in_specs=[tok(D_MODEL),
                      pl.BlockSpec((1, TOP_K, TOK_TILE), lambda i, d, ns: (tile_base + i, 0, 0)),
                      pl.BlockSpec(memory_space=pl.ANY), const(ln_g), const(ln_b)],
            out_specs=tok(D_MODEL),
            scratch_shapes=[pltpu.VMEM((2, PERM_ROWS, D_MODEL), BF16), pltpu.SemaphoreType.DMA((2,))]),
        compiler_params=_cparams(("arbitrary",)), name="moe_combine_%d" % tile_base,
    )(plan['slab_dst'], plan['nslab'], x1, plan['slot_rows'], ye, ln_g, ln_b)


def _routing_plan(ids, gates):
    nt = ids.shape[1] // TOK_TILE
    pairs = TOP_K * TOK_TILE
    ex = jnp.arange(N_EXPERTS, dtype=I32)
    per_tile = lambda a: jnp.swapaxes(a.reshape(TOP_K, nt, TOK_TILE), 0, 1)
    flat = per_tile(ids).reshape(nt, pairs)
    onehot = (flat[:, None, :] == ex[None, :, None])
    p = np.arange(pairs)
    triu = jnp.asarray(p[:, None] <= p[None, :], BF16)
    csum = jnp.dot(onehot.astype(BF16).reshape(nt * N_EXPERTS, pairs), triu,
                   preferred_element_type=F32).astype(I32).reshape(nt, N_EXPERTS, pairs)
    oh = onehot.astype(I32)
    rank = jnp.sum(oh * (csum - 1), axis=1)
    cnt = csum[:, :, -1]
    cpad = (cnt + SLAB - 1) // SLAB * SLAB
    seg_end = jnp.cumsum(cpad, axis=1)
    seg_off = seg_end - cpad
    slot = jnp.sum(oh * seg_off[:, :, None], axis=1) + rank
    run_end = jnp.cumsum(cpad, axis=0)
    ntiles_e = (run_end[-1] + ROW_TILE - 1) // ROW_TILE
    tile_end = jnp.cumsum(ntiles_e)
    dst_run = ((tile_end - ntiles_e) * ROW_TILE)[None, :] + run_end - cpad
    j8 = jnp.arange(PERM_SLABS, dtype=I32) * SLAB
    e_of = jnp.minimum(jnp.sum((j8[None, :, None] >= seg_end[:, None, :]).astype(I32), axis=2),
                       N_EXPERTS - 1)
    sel = (e_of[:, :, None] == ex).astype(I32)
    slab_dst = jnp.sum(sel * (dst_run - seg_off)[:, None, :], axis=2) + j8[None, :]
    n_used = tile_end[-1]
    row_start = (tile_end - ntiles_e) * ROW_TILE
    tail_cnt = (ntiles_e * ROW_TILE - run_end[-1]) // SLAB
    tail_end = jnp.cumsum(tail_cnt)
    k = jnp.arange(ZERO_TABLE, dtype=I32)
    e_k = jnp.minimum(jnp.sum((k[:, None] >= tail_end[None, :]).astype(I32), axis=1), N_EXPERTS - 1)
    base_k = jnp.sum((e_k[:, None] == ex).astype(I32)
                     * (row_start + run_end[-1] - SLAB * (tail_end - tail_cnt))[None, :], axis=1)
    return dict(
        slab_dst=slab_dst.reshape(-1).astype(I32), nslab=(seg_end[:, -1] // SLAB).astype(I32),
        ztab=(base_k + SLAB * k).astype(I32), zinfo=jnp.stack([tail_end[-1], n_used]).astype(I32),
        slot_rows=slot.reshape(nt, TOP_K, TOK_TILE).astype(I32),
        gate_rows=per_tile(gates).astype(F32),
        tile_start=(tile_end - ntiles_e).astype(I32), tile_count=ntiles_e.astype(I32),
        n_used=n_used.reshape(1).astype(I32))


def _max_row_tiles(n_tokens):
    rows = TOP_K * n_tokens + (n_tokens // TOK_TILE) * N_EXPERTS * (SLAB - 1)
    return (rows + ROW_TILE - 1) // ROW_TILE + N_EXPERTS


def _moe(x1s, routes, wts):
    ids = jnp.concatenate([r[TOP_K:2 * TOP_K, :] for r in routes], axis=1).astype(I32)
    gates = jnp.concatenate([r[0:TOP_K, :] for r in routes], axis=1)
    plan = _routing_plan(ids, gates)
    max_tiles = _max_row_tiles(ids.shape[1])
    bases = [0]
    for x1 in x1s[:-1]:
        bases.append(bases[-1] + x1.shape[0] // TOK_TILE)
    xs = _dispatch(plan, x1s, max_tiles)
    ye = _experts(plan, xs, wts['w_gate'], wts['w_up'], wts['w_down'])
    return [_combine(plan, base, x1, ye, wts['ln2_g'], wts['ln2_b']) for base, x1 in zip(bases, x1s)]


def _prep_weights(w_in, w_out, conv_w, a_log, dt_bias, gdn_norm_w, ln1_g, ln1_b, w_router_group,
                  w_router_expert, w_gate, w_up, w_down, ln2_g, ln2_b):
    pad_row = lambda v: jnp.pad(v.astype(F32), (0, LANES - v.shape[0]))[None, :]
    wr = jnp.pad(jnp.concatenate([w_router_group, w_router_expert], axis=1),
                 ((0, 0), (0, LANES - N_GROUPS - N_EXPERTS)))
    wr_hi = wr.astype(BF16)
    wr_mid = (wr - wr_hi.astype(F32)).astype(BF16)
    group = ATTN_HEADS // ATTN_KV_HEADS
    wq = w_in[:, :Q_COLS].reshape(D_MODEL, ATTN_KV_HEADS, group, HEAD_DIM)
    wq = jnp.swapaxes(wq, 1, 2).reshape(D_MODEL, Q_COLS)
    w_all = jnp.concatenate([wq, w_in[:, Q_COLS:]], axis=1)
    w_all = jnp.pad(w_all, ((0, 0), (0, IN_COLS_PAD - w_all.shape[1]))).astype(BF16)
    wo_q = w_out[:Q_COLS].reshape(ATTN_KV_HEADS, group, HEAD_DIM, D_MODEL)
    wo_q = jnp.swapaxes(wo_q, 0, 1).reshape(Q_COLS, D_MODEL)
    wo = jnp.concatenate([wo_q, w_out[Q_COLS:]], axis=0)
    return dict(
        w_all=w_all, convw=conv_w.astype(F32), alog=pad_row(a_log), dtb=pad_row(dt_bias),
        norm_w=gdn_norm_w.astype(F32)[None, :], wo=wo.astype(BF16),
        ln1_g=ln1_g[None, :], ln1_b=ln1_b[None, :], wr=jnp.concatenate([wr_hi, wr_mid], axis=1),
        w_gate=w_gate, w_up=w_up, w_down=w_down, ln2_g=ln2_g[None, :], ln2_b=ln2_b[None, :])


def _layer(x_prompt, x_sample, cache_k, cache_v, state_gdn, state_conv, wts):
    bp, sp, _ = x_prompt.shape
    bs, ts, _ = x_sample.shape
    n_p = bp * sp

    xp = x_prompt.reshape(n_p, D_MODEL)
    (q, k, v, qg, kg, vg, z, gcb, utail) = _proj(xp, np.arange(sp), wts, GDN_CHUNK, bp)
    attn_p = _attn_prompt(q, k, v, wts['sinks'], bp)
    gdn_p, s_p = _gdn_prompt(qg, kg, vg, z, gcb, wts['norm_w'], bp)
    last_win = lambda a: a.reshape(bp, sp, KV_COLS)[:, sp - WINDOW:].reshape(bp, WINDOW, ATTN_KV_HEADS,
                                                                            HEAD_DIM)
    new_k_p, new_v_p = last_win(k), last_win(v)
    tiles_per_seq = sp // min(PROJ_TILE, sp)
    conv_p = utail.reshape(bp, tiles_per_seq, SUBLANES, CONV_DIM)[:, -1, SUBLANES - (CONV_W - 1):]

    lo, hi = SAMPLE_FIRST, SAMPLE_FIRST + ts
    xs_rows = jnp.pad(x_sample, ((0, 0), (lo, SAMPLE_SLOTS - hi), (0, 0))).reshape(bs * SAMPLE_SLOTS, D_MODEL)
    hist = jnp.pad(state_conv, ((0, 0), (0, SAMPLE_SLOTS - lo), (0, 0))).reshape(bs * SAMPLE_SLOTS, CONV_DIM)
    slot = np.arange(SAMPLE_SLOTS)
    valid = jnp.asarray(np.tile((slot >= lo) & (slot < hi), bs)[:, None], F32)
    pos_s = np.tile(PAST_LEN + slot - lo, bs)
    (q, k, v, qg, kg, vg, z, gcb, u_s) = _proj(xs_rows, pos_s, wts, SAMPLE_SLOTS, 1, hist, valid)
    ck = cache_k.reshape(bs, WINDOW, KV_COLS)
    cv = cache_v.reshape(bs, WINDOW, KV_COLS)
    attn_s, kwin, vwin = _attn_sample(q, k, v, ck, cv, wts['sinks'], bs, ts)
    gdn_s, s_s = _gdn_sample(qg, kg, vg, z, gcb, wts['norm_w'], state_gdn)
    real = lambda a: a.reshape(bs, SAMPLE_SLOTS, -1)[:, lo:hi]
    new_k_s = kwin.reshape(bs, WINDOW, ATTN_KV_HEADS, HEAD_DIM)
    new_v_s = vwin.reshape(bs, WINDOW, ATTN_KV_HEADS, HEAD_DIM)
    conv_s = u_s.reshape(bs, SAMPLE_SLOTS, CONV_DIM)[:, hi - (CONV_W - 1):hi]

    x1_p, route_p = _post(attn_p, gdn_p, xp, wts)
    x1_s, route_s = _post(real(attn_s).reshape(bs * ts, Q_COLS), real(gdn_s).reshape(bs * ts, Z_COLS),
                          x_sample.reshape(bs * ts, D_MODEL), wts)
    y_p, y_s = _moe([x1_p, x1_s], [route_p, route_s], wts)
    return (y_p.reshape(bp, sp, D_MODEL), y_s.reshape(bs, ts, D_MODEL), new_k_p, new_v_p, s_p, conv_p,
            new_k_s, new_v_s, s_s, conv_s)


def kernel(x_prompt, x_sample, cache_attn_k, cache_attn_v, state_gdn, state_conv, w_in, w_out,
           attn_sinks, conv_w, a_log, dt_bias, gdn_norm_w, ln1_g, ln1_b, w_router_group,
           w_router_expert, w_gate, w_up, w_down, ln2_g, ln2_b):
    assert w_in.shape[0] == DEPTH
    l = 0
    wts = _prep_weights(w_in[l], w_out[l], conv_w[l], a_log[l], dt_bias[l], gdn_norm_w[l], ln1_g[l],
                        ln1_b[l], w_router_group[l], w_router_expert[l], w_gate[l], w_up[l],
                        w_down[l], ln2_g[l], ln2_b[l])
    wts['sinks'] = attn_sinks[l]
    outs = _layer(x_prompt, x_sample, cache_attn_k[l], cache_attn_v[l], state_gdn[l], state_conv[l], wts)
    (y_p, y_s, k_p, v_p, s_p, c_p, k_s, v_s, s_s, c_s) = outs
    add = lambda a: a[None]
    return (y_p, y_s, add(k_p), add(v_p), add(s_p), add(c_p), add(k_s), add(v_s), add(s_s), add(c_s))
```

```python
import functools
import math

import jax
import jax.numpy as jnp
import numpy as np
from jax import lax
from jax.experimental import pallas as pl
from jax.experimental.pallas import tpu as pltpu

F32 = jnp.float32
BF16 = jnp.bfloat16
I32 = jnp.int32

D_MODEL = 1024
ATTN_HEADS = 8
ATTN_KV_HEADS = 2
HEAD_DIM = 64
WINDOW = 128
ROT_DIM = HEAD_DIM // 4
ROPE_THETA = 500000.0
GDN_HEADS = 4
GDN_DK = 128
GDN_DV = 128
CONV_W = 4
QK_COLS = GDN_HEADS * GDN_DK
CONV_DIM = 2 * QK_COLS + GDN_HEADS * GDN_DV
Z_COLS = GDN_HEADS * GDN_DV
Q_COLS = ATTN_HEADS * HEAD_DIM
KV_COLS = ATTN_KV_HEADS * HEAD_DIM
N_GROUPS = 4
EXPERTS_PER_GROUP = 8
N_EXPERTS = N_GROUPS * EXPERTS_PER_GROUP
TOP_K = 2
EXPERT_FF = 256
NORM_EPS = 1e-5
L2_EPS = 1e-6
DEPTH = 1
DEEPNORM_ALPHA = (2 * DEPTH) ** 0.25
PAST_LEN = 8192

LANES = 128
SUBLANES = 8
IN_SPLITS = (0, Q_COLS + 2 * KV_COLS, Q_COLS + 2 * KV_COLS + CONV_DIM,
             Q_COLS + 2 * KV_COLS + CONV_DIM + Z_COLS, Q_COLS + 2 * KV_COLS + CONV_DIM + Z_COLS + LANES)
IN_COLS_PAD = IN_SPLITS[-1]
TOK_TILE = 512
PROJ_TILE = 512
PROJ_SUB = 128
POST_TILE = 1024
POST_SUB = 256
GDN_CHUNK = 128
GDN_SEQ_PER_STEP = 4
ATTN_BLOCKS_PER_STEP = 4
ATTN_SEQS_PER_STEP = 16
INV_BASE = 16
SAMPLE_SLOTS = 8
SAMPLE_FIRST = CONV_W - 1
ROW_TILE = 512
EXPERT_SUB = 256
EXPERT_IN_SLOTS = 4
EXPERT_OUT_SLOTS = 3
SLAB_UNROLL = 4
COMBINE_SUB = 128
SLAB = 16
PERM_ROWS = TOP_K * TOK_TILE + N_EXPERTS * SLAB
PERM_SLABS = PERM_ROWS // SLAB
XS_WORDS = D_MODEL + LANES
ZERO_TABLE = N_EXPERTS * (ROW_TILE // SLAB)
VMEM_LIMIT = 48 * 1024 * 1024
NEG_BIG = -1e30


def _cparams(sem):
    return pltpu.CompilerParams(dimension_semantics=sem, vmem_limit_bytes=VMEM_LIMIT)


def _bdot(a, b):
    return jnp.dot(a.astype(BF16), b.astype(BF16), preferred_element_type=F32)


def _bdot_nt(a, b):
    return lax.dot_general(a.astype(BF16), b.astype(BF16), (((1,), (1,)), ((), ())),
                           preferred_element_type=F32)


def _bdot_tn(a, b):
    return lax.dot_general(a.astype(BF16), b.astype(BF16), (((0,), (0,)), ((), ())),
                           preferred_element_type=F32)


def _div_pow2(x, n):
    return jnp.right_shift(x, int(math.log2(n)))


def _mod_pow2(x, n):
    return jnp.bitwise_and(x, n - 1)


def _split3(x):
    hi = x.astype(BF16)
    r = x - hi.astype(F32)
    mid = r.astype(BF16)
    lo = (r - mid.astype(F32)).astype(BF16)
    return hi, mid, lo


def _dot_exact_lhs01(m01, x):
    hi, mid, lo = _split3(x)
    d = lambda t: jnp.dot(m01, t, preferred_element_type=F32)
    return d(hi) + d(mid) + d(lo)


def _sigmoid(x):
    return 1.0 / (1.0 + jnp.exp(-x))


def _silu(x):
    return x * _sigmoid(x)


def _softplus(x):
    return jnp.maximum(x, 0.0) + jnp.log1p(jnp.exp(-jnp.abs(x)))


def _layer_norm(h, g, b):
    mu = jnp.mean(h, axis=-1, keepdims=True)
    d = h - mu
    var = jnp.mean(d * d, axis=-1, keepdims=True)
    return d * lax.rsqrt(var + NORM_EPS) * g + b


def _proj_kernel(*refs, tm, has_hist, full_u, one_segment):
    it = iter(refs)
    x_ref, cos_ref, sin_ref = next(it), next(it), next(it)
    w_ref = next(it)
    convw_ref, alog_ref, dtb_ref, tri_ref, seg_ref = next(it), next(it), next(it), next(it), next(it)
    hist_ref = valid_ref = None
    if has_hist:
        hist_ref, valid_ref = next(it), next(it)
    q_ref, k_ref, v_ref = next(it), next(it), next(it)
    qg_ref, kg_ref, vg_ref, z_ref, gcb_ref, u_ref = (next(it) for _ in range(6))
    ubuf = next(it)

    t = pl.program_id(1)
    sub = PROJ_SUB
    rows = [slice(j * sub, (j + 1) * sub) for j in range(tm // sub)]
    lane = lax.broadcasted_iota(I32, (sub, LANES), 1)
    first_half = _mod_pow2(lane, HEAD_DIM) < (ROT_DIM // 2)

    @pl.when(t == 0)
    def _():
        ubuf[0:SUBLANES, :] = jnp.zeros((SUBLANES, CONV_DIM), F32)

    @pl.when(t > 0)
    def _():
        ubuf[0:SUBLANES, :] = ubuf[tm:tm + SUBLANES, :]

    dots = []
    for r in rows:
        xb = x_ref[r, :].astype(BF16)
        dots.append([jnp.dot(xb, w_ref[:, lo:hi], preferred_element_type=F32)
                     for lo, hi in zip(IN_SPLITS[:-1], IN_SPLITS[1:])])

    def l2n(s):
        return s * lax.rsqrt(jnp.sum(s * s, axis=1, keepdims=True) + L2_EPS)

    for r, (pq, u, z, ab) in zip(rows, dots):
        cosv, sinv = cos_ref[r, :], sin_ref[r, :]

        def rope(s):
            sw = jnp.where(first_half, pltpu.roll(s, LANES - ROT_DIM // 2, axis=1),
                           pltpu.roll(s, ROT_DIM // 2, axis=1))
            return s * cosv + sw * sinv

        for j in range(Q_COLS // LANES):
            q_ref[r, j * LANES:(j + 1) * LANES] = rope(pq[:, j * LANES:(j + 1) * LANES])
        k_ref[r, :] = rope(pq[:, Q_COLS:Q_COLS + KV_COLS])
        v_ref[r, :] = pq[:, Q_COLS + KV_COLS:Q_COLS + 2 * KV_COLS]
        z_ref[r, :] = z

        if has_hist:
            u = u + hist_ref[r, :]
        if full_u:
            u_ref[r, :] = u
        elif r.stop == tm:
            u_ref[...] = u[sub - SUBLANES:, :]
        base = SUBLANES + r.start
        ubuf[base:base + sub, :] = u
        acc = u * convw_ref[CONV_W - 1:CONV_W, :]
        for j in range(1, CONV_W):
            acc = acc + ubuf[base - j:base - j + sub, :] * convw_ref[CONV_W - 1 - j:CONV_W - j, :]
        c = _silu(acc)
        if has_hist:
            c = c * valid_ref[r, :]
        for h in range(GDN_HEADS):
            sl = slice(h * GDN_DK, (h + 1) * GDN_DK)
            qg_ref[r, sl] = l2n(c[:, sl]) * (GDN_DK ** -0.5)
            kg_ref[r, sl] = l2n(c[:, QK_COLS + h * GDN_DK:QK_COLS + (h + 1) * GDN_DK])
        vg_ref[r, :] = c[:, 2 * QK_COLS:]

        g = -jnp.exp(alog_ref[...]) * _softplus(ab + dtb_ref[...])
        beta = _sigmoid(ab)
        if has_hist:
            g = g * valid_ref[r, :]
            beta = beta * valid_ref[r, :]
        g = jnp.where(lane < GDN_HEADS, g, 0.0)
        gc = _dot_exact_lhs01(tri_ref[...], g)
        if one_segment:
            gl = jnp.broadcast_to(gc[sub - 1:sub, :], (sub, LANES))
        else:
            gl = _dot_exact_lhs01(seg_ref[...], g)
        gcb_ref[r, :] = jnp.where(lane < GDN_HEADS, gc,
                                  jnp.where(lane < 2 * GDN_HEADS, beta,
                                            jnp.where(lane < 3 * GDN_HEADS,
                                                      pltpu.roll(gl, 2 * GDN_HEADS, axis=1), 0.0)))


def _rope_tables(pos):
    half = ROT_DIM // 2
    pos = np.asarray(pos, np.float64)
    inv_freq = ROPE_THETA ** (-np.arange(half, dtype=np.float64) * 2.0 / ROT_DIM)
    ang = pos[:, None] * inv_freq[None, :]
    cos, sin = np.cos(ang), np.sin(ang)
    p = pos.shape[0]
    cpat = np.concatenate([cos, cos, np.ones((p, HEAD_DIM - ROT_DIM))], axis=1)
    spat = np.concatenate([-sin, sin, np.zeros((p, HEAD_DIM - ROT_DIM))], axis=1)
    rep = (1, LANES // HEAD_DIM)
    return jnp.asarray(np.tile(cpat, rep), F32), jnp.asarray(np.tile(spat, rep), F32)


def _segment_matrices(tm, seg_len):
    i = np.arange(tm)
    same = (i[:, None] // seg_len) == (i[None, :] // seg_len)
    tri = same & (i[None, :] <= i[:, None])
    return jnp.asarray(tri, BF16), jnp.asarray(same, BF16)


def _proj(x, pos, wts, seg_len, n_seq, hist=None, valid=None):
    n = x.shape[0]
    rows = n // n_seq
    tm = min(PROJ_TILE, rows)
    nt = rows // tm
    has_hist = hist is not None
    cos_t, sin_t = _rope_tables(pos)
    tri, seg = _segment_matrices(PROJ_SUB, seg_len)

    tok = lambda w: pl.BlockSpec((tm, w), lambda b, t: (b * nt + t, 0))
    const = lambda a: pl.BlockSpec(a.shape, lambda b, t: (0,) * a.ndim)
    in_arrays = [x, cos_t, sin_t, wts['w_all'], wts['convw'], wts['alog'], wts['dtb'], tri, seg]
    in_specs = [tok(D_MODEL), pl.BlockSpec((tm, LANES), lambda b, t: (t, 0)),
                pl.BlockSpec((tm, LANES), lambda b, t: (t, 0))] + [const(a) for a in in_arrays[3:]]
    if has_hist:
        in_arrays += [hist, valid]
        in_specs += [tok(CONV_DIM), tok(1)]
    u_rows = n if has_hist else (n // tm) * SUBLANES
    u_block = tm if has_hist else SUBLANES
    out_shape = [jax.ShapeDtypeStruct((n, Q_COLS), F32), jax.ShapeDtypeStruct((n, KV_COLS), F32),
                 jax.ShapeDtypeStruct((n, KV_COLS), F32), jax.ShapeDtypeStruct((n, QK_COLS), F32),
                 jax.ShapeDtypeStruct((n, QK_COLS), F32), jax.ShapeDtypeStruct((n, Z_COLS), F32),
                 jax.ShapeDtypeStruct((n, Z_COLS), F32), jax.ShapeDtypeStruct((n, LANES), F32),
                 jax.ShapeDtypeStruct((u_rows, CONV_DIM), F32)]
    out_specs = [tok(Q_COLS), tok(KV_COLS), tok(KV_COLS), tok(QK_COLS), tok(QK_COLS), tok(Z_COLS),
                 tok(Z_COLS), tok(LANES),
                 pl.BlockSpec((u_block, CONV_DIM), lambda b, t: (b * nt + t, 0))]
    return pl.pallas_call(
        functools.partial(_proj_kernel, tm=tm, has_hist=has_hist, full_u=has_hist,
                          one_segment=seg_len == PROJ_SUB),
        out_shape=out_shape, grid=(n_seq, nt), in_specs=in_specs, out_specs=out_specs,
        scratch_shapes=[pltpu.VMEM((tm + SUBLANES, CONV_DIM), F32)],
        compiler_params=_cparams(("arbitrary", "arbitrary")),
        name="proj_hist" if has_hist else "proj",
    )(*in_arrays)


def _attn_blocks(qs, kcats, vcats, biases, sink, tq):
    lane = lax.broadcasted_iota(I32, (tq, LANES), 1)
    low = lane < HEAD_DIM
    n_slab = Q_COLS // LANES

    def stack(q):
        slabs = [q[:, j * LANES:(j + 1) * LANES] * (HEAD_DIM ** -0.5) for j in range(n_slab)]
        parts = ([jnp.where(low, s, 0.0) for s in slabs] + [jnp.where(low, 0.0, s) for s in slabs])
        return jnp.concatenate(parts, axis=0).astype(BF16)

    def unstack(o8):
        return [jnp.where(low, o8[j * tq:(j + 1) * tq, :], o8[(n_slab + j) * tq:(n_slab + j + 1) * tq, :])
                for j in range(n_slab)]

    rows = ATTN_HEADS * tq
    half = rows // 2
    klow = lax.broadcasted_iota(I32, (2 * WINDOW, LANES), 1) < HEAD_DIM
    one = jnp.ones((), BF16)
    q8s = _each(stack, qs)
    ss = _each(lambda q8, kc, b: _bdot_nt(q8, kc) + b, q8s, kcats, biases)
    ms = _each(lambda s: jnp.maximum(jnp.broadcast_to(jnp.max(s, axis=1, keepdims=True), (rows, LANES)),
                                     sink), ss)
    ps = _each(lambda s, m: jnp.exp(s - jnp.concatenate([m, m], axis=1)).astype(BF16), ss, ms)
    pv0 = _each(lambda p, vc: jnp.dot(p[:half], jnp.where(klow, vc, one), preferred_element_type=F32),
                ps, vcats)
    pv1 = _each(lambda p, vc: jnp.dot(p[half:], jnp.where(klow, one, vc), preferred_element_type=F32),
                ps, vcats)
    pvs = _each(lambda a, b: jnp.concatenate([a, b], axis=0), pv0, pv1)
    o8s = _each(lambda pv, m: pv / (pltpu.roll(pv, HEAD_DIM, axis=1) + jnp.exp(sink - m)), pvs, ms)
    return _each(unstack, o8s)


def _attn_prompt_kernel(q_ref, kc_ref, vc_ref, kp_ref, vp_ref, bias0_ref, bias_ref, sink_ref, o_ref, *,
                        nblk):
    kall = jnp.concatenate([kp_ref[...], kc_ref[...]], axis=0).astype(BF16)
    vall = jnp.concatenate([vp_ref[...], vc_ref[...]], axis=0).astype(BF16)
    win = lambda a, j: a[j * WINDOW:(j + 2) * WINDOW, :]
    qs = [q_ref[j * WINDOW:(j + 1) * WINDOW, :] for j in range(nblk)]
    biases = [bias0_ref[0]] + [bias_ref[...]] * (nblk - 1)
    outs = _attn_blocks(qs, [win(kall, j) for j in range(nblk)], [win(vall, j) for j in range(nblk)],
                        biases, sink_ref[...], WINDOW)
    for j, slabs in enumerate(outs):
        for c, slab in enumerate(slabs):
            o_ref[j * WINDOW:(j + 1) * WINDOW, c * LANES:(c + 1) * LANES] = slab


def _attn_sample_kernel(q_ref, kc_ref, vc_ref, kp_ref, vp_ref, bias_ref, sink_ref, o_ref, kw_ref, vw_ref,
                        *, nseq, n_new):
    tq = SAMPLE_SLOTS
    zpad = jnp.zeros((WINDOW - tq, LANES), F32)
    rows = lambda ref, j: ref[j * tq:(j + 1) * tq, :]
    cat = lambda pref, cref, j: jnp.concatenate([pref[j], rows(cref, j), zpad], axis=0).astype(BF16)
    outs = _attn_blocks([rows(q_ref, j) for j in range(nseq)],
                        [cat(kp_ref, kc_ref, j) for j in range(nseq)],
                        [cat(vp_ref, vc_ref, j) for j in range(nseq)],
                        [bias_ref[...]] * nseq, sink_ref[...], tq)
    for j, slabs in enumerate(outs):
        for c, slab in enumerate(slabs):
            o_ref[j * tq:(j + 1) * tq, c * LANES:(c + 1) * LANES] = slab
    row = lax.broadcasted_iota(I32, (WINDOW, LANES), 0)
    keep = WINDOW - n_new
    for pref, cref, wref in ((kp_ref, kc_ref, kw_ref), (vp_ref, vc_ref, vw_ref)):
        for j in range(nseq):
            new = jnp.concatenate([rows(cref, j), zpad], axis=0)
            wref[j] = jnp.where(row < keep, pltpu.roll(pref[j], keep, axis=0),
                                pltpu.roll(new, keep - SAMPLE_FIRST, axis=0))


def _sink_rows(sinks, tq):
    return jnp.broadcast_to(jnp.repeat(sinks.astype(F32), tq)[:, None], (ATTN_HEADS * tq, LANES))


def _attn_bias(tq, q_off, k_lo, k_hi, has_prev):
    qi = (np.arange(ATTN_HEADS * tq) % tq)[:, None]
    c = np.arange(2 * WINDOW)[None, :]
    cj = c - WINDOW
    vis_prev = (c < WINDOW) & (c > qi - q_off) & has_prev
    vis_cur = (c >= WINDOW) & (cj <= qi) & (cj >= k_lo) & (cj <= k_hi)
    return np.where(vis_prev | vis_cur, 0.0, NEG_BIG).astype(np.float32)


def _attn_prompt(q, k, v, sinks, n_seq):
    n = q.shape[0]
    nb = n // n_seq // WINDOW
    nblk = min(ATTN_BLOCKS_PER_STEP, nb)
    steps = nb // nblk
    tq = nblk * WINDOW
    cur = lambda w: pl.BlockSpec((tq, w), lambda b, i: (b * steps + i, 0))
    prev = pl.BlockSpec((WINDOW, LANES), lambda b, i: (b * nb + jnp.maximum(i * nblk - 1, 0), 0))
    bias2 = jnp.asarray(np.stack([_attn_bias(WINDOW, 0, 0, WINDOW - 1, False),
                                  _attn_bias(WINDOW, 0, 0, WINDOW - 1, True)]))
    rows = ATTN_HEADS * WINDOW
    return pl.pallas_call(
        functools.partial(_attn_prompt_kernel, nblk=nblk),
        out_shape=jax.ShapeDtypeStruct((n, Q_COLS), F32), grid=(n_seq, steps),
        in_specs=[cur(Q_COLS), cur(LANES), cur(LANES), prev, prev,
                  pl.BlockSpec((1, rows, 2 * WINDOW), lambda b, i: (jnp.minimum(i, 1), 0, 0)),
                  pl.BlockSpec((rows, 2 * WINDOW), lambda b, i: (0, 0)),
                  pl.BlockSpec((rows, LANES), lambda b, i: (0, 0))],
        out_specs=cur(Q_COLS),
        compiler_params=_cparams(("arbitrary", "arbitrary")), name="attn_prompt",
    )(q, k, v, k, v, bias2, bias2[1], _sink_rows(sinks, WINDOW))


def _attn_sample(q, k, v, cache_k, cache_v, sinks, n_seq, n_new):
    tq = SAMPLE_SLOTS
    nseq = min(ATTN_SEQS_PER_STEP, n_seq)
    cur = lambda w: pl.BlockSpec((nseq * tq, w), lambda b: (b, 0))
    prev = pl.BlockSpec((nseq, WINDOW, LANES), lambda b: (b, 0, 0))
    bias = jnp.asarray(_attn_bias(tq, SAMPLE_FIRST, SAMPLE_FIRST, SAMPLE_FIRST + 3, True))
    win = jax.ShapeDtypeStruct((n_seq, WINDOW, LANES), F32)
    return pl.pallas_call(
        functools.partial(_attn_sample_kernel, nseq=nseq, n_new=n_new),
        out_shape=[jax.ShapeDtypeStruct((n_seq * tq, Q_COLS), F32), win, win], grid=(n_seq // nseq,),
        in_specs=[cur(Q_COLS), cur(LANES), cur(LANES), prev, prev,
                  pl.BlockSpec(bias.shape, lambda b: (0, 0)),
                  pl.BlockSpec((ATTN_HEADS * tq, LANES), lambda b: (0, 0))],
        out_specs=[cur(Q_COLS), prev, prev],
        compiler_params=_cparams(("arbitrary",)), name="attn_sample",
    )(q, k, v, cache_k, cache_v, bias, _sink_rows(sinks, tq))


def _each(f, *lists):
    return [f(*args) for args in zip(*lists)]


def _unit_lower_inverse(ms, eye, same_base, base_only=False):
    c = ms[0].shape[0]

    def neumann(q0s, n_factors):
        xs = _each(lambda q: eye + q, q0s)
        if n_factors == 1:
            return xs
        qs = _each(_bdot, q0s, q0s)
        for _ in range(n_factors - 2):
            prods = _each(lambda x, q: _bdot(jnp.concatenate([x, q], axis=0), q), xs, qs)
            xs = _each(lambda x, pr: x + pr[:c], xs, prods)
            qs = _each(lambda pr: pr[c:], prods)
        return _each(lambda x, q: x + _bdot(x, q), xs, qs)

    ds = _each(lambda m: jnp.where(same_base, m, 0.0), ms)
    xs = neumann(_each(lambda d: -d, ds), int(math.log2(INV_BASE)))
    nblk = c // INV_BASE
    if nblk == 1 or base_only:
        return xs
    ls = _each(lambda m, d: m - d, ms, ds)
    ns = _each(lambda x, l: -_bdot(x, l), xs, ls)
    ys = neumann(ns, int(math.log2(nblk)))
    return _each(_bdot, ys, xs)


def _gdn_intra(qs, ks, vs, gcs, gls, betas, same_seq, low_incl, low_strict, eye, same_base,
               base_only=False):
    del same_seq
    e_gcs = _each(jnp.exp, gcs)

    def decay_of(gc):
        gc_row = jnp.sum(jnp.where(eye > 0, gc, 0.0), axis=0, keepdims=True)
        return jnp.where(low_incl, jnp.exp(jnp.where(low_incl, gc - gc_row, 0.0)), 0.0)

    c = qs[0].shape[0]
    decays = _each(decay_of, gcs)
    kbs = _each(lambda k, b: k * b, ks, betas)
    vbs = _each(lambda v, b: v * b, vs, betas)
    kqs = _each(lambda kb, q, k: _bdot_nt(jnp.concatenate([kb, q], axis=0), k), kbs, qs, ks)
    ms = _each(lambda kq, d: jnp.where(low_strict, kq[:c] * d, 0.0), kqs, decays)
    attns = _each(lambda kq, d: kq[c:] * d, kqs, decays)
    tmats = _unit_lower_inverse(ms, eye, same_base, base_only)
    uws = _each(lambda t, vb, kb, e: _bdot(t, jnp.concatenate([vb, kb * e], axis=1)),
                tmats, vbs, kbs, e_gcs)
    us = _each(lambda uw: uw[:, :GDN_DV], uws)
    ws = _each(lambda uw: uw[:, GDN_DV:], uws)
    q_decs = _each(lambda q, e: q * e, qs, e_gcs)
    k_decs = _each(lambda k, gl, gc: k * jnp.exp(gl - gc), ks, gls, gcs)
    return us, ws, attns, q_decs, k_decs


def _chunk_masks(c, seq_len):
    i = lax.broadcasted_iota(I32, (c, c), 0)
    j = lax.broadcasted_iota(I32, (c, c), 1)
    same_seq = _div_pow2(i, seq_len) == _div_pow2(j, seq_len)
    low_incl = same_seq & (i >= j)
    low_strict = same_seq & (i > j)
    eye = (i == j).astype(F32)
    same_base = _div_pow2(i, INV_BASE) == _div_pow2(j, INV_BASE)
    return same_seq, low_incl, low_strict, eye, same_base


def _gated_rms(o, z, nw):
    o = o * lax.rsqrt(jnp.mean(o * o, axis=1, keepdims=True) + NORM_EPS) * nw
    return o * _silu(z)


def _gdn_prompt_kernel(qg_ref, kg_ref, vg_ref, z_ref, gcb_ref, nw_ref, o_ref, s_out_ref, s_scr):
    c = GDN_CHUNK
    n = pl.program_id(1)

    @pl.when(n == 0)
    def _():
        s_scr[...] = jnp.zeros_like(s_scr)

    masks = _chunk_masks(c, c)
    nw = nw_ref[...]
    chains = [(b, h) for b in range(qg_ref.shape[0]) for h in range(GDN_HEADS)]
    hs = lambda h: slice(h * GDN_DK, (h + 1) * GDN_DK)
    col = lambda off: [gcb_ref[b, :, off + h:off + h + 1] for b, h in chains]
    gcs, betas, gls = col(0), col(GDN_HEADS), col(2 * GDN_HEADS)
    qs = [qg_ref[b, :, hs(h)] for b, h in chains]
    ks = [kg_ref[b, :, hs(h)] for b, h in chains]
    vs = [vg_ref[b, :, hs(h)] for b, h in chains]
    us, ws, attns, q_decs, k_decs = _gdn_intra(qs, ks, vs, gcs, gls, betas, *masks)
    ss = [s_scr[b, h] for b, h in chains]
    wqs = _each(lambda w, qd, s: _bdot(jnp.concatenate([w, qd], axis=0), s), ws, q_decs, ss)
    wss = _each(lambda wq: wq[:c], wqs)
    qss = _each(lambda wq: wq[c:], wqs)
    v_news = _each(lambda u, x: u - x, us, wss)
    avs = _each(_bdot, attns, v_news)
    kvs = _each(_bdot_tn, k_decs, v_news)
    for (b, h), s, gl, qsv, av, kv in zip(chains, ss, gls, qss, avs, kvs):
        s_scr[b, h] = s * jnp.exp(gl[0:1, :]) + kv
        o_ref[b, :, hs(h)] = _gated_rms(qsv + av, z_ref[b, :, hs(h)], nw)

    @pl.when(n == pl.num_programs(1) - 1)
    def _():
        s_out_ref[...] = s_scr[...]


def _gdn_prompt(qg, kg, vg, z, gcb, norm_w, n_seq):
    n = qg.shape[0]
    s_len = n // n_seq
    nb = min(GDN_SEQ_PER_STEP, n_seq)
    v3 = lambda a: a.reshape(n_seq, s_len, a.shape[-1])
    tok = lambda w: pl.BlockSpec((nb, GDN_CHUNK, w), lambda b, i: (b, i, 0))
    o, s = pl.pallas_call(
        _gdn_prompt_kernel,
        out_shape=[jax.ShapeDtypeStruct((n_seq, s_len, Z_COLS), F32),
                   jax.ShapeDtypeStruct((n_seq, GDN_HEADS, GDN_DK, GDN_DV), F32)],
        grid=(n_seq // nb, s_len // GDN_CHUNK),
        in_specs=[tok(QK_COLS), tok(QK_COLS), tok(Z_COLS), tok(Z_COLS), tok(LANES),
                  pl.BlockSpec((1, GDN_DV), lambda b, i: (0, 0))],
        out_specs=[tok(Z_COLS),
                   pl.BlockSpec((nb, GDN_HEADS, GDN_DK, GDN_DV), lambda b, i: (b, 0, 0, 0))],
        scratch_shapes=[pltpu.VMEM((nb, GDN_HEADS, GDN_DK, GDN_DV), F32)],
        compiler_params=_cparams(("arbitrary", "arbitrary")), name="gdn_prompt",
    )(v3(qg), v3(kg), v3(vg), v3(z), v3(gcb), norm_w)
    return o.reshape(n, Z_COLS), s


def _gdn_sample_kernel(qg_ref, kg_ref, vg_ref, z_ref, gcb_ref, nw_ref, s_in_ref, o_ref, s_out_ref):
    c = GDN_CHUNK
    n_sub = c // SAMPLE_SLOTS
    masks = _chunk_masks(c, SAMPLE_SLOTS)
    heads = range(GDN_HEADS)
    hs = lambda h: slice(h * GDN_DK, (h + 1) * GDN_DK)
    rs = lambda s: slice(s * SAMPLE_SLOTS, (s + 1) * SAMPLE_SLOTS)
    col = lambda off: [gcb_ref[:, off + h:off + h + 1] for h in heads]
    gcs, betas, gls = col(0), col(GDN_HEADS), col(2 * GDN_HEADS)
    us, ws, attns, q_decs, k_decs = _gdn_intra([qg_ref[:, hs(h)] for h in heads],
                                               [kg_ref[:, hs(h)] for h in heads],
                                               [vg_ref[:, hs(h)] for h in heads], gcs, gls, betas, *masks,
                                               base_only=SAMPLE_SLOTS <= INV_BASE)
    pairs = [(h, s) for h in heads for s in range(n_sub)]
    sts = [s_in_ref[s, h] for h, s in pairs]
    boths = [jnp.concatenate([ws[h][rs(s), :], q_decs[h][rs(s), :]], axis=0) for h, s in pairs]
    rr = _each(_bdot, boths, sts)
    gather = lambda h, part: jnp.concatenate(
        [rr[h * n_sub + s][part * SAMPLE_SLOTS:(part + 1) * SAMPLE_SLOTS, :] for s in range(n_sub)], axis=0)
    v_news = [us[h] - gather(h, 0) for h in heads]
    avs = _each(_bdot, attns, v_news)
    row = lax.broadcasted_iota(I32, (c, LANES), 0)
    seq_of_row = _div_pow2(row, SAMPLE_SLOTS)
    kds = [jnp.where(seq_of_row == s, k_decs[h], 0.0) for h, s in pairs]
    kvs = _each(_bdot_tn, kds, [v_news[h] for h, _ in pairs])
    egls = _each(jnp.exp, gls)
    for (h, s), st, kv in zip(pairs, sts, kvs):
        s_out_ref[s, h] = st * egls[h][s * SAMPLE_SLOTS:s * SAMPLE_SLOTS + 1, :] + kv
    nw = nw_ref[...]
    for h in heads:
        o_ref[:, hs(h)] = _gated_rms(gather(h, 1) + avs[h], z_ref[:, hs(h)], nw)


def _gdn_sample(qg, kg, vg, z, gcb, norm_w, state):
    n = qg.shape[0]
    n_sub = GDN_CHUNK // SAMPLE_SLOTS
    tok = lambda w: pl.BlockSpec((GDN_CHUNK, w), lambda i: (i, 0))
    st = pl.BlockSpec((n_sub, GDN_HEADS, GDN_DK, GDN_DV), lambda i: (i, 0, 0, 0))
    return pl.pallas_call(
        _gdn_sample_kernel,
        out_shape=[jax.ShapeDtypeStruct((n, Z_COLS), F32),
                   jax.ShapeDtypeStruct(state.shape, F32)],
        grid=(n // GDN_CHUNK,),
        in_specs=[tok(QK_COLS), tok(QK_COLS), tok(Z_COLS), tok(Z_COLS), tok(LANES),
                  pl.BlockSpec((1, GDN_DV), lambda i: (0, 0)), st],
        out_specs=[tok(Z_COLS), st],
        compiler_params=_cparams(("arbitrary",)), name="gdn_sample",
    )(qg, kg, vg, z, gcb, norm_w, state)


def _post_kernel(a_ref, g_ref, x_ref, wo_ref, ln_g_ref, ln_b_ref, wr_ref, x1_ref, route_ref, *, tm):
    sub = POST_SUB
    rows = [slice(j * sub, (j + 1) * sub) for j in range(tm // sub)]
    d = lambda a, b: jnp.dot(a, b, preferred_element_type=F32)
    mixes = [d(a_ref[r, :].astype(BF16), wo_ref[0:Q_COLS, :]) + d(g_ref[r, :].astype(BF16), wo_ref[Q_COLS:, :])
             for r in rows]
    x1s = [_layer_norm(DEEPNORM_ALPHA * x_ref[r, :] + mix, ln_g_ref[...], ln_b_ref[...])
           for r, mix in zip(rows, mixes)]
    for r, x1 in zip(rows, x1s):
        x1_ref[r, :] = x1
    w2 = wr_ref[...]
    lgs = []
    for x1 in x1s:
        xh = x1.astype(BF16)
        xm = (x1 - xh.astype(F32)).astype(BF16)
        both = d(xh, w2)
        lgs.append(both[:, :LANES] + both[:, LANES:] + d(xm, w2[:, :LANES]))
    for r, lg in zip(rows, lgs):
        route_ref[:, r] = jnp.transpose(_route(lg))[0:SUBLANES, :]


def _route(lg):
    lane = lax.broadcasted_iota(I32, lg.shape, 1)
    lane_f = lane.astype(F32)
    big = float(LANES)

    def first_max(vals, mask):
        v = jnp.where(mask, vals, NEG_BIG)
        mx = jnp.max(v, axis=1, keepdims=True)
        idx = jnp.min(jnp.where(mask & (v == mx), lane_f, big), axis=1, keepdims=True)
        return mx, idx

    gmask = lane < N_GROUPS
    gmax, gidx = first_max(lg, gmask)
    gden = jnp.sum(jnp.where(gmask, jnp.exp(lg - gmax), 0.0), axis=1, keepdims=True)
    g_top_p = 1.0 / gden
    e_lane = lane - N_GROUPS
    e_group = _div_pow2(jnp.maximum(e_lane, 0), EXPERTS_PER_GROUP).astype(F32)
    emask = (e_lane >= 0) & (e_lane < N_EXPERTS) & (e_group == gidx)
    m1, i1 = first_max(lg, emask)
    eden = jnp.sum(jnp.where(emask, jnp.exp(lg - m1), 0.0), axis=1, keepdims=True)
    m2, i2 = first_max(lg, emask & (lane_f != i1))
    p1 = 1.0 / eden
    p2 = jnp.exp(m2 - m1) / eden
    tot = p1 + p2
    gate1 = g_top_p * (p1 / tot)
    gate2 = g_top_p * (p2 / tot)
    return jnp.where(lane == 0, gate1,
                     jnp.where(lane == 1, gate2,
                               jnp.where(lane == 2, i1 - N_GROUPS,
                                         jnp.where(lane == 3, i2 - N_GROUPS, 0.0))))


def _post(attn_o, gdn_o, x, wts):
    n = x.shape[0]
    tm = min(POST_TILE, n)
    tok = lambda w: pl.BlockSpec((tm, w), lambda i: (i, 0))
    const = lambda a: pl.BlockSpec(a.shape, lambda i: (0,) * a.ndim)
    consts = [wts['wo'], wts['ln1_g'], wts['ln1_b'], wts['wr']]
    return pl.pallas_call(
        functools.partial(_post_kernel, tm=tm),
        out_shape=[jax.ShapeDtypeStruct((n, D_MODEL), F32), jax.ShapeDtypeStruct((SUBLANES, n), F32)],
        grid=(n // tm,),
        in_specs=[tok(Q_COLS), tok(Z_COLS), tok(D_MODEL)] + [const(a) for a in consts],
        out_specs=[tok(D_MODEL), pl.BlockSpec((SUBLANES, tm), lambda i: (0, i))],
        compiler_params=_cparams(("arbitrary",)), name="post_%d" % (n // tm),
    )(attn_o, gdn_o, x, *consts)


def _slab_loop(n, body):
    n_main = jnp.right_shift(n, int(math.log2(SLAB_UNROLL)))

    def main(i, c):
        for u in range(SLAB_UNROLL):
            body(i * SLAB_UNROLL + u, u)
        return c

    lax.fori_loop(0, n_main, main, 0)
    lax.fori_loop(n_main * SLAB_UNROLL, n, lambda j, c: (body(j, 0), c)[1], 0)


def _dispatch_kernel(dst_ref, nslab_ref, ztab_ref, zinfo_ref, slot_ref, gate_ref, *rest,
                     group_tiles, max_tiles):
    x_refs = rest[:len(group_tiles)]
    xs_ref, pbuf, sem, zbuf, zsem = rest[len(group_tiles):]
    n_tiles = sum(group_tiles)
    g = pl.program_id(0)
    cur = lax.rem(g, 2)

    def slab_copy(tile, buf_slot, j):
        d = pl.multiple_of(dst_ref[tile * PERM_SLABS + j], SLAB)
        src = pbuf.at[buf_slot, pl.ds(pl.multiple_of(j * SLAB, SLAB), SLAB), :]
        return pltpu.make_async_copy(src, xs_ref.at[pl.ds(d, SLAB), :], sem.at[buf_slot])

    def tail_copy(k):
        d = pl.multiple_of(ztab_ref[k], SLAB)
        return pltpu.make_async_copy(zbuf.at[pl.ds(0, SLAB), :], xs_ref.at[pl.ds(d, SLAB), :], zsem)

    def tile_copy(t):
        d = pl.multiple_of(t * ROW_TILE, ROW_TILE)
        return pltpu.make_async_copy(zbuf, xs_ref.at[pl.ds(d, ROW_TILE), :], zsem)

    @pl.when(g == 0)
    def _():
        zbuf[...] = jnp.zeros_like(zbuf)

    share = -(-ZERO_TABLE // n_tiles)
    k0 = g * share
    _slab_loop(jnp.clip(zinfo_ref[0] - k0, 0, share), lambda j, u: tail_copy(k0 + j).start(priority=1))

    @pl.when(zinfo_ref[1] + g < max_tiles)
    def _():
        tile_copy(zinfo_ref[1] + g).start(priority=1)

    x = x_refs[-1][...]
    bound = n_tiles
    for x_ref, nt in zip(x_refs[-2::-1], group_tiles[:0:-1]):
        bound -= nt
        x = jnp.where(g < bound, x_ref[...], x)

    r = lax.broadcasted_iota(I32, (PERM_ROWS, TOK_TILE), 0)
    sl = slot_ref[0]
    hit0, hit1 = r == sl[0:1, :], r == sl[1:2, :]
    onehot = jnp.where(hit0 | hit1, 1.0, 0.0).astype(BF16)
    gt = gate_ref[0]
    gcol = jnp.sum(jnp.where(hit0, gt[0:1, :], 0.0) + jnp.where(hit1, gt[1:2, :], 0.0),
                   axis=1, keepdims=True)
    pbuf[cur, :, 0:D_MODEL] = jnp.dot(onehot, x.astype(BF16), preferred_element_type=F32).astype(BF16)
    g_hi = gcol.astype(BF16).astype(F32)
    lane = lax.broadcasted_iota(I32, (PERM_ROWS, LANES), 1)
    pbuf[cur, :, D_MODEL:] = jnp.where(lane < LANES // 2, g_hi, gcol - g_hi).astype(BF16)

    @pl.when(g > 0)
    def _():
        _slab_loop(nslab_ref[g - 1], lambda j, u: slab_copy(g - 1, 1 - cur, j).wait())

    _slab_loop(nslab_ref[g], lambda j, u: slab_copy(g, cur, j).start(priority=u % 2))

    @pl.when(g == n_tiles - 1)
    def _():
        _slab_loop(nslab_ref[g], lambda j, u: slab_copy(g, cur, j).wait())
        lax.fori_loop(zinfo_ref[1] + n_tiles, max_tiles,
                      lambda t, c: (tile_copy(t).start(priority=1), c)[1], 0)
        _slab_loop(zinfo_ref[0], lambda k, u: tail_copy(k).wait())
        lax.fori_loop(zinfo_ref[1], max_tiles, lambda t, c: (tile_copy(t).wait(), c)[1], 0)


def _dispatch(plan, x1s, max_tiles):
    group_tiles = tuple(x1.shape[0] // TOK_TILE for x1 in x1s)
    n_tiles = sum(group_tiles)
    tile = lambda i, d, ns, zt, zi: (i, 0, 0)
    in_specs = [pl.BlockSpec((1, TOP_K, TOK_TILE), tile), pl.BlockSpec((1, TOP_K, TOK_TILE), tile)]
    base = 0
    for nt in group_tiles:
        in_specs.append(pl.BlockSpec(
            (TOK_TILE, D_MODEL),
            lambda i, d, ns, zt, zi, base=base, nt=nt: (jnp.clip(i - base, 0, nt - 1), 0)))
        base += nt
    return pl.pallas_call(
        functools.partial(_dispatch_kernel, group_tiles=group_tiles, max_tiles=max_tiles),
        out_shape=jax.ShapeDtypeStruct((max_tiles * ROW_TILE, XS_WORDS), BF16),
        grid_spec=pltpu.PrefetchScalarGridSpec(
            num_scalar_prefetch=4, grid=(n_tiles,), in_specs=in_specs,
            out_specs=pl.BlockSpec(memory_space=pl.ANY),
            scratch_shapes=[pltpu.VMEM((2, PERM_ROWS, XS_WORDS), BF16), pltpu.SemaphoreType.DMA((2,)),
                            pltpu.VMEM((ROW_TILE, XS_WORDS), BF16), pltpu.SemaphoreType.DMA(())]),
        compiler_params=_cparams(("arbitrary",)), name="moe_dispatch",
    )(plan['slab_dst'], plan['nslab'], plan['ztab'], plan['zinfo'], plan['slot_rows'], plan['gate_rows'],
      *x1s)


def _expert_kernel(t0_ref, nt_ref, nu_ref, wg_ref, wu_ref, wd_ref, xs_ref, ye_ref,
                   xbuf, ybuf, wgu_scr, wd_scr, zbuf, in_sem, out_sem, zsem, *, max_tiles):
    e = pl.program_id(0)
    n_used = nu_ref[0]
    ahead = EXPERT_IN_SLOTS - 1

    def in_copy(t):
        slot = lax.rem(t, EXPERT_IN_SLOTS)
        src = xs_ref.at[pl.ds(pl.multiple_of(t * ROW_TILE, ROW_TILE), ROW_TILE), :]
        return pltpu.make_async_copy(src, xbuf.at[slot], in_sem.at[slot])

    def out_copy(t):
        slot = lax.rem(t, EXPERT_OUT_SLOTS)
        dst = ye_ref.at[pl.ds(pl.multiple_of(t * ROW_TILE, ROW_TILE), ROW_TILE), :]
        return pltpu.make_async_copy(ybuf.at[slot], dst, out_sem.at[slot])

    def zero_copy(t):
        dst = ye_ref.at[pl.ds(pl.multiple_of(t * ROW_TILE, ROW_TILE), ROW_TILE), :]
        return pltpu.make_async_copy(zbuf, dst, zsem)

    @pl.when(e == 0)
    def _():
        for t in range(ahead):
            @pl.when(t < n_used)
            def _():
                in_copy(t).start()

        zbuf[...] = jnp.zeros_like(zbuf)
        lax.fori_loop(n_used, max_tiles, lambda t, c: (zero_copy(t).start(), c)[1], 0)

    wgu_scr[:, 0:EXPERT_FF] = wg_ref[0].astype(BF16)
    wgu_scr[:, EXPERT_FF:] = wu_ref[0].astype(BF16)
    wd_scr[...] = wd_ref[0].astype(BF16)

    sub = EXPERT_SUB
    rows = [slice(j * sub, (j + 1) * sub) for j in range(ROW_TILE // sub)]
    d = lambda a, b: jnp.dot(a, b, preferred_element_type=F32)

    def tile_body(j, carry):
        t = t0_ref[e] + j
        slot = lax.rem(t, EXPERT_OUT_SLOTS)
        in_slot = lax.rem(t, EXPERT_IN_SLOTS)

        @pl.when(t + ahead < n_used)
        def _():
            in_copy(t + ahead).start()

        in_copy(t).wait()

        @pl.when(t >= EXPERT_OUT_SLOTS)
        def _():
            out_copy(t - EXPERT_OUT_SLOTS).wait()

        wgu, wd = wgu_scr[...], wd_scr[...]
        xs = [xbuf[in_slot, r, 0:D_MODEL] for r in rows]
        hs = [d(x, wgu) for x in xs]
        hhs = [(_silu(h[:, :EXPERT_FF]) * h[:, EXPERT_FF:]).astype(BF16) for h in hs]
        ys = [d(hh, wd) for hh in hhs]
        for r, y in zip(rows, ys):
            parts = xbuf[in_slot, r, D_MODEL:].astype(F32)
            gate = parts + pltpu.roll(parts, LANES // 2, axis=1)
            ybuf[slot, r, :] = (y * jnp.concatenate([gate] * (D_MODEL // LANES), axis=1)).astype(BF16)
        out_copy(t).start()
        return carry

    lax.fori_loop(0, nt_ref[e], tile_body, 0)

    @pl.when(e == pl.num_programs(0) - 1)
    def _():
        for back in range(EXPERT_OUT_SLOTS, 0, -1):
            @pl.when(n_used >= back)
            def _():
                out_copy(n_used - back).wait()
        lax.fori_loop(n_used, max_tiles, lambda t, c: (zero_copy(t).wait(), c)[1], 0)


def _experts(plan, xs, w_gate, w_up, w_down):
    max_tiles = xs.shape[0] // ROW_TILE
    wsel = lambda e, t0, nt, nu: (e, 0, 0)
    return pl.pallas_call(
        functools.partial(_expert_kernel, max_tiles=max_tiles),
        out_shape=jax.ShapeDtypeStruct((xs.shape[0], D_MODEL), BF16),
        grid_spec=pltpu.PrefetchScalarGridSpec(
            num_scalar_prefetch=3, grid=(N_EXPERTS,),
            in_specs=[pl.BlockSpec((1, D_MODEL, EXPERT_FF), wsel),
                      pl.BlockSpec((1, D_MODEL, EXPERT_FF), wsel),
                      pl.BlockSpec((1, EXPERT_FF, D_MODEL), wsel),
                      pl.BlockSpec(memory_space=pl.ANY)],
            out_specs=pl.BlockSpec(memory_space=pl.ANY),
            scratch_shapes=[pltpu.VMEM((EXPERT_IN_SLOTS, ROW_TILE, XS_WORDS), BF16),
                            pltpu.VMEM((EXPERT_OUT_SLOTS, ROW_TILE, D_MODEL), BF16),
                            pltpu.VMEM((D_MODEL, 2 * EXPERT_FF), BF16),
                            pltpu.VMEM((EXPERT_FF, D_MODEL), BF16),
                            pltpu.VMEM((ROW_TILE, D_MODEL), BF16),
                            pltpu.SemaphoreType.DMA((EXPERT_IN_SLOTS,)),
                            pltpu.SemaphoreType.DMA((EXPERT_OUT_SLOTS,)),
                            pltpu.SemaphoreType.DMA(())]),
        compiler_params=_cparams(("arbitrary",)), name="moe_experts",
    )(plan['tile_start'], plan['tile_count'], plan['n_used'], w_gate, w_up, w_down, xs)


def _combine_kernel(dst_ref, nslab_ref, x1_ref, slot_ref, ye_ref, ln_g_ref, ln_b_ref, y_ref,
                    buf, sem, *, tile_base, n_tiles):
    i = pl.program_id(0)
    g = tile_base + i
    cur = lax.rem(i, 2)

    def slab_copy(tile, buf_slot, j):
        d = pl.multiple_of(dst_ref[tile * PERM_SLABS + j], SLAB)
        dst = buf.at[buf_slot, pl.ds(pl.multiple_of(j * SLAB, SLAB), SLAB), :]
        return pltpu.make_async_copy(ye_ref.at[pl.ds(d, SLAB), :], dst, sem.at[buf_slot])

    @pl.when(i == 0)
    def _():
        buf[...] = jnp.zeros_like(buf)
        _slab_loop(nslab_ref[g], lambda j, u: slab_copy(g, cur, j).start(priority=u % 2))

    @pl.when(i + 1 < n_tiles)
    def _():
        _slab_loop(nslab_ref[g + 1], lambda j, u: slab_copy(g + 1, 1 - cur, j).start(priority=u % 2))

    _slab_loop(nslab_ref[g], lambda j, u: slab_copy(g, cur, j).wait())

    sub = COMBINE_SUB
    subs = [slice(j * sub, (j + 1) * sub) for j in range(TOK_TILE // sub)]
    col = lax.broadcasted_iota(I32, (sub, PERM_ROWS), 1)
    sl = slot_ref[0]
    diag = (lax.broadcasted_iota(I32, (sub, sub), 0) == lax.broadcasted_iota(I32, (sub, sub), 1))
    as_col = lambda row: jnp.sum(jnp.where(diag, row, 0), axis=1, keepdims=True)
    picks = [jnp.where((col == as_col(sl[0:1, r])) | (col == as_col(sl[1:2, r])), 1.0, 0.0).astype(BF16)
             for r in subs]
    rows = buf[cur]
    moes = [jnp.dot(pick, rows, preferred_element_type=F32) for pick in picks]
    for r, moe in zip(subs, moes):
        y_ref[r, :] = _layer_norm(DEEPNORM_ALPHA * x1_ref[r, :] + moe, ln_g_ref[...], ln_b_ref[...])


def _combine(plan, tile_base, x1, ye, ln_g, ln_b):
    n = x1.shape[0]
    n_tiles = n // TOK_TILE
    tok = lambda w: pl.BlockSpec((TOK_TILE, w), lambda i, d, ns: (i, 0))
    const = lambda a: pl.BlockSpec(a.shape, lambda i, d, ns: (0,) * a.ndim)
    return pl.pallas_call(
        functools.partial(_combine_kernel, tile_base=tile_base, n_tiles=n_tiles),
        out_shape=jax.ShapeDtypeStruct((n, D_MODEL), F32),
        grid_spec=pltpu.PrefetchScalarGridSpec(
            num_scalar_prefetch=2, grid=(n_tiles,),
            in_specs=[tok(D_MODEL),
                      pl.BlockSpec((1, TOP_K, TOK_TILE), lambda i, d, ns: (tile_base + i, 0, 0)),
                      pl.BlockSpec(memory_space=pl.ANY), const(ln_g), const(ln_b)],
            out_specs=tok(D_MODEL),
            scratch_shapes=[pltpu.VMEM((2, PERM_ROWS, D_MODEL), BF16), pltpu.SemaphoreType.DMA((2,))]),
        compiler_params=_cparams(("arbitrary",)), name="moe_combine_%d" % tile_base,
    )(plan['slab_dst'], plan['nslab'], x1, plan['slot_rows'], ye, ln_g, ln_b)


def _routing_plan(ids, gates):
    nt = ids.shape[1] // TOK_TILE
    pairs = TOP_K * TOK_TILE
    ex = jnp.arange(N_EXPERTS, dtype=I32)
    per_tile = lambda a: jnp.swapaxes(a.reshape(TOP_K, nt, TOK_TILE), 0, 1)
    flat = per_tile(ids).reshape(nt, pairs)
    onehot = (flat[:, None, :] == ex[None, :, None])
    p = np.arange(pairs)
    triu = jnp.asarray(p[:, None] <= p[None, :], BF16)
    csum = jnp.dot(onehot.astype(BF16).reshape(nt * N_EXPERTS, pairs), triu,
                   preferred_element_type=F32).astype(I32).reshape(nt, N_EXPERTS, pairs)
    oh = onehot.astype(I32)
    rank = jnp.sum(oh * (csum - 1), axis=1)
    cnt = csum[:, :, -1]
    cpad = (cnt + SLAB - 1) // SLAB * SLAB
    seg_end = jnp.cumsum(cpad, axis=1)
    seg_off = seg_end - cpad
    slot = jnp.sum(oh * seg_off[:, :, None], axis=1) + rank
    run_end = jnp.cumsum(cpad, axis=0)
    ntiles_e = (run_end[-1] + ROW_TILE - 1) // ROW_TILE
    tile_end = jnp.cumsum(ntiles_e)
    dst_run = ((tile_end - ntiles_e) * ROW_TILE)[None, :] + run_end - cpad
    j8 = jnp.arange(PERM_SLABS, dtype=I32) * SLAB
    e_of = jnp.minimum(jnp.sum((j8[None, :, None] >= seg_end[:, None, :]).astype(I32), axis=2),
                       N_EXPERTS - 1)
    sel = (e_of[:, :, None] == ex).astype(I32)
    slab_dst = jnp.sum(sel * (dst_run - seg_off)[:, None, :], axis=2) + j8[None, :]
    n_used = tile_end[-1]
    row_start = (tile_end - ntiles_e) * ROW_TILE
    tail_cnt = (ntiles_e * ROW_TILE - run_end[-1]) // SLAB
    tail_end = jnp.cumsum(tail_cnt)
    k = jnp.arange(ZERO_TABLE, dtype=I32)
    e_k = jnp.minimum(jnp.sum((k[:, None] >= tail_end[None, :]).astype(I32), axis=1), N_EXPERTS - 1)
    base_k = jnp.sum((e_k[:, None] == ex).astype(I32)
                     * (row_start + run_end[-1] - SLAB * (tail_end - tail_cnt))[None, :], axis=1)
    return dict(
        slab_dst=slab_dst.reshape(-1).astype(I32), nslab=(seg_end[:, -1] // SLAB).astype(I32),
        ztab=(base_k + SLAB * k).astype(I32), zinfo=jnp.stack([tail_end[-1], n_used]).astype(I32),
        slot_rows=slot.reshape(nt, TOP_K, TOK_TILE).astype(I32),
        gate_rows=per_tile(gates).astype(F32),
        tile_start=(tile_end - ntiles_e).astype(I32), tile_count=ntiles_e.astype(I32),
        n_used=n_used.reshape(1).astype(I32))


def _max_row_tiles(n_tokens):
    rows = TOP_K * n_tokens + (n_tokens // TOK_TILE) * N_EXPERTS * (SLAB - 1)
    return (rows + ROW_TILE - 1) // ROW_TILE + N_EXPERTS


def _moe(x1s, routes, wts):
    ids = jnp.concatenate([r[TOP_K:2 * TOP_K, :] for r in routes], axis=1).astype(I32)
    gates = jnp.concatenate([r[0:TOP_K, :] for r in routes], axis=1)
    plan = _routing_plan(ids, gates)
    max_tiles = _max_row_tiles(ids.shape[1])
    bases = [0]
    for x1 in x1s[:-1]:
        bases.append(bases[-1] + x1.shape[0] // TOK_TILE)
    xs = _dispatch(plan, x1s, max_tiles)
    ye = _experts(plan, xs, wts['w_gate'], wts['w_up'], wts['w_down'])
    return [_combine(plan, base, x1, ye, wts['ln2_g'], wts['ln2_b']) for base, x1 in zip(bases, x1s)]


def _prep_weights(w_in, w_out, conv_w, a_log, dt_bias, gdn_norm_w, ln1_g, ln1_b, w_router_group,
                  w_router_expert, w_gate, w_up, w_down, ln2_g, ln2_b):
    pad_row = lambda v: jnp.pad(v.astype(F32), (0, LANES - v.shape[0]))[None, :]
    wr = jnp.pad(jnp.concatenate([w_router_group, w_router_expert], axis=1),
                 ((0, 0), (0, LANES - N_GROUPS - N_EXPERTS)))
    wr_hi = wr.astype(BF16)
    wr_mid = (wr - wr_hi.astype(F32)).astype(BF16)
    group = ATTN_HEADS // ATTN_KV_HEADS
    wq = w_in[:, :Q_COLS].reshape(D_MODEL, ATTN_KV_HEADS, group, HEAD_DIM)
    wq = jnp.swapaxes(wq, 1, 2).reshape(D_MODEL, Q_COLS)
    w_all = jnp.concatenate([wq, w_in[:, Q_COLS:]], axis=1)
    w_all = jnp.pad(w_all, ((0, 0), (0, IN_COLS_PAD - w_all.shape[1]))).astype(BF16)
    wo_q = w_out[:Q_COLS].reshape(ATTN_KV_HEADS, group, HEAD_DIM, D_MODEL)
    wo_q = jnp.swapaxes(wo_q, 0, 1).reshape(Q_COLS, D_MODEL)
    wo = jnp.concatenate([wo_q, w_out[Q_COLS:]], axis=0)
    return dict(
        w_all=w_all, convw=conv_w.astype(F32), alog=pad_row(a_log), dtb=pad_row(dt_bias),
        norm_w=gdn_norm_w.astype(F32)[None, :], wo=wo.astype(BF16),
        ln1_g=ln1_g[None, :], ln1_b=ln1_b[None, :], wr=jnp.concatenate([wr_hi, wr_mid], axis=1),
        w_gate=w_gate, w_up=w_up, w_down=w_down, ln2_g=ln2_g[None, :], ln2_b=ln2_b[None, :])


def _layer(x_prompt, x_sample, cache_k, cache_v, state_gdn, state_conv, wts):
    bp, sp, _ = x_prompt.shape
    bs, ts, _ = x_sample.shape
    n_p = bp * sp

    xp = x_prompt.reshape(n_p, D_MODEL)
    (q, k, v, qg, kg, vg, z, gcb, utail) = _proj(xp, np.arange(sp), wts, GDN_CHUNK, bp)
    attn_p = _attn_prompt(q, k, v, wts['sinks'], bp)
    gdn_p, s_p = _gdn_prompt(qg, kg, vg, z, gcb, wts['norm_w'], bp)
    last_win = lambda a: a.reshape(bp, sp, KV_COLS)[:, sp - WINDOW:].reshape(bp, WINDOW, ATTN_KV_HEADS,
                                                                            HEAD_DIM)
    new_k_p, new_v_p = last_win(k), last_win(v)
    tiles_per_seq = sp // min(PROJ_TILE, sp)
    conv_p = utail.reshape(bp, tiles_per_seq, SUBLANES, CONV_DIM)[:, -1, SUBLANES - (CONV_W - 1):]

    lo, hi = SAMPLE_FIRST, SAMPLE_FIRST + ts
    xs_rows = jnp.pad(x_sample, ((0, 0), (lo, SAMPLE_SLOTS - hi), (0, 0))).reshape(bs * SAMPLE_SLOTS, D_MODEL)
    hist = jnp.pad(state_conv, ((0, 0), (0, SAMPLE_SLOTS - lo), (0, 0))).reshape(bs * SAMPLE_SLOTS, CONV_DIM)
    slot = np.arange(SAMPLE_SLOTS)
    valid = jnp.asarray(np.tile((slot >= lo) & (slot < hi), bs)[:, None], F32)
    pos_s = np.tile(PAST_LEN + slot - lo, bs)
    (q, k, v, qg, kg, vg, z, gcb, u_s) = _proj(xs_rows, pos_s, wts, SAMPLE_SLOTS, 1, hist, valid)
    ck = cache_k.reshape(bs, WINDOW, KV_COLS)
    cv = cache_v.reshape(bs, WINDOW, KV_COLS)
    attn_s, kwin, vwin = _attn_sample(q, k, v, ck, cv, wts['sinks'], bs, ts)
    gdn_s, s_s = _gdn_sample(qg, kg, vg, z, gcb, wts['norm_w'], state_gdn)
    real = lambda a: a.reshape(bs, SAMPLE_SLOTS, -1)[:, lo:hi]
    new_k_s = kwin.reshape(bs, WINDOW, ATTN_KV_HEADS, HEAD_DIM)
    new_v_s = vwin.reshape(bs, WINDOW, ATTN_KV_HEADS, HEAD_DIM)
    conv_s = u_s.reshape(bs, SAMPLE_SLOTS, CONV_DIM)[:, hi - (CONV_W - 1):hi]

    x1_p, route_p = _post(attn_p, gdn_p, xp, wts)
    x1_s, route_s = _post(real(attn_s).reshape(bs * ts, Q_COLS), real(gdn_s).reshape(bs * ts, Z_COLS),
                          x_sample.reshape(bs * ts, D_MODEL), wts)
    y_p, y_s = _moe([x1_p, x1_s], [route_p, route_s], wts)
    return (y_p.reshape(bp, sp, D_MODEL), y_s.reshape(bs, ts, D_MODEL), new_k_p, new_v_p, s_p, conv_p,
            new_k_s, new_v_s, s_s, conv_s)


def kernel(x_prompt, x_sample, cache_attn_k, cache_attn_v, state_gdn, state_conv, w_in, w_out,
           attn_sinks, conv_w, a_log, dt_bias, gdn_norm_w, ln1_g, ln1_b, w_router_group,
           w_router_expert, w_gate, w_up, w_down, ln2_g, ln2_b):
    assert w_in.shape[0] == DEPTH
    l = 0
    wts = _prep_weights(w_in[l], w_out[l], conv_w[l], a_log[l], dt_bias[l], gdn_norm_w[l], ln1_g[l],
                        ln1_b[l], w_router_group[l], w_router_expert[l], w_gate[l], w_up[l],
                        w_down[l], ln2_g[l], ln2_b[l])
    wts['sinks'] = attn_sinks[l]
    outs = _layer(x_prompt, x_sample, cache_attn_k[l], cache_attn_v[l], state_gdn[l], state_conv[l], wts)
    (y_p, y_s, k_p, v_p, s_p, c_p, k_s, v_s, s_s, c_s) = outs
    add = lambda a: a[None]
    return (y_p, y_s, add(k_p), add(v_p), add(s_p), add(c_p), add(k_s), add(v_s), add(s_s), add(c_s))
```

```python
import functools
import math

import jax
import jax.numpy as jnp
import numpy as np
from jax import lax
from jax.experimental import pallas as pl
from jax.experimental.pallas import tpu as pltpu

F32 = jnp.float32
BF16 = jnp.bfloat16
I32 = jnp.int32

D_MODEL = 1024
ATTN_HEADS = 8
ATTN_KV_HEADS = 2
HEAD_DIM = 64
WINDOW = 128
ROT_DIM = HEAD_DIM // 4
ROPE_THETA = 500000.0
GDN_HEADS = 4
GDN_DK = 128
GDN_DV = 128
CONV_W = 4
QK_COLS = GDN_HEADS * GDN_DK
CONV_DIM = 2 * QK_COLS + GDN_HEADS * GDN_DV
Z_COLS = GDN_HEADS * GDN_DV
Q_COLS = ATTN_HEADS * HEAD_DIM
KV_COLS = ATTN_KV_HEADS * HEAD_DIM
N_GROUPS = 4
EXPERTS_PER_GROUP = 8
N_EXPERTS = N_GROUPS * EXPERTS_PER_GROUP
TOP_K = 2
EXPERT_FF = 256
NORM_EPS = 1e-5
L2_EPS = 1e-6
DEPTH = 1
DEEPNORM_ALPHA = (2 * DEPTH) ** 0.25
PAST_LEN = 8192

LANES = 128
SUBLANES = 8
IN_SPLITS = (0, Q_COLS + 2 * KV_COLS, Q_COLS + 2 * KV_COLS + CONV_DIM,
             Q_COLS + 2 * KV_COLS + CONV_DIM + Z_COLS, Q_COLS + 2 * KV_COLS + CONV_DIM + Z_COLS + LANES)
IN_COLS_PAD = IN_SPLITS[-1]
TOK_TILE = 512
PROJ_TILE = 512
PROJ_SUB = 128
POST_TILE = 1024
POST_SUB = 256
GDN_CHUNK = 128
GDN_SEQ_PER_STEP = 4
ATTN_BLOCKS_PER_STEP = 4
ATTN_SEQS_PER_STEP = 16
INV_BASE = 16
SAMPLE_SLOTS = 8
SAMPLE_FIRST = CONV_W - 1
ROW_TILE = 512
EXPERT_SUB = 256
EXPERT_IN_SLOTS = 4
EXPERT_OUT_SLOTS = 3
SLAB_UNROLL = 4
COMBINE_SUB = 128
SLAB = 16
PERM_ROWS = TOP_K * TOK_TILE + N_EXPERTS * SLAB
PERM_SLABS = PERM_ROWS // SLAB
XS_WORDS = D_MODEL + LANES
ZERO_TABLE = N_EXPERTS * (ROW_TILE // SLAB)
VMEM_LIMIT = 48 * 1024 * 1024
NEG_BIG = -1e30


def _cparams(sem):
    return pltpu.CompilerParams(dimension_semantics=sem, vmem_limit_bytes=VMEM_LIMIT)


def _bdot(a, b):
    return jnp.dot(a.astype(BF16), b.astype(BF16), preferred_element_type=F32)


def _bdot_nt(a, b):
    return lax.dot_general(a.astype(BF16), b.astype(BF16), (((1,), (1,)), ((), ())),
                           preferred_element_type=F32)


def _bdot_tn(a, b):
    return lax.dot_general(a.astype(BF16), b.astype(BF16), (((0,), (0,)), ((), ())),
                           preferred_element_type=F32)


def _div_pow2(x, n):
    return jnp.right_shift(x, int(math.log2(n)))


def _mod_pow2(x, n):
    return jnp.bitwise_and(x, n - 1)


def _split3(x):
    hi = x.astype(BF16)
    r = x - hi.astype(F32)
    mid = r.astype(BF16)
    lo = (r - mid.astype(F32)).astype(BF16)
    return hi, mid, lo


def _dot_exact_lhs01(m01, x):
    hi, mid, lo = _split3(x)
    d = lambda t: jnp.dot(m01, t, preferred_element_type=F32)
    return d(hi) + d(mid) + d(lo)


def _sigmoid(x):
    return 1.0 / (1.0 + jnp.exp(-x))


def _silu(x):
    return x * _sigmoid(x)


def _softplus(x):
    return jnp.maximum(x, 0.0) + jnp.log1p(jnp.exp(-jnp.abs(x)))


def _layer_norm(h, g, b):
    mu = jnp.mean(h, axis=-1, keepdims=True)
    d = h - mu
    var = jnp.mean(d * d, axis=-1, keepdims=True)
    return d * lax.rsqrt(var + NORM_EPS) * g + b


def _proj_kernel(*refs, tm, has_hist, full_u, one_segment):
    it = iter(refs)
    x_ref, cos_ref, sin_ref = next(it), next(it), next(it)
    w_ref = next(it)
    convw_ref, alog_ref, dtb_ref, tri_ref, seg_ref = next(it), next(it), next(it), next(it), next(it)
    hist_ref = valid_ref = None
    if has_hist:
        hist_ref, valid_ref = next(it), next(it)
    q_ref, k_ref, v_ref = next(it), next(it), next(it)
    qg_ref, kg_ref, vg_ref, z_ref, gcb_ref, u_ref = (next(it) for _ in range(6))
    ubuf = next(it)

    t = pl.program_id(1)
    sub = PROJ_SUB
    rows = [slice(j * sub, (j + 1) * sub) for j in range(tm // sub)]
    lane = lax.broadcasted_iota(I32, (sub, LANES), 1)
    first_half = _mod_pow2(lane, HEAD_DIM) < (ROT_DIM // 2)

    @pl.when(t == 0)
    def _():
        ubuf[0:SUBLANES, :] = jnp.zeros((SUBLANES, CONV_DIM), F32)

    @pl.when(t > 0)
    def _():
        ubuf[0:SUBLANES, :] = ubuf[tm:tm + SUBLANES, :]

    dots = []
    for r in rows:
        xb = x_ref[r, :].astype(BF16)
        dots.append([jnp.dot(xb, w_ref[:, lo:hi], preferred_element_type=F32)
                     for lo, hi in zip(IN_SPLITS[:-1], IN_SPLITS[1:])])

    def l2n(s):
        return s * lax.rsqrt(jnp.sum(s * s, axis=1, keepdims=True) + L2_EPS)

    for r, (pq, u, z, ab) in zip(rows, dots):
        cosv, sinv = cos_ref[r, :], sin_ref[r, :]

        def rope(s):
            sw = jnp.where(first_half, pltpu.roll(s, LANES - ROT_DIM // 2, axis=1),
                           pltpu.roll(s, ROT_DIM // 2, axis=1))
            return s * cosv + sw * sinv

        for j in range(Q_COLS // LANES):
            q_ref[r, j * LANES:(j + 1) * LANES] = rope(pq[:, j * LANES:(j + 1) * LANES])
        k_ref[r, :] = rope(pq[:, Q_COLS:Q_COLS + KV_COLS])
        v_ref[r, :] = pq[:, Q_COLS + KV_COLS:Q_COLS + 2 * KV_COLS]
        z_ref[r, :] = z

        if has_hist:
            u = u + hist_ref[r, :]
        if full_u:
            u_ref[r, :] = u
        elif r.stop == tm:
            u_ref[...] = u[sub - SUBLANES:, :]
        base = SUBLANES + r.start
        ubuf[base:base + sub, :] = u
        acc = u * convw_ref[CONV_W - 1:CONV_W, :]
        for j in range(1, CONV_W):
            acc = acc + ubuf[base - j:base - j + sub, :] * convw_ref[CONV_W - 1 - j:CONV_W - j, :]
        c = _silu(acc)
        if has_hist:
            c = c * valid_ref[r, :]
        for h in range(GDN_HEADS):
            sl = slice(h * GDN_DK, (h + 1) * GDN_DK)
            qg_ref[r, sl] = l2n(c[:, sl]) * (GDN_DK ** -0.5)
            kg_ref[r, sl] = l2n(c[:, QK_COLS + h * GDN_DK:QK_COLS + (h + 1) * GDN_DK])
        vg_ref[r, :] = c[:, 2 * QK_COLS:]

        g = -jnp.exp(alog_ref[...]) * _softplus(ab + dtb_ref[...])
        beta = _sigmoid(ab)
        if has_hist:
            g = g * valid_ref[r, :]
            beta = beta * valid_ref[r, :]
        g = jnp.where(lane < GDN_HEADS, g, 0.0)
        gc = _dot_exact_lhs01(tri_ref[...], g)
        if one_segment:
            gl = jnp.broadcast_to(gc[sub - 1:sub, :], (sub, LANES))
        else:
            gl = _dot_exact_lhs01(seg_ref[...], g)
        gcb_ref[r, :] = jnp.where(lane < GDN_HEADS, gc,
                                  jnp.where(lane < 2 * GDN_HEADS, beta,
                                            jnp.where(lane < 3 * GDN_HEADS,
                                                      pltpu.roll(gl, 2 * GDN_HEADS, axis=1), 0.0)))


def _rope_tables(pos):
    half = ROT_DIM // 2
    pos = np.asarray(pos, np.float64)
    inv_freq = ROPE_THETA ** (-np.arange(half, dtype=np.float64) * 2.0 / ROT_DIM)
    ang = pos[:, None] * inv_freq[None, :]
    cos, sin = np.cos(ang), np.sin(ang)
    p = pos.shape[0]
    cpat = np.concatenate([cos, cos, np.ones((p, HEAD_DIM - ROT_DIM))], axis=1)
    spat = np.concatenate([-sin, sin, np.zeros((p, HEAD_DIM - ROT_DIM))], axis=1)
    rep = (1, LANES // HEAD_DIM)
    return jnp.asarray(np.tile(cpat, rep), F32), jnp.asarray(np.tile(spat, rep), F32)


def _segment_matrices(tm, seg_len):
    i = np.arange(tm)
    same = (i[:, None] // seg_len) == (i[None, :] // seg_len)
    tri = same & (i[None, :] <= i[:, None])
    return jnp.asarray(tri, BF16), jnp.asarray(same, BF16)


def _proj(x, pos, wts, seg_len, n_seq, hist=None, valid=None):
    n = x.shape[0]
    rows = n // n_seq
    tm = min(PROJ_TILE, rows)
    nt = rows // tm
    has_hist = hist is not None
    cos_t, sin_t = _rope_tables(pos)
    tri, seg = _segment_matrices(PROJ_SUB, seg_len)

    tok = lambda w: pl.BlockSpec((tm, w), lambda b, t: (b * nt + t, 0))
    const = lambda a: pl.BlockSpec(a.shape, lambda b, t: (0,) * a.ndim)
    in_arrays = [x, cos_t, sin_t, wts['w_all'], wts['convw'], wts['alog'], wts['dtb'], tri, seg]
    in_specs = [tok(D_MODEL), pl.BlockSpec((tm, LANES), lambda b, t: (t, 0)),
                pl.BlockSpec((tm, LANES), lambda b, t: (t, 0))] + [const(a) for a in in_arrays[3:]]
    if has_hist:
        in_arrays += [hist, valid]
        in_specs += [tok(CONV_DIM), tok(1)]
    u_rows = n if has_hist else (n // tm) * SUBLANES
    u_block = tm if has_hist else SUBLANES
    out_shape = [jax.ShapeDtypeStruct((n, Q_COLS), F32), jax.ShapeDtypeStruct((n, KV_COLS), F32),
                 jax.ShapeDtypeStruct((n, KV_COLS), F32), jax.ShapeDtypeStruct((n, QK_COLS), F32),
                 jax.ShapeDtypeStruct((n, QK_COLS), F32), jax.ShapeDtypeStruct((n, Z_COLS), F32),
                 jax.ShapeDtypeStruct((n, Z_COLS), F32), jax.ShapeDtypeStruct((n, LANES), F32),
                 jax.ShapeDtypeStruct((u_rows, CONV_DIM), F32)]
    out_specs = [tok(Q_COLS), tok(KV_COLS), tok(KV_COLS), tok(QK_COLS), tok(QK_COLS), tok(Z_COLS),
                 tok(Z_COLS), tok(LANES),
                 pl.BlockSpec((u_block, CONV_DIM), lambda b, t: (b * nt + t, 0))]
    return pl.pallas_call(
        functools.partial(_proj_kernel, tm=tm, has_hist=has_hist, full_u=has_hist,
                          one_segment=seg_len == PROJ_SUB),
        out_shape=out_shape, grid=(n_seq, nt), in_specs=in_specs, out_specs=out_specs,
        scratch_shapes=[pltpu.VMEM((tm + SUBLANES, CONV_DIM), F32)],
        compiler_params=_cparams(("arbitrary", "arbitrary")),
        name="proj_hist" if has_hist else "proj",
    )(*in_arrays)


def _attn_blocks(qs, kcats, vcats, biases, sink, tq):
    lane = lax.broadcasted_iota(I32, (tq, LANES), 1)
    low = lane < HEAD_DIM
    n_slab = Q_COLS // LANES

    def stack(q):
        slabs = [q[:, j * LANES:(j + 1) * LANES] * (HEAD_DIM ** -0.5) for j in range(n_slab)]
        parts = ([jnp.where(low, s, 0.0) for s in slabs] + [jnp.where(low, 0.0, s) for s in slabs])
        return jnp.concatenate(parts, axis=0).astype(BF16)

    def unstack(o8):
        return [jnp.where(low, o8[j * tq:(j + 1) * tq, :], o8[(n_slab + j) * tq:(n_slab + j + 1) * tq, :])
                for j in range(n_slab)]

    rows = ATTN_HEADS * tq
    half = rows // 2
    klow = lax.broadcasted_iota(I32, (2 * WINDOW, LANES), 1) < HEAD_DIM
    one = jnp.ones((), BF16)
    q8s = _each(stack, qs)
    ss = _each(lambda q8, kc, b: _bdot_nt(q8, kc) + b, q8s, kcats, biases)
    ms = _each(lambda s: jnp.maximum(jnp.broadcast_to(jnp.max(s, axis=1, keepdims=True), (rows, LANES)),
                                     sink), ss)
    ps = _each(lambda s, m: jnp.exp(s - jnp.concatenate([m, m], axis=1)).astype(BF16), ss, ms)
    pv0 = _each(lambda p, vc: jnp.dot(p[:half], jnp.where(klow, vc, one), preferred_element_type=F32),
                ps, vcats)
    pv1 = _each(lambda p, vc: jnp.dot(p[half:], jnp.where(klow, one, vc), preferred_element_type=F32),
                ps, vcats)
    pvs = _each(lambda a, b: jnp.concatenate([a, b], axis=0), pv0, pv1)
    o8s = _each(lambda pv, m: pv / (pltpu.roll(pv, HEAD_DIM, axis=1) + jnp.exp(sink - m)), pvs, ms)
    return _each(unstack, o8s)


def _attn_prompt_kernel(q_ref, kc_ref, vc_ref, kp_ref, vp_ref, bias0_ref, bias_ref, sink_ref, o_ref, *,
                        nblk):
    kall = jnp.concatenate([kp_ref[...], kc_ref[...]], axis=0).astype(BF16)
    vall = jnp.concatenate([vp_ref[...], vc_ref[...]], axis=0).astype(BF16)
    win = lambda a, j: a[j * WINDOW:(j + 2) * WINDOW, :]
    qs = [q_ref[j * WINDOW:(j + 1) * WINDOW, :] for j in range(nblk)]
    biases = [bias0_ref[0]] + [bias_ref[...]] * (nblk - 1)
    outs = _attn_blocks(qs, [win(kall, j) for j in range(nblk)], [win(vall, j) for j in range(nblk)],
                        biases, sink_ref[...], WINDOW)
    for j, slabs in enumerate(outs):
        for c, slab in enumerate(slabs):
            o_ref[j * WINDOW:(j + 1) * WINDOW, c * LANES:(c + 1) * LANES] = slab


def _attn_sample_kernel(q_ref, kc_ref, vc_ref, kp_ref, vp_ref, bias_ref, sink_ref, o_ref, kw_ref, vw_ref,
                        *, nseq, n_new):
    tq = SAMPLE_SLOTS
    zpad = jnp.zeros((WINDOW - tq, LANES), F32)
    rows = lambda ref, j: ref[j * tq:(j + 1) * tq, :]
    cat = lambda pref, cref, j: jnp.concatenate([pref[j], rows(cref, j), zpad], axis=0).astype(BF16)
    outs = _attn_blocks([rows(q_ref, j) for j in range(nseq)],
                        [cat(kp_ref, kc_ref, j) for j in range(nseq)],
                        [cat(vp_ref, vc_ref, j) for j in range(nseq)],
                        [bias_ref[...]] * nseq, sink_ref[...], tq)
    for j, slabs in enumerate(outs):
        for c, slab in enumerate(slabs):
            o_ref[j * tq:(j + 1) * tq, c * LANES:(c + 1) * LANES] = slab
    row = lax.broadcasted_iota(I32, (WINDOW, LANES), 0)
    keep = WINDOW - n_new
    for pref, cref, wref in ((kp_ref, kc_ref, kw_ref), (vp_ref, vc_ref, vw_ref)):
        for j in range(nseq):
            new = jnp.concatenate([rows(cref, j), zpad], axis=0)
            wref[j] = jnp.where(row < keep, pltpu.roll(pref[j], keep, axis=0),
                                pltpu.roll(new, keep - SAMPLE_FIRST, axis=0))


def _sink_rows(sinks, tq):
    return jnp.broadcast_to(jnp.repeat(sinks.astype(F32), tq)[:, None], (ATTN_HEADS * tq, LANES))


def _attn_bias(tq, q_off, k_lo, k_hi, has_prev):
    qi = (np.arange(ATTN_HEADS * tq) % tq)[:, None]
    c = np.arange(2 * WINDOW)[None, :]
    cj = c - WINDOW
    vis_prev = (c < WINDOW) & (c > qi - q_off) & has_prev
    vis_cur = (c >= WINDOW) & (cj <= qi) & (cj >= k_lo) & (cj <= k_hi)
    return np.where(vis_prev | vis_cur, 0.0, NEG_BIG).astype(np.float32)


def _attn_prompt(q, k, v, sinks, n_seq):
    n = q.shape[0]
    nb = n // n_seq // WINDOW
    nblk = min(ATTN_BLOCKS_PER_STEP, nb)
    steps = nb // nblk
    tq = nblk * WINDOW
    cur = lambda w: pl.BlockSpec((tq, w), lambda b, i: (b * steps + i, 0))
    prev = pl.BlockSpec((WINDOW, LANES), lambda b, i: (b * nb + jnp.maximum(i * nblk - 1, 0), 0))
    bias2 = jnp.asarray(np.stack([_attn_bias(WINDOW, 0, 0, WINDOW - 1, False),
                                  _attn_bias(WINDOW, 0, 0, WINDOW - 1, True)]))
    rows = ATTN_HEADS * WINDOW
    return pl.pallas_call(
        functools.partial(_attn_prompt_kernel, nblk=nblk),
        out_shape=jax.ShapeDtypeStruct((n, Q_COLS), F32), grid=(n_seq, steps),
        in_specs=[cur(Q_COLS), cur(LANES), cur(LANES), prev, prev,
                  pl.BlockSpec((1, rows, 2 * WINDOW), lambda b, i: (jnp.minimum(i, 1), 0, 0)),
                  pl.BlockSpec((rows, 2 * WINDOW), lambda b, i: (0, 0)),
                  pl.BlockSpec((rows, LANES), lambda b, i: (0, 0))],
        out_specs=cur(Q_COLS),
        compiler_params=_cparams(("arbitrary", "arbitrary")), name="attn_prompt",
    )(q, k, v, k, v, bias2, bias2[1], _sink_rows(sinks, WINDOW))


def _attn_sample(q, k, v, cache_k, cache_v, sinks, n_seq, n_new):
    tq = SAMPLE_SLOTS
    nseq = min(ATTN_SEQS_PER_STEP, n_seq)
    cur = lambda w: pl.BlockSpec((nseq * tq, w), lambda b: (b, 0))
    prev = pl.BlockSpec((nseq, WINDOW, LANES), lambda b: (b, 0, 0))
    bias = jnp.asarray(_attn_bias(tq, SAMPLE_FIRST, SAMPLE_FIRST, SAMPLE_FIRST + 3, True))
    win = jax.ShapeDtypeStruct((n_seq, WINDOW, LANES), F32)
    return pl.pallas_call(
        functools.partial(_attn_sample_kernel, nseq=nseq, n_new=n_new),
        out_shape=[jax.ShapeDtypeStruct((n_seq * tq, Q_COLS), F32), win, win], grid=(n_seq // nseq,),
        in_specs=[cur(Q_COLS), cur(LANES), cur(LANES), prev, prev,
                  pl.BlockSpec(bias.shape, lambda b: (0, 0)),
                  pl.BlockSpec((ATTN_HEADS * tq, LANES), lambda b: (0, 0))],
        out_specs=[cur(Q_COLS), prev, prev],
        compiler_params=_cparams(("arbitrary",)), name="attn_sample",
    )(q, k, v, cache_k, cache_v, bias, _sink_rows(sinks, tq))


def _each(f, *lists):
    return [f(*args) for args in zip(*lists)]


def _unit_lower_inverse(ms, eye, same_base, base_only=False):
    c = ms[0].shape[0]

    def neumann(q0s, n_factors):
        xs = _each(lambda q: eye + q, q0s)
        if n_factors == 1:
            return xs
        qs = _each(_bdot, q0s, q0s)
        for _ in range(n_factors - 2):
            prods = _each(lambda x, q: _bdot(jnp.concatenate([x, q], axis=0), q), xs, qs)
            xs = _each(lambda x, pr: x + pr[:c], xs, prods)
            qs = _each(lambda pr: pr[c:], prods)
        return _each(lambda x, q: x + _bdot(x, q), xs, qs)

    ds = _each(lambda m: jnp.where(same_base, m, 0.0), ms)
    xs = neumann(_each(lambda d: -d, ds), int(math.log2(INV_BASE)))
    nblk = c // INV_BASE
    if nblk == 1 or base_only:
        return xs
    ls = _each(lambda m, d: m - d, ms, ds)
    ns = _each(lambda x, l: -_bdot(x, l), xs, ls)
    ys = neumann(ns, int(math.log2(nblk)))
    return _each(_bdot, ys, xs)


def _gdn_intra(qs, ks, vs, gcs, gls, betas, same_seq, low_incl, low_strict, eye, same_base,
               base_only=False):
    del same_seq
    e_gcs = _each(jnp.exp, gcs)

    def decay_of(gc):
        gc_row = jnp.sum(jnp.where(eye > 0, gc, 0.0), axis=0, keepdims=True)
        return jnp.where(low_incl, jnp.exp(jnp.where(low_incl, gc - gc_row, 0.0)), 0.0)

    c = qs[0].shape[0]
    decays = _each(decay_of, gcs)
    kbs = _each(lambda k, b: k * b, ks, betas)
    vbs = _each(lambda v, b: v * b, vs, betas)
    kqs = _each(lambda kb, q, k: _bdot_nt(jnp.concatenate([kb, q], axis=0), k), kbs, qs, ks)
    ms = _each(lambda kq, d: jnp.where(low_strict, kq[:c] * d, 0.0), kqs, decays)
    attns = _each(lambda kq, d: kq[c:] * d, kqs, decays)
    tmats = _unit_lower_inverse(ms, eye, same_base, base_only)
    uws = _each(lambda t, vb, kb, e: _bdot(t, jnp.concatenate([vb, kb * e], axis=1)),
                tmats, vbs, kbs, e_gcs)
    us = _each(lambda uw: uw[:, :GDN_DV], uws)
    ws = _each(lambda uw: uw[:, GDN_DV:], uws)
    q_decs = _each(lambda q, e: q * e, qs, e_gcs)
    k_decs = _each(lambda k, gl, gc: k * jnp.exp(gl - gc), ks, gls, gcs)
    return us, ws, attns, q_decs, k_decs


def _chunk_masks(c, seq_len):
    i = lax.broadcasted_iota(I32, (c, c), 0)
    j = lax.broadcasted_iota(I32, (c, c), 1)
    same_seq = _div_pow2(i, seq_len) == _div_pow2(j, seq_len)
    low_incl = same_seq & (i >= j)
    low_strict = same_seq & (i > j)
    eye = (i == j).astype(F32)
    same_base = _div_pow2(i, INV_BASE) == _div_pow2(j, INV_BASE)
    return same_seq, low_incl, low_strict, eye, same_base


def _gated_rms(o, z, nw):
    o = o * lax.rsqrt(jnp.mean(o * o, axis=1, keepdims=True) + NORM_EPS) * nw
    return o * _silu(z)


def _gdn_prompt_kernel(qg_ref, kg_ref, vg_ref, z_ref, gcb_ref, nw_ref, o_ref, s_out_ref, s_scr):
    c = GDN_CHUNK
    n = pl.program_id(1)

    @pl.when(n == 0)
    def _():
        s_scr[...] = jnp.zeros_like(s_scr)

    masks = _chunk_masks(c, c)
    nw = nw_ref[...]
    chains = [(b, h) for b in range(qg_ref.shape[0]) for h in range(GDN_HEADS)]
    hs = lambda h: slice(h * GDN_DK, (h + 1) * GDN_DK)
    col = lambda off: [gcb_ref[b, :, off + h:off + h + 1] for b, h in chains]
    gcs, betas, gls = col(0), col(GDN_HEADS), col(2 * GDN_HEADS)
    qs = [qg_ref[b, :, hs(h)] for b, h in chains]
    ks = [kg_ref[b, :, hs(h)] for b, h in chains]
    vs = [vg_ref[b, :, hs(h)] for b, h in chains]
    us, ws, attns, q_decs, k_decs = _gdn_intra(qs, ks, vs, gcs, gls, betas, *masks)
    ss = [s_scr[b, h] for b, h in chains]
    wqs = _each(lambda w, qd, s: _bdot(jnp.concatenate([w, qd], axis=0), s), ws, q_decs, ss)
    wss = _each(lambda wq: wq[:c], wqs)
    qss = _each(lambda wq: wq[c:], wqs)
    v_news = _each(lambda u, x: u - x, us, wss)
    avs = _each(_bdot, attns, v_news)
    kvs = _each(_bdot_tn, k_decs, v_news)
    for (b, h), s, gl, qsv, av, kv in zip(chains, ss, gls, qss, avs, kvs):
        s_scr[b, h] = s * jnp.exp(gl[0:1, :]) + kv
        o_ref[b, :, hs(h)] = _gated_rms(qsv + av, z_ref[b, :, hs(h)], nw)

    @pl.when(n == pl.num_programs(1) - 1)
    def _():
        s_out_ref[...] = s_scr[...]


def _gdn_prompt(qg, kg, vg, z, gcb, norm_w, n_seq):
    n = qg.shape[0]
    s_len = n // n_seq
    nb = min(GDN_SEQ_PER_STEP, n_seq)
    v3 = lambda a: a.reshape(n_seq, s_len, a.shape[-1])
    tok = lambda w: pl.BlockSpec((nb, GDN_CHUNK, w), lambda b, i: (b, i, 0))
    o, s = pl.pallas_call(
        _gdn_prompt_kernel,
        out_shape=[jax.ShapeDtypeStruct((n_seq, s_len, Z_COLS), F32),
                   jax.ShapeDtypeStruct((n_seq, GDN_HEADS, GDN_DK, GDN_DV), F32)],
        grid=(n_seq // nb, s_len // GDN_CHUNK),
        in_specs=[tok(QK_COLS), tok(QK_COLS), tok(Z_COLS), tok(Z_COLS), tok(LANES),
                  pl.BlockSpec((1, GDN_DV), lambda b, i: (0, 0))],
        out_specs=[tok(Z_COLS),
                   pl.BlockSpec((nb, GDN_HEADS, GDN_DK, GDN_DV), lambda b, i: (b, 0, 0, 0))],
        scratch_shapes=[pltpu.VMEM((nb, GDN_HEADS, GDN_DK, GDN_DV), F32)],
        compiler_params=_cparams(("arbitrary", "arbitrary")), name="gdn_prompt",
    )(v3(qg), v3(kg), v3(vg), v3(z), v3(gcb), norm_w)
    return o.reshape(n, Z_COLS), s


def _gdn_sample_kernel(qg_ref, kg_ref, vg_ref, z_ref, gcb_ref, nw_ref, s_in_ref, o_ref, s_out_ref):
    c = GDN_CHUNK
    n_sub = c // SAMPLE_SLOTS
    masks = _chunk_masks(c, SAMPLE_SLOTS)
    heads = range(GDN_HEADS)
    hs = lambda h: slice(h * GDN_DK, (h + 1) * GDN_DK)
    rs = lambda s: slice(s * SAMPLE_SLOTS, (s + 1) * SAMPLE_SLOTS)
    col = lambda off: [gcb_ref[:, off + h:off + h + 1] for h in heads]
    gcs, betas, gls = col(0), col(GDN_HEADS), col(2 * GDN_HEADS)
    us, ws, attns, q_decs, k_decs = _gdn_intra([qg_ref[:, hs(h)] for h in heads],
                                               [kg_ref[:, hs(h)] for h in heads],
                                               [vg_ref[:, hs(h)] for h in heads], gcs, gls, betas, *masks,
                                               base_only=SAMPLE_SLOTS <= INV_BASE)
    pairs = [(h, s) for h in heads for s in range(n_sub)]
    sts = [s_in_ref[s, h] for h, s in pairs]
    boths = [jnp.concatenate([ws[h][rs(s), :], q_decs[h][rs(s), :]], axis=0) for h, s in pairs]
    rr = _each(_bdot, boths, sts)
    gather = lambda h, part: jnp.concatenate(
        [rr[h * n_sub + s][part * SAMPLE_SLOTS:(part + 1) * SAMPLE_SLOTS, :] for s in range(n_sub)], axis=0)
    v_news = [us[h] - gather(h, 0) for h in heads]
    avs = _each(_bdot, attns, v_news)
    row = lax.broadcasted_iota(I32, (c, LANES), 0)
    seq_of_row = _div_pow2(row, SAMPLE_SLOTS)
    kds = [jnp.where(seq_of_row == s, k_decs[h], 0.0) for h, s in pairs]
    kvs = _each(_bdot_tn, kds, [v_news[h] for h, _ in pairs])
    egls = _each(jnp.exp, gls)
    for (h, s), st, kv in zip(pairs, sts, kvs):
        s_out_ref[s, h] = st * egls[h][s * SAMPLE_SLOTS:s * SAMPLE_SLOTS + 1, :] + kv
    nw = nw_ref[...]
    for h in heads:
        o_ref[:, hs(h)] = _gated_rms(gather(h, 1) + avs[h], z_ref[:, hs(h)], nw)


def _gdn_sample(qg, kg, vg, z, gcb, norm_w, state):
    n = qg.shape[0]
    n_sub = GDN_CHUNK // SAMPLE_SLOTS
    tok = lambda w: pl.BlockSpec((GDN_CHUNK, w), lambda i: (i, 0))
    st = pl.BlockSpec((n_sub, GDN_HEADS, GDN_DK, GDN_DV), lambda i: (i, 0, 0, 0))
    return pl.pallas_call(
        _gdn_sample_kernel,
        out_shape=[jax.ShapeDtypeStruct((n, Z_COLS), F32),
                   jax.ShapeDtypeStruct(state.shape, F32)],
        grid=(n // GDN_CHUNK,),
        in_specs=[tok(QK_COLS), tok(QK_COLS), tok(Z_COLS), tok(Z_COLS), tok(LANES),
                  pl.BlockSpec((1, GDN_DV), lambda i: (0, 0)), st],
        out_specs=[tok(Z_COLS), st],
        compiler_params=_cparams(("arbitrary",)), name="gdn_sample",
    )(qg, kg, vg, z, gcb, norm_w, state)


def _post_kernel(a_ref, g_ref, x_ref, wo_ref, ln_g_ref, ln_b_ref, wr_ref, x1_ref, route_ref, *, tm):
    sub = POST_SUB
    rows = [slice(j * sub, (j + 1) * sub) for j in range(tm // sub)]
    d = lambda a, b: jnp.dot(a, b, preferred_element_type=F32)
    mixes = [d(a_ref[r, :].astype(BF16), wo_ref[0:Q_COLS, :]) + d(g_ref[r, :].astype(BF16), wo_ref[Q_COLS:, :])
             for r in rows]
    x1s = [_layer_norm(DEEPNORM_ALPHA * x_ref[r, :] + mix, ln_g_ref[...], ln_b_ref[...])
           for r, mix in zip(rows, mixes)]
    for r, x1 in zip(rows, x1s):
        x1_ref[r, :] = x1
    w2 = wr_ref[...]
    lgs = []
    for x1 in x1s:
        xh = x1.astype(BF16)
        xm = (x1 - xh.astype(F32)).astype(BF16)
        both = d(xh, w2)
        lgs.append(both[:, :LANES] + both[:, LANES:] + d(xm, w2[:, :LANES]))
    for r, lg in zip(rows, lgs):
        route_ref[:, r] = jnp.transpose(_route(lg))[0:SUBLANES, :]


def _route(lg):
    lane = lax.broadcasted_iota(I32, lg.shape, 1)
    lane_f = lane.astype(F32)
    big = float(LANES)

    def first_max(vals, mask):
        v = jnp.where(mask, vals, NEG_BIG)
        mx = jnp.max(v, axis=1, keepdims=True)
        idx = jnp.min(jnp.where(mask & (v == mx), lane_f, big), axis=1, keepdims=True)
        return mx, idx

    gmask = lane < N_GROUPS
    gmax, gidx = first_max(lg, gmask)
    gden = jnp.sum(jnp.where(gmask, jnp.exp(lg - gmax), 0.0), axis=1, keepdims=True)
    g_top_p = 1.0 / gden
    e_lane = lane - N_GROUPS
    e_group = _div_pow2(jnp.maximum(e_lane, 0), EXPERTS_PER_GROUP).astype(F32)
    emask = (e_lane >= 0) & (e_lane < N_EXPERTS) & (e_group == gidx)
    m1, i1 = first_max(lg, emask)
    eden = jnp.sum(jnp.where(emask, jnp.exp(lg - m1), 0.0), axis=1, keepdims=True)
    m2, i2 = first_max(lg, emask & (lane_f != i1))
    p1 = 1.0 / eden
    p2 = jnp.exp(m2 - m1) / eden
    tot = p1 + p2
    gate1 = g_top_p * (p1 / tot)
    gate2 = g_top_p * (p2 / tot)
    return jnp.where(lane == 0, gate1,
                     jnp.where(lane == 1, gate2,
                               jnp.where(lane == 2, i1 - N_GROUPS,
                                         jnp.where(lane == 3, i2 - N_GROUPS, 0.0))))


def _post(attn_o, gdn_o, x, wts):
    n = x.shape[0]
    tm = min(POST_TILE, n)
    tok = lambda w: pl.BlockSpec((tm, w), lambda i: (i, 0))
    const = lambda a: pl.BlockSpec(a.shape, lambda i: (0,) * a.ndim)
    consts = [wts['wo'], wts['ln1_g'], wts['ln1_b'], wts['wr']]
    return pl.pallas_call(
        functools.partial(_post_kernel, tm=tm),
        out_shape=[jax.ShapeDtypeStruct((n, D_MODEL), F32), jax.ShapeDtypeStruct((SUBLANES, n), F32)],
        grid=(n // tm,),
        in_specs=[tok(Q_COLS), tok(Z_COLS), tok(D_MODEL)] + [const(a) for a in consts],
        out_specs=[tok(D_MODEL), pl.BlockSpec((SUBLANES, tm), lambda i: (0, i))],
        compiler_params=_cparams(("arbitrary",)), name="post_%d" % (n // tm),
    )(attn_o, gdn_o, x, *consts)


def _slab_loop(n, body):
    n_main = jnp.right_shift(n, int(math.log2(SLAB_UNROLL)))

    def main(i, c):
        for u in range(SLAB_UNROLL):
            body(i * SLAB_UNROLL + u, u)
        return c

    lax.fori_loop(0, n_main, main, 0)
    lax.fori_loop(n_main * SLAB_UNROLL, n, lambda j, c: (body(j, 0), c)[1], 0)


def _dispatch_kernel(dst_ref, nslab_ref, ztab_ref, zinfo_ref, slot_ref, gate_ref, *rest,
                     group_tiles, max_tiles):
    x_refs = rest[:len(group_tiles)]
    xs_ref, pbuf, sem, zbuf, zsem = rest[len(group_tiles):]
    n_tiles = sum(group_tiles)
    g = pl.program_id(0)
    cur = lax.rem(g, 2)

    def slab_copy(tile, buf_slot, j):
        d = pl.multiple_of(dst_ref[tile * PERM_SLABS + j], SLAB)
        src = pbuf.at[buf_slot, pl.ds(pl.multiple_of(j * SLAB, SLAB), SLAB), :]
        return pltpu.make_async_copy(src, xs_ref.at[pl.ds(d, SLAB), :], sem.at[buf_slot])

    def tail_copy(k):
        d = pl.multiple_of(ztab_ref[k], SLAB)
        return pltpu.make_async_copy(zbuf.at[pl.ds(0, SLAB), :], xs_ref.at[pl.ds(d, SLAB), :], zsem)

    def tile_copy(t):
        d = pl.multiple_of(t * ROW_TILE, ROW_TILE)
        return pltpu.make_async_copy(zbuf, xs_ref.at[pl.ds(d, ROW_TILE), :], zsem)

    @pl.when(g == 0)
    def _():
        zbuf[...] = jnp.zeros_like(zbuf)

    share = -(-ZERO_TABLE // n_tiles)
    k0 = g * share
    _slab_loop(jnp.clip(zinfo_ref[0] - k0, 0, share), lambda j, u: tail_copy(k0 + j).start(priority=1))

    @pl.when(zinfo_ref[1] + g < max_tiles)
    def _():
        tile_copy(zinfo_ref[1] + g).start(priority=1)

    x = x_refs[-1][...]
    bound = n_tiles
    for x_ref, nt in zip(x_refs[-2::-1], group_tiles[:0:-1]):
        bound -= nt
        x = jnp.where(g < bound, x_ref[...], x)

    r = lax.broadcasted_iota(I32, (PERM_ROWS, TOK_TILE), 0)
    sl = slot_ref[0]
    hit0, hit1 = r == sl[0:1, :], r == sl[1:2, :]
    onehot = jnp.where(hit0 | hit1, 1.0, 0.0).astype(BF16)
    gt = gate_ref[0]
    gcol = jnp.sum(jnp.where(hit0, gt[0:1, :], 0.0) + jnp.where(hit1, gt[1:2, :], 0.0),
                   axis=1, keepdims=True)
    pbuf[cur, :, 0:D_MODEL] = jnp.dot(onehot, x.astype(BF16), preferred_element_type=F32).astype(BF16)
    g_hi = gcol.astype(BF16).astype(F32)
    lane = lax.broadcasted_iota(I32, (PERM_ROWS, LANES), 1)
    pbuf[cur, :, D_MODEL:] = jnp.where(lane < LANES // 2, g_hi, gcol - g_hi).astype(BF16)

    @pl.when(g > 0)
    def _():
        _slab_loop(nslab_ref[g - 1], lambda j, u: slab_copy(g - 1, 1 - cur, j).wait())

    _slab_loop(nslab_ref[g], lambda j, u: slab_copy(g, cur, j).start(priority=u % 2))

    @pl.when(g == n_tiles - 1)
    def _():
        _slab_loop(nslab_ref[g], lambda j, u: slab_copy(g, cur, j).wait())
        lax.fori_loop(zinfo_ref[1] + n_tiles, max_tiles,
                      lambda t, c: (tile_copy(t).start(priority=1), c)[1], 0)
        _slab_loop(zinfo_ref[0], lambda k, u: tail_copy(k).wait())
        lax.fori_loop(zinfo_ref[1], max_tiles, lambda t, c: (tile_copy(t).wait(), c)[1], 0)


def _dispatch(plan, x1s, max_tiles):
    group_tiles = tuple(x1.shape[0] // TOK_TILE for x1 in x1s)
    n_tiles = sum(group_tiles)
    tile = lambda i, d, ns, zt, zi: (i, 0, 0)
    in_specs = [pl.BlockSpec((1, TOP_K, TOK_TILE), tile), pl.BlockSpec((1, TOP_K, TOK_TILE), tile)]
    base = 0
    for nt in group_tiles:
        in_specs.append(pl.BlockSpec(
            (TOK_TILE, D_MODEL),
            lambda i, d, ns, zt, zi, base=base, nt=nt: (jnp.clip(i - base, 0, nt - 1), 0)))
        base += nt
    return pl.pallas_call(
        functools.partial(_dispatch_kernel, group_tiles=group_tiles, max_tiles=max_tiles),
        out_shape=jax.ShapeDtypeStruct((max_tiles * ROW_TILE, XS_WORDS), BF16),
        grid_spec=pltpu.PrefetchScalarGridSpec(
            num_scalar_prefetch=4, grid=(n_tiles,), in_specs=in_specs,
            out_specs=pl.BlockSpec(memory_space=pl.ANY),
            scratch_shapes=[pltpu.VMEM((2, PERM_ROWS, XS_WORDS), BF16), pltpu.SemaphoreType.DMA((2,)),
                            pltpu.VMEM((ROW_TILE, XS_WORDS), BF16), pltpu.SemaphoreType.DMA(())]),
        compiler_params=_cparams(("arbitrary",)), name="moe_dispatch",
    )(plan['slab_dst'], plan['nslab'], plan['ztab'], plan['zinfo'], plan['slot_rows'], plan['gate_rows'],
      *x1s)


def _expert_kernel(t0_ref, nt_ref, nu_ref, wg_ref, wu_ref, wd_ref, xs_ref, ye_ref,
                   xbuf, ybuf, wgu_scr, wd_scr, zbuf, in_sem, out_sem, zsem, *, max_tiles):
    e = pl.program_id(0)
    n_used = nu_ref[0]
    ahead = EXPERT_IN_SLOTS - 1

    def in_copy(t):
        slot = lax.rem(t, EXPERT_IN_SLOTS)
        src = xs_ref.at[pl.ds(pl.multiple_of(t * ROW_TILE, ROW_TILE), ROW_TILE), :]
        return pltpu.make_async_copy(src, xbuf.at[slot], in_sem.at[slot])

    def out_copy(t):
        slot = lax.rem(t, EXPERT_OUT_SLOTS)
        dst = ye_ref.at[pl.ds(pl.multiple_of(t * ROW_TILE, ROW_TILE), ROW_TILE), :]
        return pltpu.make_async_copy(ybuf.at[slot], dst, out_sem.at[slot])

    def zero_copy(t):
        dst = ye_ref.at[pl.ds(pl.multiple_of(t * ROW_TILE, ROW_TILE), ROW_TILE), :]
        return pltpu.make_async_copy(zbuf, dst, zsem)

    @pl.when(e == 0)
    def _():
        for t in range(ahead):
            @pl.when(t < n_used)
            def _():
                in_copy(t).start()

        zbuf[...] = jnp.zeros_like(zbuf)
        lax.fori_loop(n_used, max_tiles, lambda t, c: (zero_copy(t).start(priority=1), c)[1], 0)

    wgu_scr[:, 0:EXPERT_FF] = wg_ref[0].astype(BF16)
    wgu_scr[:, EXPERT_FF:] = wu_ref[0].astype(BF16)
    wd_scr[...] = wd_ref[0].astype(BF16)

    sub = EXPERT_SUB
    rows = [slice(j * sub, (j + 1) * sub) for j in range(ROW_TILE // sub)]
    d = lambda a, b: jnp.dot(a, b, preferred_element_type=F32)

    def tile_body(j, carry):
        t = t0_ref[e] + j
        slot = lax.rem(t, EXPERT_OUT_SLOTS)
        in_slot = lax.rem(t, EXPERT_IN_SLOTS)

        @pl.when(t + ahead < n_used)
        def _():
            in_copy(t + ahead).start()

        in_copy(t).wait()

        @pl.when(t >= EXPERT_OUT_SLOTS)
        def _():
            out_copy(t - EXPERT_OUT_SLOTS).wait()

        wgu, wd = wgu_scr[...], wd_scr[...]
        xs = [xbuf[in_slot, r, 0:D_MODEL] for r in rows]
        hs = [d(x, wgu) for x in xs]
        hhs = [(_silu(h[:, :EXPERT_FF]) * h[:, EXPERT_FF:]).astype(BF16) for h in hs]
        ys = [d(hh, wd) for hh in hhs]
        for r, y in zip(rows, ys):
            parts = xbuf[in_slot, r, D_MODEL:].astype(F32)
            gate = parts + pltpu.roll(parts, LANES // 2, axis=1)
            ybuf[slot, r, :] = (y * jnp.concatenate([gate] * (D_MODEL // LANES), axis=1)).astype(BF16)
        out_copy(t).start(priority=1)
        return carry

    lax.fori_loop(0, nt_ref[e], tile_body, 0)

    @pl.when(e == pl.num_programs(0) - 1)
    def _():
        for back in range(EXPERT_OUT_SLOTS, 0, -1):
            @pl.when(n_used >= back)
            def _():
                out_copy(n_used - back).wait()
        lax.fori_loop(n_used, max_tiles, lambda t, c: (zero_copy(t).wait(), c)[1], 0)


def _experts(plan, xs, w_gate, w_up, w_down):
    max_tiles = xs.shape[0] // ROW_TILE
    wsel = lambda e, t0, nt, nu: (e, 0, 0)
    return pl.pallas_call(
        functools.partial(_expert_kernel, max_tiles=max_tiles),
        out_shape=jax.ShapeDtypeStruct((xs.shape[0], D_MODEL), BF16),
        grid_spec=pltpu.PrefetchScalarGridSpec(
            num_scalar_prefetch=3, grid=(N_EXPERTS,),
            in_specs=[pl.BlockSpec((1, D_MODEL, EXPERT_FF), wsel),
                      pl.BlockSpec((1, D_MODEL, EXPERT_FF), wsel),
                      pl.BlockSpec((1, EXPERT_FF, D_MODEL), wsel),
                      pl.BlockSpec(memory_space=pl.ANY)],
            out_specs=pl.BlockSpec(memory_space=pl.ANY),
            scratch_shapes=[pltpu.VMEM((EXPERT_IN_SLOTS, ROW_TILE, XS_WORDS), BF16),
                            pltpu.VMEM((EXPERT_OUT_SLOTS, ROW_TILE, D_MODEL), BF16),
                            pltpu.VMEM((D_MODEL, 2 * EXPERT_FF), BF16),
                            pltpu.VMEM((EXPERT_FF, D_MODEL), BF16),
                            pltpu.VMEM((ROW_TILE, D_MODEL), BF16),
                            pltpu.SemaphoreType.DMA((EXPERT_IN_SLOTS,)),
                            pltpu.SemaphoreType.DMA((EXPERT_OUT_SLOTS,)),
                            pltpu.SemaphoreType.DMA(())]),
        compiler_params=_cparams(("arbitrary",)), name="moe_experts",
    )(plan['tile_start'], plan['tile_count'], plan['n_used'], w_gate, w_up, w_down, xs)


def _combine_kernel(dst_ref, nslab_ref, x1_ref, slot_ref, ye_ref, ln_g_ref, ln_b_ref, y_ref,
                    buf, sem, *, tile_base, n_tiles):
    i = pl.program_id(0)
    g = tile_base + i
    cur = lax.rem(i, 2)

    def slab_copy(tile, buf_slot, j):
        d = pl.multiple_of(dst_ref[tile * PERM_SLABS + j], SLAB)
        dst = buf.at[buf_slot, pl.ds(pl.multiple_of(j * SLAB, SLAB), SLAB), :]
        return pltpu.make_async_copy(ye_ref.at[pl.ds(d, SLAB), :], dst, sem.at[buf_slot])

    @pl.when(i == 0)
    def _():
        buf[...] = jnp.zeros_like(buf)
        _slab_loop(nslab_ref[g], lambda j, u: slab_copy(g, cur, j).start(priority=u % 2))

    @pl.when(i + 1 < n_tiles)
    def _():
        _slab_loop(nslab_ref[g + 1], lambda j, u: slab_copy(g + 1, 1 - cur, j).start(priority=u % 2))

    _slab_loop(nslab_ref[g], lambda j, u: slab_copy(g, cur, j).wait())

    sub = COMBINE_SUB
    subs = [slice(j * sub, (j + 1) * sub) for j in range(TOK_TILE // sub)]
    col = lax.broadcasted_iota(I32, (sub, PERM_ROWS), 1)
    sl = slot_ref[0]
    diag = (lax.broadcasted_iota(I32, (sub, sub), 0) == lax.broadcasted_iota(I32, (sub, sub), 1))
    as_col = lambda row: jnp.sum(jnp.where(diag, row, 0), axis=1, keepdims=True)
    picks = [jnp.where((col == as_col(sl[0:1, r])) | (col == as_col(sl[1:2, r])), 1.0, 0.0).astype(BF16)
             for r in subs]
    rows = buf[cur]
    moes = [jnp.dot(pick, rows, preferred_element_type=F32) for pick in picks]
    for r, moe in zip(subs, moes):
        y_ref[r, :] = _layer_norm(DEEPNORM_ALPHA * x1_ref[r, :] + moe, ln_g_ref[...], ln_b_ref[...])


def _combine(plan, tile_base, x1, ye, ln_g, ln_b):
    n = x1.shape[0]
    n_tiles = n // TOK_TILE
    tok = lambda w: pl.BlockSpec((TOK_TILE, w), lambda i, d, ns: (i, 0))
    const = lambda a: pl.BlockSpec(a.shape, lambda i, d, ns: (0,) * a.ndim)
    return pl.pallas_call(
        functools.partial(_combine_kernel, tile_base=tile_base, n_tiles=n_tiles),
        out_shape=jax.ShapeDtypeStruct((n, D_MODEL), F32),
        grid_spec=pltpu.PrefetchScalarGridSpec(
            num_scalar_prefetch=2, grid=(n_tiles,),
            in_specs=[tok(D_MODEL),
                      pl.BlockSpec((1, TOP_K, TOK_TILE), lambda i, d, ns: (tile_base + i, 0, 0)),
                      pl.BlockSpec(memory_space=pl.ANY), const(ln_g), const(ln_b)],
            out_specs=tok(D_MODEL),
            scratch_shapes=[pltpu.VMEM((2, PERM_ROWS, D_MODEL), BF16), pltpu.SemaphoreType.DMA((2,))]),
        compiler_params=_cparams(("arbitrary",)), name="moe_combine_%d" % tile_base,
    )(plan['slab_dst'], plan['nslab'], x1, plan['slot_rows'], ye, ln_g, ln_b)


def _routing_plan(ids, gates):
    nt = ids.shape[1] // TOK_TILE
    pairs = TOP_K * TOK_TILE
    ex = jnp.arange(N_EXPERTS, dtype=I32)
    per_tile = lambda a: jnp.swapaxes(a.reshape(TOP_K, nt, TOK_TILE), 0, 1)
    flat = per_tile(ids).reshape(nt, pairs)
    onehot = (flat[:, None, :] == ex[None, :, None])
    p = np.arange(pairs)
    triu = jnp.asarray(p[:, None] <= p[None, :], BF16)
    csum = jnp.dot(onehot.astype(BF16).reshape(nt * N_EXPERTS, pairs), triu,
                   preferred_element_type=F32).astype(I32).reshape(nt, N_EXPERTS, pairs)
    oh = onehot.astype(I32)
    rank = jnp.sum(oh * (csum - 1), axis=1)
    cnt = csum[:, :, -1]
    cpad = (cnt + SLAB - 1) // SLAB * SLAB
    seg_end = jnp.cumsum(cpad, axis=1)
    seg_off = seg_end - cpad
    slot = jnp.sum(oh * seg_off[:, :, None], axis=1) + rank
    run_end = jnp.cumsum(cpad, axis=0)
    ntiles_e = (run_end[-1] + ROW_TILE - 1) // ROW_TILE
    tile_end = jnp.cumsum(ntiles_e)
    dst_run = ((tile_end - ntiles_e) * ROW_TILE)[None, :] + run_end - cpad
    j8 = jnp.arange(PERM_SLABS, dtype=I32) * SLAB
    e_of = jnp.minimum(jnp.sum((j8[None, :, None] >= seg_end[:, None, :]).astype(I32), axis=2),
                       N_EXPERTS - 1)
    sel = (e_of[:, :, None] == ex).astype(I32)
    slab_dst = jnp.sum(sel * (dst_run - seg_off)[:, None, :], axis=2) + j8[None, :]
    n_used = tile_end[-1]
    row_start = (tile_end - ntiles_e) * ROW_TILE
    tail_cnt = (ntiles_e * ROW_TILE - run_end[-1]) // SLAB
    tail_end = jnp.cumsum(tail_cnt)
    k = jnp.arange(ZERO_TABLE, dtype=I32)
    e_k = jnp.minimum(jnp.sum((k[:, None] >= tail_end[None, :]).astype(I32), axis=1), N_EXPERTS - 1)
    base_k = jnp.sum((e_k[:, None] == ex).astype(I32)
                     * (row_start + run_end[-1] - SLAB * (tail_end - tail_cnt))[None, :], axis=1)
    return dict(
        slab_dst=slab_dst.reshape(-1).astype(I32), nslab=(seg_end[:, -1] // SLAB).astype(I32),
        ztab=(base_k + SLAB * k).astype(I32), zinfo=jnp.stack([tail_end[-1], n_used]).astype(I32),
        slot_rows=slot.reshape(nt, TOP_K, TOK_TILE).astype(I32),
        gate_rows=per_tile(gates).astype(F32),
        tile_start=(tile_end - ntiles_e).astype(I32), tile_count=ntiles_e.astype(I32),
        n_used=n_used.reshape(1).astype(I32))


def _max_row_tiles(n_tokens):
    rows = TOP_K * n_tokens + (n_tokens // TOK_TILE) * N_EXPERTS * (SLAB - 1)
    return (rows + ROW_TILE - 1) // ROW_TILE + N_EXPERTS


def _moe(x1s, routes, wts):
    ids = jnp.concatenate([r[TOP_K:2 * TOP_K, :] for r in routes], axis=1).astype(I32)
    gates = jnp.concatenate([r[0:TOP_K, :] for r in routes], axis=1)
    plan = _routing_plan(ids, gates)
    max_tiles = _max_row_tiles(ids.shape[1])
    bases = [0]
    for x1 in x1s[:-1]:
        bases.append(bases[-1] + x1.shape[0] // TOK_TILE)
    xs = _dispatch(plan, x1s, max_tiles)
    ye = _experts(plan, xs, wts['w_gate'], wts['w_up'], wts['w_down'])
    return [_combine(plan, base, x1, ye, wts['ln2_g'], wts['ln2_b']) for base, x1 in zip(bases, x1s)]


def _prep_weights(w_in, w_out, conv_w, a_log, dt_bias, gdn_norm_w, ln1_g, ln1_b, w_router_group,
                  w_router_expert, w_gate, w_up, w_down, ln2_g, ln2_b):
    pad_row = lambda v: jnp.pad(v.astype(F32), (0, LANES - v.shape[0]))[None, :]
    wr = jnp.pad(jnp.concatenate([w_router_group, w_router_expert], axis=1),
                 ((0, 0), (0, LANES - N_GROUPS - N_EXPERTS)))
    wr_hi = wr.astype(BF16)
    wr_mid = (wr - wr_hi.astype(F32)).astype(BF16)
    group = ATTN_HEADS // ATTN_KV_HEADS
    wq = w_in[:, :Q_COLS].reshape(D_MODEL, ATTN_KV_HEADS, group, HEAD_DIM)
    wq = jnp.swapaxes(wq, 1, 2).reshape(D_MODEL, Q_COLS)
    w_all = jnp.concatenate([wq, w_in[:, Q_COLS:]], axis=1)
    w_all = jnp.pad(w_all, ((0, 0), (0, IN_COLS_PAD - w_all.shape[1]))).astype(BF16)
    wo_q = w_out[:Q_COLS].reshape(ATTN_KV_HEADS, group, HEAD_DIM, D_MODEL)
    wo_q = jnp.swapaxes(wo_q, 0, 1).reshape(Q_COLS, D_MODEL)
    wo = jnp.concatenate([wo_q, w_out[Q_COLS:]], axis=0)
    return dict(
        w_all=w_all, convw=conv_w.astype(F32), alog=pad_row(a_log), dtb=pad_row(dt_bias),
        norm_w=gdn_norm_w.astype(F32)[None, :], wo=wo.astype(BF16),
        ln1_g=ln1_g[None, :], ln1_b=ln1_b[None, :], wr=jnp.concatenate([wr_hi, wr_mid], axis=1),
        w_gate=w_gate, w_up=w_up, w_down=w_down, ln2_g=ln2_g[None, :], ln2_b=ln2_b[None, :])


def _layer(x_prompt, x_sample, cache_k, cache_v, state_gdn, state_conv, wts):
    bp, sp, _ = x_prompt.shape
    bs, ts, _ = x_sample.shape
    n_p = bp * sp

    xp = x_prompt.reshape(n_p, D_MODEL)
    (q, k, v, qg, kg, vg, z, gcb, utail) = _proj(xp, np.arange(sp), wts, GDN_CHUNK, bp)
    attn_p = _attn_prompt(q, k, v, wts['sinks'], bp)
    gdn_p, s_p = _gdn_prompt(qg, kg, vg, z, gcb, wts['norm_w'], bp)
    last_win = lambda a: a.reshape(bp, sp, KV_COLS)[:, sp - WINDOW:].reshape(bp, WINDOW, ATTN_KV_HEADS,
                                                                            HEAD_DIM)
    new_k_p, new_v_p = last_win(k), last_win(v)
    tiles_per_seq = sp // min(PROJ_TILE, sp)
    conv_p = utail.reshape(bp, tiles_per_seq, SUBLANES, CONV_DIM)[:, -1, SUBLANES - (CONV_W - 1):]

    lo, hi = SAMPLE_FIRST, SAMPLE_FIRST + ts
    xs_rows = jnp.pad(x_sample, ((0, 0), (lo, SAMPLE_SLOTS - hi), (0, 0))).reshape(bs * SAMPLE_SLOTS, D_MODEL)
    hist = jnp.pad(state_conv, ((0, 0), (0, SAMPLE_SLOTS - lo), (0, 0))).reshape(bs * SAMPLE_SLOTS, CONV_DIM)
    slot = np.arange(SAMPLE_SLOTS)
    valid = jnp.asarray(np.tile((slot >= lo) & (slot < hi), bs)[:, None], F32)
    pos_s = np.tile(PAST_LEN + slot - lo, bs)
    (q, k, v, qg, kg, vg, z, gcb, u_s) = _proj(xs_rows, pos_s, wts, SAMPLE_SLOTS, 1, hist, valid)
    ck = cache_k.reshape(bs, WINDOW, KV_COLS)
    cv = cache_v.reshape(bs, WINDOW, KV_COLS)
    attn_s, kwin, vwin = _attn_sample(q, k, v, ck, cv, wts['sinks'], bs, ts)
    gdn_s, s_s = _gdn_sample(qg, kg, vg, z, gcb, wts['norm_w'], state_gdn)
    real = lambda a: a.reshape(bs, SAMPLE_SLOTS, -1)[:, lo:hi]
    new_k_s = kwin.reshape(bs, WINDOW, ATTN_KV_HEADS, HEAD_DIM)
    new_v_s = vwin.reshape(bs, WINDOW, ATTN_KV_HEADS, HEAD_DIM)
    conv_s = u_s.reshape(bs, SAMPLE_SLOTS, CONV_DIM)[:, hi - (CONV_W - 1):hi]

    x1_p, route_p = _post(attn_p, gdn_p, xp, wts)
    x1_s, route_s = _post(real(attn_s).reshape(bs * ts, Q_COLS), real(gdn_s).reshape(bs * ts, Z_COLS),
                          x_sample.reshape(bs * ts, D_MODEL), wts)
    y_p, y_s = _moe([x1_p, x1_s], [route_p, route_s], wts)
    return (y_p.reshape(bp, sp, D_MODEL), y_s.reshape(bs, ts, D_MODEL), new_k_p, new_v_p, s_p, conv_p,
            new_k_s, new_v_s, s_s, conv_s)


def kernel(x_prompt, x_sample, cache_attn_k, cache_attn_v, state_gdn, state_conv, w_in, w_out,
           attn_sinks, conv_w, a_log, dt_bias, gdn_norm_w, ln1_g, ln1_b, w_router_group,
           w_router_expert, w_gate, w_up, w_down, ln2_g, ln2_b):
    assert w_in.shape[0] == DEPTH
    l = 0
    wts = _prep_weights(w_in[l], w_out[l], conv_w[l], a_log[l], dt_bias[l], gdn_norm_w[l], ln1_g[l],
                        ln1_b[l], w_router_group[l], w_router_expert[l], w_gate[l], w_up[l],
                        w_down[l], ln2_g[l], ln2_b[l])
    wts['sinks'] = attn_sinks[l]
    outs = _layer(x_prompt, x_sample, cache_attn_k[l], cache_attn_v[l], state_gdn[l], state_conv[l], wts)
    (y_p, y_s, k_p, v_p, s_p, c_p, k_s, v_s, s_s, c_s) = outs
    add = lambda a: a[None]
    return (y_p, y_s, add(k_p), add(v_p), add(s_p), add(c_p), add(k_s), add(v_s), add(s_s), add(c_s))
```

```python
import functools
import math

import jax
import jax.numpy as jnp
import numpy as np
from jax import lax
from jax.experimental import pallas as pl
from jax.experimental.pallas import tpu as pltpu

F32 = jnp.float32
BF16 = jnp.bfloat16
I32 = jnp.int32

D_MODEL = 1024
ATTN_HEADS = 8
ATTN_KV_HEADS = 2
HEAD_DIM = 64
WINDOW = 128
ROT_DIM = HEAD_DIM // 4
ROPE_THETA = 500000.0
GDN_HEADS = 4
GDN_DK = 128
GDN_DV = 128
CONV_W = 4
QK_COLS = GDN_HEADS * GDN_DK
CONV_DIM = 2 * QK_COLS + GDN_HEADS * GDN_DV
Z_COLS = GDN_HEADS * GDN_DV
Q_COLS = ATTN_HEADS * HEAD_DIM
KV_COLS = ATTN_KV_HEADS * HEAD_DIM
N_GROUPS = 4
EXPERTS_PER_GROUP = 8
N_EXPERTS = N_GROUPS * EXPERTS_PER_GROUP
TOP_K = 2
EXPERT_FF = 256
NORM_EPS = 1e-5
L2_EPS = 1e-6
DEPTH = 1
DEEPNORM_ALPHA = (2 * DEPTH) ** 0.25
PAST_LEN = 8192

LANES = 128
SUBLANES = 8
IN_SPLITS = (0, Q_COLS + 2 * KV_COLS, Q_COLS + 2 * KV_COLS + CONV_DIM,
             Q_COLS + 2 * KV_COLS + CONV_DIM + Z_COLS, Q_COLS + 2 * KV_COLS + CONV_DIM + Z_COLS + LANES)
IN_COLS_PAD = IN_SPLITS[-1]
TOK_TILE = 512
PROJ_TILE = 512
PROJ_SUB = 128
POST_TILE = 1024
POST_SUB = 256
GDN_CHUNK = 128
GDN_SEQ_PER_STEP = 4
ATTN_BLOCKS_PER_STEP = 4
ATTN_SEQS_PER_STEP = 16
INV_BASE = 16
SAMPLE_SLOTS = 8
SAMPLE_FIRST = CONV_W - 1
ROW_TILE = 512
EXPERT_SUB = 256
EXPERT_IN_SLOTS = 4
EXPERT_OUT_SLOTS = 3
SLAB_UNROLL = 4
COMBINE_SUB = 128
SLAB = 16
PERM_ROWS = TOP_K * TOK_TILE + N_EXPERTS * SLAB
PERM_SLABS = PERM_ROWS // SLAB
XS_WORDS = D_MODEL + LANES
ZERO_TABLE = N_EXPERTS * (ROW_TILE // SLAB)
VMEM_LIMIT = 48 * 1024 * 1024
NEG_BIG = -1e30


def _cparams(sem):
    return pltpu.CompilerParams(dimension_semantics=sem, vmem_limit_bytes=VMEM_LIMIT)


def _bdot(a, b):
    return jnp.dot(a.astype(BF16), b.astype(BF16), preferred_element_type=F32)


def _bdot_nt(a, b):
    return lax.dot_general(a.astype(BF16), b.astype(BF16), (((1,), (1,)), ((), ())),
                           preferred_element_type=F32)


def _bdot_tn(a, b):
    return lax.dot_general(a.astype(BF16), b.astype(BF16), (((0,), (0,)), ((), ())),
                           preferred_element_type=F32)


def _div_pow2(x, n):
    return jnp.right_shift(x, int(math.log2(n)))


def _mod_pow2(x, n):
    return jnp.bitwise_and(x, n - 1)


def _split3(x):
    hi = x.astype(BF16)
    r = x - hi.astype(F32)
    mid = r.astype(BF16)
    lo = (r - mid.astype(F32)).astype(BF16)
    return hi, mid, lo


def _dot_exact_lhs01(m01, x):
    hi, mid, lo = _split3(x)
    d = lambda t: jnp.dot(m01, t, preferred_element_type=F32)
    return d(hi) + d(mid) + d(lo)


def _sigmoid(x):
    return 1.0 / (1.0 + jnp.exp(-x))


def _silu(x):
    return x * _sigmoid(x)


def _softplus(x):
    return jnp.maximum(x, 0.0) + jnp.log1p(jnp.exp(-jnp.abs(x)))


def _layer_norm(h, g, b):
    mu = jnp.mean(h, axis=-1, keepdims=True)
    d = h - mu
    var = jnp.mean(d * d, axis=-1, keepdims=True)
    return d * lax.rsqrt(var + NORM_EPS) * g + b


def _proj_kernel(*refs, tm, has_hist, full_u, one_segment):
    it = iter(refs)
    x_ref, cos_ref, sin_ref = next(it), next(it), next(it)
    w_ref = next(it)
    convw_ref, alog_ref, dtb_ref, tri_ref, seg_ref = next(it), next(it), next(it), next(it), next(it)
    hist_ref = valid_ref = None
    if has_hist:
        hist_ref, valid_ref = next(it), next(it)
    q_ref, k_ref, v_ref = next(it), next(it), next(it)
    qg_ref, kg_ref, vg_ref, z_ref, gcb_ref, u_ref = (next(it) for _ in range(6))
    ubuf = next(it)

    t = pl.program_id(1)
    sub = PROJ_SUB
    rows = [slice(j * sub, (j + 1) * sub) for j in range(tm // sub)]
    lane = lax.broadcasted_iota(I32, (sub, LANES), 1)
    first_half = _mod_pow2(lane, HEAD_DIM) < (ROT_DIM // 2)

    @pl.when(t == 0)
    def _():
        ubuf[0:SUBLANES, :] = jnp.zeros((SUBLANES, CONV_DIM), F32)

    @pl.when(t > 0)
    def _():
        ubuf[0:SUBLANES, :] = ubuf[tm:tm + SUBLANES, :]

    dots = []
    for r in rows:
        xb = x_ref[r, :].astype(BF16)
        dots.append([jnp.dot(xb, w_ref[:, lo:hi], preferred_element_type=F32)
                     for lo, hi in zip(IN_SPLITS[:-1], IN_SPLITS[1:])])

    def l2n(s):
        return s * lax.rsqrt(jnp.sum(s * s, axis=1, keepdims=True) + L2_EPS)

    for r, (pq, u, z, ab) in zip(rows, dots):
        cosv, sinv = cos_ref[r, :], sin_ref[r, :]

        def rope(s):
            sw = jnp.where(first_half, pltpu.roll(s, LANES - ROT_DIM // 2, axis=1),
                           pltpu.roll(s, ROT_DIM // 2, axis=1))
            return s * cosv + sw * sinv

        for j in range(Q_COLS // LANES):
            q_ref[r, j * LANES:(j + 1) * LANES] = rope(pq[:, j * LANES:(j + 1) * LANES])
        k_ref[r, :] = rope(pq[:, Q_COLS:Q_COLS + KV_COLS])
        v_ref[r, :] = pq[:, Q_COLS + KV_COLS:Q_COLS + 2 * KV_COLS]
        z_ref[r, :] = z

        if has_hist:
            u = u + hist_ref[r, :]
        if full_u:
            u_ref[r, :] = u
        elif r.stop == tm:
            u_ref[...] = u[sub - SUBLANES:, :]
        base = SUBLANES + r.start
        ubuf[base:base + sub, :] = u
        acc = u * convw_ref[CONV_W - 1:CONV_W, :]
        for j in range(1, CONV_W):
            acc = acc + ubuf[base - j:base - j + sub, :] * convw_ref[CONV_W - 1 - j:CONV_W - j, :]
        c = _silu(acc)
        if has_hist:
            c = c * valid_ref[r, :]
        for h in range(GDN_HEADS):
            sl = slice(h * GDN_DK, (h + 1) * GDN_DK)
            qg_ref[r, sl] = l2n(c[:, sl]) * (GDN_DK ** -0.5)
            kg_ref[r, sl] = l2n(c[:, QK_COLS + h * GDN_DK:QK_COLS + (h + 1) * GDN_DK])
        vg_ref[r, :] = c[:, 2 * QK_COLS:]

        g = -jnp.exp(alog_ref[...]) * _softplus(ab + dtb_ref[...])
        beta = _sigmoid(ab)
        if has_hist:
            g = g * valid_ref[r, :]
            beta = beta * valid_ref[r, :]
        g = jnp.where(lane < GDN_HEADS, g, 0.0)
        gc = _dot_exact_lhs01(tri_ref[...], g)
        if one_segment:
            gl = jnp.broadcast_to(gc[sub - 1:sub, :], (sub, LANES))
        else:
            gl = _dot_exact_lhs01(seg_ref[...], g)
        gcb_ref[r, :] = jnp.where(lane < GDN_HEADS, gc,
                                  jnp.where(lane < 2 * GDN_HEADS, beta,
                                            jnp.where(lane < 3 * GDN_HEADS,
                                                      pltpu.roll(gl, 2 * GDN_HEADS, axis=1), 0.0)))


def _rope_tables(pos):
    half = ROT_DIM // 2
    pos = np.asarray(pos, np.float64)
    inv_freq = ROPE_THETA ** (-np.arange(half, dtype=np.float64) * 2.0 / ROT_DIM)
    ang = pos[:, None] * inv_freq[None, :]
    cos, sin = np.cos(ang), np.sin(ang)
    p = pos.shape[0]
    cpat = np.concatenate([cos, cos, np.ones((p, HEAD_DIM - ROT_DIM))], axis=1)
    spat = np.concatenate([-sin, sin, np.zeros((p, HEAD_DIM - ROT_DIM))], axis=1)
    rep = (1, LANES // HEAD_DIM)
    return jnp.asarray(np.tile(cpat, rep), F32), jnp.asarray(np.tile(spat, rep), F32)


def _segment_matrices(tm, seg_len):
    i = np.arange(tm)
    same = (i[:, None] // seg_len) == (i[None, :] // seg_len)
    tri = same & (i[None, :] <= i[:, None])
    return jnp.asarray(tri, BF16), jnp.asarray(same, BF16)


def _proj(x, pos, wts, seg_len, n_seq, hist=None, valid=None):
    n = x.shape[0]
    rows = n // n_seq
    tm = min(PROJ_TILE, rows)
    nt = rows // tm
    has_hist = hist is not None
    cos_t, sin_t = _rope_tables(pos)
    tri, seg = _segment_matrices(PROJ_SUB, seg_len)

    tok = lambda w: pl.BlockSpec((tm, w), lambda b, t: (b * nt + t, 0))
    const = lambda a: pl.BlockSpec(a.shape, lambda b, t: (0,) * a.ndim)
    in_arrays = [x, cos_t, sin_t, wts['w_all'], wts['convw'], wts['alog'], wts['dtb'], tri, seg]
    in_specs = [tok(D_MODEL), pl.BlockSpec((tm, LANES), lambda b, t: (t, 0)),
                pl.BlockSpec((tm, LANES), lambda b, t: (t, 0))] + [const(a) for a in in_arrays[3:]]
    if has_hist:
        in_arrays += [hist, valid]
        in_specs += [tok(CONV_DIM), tok(1)]
    u_rows = n if has_hist else (n // tm) * SUBLANES
    u_block = tm if has_hist else SUBLANES
    out_shape = [jax.ShapeDtypeStruct((n, Q_COLS), F32), jax.ShapeDtypeStruct((n, KV_COLS), F32),
                 jax.ShapeDtypeStruct((n, KV_COLS), F32), jax.ShapeDtypeStruct((n, QK_COLS), F32),
                 jax.ShapeDtypeStruct((n, QK_COLS), F32), jax.ShapeDtypeStruct((n, Z_COLS), F32),
                 jax.ShapeDtypeStruct((n, Z_COLS), F32), jax.ShapeDtypeStruct((n, LANES), F32),
                 jax.ShapeDtypeStruct((u_rows, CONV_DIM), F32)]
    out_specs = [tok(Q_COLS), tok(KV_COLS), tok(KV_COLS), tok(QK_COLS), tok(QK_COLS), tok(Z_COLS),
                 tok(Z_COLS), tok(LANES),
                 pl.BlockSpec((u_block, CONV_DIM), lambda b, t: (b * nt + t, 0))]
    return pl.pallas_call(
        functools.partial(_proj_kernel, tm=tm, has_hist=has_hist, full_u=has_hist,
                          one_segment=seg_len == PROJ_SUB),
        out_shape=out_shape, grid=(n_seq, nt), in_specs=in_specs, out_specs=out_specs,
        scratch_shapes=[pltpu.VMEM((tm + SUBLANES, CONV_DIM), F32)],
        compiler_params=_cparams(("arbitrary", "arbitrary")),
        name="proj_hist" if has_hist else "proj",
    )(*in_arrays)


def _attn_blocks(qs, kcats, vcats, biases, sink, tq):
    lane = lax.broadcasted_iota(I32, (tq, LANES), 1)
    low = lane < HEAD_DIM
    n_slab = Q_COLS // LANES

    def stack(q):
        slabs = [q[:, j * LANES:(j + 1) * LANES] * (HEAD_DIM ** -0.5) for j in range(n_slab)]
        parts = ([jnp.where(low, s, 0.0) for s in slabs] + [jnp.where(low, 0.0, s) for s in slabs])
        return jnp.concatenate(parts, axis=0).astype(BF16)

    def unstack(o8):
        return [jnp.where(low, o8[j * tq:(j + 1) * tq, :], o8[(n_slab + j) * tq:(n_slab + j + 1) * tq, :])
                for j in range(n_slab)]

    rows = ATTN_HEADS * tq
    half = rows // 2
    klow = lax.broadcasted_iota(I32, (2 * WINDOW, LANES), 1) < HEAD_DIM
    one = jnp.ones((), BF16)
    q8s = _each(stack, qs)
    ss = _each(lambda q8, kc, b: _bdot_nt(q8, kc) + b, q8s, kcats, biases)
    ms = _each(lambda s: jnp.maximum(jnp.broadcast_to(jnp.max(s, axis=1, keepdims=True), (rows, LANES)),
                                     sink), ss)
    ps = _each(lambda s, m: jnp.exp(s - jnp.concatenate([m, m], axis=1)).astype(BF16), ss, ms)
    pv0 = _each(lambda p, vc: jnp.dot(p[:half], jnp.where(klow, vc, one), preferred_element_type=F32),
                ps, vcats)
    pv1 = _each(lambda p, vc: jnp.dot(p[half:], jnp.where(klow, one, vc), preferred_element_type=F32),
                ps, vcats)
    pvs = _each(lambda a, b: jnp.concatenate([a, b], axis=0), pv0, pv1)
    o8s = _each(lambda pv, m: pv / (pltpu.roll(pv, HEAD_DIM, axis=1) + jnp.exp(sink - m)), pvs, ms)
    return _each(unstack, o8s)


def _attn_prompt_kernel(q_ref, kc_ref, vc_ref, kp_ref, vp_ref, bias0_ref, bias_ref, sink_ref, o_ref, *,
                        nblk):
    kall = jnp.concatenate([kp_ref[...], kc_ref[...]], axis=0).astype(BF16)
    vall = jnp.concatenate([vp_ref[...], vc_ref[...]], axis=0).astype(BF16)
    win = lambda a, j: a[j * WINDOW:(j + 2) * WINDOW, :]
    qs = [q_ref[j * WINDOW:(j + 1) * WINDOW, :] for j in range(nblk)]
    biases = [bias0_ref[0]] + [bias_ref[...]] * (nblk - 1)
    outs = _attn_blocks(qs, [win(kall, j) for j in range(nblk)], [win(vall, j) for j in range(nblk)],
                        biases, sink_ref[...], WINDOW)
    for j, slabs in enumerate(outs):
        for c, slab in enumerate(slabs):
            o_ref[j * WINDOW:(j + 1) * WINDOW, c * LANES:(c + 1) * LANES] = slab


def _attn_sample_kernel(q_ref, kc_ref, vc_ref, kp_ref, vp_ref, bias_ref, sink_ref, o_ref, kw_ref, vw_ref,
                        *, nseq, n_new):
    tq = SAMPLE_SLOTS
    zpad = jnp.zeros((WINDOW - tq, LANES), F32)
    rows = lambda ref, j: ref[j * tq:(j + 1) * tq, :]
    cat = lambda pref, cref, j: jnp.concatenate([pref[j], rows(cref, j), zpad], axis=0).astype(BF16)
    outs = _attn_blocks([rows(q_ref, j) for j in range(nseq)],
                        [cat(kp_ref, kc_ref, j) for j in range(nseq)],
                        [cat(vp_ref, vc_ref, j) for j in range(nseq)],
                        [bias_ref[...]] * nseq, sink_ref[...], tq)
    for j, slabs in enumerate(outs):
        for c, slab in enumerate(slabs):
            o_ref[j * tq:(j + 1) * tq, c * LANES:(c + 1) * LANES] = slab
    row = lax.broadcasted_iota(I32, (WINDOW, LANES), 0)
    keep = WINDOW - n_new
    for pref, cref, wref in ((kp_ref, kc_ref, kw_ref), (vp_ref, vc_ref, vw_ref)):
        for j in range(nseq):
            new = jnp.concatenate([rows(cref, j), zpad], axis=0)
            wref[j] = jnp.where(row < keep, pltpu.roll(pref[j], keep, axis=0),
                                pltpu.roll(new, keep - SAMPLE_FIRST, axis=0))


def _sink_rows(sinks, tq):
    return jnp.broadcast_to(jnp.repeat(sinks.astype(F32), tq)[:, None], (ATTN_HEADS * tq, LANES))


def _attn_bias(tq, q_off, k_lo, k_hi, has_prev):
    qi = (np.arange(ATTN_HEADS * tq) % tq)[:, None]
    c = np.arange(2 * WINDOW)[None, :]
    cj = c - WINDOW
    vis_prev = (c < WINDOW) & (c > qi - q_off) & has_prev
    vis_cur = (c >= WINDOW) & (cj <= qi) & (cj >= k_lo) & (cj <= k_hi)
    return np.where(vis_prev | vis_cur, 0.0, NEG_BIG).astype(np.float32)


def _attn_prompt(q, k, v, sinks, n_seq):
    n = q.shape[0]
    nb = n // n_seq // WINDOW
    nblk = min(ATTN_BLOCKS_PER_STEP, nb)
    steps = nb // nblk
    tq = nblk * WINDOW
    cur = lambda w: pl.BlockSpec((tq, w), lambda b, i: (b * steps + i, 0))
    prev = pl.BlockSpec((WINDOW, LANES), lambda b, i: (b * nb + jnp.maximum(i * nblk - 1, 0), 0))
    bias2 = jnp.asarray(np.stack([_attn_bias(WINDOW, 0, 0, WINDOW - 1, False),
                                  _attn_bias(WINDOW, 0, 0, WINDOW - 1, True)]))
    rows = ATTN_HEADS * WINDOW
    return pl.pallas_call(
        functools.partial(_attn_prompt_kernel, nblk=nblk),
        out_shape=jax.ShapeDtypeStruct((n, Q_COLS), F32), grid=(n_seq, steps),
        in_specs=[cur(Q_COLS), cur(LANES), cur(LANES), prev, prev,
                  pl.BlockSpec((1, rows, 2 * WINDOW), lambda b, i: (jnp.minimum(i, 1), 0, 0)),
                  pl.BlockSpec((rows, 2 * WINDOW), lambda b, i: (0, 0)),
                  pl.BlockSpec((rows, LANES), lambda b, i: (0, 0))],
        out_specs=cur(Q_COLS),
        compiler_params=_cparams(("arbitrary", "arbitrary")), name="attn_prompt",
    )(q, k, v, k, v, bias2, bias2[1], _sink_rows(sinks, WINDOW))


def _attn_sample(q, k, v, cache_k, cache_v, sinks, n_seq, n_new):
    tq = SAMPLE_SLOTS
    nseq = min(ATTN_SEQS_PER_STEP, n_seq)
    cur = lambda w: pl.BlockSpec((nseq * tq, w), lambda b: (b, 0))
    prev = pl.BlockSpec((nseq, WINDOW, LANES), lambda b: (b, 0, 0))
    bias = jnp.asarray(_attn_bias(tq, SAMPLE_FIRST, SAMPLE_FIRST, SAMPLE_FIRST + 3, True))
    win = jax.ShapeDtypeStruct((n_seq, WINDOW, LANES), F32)
    return pl.pallas_call(
        functools.partial(_attn_sample_kernel, nseq=nseq, n_new=n_new),
        out_shape=[jax.ShapeDtypeStruct((n_seq * tq, Q_COLS), F32), win, win], grid=(n_seq // nseq,),
        in_specs=[cur(Q_COLS), cur(LANES), cur(LANES), prev, prev,
                  pl.BlockSpec(bias.shape, lambda b: (0, 0)),
                  pl.BlockSpec((ATTN_HEADS * tq, LANES), lambda b: (0, 0))],
        out_specs=[cur(Q_COLS), prev, prev],
        compiler_params=_cparams(("arbitrary",)), name="attn_sample",
    )(q, k, v, cache_k, cache_v, bias, _sink_rows(sinks, tq))


def _each(f, *lists):
    return [f(*args) for args in zip(*lists)]


def _unit_lower_inverse(ms, eye, same_base, base_only=False):
    c = ms[0].shape[0]

    def neumann(q0s, n_factors):
        xs = _each(lambda q: eye + q, q0s)
        if n_factors == 1:
            return xs
        qs = _each(_bdot, q0s, q0s)
        for _ in range(n_factors - 2):
            prods = _each(lambda x, q: _bdot(jnp.concatenate([x, q], axis=0), q), xs, qs)
            xs = _each(lambda x, pr: x + pr[:c], xs, prods)
            qs = _each(lambda pr: pr[c:], prods)
        return _each(lambda x, q: x + _bdot(x, q), xs, qs)

    ds = _each(lambda m: jnp.where(same_base, m, 0.0), ms)
    xs = neumann(_each(lambda d: -d, ds), int(math.log2(INV_BASE)))
    nblk = c // INV_BASE
    if nblk == 1 or base_only:
        return xs
    ls = _each(lambda m, d: m - d, ms, ds)
    ns = _each(lambda x, l: -_bdot(x, l), xs, ls)
    ys = neumann(ns, int(math.log2(nblk)))
    return _each(_bdot, ys, xs)


def _gdn_intra(qs, ks, vs, gcs, gls, betas, same_seq, low_incl, low_strict, eye, same_base,
               base_only=False):
    del same_seq
    e_gcs = _each(jnp.exp, gcs)

    def decay_of(gc):
        gc_row = jnp.sum(jnp.where(eye > 0, gc, 0.0), axis=0, keepdims=True)
        return jnp.where(low_incl, jnp.exp(jnp.where(low_incl, gc - gc_row, 0.0)), 0.0)

    c = qs[0].shape[0]
    decays = _each(decay_of, gcs)
    kbs = _each(lambda k, b: k * b, ks, betas)
    vbs = _each(lambda v, b: v * b, vs, betas)
    kqs = _each(lambda kb, q, k: _bdot_nt(jnp.concatenate([kb, q], axis=0), k), kbs, qs, ks)
    ms = _each(lambda kq, d: jnp.where(low_strict, kq[:c] * d, 0.0), kqs, decays)
    attns = _each(lambda kq, d: kq[c:] * d, kqs, decays)
    tmats = _unit_lower_inverse(ms, eye, same_base, base_only)
    uws = _each(lambda t, vb, kb, e: _bdot(t, jnp.concatenate([vb, kb * e], axis=1)),
                tmats, vbs, kbs, e_gcs)
    us = _each(lambda uw: uw[:, :GDN_DV], uws)
    ws = _each(lambda uw: uw[:, GDN_DV:], uws)
    q_decs = _each(lambda q, e: q * e, qs, e_gcs)
    k_decs = _each(lambda k, gl, gc: k * jnp.exp(gl - gc), ks, gls, gcs)
    return us, ws, attns, q_decs, k_decs


def _chunk_masks(c, seq_len):
    i = lax.broadcasted_iota(I32, (c, c), 0)
    j = lax.broadcasted_iota(I32, (c, c), 1)
    same_seq = _div_pow2(i, seq_len) == _div_pow2(j, seq_len)
    low_incl = same_seq & (i >= j)
    low_strict = same_seq & (i > j)
    eye = (i == j).astype(F32)
    same_base = _div_pow2(i, INV_BASE) == _div_pow2(j, INV_BASE)
    return same_seq, low_incl, low_strict, eye, same_base


def _gated_rms(o, z, nw):
    o = o * lax.rsqrt(jnp.mean(o * o, axis=1, keepdims=True) + NORM_EPS) * nw
    return o * _silu(z)


def _gdn_prompt_kernel(qg_ref, kg_ref, vg_ref, z_ref, gcb_ref, nw_ref, o_ref, s_out_ref, s_scr):
    c = GDN_CHUNK
    n = pl.program_id(1)

    @pl.when(n == 0)
    def _():
        s_scr[...] = jnp.zeros_like(s_scr)

    masks = _chunk_masks(c, c)
    nw = nw_ref[...]
    chains = [(b, h) for b in range(qg_ref.shape[0]) for h in range(GDN_HEADS)]
    hs = lambda h: slice(h * GDN_DK, (h + 1) * GDN_DK)
    col = lambda off: [gcb_ref[b, :, off + h:off + h + 1] for b, h in chains]
    gcs, betas, gls = col(0), col(GDN_HEADS), col(2 * GDN_HEADS)
    qs = [qg_ref[b, :, hs(h)] for b, h in chains]
    ks = [kg_ref[b, :, hs(h)] for b, h in chains]
    vs = [vg_ref[b, :, hs(h)] for b, h in chains]
    us, ws, attns, q_decs, k_decs = _gdn_intra(qs, ks, vs, gcs, gls, betas, *masks)
    ss = [s_scr[b, h] for b, h in chains]
    wqs = _each(lambda w, qd, s: _bdot(jnp.concatenate([w, qd], axis=0), s), ws, q_decs, ss)
    wss = _each(lambda wq: wq[:c], wqs)
    qss = _each(lambda wq: wq[c:], wqs)
    v_news = _each(lambda u, x: u - x, us, wss)
    avs = _each(_bdot, attns, v_news)
    kvs = _each(_bdot_tn, k_decs, v_news)
    for (b, h), s, gl, qsv, av, kv in zip(chains, ss, gls, qss, avs, kvs):
        s_scr[b, h] = s * jnp.exp(gl[0:1, :]) + kv
        o_ref[b, :, hs(h)] = _gated_rms(qsv + av, z_ref[b, :, hs(h)], nw)

    @pl.when(n == pl.num_programs(1) - 1)
    def _():
        s_out_ref[...] = s_scr[...]


def _gdn_prompt(qg, kg, vg, z, gcb, norm_w, n_seq):
    n = qg.shape[0]
    s_len = n // n_seq
    nb = min(GDN_SEQ_PER_STEP, n_seq)
    v3 = lambda a: a.reshape(n_seq, s_len, a.shape[-1])
    tok = lambda w: pl.BlockSpec((nb, GDN_CHUNK, w), lambda b, i: (b, i, 0))
    o, s = pl.pallas_call(
        _gdn_prompt_kernel,
        out_shape=[jax.ShapeDtypeStruct((n_seq, s_len, Z_COLS), F32),
                   jax.ShapeDtypeStruct((n_seq, GDN_HEADS, GDN_DK, GDN_DV), F32)],
        grid=(n_seq // nb, s_len // GDN_CHUNK),
        in_specs=[tok(QK_COLS), tok(QK_COLS), tok(Z_COLS), tok(Z_COLS), tok(LANES),
                  pl.BlockSpec((1, GDN_DV), lambda b, i: (0, 0))],
        out_specs=[tok(Z_COLS),
                   pl.BlockSpec((nb, GDN_HEADS, GDN_DK, GDN_DV), lambda b, i: (b, 0, 0, 0))],
        scratch_shapes=[pltpu.VMEM((nb, GDN_HEADS, GDN_DK, GDN_DV), F32)],
        compiler_params=_cparams(("arbitrary", "arbitrary")), name="gdn_prompt",
    )(v3(qg), v3(kg), v3(vg), v3(z), v3(gcb), norm_w)
    return o.reshape(n, Z_COLS), s


def _gdn_sample_kernel(qg_ref, kg_ref, vg_ref, z_ref, gcb_ref, nw_ref, s_in_ref, o_ref, s_out_ref):
    c = GDN_CHUNK
    n_sub = c // SAMPLE_SLOTS
    masks = _chunk_masks(c, SAMPLE_SLOTS)
    heads = range(GDN_HEADS)
    hs = lambda h: slice(h * GDN_DK, (h + 1) * GDN_DK)
    rs = lambda s: slice(s * SAMPLE_SLOTS, (s + 1) * SAMPLE_SLOTS)
    col = lambda off: [gcb_ref[:, off + h:off + h + 1] for h in heads]
    gcs, betas, gls = col(0), col(GDN_HEADS), col(2 * GDN_HEADS)
    us, ws, attns, q_decs, k_decs = _gdn_intra([qg_ref[:, hs(h)] for h in heads],
                                               [kg_ref[:, hs(h)] for h in heads],
                                               [vg_ref[:, hs(h)] for h in heads], gcs, gls, betas, *masks,
                                               base_only=SAMPLE_SLOTS <= INV_BASE)
    pairs = [(h, s) for h in heads for s in range(n_sub)]
    sts = [s_in_ref[s, h] for h, s in pairs]
    boths = [jnp.concatenate([ws[h][rs(s), :], q_decs[h][rs(s), :]], axis=0) for h, s in pairs]
    rr = _each(_bdot, boths, sts)
    gather = lambda h, part: jnp.concatenate(
        [rr[h * n_sub + s][part * SAMPLE_SLOTS:(part + 1) * SAMPLE_SLOTS, :] for s in range(n_sub)], axis=0)
    v_news = [us[h] - gather(h, 0) for h in heads]
    avs = _each(_bdot, attns, v_news)
    row = lax.broadcasted_iota(I32, (c, LANES), 0)
    seq_of_row = _div_pow2(row, SAMPLE_SLOTS)
    kds = [jnp.where(seq_of_row == s, k_decs[h], 0.0) for h, s in pairs]
    kvs = _each(_bdot_tn, kds, [v_news[h] for h, _ in pairs])
    egls = _each(jnp.exp, gls)
    for (h, s), st, kv in zip(pairs, sts, kvs):
        s_out_ref[s, h] = st * egls[h][s * SAMPLE_SLOTS:s * SAMPLE_SLOTS + 1, :] + kv
    nw = nw_ref[...]
    for h in heads:
        o_ref[:, hs(h)] = _gated_rms(gather(h, 1) + avs[h], z_ref[:, hs(h)], nw)


def _gdn_sample(qg, kg, vg, z, gcb, norm_w, state):
    n = qg.shape[0]
    n_sub = GDN_CHUNK // SAMPLE_SLOTS
    tok = lambda w: pl.BlockSpec((GDN_CHUNK, w), lambda i: (i, 0))
    st = pl.BlockSpec((n_sub, GDN_HEADS, GDN_DK, GDN_DV), lambda i: (i, 0, 0, 0))
    return pl.pallas_call(
        _gdn_sample_kernel,
        out_shape=[jax.ShapeDtypeStruct((n, Z_COLS), F32),
                   jax.ShapeDtypeStruct(state.shape, F32)],
        grid=(n // GDN_CHUNK,),
        in_specs=[tok(QK_COLS), tok(QK_COLS), tok(Z_COLS), tok(Z_COLS), tok(LANES),
                  pl.BlockSpec((1, GDN_DV), lambda i: (0, 0)), st],
        out_specs=[tok(Z_COLS), st],
        compiler_params=_cparams(("arbitrary",)), name="gdn_sample",
    )(qg, kg, vg, z, gcb, norm_w, state)


def _post_kernel(a_ref, g_ref, x_ref, wo_ref, ln_g_ref, ln_b_ref, wr_ref, x1_ref, route_ref, *, tm):
    sub = POST_SUB
    rows = [slice(j * sub, (j + 1) * sub) for j in range(tm // sub)]
    d = lambda a, b: jnp.dot(a, b, preferred_element_type=F32)
    mixes = [d(a_ref[r, :].astype(BF16), wo_ref[0:Q_COLS, :]) + d(g_ref[r, :].astype(BF16), wo_ref[Q_COLS:, :])
             for r in rows]
    x1s = [_layer_norm(DEEPNORM_ALPHA * x_ref[r, :] + mix, ln_g_ref[...], ln_b_ref[...])
           for r, mix in zip(rows, mixes)]
    for r, x1 in zip(rows, x1s):
        x1_ref[r, :] = x1
    w2 = wr_ref[...]
    lgs = []
    for x1 in x1s:
        xh = x1.astype(BF16)
        xm = (x1 - xh.astype(F32)).astype(BF16)
        both = d(xh, w2)
        lgs.append(both[:, :LANES] + both[:, LANES:] + d(xm, w2[:, :LANES]))
    for r, lg in zip(rows, lgs):
        route_ref[:, r] = jnp.transpose(_route(lg))[0:SUBLANES, :]


def _route(lg):
    lane = lax.broadcasted_iota(I32, lg.shape, 1)
    lane_f = lane.astype(F32)
    big = float(LANES)

    def first_max(vals, mask):
        v = jnp.where(mask, vals, NEG_BIG)
        mx = jnp.max(v, axis=1, keepdims=True)
        idx = jnp.min(jnp.where(mask & (v == mx), lane_f, big), axis=1, keepdims=True)
        return mx, idx

    gmask = lane < N_GROUPS
    gmax, gidx = first_max(lg, gmask)
    gden = jnp.sum(jnp.where(gmask, jnp.exp(lg - gmax), 0.0), axis=1, keepdims=True)
    g_top_p = 1.0 / gden
    e_lane = lane - N_GROUPS
    e_group = _div_pow2(jnp.maximum(e_lane, 0), EXPERTS_PER_GROUP).astype(F32)
    emask = (e_lane >= 0) & (e_lane < N_EXPERTS) & (e_group == gidx)
    m1, i1 = first_max(lg, emask)
    eden = jnp.sum(jnp.where(emask, jnp.exp(lg - m1), 0.0), axis=1, keepdims=True)
    m2, i2 = first_max(lg, emask & (lane_f != i1))
    p1 = 1.0 / eden
    p2 = jnp.exp(m2 - m1) / eden
    tot = p1 + p2
    gate1 = g_top_p * (p1 / tot)
    gate2 = g_top_p * (p2 / tot)
    return jnp.where(lane == 0, gate1,
                     jnp.where(lane == 1, gate2,
                               jnp.where(lane == 2, i1 - N_GROUPS,
                                         jnp.where(lane == 3, i2 - N_GROUPS, 0.0))))


def _post(attn_o, gdn_o, x, wts):
    n = x.shape[0]
    tm = min(POST_TILE, n)
    tok = lambda w: pl.BlockSpec((tm, w), lambda i: (i, 0))
    const = lambda a: pl.BlockSpec(a.shape, lambda i: (0,) * a.ndim)
    consts = [wts['wo'], wts['ln1_g'], wts['ln1_b'], wts['wr']]
    return pl.pallas_call(
        functools.partial(_post_kernel, tm=tm),
        out_shape=[jax.ShapeDtypeStruct((n, D_MODEL), F32), jax.ShapeDtypeStruct((SUBLANES, n), F32)],
        grid=(n // tm,),
        in_specs=[tok(Q_COLS), tok(Z_COLS), tok(D_MODEL)] + [const(a) for a in consts],
        out_specs=[tok(D_MODEL), pl.BlockSpec((SUBLANES, tm), lambda i: (0, i))],
        compiler_params=_cparams(("arbitrary",)), name="post_%d" % (n // tm),
    )(attn_o, gdn_o, x, *consts)


def _slab_loop(n, body):
    n_main = jnp.right_shift(n, int(math.log2(SLAB_UNROLL)))

    def main(i, c):
        for u in range(SLAB_UNROLL):
            body(i * SLAB_UNROLL + u, u)
        return c

    lax.fori_loop(0, n_main, main, 0)
    lax.fori_loop(n_main * SLAB_UNROLL, n, lambda j, c: (body(j, 0), c)[1], 0)


def _dispatch_kernel(dst_ref, nslab_ref, ztab_ref, zinfo_ref, slot_ref, gate_ref, *rest,
                     group_tiles, max_tiles):
    x_refs = rest[:len(group_tiles)]
    xs_ref, pbuf, sem, zbuf, zsem = rest[len(group_tiles):]
    n_tiles = sum(group_tiles)
    g = pl.program_id(0)
    cur = lax.rem(g, 2)

    def slab_copy(tile, buf_slot, j):
        d = pl.multiple_of(dst_ref[tile * PERM_SLABS + j], SLAB)
        src = pbuf.at[buf_slot, pl.ds(pl.multiple_of(j * SLAB, SLAB), SLAB), :]
        return pltpu.make_async_copy(src, xs_ref.at[pl.ds(d, SLAB), :], sem.at[buf_slot])

    def tail_copy(k):
        d = pl.multiple_of(ztab_ref[k], SLAB)
        return pltpu.make_async_copy(zbuf.at[pl.ds(0, SLAB), :], xs_ref.at[pl.ds(d, SLAB), :], zsem)

    def tile_copy(t):
        d = pl.multiple_of(t * ROW_TILE, ROW_TILE)
        return pltpu.make_async_copy(zbuf, xs_ref.at[pl.ds(d, ROW_TILE), :], zsem)

    @pl.when(g == 0)
    def _():
        zbuf[...] = jnp.zeros_like(zbuf)

    share = -(-ZERO_TABLE // n_tiles)
    k0 = g * share
    _slab_loop(jnp.clip(zinfo_ref[0] - k0, 0, share), lambda j, u: tail_copy(k0 + j).start(priority=1))

    @pl.when(zinfo_ref[1] + g < max_tiles)
    def _():
        tile_copy(zinfo_ref[1] + g).start(priority=1)

    x = x_refs[-1][...]
    bound = n_tiles
    for x_ref, nt in zip(x_refs[-2::-1], group_tiles[:0:-1]):
        bound -= nt
        x = jnp.where(g < bound, x_ref[...], x)

    r = lax.broadcasted_iota(I32, (PERM_ROWS, TOK_TILE), 0)
    sl = slot_ref[0]
    hit0, hit1 = r == sl[0:1, :], r == sl[1:2, :]
    onehot = jnp.where(hit0 | hit1, 1.0, 0.0).astype(BF16)
    gt = gate_ref[0]
    gcol = jnp.sum(jnp.where(hit0, gt[0:1, :], 0.0) + jnp.where(hit1, gt[1:2, :], 0.0),
                   axis=1, keepdims=True)
    pbuf[cur, :, 0:D_MODEL] = jnp.dot(onehot, x.astype(BF16), preferred_element_type=F32).astype(BF16)
    g_hi = gcol.astype(BF16).astype(F32)
    lane = lax.broadcasted_iota(I32, (PERM_ROWS, LANES), 1)
    pbuf[cur, :, D_MODEL:] = jnp.where(lane < LANES // 2, g_hi, gcol - g_hi).astype(BF16)

    @pl.when(g > 0)
    def _():
        _slab_loop(nslab_ref[g - 1], lambda j, u: slab_copy(g - 1, 1 - cur, j).wait())

    _slab_loop(nslab_ref[g], lambda j, u: slab_copy(g, cur, j).start(priority=u % 2))

    @pl.when(g == n_tiles - 1)
    def _():
        _slab_loop(nslab_ref[g], lambda j, u: slab_copy(g, cur, j).wait())
        lax.fori_loop(zinfo_ref[1] + n_tiles, max_tiles,
                      lambda t, c: (tile_copy(t).start(priority=1), c)[1], 0)
        _slab_loop(zinfo_ref[0], lambda k, u: tail_copy(k).wait())
        lax.fori_loop(zinfo_ref[1], max_tiles, lambda t, c: (tile_copy(t).wait(), c)[1], 0)


def _dispatch(plan, x1s, max_tiles):
    group_tiles = tuple(x1.shape[0] // TOK_TILE for x1 in x1s)
    n_tiles = sum(group_tiles)
    tile = lambda i, d, ns, zt, zi: (i, 0, 0)
    in_specs = [pl.BlockSpec((1, TOP_K, TOK_TILE), tile), pl.BlockSpec((1, TOP_K, TOK_TILE), tile)]
    base = 0
    for nt in group_tiles:
        in_specs.append(pl.BlockSpec(
            (TOK_TILE, D_MODEL),
            lambda i, d, ns, zt, zi, base=base, nt=nt: (jnp.clip(i - base, 0, nt - 1), 0)))
        base += nt
    return pl.pallas_call(
        functools.partial(_dispatch_kernel, group_tiles=group_tiles, max_tiles=max_tiles),
        out_shape=jax.ShapeDtypeStruct((max_tiles * ROW_TILE, XS_WORDS), BF16),
        grid_spec=pltpu.PrefetchScalarGridSpec(
            num_scalar_prefetch=4, grid=(n_tiles,), in_specs=in_specs,
            out_specs=pl.BlockSpec(memory_space=pl.ANY),
            scratch_shapes=[pltpu.VMEM((2, PERM_ROWS, XS_WORDS), BF16), pltpu.SemaphoreType.DMA((2,)),
                            pltpu.VMEM((ROW_TILE, XS_WORDS), BF16), pltpu.SemaphoreType.DMA(())]),
        compiler_params=_cparams(("arbitrary",)), name="moe_dispatch",
    )(plan['slab_dst'], plan['nslab'], plan['ztab'], plan['zinfo'], plan['slot_rows'], plan['gate_rows'],
      *x1s)


def _expert_kernel(t0_ref, nt_ref, nu_ref, wg_ref, wu_ref, wd_ref, xs_ref, ye_ref,
                   xbuf, ybuf, wgu_scr, wd_scr, zbuf, in_sem, out_sem, zsem, *, max_tiles):
    e = pl.program_id(0)
    n_used = nu_ref[0]
    ahead = EXPERT_IN_SLOTS - 1

    def in_copy(t):
        slot = lax.rem(t, EXPERT_IN_SLOTS)
        src = xs_ref.at[pl.ds(pl.multiple_of(t * ROW_TILE, ROW_TILE), ROW_TILE), :]
        return pltpu.make_async_copy(src, xbuf.at[slot], in_sem.at[slot])

    def out_copy(t):
        slot = lax.rem(t, EXPERT_OUT_SLOTS)
        dst = ye_ref.at[pl.ds(pl.multiple_of(t * ROW_TILE, ROW_TILE), ROW_TILE), :]
        return pltpu.make_async_copy(ybuf.at[slot], dst, out_sem.at[slot])

    def zero_copy(t):
        dst = ye_ref.at[pl.ds(pl.multiple_of(t * ROW_TILE, ROW_TILE), ROW_TILE), :]
        return pltpu.make_async_copy(zbuf, dst, zsem)

    @pl.when(e == 0)
    def _():
        for t in range(ahead):
            @pl.when(t < n_used)
            def _():
                in_copy(t).start()

        zbuf[...] = jnp.zeros_like(zbuf)
        lax.fori_loop(n_used, max_tiles, lambda t, c: (zero_copy(t).start(), c)[1], 0)

    wgu_scr[:, 0:EXPERT_FF] = wg_ref[0].astype(BF16)
    wgu_scr[:, EXPERT_FF:] = wu_ref[0].astype(BF16)
    wd_scr[...] = wd_ref[0].astype(BF16)

    sub = EXPERT_SUB
    rows = [slice(j * sub, (j + 1) * sub) for j in range(ROW_TILE // sub)]
    d = lambda a, b: jnp.dot(a, b, preferred_element_type=F32)

    def tile_body(j, carry):
        t = t0_ref[e] + j
        slot = lax.rem(t, EXPERT_OUT_SLOTS)
        in_slot = lax.rem(t, EXPERT_IN_SLOTS)

        @pl.when(t + ahead < n_used)
        def _():
            in_copy(t + ahead).start()

        in_copy(t).wait()

        @pl.when(t >= EXPERT_OUT_SLOTS)
        def _():
            out_copy(t - EXPERT_OUT_SLOTS).wait()

        wgu, wd = wgu_scr[...], wd_scr[...]
        xs = [xbuf[in_slot, r, 0:D_MODEL] for r in rows]
        hs = [d(x, wgu) for x in xs]
        hhs = [(_silu(h[:, :EXPERT_FF]) * h[:, EXPERT_FF:]).astype(BF16) for h in hs]
        ys = [d(hh, wd) for hh in hhs]
        for r, y in zip(rows, ys):
            parts = xbuf[in_slot, r, D_MODEL:].astype(F32)
            gate = parts + pltpu.roll(parts, LANES // 2, axis=1)
            ybuf[slot, r, :] = (y * jnp.concatenate([gate] * (D_MODEL // LANES), axis=1)).astype(BF16)
        out_copy(t).start()
        return carry

    lax.fori_loop(0, nt_ref[e], tile_body, 0)

    @pl.when(e == pl.num_programs(0) - 1)
    def _():
        for back in range(EXPERT_OUT_SLOTS, 0, -1):
            @pl.when(n_used >= back)
            def _():
                out_copy(n_used - back).wait()
        lax.fori_loop(n_used, max_tiles, lambda t, c: (zero_copy(t).wait(), c)[1], 0)


def _experts(plan, xs, w_gate, w_up, w_down):
    max_tiles = xs.shape[0] // ROW_TILE
    wsel = lambda e, t0, nt, nu: (e, 0, 0)
    return pl.pallas_call(
        functools.partial(_expert_kernel, max_tiles=max_tiles),
        out_shape=jax.ShapeDtypeStruct((xs.shape[0], D_MODEL), BF16),
        grid_spec=pltpu.PrefetchScalarGridSpec(
            num_scalar_prefetch=3, grid=(N_EXPERTS,),
            in_specs=[pl.BlockSpec((1, D_MODEL, EXPERT_FF), wsel),
                      pl.BlockSpec((1, D_MODEL, EXPERT_FF), wsel),
                      pl.BlockSpec((1, EXPERT_FF, D_MODEL), wsel),
                      pl.BlockSpec(memory_space=pl.ANY)],
            out_specs=pl.BlockSpec(memory_space=pl.ANY),
            scratch_shapes=[pltpu.VMEM((EXPERT_IN_SLOTS, ROW_TILE, XS_WORDS), BF16),
                            pltpu.VMEM((EXPERT_OUT_SLOTS, ROW_TILE, D_MODEL), BF16),
                            pltpu.VMEM((D_MODEL, 2 * EXPERT_FF), BF16),
                            pltpu.VMEM((EXPERT_FF, D_MODEL), BF16),
                            pltpu.VMEM((ROW_TILE, D_MODEL), BF16),
                            pltpu.SemaphoreType.DMA((EXPERT_IN_SLOTS,)),
                            pltpu.SemaphoreType.DMA((EXPERT_OUT_SLOTS,)),
                            pltpu.SemaphoreType.DMA(())]),
        compiler_params=_cparams(("arbitrary",)), name="moe_experts",
    )(plan['tile_start'], plan['tile_count'], plan['n_used'], w_gate, w_up, w_down, xs)


def _combine_kernel(dst_ref, nslab_ref, x1_ref, slot_ref, ye_ref, ln_g_ref, ln_b_ref, y_ref,
                    buf, sem, *, tile_base, n_tiles):
    i = pl.program_id(0)
    g = tile_base + i
    cur = lax.rem(i, 2)

    nxt = jnp.minimum(g + 1, tile_base + n_tiles - 1)

    def slab_copy(tile, buf_slot, j):
        d = pl.multiple_of(dst_ref[tile * PERM_SLABS + j], SLAB)
        dst = buf.at[buf_slot, pl.ds(j * SLAB, SLAB), :]
        return pltpu.make_async_copy(ye_ref.at[pl.ds(d, SLAB), :], dst, sem.at[buf_slot])

    @pl.when(i == 0)
    def _():
        for j in range(PERM_SLABS):
            slab_copy(g, cur, j).start(priority=j % 2)

    for j in range(PERM_SLABS):
        slab_copy(nxt, 1 - cur, j).start(priority=j % 2)
    for j in range(PERM_SLABS):
        slab_copy(g, cur, j).wait()

    sub = COMBINE_SUB
    subs = [slice(j * sub, (j + 1) * sub) for j in range(TOK_TILE // sub)]
    col = lax.broadcasted_iota(I32, (sub, PERM_ROWS), 1)
    sl = slot_ref[0]
    diag = (lax.broadcasted_iota(I32, (sub, sub), 0) == lax.broadcasted_iota(I32, (sub, sub), 1))
    as_col = lambda row: jnp.sum(jnp.where(diag, row, 0), axis=1, keepdims=True)
    picks = [jnp.where((col == as_col(sl[0:1, r])) | (col == as_col(sl[1:2, r])), 1.0, 0.0).astype(BF16)
             for r in subs]
    rows = buf[cur]
    moes = [jnp.dot(pick, rows, preferred_element_type=F32) for pick in picks]
    for r, moe in zip(subs, moes):
        y_ref[r, :] = _layer_norm(DEEPNORM_ALPHA * x1_ref[r, :] + moe, ln_g_ref[...], ln_b_ref[...])

    @pl.when(i == n_tiles - 1)
    def _():
        for j in range(PERM_SLABS):
            slab_copy(nxt, 1 - cur, j).wait()


def _combine(plan, tile_base, x1, ye, ln_g, ln_b):
    n = x1.shape[0]
    n_tiles = n // TOK_TILE
    tok = lambda w: pl.BlockSpec((TOK_TILE, w), lambda i, d, ns: (i, 0))
    const = lambda a: pl.BlockSpec(a.shape, lambda i, d, ns: (0,) * a.ndim)
    return pl.pallas_call(
        functools.partial(_combine_kernel, tile_base=tile_base, n_tiles=n_tiles),
        out_shape=jax.ShapeDtypeStruct((n, D_MODEL), F32),
        grid_spec=pltpu.PrefetchScalarGridSpec(
            num_scalar_prefetch=2, grid=(n_tiles,),
            in_specs=[tok(D_MODEL),
                      pl.BlockSpec((1, TOP_K, TOK_TILE), lambda i, d, ns: (tile_base + i, 0, 0)),
                      pl.BlockSpec(memory_space=pl.ANY), const(ln_g), const(ln_b)],
            out_specs=tok(D_MODEL),
            scratch_shapes=[pltpu.VMEM((2, PERM_ROWS, D_MODEL), BF16), pltpu.SemaphoreType.DMA((2,))]),
        compiler_params=_cparams(("arbitrary",)), name="moe_combine_%d" % tile_base,
    )(plan['slab_dst'], plan['nslab'], x1, plan['slot_rows'], ye, ln_g, ln_b)


def _routing_plan(ids, gates):
    nt = ids.shape[1] // TOK_TILE
    pairs = TOP_K * TOK_TILE
    ex = jnp.arange(N_EXPERTS, dtype=I32)
    per_tile = lambda a: jnp.swapaxes(a.reshape(TOP_K, nt, TOK_TILE), 0, 1)
    flat = per_tile(ids).reshape(nt, pairs)
    onehot = (flat[:, None, :] == ex[None, :, None])
    p = np.arange(pairs)
    triu = jnp.asarray(p[:, None] <= p[None, :], BF16)
    csum = jnp.dot(onehot.astype(BF16).reshape(nt * N_EXPERTS, pairs), triu,
                   preferred_element_type=F32).astype(I32).reshape(nt, N_EXPERTS, pairs)
    oh = onehot.astype(I32)
    rank = jnp.sum(oh * (csum - 1), axis=1)
    cnt = csum[:, :, -1]
    cpad = (cnt + SLAB - 1) // SLAB * SLAB
    seg_end = jnp.cumsum(cpad, axis=1)
    seg_off = seg_end - cpad
    slot = jnp.sum(oh * seg_off[:, :, None], axis=1) + rank
    run_end = jnp.cumsum(cpad, axis=0)
    ntiles_e = (run_end[-1] + ROW_TILE - 1) // ROW_TILE
    tile_end = jnp.cumsum(ntiles_e)
    dst_run = ((tile_end - ntiles_e) * ROW_TILE)[None, :] + run_end - cpad
    j8 = jnp.arange(PERM_SLABS, dtype=I32) * SLAB
    e_of = jnp.minimum(jnp.sum((j8[None, :, None] >= seg_end[:, None, :]).astype(I32), axis=2),
                       N_EXPERTS - 1)
    sel = (e_of[:, :, None] == ex).astype(I32)
    last_slab = _max_row_tiles(ids.shape[1]) * ROW_TILE - SLAB
    slab_dst = jnp.minimum(jnp.sum(sel * (dst_run - seg_off)[:, None, :], axis=2) + j8[None, :], last_slab)
    n_used = tile_end[-1]
    row_start = (tile_end - ntiles_e) * ROW_TILE
    tail_cnt = (ntiles_e * ROW_TILE - run_end[-1]) // SLAB
    tail_end = jnp.cumsum(tail_cnt)
    k = jnp.arange(ZERO_TABLE, dtype=I32)
    e_k = jnp.minimum(jnp.sum((k[:, None] >= tail_end[None, :]).astype(I32), axis=1), N_EXPERTS - 1)
    base_k = jnp.sum((e_k[:, None] == ex).astype(I32)
                     * (row_start + run_end[-1] - SLAB * (tail_end - tail_cnt))[None, :], axis=1)
    return dict(
        slab_dst=slab_dst.reshape(-1).astype(I32), nslab=(seg_end[:, -1] // SLAB).astype(I32),
        ztab=(base_k + SLAB * k).astype(I32), zinfo=jnp.stack([tail_end[-1], n_used]).astype(I32),
        slot_rows=slot.reshape(nt, TOP_K, TOK_TILE).astype(I32),
        gate_rows=per_tile(gates).astype(F32),
        tile_start=(tile_end - ntiles_e).astype(I32), tile_count=ntiles_e.astype(I32),
        n_used=n_used.reshape(1).astype(I32))


def _max_row_tiles(n_tokens):
    rows = TOP_K * n_tokens + (n_tokens // TOK_TILE) * N_EXPERTS * (SLAB - 1)
    return (rows + ROW_TILE - 1) // ROW_TILE + N_EXPERTS


def _moe(x1s, routes, wts):
    ids = jnp.concatenate([r[TOP_K:2 * TOP_K, :] for r in routes], axis=1).astype(I32)
    gates = jnp.concatenate([r[0:TOP_K, :] for r in routes], axis=1)
    plan = _routing_plan(ids, gates)
    max_tiles = _max_row_tiles(ids.shape[1])
    bases = [0]
    for x1 in x1s[:-1]:
        bases.append(bases[-1] + x1.shape[0] // TOK_TILE)
    xs = _dispatch(plan, x1s, max_tiles)
    ye = _experts(plan, xs, wts['w_gate'], wts['w_up'], wts['w_down'])
    return [_combine(plan, base, x1, ye, wts['ln2_g'], wts['ln2_b']) for base, x1 in zip(bases, x1s)]


def _prep_weights(w_in, w_out, conv_w, a_log, dt_bias, gdn_norm_w, ln1_g, ln1_b, w_router_group,
                  w_router_expert, w_gate, w_up, w_down, ln2_g, ln2_b):
    pad_row = lambda v: jnp.pad(v.astype(F32), (0, LANES - v.shape[0]))[None, :]
    wr = jnp.pad(jnp.concatenate([w_router_group, w_router_expert], axis=1),
                 ((0, 0), (0, LANES - N_GROUPS - N_EXPERTS)))
    wr_hi = wr.astype(BF16)
    wr_mid = (wr - wr_hi.astype(F32)).astype(BF16)
    group = ATTN_HEADS // ATTN_KV_HEADS
    wq = w_in[:, :Q_COLS].reshape(D_MODEL, ATTN_KV_HEADS, group, HEAD_DIM)
    wq = jnp.swapaxes(wq, 1, 2).reshape(D_MODEL, Q_COLS)
    w_all = jnp.concatenate([wq, w_in[:, Q_COLS:]], axis=1)
    w_all = jnp.pad(w_all, ((0, 0), (0, IN_COLS_PAD - w_all.shape[1]))).astype(BF16)
    wo_q = w_out[:Q_COLS].reshape(ATTN_KV_HEADS, group, HEAD_DIM, D_MODEL)
    wo_q = jnp.swapaxes(wo_q, 0, 1).reshape(Q_COLS, D_MODEL)
    wo = jnp.concatenate([wo_q, w_out[Q_COLS:]], axis=0)
    return dict(
        w_all=w_all, convw=conv_w.astype(F32), alog=pad_row(a_log), dtb=pad_row(dt_bias),
        norm_w=gdn_norm_w.astype(F32)[None, :], wo=wo.astype(BF16),
        ln1_g=ln1_g[None, :], ln1_b=ln1_b[None, :], wr=jnp.concatenate([wr_hi, wr_mid], axis=1),
        w_gate=w_gate, w_up=w_up, w_down=w_down, ln2_g=ln2_g[None, :], ln2_b=ln2_b[None, :])


def _layer(x_prompt, x_sample, cache_k, cache_v, state_gdn, state_conv, wts):
    bp, sp, _ = x_prompt.shape
    bs, ts, _ = x_sample.shape
    n_p = bp * sp

    xp = x_prompt.reshape(n_p, D_MODEL)
    (q, k, v, qg, kg, vg, z, gcb, utail) = _proj(xp, np.arange(sp), wts, GDN_CHUNK, bp)
    attn_p = _attn_prompt(q, k, v, wts['sinks'], bp)
    gdn_p, s_p = _gdn_prompt(qg, kg, vg, z, gcb, wts['norm_w'], bp)
    last_win = lambda a: a.reshape(bp, sp, KV_COLS)[:, sp - WINDOW:].reshape(bp, WINDOW, ATTN_KV_HEADS,
                                                                            HEAD_DIM)
    new_k_p, new_v_p = last_win(k), last_win(v)
    tiles_per_seq = sp // min(PROJ_TILE, sp)
    conv_p = utail.reshape(bp, tiles_per_seq, SUBLANES, CONV_DIM)[:, -1, SUBLANES - (CONV_W - 1):]

    lo, hi = SAMPLE_FIRST, SAMPLE_FIRST + ts
    xs_rows = jnp.pad(x_sample, ((0, 0), (lo, SAMPLE_SLOTS - hi), (0, 0))).reshape(bs * SAMPLE_SLOTS, D_MODEL)
    hist = jnp.pad(state_conv, ((0, 0), (0, SAMPLE_SLOTS - lo), (0, 0))).reshape(bs * SAMPLE_SLOTS, CONV_DIM)
    slot = np.arange(SAMPLE_SLOTS)
    valid = jnp.asarray(np.tile((slot >= lo) & (slot < hi), bs)[:, None], F32)
    pos_s = np.tile(PAST_LEN + slot - lo, bs)
    (q, k, v, qg, kg, vg, z, gcb, u_s) = _proj(xs_rows, pos_s, wts, SAMPLE_SLOTS, 1, hist, valid)
    ck = cache_k.reshape(bs, WINDOW, KV_COLS)
    cv = cache_v.reshape(bs, WINDOW, KV_COLS)
    attn_s, kwin, vwin = _attn_sample(q, k, v, ck, cv, wts['sinks'], bs, ts)
    gdn_s, s_s = _gdn_sample(qg, kg, vg, z, gcb, wts['norm_w'], state_gdn)
    real = lambda a: a.reshape(bs, SAMPLE_SLOTS, -1)[:, lo:hi]
    new_k_s = kwin.reshape(bs, WINDOW, ATTN_KV_HEADS, HEAD_DIM)
    new_v_s = vwin.reshape(bs, WINDOW, ATTN_KV_HEADS, HEAD_DIM)
    conv_s = u_s.reshape(bs, SAMPLE_SLOTS, CONV_DIM)[:, hi - (CONV_W - 1):hi]

    x1_p, route_p = _post(attn_p, gdn_p, xp, wts)
    x1_s, route_s = _post(real(attn_s).reshape(bs * ts, Q_COLS), real(gdn_s).reshape(bs * ts, Z_COLS),
                          x_sample.reshape(bs * ts, D_MODEL), wts)
    y_p, y_s = _moe([x1_p, x1_s], [route_p, route_s], wts)
    return (y_p.reshape(bp, sp, D_MODEL), y_s.reshape(bs, ts, D_MODEL), new_k_p, new_v_p, s_p, conv_p,
            new_k_s, new_v_s, s_s, conv_s)


def kernel(x_prompt, x_sample, cache_attn_k, cache_attn_v, state_gdn, state_conv, w_in, w_out,
           attn_sinks, conv_w, a_log, dt_bias, gdn_norm_w, ln1_g, ln1_b, w_router_group,
           w_router_expert, w_gate, w_up, w_down, ln2_g, ln2_b):
    assert w_in.shape[0] == DEPTH
    l = 0
    wts = _prep_weights(w_in[l], w_out[l], conv_w[l], a_log[l], dt_bias[l], gdn_norm_w[l], ln1_g[l],
                        ln1_b[l], w_router_group[l], w_router_expert[l], w_gate[l], w_up[l],
                        w_down[l], ln2_g[l], ln2_b[l])
    wts['sinks'] = attn_sinks[l]
    outs = _layer(x_prompt, x_sample, cache_attn_k[l], cache_attn_v[l], state_gdn[l], state_conv[l], wts)
    (y_p, y_s, k_p, v_p, s_p, c_p, k_s, v_s, s_s, c_s) = outs
    add = lambda a: a[None]
    return (y_p, y_s, add(k_p), add(v_p), add(s_p), add(c_p), add(k_s), add(v_s), add(s_s), add(c_s))
```
